```python
import math
import jax
import jax.numpy as jnp
from jax import lax
import numpy as np

D_MODEL = 2048
BATCH = 4
SEQ = 2048
DEPTH = 2

GRID_W = 64
CTX_LEN = 256

D_SSM = D_MODEL // 2
SSM_HEAD_DIM = 64
SSM_HEADS = D_SSM // SSM_HEAD_DIM
SSM_GROUPS = 4
SSM_STATE = 128
SSM_CONV = 5
SSM_CHUNK = 128
D_XBC = D_SSM + 2 * SSM_GROUPS * SSM_STATE

DIFF_HEAD_DIM = 64
DIFF_HEADS = (D_MODEL // 2) // (2 * DIFF_HEAD_DIM)
D_QK_DIFF = 2 * DIFF_HEADS * DIFF_HEAD_DIM
D_V_DIFF = DIFF_HEADS * 2 * DIFF_HEAD_DIM

D_MIX_EVEN = D_SSM + D_V_DIFF
IN_EVEN = D_SSM + D_XBC + 2 * SSM_HEADS + 2 * D_QK_DIFF + D_V_DIFF

NA_HEAD_DIM = 128
NA_HEADS = D_MODEL // NA_HEAD_DIM
D_NA = NA_HEADS * NA_HEAD_DIM
WIN_ROWS = 8
WIN_COLS = 16

D_FF = ((8 * D_MODEL // 3 + 255) // 256) * 256

N_EVEN = (DEPTH + 1) // 2
N_ODD = DEPTH // 2

QUERY_BLOCK = 128
ROPE_BASE = 10000.0
NORM_EPS = 1e-6

kernel_name = 'hybrid_ssd_diffattn_natten_prefix_dit'


def rms_norm(x, g):
    xf = x.astype(jnp.float32)
    xf = xf * lax.rsqrt(jnp.mean(xf * xf, axis=-1, keepdims=True) + NORM_EPS)
    return (xf * g.astype(jnp.float32)).astype(x.dtype)


def modulate(h, shift, scale):
    return h * (1.0 + scale) + shift


def swiglu(h, w1, w3, w2):
    return (jax.nn.silu(h @ w1) * (h @ w3)) @ w2


def axial_rope(x, row, col):
    half = x.shape[-1] // 2
    inv_freq = ROPE_BASE ** (-jnp.arange(0, half, 2, dtype=jnp.float32) / half)

    def rot(u, p):
        ang = p.astype(jnp.float32)[:, None] * inv_freq[None, :]
        cos = jnp.cos(ang)[None, :, None, :].astype(u.dtype)
        sin = jnp.sin(ang)[None, :, None, :].astype(u.dtype)
        u1, u2 = jnp.split(u, 2, axis=-1)
        return jnp.concatenate([u1 * cos - u2 * sin, u1 * sin + u2 * cos], axis=-1)

    xr, xc = jnp.split(x, 2, axis=-1)
    return jnp.concatenate([rot(xr, row), rot(xc, col)], axis=-1)


def depthwise_conv_centred(u, w, bias):
    k_w = w.shape[0]
    n = u.shape[1]
    up = jnp.pad(u, ((0, 0), (k_w // 2, k_w // 2), (0, 0)))
    return sum(up[:, t:t + n] * w[t] for t in range(k_w)) + bias


def ssd_chunked_scan(xs, dt, a, bm, cm, h0):
    b, n, nh, hp = xs.shape
    ng, ns = bm.shape[2], bm.shape[3]
    nc = n // SSM_CHUNK
    rep = nh // ng
    bh = jnp.repeat(bm, rep, axis=2).reshape(b, nc, SSM_CHUNK, nh, ns)
    ch = jnp.repeat(cm, rep, axis=2).reshape(b, nc, SSM_CHUNK, nh, ns)
    xdt = (xs * dt[..., None]).reshape(b, nc, SSM_CHUNK, nh, hp)
    a_cum = jnp.cumsum((dt * a).astype(jnp.float32).reshape(b, nc, SSM_CHUNK, nh), axis=2)

    lower = jnp.tril(jnp.ones((SSM_CHUNK, SSM_CHUNK), dtype=bool))
    seg = a_cum[:, :, :, None, :] - a_cum[:, :, None, :, :]
    decay_ls = jnp.exp(jnp.where(lower[None, None, :, :, None], seg, -jnp.inf)).astype(xs.dtype)
    cb = jnp.einsum('bclhn,bcshn->bclsh', ch, bh) * decay_ls
    y_diag = jnp.einsum('bclsh,bcshp->bclhp', cb, xdt)

    decay_to_end = jnp.exp(a_cum[:, :, -1:, :] - a_cum).astype(xs.dtype)
    chunk_states = jnp.einsum('bclhn,bclhp->bchpn', bh * decay_to_end[..., None], xdt)
    chunk_decay = jnp.exp(a_cum[:, :, -1, :]).astype(xs.dtype)

    def carry_state(h, inp):
        s_c, d_c = inp
        return h * d_c[:, :, None, None] + s_c, h

    h_final, h_start = lax.scan(carry_state, h0,
                                (jnp.moveaxis(chunk_states, 1, 0), jnp.moveaxis(chunk_decay, 1, 0)))
    h_start = jnp.moveaxis(h_start, 0, 1)
    y_off = jnp.einsum('bclhn,bchpn->bclhp', ch * jnp.exp(a_cum).astype(xs.dtype)[..., None], h_start)
    return (y_diag + y_off).reshape(b, n, nh, hp), h_final


def ssd_bidirectional(xs, dt_f, dt_b, a_f, a_b, bm, cm, h0_f, h0_b):
    flip = lambda t: jnp.flip(t, axis=1)
    y_f, h_f = ssd_chunked_scan(xs, dt_f, a_f, bm, cm, h0_f)
    y_b, h_b = ssd_chunked_scan(flip(xs), flip(dt_b), a_b, flip(bm), flip(cm), h0_b)
    return y_f + flip(y_b), h_f, h_b


def diff_attend(q, k, v, lam):
    b, nq, h2, d = q.shape
    s = jnp.einsum('bqhd,bkhd->bhqk', q, k).astype(jnp.float32) * (d ** -0.5)
    p = jax.nn.softmax(s, axis=-1).reshape(b, h2 // 2, 2, nq, k.shape[1])
    pd = (p[:, :, 0] - lam * p[:, :, 1]).astype(v.dtype)
    return jnp.einsum('bhqk,bkhe->bqhe', pd, v)


def softmax_attention(q, k, v):
    s = jnp.einsum('bqhd,bkhd->bhqk', q, k).astype(jnp.float32) * (q.shape[-1] ** -0.5)
    p = jax.nn.softmax(s, axis=-1).astype(v.dtype)
    return jnp.einsum('bhqk,bkhd->bqhd', p, v)


def blocked_queries(fn, q):
    b, n = q.shape[:2]
    nb = n // QUERY_BLOCK
    qb = jnp.moveaxis(q.reshape(b, nb, QUERY_BLOCK, *q.shape[2:]), 1, 0)
    out = lax.map(fn, qb)
    return jnp.moveaxis(out, 0, 1).reshape(b, n, *out.shape[3:])


def neighbourhood_attention(q, k, v, k_ctx, v_ctx, rpb):
    b, n, nh, hd = q.shape
    rows = n // GRID_W
    kr = min(WIN_ROWS, rows)
    nw = kr * GRID_W
    scale = hd ** -0.5
    qg = q.reshape(b, rows, GRID_W, nh, hd)
    kg = k.reshape(b, rows, GRID_W, nh, hd)
    vg = v.reshape(b, rows, GRID_W, nh, hd)
    qcol = jnp.arange(GRID_W)
    kcol = jnp.arange(GRID_W)
    col_start = jnp.clip(qcol - WIN_COLS // 2, 0, GRID_W - WIN_COLS)
    col_ok = (kcol[None, :] >= col_start[:, None]) & (kcol[None, :] < col_start[:, None] + WIN_COLS)
    mask = jnp.broadcast_to(col_ok[:, None, :], (GRID_W, kr, GRID_W)).reshape(GRID_W, nw)
    dc_idx = jnp.clip(kcol[None, :] - qcol[:, None] + WIN_COLS - 1, 0, 2 * WIN_COLS - 2)
    rpb_cols = rpb[:, :, dc_idx]

    def row_block(r):
        rs = jnp.clip(r - kr // 2, 0, rows - kr)
        q_r = lax.dynamic_index_in_dim(qg, r, axis=1, keepdims=False)
        k_r = lax.dynamic_slice_in_dim(kg, rs, kr, axis=1).reshape(b, nw, nh, hd)
        v_r = lax.dynamic_slice_in_dim(vg, rs, kr, axis=1).reshape(b, nw, nh, hd)
        dr_idx = rs + jnp.arange(kr) - r + WIN_ROWS - 1
        bias = jnp.moveaxis(rpb_cols[:, dr_idx], 1, 2).reshape(nh, GRID_W, nw)
        s_win = jnp.einsum('bqhd,bkhd->bhqk', q_r, k_r).astype(jnp.float32) * scale + bias.astype(jnp.float32)
        s_win = jnp.where(mask, s_win, -jnp.inf)
        s_ctx = jnp.einsum('bqhd,bkhd->bhqk', q_r, k_ctx).astype(jnp.float32) * scale
        p = jax.nn.softmax(jnp.concatenate([s_win, s_ctx], axis=-1), axis=-1).astype(v.dtype)
        return (jnp.einsum('bhqk,bkhd->bqhd', p[..., :nw], v_r)
                + jnp.einsum('bhqk,bkhd->bqhd', p[..., nw:], v_ctx))

    out = lax.map(row_block, jnp.arange(rows))
    return jnp.moveaxis(out, 0, 1).reshape(b, n, nh, hd)


def even_mixer(h_lat, h_ctx, row, col, w_in, conv_w, conv_b, a_log, dt_bias, d_skip, ssm_norm_g,
               lam_q1, lam_k1, lam_q2, lam_k2, subln_g, w_out, lambda_init, with_ctx_out):
    b = h_lat.shape[0]
    s0 = D_SSM
    s1 = s0 + D_XBC
    s2 = s1 + SSM_HEADS
    s3 = s2 + SSM_HEADS
    s4 = s3 + D_QK_DIFF
    s5 = s4 + D_QK_DIFF
    splits = [s0, s1, s2, s3, s4, s5]
    z_l, xbc_l, dtf_l, dtb_l, q_l, k_l, v_l = jnp.split(h_lat @ w_in, splits, axis=-1)
    z_c, xbc_c, dtf_c, dtb_c, q_c, k_c, v_c = jnp.split(h_ctx @ w_in, splits, axis=-1)

    decay = -jnp.exp(a_log)

    def ssd_prep(xbc, dtf, dtb):
        n = xbc.shape[1]
        xbc = jax.nn.silu(depthwise_conv_centred(xbc, conv_w, conv_b))
        xs, bm, cm = jnp.split(xbc, [D_SSM, D_SSM + SSM_GROUPS * SSM_STATE], axis=-1)
        return (xs.reshape(b, n, SSM_HEADS, SSM_HEAD_DIM),
                jax.nn.softplus(dtf + dt_bias[0]), jax.nn.softplus(dtb + dt_bias[1]),
                bm.reshape(b, n, SSM_GROUPS, SSM_STATE), cm.reshape(b, n, SSM_GROUPS, SSM_STATE))

    def ssd_finish(y, xs, z):
        n = y.shape[1]
        y = (y + d_skip[:, None] * xs).reshape(b, n, D_SSM) * jax.nn.silu(z)
        y = rms_norm(y.reshape(b, n, SSM_GROUPS, D_SSM // SSM_GROUPS), ssm_norm_g.reshape(SSM_GROUPS, -1))
        return y.reshape(b, n, D_SSM)

    xs_c, dtf_c, dtb_c, bm_c, cm_c = ssd_prep(xbc_c, dtf_c, dtb_c)
    h0 = jnp.zeros((b, SSM_HEADS, SSM_HEAD_DIM, SSM_STATE), xs_c.dtype)
    y_c, hf_c, hb_c = ssd_bidirectional(xs_c, dtf_c, dtb_c, decay[0], decay[1], bm_c, cm_c, h0, h0)
    xs_l, dtf_l, dtb_l, bm_l, cm_l = ssd_prep(xbc_l, dtf_l, dtb_l)
    y_l, _, _ = ssd_bidirectional(xs_l, dtf_l, dtb_l, decay[0], decay[1], bm_l, cm_l, hf_c, hb_c)

    lam = (jnp.exp(jnp.sum(lam_q1.astype(jnp.float32) * lam_k1.astype(jnp.float32)))
           - jnp.exp(jnp.sum(lam_q2.astype(jnp.float32) * lam_k2.astype(jnp.float32))) + lambda_init)
    qk_heads = lambda t: t.reshape(t.shape[0], t.shape[1], 2 * DIFF_HEADS, DIFF_HEAD_DIM)
    v_heads = lambda t: t.reshape(t.shape[0], t.shape[1], DIFF_HEADS, 2 * DIFF_HEAD_DIM)
    kd_c, vd_c = qk_heads(k_c), v_heads(v_c)
    k_all = jnp.concatenate([kd_c, axial_rope(qk_heads(k_l), row, col)], axis=1)
    v_all = jnp.concatenate([vd_c, v_heads(v_l)], axis=1)
    o_l = blocked_queries(lambda qb: diff_attend(qb, k_all, v_all, lam), axial_rope(qk_heads(q_l), row, col))

    def diff_finish(o):
        return (rms_norm(o, subln_g) * (1.0 - lambda_init)).reshape(o.shape[0], o.shape[1], D_V_DIFF)

    out_lat = jnp.concatenate([ssd_finish(y_l, xs_l, z_l), diff_finish(o_l)], axis=-1) @ w_out
    if not with_ctx_out:
        return out_lat, None
    o_c = diff_attend(qk_heads(q_c), kd_c, vd_c, lam)
    out_ctx = jnp.concatenate([ssd_finish(y_c, xs_c, z_c), diff_finish(o_c)], axis=-1) @ w_out
    return out_lat, out_ctx


def odd_mixer(h_lat, h_ctx, w_in, rpb, w_out, with_ctx_out):
    b, n_lat = h_lat.shape[:2]
    n_ctx = h_ctx.shape[1]
    heads = lambda t: t.reshape(t.shape[0], t.shape[1], NA_HEADS, NA_HEAD_DIM)
    q_l, k_l, v_l = [heads(t) for t in jnp.split(h_lat @ w_in, 3, axis=-1)]
    k_c, v_c = [heads(t) for t in jnp.split(h_ctx @ w_in[:, D_NA:], 2, axis=-1)]
    out_lat = neighbourhood_attention(q_l, k_l, v_l, k_c, v_c, rpb).reshape(b, n_lat, D_NA) @ w_out
    if not with_ctx_out:
        return out_lat, None
    q_c = heads(h_ctx @ w_in[:, :D_NA])
    out_ctx = softmax_attention(q_c, k_c, v_c).reshape(b, n_ctx, D_NA) @ w_out
    return out_lat, out_ctx


def setup_inputs(seed: int = 0) -> dict:
    key = jax.random.key(seed)
    keys = iter(jax.random.split(key, 40))

    def normal(shape, scale):
        return scale * jax.random.normal(next(keys), shape, dtype=jnp.float32)

    def gain(shape):
        return 1.0 + normal(shape, 0.02)

    d = D_MODEL
    a_init = jax.random.uniform(next(keys), (N_EVEN, 2, SSM_HEADS), jnp.float32, 1.0, 16.0)
    dt0 = jnp.exp(jax.random.uniform(next(keys), (N_EVEN, 2, SSM_HEADS), jnp.float32,
                                     math.log(1e-3), math.log(1e-1)))
    return {
        'x': normal((BATCH, SEQ, d), 1.0),
        'c': normal((BATCH, d), 1.0),
        'ctx': normal((BATCH, CTX_LEN, d), 1.0),
        'c_ctx': normal((d,), 1.0),
        'ada_w': normal((DEPTH, d, 6 * d), 0.5 * d ** -0.5),
        'ada_b': normal((DEPTH, 6 * d), 0.02),
        'norm_mix_g': gain((DEPTH, d)),
        'norm_ffn_g': gain((DEPTH, d)),
        'final_norm_g': gain((d,)),
        'ffn_w1': normal((DEPTH, d, D_FF), d ** -0.5),
        'ffn_w3': normal((DEPTH, d, D_FF), d ** -0.5),
        'ffn_w2': normal((DEPTH, D_FF, d), D_FF ** -0.5),
        'ev_w_in': normal((N_EVEN, d, IN_EVEN), d ** -0.5),
        'ev_conv_w': normal((N_EVEN, SSM_CONV, D_XBC), SSM_CONV ** -0.5),
        'ev_conv_b': normal((N_EVEN, D_XBC), 0.02),
        'ev_a_log': jnp.log(a_init),
        'ev_dt_bias': dt0 + jnp.log(-jnp.expm1(-dt0)),
        'ev_d_skip': 1.0 + normal((N_EVEN, SSM_HEADS), 0.1),
        'ev_ssm_norm_g': gain((N_EVEN, D_SSM)),
        'ev_lam_q1': normal((N_EVEN, DIFF_HEAD_DIM), 0.1),
        'ev_lam_k1': normal((N_EVEN, DIFF_HEAD_DIM), 0.1),
        'ev_lam_q2': normal((N_EVEN, DIFF_HEAD_DIM), 0.1),
        'ev_lam_k2': normal((N_EVEN, DIFF_HEAD_DIM), 0.1),
        'ev_subln_g': gain((N_EVEN, 2 * DIFF_HEAD_DIM)),
        'ev_w_out': normal((N_EVEN, D_MIX_EVEN, d), D_MIX_EVEN ** -0.5),
        'od_w_in': normal((N_ODD, d, 3 * D_NA), d ** -0.5),
        'od_rpb': normal((N_ODD, NA_HEADS, 2 * WIN_ROWS - 1, 2 * WIN_COLS - 1), 0.1),
        'od_w_out': normal((N_ODD, D_NA, d), D_NA ** -0.5),
    }


def reference(x, c, ctx, c_ctx, ada_w, ada_b, norm_mix_g, norm_ffn_g, final_norm_g,
              ffn_w1, ffn_w3, ffn_w2, ev_w_in, ev_conv_w, ev_conv_b, ev_a_log, ev_dt_bias,
              ev_d_skip, ev_ssm_norm_g, ev_lam_q1, ev_lam_k1, ev_lam_q2, ev_lam_k2, ev_subln_g,
              ev_w_out, od_w_in, od_rpb, od_w_out):
    n_tok = x.shape[1]
    pos = jnp.arange(n_tok)
    row, col = pos // GRID_W, pos % GRID_W
    cond_lat = jax.nn.silu(c)
    cond_ctx = jax.nn.silu(c_ctx)
    for i in range(DEPTH):
        ctx_out = i < DEPTH - 1
        sh1, sc1, g1, sh2, sc2, g2 = jnp.split((cond_lat @ ada_w[i] + ada_b[i])[:, None, :], 6, axis=-1)
        csh1, csc1, cg1, csh2, csc2, cg2 = jnp.split(cond_ctx @ ada_w[i] + ada_b[i], 6, axis=-1)
        h_lat = modulate(rms_norm(x, norm_mix_g[i]), sh1, sc1)
        h_ctx = modulate(rms_norm(ctx, norm_mix_g[i]), csh1, csc1)
        j = i // 2
        if i % 2 == 0:
            lambda_init = 0.8 - 0.6 * math.exp(-0.3 * i)
            o_lat, o_ctx = even_mixer(h_lat, h_ctx, row, col, ev_w_in[j], ev_conv_w[j], ev_conv_b[j],
                                      ev_a_log[j], ev_dt_bias[j], ev_d_skip[j], ev_ssm_norm_g[j],
                                      ev_lam_q1[j], ev_lam_k1[j], ev_lam_q2[j], ev_lam_k2[j],
                                      ev_subln_g[j], ev_w_out[j], lambda_init, ctx_out)
        else:
            o_lat, o_ctx = odd_mixer(h_lat, h_ctx, od_w_in[j], od_rpb[j], od_w_out[j], ctx_out)
        x = x + g1 * o_lat
        x = x + g2 * swiglu(modulate(rms_norm(x, norm_ffn_g[i]), sh2, sc2), ffn_w1[i], ffn_w3[i], ffn_w2[i])
        if ctx_out:
            ctx = ctx + cg1 * o_ctx
            ctx = ctx + cg2 * swiglu(modulate(rms_norm(ctx, norm_ffn_g[i]), csh2, csc2),
                                     ffn_w1[i], ffn_w3[i], ffn_w2[i])
    return rms_norm(x, final_norm_g)
```

```python
import functools
import math

import jax
import jax.numpy as jnp
import numpy as np
from jax import lax
from jax.experimental import pallas as pl
from jax.experimental.pallas import tpu as pltpu

F32 = jnp.float32
BF16 = jnp.bfloat16

GRID_W = 64
SSM_HEAD_DIM = 64
SSM_GROUPS = 4
SSM_STATE = 128
SSM_CONV = 5
SSM_CHUNK = 128
DIFF_HEAD_DIM = 64
NA_HEAD_DIM = 128
WIN_ROWS = 8
WIN_COLS = 16
ROPE_BASE = 10000.0
NORM_EPS = 1e-6

LANES = 128
V7X_VMEM_BYTES = 64 * 1024 * 1024
VMEM_HEADROOM_BYTES = 6 * 1024 * 1024

NA_Q_ROWS = 4
NA_K_ROWS = NA_Q_ROWS + WIN_ROWS - 1
COND_ROWS = 16


def _vmem_limit(block_bytes):
    return int(min(V7X_VMEM_BYTES - VMEM_HEADROOM_BYTES, max(32 * 1024 * 1024, 2 * block_bytes + 16 * 1024 * 1024)))


def _params(semantics, block_bytes):
    return pltpu.CompilerParams(dimension_semantics=semantics, vmem_limit_bytes=_vmem_limit(block_bytes))


def _silu(v):
    return v * jax.nn.sigmoid(v)


def _mod_row(i, tm, seq, n_batch):
    return jnp.minimum((i * tm) // seq, n_batch)


def _ada_kernel(cond_ref, w_ref, b_ref, o_ref):
    s = _silu(cond_ref[...]).astype(BF16)
    o_ref[...] = jnp.dot(s, w_ref[...].astype(BF16), preferred_element_type=F32) + b_ref[...]


def _ada_mod(cond, ada_w, ada_b, tn=1024):
    depth, d, n = ada_w.shape
    return pl.pallas_call(
        _ada_kernel,
        grid=(depth, n // tn),
        in_specs=[
            pl.BlockSpec((COND_ROWS, d), lambda l, j: (0, 0)),
            pl.BlockSpec((None, d, tn), lambda l, j: (l, 0, j)),
            pl.BlockSpec((None, 1, tn), lambda l, j: (l, 0, j)),
        ],
        out_specs=pl.BlockSpec((None, COND_ROWS, tn), lambda l, j: (l, 0, j)),
        out_shape=jax.ShapeDtypeStruct((depth, COND_ROWS, n), F32),
        compiler_params=_params(("arbitrary", "arbitrary"), d * tn * 4),
        name="ada_mod",
    )(cond, ada_w, ada_b.reshape(depth, 1, n))


def _norm_mod_into(x_ref, g_ref, mod_ref, h_ref, shift_idx, scale_idx, rows=128):
    d = x_ref.shape[-1]
    g = g_ref[...]
    shift = mod_ref[:, shift_idx * d:(shift_idx + 1) * d]
    scale1 = 1.0 + mod_ref[:, scale_idx * d:(scale_idx + 1) * d]

    def body(r, carry):
        sl = pl.ds(pl.multiple_of(r * rows, rows), rows)
        x = x_ref[sl, :]
        xn = x * lax.rsqrt(jnp.mean(x * x, axis=-1, keepdims=True) + NORM_EPS) * g
        h_ref[sl, :] = (xn * scale1 + shift).astype(h_ref.dtype)
        return carry

    lax.fori_loop(0, x_ref.shape[0] // rows, body, 0)


def _proj_kernel(x_ref, g_ref, mod_ref, w_ref, *rest, has_aux):
    if has_aux:
        waux_ref, o_ref, oaux_ref, h_ref = rest
    else:
        o_ref, h_ref = rest

    @pl.when(pl.program_id(1) == 0)
    def _():
        _norm_mod_into(x_ref, g_ref, mod_ref, h_ref, 0, 1)
        if has_aux:
            oaux_ref[...] = jnp.dot(h_ref[...], waux_ref[...], preferred_element_type=F32)

    o_ref[...] = jnp.dot(h_ref[...], w_ref[...], preferred_element_type=F32).astype(o_ref.dtype)


def _proj(x, g, mod, layer, w, w_aux, out_dtype, seq, n_batch, tm=1024, tn=512):
    m, d = x.shape
    n = w.shape[1]
    has_aux = w_aux is not None
    mod_spec = pl.BlockSpec((None, None, 1, mod.shape[-1]),
                            lambda i, j: (layer, _mod_row(i, tm, seq, n_batch), 0, 0))
    in_specs = [
        pl.BlockSpec((tm, d), lambda i, j: (i, 0)),
        pl.BlockSpec((1, d), lambda i, j: (0, 0)),
        mod_spec,
        pl.BlockSpec((d, tn), lambda i, j: (0, j)),
    ]
    out_specs = [pl.BlockSpec((tm, tn), lambda i, j: (i, j))]
    out_shape = [jax.ShapeDtypeStruct((m, n), out_dtype)]
    args = [x, g.reshape(1, d), mod, w]
    if has_aux:
        na = w_aux.shape[1]
        in_specs.append(pl.BlockSpec((d, na), lambda i, j: (0, 0)))
        out_specs.append(pl.BlockSpec((tm, na), lambda i, j: (i, 0)))
        out_shape.append(jax.ShapeDtypeStruct((m, na), F32))
        args.append(w_aux)
    block_bytes = tm * d * 4 + d * tn * 2 + tm * tn * 4 + tm * d
    outs = pl.pallas_call(
        functools.partial(_proj_kernel, has_aux=has_aux),
        grid=(m // tm, n // tn),
        in_specs=in_specs,
        out_specs=out_specs,
        out_shape=out_shape,
        scratch_shapes=[pltpu.VMEM((tm, d), BF16)],
        compiler_params=_params(("arbitrary", "arbitrary"), block_bytes),
        name="proj",
    )(*args)
    return outs if has_aux else outs[0]


def _out_proj_kernel(*refs, n_lhs):
    lhs = refs[:n_lhs]
    ws = refs[n_lhs:2 * n_lhs]
    res_ref, gate_ref, o_ref = refs[2 * n_lhs:]
    acc = jnp.dot(lhs[0][...], ws[0][...], preferred_element_type=F32)
    for a_ref, w_ref in zip(lhs[1:], ws[1:]):
        acc = acc + jnp.dot(a_ref[...], w_ref[...], preferred_element_type=F32)
    o_ref[...] = res_ref[...] + gate_ref[...] * acc


def _out_proj(lhs_list, w, res, mod, layer, gate_idx, m_rows, seq, n_batch, tm=1024, tn=512):
    d = res.shape[1]
    n_lhs = len(lhs_list)
    in_specs, args = [], []
    for a in lhs_list:
        in_specs.append(pl.BlockSpec((tm, a.shape[1]), lambda i, j: (i, 0)))
        args.append(a)
    row0 = 0
    for a in lhs_list:
        kk = a.shape[1]
        in_specs.append(pl.BlockSpec((kk, tn), lambda i, j, rb=row0 // kk: (rb, j)))
        args.append(w)
        row0 += kk
    in_specs.append(pl.BlockSpec((tm, tn), lambda i, j: (i, j)))
    args.append(res)
    in_specs.append(pl.BlockSpec((None, None, 1, tn),
                                 lambda i, j: (layer, _mod_row(i, tm, seq, n_batch), 0, gate_idx * (d // tn) + j)))
    args.append(mod)
    k_total = sum(a.shape[1] for a in lhs_list)
    block_bytes = tm * k_total * 2 + k_total * tn * 2 + 2 * tm * tn * 4
    return pl.pallas_call(
        functools.partial(_out_proj_kernel, n_lhs=n_lhs),
        grid=(m_rows // tm, d // tn),
        in_specs=in_specs,
        out_specs=pl.BlockSpec((tm, tn), lambda i, j: (i, j)),
        out_shape=jax.ShapeDtypeStruct((m_rows, d), F32),
        compiler_params=_params(("arbitrary", "arbitrary"), block_bytes),
        name="out_proj",
    )(*args)


def _ffn_kernel(x_ref, g_ref, mod_ref, w1_ref, w3_ref, w2_ref, fg_ref, o_ref, h_ref, acc_ref, *, final_norm):
    j = pl.program_id(1)
    d = x_ref.shape[-1]

    @pl.when(j == 0)
    def _():
        _norm_mod_into(x_ref, g_ref, mod_ref, h_ref, 3, 4)
        acc_ref[...] = jnp.zeros_like(acc_ref)

    h = h_ref[...]
    a = jnp.dot(h, w1_ref[...], preferred_element_type=F32)
    b = jnp.dot(h, w3_ref[...], preferred_element_type=F32)
    u = (_silu(a) * b).astype(BF16)
    acc_ref[...] += jnp.dot(u, w2_ref[...], preferred_element_type=F32)

    @pl.when(j == pl.num_programs(1) - 1)
    def _():
        y = x_ref[...] + mod_ref[:, 5 * d:6 * d] * acc_ref[...]
        if final_norm:
            y = y * lax.rsqrt(jnp.mean(y * y, axis=-1, keepdims=True) + NORM_EPS) * fg_ref[...]
        o_ref[...] = y


def _ffn(x, g, mod, layer, w1, w3, w2, final_g, final_norm, m_rows, seq, n_batch, tm=512, tf=512):
    d = x.shape[1]
    ff = w1.shape[1]
    block_bytes = 2 * tm * d * 4 + 3 * d * tf * 2 + tm * d * 3
    return pl.pallas_call(
        functools.partial(_ffn_kernel, final_norm=final_norm),
        grid=(m_rows // tm, ff // tf),
        in_specs=[
            pl.BlockSpec((tm, d), lambda i, j: (i, 0)),
            pl.BlockSpec((1, d), lambda i, j: (0, 0)),
            pl.BlockSpec((None, None, 1, mod.shape[-1]),
                         lambda i, j: (layer, _mod_row(i, tm, seq, n_batch), 0, 0)),
            pl.BlockSpec((d, tf), lambda i, j: (0, j)),
            pl.BlockSpec((d, tf), lambda i, j: (0, j)),
            pl.BlockSpec((tf, d), lambda i, j: (j, 0)),
            pl.BlockSpec((1, d), lambda i, j: (0, 0)),
        ],
        out_specs=pl.BlockSpec((tm, d), lambda i, j: (i, 0)),
        out_shape=jax.ShapeDtypeStruct((m_rows, d), F32),
        scratch_shapes=[pltpu.VMEM((tm, d), BF16), pltpu.VMEM((tm, d), F32)],
        compiler_params=_params(("arbitrary", "arbitrary"), block_bytes),
        name="ffn",
    )(x, g.reshape(1, d), mod, w1, w3, w2, final_g.reshape(1, d))


def _conv_kernel(prev_ref, cur_ref, next_ref, w_ref, b_ref, o_ref, ext_ref, *, tm, halo, seq, ctx_len, n_lat_tiles):
    i = pl.program_id(0)
    ext_ref[0:halo, :] = prev_ref[...]
    ext_ref[halo:halo + tm, :] = cur_ref[...]
    ext_ref[halo + tm:, :] = next_ref[...]
    seg_len = jnp.where(i < n_lat_tiles, seq, ctx_len)
    pos = (i * tm + lax.broadcasted_iota(jnp.int32, (tm, 1), 0)) & (seg_len - 1)
    acc = jnp.zeros(cur_ref.shape, F32) + b_ref[...]
    half = SSM_CONV // 2
    for t in range(SSM_CONV):
        src = pos + (t - half)
        tap = ext_ref[halo - half + t:halo - half + t + tm, :]
        acc = acc + jnp.where((src >= 0) & (src < seg_len), tap, 0.0) * w_ref[t:t + 1, :]
    o_ref[...] = _silu(acc)


def _ssd_conv(p0, col0, width, conv_w, conv_b, seq, ctx_len, n_lat_rows, tm=1024, tc=256, halo=8):
    m = p0.shape[0]
    assert seq % tm == 0 and tm % ctx_len == 0 and (seq & (seq - 1)) == 0 and (ctx_len & (ctx_len - 1)) == 0
    cb0 = col0 // tc
    hb = tm // halo
    last_hb = m // halo - 1
    wpad = jnp.zeros((8, width), F32).at[:SSM_CONV].set(conv_w)
    return pl.pallas_call(
        functools.partial(_conv_kernel, tm=tm, halo=halo, seq=seq, ctx_len=ctx_len, n_lat_tiles=n_lat_rows // tm),
        grid=(m // tm, width // tc),
        in_specs=[
            pl.BlockSpec((halo, tc), lambda i, j: (jnp.maximum(i * hb - 1, 0), cb0 + j)),
            pl.BlockSpec((tm, tc), lambda i, j: (i, cb0 + j)),
            pl.BlockSpec((halo, tc), lambda i, j: (jnp.minimum((i + 1) * hb, last_hb), cb0 + j)),
            pl.BlockSpec((8, tc), lambda i, j: (0, j)),
            pl.BlockSpec((1, tc), lambda i, j: (0, j)),
        ],
        out_specs=pl.BlockSpec((tm, tc), lambda i, j: (i, j)),
        out_shape=jax.ShapeDtypeStruct((m, width), F32),
        scratch_shapes=[pltpu.VMEM((tm + 2 * halo, tc), F32)],
        compiler_params=_params(("arbitrary", "arbitrary"), 3 * tm * tc * 4),
        name="ssd_conv",
    )(p0, p0, p0, wpad, conv_b.reshape(1, width))


def _ssd_scan_kernel(xs_ref, bm_ref, cm_ref, dt_ref, prm_ref, y_ref, state_ref, *, n_heads):
    dirn = pl.program_id(1)
    step = pl.program_id(2)
    hp = SSM_HEAD_DIM
    rep = n_heads // SSM_GROUPS
    lc = SSM_CHUNK

    @pl.when(step == 0)
    def _():
        state_ref[...] = jnp.zeros_like(state_ref)

    dt_in = dt_ref[...] + prm_ref[1:2, :]
    dt = jnp.maximum(dt_in, 0.0) + jnp.log1p(jnp.exp(-jnp.abs(dt_in)))
    dta = dt * (-jnp.exp(prm_ref[0:1, :]))
    row = lax.broadcasted_iota(jnp.int32, (lc, lc), 0)
    col = lax.broadcasted_iota(jnp.int32, (lc, lc), 1)
    causal = (row - col) * (1 - 2 * dirn) >= 0
    a_cum = jnp.dot(causal.astype(F32), dta, preferred_element_type=F32, precision=lax.Precision.HIGHEST)
    a_cum_t = a_cum.T
    a_tot = jnp.where(dirn == 0, a_cum[lc - 1:lc, :], a_cum[0:1, :])

    for g in range(SSM_GROUPS):
        gs = slice(g * SSM_STATE, (g + 1) * SSM_STATE)
        c_g = cm_ref[:, gs].astype(BF16)
        b_g = bm_ref[:, gs].astype(BF16)
        cb = lax.dot_general(c_g, b_g, (((1,), (1,)), ((), ())), preferred_element_type=F32)
        rows_g = slice(g * rep * hp, (g + 1) * rep * hp)
        h_in = state_ref[rows_g, :]
        y_off = lax.dot_general(c_g, h_in.astype(BF16), (((1,), (1,)), ((), ())), preferred_element_type=F32)
        ys, xws = [], []
        for hh in range(rep):
            h = g * rep + hh
            a_col = a_cum[:, h:h + 1]
            decay = jnp.exp(jnp.where(causal, a_col - a_cum_t[h:h + 1, :], -jnp.inf))
            xdt = xs_ref[:, h * hp:(h + 1) * hp] * dt[:, h:h + 1]
            y_diag = jnp.dot((cb * decay).astype(BF16), xdt.astype(BF16), preferred_element_type=F32)
            ys.append(y_diag + y_off[:, hh * hp:(hh + 1) * hp] * jnp.exp(a_col))
            xws.append(xdt * jnp.exp(a_tot[:, h:h + 1] - a_col))
        y_ref[:, rows_g] = jnp.concatenate(ys, axis=1)
        xw = jnp.concatenate(xws, axis=1).astype(BF16)
        s_new = lax.dot_general(xw, b_g, (((0,), (0,)), ((), ())), preferred_element_type=F32)
        for hh in range(rep):
            h = g * rep + hh
            rs = slice(h * hp, (h + 1) * hp)
            state_ref[rs, :] = (state_ref[rs, :] * jnp.exp(a_tot[:, h:h + 1])
                                + s_new[hh * hp:(hh + 1) * hp, :])


def _ssd_scan(xbc, dt_raw, prm, n_batch, seq, ctx_len, d_ssm):
    m = xbc.shape[0]
    lc = SSM_CHUNK
    n_heads = d_ssm // SSM_HEAD_DIM
    nc_ctx, nc_lat = ctx_len // lc, seq // lc
    ctx_blk0 = n_batch * nc_lat
    gn = SSM_GROUPS * SSM_STATE

    def row_blk(b, dirn, s):
        ctx_c = jnp.where(dirn == 0, s, nc_ctx - 1 - s)
        lat_c = jnp.where(dirn == 0, s - nc_ctx, nc_lat - 1 - (s - nc_ctx))
        return jnp.where(s < nc_ctx, ctx_blk0 + b * nc_ctx + ctx_c, b * nc_lat + lat_c)

    return pl.pallas_call(
        functools.partial(_ssd_scan_kernel, n_heads=n_heads),
        grid=(n_batch, 2, nc_ctx + nc_lat),
        in_specs=[
            pl.BlockSpec((lc, d_ssm), lambda b, r, s: (row_blk(b, r, s), 0)),
            pl.BlockSpec((lc, gn), lambda b, r, s: (row_blk(b, r, s), d_ssm // gn)),
            pl.BlockSpec((lc, gn), lambda b, r, s: (row_blk(b, r, s), d_ssm // gn + 1)),
            pl.BlockSpec((lc, LANES), lambda b, r, s: (row_blk(b, r, s), r)),
            pl.BlockSpec((None, 8, LANES), lambda b, r, s: (r, 0, 0)),
        ],
        out_specs=pl.BlockSpec((None, lc, d_ssm), lambda b, r, s: (r, row_blk(b, r, s), 0)),
        out_shape=jax.ShapeDtypeStruct((2, m, d_ssm), F32),
        scratch_shapes=[pltpu.VMEM((d_ssm, SSM_STATE), F32)],
        compiler_params=_params(("arbitrary", "arbitrary", "arbitrary"), 4 * lc * d_ssm * 4),
        name="ssd_scan",
    )(xbc, xbc, xbc, dt_raw, prm)


def _ssd_finish_kernel(y_ref, xs_ref, z_ref, dskip_ref, g_ref, o_ref, *, group_width):
    y = (y_ref[0] + y_ref[1] + dskip_ref[...] * xs_ref[...]) * _silu(z_ref[...])
    for g in range(y.shape[1] // group_width):
        sl = slice(g * group_width, (g + 1) * group_width)
        v = y[:, sl]
        vn = v * lax.rsqrt(jnp.mean(v * v, axis=-1, keepdims=True) + NORM_EPS) * g_ref[:, sl]
        o_ref[:, sl] = vn.astype(o_ref.dtype)


def _ssd_finish(y2, xbc, p0, d_skip_lanes, norm_g, d_ssm, tm=512):
    m = xbc.shape[0]
    return pl.pallas_call(
        functools.partial(_ssd_finish_kernel, group_width=d_ssm // SSM_GROUPS),
        grid=(m // tm,),
        in_specs=[
            pl.BlockSpec((2, tm, d_ssm), lambda i: (0, i, 0)),
            pl.BlockSpec((tm, d_ssm), lambda i: (i, 0)),
            pl.BlockSpec((tm, d_ssm), lambda i: (i, 0)),
            pl.BlockSpec((1, d_ssm), lambda i: (0, 0)),
            pl.BlockSpec((1, d_ssm), lambda i: (0, 0)),
        ],
        out_specs=pl.BlockSpec((tm, d_ssm), lambda i: (i, 0)),
        out_shape=jax.ShapeDtypeStruct((m, d_ssm), BF16),
        compiler_params=_params(("arbitrary",), 5 * tm * d_ssm * 4),
        name="ssd_finish",
    )(y2, xbc, p0, d_skip_lanes.reshape(1, d_ssm), norm_g.reshape(1, d_ssm))


def _rope(x, cos, sin_up, sin_dn):
    quarter = DIFF_HEAD_DIM // 4
    return (x * cos + pltpu.roll(x, LANES - quarter, axis=1) * sin_up + pltpu.roll(x, quarter, axis=1) * sin_dn)


def _softmax_rows(s):
    e = jnp.exp(s - jnp.max(s, axis=-1, keepdims=True))
    return e * (1.0 / jnp.sum(e, axis=-1, keepdims=True))


def _diff_attn_kernel(*refs, with_lat, lambda_init, n_ctx):
    if with_lat:
        (q_ref, kc_ref, kl_ref, vc_ref, vl_ref, qcos_ref, qsu_ref, qsd_ref, kcos_ref, ksu_ref, ksd_ref,
         lam_ref, g_ref, o_ref, k_scr, v_scr) = refs
    else:
        q_ref, kc_ref, vc_ref, lam_ref, g_ref, o_ref, k_scr, v_scr = refs

    @pl.when(pl.program_id(2) == 0)
    def _():
        k_scr[0:n_ctx, :] = kc_ref[...].astype(BF16)
        v_scr[0:n_ctx, :] = vc_ref[...].astype(BF16)
        if with_lat:
            k_scr[n_ctx:, :] = _rope(kl_ref[...], kcos_ref[...], ksu_ref[...], ksd_ref[...]).astype(BF16)
            v_scr[n_ctx:, :] = vl_ref[...].astype(BF16)

    q = q_ref[...]
    if with_lat:
        q = _rope(q, qcos_ref[...], qsu_ref[...], qsd_ref[...])
    q = q * (DIFF_HEAD_DIM ** -0.5)
    lo = lax.broadcasted_iota(jnp.int32, q.shape, 1) < DIFF_HEAD_DIM
    k = k_scr[...]
    nt = (((1,), (1,)), ((), ()))
    s1 = lax.dot_general(jnp.where(lo, q, 0.0).astype(BF16), k, nt, preferred_element_type=F32)
    s2 = lax.dot_general(jnp.where(lo, 0.0, q).astype(BF16), k, nt, preferred_element_type=F32)
    lam_p = lam_ref[...]
    lam = (jnp.exp(jnp.sum(lam_p[0:1, :] * lam_p[1:2, :], axis=-1, keepdims=True))
           - jnp.exp(jnp.sum(lam_p[2:3, :] * lam_p[3:4, :], axis=-1, keepdims=True)) + lambda_init)
    pd = (_softmax_rows(s1) - lam * _softmax_rows(s2)).astype(BF16)
    o = jnp.dot(pd, v_scr[...], preferred_element_type=F32)
    o = o * lax.rsqrt(jnp.mean(o * o, axis=-1, keepdims=True) + NORM_EPS) * g_ref[...] * (1.0 - lambda_init)
    o_ref[...] = o.astype(o_ref.dtype)


def _diff_attn(p0, q_col0, k_col0, v_col0, n_heads, rope_tabs, lam_p, subln_g, lambda_init,
               n_batch, seq, ctx_len, with_lat, tq=256):
    cb = lambda c0: c0 // LANES
    ctx_rb0 = (n_batch * seq) // ctx_len
    n_keys = ctx_len + (seq if with_lat else 0)
    if with_lat:
        n_q, q_rows = seq // tq, n_batch * seq
        q_spec = pl.BlockSpec((tq, LANES), lambda b, h, i: (b * n_q + i, cb(q_col0) + h))
    else:
        tq = ctx_len
        n_q, q_rows = 1, n_batch * ctx_len
        q_spec = pl.BlockSpec((tq, LANES), lambda b, h, i: (ctx_rb0 + b, cb(q_col0) + h))
    kc_spec = pl.BlockSpec((ctx_len, LANES), lambda b, h, i: (ctx_rb0 + b, cb(k_col0) + h))
    vc_spec = pl.BlockSpec((ctx_len, LANES), lambda b, h, i: (ctx_rb0 + b, cb(v_col0) + h))
    small = [pl.BlockSpec((8, LANES), lambda b, h, i: (0, 0)), pl.BlockSpec((1, LANES), lambda b, h, i: (0, 0))]
    small_args = [lam_p, subln_g.reshape(1, LANES)]
    if with_lat:
        kl_spec = pl.BlockSpec((seq, LANES), lambda b, h, i: (b, cb(k_col0) + h))
        vl_spec = pl.BlockSpec((seq, LANES), lambda b, h, i: (b, cb(v_col0) + h))
        qt = pl.BlockSpec((tq, LANES), lambda b, h, i: (i, 0))
        kt = pl.BlockSpec((seq, LANES), lambda b, h, i: (0, 0))
        in_specs = [q_spec, kc_spec, kl_spec, vc_spec, vl_spec, qt, qt, qt, kt, kt, kt] + small
        args = [p0, p0, p0, p0, p0, *rope_tabs, *rope_tabs] + small_args
    else:
        in_specs = [q_spec, kc_spec, vc_spec] + small
        args = [p0, p0, p0] + small_args
    block_bytes = 10 * seq * LANES * 4 if with_lat else 8 * ctx_len * LANES * 4
    return pl.pallas_call(
        functools.partial(_diff_attn_kernel, with_lat=with_lat, lambda_init=lambda_init, n_ctx=ctx_len),
        grid=(n_batch, n_heads, n_q),
        in_specs=in_specs,
        out_specs=pl.BlockSpec((tq, LANES), lambda b, h, i: (b * n_q + i, h)),
        out_shape=jax.ShapeDtypeStruct((q_rows, n_heads * LANES), BF16),
        scratch_shapes=[pltpu.VMEM((n_keys, LANES), BF16), pltpu.VMEM((n_keys, LANES), BF16)],
        compiler_params=_params(("arbitrary", "arbitrary", "arbitrary"), block_bytes),
        name="diff_attn_lat" if with_lat else "diff_attn_ctx",
    )(*args)


def _rope_tables(seq):
    half = DIFF_HEAD_DIM // 2
    pos = jnp.arange(seq)
    row, col = pos // GRID_W, pos % GRID_W
    inv_freq = ROPE_BASE ** (-jnp.arange(0, half, 2, dtype=F32) / half)
    lane = np.arange(LANES) % DIFF_HEAD_DIM
    use_col = jnp.asarray(lane >= half)
    first = jnp.asarray((lane % half) < half // 2)
    freq = inv_freq[jnp.asarray(lane % (half // 2))]
    p = jnp.where(use_col[None, :], col[:, None], row[:, None]).astype(F32)
    ang = p * freq[None, :]
    cos, sin = jnp.cos(ang), jnp.sin(ang)
    return cos, jnp.where(first[None, :], -sin, 0.0), jnp.where(first[None, :], 0.0, sin)


def _rpb_gather_kernel(rpb_ref, o_ref):
    n = o_ref.shape[1]
    colid = pl.program_id(0) * n + lax.broadcasted_iota(jnp.int32, (LANES, n), 1)
    j = lax.broadcasted_iota(jnp.int32, (LANES, n), 0)
    qc = lax.shift_right_logical(colid, int(math.log2(GRID_W)))
    kc = colid & (GRID_W - 1)
    sel = (j == jnp.clip(kc - qc + (WIN_COLS - 1), 0, 2 * WIN_COLS - 2)).astype(F32)
    o_ref[...] = jnp.dot(rpb_ref[...], sel, preferred_element_type=F32, precision=lax.Precision.HIGHEST)


def _rpb_cols(rpb):
    nh, nr, ncol = rpb.shape
    rows = nh * nr
    rows_pad = -(-rows // 8) * 8
    rpb2 = jnp.zeros((rows_pad, LANES), F32).at[:rows, :ncol].set(rpb.reshape(rows, ncol))
    tn = 512
    out = pl.pallas_call(
        _rpb_gather_kernel,
        grid=(GRID_W * GRID_W // tn,),
        in_specs=[pl.BlockSpec((rows_pad, LANES), lambda j: (0, 0))],
        out_specs=pl.BlockSpec((rows_pad, tn), lambda j: (0, j)),
        out_shape=jax.ShapeDtypeStruct((rows_pad, GRID_W * GRID_W), F32),
        compiler_params=_params(("arbitrary",), rows_pad * tn * 4),
        name="rpb_gather",
    )(rpb2)
    return out[:rows].reshape(nh, nr, GRID_W, GRID_W)


def _na_block_plan(rows):
    kr = min(WIN_ROWS, rows)
    n_blk = rows // NA_Q_ROWS
    kb = np.clip(np.arange(n_blk) * NA_Q_ROWS - kr // 2, 0, rows - NA_K_ROWS)
    layouts, layout_of = [], []
    for blk in range(n_blk):
        dr = np.full((NA_Q_ROWS, NA_K_ROWS), -1, np.int64)
        for i in range(NA_Q_ROWS):
            r = blk * NA_Q_ROWS + i
            rs = int(np.clip(r - kr // 2, 0, rows - kr))
            for j in range(NA_K_ROWS):
                krow = kb[blk] + j
                if rs <= krow < rs + kr:
                    dr[i, j] = krow - r + WIN_ROWS - 1
        key = dr.tobytes()
        if key not in [l.tobytes() for l in layouts]:
            layouts.append(dr)
        layout_of.append([l.tobytes() for l in layouts].index(key))
    return kb, np.stack(layouts), np.asarray(layout_of)


def _na_bias(rpb_cols, layouts):
    nh = rpb_cols.shape[0]
    n_lay = layouts.shape[0]
    qcol = np.arange(GRID_W)
    col_start = np.clip(qcol - WIN_COLS // 2, 0, GRID_W - WIN_COLS)
    col_ok = (qcol[None, :] >= col_start[:, None]) & (qcol[None, :] < col_start[:, None] + WIN_COLS)
    t = jnp.take(rpb_cols, jnp.asarray(np.maximum(layouts, 0).reshape(-1)), axis=1)
    t = t.reshape(nh, n_lay, NA_Q_ROWS, NA_K_ROWS, GRID_W, GRID_W)
    ok = (layouts >= 0)[None, :, :, :, None, None] & col_ok[None, None, None, None, :, :]
    t = jnp.where(jnp.asarray(ok), t, -jnp.inf)
    return t.transpose(0, 1, 2, 4, 3, 5).reshape(nh, n_lay, NA_Q_ROWS * GRID_W, NA_K_ROWS * GRID_W)


def _na_kernel(kb_ref, lay_ref, q_ref, kl_ref, vl_ref, kc_ref, vc_ref, bias_ref, o_ref):
    blk = pl.program_id(2)
    nkw = NA_K_ROWS * GRID_W
    start = pl.multiple_of(kb_ref[blk] * GRID_W, GRID_W)
    kw = kl_ref[pl.ds(start, nkw), :]
    vw = vl_ref[pl.ds(start, nkw), :]
    q = q_ref[...]
    nt = (((1,), (1,)), ((), ()))
    scale = NA_HEAD_DIM ** -0.5
    s_w = lax.dot_general(q, kw, nt, preferred_element_type=F32) * scale + bias_ref[lay_ref[blk]]
    s_c = lax.dot_general(q, kc_ref[...], nt, preferred_element_type=F32) * scale
    m = jnp.maximum(jnp.max(s_w, axis=-1, keepdims=True), jnp.max(s_c, axis=-1, keepdims=True))
    e_w = jnp.exp(s_w - m)
    e_c = jnp.exp(s_c - m)
    l = jnp.sum(e_w, axis=-1, keepdims=True) + jnp.sum(e_c, axis=-1, keepdims=True)
    o = (jnp.dot(e_w.astype(BF16), vw, preferred_element_type=F32)
         + jnp.dot(e_c.astype(BF16), vc_ref[...], preferred_element_type=F32))
    o_ref[...] = (o * (1.0 / l)).astype(o_ref.dtype)


def _na_attn(p1, rpb, n_batch, seq, ctx_len, n_heads):
    rows = seq // GRID_W
    kb, layouts, layout_of = _na_block_plan(rows)
    bias = _na_bias(_rpb_cols(rpb), layouts)
    n_lay = layouts.shape[0]
    tq = NA_Q_ROWS * GRID_W
    nkw = NA_K_ROWS * GRID_W
    n_blk = rows // NA_Q_ROWS
    ctx_rb0 = (n_batch * seq) // ctx_len
    grid_spec = pltpu.PrefetchScalarGridSpec(
        num_scalar_prefetch=2,
        grid=(n_heads, n_batch, n_blk),
        in_specs=[
            pl.BlockSpec((tq, LANES), lambda h, b, i, kb_r, lay_r: (b * n_blk + i, h)),
            pl.BlockSpec((seq, LANES), lambda h, b, i, kb_r, lay_r: (b, n_heads + h)),
            pl.BlockSpec((seq, LANES), lambda h, b, i, kb_r, lay_r: (b, 2 * n_heads + h)),
            pl.BlockSpec((ctx_len, LANES), lambda h, b, i, kb_r, lay_r: (ctx_rb0 + b, n_heads + h)),
            pl.BlockSpec((ctx_len, LANES), lambda h, b, i, kb_r, lay_r: (ctx_rb0 + b, 2 * n_heads + h)),
            pl.BlockSpec((None, n_lay, tq, nkw), lambda h, b, i, kb_r, lay_r: (h, 0, 0, 0)),
        ],
        out_specs=pl.BlockSpec((tq, LANES), lambda h, b, i, kb_r, lay_r: (b * n_blk + i, h)),
    )
    return pl.pallas_call(
        _na_kernel,
        grid_spec=grid_spec,
        out_shape=jax.ShapeDtypeStruct((n_batch * seq, n_heads * LANES), BF16),
        compiler_params=_params(("arbitrary", "arbitrary", "arbitrary"), n_lay * tq * nkw * 4 + 2 * seq * LANES * 2),
        name="na_attn",
    )(jnp.asarray(kb, jnp.int32), jnp.asarray(layout_of, jnp.int32), p1, p1, p1, p1, p1, bias)


def _pack_rows(rows, n_rows=8):
    out = jnp.zeros((n_rows, LANES), F32)
    for r, v in enumerate(rows):
        out = out.at[r, :v.shape[0]].set(v.astype(F32))
    return out


def kernel(x, c, ctx, c_ctx, ada_w, ada_b, norm_mix_g, norm_ffn_g, final_norm_g, ffn_w1, ffn_w3, ffn_w2, ev_w_in, ev_conv_w, ev_conv_b, ev_a_log, ev_dt_bias, ev_d_skip, ev_ssm_norm_g, ev_lam_q1, ev_lam_k1, ev_lam_q2, ev_lam_k2, ev_subln_g, ev_w_out, od_w_in, od_rpb, od_w_out):
    n_batch, seq, d = x.shape
    ctx_len = ctx.shape[1]
    depth = ada_w.shape[0]
    n_lat = n_batch * seq
    n_tok = n_lat + n_batch * ctx_len

    d_ssm = ev_ssm_norm_g.shape[1]
    n_ssm_heads = ev_d_skip.shape[1]
    d_xbc = ev_conv_w.shape[2]
    d_qk = d_v = (ev_w_in.shape[2] - d_ssm - d_xbc - 2 * n_ssm_heads) // 3
    n_diff_heads = d_v // (2 * DIFF_HEAD_DIM)
    n_na_heads = od_rpb.shape[1]

    stream = jnp.concatenate([x.reshape(n_lat, d), ctx.reshape(n_batch * ctx_len, d)], axis=0)
    cond = jnp.zeros((COND_ROWS, d), F32).at[:n_batch].set(c).at[n_batch].set(c_ctx)
    mod = _ada_mod(cond, ada_w, ada_b).reshape(depth, COND_ROWS, 1, 6 * d)
    rope_tabs = _rope_tables(seq)
    kw = dict(seq=seq, n_batch=n_batch)

    for i in range(depth):
        ctx_out = i < depth - 1
        m_rows = n_tok if ctx_out else n_lat
        j = i // 2
        if i % 2 == 0:
            lambda_init = 0.8 - 0.6 * math.exp(-0.3 * i)
            w_in = ev_w_in[j]
            dt0 = d_ssm + d_xbc
            w_main = jnp.concatenate([w_in[:, :dt0], w_in[:, dt0 + 2 * n_ssm_heads:]], axis=1).astype(BF16)
            w_dt = jnp.zeros((d, 2 * LANES), F32)
            w_dt = w_dt.at[:, :n_ssm_heads].set(w_in[:, dt0:dt0 + n_ssm_heads])
            w_dt = w_dt.at[:, LANES:LANES + n_ssm_heads].set(w_in[:, dt0 + n_ssm_heads:dt0 + 2 * n_ssm_heads])
            p0, dt_raw = _proj(stream, norm_mix_g[i], mod, i, w_main, w_dt.astype(BF16), F32, **kw)
            q0 = d_ssm + d_xbc
            k0, v0 = q0 + d_qk, q0 + 2 * d_qk

            xbc = _ssd_conv(p0, d_ssm, d_xbc, ev_conv_w[j], ev_conv_b[j], seq, ctx_len, n_lat)
            prm = jnp.stack([_pack_rows([ev_a_log[j, r], ev_dt_bias[j, r]]) for r in range(2)])
            y2 = _ssd_scan(xbc, dt_raw, prm, n_batch, seq, ctx_len, d_ssm)
            mix_ssd = _ssd_finish(y2, xbc, p0, jnp.repeat(ev_d_skip[j], SSM_HEAD_DIM), ev_ssm_norm_g[j], d_ssm)

            lam_p = _pack_rows([ev_lam_q1[j], ev_lam_k1[j], ev_lam_q2[j], ev_lam_k2[j]])
            attn_args = (p0, q0, k0, v0, n_diff_heads, rope_tabs, lam_p, ev_subln_g[j], lambda_init,
                         n_batch, seq, ctx_len)
            mix_attn = _diff_attn(*attn_args, with_lat=True)
            if ctx_out:
                mix_attn = jnp.concatenate([mix_attn, _diff_attn(*attn_args, with_lat=False)], axis=0)
            lhs = [mix_ssd, mix_attn]
            w_out = ev_w_out[j].astype(BF16)
        else:
            p1 = _proj(stream, norm_mix_g[i], mod, i, od_w_in[j].astype(BF16), None, BF16, **kw)
            assert not ctx_out, "context-query neighbourhood layers are not needed at this depth"
            lhs = [_na_attn(p1, od_rpb[j], n_batch, seq, ctx_len, n_na_heads)]
            w_out = od_w_out[j].astype(BF16)
        stream = _out_proj(lhs, w_out, stream, mod, i, 2, m_rows, **kw)
        stream = _ffn(stream, norm_ffn_g[i], mod, i, ffn_w1[i].astype(BF16), ffn_w3[i].astype(BF16),
                      ffn_w2[i].astype(BF16), final_norm_g, not ctx_out, m_rows, **kw)
    return stream.reshape(n_batch, seq, d)
```

```python
import functools
import math

import jax
import jax.numpy as jnp
import numpy as np
from jax import lax
from jax.experimental import pallas as pl
from jax.experimental.pallas import tpu as pltpu

F32 = jnp.float32
BF16 = jnp.bfloat16

GRID_W = 64
SSM_HEAD_DIM = 64
SSM_GROUPS = 4
SSM_STATE = 128
SSM_CONV = 5
SSM_CHUNK = 128
DIFF_HEAD_DIM = 64
NA_HEAD_DIM = 128
WIN_ROWS = 8
WIN_COLS = 16
ROPE_BASE = 10000.0
NORM_EPS = 1e-6
LOG2E = math.log2(math.e)

LANES = 128
V7X_VMEM_BYTES = 64 * 1024 * 1024
VMEM_HEADROOM_BYTES = 6 * 1024 * 1024

NA_Q_ROWS = 4
NA_K_ROWS = NA_Q_ROWS + WIN_ROWS - 1
COND_ROWS = 16


def _vmem_limit(block_bytes):
    return int(min(V7X_VMEM_BYTES - VMEM_HEADROOM_BYTES, max(32 * 1024 * 1024, 2 * block_bytes + 16 * 1024 * 1024)))


def _params(semantics, block_bytes):
    return pltpu.CompilerParams(dimension_semantics=semantics, vmem_limit_bytes=_vmem_limit(block_bytes))


def _silu(v):
    return v * jax.nn.sigmoid(v)


def _mod_row(i, tm, seq, n_batch):
    return jnp.minimum((i * tm) // seq, n_batch)


def _ada_kernel(cond_ref, w_ref, b_ref, o_ref):
    s = _silu(cond_ref[...]).astype(BF16)
    o_ref[...] = jnp.dot(s, w_ref[...].astype(BF16), preferred_element_type=F32) + b_ref[...]


def _ada_mod(cond, ada_w, ada_b, tn=1024):
    depth, d, n = ada_w.shape
    return pl.pallas_call(
        _ada_kernel,
        grid=(depth, n // tn),
        in_specs=[
            pl.BlockSpec((COND_ROWS, d), lambda l, j: (0, 0)),
            pl.BlockSpec((None, d, tn), lambda l, j: (l, 0, j)),
            pl.BlockSpec((None, 1, tn), lambda l, j: (l, 0, j)),
        ],
        out_specs=pl.BlockSpec((None, COND_ROWS, tn), lambda l, j: (l, 0, j)),
        out_shape=jax.ShapeDtypeStruct((depth, COND_ROWS, n), F32),
        compiler_params=_params(("arbitrary", "arbitrary"), d * tn * 4),
        name="ada_mod",
    )(cond, ada_w, ada_b.reshape(depth, 1, n))


def _norm_mod_into(x_ref, g_ref, mod_ref, h_ref, shift_idx, scale_idx, rows=128):
    d = x_ref.shape[-1]
    g = g_ref[...]
    shift = mod_ref[:, shift_idx * d:(shift_idx + 1) * d]
    scale1 = 1.0 + mod_ref[:, scale_idx * d:(scale_idx + 1) * d]

    def body(r, carry):
        sl = pl.ds(pl.multiple_of(r * rows, rows), rows)
        x = x_ref[sl, :]
        xn = x * lax.rsqrt(jnp.mean(x * x, axis=-1, keepdims=True) + NORM_EPS) * g
        h_ref[sl, :] = (xn * scale1 + shift).astype(h_ref.dtype)
        return carry

    lax.fori_loop(0, x_ref.shape[0] // rows, body, 0)


def _proj_kernel(x_ref, g_ref, mod_ref, w_ref, *rest, has_aux, has_scale):
    rest = list(rest)
    cs_ref = rest.pop(0) if has_scale else None
    if has_aux:
        waux_ref, o_ref, oaux_ref, h_ref = rest
    else:
        o_ref, h_ref = rest

    @pl.when(pl.program_id(1) == 0)
    def _():
        _norm_mod_into(x_ref, g_ref, mod_ref, h_ref, 0, 1)
        if has_aux:
            oaux_ref[...] = jnp.dot(h_ref[...], waux_ref[...], preferred_element_type=F32)

    acc = jnp.dot(h_ref[...], w_ref[...], preferred_element_type=F32)
    if has_scale:
        acc = acc * cs_ref[...]
    o_ref[...] = acc.astype(o_ref.dtype)


def _proj(x, g, mod, layer, w, w_aux, out_dtype, seq, n_batch, col_scale=None, tm=1024, tn=512):
    m, d = x.shape
    n = w.shape[1]
    has_aux = w_aux is not None
    has_scale = col_scale is not None
    mod_spec = pl.BlockSpec((None, None, 1, mod.shape[-1]),
                            lambda i, j: (layer, _mod_row(i, tm, seq, n_batch), 0, 0))
    in_specs = [
        pl.BlockSpec((tm, d), lambda i, j: (i, 0)),
        pl.BlockSpec((1, d), lambda i, j: (0, 0)),
        mod_spec,
        pl.BlockSpec((d, tn), lambda i, j: (0, j)),
    ]
    out_specs = [pl.BlockSpec((tm, tn), lambda i, j: (i, j))]
    out_shape = [jax.ShapeDtypeStruct((m, n), out_dtype)]
    args = [x, g.reshape(1, d), mod, w]
    if has_scale:
        in_specs.append(pl.BlockSpec((1, tn), lambda i, j: (0, j)))
        args.append(col_scale.reshape(1, n))
    if has_aux:
        na = w_aux.shape[1]
        in_specs.append(pl.BlockSpec((d, na), lambda i, j: (0, 0)))
        out_specs.append(pl.BlockSpec((tm, na), lambda i, j: (i, 0)))
        out_shape.append(jax.ShapeDtypeStruct((m, na), F32))
        args.append(w_aux)
    block_bytes = tm * d * 4 + d * tn * 2 + tm * tn * 4 + tm * d
    outs = pl.pallas_call(
        functools.partial(_proj_kernel, has_aux=has_aux, has_scale=has_scale),
        grid=(m // tm, n // tn),
        in_specs=in_specs,
        out_specs=out_specs,
        out_shape=out_shape,
        scratch_shapes=[pltpu.VMEM((tm, d), BF16)],
        compiler_params=_params(("arbitrary", "arbitrary"), block_bytes),
        name="proj",
    )(*args)
    return outs if has_aux else outs[0]


def _out_proj_kernel(*refs, n_lhs):
    lhs = refs[:n_lhs]
    ws = refs[n_lhs:2 * n_lhs]
    res_ref, gate_ref, o_ref = refs[2 * n_lhs:]
    acc = jnp.dot(lhs[0][...], ws[0][...], preferred_element_type=F32)
    for a_ref, w_ref in zip(lhs[1:], ws[1:]):
        acc = acc + jnp.dot(a_ref[...], w_ref[...], preferred_element_type=F32)
    o_ref[...] = res_ref[...] + gate_ref[...] * acc


def _out_proj(lhs_list, w, res, mod, layer, gate_idx, m_rows, seq, n_batch, tm=1024, tn=512):
    d = res.shape[1]
    n_lhs = len(lhs_list)
    in_specs, args = [], []
    for a in lhs_list:
        in_specs.append(pl.BlockSpec((tm, a.shape[1]), lambda i, j: (i, 0)))
        args.append(a)
    row0 = 0
    for a in lhs_list:
        kk = a.shape[1]
        in_specs.append(pl.BlockSpec((kk, tn), lambda i, j, rb=row0 // kk: (rb, j)))
        args.append(w)
        row0 += kk
    in_specs.append(pl.BlockSpec((tm, tn), lambda i, j: (i, j)))
    args.append(res)
    in_specs.append(pl.BlockSpec((None, None, 1, tn),
                                 lambda i, j: (layer, _mod_row(i, tm, seq, n_batch), 0, gate_idx * (d // tn) + j)))
    args.append(mod)
    k_total = sum(a.shape[1] for a in lhs_list)
    block_bytes = tm * k_total * 2 + k_total * tn * 2 + 2 * tm * tn * 4
    return pl.pallas_call(
        functools.partial(_out_proj_kernel, n_lhs=n_lhs),
        grid=(m_rows // tm, d // tn),
        in_specs=in_specs,
        out_specs=pl.BlockSpec((tm, tn), lambda i, j: (i, j)),
        out_shape=jax.ShapeDtypeStruct((m_rows, d), F32),
        compiler_params=_params(("arbitrary", "arbitrary"), block_bytes),
        name="out_proj",
    )(*args)


def _ffn_kernel(x_ref, g_ref, mod_ref, w1_ref, w3_ref, w2_ref, fg_ref, o_ref, h_ref, acc_ref, *, final_norm):
    j = pl.program_id(1)
    d = x_ref.shape[-1]

    @pl.when(j == 0)
    def _():
        _norm_mod_into(x_ref, g_ref, mod_ref, h_ref, 3, 4)
        acc_ref[...] = jnp.zeros_like(acc_ref)

    h = h_ref[...]
    a = jnp.dot(h, w1_ref[...], preferred_element_type=F32)
    b = jnp.dot(h, w3_ref[...], preferred_element_type=F32)
    u = (_silu(a) * b).astype(BF16)
    acc_ref[...] += jnp.dot(u, w2_ref[...], preferred_element_type=F32)

    @pl.when(j == pl.num_programs(1) - 1)
    def _():
        y = x_ref[...] + mod_ref[:, 5 * d:6 * d] * acc_ref[...]
        if final_norm:
            y = y * lax.rsqrt(jnp.mean(y * y, axis=-1, keepdims=True) + NORM_EPS) * fg_ref[...]
        o_ref[...] = y


def _ffn(x, g, mod, layer, w1, w3, w2, final_g, final_norm, m_rows, seq, n_batch, tm=512, tf=512):
    d = x.shape[1]
    ff = w1.shape[2]
    block_bytes = 2 * tm * d * 4 + 3 * d * tf * 2 + tm * d * 3
    return pl.pallas_call(
        functools.partial(_ffn_kernel, final_norm=final_norm),
        grid=(m_rows // tm, ff // tf),
        in_specs=[
            pl.BlockSpec((tm, d), lambda i, j: (i, 0)),
            pl.BlockSpec((1, d), lambda i, j: (0, 0)),
            pl.BlockSpec((None, None, 1, mod.shape[-1]),
                         lambda i, j: (layer, _mod_row(i, tm, seq, n_batch), 0, 0)),
            pl.BlockSpec((None, d, tf), lambda i, j: (layer, 0, j)),
            pl.BlockSpec((None, d, tf), lambda i, j: (layer, 0, j)),
            pl.BlockSpec((None, tf, d), lambda i, j: (layer, j, 0)),
            pl.BlockSpec((1, d), lambda i, j: (0, 0)),
        ],
        out_specs=pl.BlockSpec((tm, d), lambda i, j: (i, 0)),
        out_shape=jax.ShapeDtypeStruct((m_rows, d), F32),
        scratch_shapes=[pltpu.VMEM((tm, d), BF16), pltpu.VMEM((tm, d), F32)],
        compiler_params=_params(("arbitrary", "arbitrary"), block_bytes),
        name="ffn",
    )(x, g.reshape(1, d), mod, w1, w3, w2, final_g.reshape(1, d))


def _conv_kernel(prev_ref, cur_ref, next_ref, w_ref, b_ref, o_ref, ext_ref, *, tm, halo, seq, ctx_len, n_lat_tiles):
    i = pl.program_id(0)
    ext_ref[0:halo, :] = prev_ref[...]
    ext_ref[halo:halo + tm, :] = cur_ref[...]
    ext_ref[halo + tm:, :] = next_ref[...]
    seg_len = jnp.where(i < n_lat_tiles, seq, ctx_len)
    pos = (i * tm + lax.broadcasted_iota(jnp.int32, (tm, 1), 0)) & (seg_len - 1)
    acc = jnp.zeros(cur_ref.shape, F32) + b_ref[...]
    half = SSM_CONV // 2
    for t in range(SSM_CONV):
        src = pos + (t - half)
        tap = ext_ref[halo - half + t:halo - half + t + tm, :]
        acc = acc + jnp.where((src >= 0) & (src < seg_len), tap, 0.0) * w_ref[t:t + 1, :]
    o_ref[...] = _silu(acc)


def _ssd_conv(p0, col0, width, conv_w, conv_b, seq, ctx_len, n_lat_rows, tm=1024, tc=256, halo=8):
    m = p0.shape[0]
    assert seq % tm == 0 and tm % ctx_len == 0 and (seq & (seq - 1)) == 0 and (ctx_len & (ctx_len - 1)) == 0
    cb0 = col0 // tc
    hb = tm // halo
    last_hb = m // halo - 1
    wpad = jnp.zeros((8, width), F32).at[:SSM_CONV].set(conv_w)
    return pl.pallas_call(
        functools.partial(_conv_kernel, tm=tm, halo=halo, seq=seq, ctx_len=ctx_len, n_lat_tiles=n_lat_rows // tm),
        grid=(m // tm, width // tc),
        in_specs=[
            pl.BlockSpec((halo, tc), lambda i, j: (jnp.maximum(i * hb - 1, 0), cb0 + j)),
            pl.BlockSpec((tm, tc), lambda i, j: (i, cb0 + j)),
            pl.BlockSpec((halo, tc), lambda i, j: (jnp.minimum((i + 1) * hb, last_hb), cb0 + j)),
            pl.BlockSpec((8, tc), lambda i, j: (0, j)),
            pl.BlockSpec((1, tc), lambda i, j: (0, j)),
        ],
        out_specs=pl.BlockSpec((tm, tc), lambda i, j: (i, j)),
        out_shape=jax.ShapeDtypeStruct((m, width), F32),
        scratch_shapes=[pltpu.VMEM((tm + 2 * halo, tc), F32)],
        compiler_params=_params(("arbitrary", "arbitrary"), 3 * tm * tc * 4),
        name="ssd_conv",
    )(p0, p0, p0, wpad, conv_b.reshape(1, width))


def _ssd_scan_kernel(xs_ref, bm_ref, cm_ref, dt_ref, prm_ref, y_ref, state_ref, *, n_heads):
    dirn = pl.program_id(1)
    step = pl.program_id(2)
    hp = SSM_HEAD_DIM
    rep = n_heads // SSM_GROUPS
    lc = SSM_CHUNK

    @pl.when(step == 0)
    def _():
        state_ref[...] = jnp.zeros_like(state_ref)

    dt_in = dt_ref[...] + prm_ref[1:2, :]
    dt = jnp.maximum(dt_in, 0.0) + jnp.log1p(jnp.exp(-jnp.abs(dt_in)))
    dta = dt * (-jnp.exp(prm_ref[0:1, :]))
    row = lax.broadcasted_iota(jnp.int32, (lc, lc), 0)
    col = lax.broadcasted_iota(jnp.int32, (lc, lc), 1)
    causal = (row - col) * (1 - 2 * dirn) >= 0
    a_cum = jnp.dot(causal.astype(F32), dta, preferred_element_type=F32, precision=lax.Precision.HIGHEST)
    a_cum_t = a_cum.T
    a_tot = jnp.where(dirn == 0, a_cum[lc - 1:lc, :], a_cum[0:1, :])

    for g in range(SSM_GROUPS):
        gs = slice(g * SSM_STATE, (g + 1) * SSM_STATE)
        c_g = cm_ref[:, gs].astype(BF16)
        b_g = bm_ref[:, gs].astype(BF16)
        cb = lax.dot_general(c_g, b_g, (((1,), (1,)), ((), ())), preferred_element_type=F32)
        rows_g = slice(g * rep * hp, (g + 1) * rep * hp)
        h_in = state_ref[rows_g, :]
        y_off = lax.dot_general(c_g, h_in.astype(BF16), (((1,), (1,)), ((), ())), preferred_element_type=F32)
        ys, xws = [], []
        for hh in range(rep):
            h = g * rep + hh
            a_col = a_cum[:, h:h + 1]
            decay = jnp.exp(jnp.where(causal, a_col - a_cum_t[h:h + 1, :], -jnp.inf))
            xdt = xs_ref[:, h * hp:(h + 1) * hp] * dt[:, h:h + 1]
            y_diag = jnp.dot((cb * decay).astype(BF16), xdt.astype(BF16), preferred_element_type=F32)
            ys.append(y_diag + y_off[:, hh * hp:(hh + 1) * hp] * jnp.exp(a_col))
            xws.append(xdt * jnp.exp(a_tot[:, h:h + 1] - a_col))
        y_ref[:, rows_g] = jnp.concatenate(ys, axis=1)
        xw = jnp.concatenate(xws, axis=1).astype(BF16)
        s_new = lax.dot_general(xw, b_g, (((0,), (0,)), ((), ())), preferred_element_type=F32)
        for hh in range(rep):
            h = g * rep + hh
            rs = slice(h * hp, (h + 1) * hp)
            state_ref[rs, :] = (state_ref[rs, :] * jnp.exp(a_tot[:, h:h + 1])
                                + s_new[hh * hp:(hh + 1) * hp, :])


def _ssd_scan(xbc, dt_raw, prm, n_batch, seq, ctx_len, d_ssm):
    m = xbc.shape[0]
    lc = SSM_CHUNK
    n_heads = d_ssm // SSM_HEAD_DIM
    nc_ctx, nc_lat = ctx_len // lc, seq // lc
    ctx_blk0 = n_batch * nc_lat
    gn = SSM_GROUPS * SSM_STATE

    def row_blk(b, dirn, s):
        ctx_c = jnp.where(dirn == 0, s, nc_ctx - 1 - s)
        lat_c = jnp.where(dirn == 0, s - nc_ctx, nc_lat - 1 - (s - nc_ctx))
        return jnp.where(s < nc_ctx, ctx_blk0 + b * nc_ctx + ctx_c, b * nc_lat + lat_c)

    return pl.pallas_call(
        functools.partial(_ssd_scan_kernel, n_heads=n_heads),
        grid=(n_batch, 2, nc_ctx + nc_lat),
        in_specs=[
            pl.BlockSpec((lc, d_ssm), lambda b, r, s: (row_blk(b, r, s), 0)),
            pl.BlockSpec((lc, gn), lambda b, r, s: (row_blk(b, r, s), d_ssm // gn)),
            pl.BlockSpec((lc, gn), lambda b, r, s: (row_blk(b, r, s), d_ssm // gn + 1)),
            pl.BlockSpec((lc, LANES), lambda b, r, s: (row_blk(b, r, s), r)),
            pl.BlockSpec((None, 8, LANES), lambda b, r, s: (r, 0, 0)),
        ],
        out_specs=pl.BlockSpec((None, lc, d_ssm), lambda b, r, s: (r, row_blk(b, r, s), 0)),
        out_shape=jax.ShapeDtypeStruct((2, m, d_ssm), F32),
        scratch_shapes=[pltpu.VMEM((d_ssm, SSM_STATE), F32)],
        compiler_params=_params(("arbitrary", "arbitrary", "arbitrary"), 4 * lc * d_ssm * 4),
        name="ssd_scan",
    )(xbc, xbc, xbc, dt_raw, prm)


def _ssd_finish_kernel(y_ref, xs_ref, z_ref, dskip_ref, g_ref, o_ref, *, group_width):
    y = (y_ref[0] + y_ref[1] + dskip_ref[...] * xs_ref[...]) * _silu(z_ref[...])
    for g in range(y.shape[1] // group_width):
        sl = slice(g * group_width, (g + 1) * group_width)
        v = y[:, sl]
        vn = v * lax.rsqrt(jnp.mean(v * v, axis=-1, keepdims=True) + NORM_EPS) * g_ref[:, sl]
        o_ref[:, sl] = vn.astype(o_ref.dtype)


def _ssd_finish(y2, xbc, p0, d_skip_lanes, norm_g, d_ssm, tm=512):
    m = xbc.shape[0]
    return pl.pallas_call(
        functools.partial(_ssd_finish_kernel, group_width=d_ssm // SSM_GROUPS),
        grid=(m // tm,),
        in_specs=[
            pl.BlockSpec((2, tm, d_ssm), lambda i: (0, i, 0)),
            pl.BlockSpec((tm, d_ssm), lambda i: (i, 0)),
            pl.BlockSpec((tm, d_ssm), lambda i: (i, 0)),
            pl.BlockSpec((1, d_ssm), lambda i: (0, 0)),
            pl.BlockSpec((1, d_ssm), lambda i: (0, 0)),
        ],
        out_specs=pl.BlockSpec((tm, d_ssm), lambda i: (i, 0)),
        out_shape=jax.ShapeDtypeStruct((m, d_ssm), BF16),
        compiler_params=_params(("arbitrary",), 5 * tm * d_ssm * 4),
        name="ssd_finish",
    )(y2, xbc, p0, d_skip_lanes.reshape(1, d_ssm), norm_g.reshape(1, d_ssm))


def _rope(x, cos, sin_up, sin_dn):
    quarter = DIFF_HEAD_DIM // 4
    return (x * cos + pltpu.roll(x, LANES - quarter, axis=1) * sin_up + pltpu.roll(x, quarter, axis=1) * sin_dn)


def _diff_attn_kernel(*refs, with_lat, lambda_init, n_ctx):
    if with_lat:
        (q_ref, kc_ref, kl_ref, vc_ref, vl_ref, qcos_ref, qsu_ref, qsd_ref, kcos_ref, ksu_ref, ksd_ref,
         lam_ref, g_ref, o_ref, k_scr, v_scr) = refs
    else:
        q_ref, kc_ref, vc_ref, lam_ref, g_ref, _, o_ref, k_scr, v_scr = refs

    @pl.when(pl.program_id(2) == 0)
    def _():
        k_scr[0:n_ctx, :] = kc_ref[...].astype(BF16)
        v_scr[0:n_ctx, :] = vc_ref[...].astype(BF16)
        if with_lat:
            k_scr[n_ctx:, :] = _rope(kl_ref[...], kcos_ref[...], ksu_ref[...], ksd_ref[...]).astype(BF16)
            v_scr[n_ctx:, :] = vl_ref[...].astype(BF16)

    q = q_ref[...]
    tq = q.shape[0]
    if with_lat:
        q = _rope(q, qcos_ref[...], qsu_ref[...], qsd_ref[...])
    q = q * (DIFF_HEAD_DIM ** -0.5 * LOG2E)
    lo = lax.broadcasted_iota(jnp.int32, q.shape, 1) < DIFF_HEAD_DIM
    q2 = jnp.concatenate([jnp.where(lo, q, 0.0), jnp.where(lo, 0.0, q)], axis=0).astype(BF16)
    s = lax.dot_general(q2, k_scr[...], (((1,), (1,)), ((), ())), preferred_element_type=F32)
    e = jnp.exp2(s - jnp.max(s, axis=-1, keepdims=True))
    r = 1.0 / jnp.sum(e, axis=-1, keepdims=True)
    ov = jnp.dot(e.astype(BF16), v_scr[...], preferred_element_type=F32) * r
    lam_p = lam_ref[...]
    lam = (jnp.exp(jnp.sum(lam_p[0:1, :] * lam_p[1:2, :], axis=-1, keepdims=True))
           - jnp.exp(jnp.sum(lam_p[2:3, :] * lam_p[3:4, :], axis=-1, keepdims=True)) + lambda_init)
    o = ov[:tq] - lam * ov[tq:]
    o = o * lax.rsqrt(jnp.mean(o * o, axis=-1, keepdims=True) + NORM_EPS) * g_ref[...] * (1.0 - lambda_init)
    o_ref[...] = o.astype(o_ref.dtype)


def _diff_attn(p0, q_col0, k_col0, v_col0, n_heads, rope_tabs, lam_p, subln_g, lambda_init,
               n_batch, seq, ctx_len, with_lat, out_rows, into=None, tq=256):
    cb = lambda c0: c0 // LANES
    ctx_rb0 = (n_batch * seq) // ctx_len
    n_keys = ctx_len + (seq if with_lat else 0)
    if with_lat:
        n_q = seq // tq
        q_spec = pl.BlockSpec((tq, LANES), lambda b, h, i: (b * n_q + i, cb(q_col0) + h))
    else:
        tq = ctx_len
        n_q = 1
        q_spec = pl.BlockSpec((tq, LANES), lambda b, h, i: (ctx_rb0 + b, cb(q_col0) + h))
    kc_spec = pl.BlockSpec((ctx_len, LANES), lambda b, h, i: (ctx_rb0 + b, cb(k_col0) + h))
    vc_spec = pl.BlockSpec((ctx_len, LANES), lambda b, h, i: (ctx_rb0 + b, cb(v_col0) + h))
    small = [pl.BlockSpec((8, LANES), lambda b, h, i: (0, 0)), pl.BlockSpec((1, LANES), lambda b, h, i: (0, 0))]
    small_args = [lam_p, subln_g.reshape(1, LANES)]
    if with_lat:
        kl_spec = pl.BlockSpec((seq, LANES), lambda b, h, i: (b, cb(k_col0) + h))
        vl_spec = pl.BlockSpec((seq, LANES), lambda b, h, i: (b, cb(v_col0) + h))
        qt = pl.BlockSpec((tq, LANES), lambda b, h, i: (i, 0))
        kt = pl.BlockSpec((seq, LANES), lambda b, h, i: (0, 0))
        in_specs = [q_spec, kc_spec, kl_spec, vc_spec, vl_spec, qt, qt, qt, kt, kt, kt] + small
        args = [p0, p0, p0, p0, p0, *rope_tabs, *rope_tabs] + small_args
    else:
        in_specs = [q_spec, kc_spec, vc_spec] + small
        args = [p0, p0, p0] + small_args
    block_bytes = 10 * seq * LANES * 4 if with_lat else 8 * ctx_len * LANES * 4
    aliases = {}
    if with_lat:
        out_spec = pl.BlockSpec((tq, LANES), lambda b, h, i: (b * n_q + i, h))
    else:
        out_spec = pl.BlockSpec((tq, LANES), lambda b, h, i: (ctx_rb0 + b, h))
        in_specs.append(pl.BlockSpec(memory_space=pl.ANY))
        args.append(into)
        aliases = {len(args) - 1: 0}
    return pl.pallas_call(
        functools.partial(_diff_attn_kernel, with_lat=with_lat, lambda_init=lambda_init, n_ctx=ctx_len),
        grid=(n_batch, n_heads, n_q),
        in_specs=in_specs,
        out_specs=out_spec,
        out_shape=jax.ShapeDtypeStruct((out_rows, n_heads * LANES), BF16),
        scratch_shapes=[pltpu.VMEM((n_keys, LANES), BF16), pltpu.VMEM((n_keys, LANES), BF16)],
        input_output_aliases=aliases,
        compiler_params=_params(("arbitrary", "arbitrary", "arbitrary"), block_bytes),
        name="diff_attn_lat" if with_lat else "diff_attn_ctx",
    )(*args)


def _rope_tables(seq):
    half = DIFF_HEAD_DIM // 2
    pos = jnp.arange(seq)
    row, col = pos // GRID_W, pos % GRID_W
    inv_freq = ROPE_BASE ** (-jnp.arange(0, half, 2, dtype=F32) / half)
    lane = np.arange(LANES) % DIFF_HEAD_DIM
    use_col = jnp.asarray(lane >= half)
    first = jnp.asarray((lane % half) < half // 2)
    freq = inv_freq[jnp.asarray(lane % (half // 2))]
    p = jnp.where(use_col[None, :], col[:, None], row[:, None]).astype(F32)
    ang = p * freq[None, :]
    cos, sin = jnp.cos(ang), jnp.sin(ang)
    return cos, jnp.where(first[None, :], -sin, 0.0), jnp.where(first[None, :], 0.0, sin)


def _rpb_gather_kernel(rpb_ref, o_ref):
    n = o_ref.shape[1]
    k = rpb_ref.shape[1]
    colid = pl.program_id(0) * n + lax.broadcasted_iota(jnp.int32, (k, n), 1)
    j = lax.broadcasted_iota(jnp.int32, (k, n), 0)
    qc = lax.shift_right_logical(colid, int(math.log2(LANES)))
    half = lax.shift_right_logical(colid & (LANES - 1), int(math.log2(GRID_W)))
    kc = colid & (GRID_W - 1)
    sel = (j == half * LANES + jnp.clip(kc - qc + (WIN_COLS - 1), 0, 2 * WIN_COLS - 2)).astype(F32)
    o_ref[...] = jnp.dot(rpb_ref[...], sel, preferred_element_type=F32, precision=lax.Precision.HIGHEST)


def _rpb_pairs(rpb):
    nh, nr, ncol = rpb.shape
    n_slot = nr + 1
    left = jnp.pad(rpb, ((0, 0), (1, 0), (0, LANES - ncol)))
    right = jnp.pad(rpb, ((0, 0), (0, 1), (0, LANES - ncol)))
    rows = jnp.concatenate([left, right], axis=2).reshape(nh * n_slot, 2 * LANES)
    tn = 1024
    out = pl.pallas_call(
        _rpb_gather_kernel,
        grid=(GRID_W * LANES // tn,),
        in_specs=[pl.BlockSpec((nh * n_slot, 2 * LANES), lambda j: (0, 0))],
        out_specs=pl.BlockSpec((nh * n_slot, tn), lambda j: (0, j)),
        out_shape=jax.ShapeDtypeStruct((nh * n_slot, GRID_W * LANES), F32),
        compiler_params=_params(("arbitrary",), nh * n_slot * tn * 4 + 2 * LANES * tn * 4),
        name="rpb_gather",
    )(rows)
    return out.reshape(nh, n_slot, GRID_W, LANES)


def _na_block_plan(rows):
    kr = min(WIN_ROWS, rows)
    n_blk = rows // NA_Q_ROWS
    kb = np.clip(np.arange(n_blk) * NA_Q_ROWS - kr // 2, 0, rows - NA_K_ROWS)
    layouts, layout_of = [], []
    for blk in range(n_blk):
        dr = np.full((NA_Q_ROWS, NA_K_ROWS), -1, np.int64)
        for i in range(NA_Q_ROWS):
            r = blk * NA_Q_ROWS + i
            rs = int(np.clip(r - kr // 2, 0, rows - kr))
            for j in range(NA_K_ROWS):
                krow = kb[blk] + j
                if rs <= krow < rs + kr:
                    dr[i, j] = krow - r + WIN_ROWS - 1
        key = dr.tobytes()
        if key not in [l.tobytes() for l in layouts]:
            layouts.append(dr)
        layout_of.append([l.tobytes() for l in layouts].index(key))
    return kb, np.stack(layouts), np.asarray(layout_of)


def _na_bias_kernel(tp_ref, o_ref, *, layouts):
    qc = lax.broadcasted_iota(jnp.int32, (GRID_W, LANES), 0)
    lane = lax.broadcasted_iota(jnp.int32, (GRID_W, LANES), 1)
    kc = lane & (GRID_W - 1)
    left = lane < GRID_W
    col_start = jnp.clip(qc - WIN_COLS // 2, 0, GRID_W - WIN_COLS)
    col_ok = (kc >= col_start) & (kc < col_start + WIN_COLS)
    n_lay, n_q, n_k = layouts.shape
    for lay in range(n_lay):
        for i in range(n_q):
            for p in range(-(-n_k // 2)):
                d_l = int(layouts[lay, i, 2 * p])
                d_r = int(layouts[lay, i, 2 * p + 1]) if 2 * p + 1 < n_k else -1
                width = LANES if 2 * p + 1 < n_k else GRID_W
                if d_l < 0 and d_r < 0:
                    tile = jnp.full((GRID_W, LANES), -jnp.inf, F32)
                else:
                    assert d_l < 0 or d_r < 0 or d_r == d_l + 1
                    ok = col_ok
                    if d_l < 0:
                        ok = ok & jnp.logical_not(left)
                    if d_r < 0:
                        ok = ok & left
                    slot = d_r if d_r >= 0 else d_l + 1
                    tile = jnp.where(ok, tp_ref[slot] * LOG2E, -jnp.inf)
                o_ref[lay, i * GRID_W:(i + 1) * GRID_W, p * LANES:p * LANES + width] = tile[:, :width]


def _na_bias(rpb_pairs, layouts):
    nh, n_slot = rpb_pairs.shape[:2]
    n_lay = layouts.shape[0]
    tq, nkw = NA_Q_ROWS * GRID_W, NA_K_ROWS * GRID_W
    return pl.pallas_call(
        functools.partial(_na_bias_kernel, layouts=layouts),
        grid=(nh,),
        in_specs=[pl.BlockSpec((None, n_slot, GRID_W, LANES), lambda h: (h, 0, 0, 0))],
        out_specs=pl.BlockSpec((None, n_lay, tq, nkw), lambda h: (h, 0, 0, 0)),
        out_shape=jax.ShapeDtypeStruct((nh, n_lay, tq, nkw), F32),
        compiler_params=_params(("arbitrary",), n_lay * tq * nkw * 4),
        name="na_bias",
    )(rpb_pairs)


def _na_kernel(q_ref, kl_ref, vl_ref, kc_ref, vc_ref, bias_ref, o_ref, *, plan):
    tq = NA_Q_ROWS * GRID_W
    nkw = NA_K_ROWS * GRID_W
    nt = (((1,), (1,)), ((), ()))
    kc = kc_ref[...]
    vc = vc_ref[...]
    for blk, (kb, lay) in enumerate(plan):
        qs = slice(blk * tq, (blk + 1) * tq)
        ks = slice(kb * GRID_W, kb * GRID_W + nkw)
        q = q_ref[qs, :]
        s_w = lax.dot_general(q, kl_ref[ks, :], nt, preferred_element_type=F32) + bias_ref[lay]
        s_c = lax.dot_general(q, kc, nt, preferred_element_type=F32)
        m = jnp.maximum(jnp.max(s_w, axis=-1, keepdims=True), jnp.max(s_c, axis=-1, keepdims=True))
        e_w = jnp.exp2(s_w - m)
        e_c = jnp.exp2(s_c - m)
        l = jnp.sum(e_w, axis=-1, keepdims=True) + jnp.sum(e_c, axis=-1, keepdims=True)
        o = (jnp.dot(e_w.astype(BF16), vl_ref[ks, :], preferred_element_type=F32)
             + jnp.dot(e_c.astype(BF16), vc, preferred_element_type=F32))
        o_ref[qs, :] = (o * (1.0 / l)).astype(o_ref.dtype)


def _na_attn(p1, rpb, n_batch, seq, ctx_len, n_heads):
    rows = seq // GRID_W
    kb, layouts, layout_of = _na_block_plan(rows)
    bias = _na_bias(_rpb_pairs(rpb), layouts)
    n_lay = layouts.shape[0]
    tq = NA_Q_ROWS * GRID_W
    nkw = NA_K_ROWS * GRID_W
    ctx_rb0 = (n_batch * seq) // ctx_len
    plan = tuple((int(k), int(l)) for k, l in zip(kb, layout_of))
    return pl.pallas_call(
        functools.partial(_na_kernel, plan=plan),
        grid=(n_heads, n_batch),
        in_specs=[
            pl.BlockSpec((seq, LANES), lambda h, b: (b, h)),
            pl.BlockSpec((seq, LANES), lambda h, b: (b, n_heads + h)),
            pl.BlockSpec((seq, LANES), lambda h, b: (b, 2 * n_heads + h)),
            pl.BlockSpec((ctx_len, LANES), lambda h, b: (ctx_rb0 + b, n_heads + h)),
            pl.BlockSpec((ctx_len, LANES), lambda h, b: (ctx_rb0 + b, 2 * n_heads + h)),
            pl.BlockSpec((None, n_lay, tq, nkw), lambda h, b: (h, 0, 0, 0)),
        ],
        out_specs=pl.BlockSpec((seq, LANES), lambda h, b: (b, h)),
        out_shape=jax.ShapeDtypeStruct((n_batch * seq, n_heads * LANES), BF16),
        compiler_params=_params(("arbitrary", "arbitrary"), n_lay * tq * nkw * 4 + 4 * seq * LANES * 2),
        name="na_attn",
    )(p1, p1, p1, p1, p1, bias)


def _pack_rows(rows, n_rows=8):
    out = jnp.zeros((n_rows, LANES), F32)
    for r, v in enumerate(rows):
        out = out.at[r, :v.shape[0]].set(v.astype(F32))
    return out


def kernel(x, c, ctx, c_ctx, ada_w, ada_b, norm_mix_g, norm_ffn_g, final_norm_g, ffn_w1, ffn_w3, ffn_w2, ev_w_in, ev_conv_w, ev_conv_b, ev_a_log, ev_dt_bias, ev_d_skip, ev_ssm_norm_g, ev_lam_q1, ev_lam_k1, ev_lam_q2, ev_lam_k2, ev_subln_g, ev_w_out, od_w_in, od_rpb, od_w_out):
    n_batch, seq, d = x.shape
    ctx_len = ctx.shape[1]
    depth = ada_w.shape[0]
    n_lat = n_batch * seq
    n_tok = n_lat + n_batch * ctx_len

    d_ssm = ev_ssm_norm_g.shape[1]
    n_ssm_heads = ev_d_skip.shape[1]
    d_xbc = ev_conv_w.shape[2]
    d_qk = d_v = (ev_w_in.shape[2] - d_ssm - d_xbc - 2 * n_ssm_heads) // 3
    n_diff_heads = d_v // (2 * DIFF_HEAD_DIM)
    n_na_heads = od_rpb.shape[1]

    stream = jnp.concatenate([x.reshape(n_lat, d), ctx.reshape(n_batch * ctx_len, d)], axis=0)
    cond = jnp.zeros((COND_ROWS, d), F32).at[:n_batch].set(c).at[n_batch].set(c_ctx)
    mod = _ada_mod(cond, ada_w, ada_b).reshape(depth, COND_ROWS, 1, 6 * d)
    rope_tabs = _rope_tables(seq)
    kw = dict(seq=seq, n_batch=n_batch)
    ffn_w = [w.astype(BF16) for w in (ffn_w1, ffn_w3, ffn_w2)]

    for i in range(depth):
        ctx_out = i < depth - 1
        m_rows = n_tok if ctx_out else n_lat
        j = i // 2
        if i % 2 == 0:
            lambda_init = 0.8 - 0.6 * math.exp(-0.3 * i)
            w_in = ev_w_in[j]
            dt0 = d_ssm + d_xbc
            w_main = jnp.concatenate([w_in[:, :dt0].astype(BF16), w_in[:, dt0 + 2 * n_ssm_heads:].astype(BF16)], axis=1)
            w_dt = jnp.zeros((d, 2 * LANES), F32)
            w_dt = w_dt.at[:, :n_ssm_heads].set(w_in[:, dt0:dt0 + n_ssm_heads])
            w_dt = w_dt.at[:, LANES:LANES + n_ssm_heads].set(w_in[:, dt0 + n_ssm_heads:dt0 + 2 * n_ssm_heads])
            p0, dt_raw = _proj(stream, norm_mix_g[i], mod, i, w_main, w_dt.astype(BF16), F32, **kw)
            q0 = d_ssm + d_xbc
            k0, v0 = q0 + d_qk, q0 + 2 * d_qk

            xbc = _ssd_conv(p0, d_ssm, d_xbc, ev_conv_w[j], ev_conv_b[j], seq, ctx_len, n_lat)
            prm = jnp.stack([_pack_rows([ev_a_log[j, r], ev_dt_bias[j, r]]) for r in range(2)])
            y2 = _ssd_scan(xbc, dt_raw, prm, n_batch, seq, ctx_len, d_ssm)
            mix_ssd = _ssd_finish(y2, xbc, p0, jnp.repeat(ev_d_skip[j], SSM_HEAD_DIM), ev_ssm_norm_g[j], d_ssm)

            lam_p = _pack_rows([ev_lam_q1[j], ev_lam_k1[j], ev_lam_q2[j], ev_lam_k2[j]])
            attn_args = (p0, q0, k0, v0, n_diff_heads, rope_tabs, lam_p, ev_subln_g[j], lambda_init,
                         n_batch, seq, ctx_len)
            mix_attn = _diff_attn(*attn_args, with_lat=True, out_rows=m_rows)
            if ctx_out:
                mix_attn = _diff_attn(*attn_args, with_lat=False, out_rows=m_rows, into=mix_attn)
            lhs = [mix_ssd, mix_attn]
            w_out = ev_w_out[j].astype(BF16)
        else:
            d_na = n_na_heads * NA_HEAD_DIM
            q_scale = jnp.where(jnp.arange(3 * d_na) < d_na, NA_HEAD_DIM ** -0.5 * LOG2E, 1.0).astype(F32)
            p1 = _proj(stream, norm_mix_g[i], mod, i, od_w_in[j].astype(BF16), None, BF16, col_scale=q_scale, **kw)
            assert not ctx_out, "context-query neighbourhood layers are not needed at this depth"
            lhs = [_na_attn(p1, od_rpb[j], n_batch, seq, ctx_len, n_na_heads)]
            w_out = od_w_out[j].astype(BF16)
        stream = _out_proj(lhs, w_out, stream, mod, i, 2, m_rows, **kw)
        stream = _ffn(stream, norm_ffn_g[i], mod, i, *ffn_w, final_norm_g, not ctx_out, m_rows, **kw)
    return stream.reshape(n_batch, seq, d)
```

```python
import functools
import math

import jax
import jax.numpy as jnp
import numpy as np
from jax import lax
from jax.experimental import pallas as pl
from jax.experimental.pallas import tpu as pltpu

F32 = jnp.float32
BF16 = jnp.bfloat16

GRID_W = 64
SSM_HEAD_DIM = 64
SSM_GROUPS = 4
SSM_STATE = 128
SSM_CONV = 5
SSM_CHUNK = 128
DIFF_HEAD_DIM = 64
NA_HEAD_DIM = 128
WIN_ROWS = 8
WIN_COLS = 16
ROPE_BASE = 10000.0
NORM_EPS = 1e-6
LOG2E = math.log2(math.e)

LANES = 128
V7X_VMEM_BYTES = 64 * 1024 * 1024
VMEM_HEADROOM_BYTES = 6 * 1024 * 1024

NA_Q_ROWS = 4
NA_K_ROWS = NA_Q_ROWS + WIN_ROWS - 1
COND_ROWS = 16


def _vmem_limit(block_bytes):
    return int(min(V7X_VMEM_BYTES - VMEM_HEADROOM_BYTES, max(32 * 1024 * 1024, 2 * block_bytes + 16 * 1024 * 1024)))


def _params(semantics, block_bytes):
    return pltpu.CompilerParams(dimension_semantics=semantics, vmem_limit_bytes=_vmem_limit(block_bytes))


def _silu(v):
    return v * jax.nn.sigmoid(v)


def _mod_row(i, tm, seq, n_batch):
    return jnp.minimum((i * tm) // seq, n_batch)


def _ada_kernel(cond_ref, w_ref, b_ref, o_ref):
    s = _silu(cond_ref[...]).astype(BF16)
    o_ref[...] = jnp.dot(s, w_ref[...].astype(BF16), preferred_element_type=F32) + b_ref[...]


def _ada_mod(cond, ada_w, ada_b, tn=1024):
    depth, d, n = ada_w.shape
    return pl.pallas_call(
        _ada_kernel,
        grid=(depth, n // tn),
        in_specs=[
            pl.BlockSpec((COND_ROWS, d), lambda l, j: (0, 0)),
            pl.BlockSpec((None, d, tn), lambda l, j: (l, 0, j)),
            pl.BlockSpec((None, 1, tn), lambda l, j: (l, 0, j)),
        ],
        out_specs=pl.BlockSpec((None, COND_ROWS, tn), lambda l, j: (l, 0, j)),
        out_shape=jax.ShapeDtypeStruct((depth, COND_ROWS, n), F32),
        compiler_params=_params(("arbitrary", "arbitrary"), d * tn * 4),
        name="ada_mod",
    )(cond, ada_w, ada_b.reshape(depth, 1, n))


def _norm_mod_into(x_ref, g_ref, mod_ref, h_ref, shift_idx, scale_idx, rows=128):
    d = x_ref.shape[-1]
    g = g_ref[...]
    shift = mod_ref[:, shift_idx * d:(shift_idx + 1) * d]
    scale1 = 1.0 + mod_ref[:, scale_idx * d:(scale_idx + 1) * d]

    def body(r, carry):
        sl = pl.ds(pl.multiple_of(r * rows, rows), rows)
        x = x_ref[sl, :]
        xn = x * lax.rsqrt(jnp.mean(x * x, axis=-1, keepdims=True) + NORM_EPS) * g
        h_ref[sl, :] = (xn * scale1 + shift).astype(h_ref.dtype)
        return carry

    lax.fori_loop(0, x_ref.shape[0] // rows, body, 0)


def _proj_kernel(x_ref, g_ref, mod_ref, w_ref, *rest, has_aux, has_scale):
    rest = list(rest)
    cs_ref = rest.pop(0) if has_scale else None
    if has_aux:
        waux_ref, o_ref, oaux_ref, h_ref = rest
    else:
        o_ref, h_ref = rest

    @pl.when(pl.program_id(1) == 0)
    def _():
        _norm_mod_into(x_ref, g_ref, mod_ref, h_ref, 0, 1)
        if has_aux:
            oaux_ref[...] = jnp.dot(h_ref[...], waux_ref[...], preferred_element_type=F32)

    acc = jnp.dot(h_ref[...], w_ref[...], preferred_element_type=F32)
    if has_scale:
        acc = acc * cs_ref[...]
    o_ref[...] = acc.astype(o_ref.dtype)


def _proj(x, g, mod, layer, w, w_aux, out_dtype, seq, n_batch, col_scale=None, tm=1024, tn=1024):
    m, d = x.shape
    n = w.shape[1]
    has_aux = w_aux is not None
    has_scale = col_scale is not None
    mod_spec = pl.BlockSpec((None, None, 1, mod.shape[-1]),
                            lambda i, j: (layer, _mod_row(i, tm, seq, n_batch), 0, 0))
    in_specs = [
        pl.BlockSpec((tm, d), lambda i, j: (i, 0)),
        pl.BlockSpec((1, d), lambda i, j: (0, 0)),
        mod_spec,
        pl.BlockSpec((d, tn), lambda i, j: (0, j)),
    ]
    out_specs = [pl.BlockSpec((tm, tn), lambda i, j: (i, j))]
    out_shape = [jax.ShapeDtypeStruct((m, n), out_dtype)]
    args = [x, g.reshape(1, d), mod, w]
    if has_scale:
        in_specs.append(pl.BlockSpec((1, tn), lambda i, j: (0, j)))
        args.append(col_scale.reshape(1, n))
    if has_aux:
        na = w_aux.shape[1]
        in_specs.append(pl.BlockSpec((d, na), lambda i, j: (0, 0)))
        out_specs.append(pl.BlockSpec((tm, na), lambda i, j: (i, 0)))
        out_shape.append(jax.ShapeDtypeStruct((m, na), F32))
        args.append(w_aux)
    block_bytes = tm * d * 4 + d * tn * 2 + tm * tn * 4 + tm * d
    outs = pl.pallas_call(
        functools.partial(_proj_kernel, has_aux=has_aux, has_scale=has_scale),
        grid=(m // tm, n // tn),
        in_specs=in_specs,
        out_specs=out_specs,
        out_shape=out_shape,
        scratch_shapes=[pltpu.VMEM((tm, d), BF16)],
        compiler_params=_params(("arbitrary", "arbitrary"), block_bytes),
        name="proj",
    )(*args)
    return outs if has_aux else outs[0]


def _out_proj_kernel(*refs, n_lhs):
    lhs = refs[:n_lhs]
    ws = refs[n_lhs:2 * n_lhs]
    res_ref, gate_ref, o_ref = refs[2 * n_lhs:]
    acc = jnp.dot(lhs[0][...], ws[0][...], preferred_element_type=F32)
    for a_ref, w_ref in zip(lhs[1:], ws[1:]):
        acc = acc + jnp.dot(a_ref[...], w_ref[...], preferred_element_type=F32)
    o_ref[...] = res_ref[...] + gate_ref[...] * acc


def _out_proj(lhs_list, w, res, mod, layer, gate_idx, m_rows, seq, n_batch, tm=1024, tn=1024):
    d = res.shape[1]
    n_lhs = len(lhs_list)
    in_specs, args = [], []
    for a in lhs_list:
        in_specs.append(pl.BlockSpec((tm, a.shape[1]), lambda i, j: (i, 0)))
        args.append(a)
    row0 = 0
    for a in lhs_list:
        kk = a.shape[1]
        in_specs.append(pl.BlockSpec((kk, tn), lambda i, j, rb=row0 // kk: (rb, j)))
        args.append(w)
        row0 += kk
    in_specs.append(pl.BlockSpec((tm, tn), lambda i, j: (i, j)))
    args.append(res)
    in_specs.append(pl.BlockSpec((None, None, 1, tn),
                                 lambda i, j: (layer, _mod_row(i, tm, seq, n_batch), 0, gate_idx * (d // tn) + j)))
    args.append(mod)
    k_total = sum(a.shape[1] for a in lhs_list)
    block_bytes = tm * k_total * 2 + k_total * tn * 2 + 2 * tm * tn * 4
    return pl.pallas_call(
        functools.partial(_out_proj_kernel, n_lhs=n_lhs),
        grid=(m_rows // tm, d // tn),
        in_specs=in_specs,
        out_specs=pl.BlockSpec((tm, tn), lambda i, j: (i, j)),
        out_shape=jax.ShapeDtypeStruct((m_rows, d), F32),
        compiler_params=_params(("arbitrary", "arbitrary"), block_bytes),
        name="out_proj",
    )(*args)


def _ffn_kernel(x_ref, g_ref, mod_ref, w1_ref, w3_ref, w2_ref, fg_ref, o_ref, h_ref, *, final_norm, rows):
    j = pl.program_id(1)
    d = x_ref.shape[-1]

    @pl.when(j == 0)
    def _():
        _norm_mod_into(x_ref, g_ref, mod_ref, h_ref, 3, 4)
        o_ref[...] = jnp.zeros_like(o_ref)

    h = h_ref[...]
    a = jnp.dot(h, w1_ref[...], preferred_element_type=F32)
    b = jnp.dot(h, w3_ref[...], preferred_element_type=F32)
    u = (_silu(a) * b).astype(BF16)
    o_ref[...] += jnp.dot(u, w2_ref[...], preferred_element_type=F32)

    @pl.when(j == pl.num_programs(1) - 1)
    def _():
        gate = mod_ref[:, 5 * d:6 * d]
        fg = fg_ref[...]

        def body(r, carry):
            sl = pl.ds(pl.multiple_of(r * rows, rows), rows)
            y = x_ref[sl, :] + gate * o_ref[sl, :]
            if final_norm:
                y = y * lax.rsqrt(jnp.mean(y * y, axis=-1, keepdims=True) + NORM_EPS) * fg
            o_ref[sl, :] = y
            return carry

        lax.fori_loop(0, x_ref.shape[0] // rows, body, 0)


def _ffn(x, g, mod, layer, w1, w3, w2, final_g, final_norm, m_rows, seq, n_batch, tm=1024, tf=512):
    d = x.shape[1]
    ff = w1.shape[2]
    block_bytes = 2 * tm * d * 4 + 3 * d * tf * 2 + tm * d
    return pl.pallas_call(
        functools.partial(_ffn_kernel, final_norm=final_norm, rows=128),
        grid=(m_rows // tm, ff // tf),
        in_specs=[
            pl.BlockSpec((tm, d), lambda i, j: (i, 0)),
            pl.BlockSpec((1, d), lambda i, j: (0, 0)),
            pl.BlockSpec((None, None, 1, mod.shape[-1]),
                         lambda i, j: (layer, _mod_row(i, tm, seq, n_batch), 0, 0)),
            pl.BlockSpec((None, d, tf), lambda i, j: (layer, 0, j)),
            pl.BlockSpec((None, d, tf), lambda i, j: (layer, 0, j)),
            pl.BlockSpec((None, tf, d), lambda i, j: (layer, j, 0)),
            pl.BlockSpec((1, d), lambda i, j: (0, 0)),
        ],
        out_specs=pl.BlockSpec((tm, d), lambda i, j: (i, 0)),
        out_shape=jax.ShapeDtypeStruct((m_rows, d), F32),
        scratch_shapes=[pltpu.VMEM((tm, d), BF16)],
        compiler_params=_params(("arbitrary", "arbitrary"), block_bytes),
        name="ffn",
    )(x, g.reshape(1, d), mod, w1, w3, w2, final_g.reshape(1, d))


def _conv_kernel(prev_ref, cur_ref, next_ref, w_ref, b_ref, o_ref, ext_ref, *, tm, halo, seq, ctx_len, n_lat_tiles):
    i = pl.program_id(0)
    ext_ref[0:halo, :] = prev_ref[...]
    ext_ref[halo:halo + tm, :] = cur_ref[...]
    ext_ref[halo + tm:, :] = next_ref[...]
    seg_len = jnp.where(i < n_lat_tiles, seq, ctx_len)
    pos = (i * tm + lax.broadcasted_iota(jnp.int32, (tm, 1), 0)) & (seg_len - 1)
    acc = jnp.zeros(cur_ref.shape, F32) + b_ref[...]
    half = SSM_CONV // 2
    for t in range(SSM_CONV):
        src = pos + (t - half)
        tap = ext_ref[halo - half + t:halo - half + t + tm, :]
        acc = acc + jnp.where((src >= 0) & (src < seg_len), tap, 0.0) * w_ref[t:t + 1, :]
    o_ref[...] = _silu(acc)


def _ssd_conv(p0, col0, width, conv_w, conv_b, seq, ctx_len, n_lat_rows, tm=1024, tc=512, halo=8):
    m = p0.shape[0]
    assert seq % tm == 0 and tm % ctx_len == 0 and (seq & (seq - 1)) == 0 and (ctx_len & (ctx_len - 1)) == 0
    cb0 = col0 // tc
    hb = tm // halo
    last_hb = m // halo - 1
    wpad = jnp.zeros((8, width), F32).at[:SSM_CONV].set(conv_w)
    return pl.pallas_call(
        functools.partial(_conv_kernel, tm=tm, halo=halo, seq=seq, ctx_len=ctx_len, n_lat_tiles=n_lat_rows // tm),
        grid=(m // tm, width // tc),
        in_specs=[
            pl.BlockSpec((halo, tc), lambda i, j: (jnp.maximum(i * hb - 1, 0), cb0 + j)),
            pl.BlockSpec((tm, tc), lambda i, j: (i, cb0 + j)),
            pl.BlockSpec((halo, tc), lambda i, j: (jnp.minimum((i + 1) * hb, last_hb), cb0 + j)),
            pl.BlockSpec((8, tc), lambda i, j: (0, j)),
            pl.BlockSpec((1, tc), lambda i, j: (0, j)),
        ],
        out_specs=pl.BlockSpec((tm, tc), lambda i, j: (i, j)),
        out_shape=jax.ShapeDtypeStruct((m, width), F32),
        scratch_shapes=[pltpu.VMEM((tm + 2 * halo, tc), F32)],
        compiler_params=_params(("arbitrary", "arbitrary"), 3 * tm * tc * 4),
        name="ssd_conv",
    )(p0, p0, p0, wpad, conv_b.reshape(1, width))


def _ssd_scan_kernel(xs_ref, bm_ref, cm_ref, dt_ref, prm_ref, y_ref, state_ref, *, n_heads):
    dirn = pl.program_id(1)
    step = pl.program_id(2)
    hp = SSM_HEAD_DIM
    rep = n_heads // SSM_GROUPS
    lc = SSM_CHUNK

    @pl.when(step == 0)
    def _():
        state_ref[...] = jnp.zeros_like(state_ref)

    dt_in = dt_ref[...] + prm_ref[1:2, :]
    dt = jnp.maximum(dt_in, 0.0) + jnp.log1p(jnp.exp(-jnp.abs(dt_in)))
    dta = dt * (-jnp.exp(prm_ref[0:1, :]))
    row = lax.broadcasted_iota(jnp.int32, (lc, lc), 0)
    col = lax.broadcasted_iota(jnp.int32, (lc, lc), 1)
    causal = (row - col) * (1 - 2 * dirn) >= 0
    a_cum = jnp.dot(causal.astype(F32), dta, preferred_element_type=F32, precision=lax.Precision.HIGHEST)
    a_cum_t = a_cum.T
    a_tot = jnp.where(dirn == 0, a_cum[lc - 1:lc, :], a_cum[0:1, :])

    for g in range(SSM_GROUPS):
        gs = slice(g * SSM_STATE, (g + 1) * SSM_STATE)
        c_g = cm_ref[:, gs].astype(BF16)
        b_g = bm_ref[:, gs].astype(BF16)
        cb = lax.dot_general(c_g, b_g, (((1,), (1,)), ((), ())), preferred_element_type=F32)
        rows_g = slice(g * rep * hp, (g + 1) * rep * hp)
        h_in = state_ref[rows_g, :]
        y_off = lax.dot_general(c_g, h_in.astype(BF16), (((1,), (1,)), ((), ())), preferred_element_type=F32)
        ys, xws = [], []
        for hh in range(rep):
            h = g * rep + hh
            a_col = a_cum[:, h:h + 1]
            decay = jnp.exp(jnp.where(causal, a_col - a_cum_t[h:h + 1, :], -jnp.inf))
            xdt = xs_ref[:, h * hp:(h + 1) * hp] * dt[:, h:h + 1]
            y_diag = jnp.dot((cb * decay).astype(BF16), xdt.astype(BF16), preferred_element_type=F32)
            ys.append(y_diag + y_off[:, hh * hp:(hh + 1) * hp] * jnp.exp(a_col))
            xws.append(xdt * jnp.exp(a_tot[:, h:h + 1] - a_col))
        y_ref[:, rows_g] = jnp.concatenate(ys, axis=1)
        xw = jnp.concatenate(xws, axis=1).astype(BF16)
        s_new = lax.dot_general(xw, b_g, (((0,), (0,)), ((), ())), preferred_element_type=F32)
        for hh in range(rep):
            h = g * rep + hh
            rs = slice(h * hp, (h + 1) * hp)
            state_ref[rs, :] = (state_ref[rs, :] * jnp.exp(a_tot[:, h:h + 1])
                                + s_new[hh * hp:(hh + 1) * hp, :])


def _ssd_scan(xbc, dt_raw, prm, n_batch, seq, ctx_len, d_ssm):
    m = xbc.shape[0]
    lc = SSM_CHUNK
    n_heads = d_ssm // SSM_HEAD_DIM
    nc_ctx, nc_lat = ctx_len // lc, seq // lc
    ctx_blk0 = n_batch * nc_lat
    gn = SSM_GROUPS * SSM_STATE

    def row_blk(b, dirn, s):
        ctx_c = jnp.where(dirn == 0, s, nc_ctx - 1 - s)
        lat_c = jnp.where(dirn == 0, s - nc_ctx, nc_lat - 1 - (s - nc_ctx))
        return jnp.where(s < nc_ctx, ctx_blk0 + b * nc_ctx + ctx_c, b * nc_lat + lat_c)

    return pl.pallas_call(
        functools.partial(_ssd_scan_kernel, n_heads=n_heads),
        grid=(n_batch, 2, nc_ctx + nc_lat),
        in_specs=[
            pl.BlockSpec((lc, d_ssm), lambda b, r, s: (row_blk(b, r, s), 0)),
            pl.BlockSpec((lc, gn), lambda b, r, s: (row_blk(b, r, s), d_ssm // gn)),
            pl.BlockSpec((lc, gn), lambda b, r, s: (row_blk(b, r, s), d_ssm // gn + 1)),
            pl.BlockSpec((lc, LANES), lambda b, r, s: (row_blk(b, r, s), r)),
            pl.BlockSpec((None, 8, LANES), lambda b, r, s: (r, 0, 0)),
        ],
        out_specs=pl.BlockSpec((None, lc, d_ssm), lambda b, r, s: (r, row_blk(b, r, s), 0)),
        out_shape=jax.ShapeDtypeStruct((2, m, d_ssm), F32),
        scratch_shapes=[pltpu.VMEM((d_ssm, SSM_STATE), F32)],
        compiler_params=_params(("arbitrary", "arbitrary", "arbitrary"), 4 * lc * d_ssm * 4),
        name="ssd_scan",
    )(xbc, xbc, xbc, dt_raw, prm)


def _ssd_finish_kernel(y_ref, xs_ref, z_ref, dskip_ref, g_ref, o_ref, *, group_width):
    y = (y_ref[0] + y_ref[1] + dskip_ref[...] * xs_ref[...]) * _silu(z_ref[...])
    for g in range(y.shape[1] // group_width):
        sl = slice(g * group_width, (g + 1) * group_width)
        v = y[:, sl]
        vn = v * lax.rsqrt(jnp.mean(v * v, axis=-1, keepdims=True) + NORM_EPS) * g_ref[:, sl]
        o_ref[:, sl] = vn.astype(o_ref.dtype)


def _ssd_finish(y2, xbc, p0, d_skip_lanes, norm_g, d_ssm, tm=512):
    m = xbc.shape[0]
    return pl.pallas_call(
        functools.partial(_ssd_finish_kernel, group_width=d_ssm // SSM_GROUPS),
        grid=(m // tm,),
        in_specs=[
            pl.BlockSpec((2, tm, d_ssm), lambda i: (0, i, 0)),
            pl.BlockSpec((tm, d_ssm), lambda i: (i, 0)),
            pl.BlockSpec((tm, d_ssm), lambda i: (i, 0)),
            pl.BlockSpec((1, d_ssm), lambda i: (0, 0)),
            pl.BlockSpec((1, d_ssm), lambda i: (0, 0)),
        ],
        out_specs=pl.BlockSpec((tm, d_ssm), lambda i: (i, 0)),
        out_shape=jax.ShapeDtypeStruct((m, d_ssm), BF16),
        compiler_params=_params(("arbitrary",), 5 * tm * d_ssm * 4),
        name="ssd_finish",
    )(y2, xbc, p0, d_skip_lanes.reshape(1, d_ssm), norm_g.reshape(1, d_ssm))


def _rope(x, cos, sin_up, sin_dn):
    quarter = DIFF_HEAD_DIM // 4
    return (x * cos + pltpu.roll(x, LANES - quarter, axis=1) * sin_up + pltpu.roll(x, quarter, axis=1) * sin_dn)


def _diff_attn_kernel(*refs, with_lat, lambda_init, n_ctx, sub_q):
    if with_lat:
        (q_ref, kc_ref, kl_ref, vc_ref, vl_ref, qcos_ref, qsu_ref, qsd_ref, kcos_ref, ksu_ref, ksd_ref,
         lam_ref, g_ref, o_ref, k_scr, v_scr) = refs
    else:
        q_ref, kc_ref, vc_ref, lam_ref, g_ref, _, o_ref, k_scr, v_scr = refs

    @pl.when(pl.program_id(2) == 0)
    def _():
        k_scr[:, 0:n_ctx] = kc_ref[...].T.astype(BF16)
        v_scr[0:n_ctx, :] = vc_ref[...].astype(BF16)
        if with_lat:
            k_scr[:, n_ctx:] = _rope(kl_ref[...], kcos_ref[...], ksu_ref[...], ksd_ref[...]).T.astype(BF16)
            v_scr[n_ctx:, :] = vl_ref[...].astype(BF16)

    lam_p = lam_ref[...]
    lam = (jnp.exp(jnp.sum(lam_p[0:1, :] * lam_p[1:2, :], axis=-1, keepdims=True))
           - jnp.exp(jnp.sum(lam_p[2:3, :] * lam_p[3:4, :], axis=-1, keepdims=True)) + lambda_init)
    lo = lax.broadcasted_iota(jnp.int32, (sub_q, LANES), 1) < DIFF_HEAD_DIM
    for t in range(q_ref.shape[0] // sub_q):
        rows = slice(t * sub_q, (t + 1) * sub_q)
        q = q_ref[rows, :]
        if with_lat:
            q = _rope(q, qcos_ref[rows, :], qsu_ref[rows, :], qsd_ref[rows, :])
        q = q * (DIFF_HEAD_DIM ** -0.5 * LOG2E)
        q2 = jnp.concatenate([jnp.where(lo, q, 0.0), jnp.where(lo, 0.0, q)], axis=0).astype(BF16)
        s = jnp.dot(q2, k_scr[...], preferred_element_type=F32)
        e = jnp.exp2(s - jnp.max(s, axis=-1, keepdims=True))
        r = 1.0 / jnp.sum(e, axis=-1, keepdims=True)
        ov = jnp.dot(e.astype(BF16), v_scr[...], preferred_element_type=F32) * r
        o = ov[:sub_q] - lam * ov[sub_q:]
        o = o * lax.rsqrt(jnp.mean(o * o, axis=-1, keepdims=True) + NORM_EPS) * g_ref[...] * (1.0 - lambda_init)
        o_ref[rows, :] = o.astype(o_ref.dtype)


def _diff_attn(p0, q_col0, k_col0, v_col0, n_heads, rope_tabs, lam_p, subln_g, lambda_init,
               n_batch, seq, ctx_len, with_lat, out_rows, into=None, tq=512, sub_q=128):
    cb = lambda c0: c0 // LANES
    ctx_rb0 = (n_batch * seq) // ctx_len
    n_keys = ctx_len + (seq if with_lat else 0)
    if with_lat:
        n_q = seq // tq
        q_spec = pl.BlockSpec((tq, LANES), lambda b, h, i: (b * n_q + i, cb(q_col0) + h))
    else:
        tq = ctx_len
        n_q = 1
        q_spec = pl.BlockSpec((tq, LANES), lambda b, h, i: (ctx_rb0 + b, cb(q_col0) + h))
    kc_spec = pl.BlockSpec((ctx_len, LANES), lambda b, h, i: (ctx_rb0 + b, cb(k_col0) + h))
    vc_spec = pl.BlockSpec((ctx_len, LANES), lambda b, h, i: (ctx_rb0 + b, cb(v_col0) + h))
    small = [pl.BlockSpec((8, LANES), lambda b, h, i: (0, 0)), pl.BlockSpec((1, LANES), lambda b, h, i: (0, 0))]
    small_args = [lam_p, subln_g.reshape(1, LANES)]
    if with_lat:
        kl_spec = pl.BlockSpec((seq, LANES), lambda b, h, i: (b, cb(k_col0) + h))
        vl_spec = pl.BlockSpec((seq, LANES), lambda b, h, i: (b, cb(v_col0) + h))
        qt = pl.BlockSpec((tq, LANES), lambda b, h, i: (i, 0))
        kt = pl.BlockSpec((seq, LANES), lambda b, h, i: (0, 0))
        in_specs = [q_spec, kc_spec, kl_spec, vc_spec, vl_spec, qt, qt, qt, kt, kt, kt] + small
        args = [p0, p0, p0, p0, p0, *rope_tabs, *rope_tabs] + small_args
    else:
        in_specs = [q_spec, kc_spec, vc_spec] + small
        args = [p0, p0, p0] + small_args
    block_bytes = 10 * seq * LANES * 4 if with_lat else 8 * ctx_len * LANES * 4
    aliases = {}
    if with_lat:
        out_spec = pl.BlockSpec((tq, LANES), lambda b, h, i: (b * n_q + i, h))
    else:
        out_spec = pl.BlockSpec((tq, LANES), lambda b, h, i: (ctx_rb0 + b, h))
        in_specs.append(pl.BlockSpec(memory_space=pl.ANY))
        args.append(into)
        aliases = {len(args) - 1: 0}
    return pl.pallas_call(
        functools.partial(_diff_attn_kernel, with_lat=with_lat, lambda_init=lambda_init, n_ctx=ctx_len,
                          sub_q=sub_q),
        grid=(n_batch, n_heads, n_q),
        in_specs=in_specs,
        out_specs=out_spec,
        out_shape=jax.ShapeDtypeStruct((out_rows, n_heads * LANES), BF16),
        scratch_shapes=[pltpu.VMEM((LANES, n_keys), BF16), pltpu.VMEM((n_keys, LANES), BF16)],
        input_output_aliases=aliases,
        compiler_params=_params(("arbitrary", "arbitrary", "arbitrary"), block_bytes),
        name="diff_attn_lat" if with_lat else "diff_attn_ctx",
    )(*args)


def _rope_tables(seq):
    half = DIFF_HEAD_DIM // 2
    pos = jnp.arange(seq)
    row, col = pos // GRID_W, pos % GRID_W
    inv_freq = ROPE_BASE ** (-jnp.arange(0, half, 2, dtype=F32) / half)
    lane = np.arange(LANES) % DIFF_HEAD_DIM
    use_col = jnp.asarray(lane >= half)
    first = jnp.asarray((lane % half) < half // 2)
    freq = inv_freq[jnp.asarray(lane % (half // 2))]
    p = jnp.where(use_col[None, :], col[:, None], row[:, None]).astype(F32)
    ang = p * freq[None, :]
    cos, sin = jnp.cos(ang), jnp.sin(ang)
    return cos, jnp.where(first[None, :], -sin, 0.0), jnp.where(first[None, :], 0.0, sin)


def _rpb_gather_kernel(rpb_ref, o_ref):
    n = o_ref.shape[1]
    k = rpb_ref.shape[1]
    colid = pl.program_id(0) * n + lax.broadcasted_iota(jnp.int32, (k, n), 1)
    j = lax.broadcasted_iota(jnp.int32, (k, n), 0)
    qc = lax.shift_right_logical(colid, int(math.log2(LANES)))
    half = lax.shift_right_logical(colid & (LANES - 1), int(math.log2(GRID_W)))
    kc = colid & (GRID_W - 1)
    sel = (j == half * LANES + jnp.clip(kc - qc + (WIN_COLS - 1), 0, 2 * WIN_COLS - 2)).astype(F32)
    o_ref[...] = jnp.dot(rpb_ref[...], sel, preferred_element_type=F32, precision=lax.Precision.HIGHEST)


def _rpb_pairs(rpb):
    nh, nr, ncol = rpb.shape
    n_slot = nr + 1
    left = jnp.pad(rpb, ((0, 0), (1, 0), (0, LANES - ncol)))
    right = jnp.pad(rpb, ((0, 0), (0, 1), (0, LANES - ncol)))
    rows = jnp.concatenate([left, right], axis=2).reshape(nh * n_slot, 2 * LANES)
    tn = 1024
    out = pl.pallas_call(
        _rpb_gather_kernel,
        grid=(GRID_W * LANES // tn,),
        in_specs=[pl.BlockSpec((nh * n_slot, 2 * LANES), lambda j: (0, 0))],
        out_specs=pl.BlockSpec((nh * n_slot, tn), lambda j: (0, j)),
        out_shape=jax.ShapeDtypeStruct((nh * n_slot, GRID_W * LANES), F32),
        compiler_params=_params(("arbitrary",), nh * n_slot * tn * 4 + 2 * LANES * tn * 4),
        name="rpb_gather",
    )(rows)
    return out.reshape(nh, n_slot, GRID_W, LANES)


def _na_block_plan(rows):
    kr = min(WIN_ROWS, rows)
    n_blk = rows // NA_Q_ROWS
    kb = np.clip(np.arange(n_blk) * NA_Q_ROWS - kr // 2, 0, rows - NA_K_ROWS)
    layouts, layout_of = [], []
    for blk in range(n_blk):
        dr = np.full((NA_Q_ROWS, NA_K_ROWS), -1, np.int64)
        for i in range(NA_Q_ROWS):
            r = blk * NA_Q_ROWS + i
            rs = int(np.clip(r - kr // 2, 0, rows - kr))
            for j in range(NA_K_ROWS):
                krow = kb[blk] + j
                if rs <= krow < rs + kr:
                    dr[i, j] = krow - r + WIN_ROWS - 1
        key = dr.tobytes()
        if key not in [l.tobytes() for l in layouts]:
            layouts.append(dr)
        layout_of.append([l.tobytes() for l in layouts].index(key))
    return kb, np.stack(layouts), np.asarray(layout_of)


def _na_bias_kernel(tp_ref, o_ref, *, layouts):
    qc = lax.broadcasted_iota(jnp.int32, (GRID_W, LANES), 0)
    lane = lax.broadcasted_iota(jnp.int32, (GRID_W, LANES), 1)
    kc = lane & (GRID_W - 1)
    left = lane < GRID_W
    col_start = jnp.clip(qc - WIN_COLS // 2, 0, GRID_W - WIN_COLS)
    col_ok = (kc >= col_start) & (kc < col_start + WIN_COLS)
    n_lay, n_q, n_k = layouts.shape
    for lay in range(n_lay):
        for i in range(n_q):
            for p in range(-(-n_k // 2)):
                d_l = int(layouts[lay, i, 2 * p])
                d_r = int(layouts[lay, i, 2 * p + 1]) if 2 * p + 1 < n_k else -1
                width = LANES if 2 * p + 1 < n_k else GRID_W
                if d_l < 0 and d_r < 0:
                    tile = jnp.full((GRID_W, LANES), -jnp.inf, F32)
                else:
                    assert d_l < 0 or d_r < 0 or d_r == d_l + 1
                    ok = col_ok
                    if d_l < 0:
                        ok = ok & jnp.logical_not(left)
                    if d_r < 0:
                        ok = ok & left
                    slot = d_r if d_r >= 0 else d_l + 1
                    tile = jnp.where(ok, tp_ref[slot] * LOG2E, -jnp.inf)
                o_ref[lay, i * GRID_W:(i + 1) * GRID_W, p * LANES:p * LANES + width] = tile[:, :width]


def _na_bias(rpb_pairs, layouts):
    nh, n_slot = rpb_pairs.shape[:2]
    n_lay = layouts.shape[0]
    tq, nkw = NA_Q_ROWS * GRID_W, NA_K_ROWS * GRID_W
    return pl.pallas_call(
        functools.partial(_na_bias_kernel, layouts=layouts),
        grid=(nh,),
        in_specs=[pl.BlockSpec((None, n_slot, GRID_W, LANES), lambda h: (h, 0, 0, 0))],
        out_specs=pl.BlockSpec((None, n_lay, tq, nkw), lambda h: (h, 0, 0, 0)),
        out_shape=jax.ShapeDtypeStruct((nh, n_lay, tq, nkw), F32),
        compiler_params=_params(("arbitrary",), n_lay * tq * nkw * 4),
        name="na_bias",
    )(rpb_pairs)


def _na_kernel(q_ref, kl_ref, vl_ref, kc_ref, vc_ref, bias_ref, o_ref, *, plan):
    tq = NA_Q_ROWS * GRID_W
    nkw = NA_K_ROWS * GRID_W
    nt = (((1,), (1,)), ((), ()))
    kc = kc_ref[...]
    vc = vc_ref[...]
    for blk, (kb, lay) in enumerate(plan):
        qs = slice(blk * tq, (blk + 1) * tq)
        ks = slice(kb * GRID_W, kb * GRID_W + nkw)
        q = q_ref[qs, :]
        s_w = lax.dot_general(q, kl_ref[ks, :], nt, preferred_element_type=F32) + bias_ref[lay]
        s_c = lax.dot_general(q, kc, nt, preferred_element_type=F32)
        m = jnp.maximum(jnp.max(s_w, axis=-1, keepdims=True), jnp.max(s_c, axis=-1, keepdims=True))
        e_w = jnp.exp2(s_w - m)
        e_c = jnp.exp2(s_c - m)
        l = jnp.sum(e_w, axis=-1, keepdims=True) + jnp.sum(e_c, axis=-1, keepdims=True)
        o = (jnp.dot(e_w.astype(BF16), vl_ref[ks, :], preferred_element_type=F32)
             + jnp.dot(e_c.astype(BF16), vc, preferred_element_type=F32))
        o_ref[qs, :] = (o * (1.0 / l)).astype(o_ref.dtype)


def _na_attn(p1, rpb, n_batch, seq, ctx_len, n_heads):
    rows = seq // GRID_W
    kb, layouts, layout_of = _na_block_plan(rows)
    bias = _na_bias(_rpb_pairs(rpb), layouts)
    n_lay = layouts.shape[0]
    tq = NA_Q_ROWS * GRID_W
    nkw = NA_K_ROWS * GRID_W
    ctx_rb0 = (n_batch * seq) // ctx_len
    plan = tuple((int(k), int(l)) for k, l in zip(kb, layout_of))
    return pl.pallas_call(
        functools.partial(_na_kernel, plan=plan),
        grid=(n_heads, n_batch),
        in_specs=[
            pl.BlockSpec((seq, LANES), lambda h, b: (b, h)),
            pl.BlockSpec((seq, LANES), lambda h, b: (b, n_heads + h)),
            pl.BlockSpec((seq, LANES), lambda h, b: (b, 2 * n_heads + h)),
            pl.BlockSpec((ctx_len, LANES), lambda h, b: (ctx_rb0 + b, n_heads + h)),
            pl.BlockSpec((ctx_len, LANES), lambda h, b: (ctx_rb0 + b, 2 * n_heads + h)),
            pl.BlockSpec((None, n_lay, tq, nkw), lambda h, b: (h, 0, 0, 0)),
        ],
        out_specs=pl.BlockSpec((seq, LANES), lambda h, b: (b, h)),
        out_shape=jax.ShapeDtypeStruct((n_batch * seq, n_heads * LANES), BF16),
        compiler_params=_params(("arbitrary", "arbitrary"), n_lay * tq * nkw * 4 + 4 * seq * LANES * 2),
        name="na_attn",
    )(p1, p1, p1, p1, p1, bias)


def _pack_rows(rows, n_rows=8):
    out = jnp.zeros((n_rows, LANES), F32)
    for r, v in enumerate(rows):
        out = out.at[r, :v.shape[0]].set(v.astype(F32))
    return out


def kernel(x, c, ctx, c_ctx, ada_w, ada_b, norm_mix_g, norm_ffn_g, final_norm_g, ffn_w1, ffn_w3, ffn_w2, ev_w_in, ev_conv_w, ev_conv_b, ev_a_log, ev_dt_bias, ev_d_skip, ev_ssm_norm_g, ev_lam_q1, ev_lam_k1, ev_lam_q2, ev_lam_k2, ev_subln_g, ev_w_out, od_w_in, od_rpb, od_w_out):
    n_batch, seq, d = x.shape
    ctx_len = ctx.shape[1]
    depth = ada_w.shape[0]
    n_lat = n_batch * seq
    n_tok = n_lat + n_batch * ctx_len

    d_ssm = ev_ssm_norm_g.shape[1]
    n_ssm_heads = ev_d_skip.shape[1]
    d_xbc = ev_conv_w.shape[2]
    d_qk = d_v = (ev_w_in.shape[2] - d_ssm - d_xbc - 2 * n_ssm_heads) // 3
    n_diff_heads = d_v // (2 * DIFF_HEAD_DIM)
    n_na_heads = od_rpb.shape[1]

    stream = jnp.concatenate([x.reshape(n_lat, d), ctx.reshape(n_batch * ctx_len, d)], axis=0)
    cond = jnp.zeros((COND_ROWS, d), F32).at[:n_batch].set(c).at[n_batch].set(c_ctx)
    mod = _ada_mod(cond, ada_w, ada_b).reshape(depth, COND_ROWS, 1, 6 * d)
    rope_tabs = _rope_tables(seq)
    kw = dict(seq=seq, n_batch=n_batch)
    ffn_w = [w.astype(BF16) for w in (ffn_w1, ffn_w3, ffn_w2)]

    for i in range(depth):
        ctx_out = i < depth - 1
        m_rows = n_tok if ctx_out else n_lat
        j = i // 2
        if i % 2 == 0:
            lambda_init = 0.8 - 0.6 * math.exp(-0.3 * i)
            w_in = ev_w_in[j]
            dt0 = d_ssm + d_xbc
            w_main = jnp.concatenate([w_in[:, :dt0].astype(BF16), w_in[:, dt0 + 2 * n_ssm_heads:].astype(BF16)], axis=1)
            w_dt = jnp.zeros((d, 2 * LANES), F32)
            w_dt = w_dt.at[:, :n_ssm_heads].set(w_in[:, dt0:dt0 + n_ssm_heads])
            w_dt = w_dt.at[:, LANES:LANES + n_ssm_heads].set(w_in[:, dt0 + n_ssm_heads:dt0 + 2 * n_ssm_heads])
            p0, dt_raw = _proj(stream, norm_mix_g[i], mod, i, w_main, w_dt.astype(BF16), F32, **kw)
            q0 = d_ssm + d_xbc
            k0, v0 = q0 + d_qk, q0 + 2 * d_qk

            xbc = _ssd_conv(p0, d_ssm, d_xbc, ev_conv_w[j], ev_conv_b[j], seq, ctx_len, n_lat)
            prm = jnp.stack([_pack_rows([ev_a_log[j, r], ev_dt_bias[j, r]]) for r in range(2)])
            y2 = _ssd_scan(xbc, dt_raw, prm, n_batch, seq, ctx_len, d_ssm)
            mix_ssd = _ssd_finish(y2, xbc, p0, jnp.repeat(ev_d_skip[j], SSM_HEAD_DIM), ev_ssm_norm_g[j], d_ssm)

            lam_p = _pack_rows([ev_lam_q1[j], ev_lam_k1[j], ev_lam_q2[j], ev_lam_k2[j]])
            attn_args = (p0, q0, k0, v0, n_diff_heads, rope_tabs, lam_p, ev_subln_g[j], lambda_init,
                         n_batch, seq, ctx_len)
            mix_attn = _diff_attn(*attn_args, with_lat=True, out_rows=m_rows)
            if ctx_out:
                mix_attn = _diff_attn(*attn_args, with_lat=False, out_rows=m_rows, into=mix_attn)
            lhs = [mix_ssd, mix_attn]
            w_out = ev_w_out[j].astype(BF16)
        else:
            d_na = n_na_heads * NA_HEAD_DIM
            q_scale = jnp.where(jnp.arange(3 * d_na) < d_na, NA_HEAD_DIM ** -0.5 * LOG2E, 1.0).astype(F32)
            p1 = _proj(stream, norm_mix_g[i], mod, i, od_w_in[j].astype(BF16), None, BF16, col_scale=q_scale, **kw)
            assert not ctx_out, "context-query neighbourhood layers are not needed at this depth"
            lhs = [_na_attn(p1, od_rpb[j], n_batch, seq, ctx_len, n_na_heads)]
            w_out = od_w_out[j].astype(BF16)
        stream = _out_proj(lhs, w_out, stream, mod, i, 2, m_rows, **kw)
        stream = _ffn(stream, norm_ffn_g[i], mod, i, *ffn_w, final_norm_g, not ctx_out, m_rows, **kw)
    return stream.reshape(n_batch, seq, d)
```

```python
import functools
import math

import jax
import jax.numpy as jnp
import numpy as np
from jax import lax
from jax.experimental import pallas as pl
from jax.experimental.pallas import tpu as pltpu

F32 = jnp.float32
BF16 = jnp.bfloat16

GRID_W = 64
SSM_HEAD_DIM = 64
SSM_GROUPS = 4
SSM_STATE = 128
SSM_CONV = 5
SSM_CHUNK = 128
DIFF_HEAD_DIM = 64
NA_HEAD_DIM = 128
WIN_ROWS = 8
WIN_COLS = 16
ROPE_BASE = 10000.0
NORM_EPS = 1e-6
LOG2E = math.log2(math.e)

LANES = 128
V7X_VMEM_BYTES = 64 * 1024 * 1024
VMEM_HEADROOM_BYTES = 6 * 1024 * 1024

NA_Q_ROWS = 4
NA_K_ROWS = NA_Q_ROWS + WIN_ROWS - 1
COND_ROWS = 16


def _vmem_limit(block_bytes):
    return int(min(V7X_VMEM_BYTES - VMEM_HEADROOM_BYTES, max(32 * 1024 * 1024, 2 * block_bytes + 16 * 1024 * 1024)))


def _params(semantics, block_bytes):
    return pltpu.CompilerParams(dimension_semantics=semantics, vmem_limit_bytes=_vmem_limit(block_bytes))


def _silu(v):
    return v * jax.nn.sigmoid(v)


def _mod_row(i, tm, seq, n_batch):
    return jnp.minimum((i * tm) // seq, n_batch)


def _ada_kernel(cond_ref, w_ref, b_ref, o_ref):
    s = _silu(cond_ref[...]).astype(BF16)
    o_ref[...] = jnp.dot(s, w_ref[...].astype(BF16), preferred_element_type=F32) + b_ref[...]


def _ada_mod(cond, ada_w, ada_b, tn=1024):
    depth, d, n = ada_w.shape
    return pl.pallas_call(
        _ada_kernel,
        grid=(depth, n // tn),
        in_specs=[
            pl.BlockSpec((COND_ROWS, d), lambda l, j: (0, 0)),
            pl.BlockSpec((None, d, tn), lambda l, j: (l, 0, j)),
            pl.BlockSpec((None, 1, tn), lambda l, j: (l, 0, j)),
        ],
        out_specs=pl.BlockSpec((None, COND_ROWS, tn), lambda l, j: (l, 0, j)),
        out_shape=jax.ShapeDtypeStruct((depth, COND_ROWS, n), F32),
        compiler_params=_params(("arbitrary", "arbitrary"), d * tn * 4),
        name="ada_mod",
    )(cond, ada_w, ada_b.reshape(depth, 1, n))


def _norm_mod_into(x_ref, g_ref, mod_ref, h_ref, shift_idx, scale_idx, rows=128):
    d = x_ref.shape[-1]
    g = g_ref[...]
    shift = mod_ref[:, shift_idx * d:(shift_idx + 1) * d]
    scale1 = 1.0 + mod_ref[:, scale_idx * d:(scale_idx + 1) * d]

    def body(r, carry):
        sl = pl.ds(pl.multiple_of(r * rows, rows), rows)
        x = x_ref[sl, :]
        xn = x * lax.rsqrt(jnp.mean(x * x, axis=-1, keepdims=True) + NORM_EPS) * g
        h_ref[sl, :] = (xn * scale1 + shift).astype(h_ref.dtype)
        return carry

    lax.fori_loop(0, x_ref.shape[0] // rows, body, 0)


def _proj_kernel(*refs, n_x, n_lat_tiles, has_aux, has_scale):
    rest = list(refs)
    x_refs = [rest.pop(0) for _ in range(n_x)]
    g_ref, mod_ref, w_ref = rest.pop(0), rest.pop(0), rest.pop(0)
    cs_ref = rest.pop(0) if has_scale else None
    if has_aux:
        waux_ref, o_ref, oaux_ref, h_ref = rest
    else:
        o_ref, h_ref = rest

    @pl.when(pl.program_id(1) == 0)
    def _():
        if n_x == 1:
            _norm_mod_into(x_refs[0], g_ref, mod_ref, h_ref, 0, 1)
        else:
            is_lat = pl.program_id(0) < n_lat_tiles
            pl.when(is_lat)(lambda: _norm_mod_into(x_refs[0], g_ref, mod_ref, h_ref, 0, 1))
            pl.when(jnp.logical_not(is_lat))(lambda: _norm_mod_into(x_refs[1], g_ref, mod_ref, h_ref, 0, 1))
        if has_aux:
            oaux_ref[...] = jnp.dot(h_ref[...], waux_ref[...], preferred_element_type=F32)

    acc = jnp.dot(h_ref[...], w_ref[...], preferred_element_type=F32)
    if has_scale:
        acc = acc * cs_ref[...]
    o_ref[...] = acc.astype(o_ref.dtype)


def _proj(x, g, mod, layer, w, w_aux, out_dtype, seq, n_batch, col_scale=None, tm=1024, tn=1024):
    d, n = w.shape
    m = sum(a.shape[0] for a in x) if isinstance(x, tuple) else x.shape[0]
    n_lat_tiles = (n_batch * seq) // tm
    has_aux = w_aux is not None
    has_scale = col_scale is not None
    mod_spec = pl.BlockSpec((None, None, 1, mod.shape[-1]),
                            lambda i, j: (layer, _mod_row(i, tm, seq, n_batch), 0, 0))
    in_specs, args = _row_operand(x, tm, n_lat_tiles, d, lambda j: 0)
    n_x = len(args)
    in_specs += [
        pl.BlockSpec((1, d), lambda i, j: (0, 0)),
        mod_spec,
        pl.BlockSpec((d, tn), lambda i, j: (0, j)),
    ]
    out_specs = [pl.BlockSpec((tm, tn), lambda i, j: (i, j))]
    out_shape = [jax.ShapeDtypeStruct((m, n), out_dtype)]
    args += [g.reshape(1, d), mod, w]
    if has_scale:
        in_specs.append(pl.BlockSpec((1, tn), lambda i, j: (0, j)))
        args.append(col_scale.reshape(1, n))
    if has_aux:
        na = w_aux.shape[1]
        in_specs.append(pl.BlockSpec((d, na), lambda i, j: (0, 0)))
        out_specs.append(pl.BlockSpec((tm, na), lambda i, j: (i, 0)))
        out_shape.append(jax.ShapeDtypeStruct((m, na), F32))
        args.append(w_aux)
    block_bytes = n_x * tm * d * 4 + d * tn * 2 + tm * tn * 4 + tm * d
    outs = pl.pallas_call(
        functools.partial(_proj_kernel, n_x=n_x, n_lat_tiles=n_lat_tiles, has_aux=has_aux, has_scale=has_scale),
        grid=(m // tm, n // tn),
        in_specs=in_specs,
        out_specs=out_specs,
        out_shape=out_shape,
        scratch_shapes=[pltpu.VMEM((tm, d), BF16)],
        compiler_params=_params(("arbitrary", "arbitrary"), block_bytes),
        name="proj",
    )(*args)
    return outs if has_aux else outs[0]


def _row_operand(a, tm, n_lat_tiles, width, col_of_j):
    if not isinstance(a, tuple):
        return [pl.BlockSpec((tm, width), lambda i, j: (i, col_of_j(j)))], [a]
    lat, ctx = a
    assert ctx.shape[0] == tm and lat.shape[0] == n_lat_tiles * tm
    return ([pl.BlockSpec((tm, width), lambda i, j: (jnp.minimum(i, n_lat_tiles - 1), col_of_j(j))),
             pl.BlockSpec((tm, width), lambda i, j: (0, col_of_j(j)))], [lat, ctx])


def _pick_rows(refs, n_lat_tiles):
    if len(refs) == 1:
        return refs[0][...]
    return jnp.where(pl.program_id(0) < n_lat_tiles, refs[0][...], refs[1][...])


def _out_proj_kernel(*refs, arity, n_lat_tiles):
    refs = list(refs)
    n_lhs = len(arity) - 1
    groups = [[refs.pop(0) for _ in range(n)] for n in arity[:-1]]
    ws = [refs.pop(0) for _ in range(n_lhs)]
    res = [refs.pop(0) for _ in range(arity[-1])]
    gate_ref, o_ref = refs
    acc = None
    for grp, w_ref in zip(groups, ws):
        part = jnp.dot(_pick_rows(grp, n_lat_tiles), w_ref[...], preferred_element_type=F32)
        acc = part if acc is None else acc + part
    o_ref[...] = _pick_rows(res, n_lat_tiles) + gate_ref[...] * acc


def _out_proj(lhs_list, w, res, mod, layer, gate_idx, m_rows, seq, n_batch, tm=1024, tn=1024):
    d = w.shape[1]
    n_lat_tiles = (n_batch * seq) // tm
    in_specs, args, arity, widths = [], [], [], []
    for a in lhs_list:
        kk = (a[0] if isinstance(a, tuple) else a).shape[1]
        sp, ar = _row_operand(a, tm, n_lat_tiles, kk, lambda j: 0)
        in_specs += sp
        args += ar
        arity.append(len(ar))
        widths.append(kk)
    row0 = 0
    for kk in widths:
        in_specs.append(pl.BlockSpec((kk, tn), lambda i, j, rb=row0 // kk: (rb, j)))
        args.append(w)
        row0 += kk
    sp, ar = _row_operand(res, tm, n_lat_tiles, tn, lambda j: j)
    in_specs += sp
    args += ar
    arity.append(len(ar))
    in_specs.append(pl.BlockSpec((None, None, 1, tn),
                                 lambda i, j: (layer, _mod_row(i, tm, seq, n_batch), 0, gate_idx * (d // tn) + j)))
    args.append(mod)
    k_total = sum(widths)
    block_bytes = 2 * tm * k_total * 2 + k_total * tn * 2 + 3 * tm * tn * 4
    return pl.pallas_call(
        functools.partial(_out_proj_kernel, arity=tuple(arity), n_lat_tiles=n_lat_tiles),
        grid=(m_rows // tm, d // tn),
        in_specs=in_specs,
        out_specs=pl.BlockSpec((tm, tn), lambda i, j: (i, j)),
        out_shape=jax.ShapeDtypeStruct((m_rows, d), F32),
        compiler_params=_params(("arbitrary", "arbitrary"), block_bytes),
        name="out_proj",
    )(*args)


def _ffn_kernel(x_ref, g_ref, mod_ref, w1_ref, w3_ref, w2_ref, fg_ref, o_ref, h_ref, *, final_norm, rows):
    j = pl.program_id(1)
    d = x_ref.shape[-1]

    @pl.when(j == 0)
    def _():
        _norm_mod_into(x_ref, g_ref, mod_ref, h_ref, 3, 4)
        o_ref[...] = jnp.zeros_like(o_ref)

    h = h_ref[...]
    a = jnp.dot(h, w1_ref[...], preferred_element_type=F32)
    b = jnp.dot(h, w3_ref[...], preferred_element_type=F32)
    u = (_silu(a) * b).astype(BF16)
    o_ref[...] += jnp.dot(u, w2_ref[...], preferred_element_type=F32)

    @pl.when(j == pl.num_programs(1) - 1)
    def _():
        gate = mod_ref[:, 5 * d:6 * d]
        fg = fg_ref[...]

        def body(r, carry):
            sl = pl.ds(pl.multiple_of(r * rows, rows), rows)
            y = x_ref[sl, :] + gate * o_ref[sl, :]
            if final_norm:
                y = y * lax.rsqrt(jnp.mean(y * y, axis=-1, keepdims=True) + NORM_EPS) * fg
            o_ref[sl, :] = y
            return carry

        lax.fori_loop(0, x_ref.shape[0] // rows, body, 0)


def _ffn(x, g, mod, layer, w1, w3, w2, final_g, final_norm, m_rows, seq, n_batch, tm=1024, tf=512):
    d = x.shape[1]
    ff = w1.shape[2]
    block_bytes = 2 * tm * d * 4 + 3 * d * tf * 2 + tm * d
    return pl.pallas_call(
        functools.partial(_ffn_kernel, final_norm=final_norm, rows=128),
        grid=(m_rows // tm, ff // tf),
        in_specs=[
            pl.BlockSpec((tm, d), lambda i, j: (i, 0)),
            pl.BlockSpec((1, d), lambda i, j: (0, 0)),
            pl.BlockSpec((None, None, 1, mod.shape[-1]),
                         lambda i, j: (layer, _mod_row(i, tm, seq, n_batch), 0, 0)),
            pl.BlockSpec((None, d, tf), lambda i, j: (layer, 0, j)),
            pl.BlockSpec((None, d, tf), lambda i, j: (layer, 0, j)),
            pl.BlockSpec((None, tf, d), lambda i, j: (layer, j, 0)),
            pl.BlockSpec((1, d), lambda i, j: (0, 0)),
        ],
        out_specs=pl.BlockSpec((tm, d), lambda i, j: (i, 0)),
        out_shape=jax.ShapeDtypeStruct((m_rows, d), F32),
        scratch_shapes=[pltpu.VMEM((tm, d), BF16)],
        compiler_params=_params(("arbitrary", "arbitrary"), block_bytes),
        name="ffn",
    )(x, g.reshape(1, d), mod, w1, w3, w2, final_g.reshape(1, d))


def _conv_kernel(prev_ref, cur_ref, next_ref, w_ref, b_ref, o_ref, ext_ref, *, tm, halo, seq, ctx_len, n_lat_tiles):
    i = pl.program_id(0)
    ext_ref[0:halo, :] = prev_ref[...]
    ext_ref[halo:halo + tm, :] = cur_ref[...]
    ext_ref[halo + tm:, :] = next_ref[...]
    seg_len = jnp.where(i < n_lat_tiles, seq, ctx_len)
    pos = (i * tm + lax.broadcasted_iota(jnp.int32, (tm, 1), 0)) & (seg_len - 1)
    acc = jnp.zeros(cur_ref.shape, F32) + b_ref[...]
    half = SSM_CONV // 2
    for t in range(SSM_CONV):
        src = pos + (t - half)
        tap = ext_ref[halo - half + t:halo - half + t + tm, :]
        acc = acc + jnp.where((src >= 0) & (src < seg_len), tap, 0.0) * w_ref[t:t + 1, :]
    o_ref[...] = _silu(acc)


def _ssd_conv(p0, col0, width, conv_w, conv_b, seq, ctx_len, n_lat_rows, tm=1024, tc=512, halo=8):
    m = p0.shape[0]
    assert seq % tm == 0 and tm % ctx_len == 0 and (seq & (seq - 1)) == 0 and (ctx_len & (ctx_len - 1)) == 0
    cb0 = col0 // tc
    hb = tm // halo
    last_hb = m // halo - 1
    wpad = jnp.zeros((8, width), F32).at[:SSM_CONV].set(conv_w)
    return pl.pallas_call(
        functools.partial(_conv_kernel, tm=tm, halo=halo, seq=seq, ctx_len=ctx_len, n_lat_tiles=n_lat_rows // tm),
        grid=(m // tm, width // tc),
        in_specs=[
            pl.BlockSpec((halo, tc), lambda i, j: (jnp.maximum(i * hb - 1, 0), cb0 + j)),
            pl.BlockSpec((tm, tc), lambda i, j: (i, cb0 + j)),
            pl.BlockSpec((halo, tc), lambda i, j: (jnp.minimum((i + 1) * hb, last_hb), cb0 + j)),
            pl.BlockSpec((8, tc), lambda i, j: (0, j)),
            pl.BlockSpec((1, tc), lambda i, j: (0, j)),
        ],
        out_specs=pl.BlockSpec((tm, tc), lambda i, j: (i, j)),
        out_shape=jax.ShapeDtypeStruct((m, width), F32),
        scratch_shapes=[pltpu.VMEM((tm + 2 * halo, tc), F32)],
        compiler_params=_params(("arbitrary", "arbitrary"), 3 * tm * tc * 4),
        name="ssd_conv",
    )(p0, p0, p0, wpad, conv_b.reshape(1, width))


def _ssd_scan_kernel(xs_ref, bm_ref, cm_ref, dt_ref, prm_ref, y_ref, state_ref, *, n_heads):
    dirn = pl.program_id(1)
    step = pl.program_id(2)
    hp = SSM_HEAD_DIM
    rep = n_heads // SSM_GROUPS
    lc = SSM_CHUNK
    assert 2 * hp == LANES and rep % 2 == 0 and SSM_STATE == LANES and lc == LANES

    @pl.when(step == 0)
    def _():
        state_ref[...] = jnp.zeros_like(state_ref)

    dt_in = dt_ref[...] + prm_ref[1:2, :]
    dt = jnp.maximum(dt_in, 0.0) + jnp.log1p(jnp.exp(-jnp.abs(dt_in)))
    dta = dt * (-jnp.exp(prm_ref[0:1, :]) * LOG2E)
    row = lax.broadcasted_iota(jnp.int32, (lc, lc), 0)
    col = lax.broadcasted_iota(jnp.int32, (lc, lc), 1)
    causal = (row - col) * (1 - 2 * dirn) >= 0
    a_cum = jnp.dot(causal.astype(F32), dta, preferred_element_type=F32, precision=lax.Precision.HIGHEST)
    a_tot = jnp.where(dirn == 0, a_cum[lc - 1:lc, :], a_cum[0:1, :])
    w_end = dt * jnp.exp2(a_tot - a_cum)
    a_cum_t, dt_t, w_end_t = a_cum.T, dt.T, w_end.T
    lo = col < hp

    for g in range(SSM_GROUPS):
        gs = slice(g * SSM_STATE, (g + 1) * SSM_STATE)
        b_f = bm_ref[:, gs]
        c_g = cm_ref[:, gs].astype(BF16)
        cb = lax.dot_general(c_g, b_f.astype(BF16), (((1,), (1,)), ((), ())), preferred_element_type=F32)
        b_t = b_f.T
        cols_g = slice(g * rep * hp, (g + 1) * rep * hp)
        y_off = jnp.dot(c_g, state_ref[:, cols_g].astype(BF16), preferred_element_type=F32)
        for pr in range(rep // 2):
            cols = slice(g * rep * hp + pr * LANES, g * rep * hp + (pr + 1) * LANES)
            xs2 = xs_ref[:, cols].astype(BF16)
            y_d, s_n, e_a = [], [], []
            for h in (g * rep + 2 * pr, g * rep + 2 * pr + 1):
                a_col = jnp.broadcast_to(a_cum[:, h:h + 1], (lc, lc))
                decay = jnp.exp2(jnp.where(causal, a_col - a_cum_t[h:h + 1, :], -jnp.inf))
                m = (cb * decay * dt_t[h:h + 1, :]).astype(BF16)
                y_d.append(jnp.dot(m, xs2, preferred_element_type=F32))
                s_n.append(jnp.dot((b_t * w_end_t[h:h + 1, :]).astype(BF16), xs2, preferred_element_type=F32))
                e_a.append(jnp.exp2(a_col))
            h0 = g * rep + 2 * pr
            y_ref[:, cols] = (jnp.where(lo, y_d[0], y_d[1])
                              + y_off[:, pr * LANES:(pr + 1) * LANES] * jnp.where(lo, e_a[0], e_a[1]))
            chunk_decay = jnp.where(lo[0:1, :], jnp.exp2(a_tot[:, h0:h0 + 1]), jnp.exp2(a_tot[:, h0 + 1:h0 + 2]))
            state_ref[:, cols] = state_ref[:, cols] * chunk_decay + jnp.where(lo, s_n[0], s_n[1])


def _ssd_scan(xbc, dt_raw, prm, n_batch, seq, ctx_len, d_ssm):
    m = xbc.shape[0]
    lc = SSM_CHUNK
    n_heads = d_ssm // SSM_HEAD_DIM
    nc_ctx, nc_lat = ctx_len // lc, seq // lc
    ctx_blk0 = n_batch * nc_lat
    gn = SSM_GROUPS * SSM_STATE

    def row_blk(b, dirn, s):
        ctx_c = jnp.where(dirn == 0, s, nc_ctx - 1 - s)
        lat_c = jnp.where(dirn == 0, s - nc_ctx, nc_lat - 1 - (s - nc_ctx))
        return jnp.where(s < nc_ctx, ctx_blk0 + b * nc_ctx + ctx_c, b * nc_lat + lat_c)

    return pl.pallas_call(
        functools.partial(_ssd_scan_kernel, n_heads=n_heads),
        grid=(n_batch, 2, nc_ctx + nc_lat),
        in_specs=[
            pl.BlockSpec((lc, d_ssm), lambda b, r, s: (row_blk(b, r, s), 0)),
            pl.BlockSpec((lc, gn), lambda b, r, s: (row_blk(b, r, s), d_ssm // gn)),
            pl.BlockSpec((lc, gn), lambda b, r, s: (row_blk(b, r, s), d_ssm // gn + 1)),
            pl.BlockSpec((lc, LANES), lambda b, r, s: (row_blk(b, r, s), r)),
            pl.BlockSpec((None, 8, LANES), lambda b, r, s: (r, 0, 0)),
        ],
        out_specs=pl.BlockSpec((None, lc, d_ssm), lambda b, r, s: (r, row_blk(b, r, s), 0)),
        out_shape=jax.ShapeDtypeStruct((2, m, d_ssm), F32),
        scratch_shapes=[pltpu.VMEM((SSM_STATE, d_ssm), F32)],
        compiler_params=_params(("arbitrary", "arbitrary", "arbitrary"), 4 * lc * d_ssm * 4),
        name="ssd_scan",
    )(xbc, xbc, xbc, dt_raw, prm)


def _ssd_finish_kernel(y_ref, xs_ref, z_ref, dskip_ref, g_ref, o_ref, *, group_width):
    y = (y_ref[0] + y_ref[1] + dskip_ref[...] * xs_ref[...]) * _silu(z_ref[...])
    for g in range(y.shape[1] // group_width):
        sl = slice(g * group_width, (g + 1) * group_width)
        v = y[:, sl]
        vn = v * lax.rsqrt(jnp.mean(v * v, axis=-1, keepdims=True) + NORM_EPS) * g_ref[:, sl]
        o_ref[:, sl] = vn.astype(o_ref.dtype)


def _ssd_finish(y2, xbc, p0, d_skip_lanes, norm_g, d_ssm, tm=512):
    m = xbc.shape[0]
    return pl.pallas_call(
        functools.partial(_ssd_finish_kernel, group_width=d_ssm // SSM_GROUPS),
        grid=(m // tm,),
        in_specs=[
            pl.BlockSpec((2, tm, d_ssm), lambda i: (0, i, 0)),
            pl.BlockSpec((tm, d_ssm), lambda i: (i, 0)),
            pl.BlockSpec((tm, d_ssm), lambda i: (i, 0)),
            pl.BlockSpec((1, d_ssm), lambda i: (0, 0)),
            pl.BlockSpec((1, d_ssm), lambda i: (0, 0)),
        ],
        out_specs=pl.BlockSpec((tm, d_ssm), lambda i: (i, 0)),
        out_shape=jax.ShapeDtypeStruct((m, d_ssm), BF16),
        compiler_params=_params(("arbitrary",), 5 * tm * d_ssm * 4),
        name="ssd_finish",
    )(y2, xbc, p0, d_skip_lanes.reshape(1, d_ssm), norm_g.reshape(1, d_ssm))


def _rope(x, cos, sin_up, sin_dn):
    quarter = DIFF_HEAD_DIM // 4
    return (x * cos + pltpu.roll(x, LANES - quarter, axis=1) * sin_up + pltpu.roll(x, quarter, axis=1) * sin_dn)


def _diff_attn_kernel(*refs, with_lat, lambda_init, n_ctx, sub_q):
    if with_lat:
        (q_ref, kc_ref, kl_ref, vc_ref, vl_ref, qcos_ref, qsu_ref, qsd_ref, kcos_ref, ksu_ref, ksd_ref,
         lam_ref, g_ref, o_ref, k_scr, v_scr) = refs
    else:
        q_ref, kc_ref, vc_ref, lam_ref, g_ref, o_ref, k_scr, v_scr = refs

    @pl.when(pl.program_id(2) == 0)
    def _():
        k_scr[:, 0:n_ctx] = kc_ref[...].T.astype(BF16)
        v_scr[0:n_ctx, :] = vc_ref[...].astype(BF16)
        if with_lat:
            k_scr[:, n_ctx:] = _rope(kl_ref[...], kcos_ref[...], ksu_ref[...], ksd_ref[...]).T.astype(BF16)
            v_scr[n_ctx:, :] = vl_ref[...].astype(BF16)

    lam_p = lam_ref[...]
    lam = (jnp.exp(jnp.sum(lam_p[0:1, :] * lam_p[1:2, :], axis=-1, keepdims=True))
           - jnp.exp(jnp.sum(lam_p[2:3, :] * lam_p[3:4, :], axis=-1, keepdims=True)) + lambda_init)
    lo = lax.broadcasted_iota(jnp.int32, (sub_q, LANES), 1) < DIFF_HEAD_DIM
    for t in range(q_ref.shape[0] // sub_q):
        rows = slice(t * sub_q, (t + 1) * sub_q)
        q = q_ref[rows, :]
        if with_lat:
            q = _rope(q, qcos_ref[rows, :], qsu_ref[rows, :], qsd_ref[rows, :])
        q = q * (DIFF_HEAD_DIM ** -0.5 * LOG2E)
        q2 = jnp.concatenate([jnp.where(lo, q, 0.0), jnp.where(lo, 0.0, q)], axis=0).astype(BF16)
        s = jnp.dot(q2, k_scr[...], preferred_element_type=F32)
        e = jnp.exp2(s - jnp.max(s, axis=-1, keepdims=True))
        r = 1.0 / jnp.sum(e, axis=-1, keepdims=True)
        ov = jnp.dot(e.astype(BF16), v_scr[...], preferred_element_type=F32) * r
        o = ov[:sub_q] - lam * ov[sub_q:]
        o = o * lax.rsqrt(jnp.mean(o * o, axis=-1, keepdims=True) + NORM_EPS) * g_ref[...] * (1.0 - lambda_init)
        o_ref[rows, :] = o.astype(o_ref.dtype)


def _diff_attn(p0, q_col0, k_col0, v_col0, n_heads, rope_tabs, lam_p, subln_g, lambda_init,
               n_batch, seq, ctx_len, with_lat, tq=512, sub_q=128):
    cb = lambda c0: c0 // LANES
    ctx_rb0 = (n_batch * seq) // ctx_len
    n_keys = ctx_len + (seq if with_lat else 0)
    if with_lat:
        n_q = seq // tq
        q_spec = pl.BlockSpec((tq, LANES), lambda b, h, i: (b * n_q + i, cb(q_col0) + h))
    else:
        tq = ctx_len
        n_q = 1
        q_spec = pl.BlockSpec((tq, LANES), lambda b, h, i: (ctx_rb0 + b, cb(q_col0) + h))
    kc_spec = pl.BlockSpec((ctx_len, LANES), lambda b, h, i: (ctx_rb0 + b, cb(k_col0) + h))
    vc_spec = pl.BlockSpec((ctx_len, LANES), lambda b, h, i: (ctx_rb0 + b, cb(v_col0) + h))
    small = [pl.BlockSpec((8, LANES), lambda b, h, i: (0, 0)), pl.BlockSpec((1, LANES), lambda b, h, i: (0, 0))]
    small_args = [lam_p, subln_g.reshape(1, LANES)]
    if with_lat:
        kl_spec = pl.BlockSpec((seq, LANES), lambda b, h, i: (b, cb(k_col0) + h))
        vl_spec = pl.BlockSpec((seq, LANES), lambda b, h, i: (b, cb(v_col0) + h))
        qt = pl.BlockSpec((tq, LANES), lambda b, h, i: (i, 0))
        kt = pl.BlockSpec((seq, LANES), lambda b, h, i: (0, 0))
        in_specs = [q_spec, kc_spec, kl_spec, vc_spec, vl_spec, qt, qt, qt, kt, kt, kt] + small
        args = [p0, p0, p0, p0, p0, *rope_tabs, *rope_tabs] + small_args
    else:
        in_specs = [q_spec, kc_spec, vc_spec] + small
        args = [p0, p0, p0] + small_args
    block_bytes = 10 * seq * LANES * 4 if with_lat else 8 * ctx_len * LANES * 4
    return pl.pallas_call(
        functools.partial(_diff_attn_kernel, with_lat=with_lat, lambda_init=lambda_init, n_ctx=ctx_len,
                          sub_q=sub_q),
        grid=(n_batch, n_heads, n_q),
        in_specs=in_specs,
        out_specs=pl.BlockSpec((tq, LANES), lambda b, h, i: (b * n_q + i, h)),
        out_shape=jax.ShapeDtypeStruct((n_batch * n_q * tq, n_heads * LANES), BF16),
        scratch_shapes=[pltpu.VMEM((LANES, n_keys), BF16), pltpu.VMEM((n_keys, LANES), BF16)],
        compiler_params=_params(("arbitrary", "arbitrary", "arbitrary"), block_bytes),
        name="diff_attn_lat" if with_lat else "diff_attn_ctx",
    )(*args)


def _rope_tables(seq):
    half = DIFF_HEAD_DIM // 2
    pos = jnp.arange(seq)
    row, col = pos // GRID_W, pos % GRID_W
    inv_freq = ROPE_BASE ** (-jnp.arange(0, half, 2, dtype=F32) / half)
    lane = np.arange(LANES) % DIFF_HEAD_DIM
    use_col = jnp.asarray(lane >= half)
    first = jnp.asarray((lane % half) < half // 2)
    freq = inv_freq[jnp.asarray(lane % (half // 2))]
    p = jnp.where(use_col[None, :], col[:, None], row[:, None]).astype(F32)
    ang = p * freq[None, :]
    cos, sin = jnp.cos(ang), jnp.sin(ang)
    return cos, jnp.where(first[None, :], -sin, 0.0), jnp.where(first[None, :], 0.0, sin)


def _rpb_gather_kernel(rpb_ref, o_ref):
    n = o_ref.shape[1]
    k = rpb_ref.shape[1]
    colid = pl.program_id(0) * n + lax.broadcasted_iota(jnp.int32, (k, n), 1)
    j = lax.broadcasted_iota(jnp.int32, (k, n), 0)
    qc = lax.shift_right_logical(colid, int(math.log2(LANES)))
    half = lax.shift_right_logical(colid & (LANES - 1), int(math.log2(GRID_W)))
    kc = colid & (GRID_W - 1)
    sel = (j == half * LANES + jnp.clip(kc - qc + (WIN_COLS - 1), 0, 2 * WIN_COLS - 2)).astype(F32)
    o_ref[...] = jnp.dot(rpb_ref[...], sel, preferred_element_type=F32, precision=lax.Precision.HIGHEST)


def _rpb_pairs(rpb):
    nh, nr, ncol = rpb.shape
    n_slot = nr + 1
    left = jnp.pad(rpb, ((0, 0), (1, 0), (0, LANES - ncol)))
    right = jnp.pad(rpb, ((0, 0), (0, 1), (0, LANES - ncol)))
    rows = jnp.concatenate([left, right], axis=2).reshape(nh * n_slot, 2 * LANES)
    tn = 1024
    out = pl.pallas_call(
        _rpb_gather_kernel,
        grid=(GRID_W * LANES // tn,),
        in_specs=[pl.BlockSpec((nh * n_slot, 2 * LANES), lambda j: (0, 0))],
        out_specs=pl.BlockSpec((nh * n_slot, tn), lambda j: (0, j)),
        out_shape=jax.ShapeDtypeStruct((nh * n_slot, GRID_W * LANES), F32),
        compiler_params=_params(("arbitrary",), nh * n_slot * tn * 4 + 2 * LANES * tn * 4),
        name="rpb_gather",
    )(rows)
    return out.reshape(nh, n_slot, GRID_W, LANES)


def _na_block_plan(rows):
    kr = min(WIN_ROWS, rows)
    n_blk = rows // NA_Q_ROWS
    kb = np.clip(np.arange(n_blk) * NA_Q_ROWS - kr // 2, 0, rows - NA_K_ROWS)
    layouts, layout_of = [], []
    for blk in range(n_blk):
        dr = np.full((NA_Q_ROWS, NA_K_ROWS), -1, np.int64)
        for i in range(NA_Q_ROWS):
            r = blk * NA_Q_ROWS + i
            rs = int(np.clip(r - kr // 2, 0, rows - kr))
            for j in range(NA_K_ROWS):
                krow = kb[blk] + j
                if rs <= krow < rs + kr:
                    dr[i, j] = krow - r + WIN_ROWS - 1
        key = dr.tobytes()
        if key not in [l.tobytes() for l in layouts]:
            layouts.append(dr)
        layout_of.append([l.tobytes() for l in layouts].index(key))
    return kb, np.stack(layouts), np.asarray(layout_of)


def _na_bias_kernel(tp_ref, o_ref, *, layouts):
    qc = lax.broadcasted_iota(jnp.int32, (GRID_W, LANES), 0)
    lane = lax.broadcasted_iota(jnp.int32, (GRID_W, LANES), 1)
    kc = lane & (GRID_W - 1)
    left = lane < GRID_W
    col_start = jnp.clip(qc - WIN_COLS // 2, 0, GRID_W - WIN_COLS)
    col_ok = (kc >= col_start) & (kc < col_start + WIN_COLS)
    n_lay, n_q, n_k = layouts.shape
    for lay in range(n_lay):
        for i in range(n_q):
            for p in range(-(-n_k // 2)):
                d_l = int(layouts[lay, i, 2 * p])
                d_r = int(layouts[lay, i, 2 * p + 1]) if 2 * p + 1 < n_k else -1
                width = LANES if 2 * p + 1 < n_k else GRID_W
                if d_l < 0 and d_r < 0:
                    tile = jnp.full((GRID_W, LANES), -jnp.inf, F32)
                else:
                    assert d_l < 0 or d_r < 0 or d_r == d_l + 1
                    ok = col_ok
                    if d_l < 0:
                        ok = ok & jnp.logical_not(left)
                    if d_r < 0:
                        ok = ok & left
                    slot = d_r if d_r >= 0 else d_l + 1
                    tile = jnp.where(ok, tp_ref[slot] * LOG2E, -jnp.inf)
                o_ref[lay, i * GRID_W:(i + 1) * GRID_W, p * LANES:p * LANES + width] = tile[:, :width]


def _na_bias(rpb_pairs, layouts):
    nh, n_slot = rpb_pairs.shape[:2]
    n_lay = layouts.shape[0]
    tq, nkw = NA_Q_ROWS * GRID_W, NA_K_ROWS * GRID_W
    return pl.pallas_call(
        functools.partial(_na_bias_kernel, layouts=layouts),
        grid=(nh,),
        in_specs=[pl.BlockSpec((None, n_slot, GRID_W, LANES), lambda h: (h, 0, 0, 0))],
        out_specs=pl.BlockSpec((None, n_lay, tq, nkw), lambda h: (h, 0, 0, 0)),
        out_shape=jax.ShapeDtypeStruct((nh, n_lay, tq, nkw), F32),
        compiler_params=_params(("arbitrary",), n_lay * tq * nkw * 4),
        name="na_bias",
    )(rpb_pairs)


def _na_kernel(q_ref, kl_ref, vl_ref, kc_ref, vc_ref, bias_ref, o_ref, *, plan):
    tq = NA_Q_ROWS * GRID_W
    nkw = NA_K_ROWS * GRID_W
    nt = (((1,), (1,)), ((), ()))
    kc = kc_ref[...]
    vc = vc_ref[...]
    for blk, (kb, lay) in enumerate(plan):
        qs = slice(blk * tq, (blk + 1) * tq)
        ks = slice(kb * GRID_W, kb * GRID_W + nkw)
        q = q_ref[qs, :]
        s_w = lax.dot_general(q, kl_ref[ks, :], nt, preferred_element_type=F32) + bias_ref[lay]
        s_c = lax.dot_general(q, kc, nt, preferred_element_type=F32)
        m = jnp.maximum(jnp.max(s_w, axis=-1, keepdims=True), jnp.max(s_c, axis=-1, keepdims=True))
        e_w = jnp.exp2(s_w - m)
        e_c = jnp.exp2(s_c - m)
        l = jnp.sum(e_w, axis=-1, keepdims=True) + jnp.sum(e_c, axis=-1, keepdims=True)
        o = (jnp.dot(e_w.astype(BF16), vl_ref[ks, :], preferred_element_type=F32)
             + jnp.dot(e_c.astype(BF16), vc, preferred_element_type=F32))
        o_ref[qs, :] = (o * (1.0 / l)).astype(o_ref.dtype)


def _na_attn(p1, rpb, n_batch, seq, ctx_len, n_heads):
    rows = seq // GRID_W
    kb, layouts, layout_of = _na_block_plan(rows)
    bias = _na_bias(_rpb_pairs(rpb), layouts)
    n_lay = layouts.shape[0]
    tq = NA_Q_ROWS * GRID_W
    nkw = NA_K_ROWS * GRID_W
    ctx_rb0 = (n_batch * seq) // ctx_len
    plan = tuple((int(k), int(l)) for k, l in zip(kb, layout_of))
    return pl.pallas_call(
        functools.partial(_na_kernel, plan=plan),
        grid=(n_heads, n_batch),
        in_specs=[
            pl.BlockSpec((seq, LANES), lambda h, b: (b, h)),
            pl.BlockSpec((seq, LANES), lambda h, b: (b, n_heads + h)),
            pl.BlockSpec((seq, LANES), lambda h, b: (b, 2 * n_heads + h)),
            pl.BlockSpec((ctx_len, LANES), lambda h, b: (ctx_rb0 + b, n_heads + h)),
            pl.BlockSpec((ctx_len, LANES), lambda h, b: (ctx_rb0 + b, 2 * n_heads + h)),
            pl.BlockSpec((None, n_lay, tq, nkw), lambda h, b: (h, 0, 0, 0)),
        ],
        out_specs=pl.BlockSpec((seq, LANES), lambda h, b: (b, h)),
        out_shape=jax.ShapeDtypeStruct((n_batch * seq, n_heads * LANES), BF16),
        compiler_params=_params(("arbitrary", "arbitrary"), n_lay * tq * nkw * 4 + 4 * seq * LANES * 2),
        name="na_attn",
    )(p1, p1, p1, p1, p1, bias)


def _pack_rows(rows, n_rows=8):
    out = jnp.zeros((n_rows, LANES), F32)
    for r, v in enumerate(rows):
        out = out.at[r, :v.shape[0]].set(v.astype(F32))
    return out


def kernel(x, c, ctx, c_ctx, ada_w, ada_b, norm_mix_g, norm_ffn_g, final_norm_g, ffn_w1, ffn_w3, ffn_w2, ev_w_in, ev_conv_w, ev_conv_b, ev_a_log, ev_dt_bias, ev_d_skip, ev_ssm_norm_g, ev_lam_q1, ev_lam_k1, ev_lam_q2, ev_lam_k2, ev_subln_g, ev_w_out, od_w_in, od_rpb, od_w_out):
    n_batch, seq, d = x.shape
    ctx_len = ctx.shape[1]
    depth = ada_w.shape[0]
    n_lat = n_batch * seq
    n_tok = n_lat + n_batch * ctx_len

    d_ssm = ev_ssm_norm_g.shape[1]
    n_ssm_heads = ev_d_skip.shape[1]
    d_xbc = ev_conv_w.shape[2]
    d_qk = d_v = (ev_w_in.shape[2] - d_ssm - d_xbc - 2 * n_ssm_heads) // 3
    n_diff_heads = d_v // (2 * DIFF_HEAD_DIM)
    n_na_heads = od_rpb.shape[1]

    stream = (x.reshape(n_lat, d), ctx.reshape(n_batch * ctx_len, d))
    cond = jnp.zeros((COND_ROWS, d), F32).at[:n_batch].set(c).at[n_batch].set(c_ctx)
    mod = _ada_mod(cond, ada_w, ada_b).reshape(depth, COND_ROWS, 1, 6 * d)
    rope_tabs = _rope_tables(seq)
    kw = dict(seq=seq, n_batch=n_batch)
    ffn_w = [w.astype(BF16) for w in (ffn_w1, ffn_w3, ffn_w2)]

    for i in range(depth):
        ctx_out = i < depth - 1
        m_rows = n_tok if ctx_out else n_lat
        j = i // 2
        if i % 2 == 0:
            lambda_init = 0.8 - 0.6 * math.exp(-0.3 * i)
            w_in = ev_w_in[j]
            dt0 = d_ssm + d_xbc
            w_main = jnp.concatenate([w_in[:, :dt0].astype(BF16), w_in[:, dt0 + 2 * n_ssm_heads:].astype(BF16)], axis=1)
            w_dt = jnp.zeros((d, 2 * LANES), F32)
            w_dt = w_dt.at[:, :n_ssm_heads].set(w_in[:, dt0:dt0 + n_ssm_heads])
            w_dt = w_dt.at[:, LANES:LANES + n_ssm_heads].set(w_in[:, dt0 + n_ssm_heads:dt0 + 2 * n_ssm_heads])
            p0, dt_raw = _proj(stream, norm_mix_g[i], mod, i, w_main, w_dt.astype(BF16), F32, **kw)
            q0 = d_ssm + d_xbc
            k0, v0 = q0 + d_qk, q0 + 2 * d_qk

            xbc = _ssd_conv(p0, d_ssm, d_xbc, ev_conv_w[j], ev_conv_b[j], seq, ctx_len, n_lat)
            prm = jnp.stack([_pack_rows([ev_a_log[j, r], ev_dt_bias[j, r]]) for r in range(2)])
            y2 = _ssd_scan(xbc, dt_raw, prm, n_batch, seq, ctx_len, d_ssm)
            mix_ssd = _ssd_finish(y2, xbc, p0, jnp.repeat(ev_d_skip[j], SSM_HEAD_DIM), ev_ssm_norm_g[j], d_ssm)

            lam_p = _pack_rows([ev_lam_q1[j], ev_lam_k1[j], ev_lam_q2[j], ev_lam_k2[j]])
            attn_args = (p0, q0, k0, v0, n_diff_heads, rope_tabs, lam_p, ev_subln_g[j], lambda_init,
                         n_batch, seq, ctx_len)
            mix_attn = _diff_attn(*attn_args, with_lat=True)
            if ctx_out:
                mix_attn = (mix_attn, _diff_attn(*attn_args, with_lat=False))
            lhs = [mix_ssd, mix_attn]
            w_out = ev_w_out[j].astype(BF16)
        else:
            d_na = n_na_heads * NA_HEAD_DIM
            q_scale = jnp.where(jnp.arange(3 * d_na) < d_na, NA_HEAD_DIM ** -0.5 * LOG2E, 1.0).astype(F32)
            p1 = _proj(stream, norm_mix_g[i], mod, i, od_w_in[j].astype(BF16), None, BF16, col_scale=q_scale, **kw)
            assert not ctx_out, "context-query neighbourhood layers are not needed at this depth"
            lhs = [_na_attn(p1, od_rpb[j], n_batch, seq, ctx_len, n_na_heads)]
            w_out = od_w_out[j].astype(BF16)
        stream = _out_proj(lhs, w_out, stream, mod, i, 2, m_rows, **kw)
        stream = _ffn(stream, norm_ffn_g[i], mod, i, *ffn_w, final_norm_g, not ctx_out, m_rows, **kw)
    return stream.reshape(n_batch, seq, d)
```

```python
import functools
import math

import jax
import jax.numpy as jnp
import numpy as np
from jax import lax
from jax.experimental import pallas as pl
from jax.experimental.pallas import tpu as pltpu

F32 = jnp.float32
BF16 = jnp.bfloat16

GRID_W = 64
SSM_HEAD_DIM = 64
SSM_GROUPS = 4
SSM_STATE = 128
SSM_CONV = 5
SSM_CHUNK = 128
DIFF_HEAD_DIM = 64
NA_HEAD_DIM = 128
WIN_ROWS = 8
WIN_COLS = 16
ROPE_BASE = 10000.0
NORM_EPS = 1e-6
LOG2E = math.log2(math.e)

LANES = 128
V7X_VMEM_BYTES = 64 * 1024 * 1024
VMEM_HEADROOM_BYTES = 6 * 1024 * 1024

NA_Q_ROWS = 4
NA_K_ROWS = NA_Q_ROWS + WIN_ROWS - 1
COND_ROWS = 16


def _vmem_limit(block_bytes):
    return int(min(V7X_VMEM_BYTES - VMEM_HEADROOM_BYTES, max(32 * 1024 * 1024, 2 * block_bytes + 16 * 1024 * 1024)))


def _params(semantics, block_bytes):
    return pltpu.CompilerParams(dimension_semantics=semantics, vmem_limit_bytes=_vmem_limit(block_bytes))


def _silu(v):
    return v * jax.nn.sigmoid(v)


def _mod_row(i, tm, seq, n_batch):
    return jnp.minimum((i * tm) // seq, n_batch)


def _ada_kernel(cond_ref, w_ref, b_ref, o_ref):
    s = _silu(cond_ref[...]).astype(BF16)
    o_ref[...] = jnp.dot(s, w_ref[...].astype(BF16), preferred_element_type=F32) + b_ref[...]


def _ada_mod(cond, ada_w, ada_b, tn=1024):
    depth, d, n = ada_w.shape
    return pl.pallas_call(
        _ada_kernel,
        grid=(depth, n // tn),
        in_specs=[
            pl.BlockSpec((COND_ROWS, d), lambda l, j: (0, 0)),
            pl.BlockSpec((None, d, tn), lambda l, j: (l, 0, j)),
            pl.BlockSpec((None, 1, tn), lambda l, j: (l, 0, j)),
        ],
        out_specs=pl.BlockSpec((None, COND_ROWS, tn), lambda l, j: (l, 0, j)),
        out_shape=jax.ShapeDtypeStruct((depth, COND_ROWS, n), F32),
        compiler_params=_params(("arbitrary", "arbitrary"), d * tn * 4),
        name="ada_mod",
    )(cond, ada_w, ada_b.reshape(depth, 1, n))


def _norm_mod_into(x_ref, g_ref, mod_ref, h_ref, shift_idx, scale_idx, rows=128):
    d = x_ref.shape[-1]
    g = g_ref[...]
    shift = mod_ref[:, shift_idx * d:(shift_idx + 1) * d]
    scale1 = 1.0 + mod_ref[:, scale_idx * d:(scale_idx + 1) * d]

    def body(r, carry):
        sl = pl.ds(pl.multiple_of(r * rows, rows), rows)
        x = x_ref[sl, :]
        xn = x * lax.rsqrt(jnp.mean(x * x, axis=-1, keepdims=True) + NORM_EPS) * g
        h_ref[sl, :] = (xn * scale1 + shift).astype(h_ref.dtype)
        return carry

    lax.fori_loop(0, x_ref.shape[0] // rows, body, 0)


def _proj_kernel(*refs, n_x, n_lat_tiles, has_aux, has_scale):
    rest = list(refs)
    x_refs = [rest.pop(0) for _ in range(n_x)]
    g_ref, mod_ref, w_ref = rest.pop(0), rest.pop(0), rest.pop(0)
    cs_ref = rest.pop(0) if has_scale else None
    if has_aux:
        waux_ref, o_ref, oaux_ref, h_ref = rest
    else:
        o_ref, h_ref = rest

    @pl.when(pl.program_id(1) == 0)
    def _():
        if n_x == 1:
            _norm_mod_into(x_refs[0], g_ref, mod_ref, h_ref, 0, 1)
        else:
            is_lat = pl.program_id(0) < n_lat_tiles
            pl.when(is_lat)(lambda: _norm_mod_into(x_refs[0], g_ref, mod_ref, h_ref, 0, 1))
            pl.when(jnp.logical_not(is_lat))(lambda: _norm_mod_into(x_refs[1], g_ref, mod_ref, h_ref, 0, 1))
        if has_aux:
            oaux_ref[...] = jnp.dot(h_ref[...], waux_ref[...], preferred_element_type=F32)

    acc = jnp.dot(h_ref[...], w_ref[...], preferred_element_type=F32)
    if has_scale:
        acc = acc * cs_ref[...]
    o_ref[...] = acc.astype(o_ref.dtype)


def _proj(x, g, mod, layer, w, w_aux, out_dtype, seq, n_batch, col_scale=None, tm=1024, tn=1024):
    d, n = w.shape
    m = sum(a.shape[0] for a in x) if isinstance(x, tuple) else x.shape[0]
    n_lat_tiles = (n_batch * seq) // tm
    has_aux = w_aux is not None
    has_scale = col_scale is not None
    mod_spec = pl.BlockSpec((None, None, 1, mod.shape[-1]),
                            lambda i, j: (layer, _mod_row(i, tm, seq, n_batch), 0, 0))
    in_specs, args = _row_operand(x, tm, n_lat_tiles, d, lambda j: 0)
    n_x = len(args)
    in_specs += [
        pl.BlockSpec((1, d), lambda i, j: (0, 0)),
        mod_spec,
        pl.BlockSpec((d, tn), lambda i, j: (0, j)),
    ]
    out_specs = [pl.BlockSpec((tm, tn), lambda i, j: (i, j))]
    out_shape = [jax.ShapeDtypeStruct((m, n), out_dtype)]
    args += [g.reshape(1, d), mod, w]
    if has_scale:
        in_specs.append(pl.BlockSpec((1, tn), lambda i, j: (0, j)))
        args.append(col_scale.reshape(1, n))
    if has_aux:
        na = w_aux.shape[1]
        in_specs.append(pl.BlockSpec((d, na), lambda i, j: (0, 0)))
        out_specs.append(pl.BlockSpec((tm, na), lambda i, j: (i, 0)))
        out_shape.append(jax.ShapeDtypeStruct((m, na), F32))
        args.append(w_aux)
    block_bytes = n_x * tm * d * 4 + d * tn * 2 + tm * tn * 4 + tm * d
    outs = pl.pallas_call(
        functools.partial(_proj_kernel, n_x=n_x, n_lat_tiles=n_lat_tiles, has_aux=has_aux, has_scale=has_scale),
        grid=(m // tm, n // tn),
        in_specs=in_specs,
        out_specs=out_specs,
        out_shape=out_shape,
        scratch_shapes=[pltpu.VMEM((tm, d), BF16)],
        compiler_params=_params(("arbitrary", "arbitrary"), block_bytes),
        name="proj",
    )(*args)
    return outs if has_aux else outs[0]


def _row_operand(a, tm, n_lat_tiles, width, col_of_j):
    if not isinstance(a, tuple):
        return [pl.BlockSpec((tm, width), lambda i, j: (i, col_of_j(j)))], [a]
    lat, ctx = a
    assert ctx.shape[0] == tm and lat.shape[0] == n_lat_tiles * tm
    return ([pl.BlockSpec((tm, width), lambda i, j: (jnp.minimum(i, n_lat_tiles - 1), col_of_j(j))),
             pl.BlockSpec((tm, width), lambda i, j: (0, col_of_j(j)))], [lat, ctx])


def _pick_rows(refs, n_lat_tiles):
    if len(refs) == 1:
        return refs[0][...]
    return jnp.where(pl.program_id(0) < n_lat_tiles, refs[0][...], refs[1][...])


def _out_proj_kernel(*refs, arity, n_lat_tiles):
    refs = list(refs)
    n_lhs = len(arity) - 1
    groups = [[refs.pop(0) for _ in range(n)] for n in arity[:-1]]
    ws = [refs.pop(0) for _ in range(n_lhs)]
    res = [refs.pop(0) for _ in range(arity[-1])]
    gate_ref, o_ref = refs
    acc = None
    for grp, w_ref in zip(groups, ws):
        part = jnp.dot(_pick_rows(grp, n_lat_tiles), w_ref[...], preferred_element_type=F32)
        acc = part if acc is None else acc + part
    o_ref[...] = _pick_rows(res, n_lat_tiles) + gate_ref[...] * acc


def _out_proj(lhs_list, w, res, mod, layer, gate_idx, m_rows, seq, n_batch, tm=1024, tn=1024):
    d = w.shape[1]
    n_lat_tiles = (n_batch * seq) // tm
    in_specs, args, arity, widths = [], [], [], []
    for a in lhs_list:
        kk = (a[0] if isinstance(a, tuple) else a).shape[1]
        sp, ar = _row_operand(a, tm, n_lat_tiles, kk, lambda j: 0)
        in_specs += sp
        args += ar
        arity.append(len(ar))
        widths.append(kk)
    row0 = 0
    for kk in widths:
        in_specs.append(pl.BlockSpec((kk, tn), lambda i, j, rb=row0 // kk: (rb, j)))
        args.append(w)
        row0 += kk
    sp, ar = _row_operand(res, tm, n_lat_tiles, tn, lambda j: j)
    in_specs += sp
    args += ar
    arity.append(len(ar))
    in_specs.append(pl.BlockSpec((None, None, 1, tn),
                                 lambda i, j: (layer, _mod_row(i, tm, seq, n_batch), 0, gate_idx * (d // tn) + j)))
    args.append(mod)
    k_total = sum(widths)
    block_bytes = 2 * tm * k_total * 2 + k_total * tn * 2 + 3 * tm * tn * 4
    return pl.pallas_call(
        functools.partial(_out_proj_kernel, arity=tuple(arity), n_lat_tiles=n_lat_tiles),
        grid=(m_rows // tm, d // tn),
        in_specs=in_specs,
        out_specs=pl.BlockSpec((tm, tn), lambda i, j: (i, j)),
        out_shape=jax.ShapeDtypeStruct((m_rows, d), F32),
        compiler_params=_params(("arbitrary", "arbitrary"), block_bytes),
        name="out_proj",
    )(*args)


def _ffn_kernel(x_ref, g_ref, mod_ref, w1_ref, w3_ref, w2_ref, fg_ref, o_ref, h_ref, *, final_norm, rows):
    j = pl.program_id(1)
    d = x_ref.shape[-1]

    @pl.when(j == 0)
    def _():
        _norm_mod_into(x_ref, g_ref, mod_ref, h_ref, 3, 4)
        o_ref[...] = jnp.zeros_like(o_ref)

    h = h_ref[...]
    a = jnp.dot(h, w1_ref[...].astype(BF16), preferred_element_type=F32)
    b = jnp.dot(h, w3_ref[...].astype(BF16), preferred_element_type=F32)
    u = (_silu(a) * b).astype(BF16)
    o_ref[...] += jnp.dot(u, w2_ref[...].astype(BF16), preferred_element_type=F32)

    @pl.when(j == pl.num_programs(1) - 1)
    def _():
        gate = mod_ref[:, 5 * d:6 * d]
        fg = fg_ref[...]

        def body(r, carry):
            sl = pl.ds(pl.multiple_of(r * rows, rows), rows)
            y = x_ref[sl, :] + gate * o_ref[sl, :]
            if final_norm:
                y = y * lax.rsqrt(jnp.mean(y * y, axis=-1, keepdims=True) + NORM_EPS) * fg
            o_ref[sl, :] = y
            return carry

        lax.fori_loop(0, x_ref.shape[0] // rows, body, 0)


def _ffn(x, g, mod, layer, w1, w3, w2, final_g, final_norm, m_rows, seq, n_batch, tm=1024, tf=256):
    d = x.shape[1]
    ff = w1.shape[2]
    block_bytes = 2 * tm * d * 4 + 3 * d * tf * w1.dtype.itemsize + tm * d
    return pl.pallas_call(
        functools.partial(_ffn_kernel, final_norm=final_norm, rows=128),
        grid=(m_rows // tm, ff // tf),
        in_specs=[
            pl.BlockSpec((tm, d), lambda i, j: (i, 0)),
            pl.BlockSpec((1, d), lambda i, j: (0, 0)),
            pl.BlockSpec((None, None, 1, mod.shape[-1]),
                         lambda i, j: (layer, _mod_row(i, tm, seq, n_batch), 0, 0)),
            pl.BlockSpec((None, d, tf), lambda i, j: (layer, 0, j)),
            pl.BlockSpec((None, d, tf), lambda i, j: (layer, 0, j)),
            pl.BlockSpec((None, tf, d), lambda i, j: (layer, j, 0)),
            pl.BlockSpec((1, d), lambda i, j: (0, 0)),
        ],
        out_specs=pl.BlockSpec((tm, d), lambda i, j: (i, 0)),
        out_shape=jax.ShapeDtypeStruct((m_rows, d), F32),
        scratch_shapes=[pltpu.VMEM((tm, d), BF16)],
        compiler_params=_params(("arbitrary", "arbitrary"), block_bytes),
        name="ffn",
    )(x, g.reshape(1, d), mod, w1, w3, w2, final_g.reshape(1, d))


def _conv_kernel(prev_ref, cur_ref, next_ref, w_ref, b_ref, o_ref, ext_ref, *, tm, halo, seq, ctx_len, n_lat_tiles):
    i = pl.program_id(0)
    ext_ref[0:halo, :] = prev_ref[...]
    ext_ref[halo:halo + tm, :] = cur_ref[...]
    ext_ref[halo + tm:, :] = next_ref[...]
    seg_len = jnp.where(i < n_lat_tiles, seq, ctx_len)
    pos = (i * tm + lax.broadcasted_iota(jnp.int32, (tm, 1), 0)) & (seg_len - 1)
    acc = jnp.zeros(cur_ref.shape, F32) + b_ref[...]
    half = SSM_CONV // 2
    for t in range(SSM_CONV):
        src = pos + (t - half)
        tap = ext_ref[halo - half + t:halo - half + t + tm, :]
        acc = acc + jnp.where((src >= 0) & (src < seg_len), tap, 0.0) * w_ref[t:t + 1, :]
    o_ref[...] = _silu(acc)


def _ssd_conv(p0, col0, width, conv_w, conv_b, seq, ctx_len, n_lat_rows, tm=1024, tc=512, halo=8):
    m = p0.shape[0]
    assert seq % tm == 0 and tm % ctx_len == 0 and (seq & (seq - 1)) == 0 and (ctx_len & (ctx_len - 1)) == 0
    cb0 = col0 // tc
    hb = tm // halo
    last_hb = m // halo - 1
    wpad = jnp.zeros((8, width), F32).at[:SSM_CONV].set(conv_w)
    return pl.pallas_call(
        functools.partial(_conv_kernel, tm=tm, halo=halo, seq=seq, ctx_len=ctx_len, n_lat_tiles=n_lat_rows // tm),
        grid=(m // tm, width // tc),
        in_specs=[
            pl.BlockSpec((halo, tc), lambda i, j: (jnp.maximum(i * hb - 1, 0), cb0 + j)),
            pl.BlockSpec((tm, tc), lambda i, j: (i, cb0 + j)),
            pl.BlockSpec((halo, tc), lambda i, j: (jnp.minimum((i + 1) * hb, last_hb), cb0 + j)),
            pl.BlockSpec((8, tc), lambda i, j: (0, j)),
            pl.BlockSpec((1, tc), lambda i, j: (0, j)),
        ],
        out_specs=pl.BlockSpec((tm, tc), lambda i, j: (i, j)),
        out_shape=jax.ShapeDtypeStruct((m, width), F32),
        scratch_shapes=[pltpu.VMEM((tm + 2 * halo, tc), F32)],
        compiler_params=_params(("arbitrary", "arbitrary"), 3 * tm * tc * 4),
        name="ssd_conv",
    )(p0, p0, p0, wpad, conv_b.reshape(1, width))


def _ssd_scan_kernel(xs_ref, bm_ref, cm_ref, dt_ref, prm_ref, y_ref, state_ref, *, n_heads):
    dirn = pl.program_id(1)
    step = pl.program_id(2)
    hp = SSM_HEAD_DIM
    rep = n_heads // SSM_GROUPS
    lc = SSM_CHUNK
    assert 2 * hp == LANES and rep % 2 == 0 and SSM_STATE == LANES and lc == LANES

    @pl.when(step == 0)
    def _():
        state_ref[...] = jnp.zeros_like(state_ref)

    dt_in = dt_ref[...] + prm_ref[1:2, :]
    dt = jnp.maximum(dt_in, 0.0) + jnp.log1p(jnp.exp(-jnp.abs(dt_in)))
    dta = dt * (-jnp.exp(prm_ref[0:1, :]) * LOG2E)
    row = lax.broadcasted_iota(jnp.int32, (lc, lc), 0)
    col = lax.broadcasted_iota(jnp.int32, (lc, lc), 1)
    causal = (row - col) * (1 - 2 * dirn) >= 0
    a_cum = jnp.dot(causal.astype(F32), dta, preferred_element_type=F32, precision=lax.Precision.HIGHEST)
    a_tot = jnp.where(dirn == 0, a_cum[lc - 1:lc, :], a_cum[0:1, :])
    w_end = dt * jnp.exp2(a_tot - a_cum)
    a_cum_t, dt_t, w_end_t = a_cum.T, dt.T, w_end.T
    lo = col < hp

    for g in range(SSM_GROUPS):
        gs = slice(g * SSM_STATE, (g + 1) * SSM_STATE)
        b_f = bm_ref[:, gs]
        c_g = cm_ref[:, gs].astype(BF16)
        cb = lax.dot_general(c_g, b_f.astype(BF16), (((1,), (1,)), ((), ())), preferred_element_type=F32)
        b_t = b_f.T
        cols_g = slice(g * rep * hp, (g + 1) * rep * hp)
        y_off = jnp.dot(c_g, state_ref[:, cols_g].astype(BF16), preferred_element_type=F32)
        for pr in range(rep // 2):
            cols = slice(g * rep * hp + pr * LANES, g * rep * hp + (pr + 1) * LANES)
            xs2 = xs_ref[:, cols].astype(BF16)
            y_d, s_n, e_a = [], [], []
            for h in (g * rep + 2 * pr, g * rep + 2 * pr + 1):
                a_col = jnp.broadcast_to(a_cum[:, h:h + 1], (lc, lc))
                decay = jnp.exp2(jnp.where(causal, a_col - a_cum_t[h:h + 1, :], -jnp.inf))
                m = (cb * decay * dt_t[h:h + 1, :]).astype(BF16)
                y_d.append(jnp.dot(m, xs2, preferred_element_type=F32))
                s_n.append(jnp.dot((b_t * w_end_t[h:h + 1, :]).astype(BF16), xs2, preferred_element_type=F32))
                e_a.append(jnp.exp2(a_col))
            h0 = g * rep + 2 * pr
            y_ref[:, cols] = (jnp.where(lo, y_d[0], y_d[1])
                              + y_off[:, pr * LANES:(pr + 1) * LANES] * jnp.where(lo, e_a[0], e_a[1]))
            chunk_decay = jnp.where(lo[0:1, :], jnp.exp2(a_tot[:, h0:h0 + 1]), jnp.exp2(a_tot[:, h0 + 1:h0 + 2]))
            state_ref[:, cols] = state_ref[:, cols] * chunk_decay + jnp.where(lo, s_n[0], s_n[1])


def _ssd_scan(xbc, dt_raw, prm, n_batch, seq, ctx_len, d_ssm):
    m = xbc.shape[0]
    lc = SSM_CHUNK
    n_heads = d_ssm // SSM_HEAD_DIM
    nc_ctx, nc_lat = ctx_len // lc, seq // lc
    ctx_blk0 = n_batch * nc_lat
    gn = SSM_GROUPS * SSM_STATE

    def row_blk(b, dirn, s):
        ctx_c = jnp.where(dirn == 0, s, nc_ctx - 1 - s)
        lat_c = jnp.where(dirn == 0, s - nc_ctx, nc_lat - 1 - (s - nc_ctx))
        return jnp.where(s < nc_ctx, ctx_blk0 + b * nc_ctx + ctx_c, b * nc_lat + lat_c)

    return pl.pallas_call(
        functools.partial(_ssd_scan_kernel, n_heads=n_heads),
        grid=(n_batch, 2, nc_ctx + nc_lat),
        in_specs=[
            pl.BlockSpec((lc, d_ssm), lambda b, r, s: (row_blk(b, r, s), 0)),
            pl.BlockSpec((lc, gn), lambda b, r, s: (row_blk(b, r, s), d_ssm // gn)),
            pl.BlockSpec((lc, gn), lambda b, r, s: (row_blk(b, r, s), d_ssm // gn + 1)),
            pl.BlockSpec((lc, LANES), lambda b, r, s: (row_blk(b, r, s), r)),
            pl.BlockSpec((None, 8, LANES), lambda b, r, s: (r, 0, 0)),
        ],
        out_specs=pl.BlockSpec((None, lc, d_ssm), lambda b, r, s: (r, row_blk(b, r, s), 0)),
        out_shape=jax.ShapeDtypeStruct((2, m, d_ssm), F32),
        scratch_shapes=[pltpu.VMEM((SSM_STATE, d_ssm), F32)],
        compiler_params=_params(("arbitrary", "arbitrary", "arbitrary"), 4 * lc * d_ssm * 4),
        name="ssd_scan",
    )(xbc, xbc, xbc, dt_raw, prm)


def _ssd_finish_kernel(y_ref, xs_ref, z_ref, dskip_ref, g_ref, o_ref, *, group_width):
    y = (y_ref[0] + y_ref[1] + dskip_ref[...] * xs_ref[...]) * _silu(z_ref[...])
    for g in range(y.shape[1] // group_width):
        sl = slice(g * group_width, (g + 1) * group_width)
        v = y[:, sl]
        vn = v * lax.rsqrt(jnp.mean(v * v, axis=-1, keepdims=True) + NORM_EPS) * g_ref[:, sl]
        o_ref[:, sl] = vn.astype(o_ref.dtype)


def _ssd_finish(y2, xbc, p0, d_skip_lanes, norm_g, d_ssm, tm=512):
    m = xbc.shape[0]
    return pl.pallas_call(
        functools.partial(_ssd_finish_kernel, group_width=d_ssm // SSM_GROUPS),
        grid=(m // tm,),
        in_specs=[
            pl.BlockSpec((2, tm, d_ssm), lambda i: (0, i, 0)),
            pl.BlockSpec((tm, d_ssm), lambda i: (i, 0)),
            pl.BlockSpec((tm, d_ssm), lambda i: (i, 0)),
            pl.BlockSpec((1, d_ssm), lambda i: (0, 0)),
            pl.BlockSpec((1, d_ssm), lambda i: (0, 0)),
        ],
        out_specs=pl.BlockSpec((tm, d_ssm), lambda i: (i, 0)),
        out_shape=jax.ShapeDtypeStruct((m, d_ssm), BF16),
        compiler_params=_params(("arbitrary",), 5 * tm * d_ssm * 4),
        name="ssd_finish",
    )(y2, xbc, p0, d_skip_lanes.reshape(1, d_ssm), norm_g.reshape(1, d_ssm))


def _rope(x, cos, sin_up, sin_dn):
    quarter = DIFF_HEAD_DIM // 4
    return (x * cos + pltpu.roll(x, LANES - quarter, axis=1) * sin_up + pltpu.roll(x, quarter, axis=1) * sin_dn)


def _diff_attn_kernel(*refs, with_lat, lambda_init, n_ctx, sub_q):
    if with_lat:
        (q_ref, kc_ref, kl_ref, vc_ref, vl_ref, qcos_ref, qsu_ref, qsd_ref, kcos_ref, ksu_ref, ksd_ref,
         lam_ref, g_ref, o_ref, k_scr, vt_scr) = refs
    else:
        q_ref, kc_ref, vc_ref, lam_ref, g_ref, o_ref, k_scr, vt_scr = refs

    @pl.when(pl.program_id(2) == 0)
    def _():
        k_scr[0:n_ctx, :] = kc_ref[...].astype(BF16)
        vt_scr[:, 0:n_ctx] = vc_ref[...].T.astype(BF16)
        if with_lat:
            k_scr[n_ctx:, :] = _rope(kl_ref[...], kcos_ref[...], ksu_ref[...], ksd_ref[...]).astype(BF16)
            vt_scr[:, n_ctx:] = vl_ref[...].T.astype(BF16)

    lam_p = lam_ref[...]
    lam = (jnp.exp(jnp.sum(lam_p[0:1, :] * lam_p[1:2, :], axis=-1, keepdims=True))
           - jnp.exp(jnp.sum(lam_p[2:3, :] * lam_p[3:4, :], axis=-1, keepdims=True)) + lambda_init)
    first = lax.broadcasted_iota(jnp.int32, (LANES, sub_q), 0) < DIFF_HEAD_DIM
    for t in range(q_ref.shape[0] // sub_q):
        rows = slice(t * sub_q, (t + 1) * sub_q)
        q = q_ref[rows, :]
        if with_lat:
            q = _rope(q, qcos_ref[rows, :], qsu_ref[rows, :], qsd_ref[rows, :])
        qt = (q * (DIFF_HEAD_DIM ** -0.5 * LOG2E)).T
        q2t = jnp.concatenate([jnp.where(first, qt, 0.0), jnp.where(first, 0.0, qt)], axis=1).astype(BF16)
        s = jnp.dot(k_scr[...], q2t, preferred_element_type=F32)
        e = jnp.exp2(s - jnp.max(s, axis=0, keepdims=True))
        r = 1.0 / jnp.sum(e, axis=0, keepdims=True)
        ovt = jnp.dot(vt_scr[...], e.astype(BF16), preferred_element_type=F32) * r
        ot = ovt[:, :sub_q] - lam * ovt[:, sub_q:]
        ot = ot * lax.rsqrt(jnp.mean(ot * ot, axis=0, keepdims=True) + NORM_EPS) * g_ref[...] * (1.0 - lambda_init)
        o_ref[rows, :] = ot.T.astype(o_ref.dtype)


def _diff_attn(p0, q_col0, k_col0, v_col0, n_heads, rope_tabs, lam_p, subln_g, lambda_init,
               n_batch, seq, ctx_len, with_lat, tq=1024, sub_q=256):
    cb = lambda c0: c0 // LANES
    ctx_rb0 = (n_batch * seq) // ctx_len
    n_keys = ctx_len + (seq if with_lat else 0)
    if with_lat:
        n_q = seq // tq
        q_spec = pl.BlockSpec((tq, LANES), lambda b, h, i: (b * n_q + i, cb(q_col0) + h))
    else:
        tq = ctx_len
        n_q = 1
        q_spec = pl.BlockSpec((tq, LANES), lambda b, h, i: (ctx_rb0 + b, cb(q_col0) + h))
    kc_spec = pl.BlockSpec((ctx_len, LANES), lambda b, h, i: (ctx_rb0 + b, cb(k_col0) + h))
    vc_spec = pl.BlockSpec((ctx_len, LANES), lambda b, h, i: (ctx_rb0 + b, cb(v_col0) + h))
    small =[pl.BlockSpec((8, LANES), lambda b, h, i: (0, 0)), pl.BlockSpec((LANES, sub_q), lambda b, h, i: (0, 0))]
    small_args = [lam_p, jnp.broadcast_to(subln_g[:, None], (LANES, sub_q))]
    if with_lat:
        kl_spec = pl.BlockSpec((seq, LANES), lambda b, h, i: (b, cb(k_col0) + h))
        vl_spec = pl.BlockSpec((seq, LANES), lambda b, h, i: (b, cb(v_col0) + h))
        qt = pl.BlockSpec((tq, LANES), lambda b, h, i: (i, 0))
        kt = pl.BlockSpec((seq, LANES), lambda b, h, i: (0, 0))
        in_specs = [q_spec, kc_spec, kl_spec, vc_spec, vl_spec, qt, qt, qt, kt, kt, kt] + small
        args = [p0, p0, p0, p0, p0, *rope_tabs, *rope_tabs] + small_args
    else:
        in_specs = [q_spec, kc_spec, vc_spec] + small
        args = [p0, p0, p0] + small_args
    block_bytes = 10 * seq * LANES * 4 if with_lat else 8 * ctx_len * LANES * 4
    return pl.pallas_call(
        functools.partial(_diff_attn_kernel, with_lat=with_lat, lambda_init=lambda_init, n_ctx=ctx_len,
                          sub_q=sub_q),
        grid=(n_batch, n_heads, n_q),
        in_specs=in_specs,
        out_specs=pl.BlockSpec((tq, LANES), lambda b, h, i: (b * n_q + i, h)),
        out_shape=jax.ShapeDtypeStruct((n_batch * n_q * tq, n_heads * LANES), BF16),
        scratch_shapes=[pltpu.VMEM((n_keys, LANES), BF16), pltpu.VMEM((LANES, n_keys), BF16)],
        compiler_params=_params(("arbitrary", "arbitrary", "arbitrary"), block_bytes),
        name="diff_attn_lat" if with_lat else "diff_attn_ctx",
    )(*args)


def _rope_tables(seq):
    half = DIFF_HEAD_DIM // 2
    pos = jnp.arange(seq)
    row, col = pos // GRID_W, pos % GRID_W
    inv_freq = ROPE_BASE ** (-jnp.arange(0, half, 2, dtype=F32) / half)
    lane = np.arange(LANES) % DIFF_HEAD_DIM
    use_col = jnp.asarray(lane >= half)
    first = jnp.asarray((lane % half) < half // 2)
    freq = inv_freq[jnp.asarray(lane % (half // 2))]
    p = jnp.where(use_col[None, :], col[:, None], row[:, None]).astype(F32)
    ang = p * freq[None, :]
    cos, sin = jnp.cos(ang), jnp.sin(ang)
    return cos, jnp.where(first[None, :], -sin, 0.0), jnp.where(first[None, :], 0.0, sin)


def _rpb_gather_kernel(rpb_ref, o_ref):
    n = o_ref.shape[1]
    k = rpb_ref.shape[1]
    colid = pl.program_id(0) * n + lax.broadcasted_iota(jnp.int32, (k, n), 1)
    j = lax.broadcasted_iota(jnp.int32, (k, n), 0)
    qc = lax.shift_right_logical(colid, int(math.log2(LANES)))
    half = lax.shift_right_logical(colid & (LANES - 1), int(math.log2(GRID_W)))
    kc = colid & (GRID_W - 1)
    sel = (j == half * LANES + jnp.clip(kc - qc + (WIN_COLS - 1), 0, 2 * WIN_COLS - 2)).astype(F32)
    o_ref[...] = jnp.dot(rpb_ref[...], sel, preferred_element_type=F32, precision=lax.Precision.HIGHEST)


def _rpb_pairs(rpb):
    nh, nr, ncol = rpb.shape
    n_slot = nr + 1
    left = jnp.pad(rpb, ((0, 0), (1, 0), (0, LANES - ncol)))
    right = jnp.pad(rpb, ((0, 0), (0, 1), (0, LANES - ncol)))
    rows = jnp.concatenate([left, right], axis=2).reshape(nh * n_slot, 2 * LANES)
    tn = 1024
    out = pl.pallas_call(
        _rpb_gather_kernel,
        grid=(GRID_W * LANES // tn,),
        in_specs=[pl.BlockSpec((nh * n_slot, 2 * LANES), lambda j: (0, 0))],
        out_specs=pl.BlockSpec((nh * n_slot, tn), lambda j: (0, j)),
        out_shape=jax.ShapeDtypeStruct((nh * n_slot, GRID_W * LANES), F32),
        compiler_params=_params(("arbitrary",), nh * n_slot * tn * 4 + 2 * LANES * tn * 4),
        name="rpb_gather",
    )(rows)
    return out.reshape(nh, n_slot, GRID_W, LANES)


def _na_block_plan(rows):
    kr = min(WIN_ROWS, rows)
    n_blk = rows // NA_Q_ROWS
    kb = np.clip(np.arange(n_blk) * NA_Q_ROWS - kr // 2, 0, rows - NA_K_ROWS)
    layouts, layout_of = [], []
    for blk in range(n_blk):
        dr = np.full((NA_Q_ROWS, NA_K_ROWS), -1, np.int64)
        for i in range(NA_Q_ROWS):
            r = blk * NA_Q_ROWS + i
            rs = int(np.clip(r - kr // 2, 0, rows - kr))
            for j in range(NA_K_ROWS):
                krow = kb[blk] + j
                if rs <= krow < rs + kr:
                    dr[i, j] = krow - r + WIN_ROWS - 1
        key = dr.tobytes()
        if key not in [l.tobytes() for l in layouts]:
            layouts.append(dr)
        layout_of.append([l.tobytes() for l in layouts].index(key))
    return kb, np.stack(layouts), np.asarray(layout_of)


def _na_bias_kernel(tp_ref, o_ref, *, layouts):
    qc = lax.broadcasted_iota(jnp.int32, (GRID_W, LANES), 0)
    lane = lax.broadcasted_iota(jnp.int32, (GRID_W, LANES), 1)
    kc = lane & (GRID_W - 1)
    left = lane < GRID_W
    col_start = jnp.clip(qc - WIN_COLS // 2, 0, GRID_W - WIN_COLS)
    col_ok = (kc >= col_start) & (kc < col_start + WIN_COLS)
    n_lay, n_q, n_k = layouts.shape
    for lay in range(n_lay):
        for i in range(n_q):
            for p in range(-(-n_k // 2)):
                d_l = int(layouts[lay, i, 2 * p])
                d_r = int(layouts[lay, i, 2 * p + 1]) if 2 * p + 1 < n_k else -1
                width = LANES if 2 * p + 1 < n_k else GRID_W
                if d_l < 0 and d_r < 0:
                    tile = jnp.full((GRID_W, LANES), -jnp.inf, F32)
                else:
                    assert d_l < 0 or d_r < 0 or d_r == d_l + 1
                    ok = col_ok
                    if d_l < 0:
                        ok = ok & jnp.logical_not(left)
                    if d_r < 0:
                        ok = ok & left
                    slot = d_r if d_r >= 0 else d_l + 1
                    tile = jnp.where(ok, tp_ref[slot] * LOG2E, -jnp.inf)
                o_ref[lay, i * GRID_W:(i + 1) * GRID_W, p * LANES:p * LANES + width] = tile[:, :width]


def _na_bias(rpb_pairs, layouts):
    nh, n_slot = rpb_pairs.shape[:2]
    n_lay = layouts.shape[0]
    tq, nkw = NA_Q_ROWS * GRID_W, NA_K_ROWS * GRID_W
    return pl.pallas_call(
        functools.partial(_na_bias_kernel, layouts=layouts),
        grid=(nh,),
        in_specs=[pl.BlockSpec((None, n_slot, GRID_W, LANES), lambda h: (h, 0, 0, 0))],
        out_specs=pl.BlockSpec((None, n_lay, tq, nkw), lambda h: (h, 0, 0, 0)),
        out_shape=jax.ShapeDtypeStruct((nh, n_lay, tq, nkw), F32),
        compiler_params=_params(("arbitrary",), n_lay * tq * nkw * 4),
        name="na_bias",
    )(rpb_pairs)


def _na_kernel(q_ref, kl_ref, vl_ref, kc_ref, vc_ref, bias_ref, o_ref, *, plan):
    tq = NA_Q_ROWS * GRID_W
    nkw = NA_K_ROWS * GRID_W
    nt = (((1,), (1,)), ((), ()))
    kc = kc_ref[...]
    vc = vc_ref[...]
    for blk, (kb, lay) in enumerate(plan):
        qs = slice(blk * tq, (blk + 1) * tq)
        ks = slice(kb * GRID_W, kb * GRID_W + nkw)
        q = q_ref[qs, :]
        s_w = lax.dot_general(q, kl_ref[ks, :], nt, preferred_element_type=F32) + bias_ref[lay]
        s_c = lax.dot_general(q, kc, nt, preferred_element_type=F32)
        m = jnp.maximum(jnp.max(s_w, axis=-1, keepdims=True), jnp.max(s_c, axis=-1, keepdims=True))
        e_w = jnp.exp2(s_w - m)
        e_c = jnp.exp2(s_c - m)
        l = jnp.sum(e_w, axis=-1, keepdims=True) + jnp.sum(e_c, axis=-1, keepdims=True)
        o = (jnp.dot(e_w.astype(BF16), vl_ref[ks, :], preferred_element_type=F32)
             + jnp.dot(e_c.astype(BF16), vc, preferred_element_type=F32))
        o_ref[qs, :] = (o * (1.0 / l)).astype(o_ref.dtype)


def _na_attn(p1, rpb, n_batch, seq, ctx_len, n_heads):
    rows = seq // GRID_W
    kb, layouts, layout_of = _na_block_plan(rows)
    bias = _na_bias(_rpb_pairs(rpb), layouts)
    n_lay = layouts.shape[0]
    tq = NA_Q_ROWS * GRID_W
    nkw = NA_K_ROWS * GRID_W
    ctx_rb0 = (n_batch * seq) // ctx_len
    plan = tuple((int(k), int(l)) for k, l in zip(kb, layout_of))
    return pl.pallas_call(
        functools.partial(_na_kernel, plan=plan),
        grid=(n_heads, n_batch),
        in_specs=[
            pl.BlockSpec((seq, LANES), lambda h, b: (b, h)),
            pl.BlockSpec((seq, LANES), lambda h, b: (b, n_heads + h)),
            pl.BlockSpec((seq, LANES), lambda h, b: (b, 2 * n_heads + h)),
            pl.BlockSpec((ctx_len, LANES), lambda h, b: (ctx_rb0 + b, n_heads + h)),
            pl.BlockSpec((ctx_len, LANES), lambda h, b: (ctx_rb0 + b, 2 * n_heads + h)),
            pl.BlockSpec((None, n_lay, tq, nkw), lambda h, b: (h, 0, 0, 0)),
        ],
        out_specs=pl.BlockSpec((seq, LANES), lambda h, b: (b, h)),
        out_shape=jax.ShapeDtypeStruct((n_batch * seq, n_heads * LANES), BF16),
        compiler_params=_params(("arbitrary", "arbitrary"), n_lay * tq * nkw * 4 + 4 * seq * LANES * 2),
        name="na_attn",
    )(p1, p1, p1, p1, p1, bias)


def _pack_rows(rows, n_rows=8):
    out = jnp.zeros((n_rows, LANES), F32)
    for r, v in enumerate(rows):
        out = out.at[r, :v.shape[0]].set(v.astype(F32))
    return out


def kernel(x, c, ctx, c_ctx, ada_w, ada_b, norm_mix_g, norm_ffn_g, final_norm_g, ffn_w1, ffn_w3, ffn_w2, ev_w_in, ev_conv_w, ev_conv_b, ev_a_log, ev_dt_bias, ev_d_skip, ev_ssm_norm_g, ev_lam_q1, ev_lam_k1, ev_lam_q2, ev_lam_k2, ev_subln_g, ev_w_out, od_w_in, od_rpb, od_w_out):
    n_batch, seq, d = x.shape
    ctx_len = ctx.shape[1]
    depth = ada_w.shape[0]
    n_lat = n_batch * seq
    n_tok = n_lat + n_batch * ctx_len

    d_ssm = ev_ssm_norm_g.shape[1]
    n_ssm_heads = ev_d_skip.shape[1]
    d_xbc = ev_conv_w.shape[2]
    d_qk = d_v = (ev_w_in.shape[2] - d_ssm - d_xbc - 2 * n_ssm_heads) // 3
    n_diff_heads = d_v // (2 * DIFF_HEAD_DIM)
    n_na_heads = od_rpb.shape[1]

    stream = (x.reshape(n_lat, d), ctx.reshape(n_batch * ctx_len, d))
    cond = jnp.zeros((COND_ROWS, d), F32).at[:n_batch].set(c).at[n_batch].set(c_ctx)
    mod = _ada_mod(cond, ada_w, ada_b).reshape(depth, COND_ROWS, 1, 6 * d)
    rope_tabs = _rope_tables(seq)
    kw = dict(seq=seq, n_batch=n_batch)
    ffn_w = (ffn_w1, ffn_w3, ffn_w2)

    for i in range(depth):
        ctx_out = i < depth - 1
        m_rows = n_tok if ctx_out else n_lat
        j = i // 2
        if i % 2 == 0:
            lambda_init = 0.8 - 0.6 * math.exp(-0.3 * i)
            w_in = ev_w_in[j]
            dt0 = d_ssm + d_xbc
            w_main = jnp.concatenate([w_in[:, :dt0].astype(BF16), w_in[:, dt0 + 2 * n_ssm_heads:].astype(BF16)], axis=1)
            w_dt = jnp.zeros((d, 2 * LANES), F32)
            w_dt = w_dt.at[:, :n_ssm_heads].set(w_in[:, dt0:dt0 + n_ssm_heads])
            w_dt = w_dt.at[:, LANES:LANES + n_ssm_heads].set(w_in[:, dt0 + n_ssm_heads:dt0 + 2 * n_ssm_heads])
            p0, dt_raw = _proj(stream, norm_mix_g[i], mod, i, w_main, w_dt.astype(BF16), F32, **kw)
            q0 = d_ssm + d_xbc
            k0, v0 = q0 + d_qk, q0 + 2 * d_qk

            xbc = _ssd_conv(p0, d_ssm, d_xbc, ev_conv_w[j], ev_conv_b[j], seq, ctx_len, n_lat)
            prm = jnp.stack([_pack_rows([ev_a_log[j, r], ev_dt_bias[j, r]]) for r in range(2)])
            y2 = _ssd_scan(xbc, dt_raw, prm, n_batch, seq, ctx_len, d_ssm)
            mix_ssd = _ssd_finish(y2, xbc, p0, jnp.repeat(ev_d_skip[j], SSM_HEAD_DIM), ev_ssm_norm_g[j], d_ssm)

            lam_p = _pack_rows([ev_lam_q1[j], ev_lam_k1[j], ev_lam_q2[j], ev_lam_k2[j]])
            attn_args = (p0, q0, k0, v0, n_diff_heads, rope_tabs, lam_p, ev_subln_g[j], lambda_init,
                         n_batch, seq, ctx_len)
            mix_attn = _diff_attn(*attn_args, with_lat=True)
            if ctx_out:
                mix_attn = (mix_attn, _diff_attn(*attn_args, with_lat=False))
            lhs = [mix_ssd, mix_attn]
            w_out = ev_w_out[j].astype(BF16)
        else:
            d_na = n_na_heads * NA_HEAD_DIM
            q_scale = jnp.where(jnp.arange(3 * d_na) < d_na, NA_HEAD_DIM ** -0.5 * LOG2E, 1.0).astype(F32)
            p1 = _proj(stream, norm_mix_g[i], mod, i, od_w_in[j].astype(BF16), None, BF16, col_scale=q_scale, **kw)
            assert not ctx_out, "context-query neighbourhood layers are not needed at this depth"
            lhs = [_na_attn(p1, od_rpb[j], n_batch, seq, ctx_len, n_na_heads)]
            w_out = od_w_out[j].astype(BF16)
        stream = _out_proj(lhs, w_out, stream, mod, i, 2, m_rows, **kw)
        stream = _ffn(stream, norm_ffn_g[i], mod, i, *ffn_w, final_norm_g, not ctx_out, m_rows, **kw)
    return stream.reshape(n_batch, seq, d)
```

```python
import functools
import math

import jax
import jax.numpy as jnp
import numpy as np
from jax import lax
from jax.experimental import pallas as pl
from jax.experimental.pallas import tpu as pltpu

F32 = jnp.float32
BF16 = jnp.bfloat16

GRID_W = 64
SSM_HEAD_DIM = 64
SSM_GROUPS = 4
SSM_STATE = 128
SSM_CONV = 5
SSM_CHUNK = 128
DIFF_HEAD_DIM = 64
NA_HEAD_DIM = 128
WIN_ROWS = 8
WIN_COLS = 16
ROPE_BASE = 10000.0
NORM_EPS = 1e-6
LOG2E = math.log2(math.e)

LANES = 128
V7X_VMEM_BYTES = 64 * 1024 * 1024
VMEM_HEADROOM_BYTES = 6 * 1024 * 1024

NA_Q_ROWS = 4
NA_K_ROWS = NA_Q_ROWS + WIN_ROWS - 1
COND_ROWS = 16


def _vmem_limit(block_bytes):
    return int(min(V7X_VMEM_BYTES - VMEM_HEADROOM_BYTES, max(32 * 1024 * 1024, 2 * block_bytes + 16 * 1024 * 1024)))


def _params(semantics, block_bytes):
    return pltpu.CompilerParams(dimension_semantics=semantics, vmem_limit_bytes=_vmem_limit(block_bytes))


def _silu(v):
    return v * jax.nn.sigmoid(v)


def _mod_row(i, tm, seq, n_batch):
    return jnp.minimum((i * tm) // seq, n_batch)


def _ada_kernel(cond_ref, w_ref, b_ref, o_ref):
    s = _silu(cond_ref[...]).astype(BF16)
    o_ref[...] = jnp.dot(s, w_ref[...].astype(BF16), preferred_element_type=F32) + b_ref[...]


def _ada_mod(cond, ada_w, ada_b, tn=1024):
    depth, d, n = ada_w.shape
    return pl.pallas_call(
        _ada_kernel,
        grid=(depth, n // tn),
        in_specs=[
            pl.BlockSpec((COND_ROWS, d), lambda l, j: (0, 0)),
            pl.BlockSpec((None, d, tn), lambda l, j: (l, 0, j)),
            pl.BlockSpec((None, 1, tn), lambda l, j: (l, 0, j)),
        ],
        out_specs=pl.BlockSpec((None, COND_ROWS, tn), lambda l, j: (l, 0, j)),
        out_shape=jax.ShapeDtypeStruct((depth, COND_ROWS, n), F32),
        compiler_params=_params(("arbitrary", "arbitrary"), d * tn * 4),
        name="ada_mod",
    )(cond, ada_w, ada_b.reshape(depth, 1, n))


def _norm_mod_into(x_ref, g_ref, mod_ref, h_ref, shift_idx, scale_idx, rows=128):
    d = x_ref.shape[-1]
    shift = mod_ref[:, shift_idx * d:(shift_idx + 1) * d]
    gain = g_ref[...] * (1.0 + mod_ref[:, scale_idx * d:(scale_idx + 1) * d])

    def body(r, carry):
        sl = pl.ds(pl.multiple_of(r * rows, rows), rows)
        x = x_ref[sl, :]
        inv = lax.rsqrt(jnp.mean(x * x, axis=-1, keepdims=True) + NORM_EPS)
        h_ref[sl, :] = (x * inv * gain + shift).astype(h_ref.dtype)
        return carry

    lax.fori_loop(0, x_ref.shape[0] // rows, body, 0)


def _proj_kernel(*refs, n_x, n_lat_tiles, has_aux, has_scale):
    rest = list(refs)
    x_refs = [rest.pop(0) for _ in range(n_x)]
    g_ref, mod_ref, w_ref = rest.pop(0), rest.pop(0), rest.pop(0)
    cs_ref = rest.pop(0) if has_scale else None
    if has_aux:
        waux_ref, o_ref, oaux_ref, h_ref = rest
    else:
        o_ref, h_ref = rest

    @pl.when(pl.program_id(1) == 0)
    def _():
        if n_x == 1:
            _norm_mod_into(x_refs[0], g_ref, mod_ref, h_ref, 0, 1)
        else:
            is_lat = pl.program_id(0) < n_lat_tiles
            pl.when(is_lat)(lambda: _norm_mod_into(x_refs[0], g_ref, mod_ref, h_ref, 0, 1))
            pl.when(jnp.logical_not(is_lat))(lambda: _norm_mod_into(x_refs[1], g_ref, mod_ref, h_ref, 0, 1))
        if has_aux:
            oaux_ref[...] = jnp.dot(h_ref[...], waux_ref[...], preferred_element_type=F32)

    acc = jnp.dot(h_ref[...], w_ref[...], preferred_element_type=F32)
    if has_scale:
        acc = acc * cs_ref[...]
    o_ref[...] = acc.astype(o_ref.dtype)


def _proj(x, g, mod, layer, w, w_aux, out_dtype, seq, n_batch, col_scale=None, tm=1024, tn=1024):
    d, n = w.shape
    m = sum(a.shape[0] for a in x) if isinstance(x, tuple) else x.shape[0]
    n_lat_tiles = (n_batch * seq) // tm
    has_aux = w_aux is not None
    has_scale = col_scale is not None
    mod_spec = pl.BlockSpec((None, None, 1, mod.shape[-1]),
                            lambda i, j: (layer, _mod_row(i, tm, seq, n_batch), 0, 0))
    in_specs, args = _row_operand(x, tm, n_lat_tiles, d, lambda j: 0)
    n_x = len(args)
    in_specs += [
        pl.BlockSpec((1, d), lambda i, j: (0, 0)),
        mod_spec,
        pl.BlockSpec((d, tn), lambda i, j: (0, j)),
    ]
    out_specs = [pl.BlockSpec((tm, tn), lambda i, j: (i, j))]
    out_shape = [jax.ShapeDtypeStruct((m, n), out_dtype)]
    args += [g.reshape(1, d), mod, w]
    if has_scale:
        in_specs.append(pl.BlockSpec((1, tn), lambda i, j: (0, j)))
        args.append(col_scale.reshape(1, n))
    if has_aux:
        na = w_aux.shape[1]
        in_specs.append(pl.BlockSpec((d, na), lambda i, j: (0, 0)))
        out_specs.append(pl.BlockSpec((tm, na), lambda i, j: (i, 0)))
        out_shape.append(jax.ShapeDtypeStruct((m, na), F32))
        args.append(w_aux)
    block_bytes = n_x * tm * d * 4 + d * tn * 2 + tm * tn * 4 + tm * d
    outs = pl.pallas_call(
        functools.partial(_proj_kernel, n_x=n_x, n_lat_tiles=n_lat_tiles, has_aux=has_aux, has_scale=has_scale),
        grid=(m // tm, n // tn),
        in_specs=in_specs,
        out_specs=out_specs,
        out_shape=out_shape,
        scratch_shapes=[pltpu.VMEM((tm, d), BF16)],
        compiler_params=_params(("arbitrary", "arbitrary"), block_bytes),
        name="proj",
    )(*args)
    return outs if has_aux else outs[0]


def _row_operand(a, tm, n_lat_tiles, width, col_of_j):
    if not isinstance(a, tuple):
        return [pl.BlockSpec((tm, width), lambda i, j: (i, col_of_j(j)))], [a]
    lat, ctx = a
    assert ctx.shape[0] == tm and lat.shape[0] == n_lat_tiles * tm

    def lat_index(i, j):
        return jnp.minimum(i, n_lat_tiles - 1), jnp.where(i < n_lat_tiles, col_of_j(j), 0)

    def ctx_index(i, j):
        return 0, jnp.where(i < n_lat_tiles, 0, col_of_j(j))

    return [pl.BlockSpec((tm, width), lat_index), pl.BlockSpec((tm, width), ctx_index)], [lat, ctx]


def _pick_rows(refs, n_lat_tiles):
    if len(refs) == 1:
        return refs[0][...]
    return jnp.where(pl.program_id(0) < n_lat_tiles, refs[0][...], refs[1][...])


def _out_proj_kernel(*refs, arity, n_lat_tiles):
    refs = list(refs)
    n_lhs = len(arity) - 1
    groups = [[refs.pop(0) for _ in range(n)] for n in arity[:-1]]
    ws = [refs.pop(0) for _ in range(n_lhs)]
    res = [refs.pop(0) for _ in range(arity[-1])]
    gate_ref, o_ref = refs
    acc = None
    for grp, w_ref in zip(groups, ws):
        part = jnp.dot(_pick_rows(grp, n_lat_tiles), w_ref[...], preferred_element_type=F32)
        acc = part if acc is None else acc + part
    o_ref[...] = _pick_rows(res, n_lat_tiles) + gate_ref[...] * acc


def _out_proj(lhs_list, w, res, mod, layer, gate_idx, m_rows, seq, n_batch, tm=1024, tn=1024):
    d = w.shape[1]
    n_lat_tiles = (n_batch * seq) // tm
    in_specs, args, arity, widths = [], [], [], []
    for a in lhs_list:
        kk = (a[0] if isinstance(a, tuple) else a).shape[1]
        sp, ar = _row_operand(a, tm, n_lat_tiles, kk, lambda j: 0)
        in_specs += sp
        args += ar
        arity.append(len(ar))
        widths.append(kk)
    row0 = 0
    for kk in widths:
        in_specs.append(pl.BlockSpec((kk, tn), lambda i, j, rb=row0 // kk: (rb, j)))
        args.append(w)
        row0 += kk
    sp, ar = _row_operand(res, tm, n_lat_tiles, tn, lambda j: j)
    in_specs += sp
    args += ar
    arity.append(len(ar))
    in_specs.append(pl.BlockSpec((None, None, 1, tn),
                                 lambda i, j: (layer, _mod_row(i, tm, seq, n_batch), 0, gate_idx * (d // tn) + j)))
    args.append(mod)
    k_total = sum(widths)
    block_bytes = 2 * tm * k_total * 2 + k_total * tn * 2 + 3 * tm * tn * 4
    return pl.pallas_call(
        functools.partial(_out_proj_kernel, arity=tuple(arity), n_lat_tiles=n_lat_tiles),
        grid=(m_rows // tm, d // tn),
        in_specs=in_specs,
        out_specs=pl.BlockSpec((tm, tn), lambda i, j: (i, j)),
        out_shape=jax.ShapeDtypeStruct((m_rows, d), F32),
        compiler_params=_params(("arbitrary", "arbitrary"), block_bytes),
        name="out_proj",
    )(*args)


def _ffn_kernel(x_ref, g_ref, mod_ref, w1_ref, w3_ref, w2_ref, fg_ref, o_ref, h_ref, *, final_norm, rows):
    j = pl.program_id(1)
    d = x_ref.shape[-1]

    @pl.when(j == 0)
    def _():
        _norm_mod_into(x_ref, g_ref, mod_ref, h_ref, 3, 4)
        o_ref[...] = jnp.zeros_like(o_ref)

    h = h_ref[...]
    a = jnp.dot(h, w1_ref[...].astype(BF16), preferred_element_type=F32)
    b = jnp.dot(h, w3_ref[...].astype(BF16), preferred_element_type=F32)
    u = (_silu(a) * b).astype(BF16)
    o_ref[...] += jnp.dot(u, w2_ref[...].astype(BF16), preferred_element_type=F32)

    @pl.when(j == pl.num_programs(1) - 1)
    def _():
        gate = mod_ref[:, 5 * d:6 * d]
        fg = fg_ref[...]

        def body(r, carry):
            sl = pl.ds(pl.multiple_of(r * rows, rows), rows)
            y = x_ref[sl, :] + gate * o_ref[sl, :]
            if final_norm:
                y = y * lax.rsqrt(jnp.mean(y * y, axis=-1, keepdims=True) + NORM_EPS) * fg
            o_ref[sl, :] = y
            return carry

        lax.fori_loop(0, x_ref.shape[0] // rows, body, 0)


def _ffn(x, g, mod, layer, w1, w3, w2, final_g, final_norm, m_rows, seq, n_batch, tm=1024, tf=256):
    d = x.shape[1]
    ff = w1.shape[2]
    block_bytes = 2 * tm * d * 4 + 3 * d * tf * w1.dtype.itemsize + tm * d
    return pl.pallas_call(
        functools.partial(_ffn_kernel, final_norm=final_norm, rows=128),
        grid=(m_rows // tm, ff // tf),
        in_specs=[
            pl.BlockSpec((tm, d), lambda i, j: (i, 0)),
            pl.BlockSpec((1, d), lambda i, j: (0, 0)),
            pl.BlockSpec((None, None, 1, mod.shape[-1]),
                         lambda i, j: (layer, _mod_row(i, tm, seq, n_batch), 0, 0)),
            pl.BlockSpec((None, d, tf), lambda i, j: (layer, 0, j)),
            pl.BlockSpec((None, d, tf), lambda i, j: (layer, 0, j)),
            pl.BlockSpec((None, tf, d), lambda i, j: (layer, j, 0)),
            pl.BlockSpec((1, d), lambda i, j: (0, 0)),
        ],
        out_specs=pl.BlockSpec((tm, d), lambda i, j: (i, 0)),
        out_shape=jax.ShapeDtypeStruct((m_rows, d), F32),
        scratch_shapes=[pltpu.VMEM((tm, d), BF16)],
        compiler_params=_params(("arbitrary", "arbitrary"), block_bytes),
        name="ffn",
    )(x, g.reshape(1, d), mod, w1, w3, w2, final_g.reshape(1, d))


def _conv_kernel(prev_ref, cur_ref, next_ref, w_ref, b_ref, o_ref, ext_ref, *, tm, halo, seq, ctx_len, n_lat_tiles):
    i = pl.program_id(0)
    ext_ref[0:halo, :] = prev_ref[...]
    ext_ref[halo:halo + tm, :] = cur_ref[...]
    ext_ref[halo + tm:, :] = next_ref[...]
    seg_len = jnp.where(i < n_lat_tiles, seq, ctx_len)
    pos = (i * tm + lax.broadcasted_iota(jnp.int32, (tm, 1), 0)) & (seg_len - 1)
    acc = jnp.zeros(cur_ref.shape, F32) + b_ref[...]
    half = SSM_CONV // 2
    for t in range(SSM_CONV):
        src = pos + (t - half)
        tap = ext_ref[halo - half + t:halo - half + t + tm, :]
        acc = acc + jnp.where((src >= 0) & (src < seg_len), tap, 0.0) * w_ref[t:t + 1, :]
    o_ref[...] = _silu(acc)


def _ssd_conv(p0, col0, width, conv_w, conv_b, seq, ctx_len, n_lat_rows, tm=1024, tc=512, halo=8):
    m = p0.shape[0]
    assert seq % tm == 0 and tm % ctx_len == 0 and (seq & (seq - 1)) == 0 and (ctx_len & (ctx_len - 1)) == 0
    cb0 = col0 // tc
    hb = tm // halo
    last_hb = m // halo - 1
    wpad = jnp.zeros((8, width), F32).at[:SSM_CONV].set(conv_w)
    return pl.pallas_call(
        functools.partial(_conv_kernel, tm=tm, halo=halo, seq=seq, ctx_len=ctx_len, n_lat_tiles=n_lat_rows // tm),
        grid=(m // tm, width // tc),
        in_specs=[
            pl.BlockSpec((halo, tc), lambda i, j: (jnp.maximum(i * hb - 1, 0), cb0 + j)),
            pl.BlockSpec((tm, tc), lambda i, j: (i, cb0 + j)),
            pl.BlockSpec((halo, tc), lambda i, j: (jnp.minimum((i + 1) * hb, last_hb), cb0 + j)),
            pl.BlockSpec((8, tc), lambda i, j: (0, j)),
            pl.BlockSpec((1, tc), lambda i, j: (0, j)),
        ],
        out_specs=pl.BlockSpec((tm, tc), lambda i, j: (i, j)),
        out_shape=jax.ShapeDtypeStruct((m, width), F32),
        scratch_shapes=[pltpu.VMEM((tm + 2 * halo, tc), F32)],
        compiler_params=_params(("arbitrary", "arbitrary"), 3 * tm * tc * 4),
        name="ssd_conv",
    )(p0, p0, p0, wpad, conv_b.reshape(1, width))


def _ssd_scan_kernel(xs_ref, bm_ref, cm_ref, dt_ref, prm_ref, y_ref, state_ref, *, n_heads):
    dirn = pl.program_id(1)
    step = pl.program_id(2)
    hp = SSM_HEAD_DIM
    rep = n_heads // SSM_GROUPS
    lc = SSM_CHUNK
    assert 2 * hp == LANES and rep % 2 == 0 and SSM_STATE == LANES and lc == LANES

    @pl.when(step == 0)
    def _():
        state_ref[...] = jnp.zeros_like(state_ref)

    dt_in = dt_ref[...] + prm_ref[1:2, :]
    dt = jnp.maximum(dt_in, 0.0) + jnp.log1p(jnp.exp(-jnp.abs(dt_in)))
    dta = dt * (-jnp.exp(prm_ref[0:1, :]) * LOG2E)
    row = lax.broadcasted_iota(jnp.int32, (lc, lc), 0)
    col = lax.broadcasted_iota(jnp.int32, (lc, lc), 1)
    causal = (row - col) * (1 - 2 * dirn) >= 0
    a_cum = jnp.dot(causal.astype(F32), dta, preferred_element_type=F32, precision=lax.Precision.HIGHEST)
    a_tot = jnp.where(dirn == 0, a_cum[lc - 1:lc, :], a_cum[0:1, :])
    w_end = dt * jnp.exp2(a_tot - a_cum)
    a_cum_t, dt_t, w_end_t = a_cum.T, dt.T, w_end.T
    lo = col < hp

    for g in range(SSM_GROUPS):
        gs = slice(g * SSM_STATE, (g + 1) * SSM_STATE)
        b_f = bm_ref[:, gs]
        c_g = cm_ref[:, gs].astype(BF16)
        cb = lax.dot_general(c_g, b_f.astype(BF16), (((1,), (1,)), ((), ())), preferred_element_type=F32)
        b_t = b_f.T
        cols_g = slice(g * rep * hp, (g + 1) * rep * hp)
        y_off = jnp.dot(c_g, state_ref[:, cols_g].astype(BF16), preferred_element_type=F32)
        for pr in range(rep // 2):
            cols = slice(g * rep * hp + pr * LANES, g * rep * hp + (pr + 1) * LANES)
            xs2 = xs_ref[:, cols].astype(BF16)
            y_d, s_n, e_a = [], [], []
            for h in (g * rep + 2 * pr, g * rep + 2 * pr + 1):
                a_col = jnp.broadcast_to(a_cum[:, h:h + 1], (lc, lc))
                decay = jnp.exp2(jnp.where(causal, a_col - a_cum_t[h:h + 1, :], -jnp.inf))
                m = (cb * decay * dt_t[h:h + 1, :]).astype(BF16)
                y_d.append(jnp.dot(m, xs2, preferred_element_type=F32))
                s_n.append(jnp.dot((b_t * w_end_t[h:h + 1, :]).astype(BF16), xs2, preferred_element_type=F32))
                e_a.append(jnp.exp2(a_col))
            h0 = g * rep + 2 * pr
            y_ref[:, cols] = (jnp.where(lo, y_d[0], y_d[1])
                              + y_off[:, pr * LANES:(pr + 1) * LANES] * jnp.where(lo, e_a[0], e_a[1]))
            chunk_decay = jnp.where(lo[0:1, :], jnp.exp2(a_tot[:, h0:h0 + 1]), jnp.exp2(a_tot[:, h0 + 1:h0 + 2]))
            state_ref[:, cols] = state_ref[:, cols] * chunk_decay + jnp.where(lo, s_n[0], s_n[1])


def _ssd_scan(xbc, dt_raw, prm, n_batch, seq, ctx_len, d_ssm):
    m = xbc.shape[0]
    lc = SSM_CHUNK
    n_heads = d_ssm // SSM_HEAD_DIM
    nc_ctx, nc_lat = ctx_len // lc, seq // lc
    ctx_blk0 = n_batch * nc_lat
    gn = SSM_GROUPS * SSM_STATE

    def row_blk(b, dirn, s):
        ctx_c = jnp.where(dirn == 0, s, nc_ctx - 1 - s)
        lat_c = jnp.where(dirn == 0, s - nc_ctx, nc_lat - 1 - (s - nc_ctx))
        return jnp.where(s < nc_ctx, ctx_blk0 + b * nc_ctx + ctx_c, b * nc_lat + lat_c)

    return pl.pallas_call(
        functools.partial(_ssd_scan_kernel, n_heads=n_heads),
        grid=(n_batch, 2, nc_ctx + nc_lat),
        in_specs=[
            pl.BlockSpec((lc, d_ssm), lambda b, r, s: (row_blk(b, r, s), 0)),
            pl.BlockSpec((lc, gn), lambda b, r, s: (row_blk(b, r, s), d_ssm // gn)),
            pl.BlockSpec((lc, gn), lambda b, r, s: (row_blk(b, r, s), d_ssm // gn + 1)),
            pl.BlockSpec((lc, LANES), lambda b, r, s: (row_blk(b, r, s), r)),
            pl.BlockSpec((None, 8, LANES), lambda b, r, s: (r, 0, 0)),
        ],
        out_specs=pl.BlockSpec((None, lc, d_ssm), lambda b, r, s: (r, row_blk(b, r, s), 0)),
        out_shape=jax.ShapeDtypeStruct((2, m, d_ssm), F32),
        scratch_shapes=[pltpu.VMEM((SSM_STATE, d_ssm), F32)],
        compiler_params=_params(("arbitrary", "arbitrary", "arbitrary"), 4 * lc * d_ssm * 4),
        name="ssd_scan",
    )(xbc, xbc, xbc, dt_raw, prm)


def _ssd_finish_kernel(y_ref, xs_ref, z_ref, dskip_ref, g_ref, o_ref, *, group_width):
    y = (y_ref[0] + y_ref[1] + dskip_ref[...] * xs_ref[...]) * _silu(z_ref[...])
    for g in range(y.shape[1] // group_width):
        sl = slice(g * group_width, (g + 1) * group_width)
        v = y[:, sl]
        vn = v * lax.rsqrt(jnp.mean(v * v, axis=-1, keepdims=True) + NORM_EPS) * g_ref[:, sl]
        o_ref[:, sl] = vn.astype(o_ref.dtype)


def _ssd_finish(y2, xbc, p0, d_skip_lanes, norm_g, d_ssm, tm=512):
    m = xbc.shape[0]
    return pl.pallas_call(
        functools.partial(_ssd_finish_kernel, group_width=d_ssm // SSM_GROUPS),
        grid=(m // tm,),
        in_specs=[
            pl.BlockSpec((2, tm, d_ssm), lambda i: (0, i, 0)),
            pl.BlockSpec((tm, d_ssm), lambda i: (i, 0)),
            pl.BlockSpec((tm, d_ssm), lambda i: (i, 0)),
            pl.BlockSpec((1, d_ssm), lambda i: (0, 0)),
            pl.BlockSpec((1, d_ssm), lambda i: (0, 0)),
        ],
        out_specs=pl.BlockSpec((tm, d_ssm), lambda i: (i, 0)),
        out_shape=jax.ShapeDtypeStruct((m, d_ssm), BF16),
        compiler_params=_params(("arbitrary",), 5 * tm * d_ssm * 4),
        name="ssd_finish",
    )(y2, xbc, p0, d_skip_lanes.reshape(1, d_ssm), norm_g.reshape(1, d_ssm))


def _rope(x, cos, sin_up, sin_dn):
    quarter = DIFF_HEAD_DIM // 4
    return (x * cos + pltpu.roll(x, LANES - quarter, axis=1) * sin_up + pltpu.roll(x, quarter, axis=1) * sin_dn)


def _diff_attn_kernel(*refs, with_lat, lambda_init, n_ctx, sub_q):
    if with_lat:
        (q_ref, kc_ref, kl_ref, vc_ref, vl_ref, qcos_ref, qsu_ref, qsd_ref, kcos_ref, ksu_ref, ksd_ref,
         lam_ref, g_ref, o_ref, k_scr, vt_scr) = refs
    else:
        q_ref, kc_ref, vc_ref, lam_ref, g_ref, o_ref, k_scr, vt_scr = refs

    @pl.when(pl.program_id(2) == 0)
    def _():
        k_scr[0:n_ctx, :] = kc_ref[...].astype(BF16)
        vt_scr[:, 0:n_ctx] = vc_ref[...].T.astype(BF16)
        if with_lat:
            k_scr[n_ctx:, :] = _rope(kl_ref[...], kcos_ref[...], ksu_ref[...], ksd_ref[...]).astype(BF16)
            vt_scr[:, n_ctx:] = vl_ref[...].T.astype(BF16)

    lam_p = lam_ref[...]
    lam = (jnp.exp(jnp.sum(lam_p[0:1, :] * lam_p[1:2, :], axis=-1, keepdims=True))
           - jnp.exp(jnp.sum(lam_p[2:3, :] * lam_p[3:4, :], axis=-1, keepdims=True)) + lambda_init)
    first = lax.broadcasted_iota(jnp.int32, (LANES, sub_q), 0) < DIFF_HEAD_DIM
    n_sub = q_ref.shape[0] // sub_q

    def scores(t):
        rows = slice(t * sub_q, (t + 1) * sub_q)
        q = q_ref[rows, :]
        if with_lat:
            q = _rope(q, qcos_ref[rows, :], qsu_ref[rows, :], qsd_ref[rows, :])
        qt = (q * (DIFF_HEAD_DIM ** -0.5 * LOG2E)).T
        q2t = jnp.concatenate([jnp.where(first, qt, 0.0), jnp.where(first, 0.0, qt)], axis=1).astype(BF16)
        return jnp.dot(k_scr[...], q2t, preferred_element_type=F32)

    s_next = scores(0)
    for t in range(n_sub):
        rows = slice(t * sub_q, (t + 1) * sub_q)
        s = s_next
        if t + 1 < n_sub:
            s_next = scores(t + 1)
        e = jnp.exp2(s - jnp.max(s, axis=0, keepdims=True))
        r = 1.0 / jnp.sum(e, axis=0, keepdims=True)
        ovt = jnp.dot(vt_scr[...], e.astype(BF16), preferred_element_type=F32) * r
        ot = ovt[:, :sub_q] - lam * ovt[:, sub_q:]
        ot = ot * lax.rsqrt(jnp.mean(ot * ot, axis=0, keepdims=True) + NORM_EPS) * g_ref[...] * (1.0 - lambda_init)
        o_ref[rows, :] = ot.T.astype(o_ref.dtype)


def _diff_attn(p0, q_col0, k_col0, v_col0, n_heads, rope_tabs, lam_p, subln_g, lambda_init,
               n_batch, seq, ctx_len, with_lat, tq=1024, sub_q=256):
    cb = lambda c0: c0 // LANES
    ctx_rb0 = (n_batch * seq) // ctx_len
    n_keys = ctx_len + (seq if with_lat else 0)
    if with_lat:
        n_q = seq // tq
        q_spec = pl.BlockSpec((tq, LANES), lambda b, h, i: (b * n_q + i, cb(q_col0) + h))
    else:
        tq = ctx_len
        n_q = 1
        q_spec = pl.BlockSpec((tq, LANES), lambda b, h, i: (ctx_rb0 + b, cb(q_col0) + h))
    kc_spec = pl.BlockSpec((ctx_len, LANES), lambda b, h, i: (ctx_rb0 + b, cb(k_col0) + h))
    vc_spec = pl.BlockSpec((ctx_len, LANES), lambda b, h, i: (ctx_rb0 + b, cb(v_col0) + h))
    small =[pl.BlockSpec((8, LANES), lambda b, h, i: (0, 0)), pl.BlockSpec((LANES, sub_q), lambda b, h, i: (0, 0))]
    small_args = [lam_p, jnp.broadcast_to(subln_g[:, None], (LANES, sub_q))]
    if with_lat:
        kl_spec = pl.BlockSpec((seq, LANES), lambda b, h, i: (b, cb(k_col0) + h))
        vl_spec = pl.BlockSpec((seq, LANES), lambda b, h, i: (b, cb(v_col0) + h))
        qt = pl.BlockSpec((tq, LANES), lambda b, h, i: (i, 0))
        kt = pl.BlockSpec((seq, LANES), lambda b, h, i: (0, 0))
        in_specs = [q_spec, kc_spec, kl_spec, vc_spec, vl_spec, qt, qt, qt, kt, kt, kt] + small
        args = [p0, p0, p0, p0, p0, *rope_tabs, *rope_tabs] + small_args
    else:
        in_specs = [q_spec, kc_spec, vc_spec] + small
        args = [p0, p0, p0] + small_args
    block_bytes = 10 * seq * LANES * 4 if with_lat else 8 * ctx_len * LANES * 4
    return pl.pallas_call(
        functools.partial(_diff_attn_kernel, with_lat=with_lat, lambda_init=lambda_init, n_ctx=ctx_len,
                          sub_q=sub_q),
        grid=(n_batch, n_heads, n_q),
        in_specs=in_specs,
        out_specs=pl.BlockSpec((tq, LANES), lambda b, h, i: (b * n_q + i, h)),
        out_shape=jax.ShapeDtypeStruct((n_batch * n_q * tq, n_heads * LANES), BF16),
        scratch_shapes=[pltpu.VMEM((n_keys, LANES), BF16), pltpu.VMEM((LANES, n_keys), BF16)],
        compiler_params=_params(("arbitrary", "arbitrary", "arbitrary"), block_bytes),
        name="diff_attn_lat" if with_lat else "diff_attn_ctx",
    )(*args)


def _rope_tables(seq):
    half = DIFF_HEAD_DIM // 2
    pos = jnp.arange(seq)
    row, col = pos // GRID_W, pos % GRID_W
    inv_freq = ROPE_BASE ** (-jnp.arange(0, half, 2, dtype=F32) / half)
    lane = np.arange(LANES) % DIFF_HEAD_DIM
    use_col = jnp.asarray(lane >= half)
    first = jnp.asarray((lane % half) < half // 2)
    freq = inv_freq[jnp.asarray(lane % (half // 2))]
    p = jnp.where(use_col[None, :], col[:, None], row[:, None]).astype(F32)
    ang = p * freq[None, :]
    cos, sin = jnp.cos(ang), jnp.sin(ang)
    return cos, jnp.where(first[None, :], -sin, 0.0), jnp.where(first[None, :], 0.0, sin)


def _rpb_gather_kernel(rpb_ref, o_ref):
    n = o_ref.shape[1]
    k = rpb_ref.shape[1]
    colid = pl.program_id(0) * n + lax.broadcasted_iota(jnp.int32, (k, n), 1)
    j = lax.broadcasted_iota(jnp.int32, (k, n), 0)
    qc = lax.shift_right_logical(colid, int(math.log2(LANES)))
    half = lax.shift_right_logical(colid & (LANES - 1), int(math.log2(GRID_W)))
    kc = colid & (GRID_W - 1)
    sel = (j == half * LANES + jnp.clip(kc - qc + (WIN_COLS - 1), 0, 2 * WIN_COLS - 2)).astype(F32)
    o_ref[...] = jnp.dot(rpb_ref[...], sel, preferred_element_type=F32, precision=lax.Precision.HIGHEST)


def _rpb_pairs(rpb):
    nh, nr, ncol = rpb.shape
    n_slot = nr + 1
    left = jnp.pad(rpb, ((0, 0), (1, 0), (0, LANES - ncol)))
    right = jnp.pad(rpb, ((0, 0), (0, 1), (0, LANES - ncol)))
    rows = jnp.concatenate([left, right], axis=2).reshape(nh * n_slot, 2 * LANES)
    tn = 1024
    out = pl.pallas_call(
        _rpb_gather_kernel,
        grid=(GRID_W * LANES // tn,),
        in_specs=[pl.BlockSpec((nh * n_slot, 2 * LANES), lambda j: (0, 0))],
        out_specs=pl.BlockSpec((nh * n_slot, tn), lambda j: (0, j)),
        out_shape=jax.ShapeDtypeStruct((nh * n_slot, GRID_W * LANES), F32),
        compiler_params=_params(("arbitrary",), nh * n_slot * tn * 4 + 2 * LANES * tn * 4),
        name="rpb_gather",
    )(rows)
    return out.reshape(nh, n_slot, GRID_W, LANES)


def _na_block_plan(rows):
    kr = min(WIN_ROWS, rows)
    n_blk = rows // NA_Q_ROWS
    kb = np.clip(np.arange(n_blk) * NA_Q_ROWS - kr // 2, 0, rows - NA_K_ROWS)
    layouts, layout_of = [], []
    for blk in range(n_blk):
        dr = np.full((NA_Q_ROWS, NA_K_ROWS), -1, np.int64)
        for i in range(NA_Q_ROWS):
            r = blk * NA_Q_ROWS + i
            rs = int(np.clip(r - kr // 2, 0, rows - kr))
            for j in range(NA_K_ROWS):
                krow = kb[blk] + j
                if rs <= krow < rs + kr:
                    dr[i, j] = krow - r + WIN_ROWS - 1
        key = dr.tobytes()
        if key not in [l.tobytes() for l in layouts]:
            layouts.append(dr)
        layout_of.append([l.tobytes() for l in layouts].index(key))
    return kb, np.stack(layouts), np.asarray(layout_of)


def _na_bias_kernel(tp_ref, o_ref, *, layouts):
    qc = lax.broadcasted_iota(jnp.int32, (GRID_W, LANES), 0)
    lane = lax.broadcasted_iota(jnp.int32, (GRID_W, LANES), 1)
    kc = lane & (GRID_W - 1)
    left = lane < GRID_W
    col_start = jnp.clip(qc - WIN_COLS // 2, 0, GRID_W - WIN_COLS)
    col_ok = (kc >= col_start) & (kc < col_start + WIN_COLS)
    n_lay, n_q, n_k = layouts.shape
    for lay in range(n_lay):
        for i in range(n_q):
            for p in range(-(-n_k // 2)):
                d_l = int(layouts[lay, i, 2 * p])
                d_r = int(layouts[lay, i, 2 * p + 1]) if 2 * p + 1 < n_k else -1
                width = LANES if 2 * p + 1 < n_k else GRID_W
                if d_l < 0 and d_r < 0:
                    tile = jnp.full((GRID_W, LANES), -jnp.inf, F32)
                else:
                    assert d_l < 0 or d_r < 0 or d_r == d_l + 1
                    ok = col_ok
                    if d_l < 0:
                        ok = ok & jnp.logical_not(left)
                    if d_r < 0:
                        ok = ok & left
                    slot = d_r if d_r >= 0 else d_l + 1
                    tile = jnp.where(ok, tp_ref[slot] * LOG2E, -jnp.inf)
                o_ref[lay, i * GRID_W:(i + 1) * GRID_W, p * LANES:p * LANES + width] = tile[:, :width]


def _na_bias(rpb_pairs, layouts):
    nh, n_slot = rpb_pairs.shape[:2]
    n_lay = layouts.shape[0]
    tq, nkw = NA_Q_ROWS * GRID_W, NA_K_ROWS * GRID_W
    return pl.pallas_call(
        functools.partial(_na_bias_kernel, layouts=layouts),
        grid=(nh,),
        in_specs=[pl.BlockSpec((None, n_slot, GRID_W, LANES), lambda h: (h, 0, 0, 0))],
        out_specs=pl.BlockSpec((None, n_lay, tq, nkw), lambda h: (h, 0, 0, 0)),
        out_shape=jax.ShapeDtypeStruct((nh, n_lay, tq, nkw), F32),
        compiler_params=_params(("arbitrary",), n_lay * tq * nkw * 4),
        name="na_bias",
    )(rpb_pairs)


def _na_kernel(q_ref, kl_ref, vl_ref, kc_ref, vc_ref, bias_ref, o_ref, *, plan):
    tq = NA_Q_ROWS * GRID_W
    nkw = NA_K_ROWS * GRID_W
    nt = (((1,), (1,)), ((), ()))
    kc = kc_ref[...]
    vc = vc_ref[...]

    def scores(blk):
        kb, lay = plan[blk]
        q = q_ref[blk * tq:(blk + 1) * tq, :]
        ks = slice(kb * GRID_W, kb * GRID_W + nkw)
        return (lax.dot_general(q, kl_ref[ks, :], nt, preferred_element_type=F32) + bias_ref[lay],
                lax.dot_general(q, kc, nt, preferred_element_type=F32))

    def weighted_values(blk, e_w, e_c, l):
        kb, _ = plan[blk]
        ks = slice(kb * GRID_W, kb * GRID_W + nkw)
        o = (jnp.dot(e_w, vl_ref[ks, :], preferred_element_type=F32) + jnp.dot(e_c, vc, preferred_element_type=F32))
        o_ref[blk * tq:(blk + 1) * tq, :] = (o * (1.0 / l)).astype(o_ref.dtype)

    s_next = scores(0)
    pending = None
    for blk in range(len(plan)):
        s_w, s_c = s_next
        if blk + 1 < len(plan):
            s_next = scores(blk + 1)
        if pending is not None:
            weighted_values(blk - 1, *pending)
        m = jnp.maximum(jnp.max(s_w, axis=-1, keepdims=True), jnp.max(s_c, axis=-1, keepdims=True))
        e_w = jnp.exp2(s_w - m)
        e_c = jnp.exp2(s_c - m)
        l = jnp.sum(e_w, axis=-1, keepdims=True) + jnp.sum(e_c, axis=-1, keepdims=True)
        pending = (e_w.astype(BF16), e_c.astype(BF16), l)
    weighted_values(len(plan) - 1, *pending)


def _na_attn(p1, rpb, n_batch, seq, ctx_len, n_heads):
    rows = seq // GRID_W
    kb, layouts, layout_of = _na_block_plan(rows)
    bias = _na_bias(_rpb_pairs(rpb), layouts)
    n_lay = layouts.shape[0]
    tq = NA_Q_ROWS * GRID_W
    nkw = NA_K_ROWS * GRID_W
    ctx_rb0 = (n_batch * seq) // ctx_len
    plan = tuple((int(k), int(l)) for k, l in zip(kb, layout_of))
    return pl.pallas_call(
        functools.partial(_na_kernel, plan=plan),
        grid=(n_heads, n_batch),
        in_specs=[
            pl.BlockSpec((seq, LANES), lambda h, b: (b, h)),
            pl.BlockSpec((seq, LANES), lambda h, b: (b, n_heads + h)),
            pl.BlockSpec((seq, LANES), lambda h, b: (b, 2 * n_heads + h)),
            pl.BlockSpec((ctx_len, LANES), lambda h, b: (ctx_rb0 + b, n_heads + h)),
            pl.BlockSpec((ctx_len, LANES), lambda h, b: (ctx_rb0 + b, 2 * n_heads + h)),
            pl.BlockSpec((None, n_lay, tq, nkw), lambda h, b: (h, 0, 0, 0)),
        ],
        out_specs=pl.BlockSpec((seq, LANES), lambda h, b: (b, h)),
        out_shape=jax.ShapeDtypeStruct((n_batch * seq, n_heads * LANES), BF16),
        compiler_params=_params(("arbitrary", "arbitrary"), n_lay * tq * nkw * 4 + 4 * seq * LANES * 2),
        name="na_attn",
    )(p1, p1, p1, p1, p1, bias)


def _pack_rows(rows, n_rows=8):
    out = jnp.zeros((n_rows, LANES), F32)
    for r, v in enumerate(rows):
        out = out.at[r, :v.shape[0]].set(v.astype(F32))
    return out


def kernel(x, c, ctx, c_ctx, ada_w, ada_b, norm_mix_g, norm_ffn_g, final_norm_g, ffn_w1, ffn_w3, ffn_w2, ev_w_in, ev_conv_w, ev_conv_b, ev_a_log, ev_dt_bias, ev_d_skip, ev_ssm_norm_g, ev_lam_q1, ev_lam_k1, ev_lam_q2, ev_lam_k2, ev_subln_g, ev_w_out, od_w_in, od_rpb, od_w_out):
    n_batch, seq, d = x.shape
    ctx_len = ctx.shape[1]
    depth = ada_w.shape[0]
    n_lat = n_batch * seq
    n_tok = n_lat + n_batch * ctx_len

    d_ssm = ev_ssm_norm_g.shape[1]
    n_ssm_heads = ev_d_skip.shape[1]
    d_xbc = ev_conv_w.shape[2]
    d_qk = d_v = (ev_w_in.shape[2] - d_ssm - d_xbc - 2 * n_ssm_heads) // 3
    n_diff_heads = d_v // (2 * DIFF_HEAD_DIM)
    n_na_heads = od_rpb.shape[1]

    stream = (x.reshape(n_lat, d), ctx.reshape(n_batch * ctx_len, d))
    cond = jnp.zeros((COND_ROWS, d), F32).at[:n_batch].set(c).at[n_batch].set(c_ctx)
    mod = _ada_mod(cond, ada_w, ada_b).reshape(depth, COND_ROWS, 1, 6 * d)
    rope_tabs = _rope_tables(seq)
    kw = dict(seq=seq, n_batch=n_batch)
    ffn_w = (ffn_w1, ffn_w3, ffn_w2)

    for i in range(depth):
        ctx_out = i < depth - 1
        m_rows = n_tok if ctx_out else n_lat
        j = i // 2
        if i % 2 == 0:
            lambda_init = 0.8 - 0.6 * math.exp(-0.3 * i)
            w_in = ev_w_in[j]
            dt0 = d_ssm + d_xbc
            w_main = jnp.concatenate([w_in[:, :dt0].astype(BF16), w_in[:, dt0 + 2 * n_ssm_heads:].astype(BF16)], axis=1)
            w_dt = jnp.zeros((d, 2 * LANES), F32)
            w_dt = w_dt.at[:, :n_ssm_heads].set(w_in[:, dt0:dt0 + n_ssm_heads])
            w_dt = w_dt.at[:, LANES:LANES + n_ssm_heads].set(w_in[:, dt0 + n_ssm_heads:dt0 + 2 * n_ssm_heads])
            p0, dt_raw = _proj(stream, norm_mix_g[i], mod, i, w_main, w_dt.astype(BF16), F32, **kw)
            q0 = d_ssm + d_xbc
            k0, v0 = q0 + d_qk, q0 + 2 * d_qk

            xbc = _ssd_conv(p0, d_ssm, d_xbc, ev_conv_w[j], ev_conv_b[j], seq, ctx_len, n_lat)
            prm = jnp.stack([_pack_rows([ev_a_log[j, r], ev_dt_bias[j, r]]) for r in range(2)])
            y2 = _ssd_scan(xbc, dt_raw, prm, n_batch, seq, ctx_len, d_ssm)
            mix_ssd = _ssd_finish(y2, xbc, p0, jnp.repeat(ev_d_skip[j], SSM_HEAD_DIM), ev_ssm_norm_g[j], d_ssm)

            lam_p = _pack_rows([ev_lam_q1[j], ev_lam_k1[j], ev_lam_q2[j], ev_lam_k2[j]])
            attn_args = (p0, q0, k0, v0, n_diff_heads, rope_tabs, lam_p, ev_subln_g[j], lambda_init,
                         n_batch, seq, ctx_len)
            mix_attn = _diff_attn(*attn_args, with_lat=True)
            if ctx_out:
                mix_attn = (mix_attn, _diff_attn(*attn_args, with_lat=False))
            lhs = [mix_ssd, mix_attn]
            w_out = ev_w_out[j].astype(BF16)
        else:
            d_na = n_na_heads * NA_HEAD_DIM
            q_scale = jnp.where(jnp.arange(3 * d_na) < d_na, NA_HEAD_DIM ** -0.5 * LOG2E, 1.0).astype(F32)
            p1 = _proj(stream, norm_mix_g[i], mod, i, od_w_in[j].astype(BF16), None, BF16, col_scale=q_scale, **kw)
            assert not ctx_out, "context-query neighbourhood layers are not needed at this depth"
            lhs = [_na_attn(p1, od_rpb[j], n_batch, seq, ctx_len, n_na_heads)]
            w_out = od_w_out[j].astype(BF16)
        stream = _out_proj(lhs, w_out, stream, mod, i, 2, m_rows, **kw)
        stream = _ffn(stream, norm_ffn_g[i], mod, i, *ffn_w, final_norm_g, not ctx_out, m_rows, **kw)
    return stream.reshape(n_batch, seq, d)
```

```python
import functools
import math

import jax
import jax.numpy as jnp
import numpy as np
from jax import lax
from jax.experimental import pallas as pl
from jax.experimental.pallas import tpu as pltpu

F32 = jnp.float32
BF16 = jnp.bfloat16

GRID_W = 64
SSM_HEAD_DIM = 64
SSM_GROUPS = 4
SSM_STATE = 128
SSM_CONV = 5
SSM_CHUNK = 128
DIFF_HEAD_DIM = 64
NA_HEAD_DIM = 128
WIN_ROWS = 8
WIN_COLS = 16
ROPE_BASE = 10000.0
NORM_EPS = 1e-6
LOG2E = math.log2(math.e)

LANES = 128
V7X_VMEM_BYTES = 64 * 1024 * 1024
VMEM_HEADROOM_BYTES = 6 * 1024 * 1024

NA_Q_ROWS = 4
NA_K_ROWS = NA_Q_ROWS + WIN_ROWS - 1
COND_ROWS = 16


def _vmem_limit(block_bytes):
    return int(min(V7X_VMEM_BYTES - VMEM_HEADROOM_BYTES, max(32 * 1024 * 1024, 2 * block_bytes + 16 * 1024 * 1024)))


def _params(semantics, block_bytes):
    return pltpu.CompilerParams(dimension_semantics=semantics, vmem_limit_bytes=_vmem_limit(block_bytes))


def _silu(v):
    return v * jax.nn.sigmoid(v)


def _mod_row(i, tm, seq, n_batch):
    return jnp.minimum((i * tm) // seq, n_batch)


def _ada_kernel(cond_ref, w_ref, b_ref, o_ref):
    s = _silu(cond_ref[...]).astype(BF16)
    o_ref[...] = jnp.dot(s, w_ref[...].astype(BF16), preferred_element_type=F32) + b_ref[...]


def _ada_mod(cond, ada_w, ada_b, tn=1024):
    depth, d, n = ada_w.shape
    return pl.pallas_call(
        _ada_kernel,
        grid=(depth, n // tn),
        in_specs=[
            pl.BlockSpec((COND_ROWS, d), lambda l, j: (0, 0)),
            pl.BlockSpec((None, d, tn), lambda l, j: (l, 0, j)),
            pl.BlockSpec((None, 1, tn), lambda l, j: (l, 0, j)),
        ],
        out_specs=pl.BlockSpec((None, COND_ROWS, tn), lambda l, j: (l, 0, j)),
        out_shape=jax.ShapeDtypeStruct((depth, COND_ROWS, n), F32),
        compiler_params=_params(("arbitrary", "arbitrary"), d * tn * 4),
        name="ada_mod",
    )(cond, ada_w, ada_b.reshape(depth, 1, n))


def _norm_mod_into(x_ref, g_ref, mod_ref, h_ref, shift_idx, scale_idx, rows=128):
    d = x_ref.shape[-1]
    shift = mod_ref[:, shift_idx * d:(shift_idx + 1) * d]
    gain = g_ref[...] * (1.0 + mod_ref[:, scale_idx * d:(scale_idx + 1) * d])

    def body(r, carry):
        sl = pl.ds(pl.multiple_of(r * rows, rows), rows)
        x = x_ref[sl, :]
        inv = lax.rsqrt(jnp.mean(x * x, axis=-1, keepdims=True) + NORM_EPS)
        h_ref[sl, :] = (x * inv * gain + shift).astype(h_ref.dtype)
        return carry

    lax.fori_loop(0, x_ref.shape[0] // rows, body, 0)


def _proj_kernel(*refs, n_x, n_lat_tiles, has_aux, has_scale):
    rest = list(refs)
    x_refs = [rest.pop(0) for _ in range(n_x)]
    g_ref, mod_ref, w_ref = rest.pop(0), rest.pop(0), rest.pop(0)
    cs_ref = rest.pop(0) if has_scale else None
    if has_aux:
        waux_ref, o_ref, oaux_ref, h_ref = rest
    else:
        o_ref, h_ref = rest

    @pl.when(pl.program_id(1) == 0)
    def _():
        if n_x == 1:
            _norm_mod_into(x_refs[0], g_ref, mod_ref, h_ref, 0, 1)
        else:
            is_lat = pl.program_id(0) < n_lat_tiles
            pl.when(is_lat)(lambda: _norm_mod_into(x_refs[0], g_ref, mod_ref, h_ref, 0, 1))
            pl.when(jnp.logical_not(is_lat))(lambda: _norm_mod_into(x_refs[1], g_ref, mod_ref, h_ref, 0, 1))
        if has_aux:
            oaux_ref[...] = jnp.dot(h_ref[...], waux_ref[...], preferred_element_type=F32)

    acc = jnp.dot(h_ref[...], w_ref[...], preferred_element_type=F32)
    if has_scale:
        acc = acc * cs_ref[...]
    o_ref[...] = acc.astype(o_ref.dtype)


def _proj(x, g, mod, layer, w, w_aux, out_dtype, seq, n_batch, col_scale=None, tm=1024, tn=1024):
    d, n = w.shape
    m = sum(a.shape[0] for a in x) if isinstance(x, tuple) else x.shape[0]
    n_lat_tiles = (n_batch * seq) // tm
    has_aux = w_aux is not None
    has_scale = col_scale is not None
    mod_spec = pl.BlockSpec((None, None, 1, mod.shape[-1]),
                            lambda i, j: (layer, _mod_row(i, tm, seq, n_batch), 0, 0))
    in_specs, args = _row_operand(x, tm, n_lat_tiles, d, lambda j: 0)
    n_x = len(args)
    in_specs += [
        pl.BlockSpec((1, d), lambda i, j: (0, 0)),
        mod_spec,
        pl.BlockSpec((d, tn), lambda i, j: (0, j)),
    ]
    out_specs = [pl.BlockSpec((tm, tn), lambda i, j: (i, j))]
    out_shape = [jax.ShapeDtypeStruct((m, n), out_dtype)]
    args += [g.reshape(1, d), mod, w]
    if has_scale:
        in_specs.append(pl.BlockSpec((1, tn), lambda i, j: (0, j)))
        args.append(col_scale.reshape(1, n))
    if has_aux:
        na = w_aux.shape[1]
        in_specs.append(pl.BlockSpec((d, na), lambda i, j: (0, 0)))
        out_specs.append(pl.BlockSpec((tm, na), lambda i, j: (i, 0)))
        out_shape.append(jax.ShapeDtypeStruct((m, na), F32))
        args.append(w_aux)
    block_bytes = n_x * tm * d * 4 + d * tn * 2 + tm * tn * 4 + tm * d
    outs = pl.pallas_call(
        functools.partial(_proj_kernel, n_x=n_x, n_lat_tiles=n_lat_tiles, has_aux=has_aux, has_scale=has_scale),
        grid=(m // tm, n // tn),
        in_specs=in_specs,
        out_specs=out_specs,
        out_shape=out_shape,
        scratch_shapes=[pltpu.VMEM((tm, d), BF16)],
        compiler_params=_params(("arbitrary", "arbitrary"), block_bytes),
        name="proj",
    )(*args)
    return outs if has_aux else outs[0]


def _repack_kernel(w_ref, o_ref, odt_ref, *, dt0, n_heads):
    o_ref[:, :dt0] = w_ref[:, :dt0].astype(BF16)
    o_ref[:, dt0:] = w_ref[:, dt0 + 2 * n_heads:].astype(BF16)
    odt_ref[...] = jnp.zeros_like(odt_ref)
    odt_ref[:, :n_heads] = w_ref[:, dt0:dt0 + n_heads].astype(BF16)
    odt_ref[:, LANES:LANES + n_heads] = w_ref[:, dt0 + n_heads:dt0 + 2 * n_heads].astype(BF16)


def _repack_even_w_in(w_in, dt0, n_heads, tr=256):
    d, n = w_in.shape
    return pl.pallas_call(
        functools.partial(_repack_kernel, dt0=dt0, n_heads=n_heads),
        grid=(d // tr,),
        in_specs=[pl.BlockSpec((tr, n), lambda i: (i, 0))],
        out_specs=[pl.BlockSpec((tr, n - 2 * n_heads), lambda i: (i, 0)),
                   pl.BlockSpec((tr, 2 * LANES), lambda i: (i, 0))],
        out_shape=[jax.ShapeDtypeStruct((d, n - 2 * n_heads), BF16), jax.ShapeDtypeStruct((d, 2 * LANES), BF16)],
        compiler_params=_params(("arbitrary",), tr * n * 6),
        name="repack_w_in",
    )(w_in)


def _row_operand(a, tm, n_lat_tiles, width, col_of_j):
    if not isinstance(a, tuple):
        return [pl.BlockSpec((tm, width), lambda i, j: (i, col_of_j(j)))], [a]
    lat, ctx = a
    assert ctx.shape[0] == tm and lat.shape[0] == n_lat_tiles * tm

    def lat_index(i, j):
        return jnp.minimum(i, n_lat_tiles - 1), jnp.where(i < n_lat_tiles, col_of_j(j), 0)

    def ctx_index(i, j):
        return 0, jnp.where(i < n_lat_tiles, 0, col_of_j(j))

    return [pl.BlockSpec((tm, width), lat_index), pl.BlockSpec((tm, width), ctx_index)], [lat, ctx]


def _pick_rows(refs, n_lat_tiles):
    if len(refs) == 1:
        return refs[0][...]
    return jnp.where(pl.program_id(0) < n_lat_tiles, refs[0][...], refs[1][...])


def _out_proj_kernel(*refs, arity, n_lat_tiles):
    refs = list(refs)
    n_lhs = len(arity) - 1
    groups = [[refs.pop(0) for _ in range(n)] for n in arity[:-1]]
    ws = [refs.pop(0) for _ in range(n_lhs)]
    res = [refs.pop(0) for _ in range(arity[-1])]
    gate_ref, o_ref = refs
    acc = None
    for grp, w_ref in zip(groups, ws):
        part = jnp.dot(_pick_rows(grp, n_lat_tiles), w_ref[...], preferred_element_type=F32)
        acc = part if acc is None else acc + part
    o_ref[...] = _pick_rows(res, n_lat_tiles) + gate_ref[...] * acc


def _out_proj(lhs_list, w, res, mod, layer, gate_idx, m_rows, seq, n_batch, tm=1024, tn=1024):
    d = w.shape[1]
    n_lat_tiles = (n_batch * seq) // tm
    in_specs, args, arity, widths = [], [], [], []
    for a in lhs_list:
        kk = (a[0] if isinstance(a, tuple) else a).shape[1]
        sp, ar = _row_operand(a, tm, n_lat_tiles, kk, lambda j: 0)
        in_specs += sp
        args += ar
        arity.append(len(ar))
        widths.append(kk)
    row0 = 0
    for kk in widths:
        in_specs.append(pl.BlockSpec((kk, tn), lambda i, j, rb=row0 // kk: (rb, j)))
        args.append(w)
        row0 += kk
    sp, ar = _row_operand(res, tm, n_lat_tiles, tn, lambda j: j)
    in_specs += sp
    args += ar
    arity.append(len(ar))
    in_specs.append(pl.BlockSpec((None, None, 1, tn),
                                 lambda i, j: (layer, _mod_row(i, tm, seq, n_batch), 0, gate_idx * (d // tn) + j)))
    args.append(mod)
    k_total = sum(widths)
    block_bytes = 2 * tm * k_total * 2 + k_total * tn * 2 + 3 * tm * tn * 4
    return pl.pallas_call(
        functools.partial(_out_proj_kernel, arity=tuple(arity), n_lat_tiles=n_lat_tiles),
        grid=(m_rows // tm, d // tn),
        in_specs=in_specs,
        out_specs=pl.BlockSpec((tm, tn), lambda i, j: (i, j)),
        out_shape=jax.ShapeDtypeStruct((m_rows, d), F32),
        compiler_params=_params(("arbitrary", "arbitrary"), block_bytes),
        name="out_proj",
    )(*args)


def _ffn_kernel(x_ref, g_ref, mod_ref, w1_ref, w3_ref, w2_ref, fg_ref, o_ref, h_ref, *, final_norm, rows):
    j = pl.program_id(1)
    d = x_ref.shape[-1]

    @pl.when(j == 0)
    def _():
        _norm_mod_into(x_ref, g_ref, mod_ref, h_ref, 3, 4)
        o_ref[...] = jnp.zeros_like(o_ref)

    h = h_ref[...]
    a = jnp.dot(h, w1_ref[...].astype(BF16), preferred_element_type=F32)
    b = jnp.dot(h, w3_ref[...].astype(BF16), preferred_element_type=F32)
    u = (_silu(a) * b).astype(BF16)
    o_ref[...] += jnp.dot(u, w2_ref[...].astype(BF16), preferred_element_type=F32)

    @pl.when(j == pl.num_programs(1) - 1)
    def _():
        gate = mod_ref[:, 5 * d:6 * d]
        fg = fg_ref[...]

        def body(r, carry):
            sl = pl.ds(pl.multiple_of(r * rows, rows), rows)
            y = x_ref[sl, :] + gate * o_ref[sl, :]
            if final_norm:
                y = y * lax.rsqrt(jnp.mean(y * y, axis=-1, keepdims=True) + NORM_EPS) * fg
            o_ref[sl, :] = y
            return carry

        lax.fori_loop(0, x_ref.shape[0] // rows, body, 0)


def _ffn(x, g, mod, layer, w1, w3, w2, final_g, final_norm, m_rows, seq, n_batch, tm=1024, tf=256):
    d = x.shape[1]
    ff = w1.shape[2]
    block_bytes = 2 * tm * d * 4 + 3 * d * tf * w1.dtype.itemsize + tm * d
    return pl.pallas_call(
        functools.partial(_ffn_kernel, final_norm=final_norm, rows=128),
        grid=(m_rows // tm, ff // tf),
        in_specs=[
            pl.BlockSpec((tm, d), lambda i, j: (i, 0)),
            pl.BlockSpec((1, d), lambda i, j: (0, 0)),
            pl.BlockSpec((None, None, 1, mod.shape[-1]),
                         lambda i, j: (layer, _mod_row(i, tm, seq, n_batch), 0, 0)),
            pl.BlockSpec((None, d, tf), lambda i, j: (layer, 0, j)),
            pl.BlockSpec((None, d, tf), lambda i, j: (layer, 0, j)),
            pl.BlockSpec((None, tf, d), lambda i, j: (layer, j, 0)),
            pl.BlockSpec((1, d), lambda i, j: (0, 0)),
        ],
        out_specs=pl.BlockSpec((tm, d), lambda i, j: (i, 0)),
        out_shape=jax.ShapeDtypeStruct((m_rows, d), F32),
        scratch_shapes=[pltpu.VMEM((tm, d), BF16)],
        compiler_params=_params(("arbitrary", "arbitrary"), block_bytes),
        name="ffn",
    )(x, g.reshape(1, d), mod, w1, w3, w2, final_g.reshape(1, d))


def _conv_kernel(prev_ref, cur_ref, next_ref, w_ref, b_ref, o_ref, ext_ref, *, tm, halo, seq, ctx_len, n_lat_tiles):
    i = pl.program_id(0)
    ext_ref[0:halo, :] = prev_ref[...]
    ext_ref[halo:halo + tm, :] = cur_ref[...]
    ext_ref[halo + tm:, :] = next_ref[...]
    seg_len = jnp.where(i < n_lat_tiles, seq, ctx_len)
    pos = (i * tm + lax.broadcasted_iota(jnp.int32, (tm, 1), 0)) & (seg_len - 1)
    acc = jnp.zeros(cur_ref.shape, F32) + b_ref[...]
    half = SSM_CONV // 2
    for t in range(SSM_CONV):
        src = pos + (t - half)
        tap = ext_ref[halo - half + t:halo - half + t + tm, :]
        acc = acc + jnp.where((src >= 0) & (src < seg_len), tap, 0.0) * w_ref[t:t + 1, :]
    o_ref[...] = _silu(acc)


def _ssd_conv(p0, col0, width, conv_w, conv_b, seq, ctx_len, n_lat_rows, tm=1024, tc=512, halo=8):
    m = p0.shape[0]
    assert seq % tm == 0 and tm % ctx_len == 0 and (seq & (seq - 1)) == 0 and (ctx_len & (ctx_len - 1)) == 0
    cb0 = col0 // tc
    hb = tm // halo
    last_hb = m // halo - 1
    wpad = jnp.zeros((8, width), F32).at[:SSM_CONV].set(conv_w)
    return pl.pallas_call(
        functools.partial(_conv_kernel, tm=tm, halo=halo, seq=seq, ctx_len=ctx_len, n_lat_tiles=n_lat_rows // tm),
        grid=(m // tm, width // tc),
        in_specs=[
            pl.BlockSpec((halo, tc), lambda i, j: (jnp.maximum(i * hb - 1, 0), cb0 + j)),
            pl.BlockSpec((tm, tc), lambda i, j: (i, cb0 + j)),
            pl.BlockSpec((halo, tc), lambda i, j: (jnp.minimum((i + 1) * hb, last_hb), cb0 + j)),
            pl.BlockSpec((8, tc), lambda i, j: (0, j)),
            pl.BlockSpec((1, tc), lambda i, j: (0, j)),
        ],
        out_specs=pl.BlockSpec((tm, tc), lambda i, j: (i, j)),
        out_shape=jax.ShapeDtypeStruct((m, width), F32),
        scratch_shapes=[pltpu.VMEM((tm + 2 * halo, tc), F32)],
        compiler_params=_params(("arbitrary", "arbitrary"), 3 * tm * tc * 4),
        name="ssd_conv",
    )(p0, p0, p0, wpad, conv_b.reshape(1, width))


def _ssd_scan_kernel(xs_f, bm_f, cm_f, dt_f, xs_b, bm_b, cm_b, dt_b, prm_ref, yf_ref, yb_ref, state_ref, *, n_heads):
    @pl.when(pl.program_id(1) == 0)
    def _():
        state_ref[...] = jnp.zeros_like(state_ref)

    fwd = _ssd_chunk(xs_f, bm_f, cm_f, dt_f, prm_ref.at[0], yf_ref, state_ref.at[0], backward=False, n_heads=n_heads)
    bwd = _ssd_chunk(xs_b, bm_b, cm_b, dt_b, prm_ref.at[1], yb_ref, state_ref.at[1], backward=True, n_heads=n_heads)
    for g in range(SSM_GROUPS):
        fwd(g)
        bwd(g)


def _ssd_chunk(xs_ref, bm_ref, cm_ref, dt_ref, prm_ref, y_ref, state_ref, *, backward, n_heads):
    hp = SSM_HEAD_DIM
    rep = n_heads // SSM_GROUPS
    lc = SSM_CHUNK
    assert 2 * hp == LANES and rep % 2 == 0 and SSM_STATE == LANES and lc == LANES

    dt_in = dt_ref[...] + prm_ref[1:2, :]
    dt = jnp.maximum(dt_in, 0.0) + jnp.log1p(jnp.exp(-jnp.abs(dt_in)))
    dta = dt * (-jnp.exp(prm_ref[0:1, :]) * LOG2E)
    row = lax.broadcasted_iota(jnp.int32, (lc, lc), 0)
    col = lax.broadcasted_iota(jnp.int32, (lc, lc), 1)
    causal = (row <= col) if backward else (row >= col)
    a_cum = jnp.dot(causal.astype(F32), dta, preferred_element_type=F32, precision=lax.Precision.HIGHEST)
    a_tot = a_cum[0:1, :] if backward else a_cum[lc - 1:lc, :]
    w_end = dt * jnp.exp2(a_tot - a_cum)
    a_cum_t, dt_t, w_end_t = a_cum.T, dt.T, w_end.T
    lo = col < hp

    def do_group(g):
        gs = slice(g * SSM_STATE, (g + 1) * SSM_STATE)
        b_f = bm_ref[:, gs]
        c_g = cm_ref[:, gs].astype(BF16)
        cb = lax.dot_general(c_g, b_f.astype(BF16), (((1,), (1,)), ((), ())), preferred_element_type=F32)
        b_t = b_f.T
        cols_g = slice(g * rep * hp, (g + 1) * rep * hp)
        y_off = jnp.dot(c_g, state_ref[:, cols_g].astype(BF16), preferred_element_type=F32)
        for pr in range(rep // 2):
            cols = slice(g * rep * hp + pr * LANES, g * rep * hp + (pr + 1) * LANES)
            xs2 = xs_ref[:, cols].astype(BF16)
            y_d, s_n, e_a = [], [], []
            for h in (g * rep + 2 * pr, g * rep + 2 * pr + 1):
                a_col = jnp.broadcast_to(a_cum[:, h:h + 1], (lc, lc))
                decay = jnp.exp2(jnp.where(causal, a_col - a_cum_t[h:h + 1, :], -jnp.inf))
                m = (cb * decay * dt_t[h:h + 1, :]).astype(BF16)
                y_d.append(jnp.dot(m, xs2, preferred_element_type=F32))
                s_n.append(jnp.dot((b_t * w_end_t[h:h + 1, :]).astype(BF16), xs2, preferred_element_type=F32))
                e_a.append(jnp.exp2(a_col))
            h0 = g * rep + 2 * pr
            y_ref[:, cols] = (jnp.where(lo, y_d[0], y_d[1])
                              + y_off[:, pr * LANES:(pr + 1) * LANES] * jnp.where(lo, e_a[0], e_a[1]))
            chunk_decay = jnp.where(lo[0:1, :], jnp.exp2(a_tot[:, h0:h0 + 1]), jnp.exp2(a_tot[:, h0 + 1:h0 + 2]))
            state_ref[:, cols] = state_ref[:, cols] * chunk_decay + jnp.where(lo, s_n[0], s_n[1])

    return do_group


def _ssd_scan(xbc, dt_raw, prm, n_batch, seq, ctx_len, d_ssm):
    m = xbc.shape[0]
    lc = SSM_CHUNK
    n_heads = d_ssm // SSM_HEAD_DIM
    nc_ctx, nc_lat = ctx_len // lc, seq // lc
    ctx_blk0 = n_batch * nc_lat
    gn = SSM_GROUPS * SSM_STATE

    def row_blk(b, s, backward):
        ctx_c = nc_ctx - 1 - s if backward else s
        lat_c = nc_lat - 1 - (s - nc_ctx) if backward else s - nc_ctx
        return jnp.where(s < nc_ctx, ctx_blk0 + b * nc_ctx + ctx_c, b * nc_lat + lat_c)

    def chunk_specs(backward):
        r = int(backward)
        return [
            pl.BlockSpec((lc, d_ssm), lambda b, s: (row_blk(b, s, backward), 0)),
            pl.BlockSpec((lc, gn), lambda b, s: (row_blk(b, s, backward), d_ssm // gn)),
            pl.BlockSpec((lc, gn), lambda b, s: (row_blk(b, s, backward), d_ssm // gn + 1)),
            pl.BlockSpec((lc, LANES), lambda b, s: (row_blk(b, s, backward), r)),
        ]

    return pl.pallas_call(
        functools.partial(_ssd_scan_kernel, n_heads=n_heads),
        grid=(n_batch, nc_ctx + nc_lat),
        in_specs=chunk_specs(False) + chunk_specs(True) + [pl.BlockSpec((2, 8, LANES), lambda b, s: (0, 0, 0))],
        out_specs=[pl.BlockSpec((lc, d_ssm), lambda b, s: (row_blk(b, s, False), 0)),
                   pl.BlockSpec((lc, d_ssm), lambda b, s: (row_blk(b, s, True), 0))],
        out_shape=[jax.ShapeDtypeStruct((m, d_ssm), F32), jax.ShapeDtypeStruct((m, d_ssm), F32)],
        scratch_shapes=[pltpu.VMEM((2, SSM_STATE, d_ssm), F32)],
        compiler_params=_params(("arbitrary", "arbitrary"), 8 * lc * d_ssm * 4),
        name="ssd_scan",
    )(xbc, xbc, xbc, dt_raw, xbc, xbc, xbc, dt_raw, prm)


def _ssd_finish_kernel(yf_ref, yb_ref, xs_ref, z_ref, dskip_ref, g_ref, o_ref, *, group_width):
    y = (yf_ref[...] + yb_ref[...] + dskip_ref[...] * xs_ref[...]) * _silu(z_ref[...])
    for g in range(y.shape[1] // group_width):
        sl = slice(g * group_width, (g + 1) * group_width)
        v = y[:, sl]
        vn = v * lax.rsqrt(jnp.mean(v * v, axis=-1, keepdims=True) + NORM_EPS) * g_ref[:, sl]
        o_ref[:, sl] = vn.astype(o_ref.dtype)


def _ssd_finish(y2, xbc, p0, d_skip_lanes, norm_g, d_ssm, tm=512):
    m = xbc.shape[0]
    return pl.pallas_call(
        functools.partial(_ssd_finish_kernel, group_width=d_ssm // SSM_GROUPS),
        grid=(m // tm,),
        in_specs=[
            pl.BlockSpec((tm, d_ssm), lambda i: (i, 0)),
            pl.BlockSpec((tm, d_ssm), lambda i: (i, 0)),
            pl.BlockSpec((tm, d_ssm), lambda i: (i, 0)),
            pl.BlockSpec((tm, d_ssm), lambda i: (i, 0)),
            pl.BlockSpec((1, d_ssm), lambda i: (0, 0)),
            pl.BlockSpec((1, d_ssm), lambda i: (0, 0)),
        ],
        out_specs=pl.BlockSpec((tm, d_ssm), lambda i: (i, 0)),
        out_shape=jax.ShapeDtypeStruct((m, d_ssm), BF16),
        compiler_params=_params(("arbitrary",), 5 * tm * d_ssm * 4),
        name="ssd_finish",
    )(*y2, xbc, p0, d_skip_lanes.reshape(1, d_ssm), norm_g.reshape(1, d_ssm))


def _rope(x, cos, sin_up, sin_dn):
    quarter = DIFF_HEAD_DIM // 4
    return (x * cos + pltpu.roll(x, LANES - quarter, axis=1) * sin_up + pltpu.roll(x, quarter, axis=1) * sin_dn)


def _diff_attn_kernel(*refs, with_lat, lambda_init, n_ctx, sub_q):
    if with_lat:
        (q_ref, kc_ref, kl_ref, vc_ref, vl_ref, qcos_ref, qsu_ref, qsd_ref, kcos_ref, ksu_ref, ksd_ref,
         lam_ref, g_ref, o_ref, k_scr, vt_scr) = refs
    else:
        q_ref, kc_ref, vc_ref, lam_ref, g_ref, o_ref, k_scr, vt_scr = refs

    @pl.when(pl.program_id(2) == 0)
    def _():
        k_scr[0:n_ctx, :] = kc_ref[...].astype(BF16)
        vt_scr[:, 0:n_ctx] = vc_ref[...].T.astype(BF16)
        if with_lat:
            k_scr[n_ctx:, :] = _rope(kl_ref[...], kcos_ref[...], ksu_ref[...], ksd_ref[...]).astype(BF16)
            vt_scr[:, n_ctx:] = vl_ref[...].T.astype(BF16)

    lam_p = lam_ref[...]
    lam = (jnp.exp(jnp.sum(lam_p[0:1, :] * lam_p[1:2, :], axis=-1, keepdims=True))
           - jnp.exp(jnp.sum(lam_p[2:3, :] * lam_p[3:4, :], axis=-1, keepdims=True)) + lambda_init)
    first = lax.broadcasted_iota(jnp.int32, (LANES, sub_q), 0) < DIFF_HEAD_DIM
    n_sub = q_ref.shape[0] // sub_q

    def scores(t):
        rows = slice(t * sub_q, (t + 1) * sub_q)
        q = q_ref[rows, :]
        if with_lat:
            q = _rope(q, qcos_ref[rows, :], qsu_ref[rows, :], qsd_ref[rows, :])
        qt = (q * (DIFF_HEAD_DIM ** -0.5 * LOG2E)).T
        q2t = jnp.concatenate([jnp.where(first, qt, 0.0), jnp.where(first, 0.0, qt)], axis=1).astype(BF16)
        return jnp.dot(k_scr[...], q2t, preferred_element_type=F32)

    s_next = scores(0)
    for t in range(n_sub):
        rows = slice(t * sub_q, (t + 1) * sub_q)
        s = s_next
        if t + 1 < n_sub:
            s_next = scores(t + 1)
        e = jnp.exp2(s - jnp.max(s, axis=0, keepdims=True))
        r = 1.0 / jnp.sum(e, axis=0, keepdims=True)
        ovt = jnp.dot(vt_scr[...], e.astype(BF16), preferred_element_type=F32) * r
        ot = ovt[:, :sub_q] - lam * ovt[:, sub_q:]
        ot = ot * lax.rsqrt(jnp.mean(ot * ot, axis=0, keepdims=True) + NORM_EPS) * g_ref[...] * (1.0 - lambda_init)
        o_ref[rows, :] = ot.T.astype(o_ref.dtype)


def _diff_attn(p0, q_col0, k_col0, v_col0, n_heads, rope_tabs, lam_p, subln_g, lambda_init,
               n_batch, seq, ctx_len, with_lat, tq=1024, sub_q=256):
    cb = lambda c0: c0 // LANES
    ctx_rb0 = (n_batch * seq) // ctx_len
    n_keys = ctx_len + (seq if with_lat else 0)
    if with_lat:
        n_q = seq // tq
        q_spec = pl.BlockSpec((tq, LANES), lambda b, h, i: (b * n_q + i, cb(q_col0) + h))
    else:
        tq = ctx_len
        n_q = 1
        q_spec = pl.BlockSpec((tq, LANES), lambda b, h, i: (ctx_rb0 + b, cb(q_col0) + h))
    kc_spec = pl.BlockSpec((ctx_len, LANES), lambda b, h, i: (ctx_rb0 + b, cb(k_col0) + h))
    vc_spec = pl.BlockSpec((ctx_len, LANES), lambda b, h, i: (ctx_rb0 + b, cb(v_col0) + h))
    small =[pl.BlockSpec((8, LANES), lambda b, h, i: (0, 0)), pl.BlockSpec((LANES, sub_q), lambda b, h, i: (0, 0))]
    small_args = [lam_p, jnp.broadcast_to(subln_g[:, None], (LANES, sub_q))]
    if with_lat:
        kl_spec = pl.BlockSpec((seq, LANES), lambda b, h, i: (b, cb(k_col0) + h))
        vl_spec = pl.BlockSpec((seq, LANES), lambda b, h, i: (b, cb(v_col0) + h))
        qt = pl.BlockSpec((tq, LANES), lambda b, h, i: (i, 0))
        kt = pl.BlockSpec((seq, LANES), lambda b, h, i: (0, 0))
        in_specs = [q_spec, kc_spec, kl_spec, vc_spec, vl_spec, qt, qt, qt, kt, kt, kt] + small
        args = [p0, p0, p0, p0, p0, *rope_tabs, *rope_tabs] + small_args
    else:
        in_specs = [q_spec, kc_spec, vc_spec] + small
        args = [p0, p0, p0] + small_args
    block_bytes = 10 * seq * LANES * 4 if with_lat else 8 * ctx_len * LANES * 4
    return pl.pallas_call(
        functools.partial(_diff_attn_kernel, with_lat=with_lat, lambda_init=lambda_init, n_ctx=ctx_len,
                          sub_q=sub_q),
        grid=(n_batch, n_heads, n_q),
        in_specs=in_specs,
        out_specs=pl.BlockSpec((tq, LANES), lambda b, h, i: (b * n_q + i, h)),
        out_shape=jax.ShapeDtypeStruct((n_batch * n_q * tq, n_heads * LANES), BF16),
        scratch_shapes=[pltpu.VMEM((n_keys, LANES), BF16), pltpu.VMEM((LANES, n_keys), BF16)],
        compiler_params=_params(("arbitrary", "arbitrary", "arbitrary"), block_bytes),
        name="diff_attn_lat" if with_lat else "diff_attn_ctx",
    )(*args)


def _rope_tables(seq):
    half = DIFF_HEAD_DIM // 2
    pos = jnp.arange(seq)
    row, col = pos // GRID_W, pos % GRID_W
    inv_freq = ROPE_BASE ** (-jnp.arange(0, half, 2, dtype=F32) / half)
    lane = np.arange(LANES) % DIFF_HEAD_DIM
    use_col = jnp.asarray(lane >= half)
    first = jnp.asarray((lane % half) < half // 2)
    freq = inv_freq[jnp.asarray(lane % (half // 2))]
    p = jnp.where(use_col[None, :], col[:, None], row[:, None]).astype(F32)
    ang = p * freq[None, :]
    cos, sin = jnp.cos(ang), jnp.sin(ang)
    return cos, jnp.where(first[None, :], -sin, 0.0), jnp.where(first[None, :], 0.0, sin)


def _rpb_gather_kernel(rpb_ref, o_ref):
    n = o_ref.shape[1]
    k = rpb_ref.shape[1]
    colid = pl.program_id(0) * n + lax.broadcasted_iota(jnp.int32, (k, n), 1)
    j = lax.broadcasted_iota(jnp.int32, (k, n), 0)
    qc = lax.shift_right_logical(colid, int(math.log2(LANES)))
    half = lax.shift_right_logical(colid & (LANES - 1), int(math.log2(GRID_W)))
    kc = colid & (GRID_W - 1)
    sel = (j == half * LANES + jnp.clip(kc - qc + (WIN_COLS - 1), 0, 2 * WIN_COLS - 2)).astype(F32)
    o_ref[...] = jnp.dot(rpb_ref[...], sel, preferred_element_type=F32, precision=lax.Precision.HIGHEST)


def _rpb_pairs(rpb):
    nh, nr, ncol = rpb.shape
    n_slot = nr + 1
    left = jnp.pad(rpb, ((0, 0), (1, 0), (0, LANES - ncol)))
    right = jnp.pad(rpb, ((0, 0), (0, 1), (0, LANES - ncol)))
    rows = jnp.concatenate([left, right], axis=2).reshape(nh * n_slot, 2 * LANES)
    tn = 1024
    out = pl.pallas_call(
        _rpb_gather_kernel,
        grid=(GRID_W * LANES // tn,),
        in_specs=[pl.BlockSpec((nh * n_slot, 2 * LANES), lambda j: (0, 0))],
        out_specs=pl.BlockSpec((nh * n_slot, tn), lambda j: (0, j)),
        out_shape=jax.ShapeDtypeStruct((nh * n_slot, GRID_W * LANES), F32),
        compiler_params=_params(("arbitrary",), nh * n_slot * tn * 4 + 2 * LANES * tn * 4),
        name="rpb_gather",
    )(rows)
    return out.reshape(nh, n_slot, GRID_W, LANES)


def _na_block_plan(rows):
    kr = min(WIN_ROWS, rows)
    n_blk = rows // NA_Q_ROWS
    kb = np.clip(np.arange(n_blk) * NA_Q_ROWS - kr // 2, 0, rows - NA_K_ROWS)
    layouts, layout_of = [], []
    for blk in range(n_blk):
        dr = np.full((NA_Q_ROWS, NA_K_ROWS), -1, np.int64)
        for i in range(NA_Q_ROWS):
            r = blk * NA_Q_ROWS + i
            rs = int(np.clip(r - kr // 2, 0, rows - kr))
            for j in range(NA_K_ROWS):
                krow = kb[blk] + j
                if rs <= krow < rs + kr:
                    dr[i, j] = krow - r + WIN_ROWS - 1
        key = dr.tobytes()
        if key not in [l.tobytes() for l in layouts]:
            layouts.append(dr)
        layout_of.append([l.tobytes() for l in layouts].index(key))
    return kb, np.stack(layouts), np.asarray(layout_of)


def _na_bias_kernel(tp_ref, o_ref, *, layouts):
    qc = lax.broadcasted_iota(jnp.int32, (GRID_W, LANES), 0)
    lane = lax.broadcasted_iota(jnp.int32, (GRID_W, LANES), 1)
    kc = lane & (GRID_W - 1)
    left = lane < GRID_W
    col_start = jnp.clip(qc - WIN_COLS // 2, 0, GRID_W - WIN_COLS)
    col_ok = (kc >= col_start) & (kc < col_start + WIN_COLS)
    n_lay, n_q, n_k = layouts.shape
    for lay in range(n_lay):
        for i in range(n_q):
            for p in range(-(-n_k // 2)):
                d_l = int(layouts[lay, i, 2 * p])
                d_r = int(layouts[lay, i, 2 * p + 1]) if 2 * p + 1 < n_k else -1
                width = LANES if 2 * p + 1 < n_k else GRID_W
                if d_l < 0 and d_r < 0:
                    tile = jnp.full((GRID_W, LANES), -jnp.inf, F32)
                else:
                    assert d_l < 0 or d_r < 0 or d_r == d_l + 1
                    ok = col_ok
                    if d_l < 0:
                        ok = ok & jnp.logical_not(left)
                    if d_r < 0:
                        ok = ok & left
                    slot = d_r if d_r >= 0 else d_l + 1
                    tile = jnp.where(ok, tp_ref[slot] * LOG2E, -jnp.inf)
                o_ref[lay, i * GRID_W:(i + 1) * GRID_W, p * LANES:p * LANES + width] = tile[:, :width]


def _na_bias(rpb_pairs, layouts):
    nh, n_slot = rpb_pairs.shape[:2]
    n_lay = layouts.shape[0]
    tq, nkw = NA_Q_ROWS * GRID_W, NA_K_ROWS * GRID_W
    return pl.pallas_call(
        functools.partial(_na_bias_kernel, layouts=layouts),
        grid=(nh,),
        in_specs=[pl.BlockSpec((None, n_slot, GRID_W, LANES), lambda h: (h, 0, 0, 0))],
        out_specs=pl.BlockSpec((None, n_lay, tq, nkw), lambda h: (h, 0, 0, 0)),
        out_shape=jax.ShapeDtypeStruct((nh, n_lay, tq, nkw), F32),
        compiler_params=_params(("arbitrary",), n_lay * tq * nkw * 4),
        name="na_bias",
    )(rpb_pairs)


def _na_kernel(q_ref, kl_ref, vl_ref, kc_ref, vc_ref, bias_ref, o_ref, *, plan):
    tq = NA_Q_ROWS * GRID_W
    nkw = NA_K_ROWS * GRID_W
    nt = (((1,), (1,)), ((), ()))
    kc = kc_ref[...]
    vc = vc_ref[...]

    def scores(blk):
        kb, lay = plan[blk]
        q = q_ref[blk * tq:(blk + 1) * tq, :]
        ks = slice(kb * GRID_W, kb * GRID_W + nkw)
        return (lax.dot_general(q, kl_ref[ks, :], nt, preferred_element_type=F32) + bias_ref[lay],
                lax.dot_general(q, kc, nt, preferred_element_type=F32))

    def weighted_values(blk, e_w, e_c, l):
        kb, _ = plan[blk]
        ks = slice(kb * GRID_W, kb * GRID_W + nkw)
        o = (jnp.dot(e_w, vl_ref[ks, :], preferred_element_type=F32) + jnp.dot(e_c, vc, preferred_element_type=F32))
        o_ref[blk * tq:(blk + 1) * tq, :] = (o * (1.0 / l)).astype(o_ref.dtype)

    s_next = scores(0)
    pending = None
    for blk in range(len(plan)):
        s_w, s_c = s_next
        if blk + 1 < len(plan):
            s_next = scores(blk + 1)
        if pending is not None:
            weighted_values(blk - 1, *pending)
        m = jnp.maximum(jnp.max(s_w, axis=-1, keepdims=True), jnp.max(s_c, axis=-1, keepdims=True))
        e_w = jnp.exp2(s_w - m)
        e_c = jnp.exp2(s_c - m)
        l = jnp.sum(e_w, axis=-1, keepdims=True) + jnp.sum(e_c, axis=-1, keepdims=True)
        pending = (e_w.astype(BF16), e_c.astype(BF16), l)
    weighted_values(len(plan) - 1, *pending)


def _na_attn(p1, rpb, n_batch, seq, ctx_len, n_heads):
    rows = seq // GRID_W
    kb, layouts, layout_of = _na_block_plan(rows)
    bias = _na_bias(_rpb_pairs(rpb), layouts)
    n_lay = layouts.shape[0]
    tq = NA_Q_ROWS * GRID_W
    nkw = NA_K_ROWS * GRID_W
    ctx_rb0 = (n_batch * seq) // ctx_len
    plan = tuple((int(k), int(l)) for k, l in zip(kb, layout_of))
    return pl.pallas_call(
        functools.partial(_na_kernel, plan=plan),
        grid=(n_heads, n_batch),
        in_specs=[
            pl.BlockSpec((seq, LANES), lambda h, b: (b, h)),
            pl.BlockSpec((seq, LANES), lambda h, b: (b, n_heads + h)),
            pl.BlockSpec((seq, LANES), lambda h, b: (b, 2 * n_heads + h)),
            pl.BlockSpec((ctx_len, LANES), lambda h, b: (ctx_rb0 + b, n_heads + h)),
            pl.BlockSpec((ctx_len, LANES), lambda h, b: (ctx_rb0 + b, 2 * n_heads + h)),
            pl.BlockSpec((None, n_lay, tq, nkw), lambda h, b: (h, 0, 0, 0)),
        ],
        out_specs=pl.BlockSpec((seq, LANES), lambda h, b: (b, h)),
        out_shape=jax.ShapeDtypeStruct((n_batch * seq, n_heads * LANES), BF16),
        compiler_params=_params(("arbitrary", "arbitrary"), n_lay * tq * nkw * 4 + 4 * seq * LANES * 2),
        name="na_attn",
    )(p1, p1, p1, p1, p1, bias)


def _pack_rows(rows, n_rows=8):
    out = jnp.zeros((n_rows, LANES), F32)
    for r, v in enumerate(rows):
        out = out.at[r, :v.shape[0]].set(v.astype(F32))
    return out


def kernel(x, c, ctx, c_ctx, ada_w, ada_b, norm_mix_g, norm_ffn_g, final_norm_g, ffn_w1, ffn_w3, ffn_w2, ev_w_in, ev_conv_w, ev_conv_b, ev_a_log, ev_dt_bias, ev_d_skip, ev_ssm_norm_g, ev_lam_q1, ev_lam_k1, ev_lam_q2, ev_lam_k2, ev_subln_g, ev_w_out, od_w_in, od_rpb, od_w_out):
    n_batch, seq, d = x.shape
    ctx_len = ctx.shape[1]
    depth = ada_w.shape[0]
    n_lat = n_batch * seq
    n_tok = n_lat + n_batch * ctx_len

    d_ssm = ev_ssm_norm_g.shape[1]
    n_ssm_heads = ev_d_skip.shape[1]
    d_xbc = ev_conv_w.shape[2]
    d_qk = d_v = (ev_w_in.shape[2] - d_ssm - d_xbc - 2 * n_ssm_heads) // 3
    n_diff_heads = d_v // (2 * DIFF_HEAD_DIM)
    n_na_heads = od_rpb.shape[1]

    stream = (x.reshape(n_lat, d), ctx.reshape(n_batch * ctx_len, d))
    cond = jnp.zeros((COND_ROWS, d), F32).at[:n_batch].set(c).at[n_batch].set(c_ctx)
    mod = _ada_mod(cond, ada_w, ada_b).reshape(depth, COND_ROWS, 1, 6 * d)
    rope_tabs = _rope_tables(seq)
    kw = dict(seq=seq, n_batch=n_batch)
    ffn_w = (ffn_w1, ffn_w3, ffn_w2)

    for i in range(depth):
        ctx_out = i < depth - 1
        m_rows = n_tok if ctx_out else n_lat
        j = i // 2
        if i % 2 == 0:
            lambda_init = 0.8 - 0.6 * math.exp(-0.3 * i)
            w_in = ev_w_in[j]
            dt0 = d_ssm + d_xbc
            w_main, w_dt = _repack_even_w_in(w_in, dt0, n_ssm_heads)
            p0, dt_raw = _proj(stream, norm_mix_g[i], mod, i, w_main, w_dt, F32, **kw)
            q0 = d_ssm + d_xbc
            k0, v0 = q0 + d_qk, q0 + 2 * d_qk

            xbc = _ssd_conv(p0, d_ssm, d_xbc, ev_conv_w[j], ev_conv_b[j], seq, ctx_len, n_lat)
            prm = jnp.stack([_pack_rows([ev_a_log[j, r], ev_dt_bias[j, r]]) for r in range(2)])
            y2 = _ssd_scan(xbc, dt_raw, prm, n_batch, seq, ctx_len, d_ssm)
            mix_ssd = _ssd_finish(y2, xbc, p0, jnp.repeat(ev_d_skip[j], SSM_HEAD_DIM), ev_ssm_norm_g[j], d_ssm)

            lam_p = _pack_rows([ev_lam_q1[j], ev_lam_k1[j], ev_lam_q2[j], ev_lam_k2[j]])
            attn_args = (p0, q0, k0, v0, n_diff_heads, rope_tabs, lam_p, ev_subln_g[j], lambda_init,
                         n_batch, seq, ctx_len)
            mix_attn = _diff_attn(*attn_args, with_lat=True)
            if ctx_out:
                mix_attn = (mix_attn, _diff_attn(*attn_args, with_lat=False))
            lhs = [mix_ssd, mix_attn]
            w_out = ev_w_out[j].astype(BF16)
        else:
            d_na = n_na_heads * NA_HEAD_DIM
            q_scale = jnp.where(jnp.arange(3 * d_na) < d_na, NA_HEAD_DIM ** -0.5 * LOG2E, 1.0).astype(F32)
            p1 = _proj(stream, norm_mix_g[i], mod, i, od_w_in[j].astype(BF16), None, BF16, col_scale=q_scale, **kw)
            assert not ctx_out, "context-query neighbourhood layers are not needed at this depth"
            lhs = [_na_attn(p1, od_rpb[j], n_batch, seq, ctx_len, n_na_heads)]
            w_out = od_w_out[j].astype(BF16)
        stream = _out_proj(lhs, w_out, stream, mod, i, 2, m_rows, **kw)
        stream = _ffn(stream, norm_ffn_g[i], mod, i, *ffn_w, final_norm_g, not ctx_out, m_rows, **kw)
    return stream.reshape(n_batch, seq, d)
```

```python
import functools
import math

import jax
import jax.numpy as jnp
import numpy as np
from jax import lax
from jax.experimental import pallas as pl
from jax.experimental.pallas import tpu as pltpu

F32 = jnp.float32
BF16 = jnp.bfloat16

GRID_W = 64
SSM_HEAD_DIM = 64
SSM_GROUPS = 4
SSM_STATE = 128
SSM_CONV = 5
SSM_CHUNK = 128
DIFF_HEAD_DIM = 64
NA_HEAD_DIM = 128
WIN_ROWS = 8
WIN_COLS = 16
ROPE_BASE = 10000.0
NORM_EPS = 1e-6
LOG2E = math.log2(math.e)

LANES = 128
V7X_VMEM_BYTES = 64 * 1024 * 1024
VMEM_HEADROOM_BYTES = 3 * 1024 * 1024

NA_Q_ROWS = 4
NA_K_ROWS = NA_Q_ROWS + WIN_ROWS - 1
COND_ROWS = 16


def _vmem_limit(block_bytes):
    return int(min(V7X_VMEM_BYTES - VMEM_HEADROOM_BYTES, max(32 * 1024 * 1024, 2 * block_bytes + 16 * 1024 * 1024)))


def _params(semantics, block_bytes):
    return pltpu.CompilerParams(dimension_semantics=semantics, vmem_limit_bytes=_vmem_limit(block_bytes))


def _silu(v):
    return v * jax.nn.sigmoid(v)


def _mod_row(i, tm, seq, n_batch):
    return jnp.minimum((i * tm) // seq, n_batch)


def _ada_kernel(cond_ref, w_ref, b_ref, o_ref):
    s = _silu(cond_ref[...]).astype(BF16)
    o_ref[...] = jnp.dot(s, w_ref[...].astype(BF16), preferred_element_type=F32) + b_ref[...]


def _ada_mod(cond, ada_w, ada_b, tn=1024):
    depth, d, n = ada_w.shape
    return pl.pallas_call(
        _ada_kernel,
        grid=(depth, n // tn),
        in_specs=[
            pl.BlockSpec((COND_ROWS, d), lambda l, j: (0, 0)),
            pl.BlockSpec((None, d, tn), lambda l, j: (l, 0, j)),
            pl.BlockSpec((None, 1, tn), lambda l, j: (l, 0, j)),
        ],
        out_specs=pl.BlockSpec((None, COND_ROWS, tn), lambda l, j: (l, 0, j)),
        out_shape=jax.ShapeDtypeStruct((depth, COND_ROWS, n), F32),
        compiler_params=_params(("arbitrary", "arbitrary"), d * tn * 4),
        name="ada_mod",
    )(cond, ada_w, ada_b.reshape(depth, 1, n))


def _norm_mod_into(x_ref, g_ref, mod_ref, h_ref, shift_idx, scale_idx, rows=128):
    d = x_ref.shape[-1]
    shift = mod_ref[:, shift_idx * d:(shift_idx + 1) * d]
    gain = g_ref[...] * (1.0 + mod_ref[:, scale_idx * d:(scale_idx + 1) * d])

    def body(r, carry):
        sl = pl.ds(pl.multiple_of(r * rows, rows), rows)
        x = x_ref[sl, :]
        inv = lax.rsqrt(jnp.mean(x * x, axis=-1, keepdims=True) + NORM_EPS)
        h_ref[sl, :] = (x * inv * gain + shift).astype(h_ref.dtype)
        return carry

    lax.fori_loop(0, x_ref.shape[0] // rows, body, 0)


def _proj_kernel(*refs, n_x, n_w, tiles_per_part, n_lat_tiles, has_aux, has_scale):
    rest = list(refs)
    x_refs = [rest.pop(0) for _ in range(n_x)]
    g_ref, mod_ref = rest.pop(0), rest.pop(0)
    w_refs = [rest.pop(0) for _ in range(max(n_w, 1))]
    cs_ref = rest.pop(0) if has_scale else None
    if has_aux:
        waux_ref, o_ref, oaux_ref, h_ref = rest
    else:
        o_ref, h_ref = rest
    nt = (((1,), (1,)), ((), ()))
    j = pl.program_id(1)

    @pl.when(j == 0)
    def _():
        if n_x == 1:
            _norm_mod_into(x_refs[0], g_ref, mod_ref, h_ref, 0, 1)
        else:
            is_lat = pl.program_id(0) < n_lat_tiles
            pl.when(is_lat)(lambda: _norm_mod_into(x_refs[0], g_ref, mod_ref, h_ref, 0, 1))
            pl.when(jnp.logical_not(is_lat))(lambda: _norm_mod_into(x_refs[1], g_ref, mod_ref, h_ref, 0, 1))
        if has_aux:
            dims = nt if n_w else (((1,), (0,)), ((), ()))
            oaux_ref[...] = lax.dot_general(h_ref[...], waux_ref[...], dims, preferred_element_type=F32)

    def emit(w_ref, dims):
        acc = lax.dot_general(h_ref[...], w_ref[...], dims, preferred_element_type=F32)
        if has_scale:
            acc = acc * cs_ref[...]
        o_ref[...] = acc.astype(o_ref.dtype)

    if n_w:
        for p in range(n_w):
            in_part = jnp.logical_and(j >= p * tiles_per_part, j < (p + 1) * tiles_per_part)
            pl.when(in_part)(functools.partial(emit, w_refs[p], nt))
    else:
        emit(w_refs[0], (((1,), (0,)), ((), ())))


def _proj(x, g, mod, layer, w, w_aux, out_dtype, seq, n_batch, col_scale=None, tm=1024, tn=1024):
    parts = list(w) if isinstance(w, (list, tuple)) else None
    if parts is None:
        d, n = w.shape
        n_w, tiles_per_part = 0, 0
    else:
        d = parts[0].shape[1]
        n_w, tiles_per_part = len(parts), parts[0].shape[0] // tn
        assert all(p.shape == (tiles_per_part * tn, d) for p in parts)
        n = n_w * tiles_per_part * tn
    m = sum(a.shape[0] for a in x) if isinstance(x, tuple) else x.shape[0]
    n_lat_tiles = (n_batch * seq) // tm
    has_aux = w_aux is not None
    has_scale = col_scale is not None
    mod_spec = pl.BlockSpec((None, None, 1, mod.shape[-1]),
                            lambda i, j: (layer, _mod_row(i, tm, seq, n_batch), 0, 0))
    in_specs, args = _row_operand(x, tm, n_lat_tiles, d, lambda j: 0, single_ctx_buffer=True)
    n_x = len(args)
    in_specs += [pl.BlockSpec((1, d), lambda i, j: (0, 0)), mod_spec]
    args += [g.reshape(1, d), mod]
    if parts is None:
        in_specs.append(pl.BlockSpec((d, tn), lambda i, j: (0, j)))
        args.append(w)
    else:
        for p, part in enumerate(parts):
            in_specs.append(pl.BlockSpec(
                (tn, d), lambda i, j, p=p: (jnp.clip(j - p * tiles_per_part, 0, tiles_per_part - 1), 0)))
            args.append(part)
    out_specs = [pl.BlockSpec((tm, tn), lambda i, j: (i, j))]
    out_shape = [jax.ShapeDtypeStruct((m, n), out_dtype)]
    if has_scale:
        in_specs.append(pl.BlockSpec((1, tn), lambda i, j: (0, j)))
        args.append(col_scale.reshape(1, n))
    if has_aux:
        na = w_aux.shape[0] if parts is not None else w_aux.shape[1]
        in_specs.append(pl.BlockSpec(w_aux.shape, lambda i, j: (0, 0)))
        out_specs.append(pl.BlockSpec((tm, na), lambda i, j: (i, 0)))
        out_shape.append(jax.ShapeDtypeStruct((m, na), F32))
        args.append(w_aux)
    block_bytes = n_x * tm * d * 4 + max(n_w, 1) * d * tn * 2 + tm * tn * 4 + tm * d
    outs = pl.pallas_call(
        functools.partial(_proj_kernel, n_x=n_x, n_w=n_w, tiles_per_part=tiles_per_part, n_lat_tiles=n_lat_tiles,
                          has_aux=has_aux, has_scale=has_scale),
        grid=(m // tm, n // tn),
        in_specs=in_specs,
        out_specs=out_specs,
        out_shape=out_shape,
        scratch_shapes=[pltpu.VMEM((tm, d), BF16)],
        compiler_params=_params(("arbitrary", "arbitrary"), block_bytes),
        name="proj",
    )(*args)
    return outs if has_aux else outs[0]


def _row_operand(a, tm, n_lat_tiles, width, col_of_j, single_ctx_buffer=False):
    if not isinstance(a, tuple):
        return [pl.BlockSpec((tm, width), lambda i, j: (i, col_of_j(j)))], [a]
    lat, ctx = a
    assert ctx.shape[0] == tm and lat.shape[0] == n_lat_tiles * tm

    def lat_index(i, j):
        return jnp.minimum(i, n_lat_tiles - 1), jnp.where(i < n_lat_tiles, col_of_j(j), 0)

    def ctx_index(i, j):
        return 0, jnp.where(i < n_lat_tiles, 0, col_of_j(j))

    ctx_mode = dict(pipeline_mode=pl.Buffered(1)) if single_ctx_buffer else {}
    return [pl.BlockSpec((tm, width), lat_index), pl.BlockSpec((tm, width), ctx_index, **ctx_mode)], [lat, ctx]


def _pick_rows(refs, n_lat_tiles):
    if len(refs) == 1:
        return refs[0][...]
    return jnp.where(pl.program_id(0) < n_lat_tiles, refs[0][...], refs[1][...])


def _out_proj_kernel(*refs, arity, n_lat_tiles):
    refs = list(refs)
    n_lhs = len(arity) - 1
    groups = [[refs.pop(0) for _ in range(n)] for n in arity[:-1]]
    ws = [refs.pop(0) for _ in range(n_lhs)]
    res = [refs.pop(0) for _ in range(arity[-1])]
    gate_ref, o_ref = refs
    acc = None
    for grp, w_ref in zip(groups, ws):
        part = jnp.dot(_pick_rows(grp, n_lat_tiles), w_ref[...], preferred_element_type=F32)
        acc = part if acc is None else acc + part
    o_ref[...] = _pick_rows(res, n_lat_tiles) + gate_ref[...] * acc


def _out_proj(lhs_list, w, res, mod, layer, gate_idx, m_rows, seq, n_batch, tm=1024, tn=1024):
    d = w.shape[1]
    n_lat_tiles = (n_batch * seq) // tm
    in_specs, args, arity, widths = [], [], [], []
    for a in lhs_list:
        kk = (a[0] if isinstance(a, tuple) else a).shape[1]
        sp, ar = _row_operand(a, tm, n_lat_tiles, kk, lambda j: 0)
        in_specs += sp
        args += ar
        arity.append(len(ar))
        widths.append(kk)
    row0 = 0
    for kk in widths:
        in_specs.append(pl.BlockSpec((kk, tn), lambda i, j, rb=row0 // kk: (rb, j)))
        args.append(w)
        row0 += kk
    sp, ar = _row_operand(res, tm, n_lat_tiles, tn, lambda j: j)
    in_specs += sp
    args += ar
    arity.append(len(ar))
    in_specs.append(pl.BlockSpec((None, None, 1, tn),
                                 lambda i, j: (layer, _mod_row(i, tm, seq, n_batch), 0, gate_idx * (d // tn) + j)))
    args.append(mod)
    k_total = sum(widths)
    block_bytes = 2 * tm * k_total * 2 + k_total * tn * 2 + 3 * tm * tn * 4
    return pl.pallas_call(
        functools.partial(_out_proj_kernel, arity=tuple(arity), n_lat_tiles=n_lat_tiles),
        grid=(m_rows // tm, d // tn),
        in_specs=in_specs,
        out_specs=pl.BlockSpec((tm, tn), lambda i, j: (i, j)),
        out_shape=jax.ShapeDtypeStruct((m_rows, d), F32),
        compiler_params=_params(("arbitrary", "arbitrary"), block_bytes),
        name="out_proj",
    )(*args)


def _ffn_kernel(x_ref, g_ref, mod_ref, w1_ref, w3_ref, w2_ref, fg_ref, o_ref, h_ref, *, final_norm, rows):
    j = pl.program_id(1)
    d = x_ref.shape[-1]

    @pl.when(j == 0)
    def _():
        _norm_mod_into(x_ref, g_ref, mod_ref, h_ref, 3, 4)
        o_ref[...] = jnp.zeros_like(o_ref)

    h = h_ref[...]
    a = jnp.dot(h, w1_ref[...].astype(BF16), preferred_element_type=F32)
    b = jnp.dot(h, w3_ref[...].astype(BF16), preferred_element_type=F32)
    u = (_silu(a) * b).astype(BF16)
    o_ref[...] += jnp.dot(u, w2_ref[...].astype(BF16), preferred_element_type=F32)

    @pl.when(j == pl.num_programs(1) - 1)
    def _():
        gate = mod_ref[:, 5 * d:6 * d]
        fg = fg_ref[...]

        def body(r, carry):
            sl = pl.ds(pl.multiple_of(r * rows, rows), rows)
            y = x_ref[sl, :] + gate * o_ref[sl, :]
            if final_norm:
                y = y * lax.rsqrt(jnp.mean(y * y, axis=-1, keepdims=True) + NORM_EPS) * fg
            o_ref[sl, :] = y
            return carry

        lax.fori_loop(0, x_ref.shape[0] // rows, body, 0)


def _ffn(x, g, mod, layer, w1, w3, w2, final_g, final_norm, m_rows, seq, n_batch, tm=1024, tf=256):
    d = x.shape[1]
    ff = w1.shape[2]
    block_bytes = 2 * tm * d * 4 + 3 * d * tf * w1.dtype.itemsize + tm * d
    return pl.pallas_call(
        functools.partial(_ffn_kernel, final_norm=final_norm, rows=128),
        grid=(m_rows // tm, ff // tf),
        in_specs=[
            pl.BlockSpec((tm, d), lambda i, j: (i, 0)),
            pl.BlockSpec((1, d), lambda i, j: (0, 0)),
            pl.BlockSpec((None, None, 1, mod.shape[-1]),
                         lambda i, j: (layer, _mod_row(i, tm, seq, n_batch), 0, 0)),
            pl.BlockSpec((None, d, tf), lambda i, j: (layer, 0, j)),
            pl.BlockSpec((None, d, tf), lambda i, j: (layer, 0, j)),
            pl.BlockSpec((None, tf, d), lambda i, j: (layer, j, 0)),
            pl.BlockSpec((1, d), lambda i, j: (0, 0)),
        ],
        out_specs=pl.BlockSpec((tm, d), lambda i, j: (i, 0)),
        out_shape=jax.ShapeDtypeStruct((m_rows, d), F32),
        scratch_shapes=[pltpu.VMEM((tm, d), BF16)],
        compiler_params=_params(("arbitrary", "arbitrary"), block_bytes),
        name="ffn",
    )(x, g.reshape(1, d), mod, w1, w3, w2, final_g.reshape(1, d))


def _conv_kernel(prev_ref, cur_ref, next_ref, w_ref, b_ref, o_ref, ext_ref, *, tm, halo, seq, ctx_len, n_lat_tiles):
    i = pl.program_id(0)
    ext_ref[0:halo, :] = prev_ref[...]
    ext_ref[halo:halo + tm, :] = cur_ref[...]
    ext_ref[halo + tm:, :] = next_ref[...]
    seg_len = jnp.where(i < n_lat_tiles, seq, ctx_len)
    pos = (i * tm + lax.broadcasted_iota(jnp.int32, (tm, 1), 0)) & (seg_len - 1)
    acc = jnp.zeros(cur_ref.shape, F32) + b_ref[...]
    half = SSM_CONV // 2
    for t in range(SSM_CONV):
        src = pos + (t - half)
        tap = ext_ref[halo - half + t:halo - half + t + tm, :]
        acc = acc + jnp.where((src >= 0) & (src < seg_len), tap, 0.0) * w_ref[t:t + 1, :]
    o_ref[...] = _silu(acc)


def _ssd_conv(p0, col0, width, conv_w, conv_b, seq, ctx_len, n_lat_rows, tm=1024, tc=512, halo=8):
    m = p0.shape[0]
    assert seq % tm == 0 and tm % ctx_len == 0 and (seq & (seq - 1)) == 0 and (ctx_len & (ctx_len - 1)) == 0
    cb0 = col0 // tc
    hb = tm // halo
    last_hb = m // halo - 1
    wpad = jnp.zeros((8, width), F32).at[:SSM_CONV].set(conv_w)
    return pl.pallas_call(
        functools.partial(_conv_kernel, tm=tm, halo=halo, seq=seq, ctx_len=ctx_len, n_lat_tiles=n_lat_rows // tm),
        grid=(m // tm, width // tc),
        in_specs=[
            pl.BlockSpec((halo, tc), lambda i, j: (jnp.maximum(i * hb - 1, 0), cb0 + j)),
            pl.BlockSpec((tm, tc), lambda i, j: (i, cb0 + j)),
            pl.BlockSpec((halo, tc), lambda i, j: (jnp.minimum((i + 1) * hb, last_hb), cb0 + j)),
            pl.BlockSpec((8, tc), lambda i, j: (0, j)),
            pl.BlockSpec((1, tc), lambda i, j: (0, j)),
        ],
        out_specs=pl.BlockSpec((tm, tc), lambda i, j: (i, j)),
        out_shape=jax.ShapeDtypeStruct((m, width), F32),
        scratch_shapes=[pltpu.VMEM((tm + 2 * halo, tc), F32)],
        compiler_params=_params(("arbitrary", "arbitrary"), 3 * tm * tc * 4),
        name="ssd_conv",
    )(p0, p0, p0, wpad, conv_b.reshape(1, width))


def _ssd_scan_kernel(xs_f, bm_f, cm_f, dt_f, xs_b, bm_b, cm_b, dt_b, prm_ref, yf_ref, yb_ref, state_ref, *, n_heads):
    @pl.when(pl.program_id(1) == 0)
    def _():
        state_ref[...] = jnp.zeros_like(state_ref)

    fwd = _ssd_chunk(xs_f, bm_f, cm_f, dt_f, prm_ref.at[0], yf_ref, state_ref.at[0], backward=False, n_heads=n_heads)
    bwd = _ssd_chunk(xs_b, bm_b, cm_b, dt_b, prm_ref.at[1], yb_ref, state_ref.at[1], backward=True, n_heads=n_heads)
    for g in range(SSM_GROUPS):
        fwd(g)
        bwd(g)


def _ssd_chunk(xs_ref, bm_ref, cm_ref, dt_ref, prm_ref, y_ref, state_ref, *, backward, n_heads):
    hp = SSM_HEAD_DIM
    rep = n_heads // SSM_GROUPS
    lc = SSM_CHUNK
    assert 2 * hp == LANES and rep % 2 == 0 and SSM_STATE == LANES and lc == LANES

    dt_in = dt_ref[...] + prm_ref[1:2, :]
    dt = jnp.maximum(dt_in, 0.0) + jnp.log1p(jnp.exp(-jnp.abs(dt_in)))
    dta = dt * (-jnp.exp(prm_ref[0:1, :]) * LOG2E)
    row = lax.broadcasted_iota(jnp.int32, (lc, lc), 0)
    col = lax.broadcasted_iota(jnp.int32, (lc, lc), 1)
    causal = (row <= col) if backward else (row >= col)
    a_cum = jnp.dot(causal.astype(F32), dta, preferred_element_type=F32, precision=lax.Precision.HIGHEST)
    a_tot = a_cum[0:1, :] if backward else a_cum[lc - 1:lc, :]
    w_end = dt * jnp.exp2(a_tot - a_cum)
    a_cum_t, dt_t, w_end_t = a_cum.T, dt.T, w_end.T
    lo = col < hp

    def do_group(g):
        gs = slice(g * SSM_STATE, (g + 1) * SSM_STATE)
        b_f = bm_ref[:, gs]
        c_g = cm_ref[:, gs].astype(BF16)
        cb = lax.dot_general(c_g, b_f.astype(BF16), (((1,), (1,)), ((), ())), preferred_element_type=F32)
        b_t = b_f.T
        cols_g = slice(g * rep * hp, (g + 1) * rep * hp)
        y_off = jnp.dot(c_g, state_ref[:, cols_g].astype(BF16), preferred_element_type=F32)
        for pr in range(rep // 2):
            cols = slice(g * rep * hp + pr * LANES, g * rep * hp + (pr + 1) * LANES)
            xs2 = xs_ref[:, cols].astype(BF16)
            y_d, s_n, e_a = [], [], []
            for h in (g * rep + 2 * pr, g * rep + 2 * pr + 1):
                a_col = jnp.broadcast_to(a_cum[:, h:h + 1], (lc, lc))
                decay = jnp.exp2(jnp.where(causal, a_col - a_cum_t[h:h + 1, :], -jnp.inf))
                m = (cb * decay * dt_t[h:h + 1, :]).astype(BF16)
                y_d.append(jnp.dot(m, xs2, preferred_element_type=F32))
                s_n.append(jnp.dot((b_t * w_end_t[h:h + 1, :]).astype(BF16), xs2, preferred_element_type=F32))
                e_a.append(jnp.exp2(a_col))
            h0 = g * rep + 2 * pr
            y_ref[:, cols] = (jnp.where(lo, y_d[0], y_d[1])
                              + y_off[:, pr * LANES:(pr + 1) * LANES] * jnp.where(lo, e_a[0], e_a[1]))
            chunk_decay = jnp.where(lo[0:1, :], jnp.exp2(a_tot[:, h0:h0 + 1]), jnp.exp2(a_tot[:, h0 + 1:h0 + 2]))
            state_ref[:, cols] = state_ref[:, cols] * chunk_decay + jnp.where(lo, s_n[0], s_n[1])

    return do_group


def _ssd_scan(xbc, dt_raw, prm, n_batch, seq, ctx_len, d_ssm):
    m = xbc.shape[0]
    lc = SSM_CHUNK
    n_heads = d_ssm // SSM_HEAD_DIM
    nc_ctx, nc_lat = ctx_len // lc, seq // lc
    ctx_blk0 = n_batch * nc_lat
    gn = SSM_GROUPS * SSM_STATE

    def row_blk(b, s, backward):
        ctx_c = nc_ctx - 1 - s if backward else s
        lat_c = nc_lat - 1 - (s - nc_ctx) if backward else s - nc_ctx
        return jnp.where(s < nc_ctx, ctx_blk0 + b * nc_ctx + ctx_c, b * nc_lat + lat_c)

    def chunk_specs(backward):
        r = int(backward)
        return [
            pl.BlockSpec((lc, d_ssm), lambda b, s: (row_blk(b, s, backward), 0)),
            pl.BlockSpec((lc, gn), lambda b, s: (row_blk(b, s, backward), d_ssm // gn)),
            pl.BlockSpec((lc, gn), lambda b, s: (row_blk(b, s, backward), d_ssm // gn + 1)),
            pl.BlockSpec((lc, LANES), lambda b, s: (row_blk(b, s, backward), r)),
        ]

    return pl.pallas_call(
        functools.partial(_ssd_scan_kernel, n_heads=n_heads),
        grid=(n_batch, nc_ctx + nc_lat),
        in_specs=chunk_specs(False) + chunk_specs(True) + [pl.BlockSpec((2, 8, LANES), lambda b, s: (0, 0, 0))],
        out_specs=[pl.BlockSpec((lc, d_ssm), lambda b, s: (row_blk(b, s, False), 0)),
                   pl.BlockSpec((lc, d_ssm), lambda b, s: (row_blk(b, s, True), 0))],
        out_shape=[jax.ShapeDtypeStruct((m, d_ssm), F32), jax.ShapeDtypeStruct((m, d_ssm), F32)],
        scratch_shapes=[pltpu.VMEM((2, SSM_STATE, d_ssm), F32)],
        compiler_params=_params(("arbitrary", "arbitrary"), 8 * lc * d_ssm * 4),
        name="ssd_scan",
    )(xbc, xbc, xbc, dt_raw, xbc, xbc, xbc, dt_raw, prm)


def _ssd_finish_kernel(yf_ref, yb_ref, xs_ref, z_ref, dskip_ref, g_ref, o_ref, *, group_width):
    y = (yf_ref[...] + yb_ref[...] + dskip_ref[...] * xs_ref[...]) * _silu(z_ref[...])
    for g in range(y.shape[1] // group_width):
        sl = slice(g * group_width, (g + 1) * group_width)
        v = y[:, sl]
        vn = v * lax.rsqrt(jnp.mean(v * v, axis=-1, keepdims=True) + NORM_EPS) * g_ref[:, sl]
        o_ref[:, sl] = vn.astype(o_ref.dtype)


def _ssd_finish(y2, xbc, p0, d_skip_lanes, norm_g, d_ssm, tm=512):
    m = xbc.shape[0]
    return pl.pallas_call(
        functools.partial(_ssd_finish_kernel, group_width=d_ssm // SSM_GROUPS),
        grid=(m // tm,),
        in_specs=[
            pl.BlockSpec((tm, d_ssm), lambda i: (i, 0)),
            pl.BlockSpec((tm, d_ssm), lambda i: (i, 0)),
            pl.BlockSpec((tm, d_ssm), lambda i: (i, 0)),
            pl.BlockSpec((tm, d_ssm), lambda i: (i, 0)),
            pl.BlockSpec((1, d_ssm), lambda i: (0, 0)),
            pl.BlockSpec((1, d_ssm), lambda i: (0, 0)),
        ],
        out_specs=pl.BlockSpec((tm, d_ssm), lambda i: (i, 0)),
        out_shape=jax.ShapeDtypeStruct((m, d_ssm), BF16),
        compiler_params=_params(("arbitrary",), 5 * tm * d_ssm * 4),
        name="ssd_finish",
    )(*y2, xbc, p0, d_skip_lanes.reshape(1, d_ssm), norm_g.reshape(1, d_ssm))


def _rope(x, cos, sin_up, sin_dn):
    quarter = DIFF_HEAD_DIM // 4
    return (x * cos + pltpu.roll(x, LANES - quarter, axis=1) * sin_up + pltpu.roll(x, quarter, axis=1) * sin_dn)


def _diff_attn_kernel(*refs, with_lat, lambda_init, n_ctx, sub_q):
    if with_lat:
        (q_ref, kc_ref, kl_ref, vc_ref, vl_ref, qcos_ref, qsu_ref, qsd_ref, kcos_ref, ksu_ref, ksd_ref,
         lam_ref, g_ref, o_ref, k_scr, vt_scr) = refs
    else:
        q_ref, kc_ref, vc_ref, lam_ref, g_ref, o_ref, k_scr, vt_scr = refs

    @pl.when(pl.program_id(2) == 0)
    def _():
        k_scr[0:n_ctx, :] = kc_ref[...].astype(BF16)
        vt_scr[:, 0:n_ctx] = vc_ref[...].T.astype(BF16)
        if with_lat:
            k_scr[n_ctx:, :] = _rope(kl_ref[...], kcos_ref[...], ksu_ref[...], ksd_ref[...]).astype(BF16)
            vt_scr[:, n_ctx:] = vl_ref[...].T.astype(BF16)

    lam_p = lam_ref[...]
    lam = (jnp.exp(jnp.sum(lam_p[0:1, :] * lam_p[1:2, :], axis=-1, keepdims=True))
           - jnp.exp(jnp.sum(lam_p[2:3, :] * lam_p[3:4, :], axis=-1, keepdims=True)) + lambda_init)
    first = lax.broadcasted_iota(jnp.int32, (LANES, sub_q), 0) < DIFF_HEAD_DIM
    n_sub = q_ref.shape[0] // sub_q

    def scores(t):
        rows = slice(t * sub_q, (t + 1) * sub_q)
        q = q_ref[rows, :]
        if with_lat:
            q = _rope(q, qcos_ref[rows, :], qsu_ref[rows, :], qsd_ref[rows, :])
        qt = (q * (DIFF_HEAD_DIM ** -0.5 * LOG2E)).T
        q2t = jnp.concatenate([jnp.where(first, qt, 0.0), jnp.where(first, 0.0, qt)], axis=1).astype(BF16)
        return jnp.dot(k_scr[...], q2t, preferred_element_type=F32)

    s_next = scores(0)
    for t in range(n_sub):
        rows = slice(t * sub_q, (t + 1) * sub_q)
        s = s_next
        if t + 1 < n_sub:
            s_next = scores(t + 1)
        e = jnp.exp2(s - jnp.max(s, axis=0, keepdims=True))
        r = 1.0 / jnp.sum(e, axis=0, keepdims=True)
        ovt = jnp.dot(vt_scr[...], e.astype(BF16), preferred_element_type=F32) * r
        ot = ovt[:, :sub_q] - lam * ovt[:, sub_q:]
        ot = ot * lax.rsqrt(jnp.mean(ot * ot, axis=0, keepdims=True) + NORM_EPS) * g_ref[...] * (1.0 - lambda_init)
        o_ref[rows, :] = ot.T.astype(o_ref.dtype)


def _diff_attn(p0, q_col0, k_col0, v_col0, n_heads, rope_tabs, lam_p, subln_g, lambda_init,
               n_batch, seq, ctx_len, with_lat, tq=2048, sub_q=256):
    cb = lambda c0: c0 // LANES
    ctx_rb0 = (n_batch * seq) // ctx_len
    n_keys = ctx_len + (seq if with_lat else 0)
    if with_lat:
        n_q = seq // tq
        q_spec = pl.BlockSpec((tq, LANES), lambda b, h, i: (b * n_q + i, cb(q_col0) + h))
    else:
        tq = ctx_len
        n_q = 1
        q_spec = pl.BlockSpec((tq, LANES), lambda b, h, i: (ctx_rb0 + b, cb(q_col0) + h))
    kc_spec = pl.BlockSpec((ctx_len, LANES), lambda b, h, i: (ctx_rb0 + b, cb(k_col0) + h))
    vc_spec = pl.BlockSpec((ctx_len, LANES), lambda b, h, i: (ctx_rb0 + b, cb(v_col0) + h))
    small =[pl.BlockSpec((8, LANES), lambda b, h, i: (0, 0)), pl.BlockSpec((LANES, sub_q), lambda b, h, i: (0, 0))]
    small_args = [lam_p, jnp.broadcast_to(subln_g[:, None], (LANES, sub_q))]
    if with_lat:
        kl_spec = pl.BlockSpec((seq, LANES), lambda b, h, i: (b, cb(k_col0) + h))
        vl_spec = pl.BlockSpec((seq, LANES), lambda b, h, i: (b, cb(v_col0) + h))
        qt = pl.BlockSpec((tq, LANES), lambda b, h, i: (i, 0))
        kt = pl.BlockSpec((seq, LANES), lambda b, h, i: (0, 0))
        in_specs = [q_spec, kc_spec, kl_spec, vc_spec, vl_spec, qt, qt, qt, kt, kt, kt] + small
        args = [p0, p0, p0, p0, p0, *rope_tabs, *rope_tabs] + small_args
    else:
        in_specs = [q_spec, kc_spec, vc_spec] + small
        args = [p0, p0, p0] + small_args
    block_bytes = 10 * seq * LANES * 4 if with_lat else 8 * ctx_len * LANES * 4
    return pl.pallas_call(
        functools.partial(_diff_attn_kernel, with_lat=with_lat, lambda_init=lambda_init, n_ctx=ctx_len,
                          sub_q=sub_q),
        grid=(n_batch, n_heads, n_q),
        in_specs=in_specs,
        out_specs=pl.BlockSpec((tq, LANES), lambda b, h, i: (b * n_q + i, h)),
        out_shape=jax.ShapeDtypeStruct((n_batch * n_q * tq, n_heads * LANES), BF16),
        scratch_shapes=[pltpu.VMEM((n_keys, LANES), BF16), pltpu.VMEM((LANES, n_keys), BF16)],
        compiler_params=_params(("arbitrary", "arbitrary", "arbitrary"), block_bytes),
        name="diff_attn_lat" if with_lat else "diff_attn_ctx",
    )(*args)


def _rope_tables(seq):
    half = DIFF_HEAD_DIM // 2
    pos = jnp.arange(seq)
    row, col = pos // GRID_W, pos % GRID_W
    inv_freq = ROPE_BASE ** (-jnp.arange(0, half, 2, dtype=F32) / half)
    lane = np.arange(LANES) % DIFF_HEAD_DIM
    use_col = jnp.asarray(lane >= half)
    first = jnp.asarray((lane % half) < half // 2)
    freq = inv_freq[jnp.asarray(lane % (half // 2))]
    p = jnp.where(use_col[None, :], col[:, None], row[:, None]).astype(F32)
    ang = p * freq[None, :]
    cos, sin = jnp.cos(ang), jnp.sin(ang)
    return cos, jnp.where(first[None, :], -sin, 0.0), jnp.where(first[None, :], 0.0, sin)


def _rpb_gather_kernel(rpb_ref, o_ref):
    n = o_ref.shape[1]
    k = rpb_ref.shape[1]
    colid = pl.program_id(0) * n + lax.broadcasted_iota(jnp.int32, (k, n), 1)
    j = lax.broadcasted_iota(jnp.int32, (k, n), 0)
    qc = lax.shift_right_logical(colid, int(math.log2(LANES)))
    half = lax.shift_right_logical(colid & (LANES - 1), int(math.log2(GRID_W)))
    kc = colid & (GRID_W - 1)
    sel = (j == half * LANES + jnp.clip(kc - qc + (WIN_COLS - 1), 0, 2 * WIN_COLS - 2)).astype(F32)
    o_ref[...] = jnp.dot(rpb_ref[...], sel, preferred_element_type=F32, precision=lax.Precision.HIGHEST)


def _rpb_pairs(rpb):
    nh, nr, ncol = rpb.shape
    n_slot = nr + 1
    left = jnp.pad(rpb, ((0, 0), (1, 0), (0, LANES - ncol)))
    right = jnp.pad(rpb, ((0, 0), (0, 1), (0, LANES - ncol)))
    rows = jnp.concatenate([left, right], axis=2).reshape(nh * n_slot, 2 * LANES)
    tn = 1024
    out = pl.pallas_call(
        _rpb_gather_kernel,
        grid=(GRID_W * LANES // tn,),
        in_specs=[pl.BlockSpec((nh * n_slot, 2 * LANES), lambda j: (0, 0))],
        out_specs=pl.BlockSpec((nh * n_slot, tn), lambda j: (0, j)),
        out_shape=jax.ShapeDtypeStruct((nh * n_slot, GRID_W * LANES), F32),
        compiler_params=_params(("arbitrary",), nh * n_slot * tn * 4 + 2 * LANES * tn * 4),
        name="rpb_gather",
    )(rows)
    return out.reshape(nh, n_slot, GRID_W, LANES)


def _na_block_plan(rows):
    kr = min(WIN_ROWS, rows)
    n_blk = rows // NA_Q_ROWS
    kb = np.clip(np.arange(n_blk) * NA_Q_ROWS - kr // 2, 0, rows - NA_K_ROWS)
    layouts, layout_of = [], []
    for blk in range(n_blk):
        dr = np.full((NA_Q_ROWS, NA_K_ROWS), -1, np.int64)
        for i in range(NA_Q_ROWS):
            r = blk * NA_Q_ROWS + i
            rs = int(np.clip(r - kr // 2, 0, rows - kr))
            for j in range(NA_K_ROWS):
                krow = kb[blk] + j
                if rs <= krow < rs + kr:
                    dr[i, j] = krow - r + WIN_ROWS - 1
        key = dr.tobytes()
        if key not in [l.tobytes() for l in layouts]:
            layouts.append(dr)
        layout_of.append([l.tobytes() for l in layouts].index(key))
    return kb, np.stack(layouts), np.asarray(layout_of)


def _na_bias_kernel(tp_ref, o_ref, *, layouts):
    qc = lax.broadcasted_iota(jnp.int32, (GRID_W, LANES), 0)
    lane = lax.broadcasted_iota(jnp.int32, (GRID_W, LANES), 1)
    kc = lane & (GRID_W - 1)
    left = lane < GRID_W
    col_start = jnp.clip(qc - WIN_COLS // 2, 0, GRID_W - WIN_COLS)
    col_ok = (kc >= col_start) & (kc < col_start + WIN_COLS)
    n_lay, n_q, n_k = layouts.shape
    for lay in range(n_lay):
        for i in range(n_q):
            for p in range(-(-n_k // 2)):
                d_l = int(layouts[lay, i, 2 * p])
                d_r = int(layouts[lay, i, 2 * p + 1]) if 2 * p + 1 < n_k else -1
                width = LANES if 2 * p + 1 < n_k else GRID_W
                if d_l < 0 and d_r < 0:
                    tile = jnp.full((GRID_W, LANES), -jnp.inf, F32)
                else:
                    assert d_l < 0 or d_r < 0 or d_r == d_l + 1
                    ok = col_ok
                    if d_l < 0:
                        ok = ok & jnp.logical_not(left)
                    if d_r < 0:
                        ok = ok & left
                    slot = d_r if d_r >= 0 else d_l + 1
                    tile = jnp.where(ok, tp_ref[slot] * LOG2E, -jnp.inf)
                o_ref[lay, i * GRID_W:(i + 1) * GRID_W, p * LANES:p * LANES + width] = tile[:, :width]


def _na_bias(rpb_pairs, layouts):
    nh, n_slot = rpb_pairs.shape[:2]
    n_lay = layouts.shape[0]
    tq, nkw = NA_Q_ROWS * GRID_W, NA_K_ROWS * GRID_W
    return pl.pallas_call(
        functools.partial(_na_bias_kernel, layouts=layouts),
        grid=(nh,),
        in_specs=[pl.BlockSpec((None, n_slot, GRID_W, LANES), lambda h: (h, 0, 0, 0))],
        out_specs=pl.BlockSpec((None, n_lay, tq, nkw), lambda h: (h, 0, 0, 0)),
        out_shape=jax.ShapeDtypeStruct((nh, n_lay, tq, nkw), F32),
        compiler_params=_params(("arbitrary",), n_lay * tq * nkw * 4),
        name="na_bias",
    )(rpb_pairs)


def _na_kernel(q_ref, kl_ref, vl_ref, kc_ref, vc_ref, bias_ref, o_ref, *, plan):
    tq = NA_Q_ROWS * GRID_W
    nkw = NA_K_ROWS * GRID_W
    nt = (((1,), (1,)), ((), ()))
    kc = kc_ref[...]
    vc = vc_ref[...]

    def scores(blk):
        kb, lay = plan[blk]
        q = q_ref[blk * tq:(blk + 1) * tq, :]
        ks = slice(kb * GRID_W, kb * GRID_W + nkw)
        return (lax.dot_general(q, kl_ref[ks, :], nt, preferred_element_type=F32) + bias_ref[lay],
                lax.dot_general(q, kc, nt, preferred_element_type=F32))

    def weighted_values(blk, e_w, e_c, l):
        kb, _ = plan[blk]
        ks = slice(kb * GRID_W, kb * GRID_W + nkw)
        o = (jnp.dot(e_w, vl_ref[ks, :], preferred_element_type=F32) + jnp.dot(e_c, vc, preferred_element_type=F32))
        o_ref[blk * tq:(blk + 1) * tq, :] = (o * (1.0 / l)).astype(o_ref.dtype)

    s_next = scores(0)
    pending = None
    for blk in range(len(plan)):
        s_w, s_c = s_next
        if blk + 1 < len(plan):
            s_next = scores(blk + 1)
        if pending is not None:
            weighted_values(blk - 1, *pending)
        m = jnp.maximum(jnp.max(s_w, axis=-1, keepdims=True), jnp.max(s_c, axis=-1, keepdims=True))
        e_w = jnp.exp2(s_w - m)
        e_c = jnp.exp2(s_c - m)
        l = jnp.sum(e_w, axis=-1, keepdims=True) + jnp.sum(e_c, axis=-1, keepdims=True)
        pending = (e_w.astype(BF16), e_c.astype(BF16), l)
    weighted_values(len(plan) - 1, *pending)


def _na_attn(p1, rpb, n_batch, seq, ctx_len, n_heads):
    rows = seq // GRID_W
    kb, layouts, layout_of = _na_block_plan(rows)
    bias = _na_bias(_rpb_pairs(rpb), layouts)
    n_lay = layouts.shape[0]
    tq = NA_Q_ROWS * GRID_W
    nkw = NA_K_ROWS * GRID_W
    ctx_rb0 = (n_batch * seq) // ctx_len
    plan = tuple((int(k), int(l)) for k, l in zip(kb, layout_of))
    return pl.pallas_call(
        functools.partial(_na_kernel, plan=plan),
        grid=(n_heads, n_batch),
        in_specs=[
            pl.BlockSpec((seq, LANES), lambda h, b: (b, h)),
            pl.BlockSpec((seq, LANES), lambda h, b: (b, n_heads + h)),
            pl.BlockSpec((seq, LANES), lambda h, b: (b, 2 * n_heads + h)),
            pl.BlockSpec((ctx_len, LANES), lambda h, b: (ctx_rb0 + b, n_heads + h)),
            pl.BlockSpec((ctx_len, LANES), lambda h, b: (ctx_rb0 + b, 2 * n_heads + h)),
            pl.BlockSpec((None, n_lay, tq, nkw), lambda h, b: (h, 0, 0, 0)),
        ],
        out_specs=pl.BlockSpec((seq, LANES), lambda h, b: (b, h)),
        out_shape=jax.ShapeDtypeStruct((n_batch * seq, n_heads * LANES), BF16),
        compiler_params=_params(("arbitrary", "arbitrary"), n_lay * tq * nkw * 4 + 4 * seq * LANES * 2),
        name="na_attn",
    )(p1, p1, p1, p1, p1, bias)


def _pack_rows(rows, n_rows=8):
    out = jnp.zeros((n_rows, LANES), F32)
    for r, v in enumerate(rows):
        out = out.at[r, :v.shape[0]].set(v.astype(F32))
    return out


def kernel(x, c, ctx, c_ctx, ada_w, ada_b, norm_mix_g, norm_ffn_g, final_norm_g, ffn_w1, ffn_w3, ffn_w2, ev_w_in, ev_conv_w, ev_conv_b, ev_a_log, ev_dt_bias, ev_d_skip, ev_ssm_norm_g, ev_lam_q1, ev_lam_k1, ev_lam_q2, ev_lam_k2, ev_subln_g, ev_w_out, od_w_in, od_rpb, od_w_out):
    n_batch, seq, d = x.shape
    ctx_len = ctx.shape[1]
    depth = ada_w.shape[0]
    n_lat = n_batch * seq
    n_tok = n_lat + n_batch * ctx_len

    d_ssm = ev_ssm_norm_g.shape[1]
    n_ssm_heads = ev_d_skip.shape[1]
    d_xbc = ev_conv_w.shape[2]
    d_qk = d_v = (ev_w_in.shape[2] - d_ssm - d_xbc - 2 * n_ssm_heads) // 3
    n_diff_heads = d_v // (2 * DIFF_HEAD_DIM)
    n_na_heads = od_rpb.shape[1]

    stream = (x.reshape(n_lat, d), ctx.reshape(n_batch * ctx_len, d))
    cond = jnp.zeros((COND_ROWS, d), F32).at[:n_batch].set(c).at[n_batch].set(c_ctx)
    mod = _ada_mod(cond, ada_w, ada_b).reshape(depth, COND_ROWS, 1, 6 * d)
    rope_tabs = _rope_tables(seq)
    kw = dict(seq=seq, n_batch=n_batch)
    ffn_w = (ffn_w1, ffn_w3, ffn_w2)

    for i in range(depth):
        ctx_out = i < depth - 1
        m_rows = n_tok if ctx_out else n_lat
        j = i // 2
        if i % 2 == 0:
            lambda_init = 0.8 - 0.6 * math.exp(-0.3 * i)
            w_in = ev_w_in[j]
            dt0 = d_ssm + d_xbc
            w_t = jnp.swapaxes(w_in, 0, 1)
            w_parts = [w_t[:dt0].astype(BF16), w_t[dt0 + 2 * n_ssm_heads:].astype(BF16)]
            w_dt = jnp.zeros((2 * LANES, d), F32)
            w_dt = w_dt.at[:n_ssm_heads].set(w_t[dt0:dt0 + n_ssm_heads])
            w_dt = w_dt.at[LANES:LANES + n_ssm_heads].set(w_t[dt0 + n_ssm_heads:dt0 + 2 * n_ssm_heads])
            p0, dt_raw = _proj(stream, norm_mix_g[i], mod, i, w_parts, w_dt.astype(BF16), F32, **kw)
            q0 = d_ssm + d_xbc
            k0, v0 = q0 + d_qk, q0 + 2 * d_qk

            xbc = _ssd_conv(p0, d_ssm, d_xbc, ev_conv_w[j], ev_conv_b[j], seq, ctx_len, n_lat)
            prm = jnp.stack([_pack_rows([ev_a_log[j, r], ev_dt_bias[j, r]]) for r in range(2)])
            y2 = _ssd_scan(xbc, dt_raw, prm, n_batch, seq, ctx_len, d_ssm)
            mix_ssd = _ssd_finish(y2, xbc, p0, jnp.repeat(ev_d_skip[j], SSM_HEAD_DIM), ev_ssm_norm_g[j], d_ssm)

            lam_p = _pack_rows([ev_lam_q1[j], ev_lam_k1[j], ev_lam_q2[j], ev_lam_k2[j]])
            attn_args = (p0, q0, k0, v0, n_diff_heads, rope_tabs, lam_p, ev_subln_g[j], lambda_init,
                         n_batch, seq, ctx_len)
            mix_attn = _diff_attn(*attn_args, with_lat=True)
            if ctx_out:
                mix_attn = (mix_attn, _diff_attn(*attn_args, with_lat=False))
            lhs = [mix_ssd, mix_attn]
            w_out = ev_w_out[j].astype(BF16)
        else:
            d_na = n_na_heads * NA_HEAD_DIM
            q_scale = jnp.where(jnp.arange(3 * d_na) < d_na, NA_HEAD_DIM ** -0.5 * LOG2E, 1.0).astype(F32)
            p1 = _proj(stream, norm_mix_g[i], mod, i, od_w_in[j].astype(BF16), None, BF16, col_scale=q_scale, **kw)
            assert not ctx_out, "context-query neighbourhood layers are not needed at this depth"
            lhs = [_na_attn(p1, od_rpb[j], n_batch, seq, ctx_len, n_na_heads)]
            w_out = od_w_out[j].astype(BF16)
        stream = _out_proj(lhs, w_out, stream, mod, i, 2, m_rows, **kw)
        stream = _ffn(stream, norm_ffn_g[i], mod, i, *ffn_w, final_norm_g, not ctx_out, m_rows, **kw)
    return stream.reshape(n_batch, seq, d)
```

```python
import functools
import math

import jax
import jax.numpy as jnp
import numpy as np
from jax import lax
from jax.experimental import pallas as pl
from jax.experimental.pallas import tpu as pltpu

F32 = jnp.float32
BF16 = jnp.bfloat16

GRID_W = 64
SSM_HEAD_DIM = 64
SSM_GROUPS = 4
SSM_STATE = 128
SSM_CONV = 5
SSM_CHUNK = 128
DIFF_HEAD_DIM = 64
NA_HEAD_DIM = 128
WIN_ROWS = 8
WIN_COLS = 16
ROPE_BASE = 10000.0
NORM_EPS = 1e-6
LOG2E = math.log2(math.e)

LANES = 128
V7X_VMEM_BYTES = 64 * 1024 * 1024
VMEM_HEADROOM_BYTES = 3 * 1024 * 1024

NA_Q_ROWS = 4
NA_K_ROWS = NA_Q_ROWS + WIN_ROWS - 1
COND_ROWS = 16


def _vmem_limit(block_bytes):
    return int(min(V7X_VMEM_BYTES - VMEM_HEADROOM_BYTES, max(32 * 1024 * 1024, 2 * block_bytes + 16 * 1024 * 1024)))


def _params(semantics, block_bytes):
    return pltpu.CompilerParams(dimension_semantics=semantics, vmem_limit_bytes=_vmem_limit(block_bytes))


def _silu(v):
    return v * jax.nn.sigmoid(v)


def _mod_row(i, tm, seq, n_batch):
    return jnp.minimum((i * tm) // seq, n_batch)


def _ada_kernel(cond_ref, w_ref, b_ref, o_ref):
    s = _silu(cond_ref[...]).astype(BF16)
    o_ref[...] = jnp.dot(s, w_ref[...].astype(BF16), preferred_element_type=F32) + b_ref[...]


def _ada_mod(cond, ada_w, ada_b, tn=1024):
    depth, d, n = ada_w.shape
    return pl.pallas_call(
        _ada_kernel,
        grid=(depth, n // tn),
        in_specs=[
            pl.BlockSpec((COND_ROWS, d), lambda l, j: (0, 0)),
            pl.BlockSpec((None, d, tn), lambda l, j: (l, 0, j)),
            pl.BlockSpec((None, 1, tn), lambda l, j: (l, 0, j)),
        ],
        out_specs=pl.BlockSpec((None, COND_ROWS, tn), lambda l, j: (l, 0, j)),
        out_shape=jax.ShapeDtypeStruct((depth, COND_ROWS, n), F32),
        compiler_params=_params(("arbitrary", "arbitrary"), d * tn * 4),
        name="ada_mod",
    )(cond, ada_w, ada_b.reshape(depth, 1, n))


def _norm_mod_pipeline(x_ref, g_ref, mod_ref, shift_idx, scale_idx, n_chunks, consume):
    tm, d = x_ref.shape
    shift = mod_ref[:, shift_idx * d:(shift_idx + 1) * d]
    gain = g_ref[...] * (1.0 + mod_ref[:, scale_idx * d:(scale_idx + 1) * d])
    chunk = tm // n_chunks

    def norm_rows(c):
        x = x_ref[c * chunk:(c + 1) * chunk, :]
        inv = lax.rsqrt(jnp.mean(x * x, axis=-1, keepdims=True) + NORM_EPS)
        return (x * inv * gain + shift).astype(BF16)

    h_next = norm_rows(0)
    for c in range(n_chunks):
        h = h_next
        if c + 1 < n_chunks:
            h_next = norm_rows(c + 1)
        consume(slice(c * chunk, (c + 1) * chunk), h)


def _proj_kernel(*refs, n_x, n_w, tiles_per_part, n_lat_tiles, has_aux, has_scale, first_step_chunks=4):
    rest = list(refs)
    x_refs = [rest.pop(0) for _ in range(n_x)]
    g_ref, mod_ref = rest.pop(0), rest.pop(0)
    w_refs = [rest.pop(0) for _ in range(max(n_w, 1))]
    cs_ref = rest.pop(0) if has_scale else None
    if has_aux:
        waux_ref, o_ref, oaux_ref, h_ref = rest
    else:
        o_ref, h_ref = rest
    dims = (((1,), (1,)), ((), ())) if n_w else (((1,), (0,)), ((), ()))
    j = pl.program_id(1)
    tm, d = x_refs[0].shape

    def emit(h, rows, w_ref):
        acc = lax.dot_general(h, w_ref[...], dims, preferred_element_type=F32)
        if has_scale:
            acc = acc * cs_ref[...]
        o_ref[rows, :] = acc.astype(o_ref.dtype)

    def first_tile(x_ref):
        def consume(rows, h):
            h_ref[rows, :] = h
            emit(h, rows, w_refs[0])
            if has_aux:
                oaux_ref[rows, :] = lax.dot_general(h, waux_ref[...], dims, preferred_element_type=F32)

        _norm_mod_pipeline(x_ref, g_ref, mod_ref, 0, 1, first_step_chunks, consume)

    if n_x == 1:
        pl.when(j == 0)(functools.partial(first_tile, x_refs[0]))
    else:
        is_lat = pl.program_id(0) < n_lat_tiles
        pl.when(jnp.logical_and(j == 0, is_lat))(functools.partial(first_tile, x_refs[0]))
        pl.when(jnp.logical_and(j == 0, jnp.logical_not(is_lat)))(functools.partial(first_tile, x_refs[1]))

    for p in range(max(n_w, 1)):
        lo = max(p * tiles_per_part, 1)
        in_part = (j >= lo) if n_w == 0 else jnp.logical_and(j >= lo, j < (p + 1) * tiles_per_part)
        pl.when(in_part)(lambda p=p: emit(h_ref[...], slice(None), w_refs[p]))


def _proj(x, g, mod, layer, w, w_aux, out_dtype, seq, n_batch, col_scale=None, tm=1024, tn=1024):
    parts = list(w) if isinstance(w, (list, tuple)) else None
    if parts is None:
        d, n = w.shape
        n_w, tiles_per_part = 0, 0
    else:
        d = parts[0].shape[1]
        n_w, tiles_per_part = len(parts), parts[0].shape[0] // tn
        assert all(p.shape == (tiles_per_part * tn, d) for p in parts)
        n = n_w * tiles_per_part * tn
    m = sum(a.shape[0] for a in x) if isinstance(x, tuple) else x.shape[0]
    n_lat_tiles = (n_batch * seq) // tm
    has_aux = w_aux is not None
    has_scale = col_scale is not None
    mod_spec = pl.BlockSpec((None, None, 1, mod.shape[-1]),
                            lambda i, j: (layer, _mod_row(i, tm, seq, n_batch), 0, 0))
    in_specs, args = _row_operand(x, tm, n_lat_tiles, d, lambda j: 0, single_ctx_buffer=True)
    n_x = len(args)
    in_specs += [pl.BlockSpec((1, d), lambda i, j: (0, 0)), mod_spec]
    args += [g.reshape(1, d), mod]
    if parts is None:
        in_specs.append(pl.BlockSpec((d, tn), lambda i, j: (0, j)))
        args.append(w)
    else:
        for p, part in enumerate(parts):
            in_specs.append(pl.BlockSpec(
                (tn, d), lambda i, j, p=p: (jnp.clip(j - p * tiles_per_part, 0, tiles_per_part - 1), 0)))
            args.append(part)
    out_specs = [pl.BlockSpec((tm, tn), lambda i, j: (i, j))]
    out_shape = [jax.ShapeDtypeStruct((m, n), out_dtype)]
    if has_scale:
        in_specs.append(pl.BlockSpec((1, tn), lambda i, j: (0, j)))
        args.append(col_scale.reshape(1, n))
    if has_aux:
        na = w_aux.shape[0] if parts is not None else w_aux.shape[1]
        in_specs.append(pl.BlockSpec(w_aux.shape, lambda i, j: (0, 0)))
        out_specs.append(pl.BlockSpec((tm, na), lambda i, j: (i, 0)))
        out_shape.append(jax.ShapeDtypeStruct((m, na), F32))
        args.append(w_aux)
    block_bytes = n_x * tm * d * 4 + max(n_w, 1) * d * tn * 2 + tm * tn * 4 + tm * d
    outs = pl.pallas_call(
        functools.partial(_proj_kernel, n_x=n_x, n_w=n_w, tiles_per_part=tiles_per_part, n_lat_tiles=n_lat_tiles,
                          has_aux=has_aux, has_scale=has_scale),
        grid=(m // tm, n // tn),
        in_specs=in_specs,
        out_specs=out_specs,
        out_shape=out_shape,
        scratch_shapes=[pltpu.VMEM((tm, d), BF16)],
        compiler_params=_params(("arbitrary", "arbitrary"), block_bytes),
        name="proj",
    )(*args)
    return outs if has_aux else outs[0]


def _row_operand(a, tm, n_lat_tiles, width, col_of_j, single_ctx_buffer=False):
    if not isinstance(a, tuple):
        return [pl.BlockSpec((tm, width), lambda i, j: (i, col_of_j(j)))], [a]
    lat, ctx = a
    assert ctx.shape[0] == tm and lat.shape[0] == n_lat_tiles * tm

    def lat_index(i, j):
        return jnp.minimum(i, n_lat_tiles - 1), jnp.where(i < n_lat_tiles, col_of_j(j), 0)

    def ctx_index(i, j):
        return 0, jnp.where(i < n_lat_tiles, 0, col_of_j(j))

    ctx_mode = dict(pipeline_mode=pl.Buffered(1)) if single_ctx_buffer else {}
    return [pl.BlockSpec((tm, width), lat_index), pl.BlockSpec((tm, width), ctx_index, **ctx_mode)], [lat, ctx]


def _pick_rows(refs, n_lat_tiles):
    if len(refs) == 1:
        return refs[0][...]
    return jnp.where(pl.program_id(0) < n_lat_tiles, refs[0][...], refs[1][...])


def _out_proj_kernel(*refs, arity, n_lat_tiles):
    refs = list(refs)
    n_lhs = len(arity) - 1
    groups = [[refs.pop(0) for _ in range(n)] for n in arity[:-1]]
    ws = [refs.pop(0) for _ in range(n_lhs)]
    res = [refs.pop(0) for _ in range(arity[-1])]
    gate_ref, o_ref = refs
    acc = None
    for grp, w_ref in zip(groups, ws):
        part = jnp.dot(_pick_rows(grp, n_lat_tiles), w_ref[...], preferred_element_type=F32)
        acc = part if acc is None else acc + part
    o_ref[...] = _pick_rows(res, n_lat_tiles) + gate_ref[...] * acc


def _out_proj(lhs_list, w, res, mod, layer, gate_idx, m_rows, seq, n_batch, tm=1024, tn=1024):
    d = w.shape[1]
    n_lat_tiles = (n_batch * seq) // tm
    in_specs, args, arity, widths = [], [], [], []
    for a in lhs_list:
        kk = (a[0] if isinstance(a, tuple) else a).shape[1]
        sp, ar = _row_operand(a, tm, n_lat_tiles, kk, lambda j: 0)
        in_specs += sp
        args += ar
        arity.append(len(ar))
        widths.append(kk)
    row0 = 0
    for kk in widths:
        in_specs.append(pl.BlockSpec((kk, tn), lambda i, j, rb=row0 // kk: (rb, j)))
        args.append(w)
        row0 += kk
    sp, ar = _row_operand(res, tm, n_lat_tiles, tn, lambda j: j)
    in_specs += sp
    args += ar
    arity.append(len(ar))
    in_specs.append(pl.BlockSpec((None, None, 1, tn),
                                 lambda i, j: (layer, _mod_row(i, tm, seq, n_batch), 0, gate_idx * (d // tn) + j)))
    args.append(mod)
    k_total = sum(widths)
    block_bytes = 2 * tm * k_total * 2 + k_total * tn * 2 + 3 * tm * tn * 4
    return pl.pallas_call(
        functools.partial(_out_proj_kernel, arity=tuple(arity), n_lat_tiles=n_lat_tiles),
        grid=(m_rows // tm, d // tn),
        in_specs=in_specs,
        out_specs=pl.BlockSpec((tm, tn), lambda i, j: (i, j)),
        out_shape=jax.ShapeDtypeStruct((m_rows, d), F32),
        compiler_params=_params(("arbitrary", "arbitrary"), block_bytes),
        name="out_proj",
    )(*args)


def _ffn_kernel(x_ref, g_ref, mod_ref, w1_ref, w3_ref, w2_ref, fg_ref, o_ref, h_ref, *, final_norm, rows,
                first_step_chunks):
    j = pl.program_id(1)
    tm, d = x_ref.shape

    def swiglu(h, w1, w3, w2):
        a = jnp.dot(h, w1, preferred_element_type=F32)
        b = jnp.dot(h, w3, preferred_element_type=F32)
        return jnp.dot((_silu(a) * b).astype(BF16), w2, preferred_element_type=F32)

    @pl.when(j == 0)
    def _():
        w1, w3, w2 = (w_ref[...].astype(BF16) for w_ref in (w1_ref, w3_ref, w2_ref))

        def consume(rows, h):
            h_ref[rows, :] = h
            o_ref[rows, :] = swiglu(h, w1, w3, w2)

        _norm_mod_pipeline(x_ref, g_ref, mod_ref, 3, 4, first_step_chunks, consume)

    @pl.when(j > 0)
    def _():
        o_ref[...] += swiglu(h_ref[...], w1_ref[...].astype(BF16), w3_ref[...].astype(BF16),
                             w2_ref[...].astype(BF16))

    @pl.when(j == pl.num_programs(1) - 1)
    def _():
        gate = mod_ref[:, 5 * d:6 * d]
        fg = fg_ref[...]

        def body(r, carry):
            sl = pl.ds(pl.multiple_of(r * rows, rows), rows)
            y = x_ref[sl, :] + gate * o_ref[sl, :]
            if final_norm:
                y = y * lax.rsqrt(jnp.mean(y * y, axis=-1, keepdims=True) + NORM_EPS) * fg
            o_ref[sl, :] = y
            return carry

        lax.fori_loop(0, x_ref.shape[0] // rows, body, 0)


def _ffn(x, g, mod, layer, w1, w3, w2, final_g, final_norm, m_rows, seq, n_batch, tm=1024, tf=256):
    d = x.shape[1]
    ff = w1.shape[2]
    block_bytes = 2 * tm * d * 4 + 3 * d * tf * w1.dtype.itemsize + tm * d
    return pl.pallas_call(
        functools.partial(_ffn_kernel, final_norm=final_norm, rows=128, first_step_chunks=4),
        grid=(m_rows // tm, ff // tf),
        in_specs=[
            pl.BlockSpec((tm, d), lambda i, j: (i, 0)),
            pl.BlockSpec((1, d), lambda i, j: (0, 0)),
            pl.BlockSpec((None, None, 1, mod.shape[-1]),
                         lambda i, j: (layer, _mod_row(i, tm, seq, n_batch), 0, 0)),
            pl.BlockSpec((None, d, tf), lambda i, j: (layer, 0, j)),
            pl.BlockSpec((None, d, tf), lambda i, j: (layer, 0, j)),
            pl.BlockSpec((None, tf, d), lambda i, j: (layer, j, 0)),
            pl.BlockSpec((1, d), lambda i, j: (0, 0)),
        ],
        out_specs=pl.BlockSpec((tm, d), lambda i, j: (i, 0)),
        out_shape=jax.ShapeDtypeStruct((m_rows, d), F32),
        scratch_shapes=[pltpu.VMEM((tm, d), BF16)],
        compiler_params=_params(("arbitrary", "arbitrary"), block_bytes),
        name="ffn",
    )(x, g.reshape(1, d), mod, w1, w3, w2, final_g.reshape(1, d))


def _conv_kernel(prev_ref, cur_ref, next_ref, w_ref, b_ref, o_ref, ext_ref, *, tm, halo, seq, ctx_len, n_lat_tiles):
    i = pl.program_id(0)
    ext_ref[0:halo, :] = prev_ref[...]
    ext_ref[halo:halo + tm, :] = cur_ref[...]
    ext_ref[halo + tm:, :] = next_ref[...]
    seg_len = jnp.where(i < n_lat_tiles, seq, ctx_len)
    pos = (i * tm + lax.broadcasted_iota(jnp.int32, (tm, 1), 0)) & (seg_len - 1)
    acc = jnp.zeros(cur_ref.shape, F32) + b_ref[...]
    half = SSM_CONV // 2
    for t in range(SSM_CONV):
        src = pos + (t - half)
        tap = ext_ref[halo - half + t:halo - half + t + tm, :]
        acc = acc + jnp.where((src >= 0) & (src < seg_len), tap, 0.0) * w_ref[t:t + 1, :]
    o_ref[...] = _silu(acc)


def _ssd_conv(p0, col0, width, conv_w, conv_b, seq, ctx_len, n_lat_rows, tm=1024, tc=512, halo=8):
    m = p0.shape[0]
    assert seq % tm == 0 and tm % ctx_len == 0 and (seq & (seq - 1)) == 0 and (ctx_len & (ctx_len - 1)) == 0
    cb0 = col0 // tc
    hb = tm // halo
    last_hb = m // halo - 1
    wpad = jnp.zeros((8, width), F32).at[:SSM_CONV].set(conv_w)
    return pl.pallas_call(
        functools.partial(_conv_kernel, tm=tm, halo=halo, seq=seq, ctx_len=ctx_len, n_lat_tiles=n_lat_rows // tm),
        grid=(m // tm, width // tc),
        in_specs=[
            pl.BlockSpec((halo, tc), lambda i, j: (jnp.maximum(i * hb - 1, 0), cb0 + j)),
            pl.BlockSpec((tm, tc), lambda i, j: (i, cb0 + j)),
            pl.BlockSpec((halo, tc), lambda i, j: (jnp.minimum((i + 1) * hb, last_hb), cb0 + j)),
            pl.BlockSpec((8, tc), lambda i, j: (0, j)),
            pl.BlockSpec((1, tc), lambda i, j: (0, j)),
        ],
        out_specs=pl.BlockSpec((tm, tc), lambda i, j: (i, j)),
        out_shape=jax.ShapeDtypeStruct((m, width), F32),
        scratch_shapes=[pltpu.VMEM((tm + 2 * halo, tc), F32)],
        compiler_params=_params(("arbitrary", "arbitrary"), 3 * tm * tc * 4),
        name="ssd_conv",
    )(p0, p0, p0, wpad, conv_b.reshape(1, width))


def _ssd_scan_kernel(xs_f, bm_f, cm_f, dt_f, xs_b, bm_b, cm_b, dt_b, prm_ref, yf_ref, yb_ref, state_ref, *, n_heads):
    @pl.when(pl.program_id(1) == 0)
    def _():
        state_ref[...] = jnp.zeros_like(state_ref)

    fwd = _ssd_chunk(xs_f, bm_f, cm_f, dt_f, prm_ref.at[0], yf_ref, state_ref.at[0], backward=False, n_heads=n_heads)
    bwd = _ssd_chunk(xs_b, bm_b, cm_b, dt_b, prm_ref.at[1], yb_ref, state_ref.at[1], backward=True, n_heads=n_heads)
    for g in range(SSM_GROUPS):
        fwd(g)
        bwd(g)


def _ssd_chunk(xs_ref, bm_ref, cm_ref, dt_ref, prm_ref, y_ref, state_ref, *, backward, n_heads):
    hp = SSM_HEAD_DIM
    rep = n_heads // SSM_GROUPS
    lc = SSM_CHUNK
    assert 2 * hp == LANES and rep % 2 == 0 and SSM_STATE == LANES and lc == LANES

    dt_in = dt_ref[...] + prm_ref[1:2, :]
    dt = jnp.maximum(dt_in, 0.0) + jnp.log1p(jnp.exp(-jnp.abs(dt_in)))
    dta = dt * (-jnp.exp(prm_ref[0:1, :]) * LOG2E)
    row = lax.broadcasted_iota(jnp.int32, (lc, lc), 0)
    col = lax.broadcasted_iota(jnp.int32, (lc, lc), 1)
    causal = (row <= col) if backward else (row >= col)
    a_cum = jnp.dot(causal.astype(F32), dta, preferred_element_type=F32, precision=lax.Precision.HIGHEST)
    a_tot = a_cum[0:1, :] if backward else a_cum[lc - 1:lc, :]
    w_end = dt * jnp.exp2(a_tot - a_cum)
    a_cum_t, dt_t, w_end_t = a_cum.T, dt.T, w_end.T
    lo = col < hp

    def do_group(g):
        gs = slice(g * SSM_STATE, (g + 1) * SSM_STATE)
        b_f = bm_ref[:, gs]
        c_g = cm_ref[:, gs].astype(BF16)
        cb = lax.dot_general(c_g, b_f.astype(BF16), (((1,), (1,)), ((), ())), preferred_element_type=F32)
        b_t = b_f.T
        cols_g = slice(g * rep * hp, (g + 1) * rep * hp)
        y_off = jnp.dot(c_g, state_ref[:, cols_g].astype(BF16), preferred_element_type=F32)
        for pr in range(rep // 2):
            cols = slice(g * rep * hp + pr * LANES, g * rep * hp + (pr + 1) * LANES)
            xs2 = xs_ref[:, cols].astype(BF16)
            y_d, s_n, e_a = [], [], []
            for h in (g * rep + 2 * pr, g * rep + 2 * pr + 1):
                a_col = jnp.broadcast_to(a_cum[:, h:h + 1], (lc, lc))
                decay = jnp.exp2(jnp.where(causal, a_col - a_cum_t[h:h + 1, :], -jnp.inf))
                m = (cb * decay * dt_t[h:h + 1, :]).astype(BF16)
                y_d.append(jnp.dot(m, xs2, preferred_element_type=F32))
                s_n.append(jnp.dot((b_t * w_end_t[h:h + 1, :]).astype(BF16), xs2, preferred_element_type=F32))
                e_a.append(jnp.exp2(a_col))
            h0 = g * rep + 2 * pr
            y_ref[:, cols] = (jnp.where(lo, y_d[0], y_d[1])
                              + y_off[:, pr * LANES:(pr + 1) * LANES] * jnp.where(lo, e_a[0], e_a[1]))
            chunk_decay = jnp.where(lo[0:1, :], jnp.exp2(a_tot[:, h0:h0 + 1]), jnp.exp2(a_tot[:, h0 + 1:h0 + 2]))
            state_ref[:, cols] = state_ref[:, cols] * chunk_decay + jnp.where(lo, s_n[0], s_n[1])

    return do_group


def _ssd_scan(xbc, dt_raw, prm, n_batch, seq, ctx_len, d_ssm):
    m = xbc.shape[0]
    lc = SSM_CHUNK
    n_heads = d_ssm // SSM_HEAD_DIM
    nc_ctx, nc_lat = ctx_len // lc, seq // lc
    ctx_blk0 = n_batch * nc_lat
    gn = SSM_GROUPS * SSM_STATE

    def row_blk(b, s, backward):
        ctx_c = nc_ctx - 1 - s if backward else s
        lat_c = nc_lat - 1 - (s - nc_ctx) if backward else s - nc_ctx
        return jnp.where(s < nc_ctx, ctx_blk0 + b * nc_ctx + ctx_c, b * nc_lat + lat_c)

    def chunk_specs(backward):
        r = int(backward)
        return [
            pl.BlockSpec((lc, d_ssm), lambda b, s: (row_blk(b, s, backward), 0)),
            pl.BlockSpec((lc, gn), lambda b, s: (row_blk(b, s, backward), d_ssm // gn)),
            pl.BlockSpec((lc, gn), lambda b, s: (row_blk(b, s, backward), d_ssm // gn + 1)),
            pl.BlockSpec((lc, LANES), lambda b, s: (row_blk(b, s, backward), r)),
        ]

    return pl.pallas_call(
        functools.partial(_ssd_scan_kernel, n_heads=n_heads),
        grid=(n_batch, nc_ctx + nc_lat),
        in_specs=chunk_specs(False) + chunk_specs(True) + [pl.BlockSpec((2, 8, LANES), lambda b, s: (0, 0, 0))],
        out_specs=[pl.BlockSpec((lc, d_ssm), lambda b, s: (row_blk(b, s, False), 0)),
                   pl.BlockSpec((lc, d_ssm), lambda b, s: (row_blk(b, s, True), 0))],
        out_shape=[jax.ShapeDtypeStruct((m, d_ssm), F32), jax.ShapeDtypeStruct((m, d_ssm), F32)],
        scratch_shapes=[pltpu.VMEM((2, SSM_STATE, d_ssm), F32)],
        compiler_params=_params(("arbitrary", "arbitrary"), 8 * lc * d_ssm * 4),
        name="ssd_scan",
    )(xbc, xbc, xbc, dt_raw, xbc, xbc, xbc, dt_raw, prm)


def _ssd_finish_kernel(yf_ref, yb_ref, xs_ref, z_ref, dskip_ref, g_ref, o_ref, *, group_width):
    y = (yf_ref[...] + yb_ref[...] + dskip_ref[...] * xs_ref[...]) * _silu(z_ref[...])
    for g in range(y.shape[1] // group_width):
        sl = slice(g * group_width, (g + 1) * group_width)
        v = y[:, sl]
        vn = v * lax.rsqrt(jnp.mean(v * v, axis=-1, keepdims=True) + NORM_EPS) * g_ref[:, sl]
        o_ref[:, sl] = vn.astype(o_ref.dtype)


def _ssd_finish(y2, xbc, p0, d_skip_lanes, norm_g, d_ssm, tm=512):
    m = xbc.shape[0]
    return pl.pallas_call(
        functools.partial(_ssd_finish_kernel, group_width=d_ssm // SSM_GROUPS),
        grid=(m // tm,),
        in_specs=[
            pl.BlockSpec((tm, d_ssm), lambda i: (i, 0)),
            pl.BlockSpec((tm, d_ssm), lambda i: (i, 0)),
            pl.BlockSpec((tm, d_ssm), lambda i: (i, 0)),
            pl.BlockSpec((tm, d_ssm), lambda i: (i, 0)),
            pl.BlockSpec((1, d_ssm), lambda i: (0, 0)),
            pl.BlockSpec((1, d_ssm), lambda i: (0, 0)),
        ],
        out_specs=pl.BlockSpec((tm, d_ssm), lambda i: (i, 0)),
        out_shape=jax.ShapeDtypeStruct((m, d_ssm), BF16),
        compiler_params=_params(("arbitrary",), 5 * tm * d_ssm * 4),
        name="ssd_finish",
    )(*y2, xbc, p0, d_skip_lanes.reshape(1, d_ssm), norm_g.reshape(1, d_ssm))


def _rope(x, cos, sin_up, sin_dn):
    quarter = DIFF_HEAD_DIM // 4
    return (x * cos + pltpu.roll(x, LANES - quarter, axis=1) * sin_up + pltpu.roll(x, quarter, axis=1) * sin_dn)


def _diff_attn_kernel(*refs, with_lat, lambda_init, n_ctx, sub_q):
    if with_lat:
        (q_ref, kc_ref, kl_ref, vc_ref, vl_ref, qcos_ref, qsu_ref, qsd_ref, kcos_ref, ksu_ref, ksd_ref,
         lam_ref, g_ref, o_ref, k_scr, vt_scr) = refs
    else:
        q_ref, kc_ref, vc_ref, lam_ref, g_ref, o_ref, k_scr, vt_scr = refs

    @pl.when(pl.program_id(2) == 0)
    def _():
        k_scr[0:n_ctx, :] = kc_ref[...].astype(BF16)
        vt_scr[:, 0:n_ctx] = vc_ref[...].T.astype(BF16)
        if with_lat:
            k_scr[n_ctx:, :] = _rope(kl_ref[...], kcos_ref[...], ksu_ref[...], ksd_ref[...]).astype(BF16)
            vt_scr[:, n_ctx:] = vl_ref[...].T.astype(BF16)

    lam_p = lam_ref[...]
    lam = (jnp.exp(jnp.sum(lam_p[0:1, :] * lam_p[1:2, :], axis=-1, keepdims=True))
           - jnp.exp(jnp.sum(lam_p[2:3, :] * lam_p[3:4, :], axis=-1, keepdims=True)) + lambda_init)
    first = lax.broadcasted_iota(jnp.int32, (LANES, sub_q), 0) < DIFF_HEAD_DIM
    n_sub = q_ref.shape[0] // sub_q

    def scores(t):
        rows = slice(t * sub_q, (t + 1) * sub_q)
        q = q_ref[rows, :]
        if with_lat:
            q = _rope(q, qcos_ref[rows, :], qsu_ref[rows, :], qsd_ref[rows, :])
        qt = (q * (DIFF_HEAD_DIM ** -0.5 * LOG2E)).T
        q2t = jnp.concatenate([jnp.where(first, qt, 0.0), jnp.where(first, 0.0, qt)], axis=1).astype(BF16)
        return jnp.dot(k_scr[...], q2t, preferred_element_type=F32)

    s_next = scores(0)
    for t in range(n_sub):
        rows = slice(t * sub_q, (t + 1) * sub_q)
        s = s_next
        if t + 1 < n_sub:
            s_next = scores(t + 1)
        e = jnp.exp2(s - jnp.max(s, axis=0, keepdims=True))
        r = 1.0 / jnp.sum(e, axis=0, keepdims=True)
        ovt = jnp.dot(vt_scr[...], e.astype(BF16), preferred_element_type=F32) * r
        ot = ovt[:, :sub_q] - lam * ovt[:, sub_q:]
        ot = ot * lax.rsqrt(jnp.mean(ot * ot, axis=0, keepdims=True) + NORM_EPS) * g_ref[...] * (1.0 - lambda_init)
        o_ref[rows, :] = ot.T.astype(o_ref.dtype)


def _diff_attn(p0, q_col0, k_col0, v_col0, n_heads, rope_tabs, lam_p, subln_g, lambda_init,
               n_batch, seq, ctx_len, with_lat, tq=2048, sub_q=256):
    cb = lambda c0: c0 // LANES
    ctx_rb0 = (n_batch * seq) // ctx_len
    n_keys = ctx_len + (seq if with_lat else 0)
    if with_lat:
        n_q = seq // tq
        q_spec = pl.BlockSpec((tq, LANES), lambda b, h, i: (b * n_q + i, cb(q_col0) + h))
    else:
        tq = ctx_len
        n_q = 1
        q_spec = pl.BlockSpec((tq, LANES), lambda b, h, i: (ctx_rb0 + b, cb(q_col0) + h))
    kc_spec = pl.BlockSpec((ctx_len, LANES), lambda b, h, i: (ctx_rb0 + b, cb(k_col0) + h))
    vc_spec = pl.BlockSpec((ctx_len, LANES), lambda b, h, i: (ctx_rb0 + b, cb(v_col0) + h))
    small =[pl.BlockSpec((8, LANES), lambda b, h, i: (0, 0)), pl.BlockSpec((LANES, sub_q), lambda b, h, i: (0, 0))]
    small_args = [lam_p, jnp.broadcast_to(subln_g[:, None], (LANES, sub_q))]
    if with_lat:
        kl_spec = pl.BlockSpec((seq, LANES), lambda b, h, i: (b, cb(k_col0) + h))
        vl_spec = pl.BlockSpec((seq, LANES), lambda b, h, i: (b, cb(v_col0) + h))
        qt = pl.BlockSpec((tq, LANES), lambda b, h, i: (i, 0))
        kt = pl.BlockSpec((seq, LANES), lambda b, h, i: (0, 0))
        in_specs = [q_spec, kc_spec, kl_spec, vc_spec, vl_spec, qt, qt, qt, kt, kt, kt] + small
        args = [p0, p0, p0, p0, p0, *rope_tabs, *rope_tabs] + small_args
    else:
        in_specs = [q_spec, kc_spec, vc_spec] + small
        args = [p0, p0, p0] + small_args
    block_bytes = 10 * seq * LANES * 4 if with_lat else 8 * ctx_len * LANES * 4
    return pl.pallas_call(
        functools.partial(_diff_attn_kernel, with_lat=with_lat, lambda_init=lambda_init, n_ctx=ctx_len,
                          sub_q=sub_q),
        grid=(n_batch, n_heads, n_q),
        in_specs=in_specs,
        out_specs=pl.BlockSpec((tq, LANES), lambda b, h, i: (b * n_q + i, h)),
        out_shape=jax.ShapeDtypeStruct((n_batch * n_q * tq, n_heads * LANES), BF16),
        scratch_shapes=[pltpu.VMEM((n_keys, LANES), BF16), pltpu.VMEM((LANES, n_keys), BF16)],
        compiler_params=_params(("arbitrary", "arbitrary", "arbitrary"), block_bytes),
        name="diff_attn_lat" if with_lat else "diff_attn_ctx",
    )(*args)


def _rope_tables(seq):
    half = DIFF_HEAD_DIM // 2
    pos = jnp.arange(seq)
    row, col = pos // GRID_W, pos % GRID_W
    inv_freq = ROPE_BASE ** (-jnp.arange(0, half, 2, dtype=F32) / half)
    lane = np.arange(LANES) % DIFF_HEAD_DIM
    use_col = jnp.asarray(lane >= half)
    first = jnp.asarray((lane % half) < half // 2)
    freq = inv_freq[jnp.asarray(lane % (half // 2))]
    p = jnp.where(use_col[None, :], col[:, None], row[:, None]).astype(F32)
    ang = p * freq[None, :]
    cos, sin = jnp.cos(ang), jnp.sin(ang)
    return cos, jnp.where(first[None, :], -sin, 0.0), jnp.where(first[None, :], 0.0, sin)


def _rpb_gather_kernel(rpb_ref, o_ref):
    n = o_ref.shape[1]
    k = rpb_ref.shape[1]
    colid = pl.program_id(0) * n + lax.broadcasted_iota(jnp.int32, (k, n), 1)
    j = lax.broadcasted_iota(jnp.int32, (k, n), 0)
    qc = lax.shift_right_logical(colid, int(math.log2(LANES)))
    half = lax.shift_right_logical(colid & (LANES - 1), int(math.log2(GRID_W)))
    kc = colid & (GRID_W - 1)
    sel = (j == half * LANES + jnp.clip(kc - qc + (WIN_COLS - 1), 0, 2 * WIN_COLS - 2)).astype(F32)
    o_ref[...] = jnp.dot(rpb_ref[...], sel, preferred_element_type=F32, precision=lax.Precision.HIGHEST)


def _rpb_pairs(rpb):
    nh, nr, ncol = rpb.shape
    n_slot = nr + 1
    left = jnp.pad(rpb, ((0, 0), (1, 0), (0, LANES - ncol)))
    right = jnp.pad(rpb, ((0, 0), (0, 1), (0, LANES - ncol)))
    rows = jnp.concatenate([left, right], axis=2).reshape(nh * n_slot, 2 * LANES)
    tn = 1024
    out = pl.pallas_call(
        _rpb_gather_kernel,
        grid=(GRID_W * LANES // tn,),
        in_specs=[pl.BlockSpec((nh * n_slot, 2 * LANES), lambda j: (0, 0))],
        out_specs=pl.BlockSpec((nh * n_slot, tn), lambda j: (0, j)),
        out_shape=jax.ShapeDtypeStruct((nh * n_slot, GRID_W * LANES), F32),
        compiler_params=_params(("arbitrary",), nh * n_slot * tn * 4 + 2 * LANES * tn * 4),
        name="rpb_gather",
    )(rows)
    return out.reshape(nh, n_slot, GRID_W, LANES)


def _na_block_plan(rows):
    kr = min(WIN_ROWS, rows)
    n_blk = rows // NA_Q_ROWS
    kb = np.clip(np.arange(n_blk) * NA_Q_ROWS - kr // 2, 0, rows - NA_K_ROWS)
    layouts, layout_of = [], []
    for blk in range(n_blk):
        dr = np.full((NA_Q_ROWS, NA_K_ROWS), -1, np.int64)
        for i in range(NA_Q_ROWS):
            r = blk * NA_Q_ROWS + i
            rs = int(np.clip(r - kr // 2, 0, rows - kr))
            for j in range(NA_K_ROWS):
                krow = kb[blk] + j
                if rs <= krow < rs + kr:
                    dr[i, j] = krow - r + WIN_ROWS - 1
        key = dr.tobytes()
        if key not in [l.tobytes() for l in layouts]:
            layouts.append(dr)
        layout_of.append([l.tobytes() for l in layouts].index(key))
    return kb, np.stack(layouts), np.asarray(layout_of)


def _na_bias_kernel(tp_ref, o_ref, *, layouts):
    qc = lax.broadcasted_iota(jnp.int32, (GRID_W, LANES), 0)
    lane = lax.broadcasted_iota(jnp.int32, (GRID_W, LANES), 1)
    kc = lane & (GRID_W - 1)
    left = lane < GRID_W
    col_start = jnp.clip(qc - WIN_COLS // 2, 0, GRID_W - WIN_COLS)
    col_ok = (kc >= col_start) & (kc < col_start + WIN_COLS)
    n_lay, n_q, n_k = layouts.shape
    for lay in range(n_lay):
        for i in range(n_q):
            for p in range(-(-n_k // 2)):
                d_l = int(layouts[lay, i, 2 * p])
                d_r = int(layouts[lay, i, 2 * p + 1]) if 2 * p + 1 < n_k else -1
                width = LANES if 2 * p + 1 < n_k else GRID_W
                if d_l < 0 and d_r < 0:
                    tile = jnp.full((GRID_W, LANES), -jnp.inf, F32)
                else:
                    assert d_l < 0 or d_r < 0 or d_r == d_l + 1
                    ok = col_ok
                    if d_l < 0:
                        ok = ok & jnp.logical_not(left)
                    if d_r < 0:
                        ok = ok & left
                    slot = d_r if d_r >= 0 else d_l + 1
                    tile = jnp.where(ok, tp_ref[slot] * LOG2E, -jnp.inf)
                o_ref[lay, i * GRID_W:(i + 1) * GRID_W, p * LANES:p * LANES + width] = tile[:, :width]


def _na_bias(rpb_pairs, layouts):
    nh, n_slot = rpb_pairs.shape[:2]
    n_lay = layouts.shape[0]
    tq, nkw = NA_Q_ROWS * GRID_W, NA_K_ROWS * GRID_W
    return pl.pallas_call(
        functools.partial(_na_bias_kernel, layouts=layouts),
        grid=(nh,),
        in_specs=[pl.BlockSpec((None, n_slot, GRID_W, LANES), lambda h: (h, 0, 0, 0))],
        out_specs=pl.BlockSpec((None, n_lay, tq, nkw), lambda h: (h, 0, 0, 0)),
        out_shape=jax.ShapeDtypeStruct((nh, n_lay, tq, nkw), F32),
        compiler_params=_params(("arbitrary",), n_lay * tq * nkw * 4),
        name="na_bias",
    )(rpb_pairs)


def _na_kernel(q_ref, kl_ref, vl_ref, kc_ref, vc_ref, bias_ref, o_ref, *, plan):
    tq = NA_Q_ROWS * GRID_W
    nkw = NA_K_ROWS * GRID_W
    nt = (((1,), (1,)), ((), ()))
    kc = kc_ref[...]
    vc = vc_ref[...]

    def scores(blk):
        kb, lay = plan[blk]
        q = q_ref[blk * tq:(blk + 1) * tq, :]
        ks = slice(kb * GRID_W, kb * GRID_W + nkw)
        return (lax.dot_general(q, kl_ref[ks, :], nt, preferred_element_type=F32) + bias_ref[lay],
                lax.dot_general(q, kc, nt, preferred_element_type=F32))

    def weighted_values(blk, e_w, e_c, l):
        kb, _ = plan[blk]
        ks = slice(kb * GRID_W, kb * GRID_W + nkw)
        o = (jnp.dot(e_w, vl_ref[ks, :], preferred_element_type=F32) + jnp.dot(e_c, vc, preferred_element_type=F32))
        o_ref[blk * tq:(blk + 1) * tq, :] = (o * (1.0 / l)).astype(o_ref.dtype)

    s_next = scores(0)
    pending = None
    for blk in range(len(plan)):
        s_w, s_c = s_next
        if blk + 1 < len(plan):
            s_next = scores(blk + 1)
        if pending is not None:
            weighted_values(blk - 1, *pending)
        m = jnp.maximum(jnp.max(s_w, axis=-1, keepdims=True), jnp.max(s_c, axis=-1, keepdims=True))
        e_w = jnp.exp2(s_w - m)
        e_c = jnp.exp2(s_c - m)
        l = jnp.sum(e_w, axis=-1, keepdims=True) + jnp.sum(e_c, axis=-1, keepdims=True)
        pending = (e_w.astype(BF16), e_c.astype(BF16), l)
    weighted_values(len(plan) - 1, *pending)


def _na_attn(p1, rpb, n_batch, seq, ctx_len, n_heads):
    rows = seq // GRID_W
    kb, layouts, layout_of = _na_block_plan(rows)
    bias = _na_bias(_rpb_pairs(rpb), layouts)
    n_lay = layouts.shape[0]
    tq = NA_Q_ROWS * GRID_W
    nkw = NA_K_ROWS * GRID_W
    ctx_rb0 = (n_batch * seq) // ctx_len
    plan = tuple((int(k), int(l)) for k, l in zip(kb, layout_of))
    return pl.pallas_call(
        functools.partial(_na_kernel, plan=plan),
        grid=(n_heads, n_batch),
        in_specs=[
            pl.BlockSpec((seq, LANES), lambda h, b: (b, h)),
            pl.BlockSpec((seq, LANES), lambda h, b: (b, n_heads + h)),
            pl.BlockSpec((seq, LANES), lambda h, b: (b, 2 * n_heads + h)),
            pl.BlockSpec((ctx_len, LANES), lambda h, b: (ctx_rb0 + b, n_heads + h)),
            pl.BlockSpec((ctx_len, LANES), lambda h, b: (ctx_rb0 + b, 2 * n_heads + h)),
            pl.BlockSpec((None, n_lay, tq, nkw), lambda h, b: (h, 0, 0, 0)),
        ],
        out_specs=pl.BlockSpec((seq, LANES), lambda h, b: (b, h)),
        out_shape=jax.ShapeDtypeStruct((n_batch * seq, n_heads * LANES), BF16),
        compiler_params=_params(("arbitrary", "arbitrary"), n_lay * tq * nkw * 4 + 4 * seq * LANES * 2),
        name="na_attn",
    )(p1, p1, p1, p1, p1, bias)


def _pack_rows(rows, n_rows=8):
    out = jnp.zeros((n_rows, LANES), F32)
    for r, v in enumerate(rows):
        out = out.at[r, :v.shape[0]].set(v.astype(F32))
    return out


def kernel(x, c, ctx, c_ctx, ada_w, ada_b, norm_mix_g, norm_ffn_g, final_norm_g, ffn_w1, ffn_w3, ffn_w2, ev_w_in, ev_conv_w, ev_conv_b, ev_a_log, ev_dt_bias, ev_d_skip, ev_ssm_norm_g, ev_lam_q1, ev_lam_k1, ev_lam_q2, ev_lam_k2, ev_subln_g, ev_w_out, od_w_in, od_rpb, od_w_out):
    n_batch, seq, d = x.shape
    ctx_len = ctx.shape[1]
    depth = ada_w.shape[0]
    n_lat = n_batch * seq
    n_tok = n_lat + n_batch * ctx_len

    d_ssm = ev_ssm_norm_g.shape[1]
    n_ssm_heads = ev_d_skip.shape[1]
    d_xbc = ev_conv_w.shape[2]
    d_qk = d_v = (ev_w_in.shape[2] - d_ssm - d_xbc - 2 * n_ssm_heads) // 3
    n_diff_heads = d_v // (2 * DIFF_HEAD_DIM)
    n_na_heads = od_rpb.shape[1]

    stream = (x.reshape(n_lat, d), ctx.reshape(n_batch * ctx_len, d))
    cond = jnp.zeros((COND_ROWS, d), F32).at[:n_batch].set(c).at[n_batch].set(c_ctx)
    mod = _ada_mod(cond, ada_w, ada_b).reshape(depth, COND_ROWS, 1, 6 * d)
    rope_tabs = _rope_tables(seq)
    kw = dict(seq=seq, n_batch=n_batch)
    ffn_w = (ffn_w1, ffn_w3, ffn_w2)

    for i in range(depth):
        ctx_out = i < depth - 1
        m_rows = n_tok if ctx_out else n_lat
        j = i // 2
        if i % 2 == 0:
            lambda_init = 0.8 - 0.6 * math.exp(-0.3 * i)
            w_in = ev_w_in[j]
            dt0 = d_ssm + d_xbc
            w_t = jnp.swapaxes(w_in, 0, 1)
            w_parts = [w_t[:dt0].astype(BF16), w_t[dt0 + 2 * n_ssm_heads:].astype(BF16)]
            w_dt = jnp.zeros((2 * LANES, d), F32)
            w_dt = w_dt.at[:n_ssm_heads].set(w_t[dt0:dt0 + n_ssm_heads])
            w_dt = w_dt.at[LANES:LANES + n_ssm_heads].set(w_t[dt0 + n_ssm_heads:dt0 + 2 * n_ssm_heads])
            p0, dt_raw = _proj(stream, norm_mix_g[i], mod, i, w_parts, w_dt.astype(BF16), F32, **kw)
            q0 = d_ssm + d_xbc
            k0, v0 = q0 + d_qk, q0 + 2 * d_qk

            xbc = _ssd_conv(p0, d_ssm, d_xbc, ev_conv_w[j], ev_conv_b[j], seq, ctx_len, n_lat)
            prm = jnp.stack([_pack_rows([ev_a_log[j, r], ev_dt_bias[j, r]]) for r in range(2)])
            y2 = _ssd_scan(xbc, dt_raw, prm, n_batch, seq, ctx_len, d_ssm)
            mix_ssd = _ssd_finish(y2, xbc, p0, jnp.repeat(ev_d_skip[j], SSM_HEAD_DIM), ev_ssm_norm_g[j], d_ssm)

            lam_p = _pack_rows([ev_lam_q1[j], ev_lam_k1[j], ev_lam_q2[j], ev_lam_k2[j]])
            attn_args = (p0, q0, k0, v0, n_diff_heads, rope_tabs, lam_p, ev_subln_g[j], lambda_init,
                         n_batch, seq, ctx_len)
            mix_attn = _diff_attn(*attn_args, with_lat=True)
            if ctx_out:
                mix_attn = (mix_attn, _diff_attn(*attn_args, with_lat=False))
            lhs = [mix_ssd, mix_attn]
            w_out = ev_w_out[j].astype(BF16)
        else:
            d_na = n_na_heads * NA_HEAD_DIM
            q_scale = jnp.where(jnp.arange(3 * d_na) < d_na, NA_HEAD_DIM ** -0.5 * LOG2E, 1.0).astype(F32)
            p1 = _proj(stream, norm_mix_g[i], mod, i, od_w_in[j].astype(BF16), None, BF16, col_scale=q_scale, **kw)
            assert not ctx_out, "context-query neighbourhood layers are not needed at this depth"
            lhs = [_na_attn(p1, od_rpb[j], n_batch, seq, ctx_len, n_na_heads)]
            w_out = od_w_out[j].astype(BF16)
        stream = _out_proj(lhs, w_out, stream, mod, i, 2, m_rows, **kw)
        stream = _ffn(stream, norm_ffn_g[i], mod, i, *ffn_w, final_norm_g, not ctx_out, m_rows, **kw)
    return stream.reshape(n_batch, seq, d)
```

```python
import functools
import math

import jax
import jax.numpy as jnp
import numpy as np
from jax import lax
from jax.experimental import pallas as pl
from jax.experimental.pallas import tpu as pltpu

F32 = jnp.float32
BF16 = jnp.bfloat16

GRID_W = 64
SSM_HEAD_DIM = 64
SSM_GROUPS = 4
SSM_STATE = 128
SSM_CONV = 5
SSM_CHUNK = 128
DIFF_HEAD_DIM = 64
NA_HEAD_DIM = 128
WIN_ROWS = 8
WIN_COLS = 16
ROPE_BASE = 10000.0
NORM_EPS = 1e-6
LOG2E = math.log2(math.e)

LANES = 128
V7X_VMEM_BYTES = 64 * 1024 * 1024
VMEM_HEADROOM_BYTES = 3 * 1024 * 1024

NA_Q_ROWS = 4
NA_K_ROWS = NA_Q_ROWS + WIN_ROWS - 1
COND_ROWS = 16


def _vmem_limit(block_bytes):
    return int(min(V7X_VMEM_BYTES - VMEM_HEADROOM_BYTES, max(32 * 1024 * 1024, 2 * block_bytes + 16 * 1024 * 1024)))


def _params(semantics, block_bytes):
    return pltpu.CompilerParams(dimension_semantics=semantics, vmem_limit_bytes=_vmem_limit(block_bytes))


def _silu(v):
    return v * jax.nn.sigmoid(v)


def _mod_row(i, tm, seq, n_batch):
    return jnp.minimum((i * tm) // seq, n_batch)


def _ada_kernel(cond_ref, w_ref, b_ref, o_ref):
    s = _silu(cond_ref[...]).astype(BF16)
    o_ref[...] = jnp.dot(s, w_ref[...].astype(BF16), preferred_element_type=F32) + b_ref[...]


def _ada_mod(cond, ada_w, ada_b, tn=1024):
    depth, d, n = ada_w.shape
    return pl.pallas_call(
        _ada_kernel,
        grid=(depth, n // tn),
        in_specs=[
            pl.BlockSpec((COND_ROWS, d), lambda l, j: (0, 0)),
            pl.BlockSpec((None, d, tn), lambda l, j: (l, 0, j)),
            pl.BlockSpec((None, 1, tn), lambda l, j: (l, 0, j)),
        ],
        out_specs=pl.BlockSpec((None, COND_ROWS, tn), lambda l, j: (l, 0, j)),
        out_shape=jax.ShapeDtypeStruct((depth, COND_ROWS, n), F32),
        compiler_params=_params(("arbitrary", "arbitrary"), d * tn * 4),
        name="ada_mod",
    )(cond, ada_w, ada_b.reshape(depth, 1, n))


def _norm_mod_pipeline(x_ref, g_ref, mod_ref, shift_idx, scale_idx, n_chunks, consume):
    tm, d = x_ref.shape
    shift = mod_ref[:, shift_idx * d:(shift_idx + 1) * d]
    gain = g_ref[...] * (1.0 + mod_ref[:, scale_idx * d:(scale_idx + 1) * d])
    chunk = tm // n_chunks

    def norm_rows(c):
        x = x_ref[c * chunk:(c + 1) * chunk, :]
        inv = lax.rsqrt(jnp.mean(x * x, axis=-1, keepdims=True) + NORM_EPS)
        return (x * inv * gain + shift).astype(BF16)

    h_next = norm_rows(0)
    for c in range(n_chunks):
        h = h_next
        if c + 1 < n_chunks:
            h_next = norm_rows(c + 1)
        consume(slice(c * chunk, (c + 1) * chunk), h)


def _proj_kernel(*refs, n_x, n_w, tiles_per_part, n_lat_tiles, has_aux, has_scale, first_step_chunks=4):
    rest = list(refs)
    x_refs = [rest.pop(0) for _ in range(n_x)]
    g_ref, mod_ref = rest.pop(0), rest.pop(0)
    w_refs = [rest.pop(0) for _ in range(max(n_w, 1))]
    cs_ref = rest.pop(0) if has_scale else None
    if has_aux:
        waux_ref, o_ref, oaux_ref, h_ref = rest
    else:
        o_ref, h_ref = rest
    dims = (((1,), (1,)), ((), ())) if n_w else (((1,), (0,)), ((), ()))
    j = pl.program_id(1)
    tm, d = x_refs[0].shape

    def emit(h, rows, w_ref):
        acc = lax.dot_general(h, w_ref[...], dims, preferred_element_type=F32)
        if has_scale:
            acc = acc * cs_ref[...]
        o_ref[rows, :] = acc.astype(o_ref.dtype)

    def first_tile(x_ref):
        def consume(rows, h):
            h_ref[rows, :] = h
            emit(h, rows, w_refs[0])
            if has_aux:
                oaux_ref[rows, :] = lax.dot_general(h, waux_ref[...], dims, preferred_element_type=F32)

        _norm_mod_pipeline(x_ref, g_ref, mod_ref, 0, 1, first_step_chunks, consume)

    if n_x == 1:
        pl.when(j == 0)(functools.partial(first_tile, x_refs[0]))
    else:
        is_lat = pl.program_id(0) < n_lat_tiles
        pl.when(jnp.logical_and(j == 0, is_lat))(functools.partial(first_tile, x_refs[0]))
        pl.when(jnp.logical_and(j == 0, jnp.logical_not(is_lat)))(functools.partial(first_tile, x_refs[1]))

    for p in range(max(n_w, 1)):
        lo = max(p * tiles_per_part, 1)
        in_part = (j >= lo) if n_w == 0 else jnp.logical_and(j >= lo, j < (p + 1) * tiles_per_part)
        pl.when(in_part)(lambda p=p: emit(h_ref[...], slice(None), w_refs[p]))


def _proj(x, g, mod, layer, w, w_aux, out_dtype, seq, n_batch, col_scale=None, tm=1024, tn=1024):
    parts = list(w) if isinstance(w, (list, tuple)) else None
    if parts is None:
        d, n = w.shape
        n_w, tiles_per_part = 0, 0
    else:
        d = parts[0].shape[1]
        n_w, tiles_per_part = len(parts), parts[0].shape[0] // tn
        assert all(p.shape == (tiles_per_part * tn, d) for p in parts)
        n = n_w * tiles_per_part * tn
    m = sum(a.shape[0] for a in x) if isinstance(x, tuple) else x.shape[0]
    n_lat_tiles = (n_batch * seq) // tm
    has_aux = w_aux is not None
    has_scale = col_scale is not None
    mod_spec = pl.BlockSpec((None, None, 1, mod.shape[-1]),
                            lambda i, j: (layer, _mod_row(i, tm, seq, n_batch), 0, 0))
    in_specs, args = _row_operand(x, tm, n_lat_tiles, d, lambda j: 0, single_ctx_buffer=True)
    n_x = len(args)
    in_specs += [pl.BlockSpec((1, d), lambda i, j: (0, 0)), mod_spec]
    args += [g.reshape(1, d), mod]
    if parts is None:
        in_specs.append(pl.BlockSpec((d, tn), lambda i, j: (0, j)))
        args.append(w)
    else:
        for p, part in enumerate(parts):
            in_specs.append(pl.BlockSpec(
                (tn, d), lambda i, j, p=p: (jnp.clip(j - p * tiles_per_part, 0, tiles_per_part - 1), 0)))
            args.append(part)
    out_specs = [pl.BlockSpec((tm, tn), lambda i, j: (i, j))]
    out_shape = [jax.ShapeDtypeStruct((m, n), out_dtype)]
    if has_scale:
        in_specs.append(pl.BlockSpec((1, tn), lambda i, j: (0, j)))
        args.append(col_scale.reshape(1, n))
    if has_aux:
        na = w_aux.shape[0] if parts is not None else w_aux.shape[1]
        in_specs.append(pl.BlockSpec(w_aux.shape, lambda i, j: (0, 0)))
        out_specs.append(pl.BlockSpec((tm, na), lambda i, j: (i, 0)))
        out_shape.append(jax.ShapeDtypeStruct((m, na), F32))
        args.append(w_aux)
    block_bytes = n_x * tm * d * 4 + max(n_w, 1) * d * tn * 2 + tm * tn * 4 + tm * d
    outs = pl.pallas_call(
        functools.partial(_proj_kernel, n_x=n_x, n_w=n_w, tiles_per_part=tiles_per_part, n_lat_tiles=n_lat_tiles,
                          has_aux=has_aux, has_scale=has_scale),
        grid=(m // tm, n // tn),
        in_specs=in_specs,
        out_specs=out_specs,
        out_shape=out_shape,
        scratch_shapes=[pltpu.VMEM((tm, d), BF16)],
        compiler_params=_params(("arbitrary", "arbitrary"), block_bytes),
        name="proj",
    )(*args)
    return outs if has_aux else outs[0]


def _row_operand(a, tm, n_lat_tiles, width, col_of_j, single_ctx_buffer=False):
    if not isinstance(a, tuple):
        return [pl.BlockSpec((tm, width), lambda i, j: (i, col_of_j(j)))], [a]
    lat, ctx = a
    assert ctx.shape[0] == tm and lat.shape[0] == n_lat_tiles * tm

    def lat_index(i, j):
        return jnp.minimum(i, n_lat_tiles - 1), jnp.where(i < n_lat_tiles, col_of_j(j), 0)

    def ctx_index(i, j):
        return 0, jnp.where(i < n_lat_tiles, 0, col_of_j(j))

    ctx_mode = dict(pipeline_mode=pl.Buffered(1)) if single_ctx_buffer else {}
    return [pl.BlockSpec((tm, width), lat_index), pl.BlockSpec((tm, width), ctx_index, **ctx_mode)], [lat, ctx]


def _pick_rows(refs, n_lat_tiles):
    if len(refs) == 1:
        return refs[0][...]
    return jnp.where(pl.program_id(0) < n_lat_tiles, refs[0][...], refs[1][...])


def _out_proj_kernel(*refs, arity, n_lat_tiles):
    refs = list(refs)
    n_lhs = len(arity) - 1
    groups = [[refs.pop(0) for _ in range(n)] for n in arity[:-1]]
    ws = [refs.pop(0) for _ in range(n_lhs)]
    res = [refs.pop(0) for _ in range(arity[-1])]
    gate_ref, o_ref = refs
    acc = None
    for grp, w_ref in zip(groups, ws):
        part = jnp.dot(_pick_rows(grp, n_lat_tiles), w_ref[...], preferred_element_type=F32)
        acc = part if acc is None else acc + part
    o_ref[...] = _pick_rows(res, n_lat_tiles) + gate_ref[...] * acc


def _out_proj(lhs_list, w, res, mod, layer, gate_idx, m_rows, seq, n_batch, tm=1024, tn=1024):
    d = w.shape[1]
    n_lat_tiles = (n_batch * seq) // tm
    in_specs, args, arity, widths = [], [], [], []
    for a in lhs_list:
        kk = (a[0] if isinstance(a, tuple) else a).shape[1]
        sp, ar = _row_operand(a, tm, n_lat_tiles, kk, lambda j: 0)
        in_specs += sp
        args += ar
        arity.append(len(ar))
        widths.append(kk)
    row0 = 0
    for kk in widths:
        in_specs.append(pl.BlockSpec((kk, tn), lambda i, j, rb=row0 // kk: (rb, j)))
        args.append(w)
        row0 += kk
    sp, ar = _row_operand(res, tm, n_lat_tiles, tn, lambda j: j)
    in_specs += sp
    args += ar
    arity.append(len(ar))
    in_specs.append(pl.BlockSpec((None, None, 1, tn),
                                 lambda i, j: (layer, _mod_row(i, tm, seq, n_batch), 0, gate_idx * (d // tn) + j)))
    args.append(mod)
    k_total = sum(widths)
    block_bytes = 2 * tm * k_total * 2 + k_total * tn * 2 + 3 * tm * tn * 4
    return pl.pallas_call(
        functools.partial(_out_proj_kernel, arity=tuple(arity), n_lat_tiles=n_lat_tiles),
        grid=(m_rows // tm, d // tn),
        in_specs=in_specs,
        out_specs=pl.BlockSpec((tm, tn), lambda i, j: (i, j)),
        out_shape=jax.ShapeDtypeStruct((m_rows, d), F32),
        compiler_params=_params(("arbitrary", "arbitrary"), block_bytes),
        name="out_proj",
    )(*args)


def _ffn_kernel(x_ref, g_ref, mod_ref, w1_ref, w3_ref, w2_ref, fg_ref, o_ref, h_ref, *, final_norm, rows,
                first_step_chunks):
    j = pl.program_id(1)
    tm, d = x_ref.shape

    def swiglu(h, w1, w3, w2):
        a = jnp.dot(h, w1, preferred_element_type=F32)
        b = jnp.dot(h, w3, preferred_element_type=F32)
        return jnp.dot((_silu(a) * b).astype(BF16), w2, preferred_element_type=F32)

    @pl.when(j == 0)
    def _():
        w1, w3, w2 = (w_ref[...].astype(BF16) for w_ref in (w1_ref, w3_ref, w2_ref))

        def consume(rows, h):
            h_ref[rows, :] = h
            o_ref[rows, :] = swiglu(h, w1, w3, w2)

        _norm_mod_pipeline(x_ref, g_ref, mod_ref, 3, 4, first_step_chunks, consume)

    @pl.when(j > 0)
    def _():
        o_ref[...] += swiglu(h_ref[...], w1_ref[...].astype(BF16), w3_ref[...].astype(BF16),
                             w2_ref[...].astype(BF16))

    @pl.when(j == pl.num_programs(1) - 1)
    def _():
        gate = mod_ref[:, 5 * d:6 * d]
        fg = fg_ref[...]

        def body(r, carry):
            sl = pl.ds(pl.multiple_of(r * rows, rows), rows)
            y = x_ref[sl, :] + gate * o_ref[sl, :]
            if final_norm:
                y = y * lax.rsqrt(jnp.mean(y * y, axis=-1, keepdims=True) + NORM_EPS) * fg
            o_ref[sl, :] = y
            return carry

        lax.fori_loop(0, x_ref.shape[0] // rows, body, 0)


def _ffn(x, g, mod, layer, w1, w3, w2, final_g, final_norm, m_rows, seq, n_batch, tm=1024, tf=512):
    d = x.shape[1]
    ff = w1.shape[2]
    block_bytes = 2 * tm * d * 4 + d * tf * sum(w.dtype.itemsize for w in (w1, w3, w2)) + tm * d
    return pl.pallas_call(
        functools.partial(_ffn_kernel, final_norm=final_norm, rows=128, first_step_chunks=4),
        grid=(m_rows // tm, ff // tf),
        in_specs=[
            pl.BlockSpec((tm, d), lambda i, j: (i, 0)),
            pl.BlockSpec((1, d), lambda i, j: (0, 0)),
            pl.BlockSpec((None, None, 1, mod.shape[-1]),
                         lambda i, j: (layer, _mod_row(i, tm, seq, n_batch), 0, 0)),
            pl.BlockSpec((None, d, tf), lambda i, j: (layer, 0, j)),
            pl.BlockSpec((None, d, tf), lambda i, j: (layer, 0, j)),
            pl.BlockSpec((None, tf, d), lambda i, j: (layer, j, 0)),
            pl.BlockSpec((1, d), lambda i, j: (0, 0)),
        ],
        out_specs=pl.BlockSpec((tm, d), lambda i, j: (i, 0)),
        out_shape=jax.ShapeDtypeStruct((m_rows, d), F32),
        scratch_shapes=[pltpu.VMEM((tm, d), BF16)],
        compiler_params=_params(("arbitrary", "arbitrary"), block_bytes),
        name="ffn",
    )(x, g.reshape(1, d), mod, w1, w3, w2, final_g.reshape(1, d))


def _conv_kernel(prev_ref, cur_ref, next_ref, w_ref, b_ref, o_ref, ext_ref, *, tm, halo, seq, ctx_len, n_lat_tiles):
    i = pl.program_id(0)
    ext_ref[0:halo, :] = prev_ref[...]
    ext_ref[halo:halo + tm, :] = cur_ref[...]
    ext_ref[halo + tm:, :] = next_ref[...]
    seg_len = jnp.where(i < n_lat_tiles, seq, ctx_len)
    pos = (i * tm + lax.broadcasted_iota(jnp.int32, (tm, 1), 0)) & (seg_len - 1)
    acc = jnp.zeros(cur_ref.shape, F32) + b_ref[...]
    half = SSM_CONV // 2
    for t in range(SSM_CONV):
        src = pos + (t - half)
        tap = ext_ref[halo - half + t:halo - half + t + tm, :]
        acc = acc + jnp.where((src >= 0) & (src < seg_len), tap, 0.0) * w_ref[t:t + 1, :]
    o_ref[...] = _silu(acc)


def _ssd_conv(p0, col0, width, conv_w, conv_b, seq, ctx_len, n_lat_rows, tm=1024, tc=512, halo=8):
    m = p0.shape[0]
    assert seq % tm == 0 and tm % ctx_len == 0 and (seq & (seq - 1)) == 0 and (ctx_len & (ctx_len - 1)) == 0
    cb0 = col0 // tc
    hb = tm // halo
    last_hb = m // halo - 1
    wpad = jnp.zeros((8, width), F32).at[:SSM_CONV].set(conv_w)
    return pl.pallas_call(
        functools.partial(_conv_kernel, tm=tm, halo=halo, seq=seq, ctx_len=ctx_len, n_lat_tiles=n_lat_rows // tm),
        grid=(m // tm, width // tc),
        in_specs=[
            pl.BlockSpec((halo, tc), lambda i, j: (jnp.maximum(i * hb - 1, 0), cb0 + j)),
            pl.BlockSpec((tm, tc), lambda i, j: (i, cb0 + j)),
            pl.BlockSpec((halo, tc), lambda i, j: (jnp.minimum((i + 1) * hb, last_hb), cb0 + j)),
            pl.BlockSpec((8, tc), lambda i, j: (0, j)),
            pl.BlockSpec((1, tc), lambda i, j: (0, j)),
        ],
        out_specs=pl.BlockSpec((tm, tc), lambda i, j: (i, j)),
        out_shape=jax.ShapeDtypeStruct((m, width), F32),
        scratch_shapes=[pltpu.VMEM((tm + 2 * halo, tc), F32)],
        compiler_params=_params(("arbitrary", "arbitrary"), 3 * tm * tc * 4),
        name="ssd_conv",
    )(p0, p0, p0, wpad, conv_b.reshape(1, width))


def _ssd_scan_kernel(xs_f, bm_f, cm_f, dt_f, xs_b, bm_b, cm_b, dt_b, prm_ref, yf_ref, yb_ref, state_ref, *, n_heads):
    @pl.when(pl.program_id(1) == 0)
    def _():
        state_ref[...] = jnp.zeros_like(state_ref)

    fwd = _ssd_chunk(xs_f, bm_f, cm_f, dt_f, prm_ref.at[0], yf_ref, state_ref.at[0], backward=False, n_heads=n_heads)
    bwd = _ssd_chunk(xs_b, bm_b, cm_b, dt_b, prm_ref.at[1], yb_ref, state_ref.at[1], backward=True, n_heads=n_heads)
    for g in range(SSM_GROUPS):
        fwd(g)
        bwd(g)


def _ssd_chunk(xs_ref, bm_ref, cm_ref, dt_ref, prm_ref, y_ref, state_ref, *, backward, n_heads):
    hp = SSM_HEAD_DIM
    rep = n_heads // SSM_GROUPS
    lc = SSM_CHUNK
    assert 2 * hp == LANES and rep % 2 == 0 and SSM_STATE == LANES and lc == LANES

    dt_in = dt_ref[...] + prm_ref[1:2, :]
    dt = jnp.maximum(dt_in, 0.0) + jnp.log1p(jnp.exp(-jnp.abs(dt_in)))
    dta = dt * (-jnp.exp(prm_ref[0:1, :]) * LOG2E)
    row = lax.broadcasted_iota(jnp.int32, (lc, lc), 0)
    col = lax.broadcasted_iota(jnp.int32, (lc, lc), 1)
    causal = (row <= col) if backward else (row >= col)
    a_cum = jnp.dot(causal.astype(F32), dta, preferred_element_type=F32, precision=lax.Precision.HIGHEST)
    a_tot = a_cum[0:1, :] if backward else a_cum[lc - 1:lc, :]
    w_end = dt * jnp.exp2(a_tot - a_cum)
    a_cum_t, dt_t, w_end_t = a_cum.T, dt.T, w_end.T
    lo = col < hp

    def do_group(g):
        gs = slice(g * SSM_STATE, (g + 1) * SSM_STATE)
        b_f = bm_ref[:, gs]
        c_g = cm_ref[:, gs].astype(BF16)
        cb = lax.dot_general(c_g, b_f.astype(BF16), (((1,), (1,)), ((), ())), preferred_element_type=F32)
        b_t = b_f.T
        cols_g = slice(g * rep * hp, (g + 1) * rep * hp)
        y_off = jnp.dot(c_g, state_ref[:, cols_g].astype(BF16), preferred_element_type=F32)
        for pr in range(rep // 2):
            cols = slice(g * rep * hp + pr * LANES, g * rep * hp + (pr + 1) * LANES)
            xs2 = xs_ref[:, cols].astype(BF16)
            y_d, s_n, e_a = [], [], []
            for h in (g * rep + 2 * pr, g * rep + 2 * pr + 1):
                a_col = jnp.broadcast_to(a_cum[:, h:h + 1], (lc, lc))
                decay = jnp.exp2(jnp.where(causal, a_col - a_cum_t[h:h + 1, :], -jnp.inf))
                m = (cb * decay * dt_t[h:h + 1, :]).astype(BF16)
                y_d.append(jnp.dot(m, xs2, preferred_element_type=F32))
                s_n.append(jnp.dot((b_t * w_end_t[h:h + 1, :]).astype(BF16), xs2, preferred_element_type=F32))
                e_a.append(jnp.exp2(a_col))
            h0 = g * rep + 2 * pr
            y_ref[:, cols] = (jnp.where(lo, y_d[0], y_d[1])
                              + y_off[:, pr * LANES:(pr + 1) * LANES] * jnp.where(lo, e_a[0], e_a[1]))
            chunk_decay = jnp.where(lo[0:1, :], jnp.exp2(a_tot[:, h0:h0 + 1]), jnp.exp2(a_tot[:, h0 + 1:h0 + 2]))
            state_ref[:, cols] = state_ref[:, cols] * chunk_decay + jnp.where(lo, s_n[0], s_n[1])

    return do_group


def _ssd_scan(xbc, dt_raw, prm, n_batch, seq, ctx_len, d_ssm):
    m = xbc.shape[0]
    lc = SSM_CHUNK
    n_heads = d_ssm // SSM_HEAD_DIM
    nc_ctx, nc_lat = ctx_len // lc, seq // lc
    ctx_blk0 = n_batch * nc_lat
    gn = SSM_GROUPS * SSM_STATE

    def row_blk(b, s, backward):
        ctx_c = nc_ctx - 1 - s if backward else s
        lat_c = nc_lat - 1 - (s - nc_ctx) if backward else s - nc_ctx
        return jnp.where(s < nc_ctx, ctx_blk0 + b * nc_ctx + ctx_c, b * nc_lat + lat_c)

    def chunk_specs(backward):
        r = int(backward)
        return [
            pl.BlockSpec((lc, d_ssm), lambda b, s: (row_blk(b, s, backward), 0)),
            pl.BlockSpec((lc, gn), lambda b, s: (row_blk(b, s, backward), d_ssm // gn)),
            pl.BlockSpec((lc, gn), lambda b, s: (row_blk(b, s, backward), d_ssm // gn + 1)),
            pl.BlockSpec((lc, LANES), lambda b, s: (row_blk(b, s, backward), r)),
        ]

    return pl.pallas_call(
        functools.partial(_ssd_scan_kernel, n_heads=n_heads),
        grid=(n_batch, nc_ctx + nc_lat),
        in_specs=chunk_specs(False) + chunk_specs(True) + [pl.BlockSpec((2, 8, LANES), lambda b, s: (0, 0, 0))],
        out_specs=[pl.BlockSpec((lc, d_ssm), lambda b, s: (row_blk(b, s, False), 0)),
                   pl.BlockSpec((lc, d_ssm), lambda b, s: (row_blk(b, s, True), 0))],
        out_shape=[jax.ShapeDtypeStruct((m, d_ssm), F32), jax.ShapeDtypeStruct((m, d_ssm), F32)],
        scratch_shapes=[pltpu.VMEM((2, SSM_STATE, d_ssm), F32)],
        compiler_params=_params(("arbitrary", "arbitrary"), 8 * lc * d_ssm * 4),
        name="ssd_scan",
    )(xbc, xbc, xbc, dt_raw, xbc, xbc, xbc, dt_raw, prm)


def _ssd_finish_kernel(yf_ref, yb_ref, xs_ref, z_ref, dskip_ref, g_ref, o_ref, *, group_width):
    y = (yf_ref[...] + yb_ref[...] + dskip_ref[...] * xs_ref[...]) * _silu(z_ref[...])
    for g in range(y.shape[1] // group_width):
        sl = slice(g * group_width, (g + 1) * group_width)
        v = y[:, sl]
        vn = v * lax.rsqrt(jnp.mean(v * v, axis=-1, keepdims=True) + NORM_EPS) * g_ref[:, sl]
        o_ref[:, sl] = vn.astype(o_ref.dtype)


def _ssd_finish(y2, xbc, p0, d_skip_lanes, norm_g, d_ssm, tm=512):
    m = xbc.shape[0]
    return pl.pallas_call(
        functools.partial(_ssd_finish_kernel, group_width=d_ssm // SSM_GROUPS),
        grid=(m // tm,),
        in_specs=[
            pl.BlockSpec((tm, d_ssm), lambda i: (i, 0)),
            pl.BlockSpec((tm, d_ssm), lambda i: (i, 0)),
            pl.BlockSpec((tm, d_ssm), lambda i: (i, 0)),
            pl.BlockSpec((tm, d_ssm), lambda i: (i, 0)),
            pl.BlockSpec((1, d_ssm), lambda i: (0, 0)),
            pl.BlockSpec((1, d_ssm), lambda i: (0, 0)),
        ],
        out_specs=pl.BlockSpec((tm, d_ssm), lambda i: (i, 0)),
        out_shape=jax.ShapeDtypeStruct((m, d_ssm), BF16),
        compiler_params=_params(("arbitrary",), 5 * tm * d_ssm * 4),
        name="ssd_finish",
    )(*y2, xbc, p0, d_skip_lanes.reshape(1, d_ssm), norm_g.reshape(1, d_ssm))


def _rope(x, cos, sin_up, sin_dn):
    quarter = DIFF_HEAD_DIM // 4
    return (x * cos + pltpu.roll(x, LANES - quarter, axis=1) * sin_up + pltpu.roll(x, quarter, axis=1) * sin_dn)


def _diff_attn_kernel(*refs, with_lat, lambda_init, n_ctx, sub_q):
    if with_lat:
        (q_ref, kc_ref, kl_ref, vc_ref, vl_ref, qcos_ref, qsu_ref, qsd_ref, kcos_ref, ksu_ref, ksd_ref,
         lam_ref, g_ref, o_ref, k_scr, vt_scr) = refs
    else:
        q_ref, kc_ref, vc_ref, lam_ref, g_ref, o_ref, k_scr, vt_scr = refs

    @pl.when(pl.program_id(2) == 0)
    def _():
        k_scr[0:n_ctx, :] = kc_ref[...].astype(BF16)
        vt_scr[:, 0:n_ctx] = vc_ref[...].T.astype(BF16)
        if with_lat:
            k_scr[n_ctx:, :] = _rope(kl_ref[...], kcos_ref[...], ksu_ref[...], ksd_ref[...]).astype(BF16)
            vt_scr[:, n_ctx:] = vl_ref[...].T.astype(BF16)

    lam_p = lam_ref[...]
    lam = (jnp.exp(jnp.sum(lam_p[0:1, :] * lam_p[1:2, :], axis=-1, keepdims=True))
           - jnp.exp(jnp.sum(lam_p[2:3, :] * lam_p[3:4, :], axis=-1, keepdims=True)) + lambda_init)
    first = lax.broadcasted_iota(jnp.int32, (LANES, sub_q), 0) < DIFF_HEAD_DIM
    n_sub = q_ref.shape[0] // sub_q

    def scores(t):
        rows = slice(t * sub_q, (t + 1) * sub_q)
        q = q_ref[rows, :]
        if with_lat:
            q = _rope(q, qcos_ref[rows, :], qsu_ref[rows, :], qsd_ref[rows, :])
        qt = (q * (DIFF_HEAD_DIM ** -0.5 * LOG2E)).T
        q2t = jnp.concatenate([jnp.where(first, qt, 0.0), jnp.where(first, 0.0, qt)], axis=1).astype(BF16)
        return jnp.dot(k_scr[...], q2t, preferred_element_type=F32)

    s_next = scores(0)
    for t in range(n_sub):
        rows = slice(t * sub_q, (t + 1) * sub_q)
        s = s_next
        if t + 1 < n_sub:
            s_next = scores(t + 1)
        e = jnp.exp2(s - jnp.max(s, axis=0, keepdims=True))
        r = 1.0 / jnp.sum(e, axis=0, keepdims=True)
        ovt = jnp.dot(vt_scr[...], e.astype(BF16), preferred_element_type=F32) * r
        ot = ovt[:, :sub_q] - lam * ovt[:, sub_q:]
        ot = ot * lax.rsqrt(jnp.mean(ot * ot, axis=0, keepdims=True) + NORM_EPS) * g_ref[...] * (1.0 - lambda_init)
        o_ref[rows, :] = ot.T.astype(o_ref.dtype)


def _diff_attn(p0, q_col0, k_col0, v_col0, n_heads, rope_tabs, lam_p, subln_g, lambda_init,
               n_batch, seq, ctx_len, with_lat, tq=2048, sub_q=256):
    cb = lambda c0: c0 // LANES
    ctx_rb0 = (n_batch * seq) // ctx_len
    n_keys = ctx_len + (seq if with_lat else 0)
    if with_lat:
        n_q = seq // tq
        q_spec = pl.BlockSpec((tq, LANES), lambda b, h, i: (b * n_q + i, cb(q_col0) + h))
    else:
        tq = ctx_len
        n_q = 1
        q_spec = pl.BlockSpec((tq, LANES), lambda b, h, i: (ctx_rb0 + b, cb(q_col0) + h))
    kc_spec = pl.BlockSpec((ctx_len, LANES), lambda b, h, i: (ctx_rb0 + b, cb(k_col0) + h))
    vc_spec = pl.BlockSpec((ctx_len, LANES), lambda b, h, i: (ctx_rb0 + b, cb(v_col0) + h))
    small =[pl.BlockSpec((8, LANES), lambda b, h, i: (0, 0)), pl.BlockSpec((LANES, sub_q), lambda b, h, i: (0, 0))]
    small_args = [lam_p, jnp.broadcast_to(subln_g[:, None], (LANES, sub_q))]
    if with_lat:
        kl_spec = pl.BlockSpec((seq, LANES), lambda b, h, i: (b, cb(k_col0) + h))
        vl_spec = pl.BlockSpec((seq, LANES), lambda b, h, i: (b, cb(v_col0) + h))
        qt = pl.BlockSpec((tq, LANES), lambda b, h, i: (i, 0))
        kt = pl.BlockSpec((seq, LANES), lambda b, h, i: (0, 0))
        in_specs = [q_spec, kc_spec, kl_spec, vc_spec, vl_spec, qt, qt, qt, kt, kt, kt] + small
        args = [p0, p0, p0, p0, p0, *rope_tabs, *rope_tabs] + small_args
    else:
        in_specs = [q_spec, kc_spec, vc_spec] + small
        args = [p0, p0, p0] + small_args
    block_bytes = 10 * seq * LANES * 4 if with_lat else 8 * ctx_len * LANES * 4
    return pl.pallas_call(
        functools.partial(_diff_attn_kernel, with_lat=with_lat, lambda_init=lambda_init, n_ctx=ctx_len,
                          sub_q=sub_q),
        grid=(n_batch, n_heads, n_q),
        in_specs=in_specs,
        out_specs=pl.BlockSpec((tq, LANES), lambda b, h, i: (b * n_q + i, h)),
        out_shape=jax.ShapeDtypeStruct((n_batch * n_q * tq, n_heads * LANES), BF16),
        scratch_shapes=[pltpu.VMEM((n_keys, LANES), BF16), pltpu.VMEM((LANES, n_keys), BF16)],
        compiler_params=_params(("arbitrary", "arbitrary", "arbitrary"), block_bytes),
        name="diff_attn_lat" if with_lat else "diff_attn_ctx",
    )(*args)


def _rope_tables(seq):
    half = DIFF_HEAD_DIM // 2
    pos = jnp.arange(seq)
    row, col = pos // GRID_W, pos % GRID_W
    inv_freq = ROPE_BASE ** (-jnp.arange(0, half, 2, dtype=F32) / half)
    lane = np.arange(LANES) % DIFF_HEAD_DIM
    use_col = jnp.asarray(lane >= half)
    first = jnp.asarray((lane % half) < half // 2)
    freq = inv_freq[jnp.asarray(lane % (half // 2))]
    p = jnp.where(use_col[None, :], col[:, None], row[:, None]).astype(F32)
    ang = p * freq[None, :]
    cos, sin = jnp.cos(ang), jnp.sin(ang)
    return cos, jnp.where(first[None, :], -sin, 0.0), jnp.where(first[None, :], 0.0, sin)


def _rpb_gather_kernel(rpb_ref, o_ref):
    n = o_ref.shape[1]
    k = rpb_ref.shape[1]
    colid = pl.program_id(0) * n + lax.broadcasted_iota(jnp.int32, (k, n), 1)
    j = lax.broadcasted_iota(jnp.int32, (k, n), 0)
    qc = lax.shift_right_logical(colid, int(math.log2(LANES)))
    half = lax.shift_right_logical(colid & (LANES - 1), int(math.log2(GRID_W)))
    kc = colid & (GRID_W - 1)
    sel = (j == half * LANES + jnp.clip(kc - qc + (WIN_COLS - 1), 0, 2 * WIN_COLS - 2)).astype(F32)
    o_ref[...] = jnp.dot(rpb_ref[...], sel, preferred_element_type=F32, precision=lax.Precision.HIGHEST)


def _rpb_pairs(rpb):
    nh, nr, ncol = rpb.shape
    n_slot = nr + 1
    left = jnp.pad(rpb, ((0, 0), (1, 0), (0, LANES - ncol)))
    right = jnp.pad(rpb, ((0, 0), (0, 1), (0, LANES - ncol)))
    rows = jnp.concatenate([left, right], axis=2).reshape(nh * n_slot, 2 * LANES)
    tn = 1024
    out = pl.pallas_call(
        _rpb_gather_kernel,
        grid=(GRID_W * LANES // tn,),
        in_specs=[pl.BlockSpec((nh * n_slot, 2 * LANES), lambda j: (0, 0))],
        out_specs=pl.BlockSpec((nh * n_slot, tn), lambda j: (0, j)),
        out_shape=jax.ShapeDtypeStruct((nh * n_slot, GRID_W * LANES), F32),
        compiler_params=_params(("arbitrary",), nh * n_slot * tn * 4 + 2 * LANES * tn * 4),
        name="rpb_gather",
    )(rows)
    return out.reshape(nh, n_slot, GRID_W, LANES)


def _na_block_plan(rows):
    kr = min(WIN_ROWS, rows)
    n_blk = rows // NA_Q_ROWS
    kb = np.clip(np.arange(n_blk) * NA_Q_ROWS - kr // 2, 0, rows - NA_K_ROWS)
    layouts, layout_of = [], []
    for blk in range(n_blk):
        dr = np.full((NA_Q_ROWS, NA_K_ROWS), -1, np.int64)
        for i in range(NA_Q_ROWS):
            r = blk * NA_Q_ROWS + i
            rs = int(np.clip(r - kr // 2, 0, rows - kr))
            for j in range(NA_K_ROWS):
                krow = kb[blk] + j
                if rs <= krow < rs + kr:
                    dr[i, j] = krow - r + WIN_ROWS - 1
        key = dr.tobytes()
        if key not in [l.tobytes() for l in layouts]:
            layouts.append(dr)
        layout_of.append([l.tobytes() for l in layouts].index(key))
    return kb, np.stack(layouts), np.asarray(layout_of)


def _na_bias_kernel(tp_ref, o_ref, *, layouts):
    qc = lax.broadcasted_iota(jnp.int32, (GRID_W, LANES), 0)
    lane = lax.broadcasted_iota(jnp.int32, (GRID_W, LANES), 1)
    kc = lane & (GRID_W - 1)
    left = lane < GRID_W
    col_start = jnp.clip(qc - WIN_COLS // 2, 0, GRID_W - WIN_COLS)
    col_ok = (kc >= col_start) & (kc < col_start + WIN_COLS)
    n_lay, n_q, n_k = layouts.shape
    for lay in range(n_lay):
        for i in range(n_q):
            for p in range(-(-n_k // 2)):
                d_l = int(layouts[lay, i, 2 * p])
                d_r = int(layouts[lay, i, 2 * p + 1]) if 2 * p + 1 < n_k else -1
                width = LANES if 2 * p + 1 < n_k else GRID_W
                if d_l < 0 and d_r < 0:
                    tile = jnp.full((GRID_W, LANES), -jnp.inf, F32)
                else:
                    assert d_l < 0 or d_r < 0 or d_r == d_l + 1
                    ok = col_ok
                    if d_l < 0:
                        ok = ok & jnp.logical_not(left)
                    if d_r < 0:
                        ok = ok & left
                    slot = d_r if d_r >= 0 else d_l + 1
                    tile = jnp.where(ok, tp_ref[slot] * LOG2E, -jnp.inf)
                o_ref[lay, i * GRID_W:(i + 1) * GRID_W, p * LANES:p * LANES + width] = tile[:, :width]


def _na_bias(rpb_pairs, layouts):
    nh, n_slot = rpb_pairs.shape[:2]
    n_lay = layouts.shape[0]
    tq, nkw = NA_Q_ROWS * GRID_W, NA_K_ROWS * GRID_W
    return pl.pallas_call(
        functools.partial(_na_bias_kernel, layouts=layouts),
        grid=(nh,),
        in_specs=[pl.BlockSpec((None, n_slot, GRID_W, LANES), lambda h: (h, 0, 0, 0))],
        out_specs=pl.BlockSpec((None, n_lay, tq, nkw), lambda h: (h, 0, 0, 0)),
        out_shape=jax.ShapeDtypeStruct((nh, n_lay, tq, nkw), F32),
        compiler_params=_params(("arbitrary",), n_lay * tq * nkw * 4),
        name="na_bias",
    )(rpb_pairs)


def _na_kernel(q_ref, kl_ref, vl_ref, kc_ref, vc_ref, bias_ref, o_ref, *, plan):
    tq = NA_Q_ROWS * GRID_W
    nkw = NA_K_ROWS * GRID_W
    nt = (((1,), (1,)), ((), ()))
    kc = kc_ref[...]
    vc = vc_ref[...]

    def scores(blk):
        kb, lay = plan[blk]
        q = q_ref[blk * tq:(blk + 1) * tq, :]
        ks = slice(kb * GRID_W, kb * GRID_W + nkw)
        return (lax.dot_general(q, kl_ref[ks, :], nt, preferred_element_type=F32) + bias_ref[lay],
                lax.dot_general(q, kc, nt, preferred_element_type=F32))

    def weighted_values(blk, e_w, e_c, l):
        kb, _ = plan[blk]
        ks = slice(kb * GRID_W, kb * GRID_W + nkw)
        o = (jnp.dot(e_w, vl_ref[ks, :], preferred_element_type=F32) + jnp.dot(e_c, vc, preferred_element_type=F32))
        o_ref[blk * tq:(blk + 1) * tq, :] = (o * (1.0 / l)).astype(o_ref.dtype)

    s_next = scores(0)
    pending = None
    for blk in range(len(plan)):
        s_w, s_c = s_next
        if blk + 1 < len(plan):
            s_next = scores(blk + 1)
        if pending is not None:
            weighted_values(blk - 1, *pending)
        m = jnp.maximum(jnp.max(s_w, axis=-1, keepdims=True), jnp.max(s_c, axis=-1, keepdims=True))
        e_w = jnp.exp2(s_w - m)
        e_c = jnp.exp2(s_c - m)
        l = jnp.sum(e_w, axis=-1, keepdims=True) + jnp.sum(e_c, axis=-1, keepdims=True)
        pending = (e_w.astype(BF16), e_c.astype(BF16), l)
    weighted_values(len(plan) - 1, *pending)


def _na_attn(p1, rpb, n_batch, seq, ctx_len, n_heads):
    rows = seq // GRID_W
    kb, layouts, layout_of = _na_block_plan(rows)
    bias = _na_bias(_rpb_pairs(rpb), layouts)
    n_lay = layouts.shape[0]
    tq = NA_Q_ROWS * GRID_W
    nkw = NA_K_ROWS * GRID_W
    ctx_rb0 = (n_batch * seq) // ctx_len
    plan = tuple((int(k), int(l)) for k, l in zip(kb, layout_of))
    return pl.pallas_call(
        functools.partial(_na_kernel, plan=plan),
        grid=(n_heads, n_batch),
        in_specs=[
            pl.BlockSpec((seq, LANES), lambda h, b: (b, h)),
            pl.BlockSpec((seq, LANES), lambda h, b: (b, n_heads + h)),
            pl.BlockSpec((seq, LANES), lambda h, b: (b, 2 * n_heads + h)),
            pl.BlockSpec((ctx_len, LANES), lambda h, b: (ctx_rb0 + b, n_heads + h)),
            pl.BlockSpec((ctx_len, LANES), lambda h, b: (ctx_rb0 + b, 2 * n_heads + h)),
            pl.BlockSpec((None, n_lay, tq, nkw), lambda h, b: (h, 0, 0, 0)),
        ],
        out_specs=pl.BlockSpec((seq, LANES), lambda h, b: (b, h)),
        out_shape=jax.ShapeDtypeStruct((n_batch * seq, n_heads * LANES), BF16),
        compiler_params=_params(("arbitrary", "arbitrary"), n_lay * tq * nkw * 4 + 4 * seq * LANES * 2),
        name="na_attn",
    )(p1, p1, p1, p1, p1, bias)


def _pack_rows(rows, n_rows=8):
    out = jnp.zeros((n_rows, LANES), F32)
    for r, v in enumerate(rows):
        out = out.at[r, :v.shape[0]].set(v.astype(F32))
    return out


def kernel(x, c, ctx, c_ctx, ada_w, ada_b, norm_mix_g, norm_ffn_g, final_norm_g, ffn_w1, ffn_w3, ffn_w2, ev_w_in, ev_conv_w, ev_conv_b, ev_a_log, ev_dt_bias, ev_d_skip, ev_ssm_norm_g, ev_lam_q1, ev_lam_k1, ev_lam_q2, ev_lam_k2, ev_subln_g, ev_w_out, od_w_in, od_rpb, od_w_out):
    n_batch, seq, d = x.shape
    ctx_len = ctx.shape[1]
    depth = ada_w.shape[0]
    n_lat = n_batch * seq
    n_tok = n_lat + n_batch * ctx_len

    d_ssm = ev_ssm_norm_g.shape[1]
    n_ssm_heads = ev_d_skip.shape[1]
    d_xbc = ev_conv_w.shape[2]
    d_qk = d_v = (ev_w_in.shape[2] - d_ssm - d_xbc - 2 * n_ssm_heads) // 3
    n_diff_heads = d_v // (2 * DIFF_HEAD_DIM)
    n_na_heads = od_rpb.shape[1]

    stream = (x.reshape(n_lat, d), ctx.reshape(n_batch * ctx_len, d))
    cond = jnp.zeros((COND_ROWS, d), F32).at[:n_batch].set(c).at[n_batch].set(c_ctx)
    mod = _ada_mod(cond, ada_w, ada_b).reshape(depth, COND_ROWS, 1, 6 * d)
    rope_tabs = _rope_tables(seq)
    kw = dict(seq=seq, n_batch=n_batch)
    ffn_w = (ffn_w1.astype(BF16), ffn_w3.astype(BF16), ffn_w2)

    for i in range(depth):
        ctx_out = i < depth - 1
        m_rows = n_tok if ctx_out else n_lat
        j = i // 2
        if i % 2 == 0:
            lambda_init = 0.8 - 0.6 * math.exp(-0.3 * i)
            w_in = ev_w_in[j]
            dt0 = d_ssm + d_xbc
            w_t = jnp.swapaxes(w_in, 0, 1)
            w_parts = [w_t[:dt0].astype(BF16), w_t[dt0 + 2 * n_ssm_heads:].astype(BF16)]
            w_dt = jnp.zeros((2 * LANES, d), F32)
            w_dt = w_dt.at[:n_ssm_heads].set(w_t[dt0:dt0 + n_ssm_heads])
            w_dt = w_dt.at[LANES:LANES + n_ssm_heads].set(w_t[dt0 + n_ssm_heads:dt0 + 2 * n_ssm_heads])
            p0, dt_raw = _proj(stream, norm_mix_g[i], mod, i, w_parts, w_dt.astype(BF16), F32, **kw)
            q0 = d_ssm + d_xbc
            k0, v0 = q0 + d_qk, q0 + 2 * d_qk

            xbc = _ssd_conv(p0, d_ssm, d_xbc, ev_conv_w[j], ev_conv_b[j], seq, ctx_len, n_lat)
            prm = jnp.stack([_pack_rows([ev_a_log[j, r], ev_dt_bias[j, r]]) for r in range(2)])
            y2 = _ssd_scan(xbc, dt_raw, prm, n_batch, seq, ctx_len, d_ssm)
            mix_ssd = _ssd_finish(y2, xbc, p0, jnp.repeat(ev_d_skip[j], SSM_HEAD_DIM), ev_ssm_norm_g[j], d_ssm)

            lam_p = _pack_rows([ev_lam_q1[j], ev_lam_k1[j], ev_lam_q2[j], ev_lam_k2[j]])
            attn_args = (p0, q0, k0, v0, n_diff_heads, rope_tabs, lam_p, ev_subln_g[j], lambda_init,
                         n_batch, seq, ctx_len)
            mix_attn = _diff_attn(*attn_args, with_lat=True)
            if ctx_out:
                mix_attn = (mix_attn, _diff_attn(*attn_args, with_lat=False))
            lhs = [mix_ssd, mix_attn]
            w_out = ev_w_out[j].astype(BF16)
        else:
            d_na = n_na_heads * NA_HEAD_DIM
            q_scale = jnp.where(jnp.arange(3 * d_na) < d_na, NA_HEAD_DIM ** -0.5 * LOG2E, 1.0).astype(F32)
            p1 = _proj(stream, norm_mix_g[i], mod, i, od_w_in[j].astype(BF16), None, BF16, col_scale=q_scale, **kw)
            assert not ctx_out, "context-query neighbourhood layers are not needed at this depth"
            lhs = [_na_attn(p1, od_rpb[j], n_batch, seq, ctx_len, n_na_heads)]
            w_out = od_w_out[j].astype(BF16)
        stream = _out_proj(lhs, w_out, stream, mod, i, 2, m_rows, **kw)
        stream = _ffn(stream, norm_ffn_g[i], mod, i, *ffn_w, final_norm_g, not ctx_out, m_rows, **kw)
    return stream.reshape(n_batch, seq, d)
```

```python
import functools
import math

import jax
import jax.numpy as jnp
import numpy as np
from jax import lax
from jax.experimental import pallas as pl
from jax.experimental.pallas import tpu as pltpu

F32 = jnp.float32
BF16 = jnp.bfloat16

GRID_W = 64
SSM_HEAD_DIM = 64
SSM_GROUPS = 4
SSM_STATE = 128
SSM_CONV = 5
SSM_CHUNK = 128
DIFF_HEAD_DIM = 64
NA_HEAD_DIM = 128
WIN_ROWS = 8
WIN_COLS = 16
ROPE_BASE = 10000.0
NORM_EPS = 1e-6
LOG2E = math.log2(math.e)

LANES = 128
V7X_VMEM_BYTES = 64 * 1024 * 1024
VMEM_HEADROOM_BYTES = 3 * 1024 * 1024

NA_Q_ROWS = 4
NA_K_ROWS = NA_Q_ROWS + WIN_ROWS - 1
COND_ROWS = 16


def _vmem_limit(block_bytes):
    return int(min(V7X_VMEM_BYTES - VMEM_HEADROOM_BYTES, max(32 * 1024 * 1024, 2 * block_bytes + 16 * 1024 * 1024)))


def _params(semantics, block_bytes):
    return pltpu.CompilerParams(dimension_semantics=semantics, vmem_limit_bytes=_vmem_limit(block_bytes))


def _silu(v):
    return v * jax.nn.sigmoid(v)


def _mod_row(i, tm, seq, n_batch):
    return jnp.minimum((i * tm) // seq, n_batch)


def _ada_kernel(cond_ref, w_ref, b_ref, o_ref):
    s = _silu(cond_ref[...]).astype(BF16)
    o_ref[...] = jnp.dot(s, w_ref[...].astype(BF16), preferred_element_type=F32) + b_ref[...]


def _ada_mod(cond, ada_w, ada_b, tn=1024):
    depth, d, n = ada_w.shape
    return pl.pallas_call(
        _ada_kernel,
        grid=(depth, n // tn),
        in_specs=[
            pl.BlockSpec((COND_ROWS, d), lambda l, j: (0, 0)),
            pl.BlockSpec((None, d, tn), lambda l, j: (l, 0, j)),
            pl.BlockSpec((None, 1, tn), lambda l, j: (l, 0, j)),
        ],
        out_specs=pl.BlockSpec((None, COND_ROWS, tn), lambda l, j: (l, 0, j)),
        out_shape=jax.ShapeDtypeStruct((depth, COND_ROWS, n), F32),
        compiler_params=_params(("arbitrary", "arbitrary"), d * tn * 4),
        name="ada_mod",
    )(cond, ada_w, ada_b.reshape(depth, 1, n))


def _norm_mod_pipeline(x_ref, g_ref, mod_ref, shift_idx, scale_idx, n_chunks, consume):
    tm, d = x_ref.shape
    shift = mod_ref[:, shift_idx * d:(shift_idx + 1) * d]
    gain = g_ref[...] * (1.0 + mod_ref[:, scale_idx * d:(scale_idx + 1) * d])
    chunk = tm // n_chunks

    def norm_rows(c):
        x = x_ref[c * chunk:(c + 1) * chunk, :]
        inv = lax.rsqrt(jnp.mean(x * x, axis=-1, keepdims=True) + NORM_EPS)
        return (x * inv * gain + shift).astype(BF16)

    h_next = norm_rows(0)
    for c in range(n_chunks):
        h = h_next
        if c + 1 < n_chunks:
            h_next = norm_rows(c + 1)
        consume(slice(c * chunk, (c + 1) * chunk), h)


def _proj_kernel(*refs, n_x, n_w, tiles_per_part, n_lat_tiles, has_aux, has_scale, first_step_chunks=4):
    rest = list(refs)
    x_refs = [rest.pop(0) for _ in range(n_x)]
    g_ref, mod_ref = rest.pop(0), rest.pop(0)
    w_refs = [rest.pop(0) for _ in range(max(n_w, 1))]
    cs_ref = rest.pop(0) if has_scale else None
    if has_aux:
        waux_ref, o_ref, oaux_ref, h_ref = rest
    else:
        o_ref, h_ref = rest
    dims = (((1,), (1,)), ((), ())) if n_w else (((1,), (0,)), ((), ()))
    j = pl.program_id(1)
    tm, d = x_refs[0].shape

    def emit(h, rows, w_ref):
        acc = lax.dot_general(h, w_ref[...], dims, preferred_element_type=F32)
        if has_scale:
            acc = acc * cs_ref[...]
        o_ref[rows, :] = acc.astype(o_ref.dtype)

    def first_tile(x_ref):
        def consume(rows, h):
            h_ref[rows, :] = h
            emit(h, rows, w_refs[0])
            if has_aux:
                oaux_ref[rows, :] = lax.dot_general(h, waux_ref[...], dims, preferred_element_type=F32)

        _norm_mod_pipeline(x_ref, g_ref, mod_ref, 0, 1, first_step_chunks, consume)

    if n_x == 1:
        pl.when(j == 0)(functools.partial(first_tile, x_refs[0]))
    else:
        is_lat = pl.program_id(0) < n_lat_tiles
        pl.when(jnp.logical_and(j == 0, is_lat))(functools.partial(first_tile, x_refs[0]))
        pl.when(jnp.logical_and(j == 0, jnp.logical_not(is_lat)))(functools.partial(first_tile, x_refs[1]))

    for p in range(max(n_w, 1)):
        lo = max(p * tiles_per_part, 1)
        in_part = (j >= lo) if n_w == 0 else jnp.logical_and(j >= lo, j < (p + 1) * tiles_per_part)
        pl.when(in_part)(lambda p=p: emit(h_ref[...], slice(None), w_refs[p]))


def _proj(x, g, mod, layer, w, w_aux, out_dtype, seq, n_batch, col_scale=None, tm=1024, tn=1024):
    parts = list(w) if isinstance(w, (list, tuple)) else None
    if parts is None:
        d, n = w.shape
        n_w, tiles_per_part = 0, 0
    else:
        d = parts[0].shape[1]
        n_w, tiles_per_part = len(parts), parts[0].shape[0] // tn
        assert all(p.shape == (tiles_per_part * tn, d) for p in parts)
        n = n_w * tiles_per_part * tn
    m = sum(a.shape[0] for a in x) if isinstance(x, tuple) else x.shape[0]
    n_lat_tiles = (n_batch * seq) // tm
    has_aux = w_aux is not None
    has_scale = col_scale is not None
    mod_spec = pl.BlockSpec((None, None, 1, mod.shape[-1]),
                            lambda i, j: (layer, _mod_row(i, tm, seq, n_batch), 0, 0))
    in_specs, args = _row_operand(x, tm, n_lat_tiles, d, lambda j: 0, single_ctx_buffer=True)
    n_x = len(args)
    in_specs += [pl.BlockSpec((1, d), lambda i, j: (0, 0)), mod_spec]
    args += [g.reshape(1, d), mod]
    if parts is None:
        in_specs.append(pl.BlockSpec((d, tn), lambda i, j: (0, j)))
        args.append(w)
    else:
        for p, part in enumerate(parts):
            in_specs.append(pl.BlockSpec(
                (tn, d), lambda i, j, p=p: (jnp.clip(j - p * tiles_per_part, 0, tiles_per_part - 1), 0)))
            args.append(part)
    out_specs = [pl.BlockSpec((tm, tn), lambda i, j: (i, j))]
    out_shape = [jax.ShapeDtypeStruct((m, n), out_dtype)]
    if has_scale:
        in_specs.append(pl.BlockSpec((1, tn), lambda i, j: (0, j)))
        args.append(col_scale.reshape(1, n))
    if has_aux:
        na = w_aux.shape[0] if parts is not None else w_aux.shape[1]
        in_specs.append(pl.BlockSpec(w_aux.shape, lambda i, j: (0, 0)))
        out_specs.append(pl.BlockSpec((tm, na), lambda i, j: (i, 0)))
        out_shape.append(jax.ShapeDtypeStruct((m, na), F32))
        args.append(w_aux)
    block_bytes = n_x * tm * d * 4 + max(n_w, 1) * d * tn * 2 + tm * tn * 4 + tm * d
    outs = pl.pallas_call(
        functools.partial(_proj_kernel, n_x=n_x, n_w=n_w, tiles_per_part=tiles_per_part, n_lat_tiles=n_lat_tiles,
                          has_aux=has_aux, has_scale=has_scale),
        grid=(m // tm, n // tn),
        in_specs=in_specs,
        out_specs=out_specs,
        out_shape=out_shape,
        scratch_shapes=[pltpu.VMEM((tm, d), BF16)],
        compiler_params=_params(("arbitrary", "arbitrary"), block_bytes),
        name="proj",
    )(*args)
    return outs if has_aux else outs[0]


def _row_operand(a, tm, n_lat_tiles, width, col_of_j, single_ctx_buffer=False):
    if not isinstance(a, tuple):
        return [pl.BlockSpec((tm, width), lambda i, j: (i, col_of_j(j)))], [a]
    lat, ctx = a
    assert ctx.shape[0] == tm and lat.shape[0] == n_lat_tiles * tm

    def lat_index(i, j):
        return jnp.minimum(i, n_lat_tiles - 1), jnp.where(i < n_lat_tiles, col_of_j(j), 0)

    def ctx_index(i, j):
        return 0, jnp.where(i < n_lat_tiles, 0, col_of_j(j))

    ctx_mode = dict(pipeline_mode=pl.Buffered(1)) if single_ctx_buffer else {}
    return [pl.BlockSpec((tm, width), lat_index), pl.BlockSpec((tm, width), ctx_index, **ctx_mode)], [lat, ctx]


def _pick_rows(refs, n_lat_tiles):
    if len(refs) == 1:
        return refs[0][...]
    return jnp.where(pl.program_id(0) < n_lat_tiles, refs[0][...], refs[1][...])


def _out_proj_kernel(*refs, arity, n_lat_tiles):
    refs = list(refs)
    n_lhs = len(arity) - 1
    groups = [[refs.pop(0) for _ in range(n)] for n in arity[:-1]]
    ws = [refs.pop(0) for _ in range(n_lhs)]
    res = [refs.pop(0) for _ in range(arity[-1])]
    gate_ref, o_ref = refs
    acc = None
    for grp, w_ref in zip(groups, ws):
        part = jnp.dot(_pick_rows(grp, n_lat_tiles), w_ref[...], preferred_element_type=F32)
        acc = part if acc is None else acc + part
    o_ref[...] = _pick_rows(res, n_lat_tiles) + gate_ref[...] * acc


def _out_proj(lhs_list, w, res, mod, layer, gate_idx, m_rows, seq, n_batch, tm=1024, tn=1024):
    d = w.shape[1]
    n_lat_tiles = (n_batch * seq) // tm
    in_specs, args, arity, widths = [], [], [], []
    for a in lhs_list:
        kk = (a[0] if isinstance(a, tuple) else a).shape[1]
        sp, ar = _row_operand(a, tm, n_lat_tiles, kk, lambda j: 0)
        in_specs += sp
        args += ar
        arity.append(len(ar))
        widths.append(kk)
    row0 = 0
    for kk in widths:
        in_specs.append(pl.BlockSpec((kk, tn), lambda i, j, rb=row0 // kk: (rb, j)))
        args.append(w)
        row0 += kk
    sp, ar = _row_operand(res, tm, n_lat_tiles, tn, lambda j: j)
    in_specs += sp
    args += ar
    arity.append(len(ar))
    in_specs.append(pl.BlockSpec((None, None, 1, tn),
                                 lambda i, j: (layer, _mod_row(i, tm, seq, n_batch), 0, gate_idx * (d // tn) + j)))
    args.append(mod)
    k_total = sum(widths)
    block_bytes = 2 * tm * k_total * 2 + k_total * tn * 2 + 3 * tm * tn * 4
    return pl.pallas_call(
        functools.partial(_out_proj_kernel, arity=tuple(arity), n_lat_tiles=n_lat_tiles),
        grid=(m_rows // tm, d // tn),
        in_specs=in_specs,
        out_specs=pl.BlockSpec((tm, tn), lambda i, j: (i, j)),
        out_shape=jax.ShapeDtypeStruct((m_rows, d), F32),
        compiler_params=_params(("arbitrary", "arbitrary"), block_bytes),
        name="out_proj",
    )(*args)


def _ffn_kernel(x_ref, g_ref, mod_ref, w1_ref, w3_ref, w2_ref, fg_ref, o_ref, h_ref, *, final_norm, rows,
                first_step_chunks):
    j = pl.program_id(1)
    tm, d = x_ref.shape

    def swiglu(h, w1, w3, w2):
        a = jnp.dot(h, w1, preferred_element_type=F32)
        b = jnp.dot(h, w3, preferred_element_type=F32)
        return jnp.dot((_silu(a) * b).astype(BF16), w2, preferred_element_type=F32)

    @pl.when(j == 0)
    def _():
        w1, w3, w2 = (w_ref[...].astype(BF16) for w_ref in (w1_ref, w3_ref, w2_ref))

        def consume(rows, h):
            h_ref[rows, :] = h
            o_ref[rows, :] = swiglu(h, w1, w3, w2)

        _norm_mod_pipeline(x_ref, g_ref, mod_ref, 3, 4, first_step_chunks, consume)

    @pl.when(j > 0)
    def _():
        o_ref[...] += swiglu(h_ref[...], w1_ref[...].astype(BF16), w3_ref[...].astype(BF16),
                             w2_ref[...].astype(BF16))

    @pl.when(j == pl.num_programs(1) - 1)
    def _():
        gate = mod_ref[:, 5 * d:6 * d]
        fg = fg_ref[...]

        def body(r, carry):
            sl = pl.ds(pl.multiple_of(r * rows, rows), rows)
            y = x_ref[sl, :] + gate * o_ref[sl, :]
            if final_norm:
                y = y * lax.rsqrt(jnp.mean(y * y, axis=-1, keepdims=True) + NORM_EPS) * fg
            o_ref[sl, :] = y
            return carry

        lax.fori_loop(0, x_ref.shape[0] // rows, body, 0)


def _ffn(x, g, mod, layer, w1, w3, w2, final_g, final_norm, m_rows, seq, n_batch, tm=1024, tf=256):
    d = x.shape[1]
    ff = w1.shape[2]
    block_bytes = 2 * tm * d * 4 + d * tf * sum(w.dtype.itemsize for w in (w1, w3, w2)) + tm * d
    return pl.pallas_call(
        functools.partial(_ffn_kernel, final_norm=final_norm, rows=128, first_step_chunks=4),
        grid=(m_rows // tm, ff // tf),
        in_specs=[
            pl.BlockSpec((tm, d), lambda i, j: (i, 0)),
            pl.BlockSpec((1, d), lambda i, j: (0, 0)),
            pl.BlockSpec((None, None, 1, mod.shape[-1]),
                         lambda i, j: (layer, _mod_row(i, tm, seq, n_batch), 0, 0)),
            pl.BlockSpec((None, d, tf), lambda i, j: (layer, 0, j)),
            pl.BlockSpec((None, d, tf), lambda i, j: (layer, 0, j)),
            pl.BlockSpec((None, tf, d), lambda i, j: (layer, j, 0)),
            pl.BlockSpec((1, d), lambda i, j: (0, 0)),
        ],
        out_specs=pl.BlockSpec((tm, d), lambda i, j: (i, 0)),
        out_shape=jax.ShapeDtypeStruct((m_rows, d), F32),
        scratch_shapes=[pltpu.VMEM((tm, d), BF16)],
        compiler_params=_params(("arbitrary", "arbitrary"), block_bytes),
        name="ffn",
    )(x, g.reshape(1, d), mod, w1, w3, w2, final_g.reshape(1, d))


def _conv_kernel(prev_ref, cur_ref, next_ref, w_ref, b_ref, o_ref, ext_ref, *, tm, halo, seq, ctx_len, n_lat_tiles):
    i = pl.program_id(0)
    half = SSM_CONV // 2
    ext_ref[halo:halo + tm, :] = cur_ref[...]

    def taps(mask_of):
        acc = jnp.zeros(cur_ref.shape, F32) + b_ref[...]
        for t in range(SSM_CONV):
            tap = ext_ref[halo - half + t:halo - half + t + tm, :]
            acc = acc + mask_of(t, tap) * w_ref[t:t + 1, :]
        o_ref[...] = _silu(acc)

    @pl.when(i < n_lat_tiles)
    def _():
        prev_in_seg = ((i * tm) & (seq - 1)) != 0
        next_in_seg = (((i + 1) * tm) & (seq - 1)) != 0
        ext_ref[0:halo, :] = jnp.where(prev_in_seg, prev_ref[...], 0.0)
        ext_ref[halo + tm:, :] = jnp.where(next_in_seg, next_ref[...], 0.0)
        taps(lambda t, tap: tap)

    @pl.when(i >= n_lat_tiles)
    def _():
        ext_ref[0:halo, :] = prev_ref[...]
        ext_ref[halo + tm:, :] = next_ref[...]
        pos = (i * tm + lax.broadcasted_iota(jnp.int32, (tm, 1), 0)) & (ctx_len - 1)

        def masked(t, tap):
            src = pos + (t - half)
            return jnp.where((src >= 0) & (src < ctx_len), tap, 0.0)

        taps(masked)


def _ssd_conv(p0, col0, width, conv_w, conv_b, seq, ctx_len, n_lat_rows, tm=1024, tc=512, halo=8):
    m = p0.shape[0]
    assert seq % tm == 0 and tm % ctx_len == 0 and (seq & (seq - 1)) == 0 and (ctx_len & (ctx_len - 1)) == 0
    cb0 = col0 // tc
    hb = tm // halo
    last_hb = m // halo - 1
    wpad = jnp.zeros((8, width), F32).at[:SSM_CONV].set(conv_w)
    return pl.pallas_call(
        functools.partial(_conv_kernel, tm=tm, halo=halo, seq=seq, ctx_len=ctx_len, n_lat_tiles=n_lat_rows // tm),
        grid=(m // tm, width // tc),
        in_specs=[
            pl.BlockSpec((halo, tc), lambda i, j: (jnp.maximum(i * hb - 1, 0), cb0 + j)),
            pl.BlockSpec((tm, tc), lambda i, j: (i, cb0 + j)),
            pl.BlockSpec((halo, tc), lambda i, j: (jnp.minimum((i + 1) * hb, last_hb), cb0 + j)),
            pl.BlockSpec((8, tc), lambda i, j: (0, j)),
            pl.BlockSpec((1, tc), lambda i, j: (0, j)),
        ],
        out_specs=pl.BlockSpec((tm, tc), lambda i, j: (i, j)),
        out_shape=jax.ShapeDtypeStruct((m, width), F32),
        scratch_shapes=[pltpu.VMEM((tm + 2 * halo, tc), F32)],
        compiler_params=_params(("arbitrary", "arbitrary"), 3 * tm * tc * 4),
        name="ssd_conv",
    )(p0, p0, p0, wpad, conv_b.reshape(1, width))


def _ssd_scan_kernel(xs_f, bm_f, cm_f, dt_f, xs_b, bm_b, cm_b, dt_b, prm_ref, yf_ref, yb_ref, state_ref, *, n_heads):
    @pl.when(pl.program_id(1) == 0)
    def _():
        state_ref[...] = jnp.zeros_like(state_ref)

    fwd = _ssd_chunk(xs_f, bm_f, cm_f, dt_f, prm_ref.at[0], yf_ref, state_ref.at[0], backward=False, n_heads=n_heads)
    bwd = _ssd_chunk(xs_b, bm_b, cm_b, dt_b, prm_ref.at[1], yb_ref, state_ref.at[1], backward=True, n_heads=n_heads)
    for g in range(SSM_GROUPS):
        fwd(g)
        bwd(g)


def _ssd_chunk(xs_ref, bm_ref, cm_ref, dt_ref, prm_ref, y_ref, state_ref, *, backward, n_heads):
    hp = SSM_HEAD_DIM
    rep = n_heads // SSM_GROUPS
    lc = SSM_CHUNK
    assert 2 * hp == LANES and rep % 2 == 0 and SSM_STATE == LANES and lc == LANES

    dt_in = dt_ref[...] + prm_ref[1:2, :]
    dt = jnp.maximum(dt_in, 0.0) + jnp.log1p(jnp.exp(-jnp.abs(dt_in)))
    dta = dt * (-jnp.exp(prm_ref[0:1, :]) * LOG2E)
    row = lax.broadcasted_iota(jnp.int32, (lc, lc), 0)
    col = lax.broadcasted_iota(jnp.int32, (lc, lc), 1)
    causal = (row <= col) if backward else (row >= col)
    a_cum = jnp.dot(causal.astype(F32), dta, preferred_element_type=F32, precision=lax.Precision.HIGHEST)
    a_tot = a_cum[0:1, :] if backward else a_cum[lc - 1:lc, :]
    w_end = dt * jnp.exp2(a_tot - a_cum)
    a_cum_t, dt_t, w_end_t = a_cum.T, dt.T, w_end.T
    lo = col < hp

    def do_group(g):
        gs = slice(g * SSM_STATE, (g + 1) * SSM_STATE)
        b_f = bm_ref[:, gs]
        c_g = cm_ref[:, gs].astype(BF16)
        cb = lax.dot_general(c_g, b_f.astype(BF16), (((1,), (1,)), ((), ())), preferred_element_type=F32)
        b_t = b_f.T
        cols_g = slice(g * rep * hp, (g + 1) * rep * hp)
        y_off = jnp.dot(c_g, state_ref[:, cols_g].astype(BF16), preferred_element_type=F32)
        for pr in range(rep // 2):
            cols = slice(g * rep * hp + pr * LANES, g * rep * hp + (pr + 1) * LANES)
            xs2 = xs_ref[:, cols].astype(BF16)
            y_d, s_n, e_a = [], [], []
            for h in (g * rep + 2 * pr, g * rep + 2 * pr + 1):
                a_col = jnp.broadcast_to(a_cum[:, h:h + 1], (lc, lc))
                decay = jnp.exp2(jnp.where(causal, a_col - a_cum_t[h:h + 1, :], -jnp.inf))
                m = (cb * decay * dt_t[h:h + 1, :]).astype(BF16)
                y_d.append(jnp.dot(m, xs2, preferred_element_type=F32))
                s_n.append(jnp.dot((b_t * w_end_t[h:h + 1, :]).astype(BF16), xs2, preferred_element_type=F32))
                e_a.append(jnp.exp2(a_col))
            h0 = g * rep + 2 * pr
            y_ref[:, cols] = (jnp.where(lo, y_d[0], y_d[1])
                              + y_off[:, pr * LANES:(pr + 1) * LANES] * jnp.where(lo, e_a[0], e_a[1]))
            chunk_decay = jnp.where(lo[0:1, :], jnp.exp2(a_tot[:, h0:h0 + 1]), jnp.exp2(a_tot[:, h0 + 1:h0 + 2]))
            state_ref[:, cols] = state_ref[:, cols] * chunk_decay + jnp.where(lo, s_n[0], s_n[1])

    return do_group


def _ssd_scan(xbc, dt_raw, prm, n_batch, seq, ctx_len, d_ssm):
    m = xbc.shape[0]
    lc = SSM_CHUNK
    n_heads = d_ssm // SSM_HEAD_DIM
    nc_ctx, nc_lat = ctx_len // lc, seq // lc
    ctx_blk0 = n_batch * nc_lat
    gn = SSM_GROUPS * SSM_STATE

    def row_blk(b, s, backward):
        ctx_c = nc_ctx - 1 - s if backward else s
        lat_c = nc_lat - 1 - (s - nc_ctx) if backward else s - nc_ctx
        return jnp.where(s < nc_ctx, ctx_blk0 + b * nc_ctx + ctx_c, b * nc_lat + lat_c)

    def chunk_specs(backward):
        r = int(backward)
        return [
            pl.BlockSpec((lc, d_ssm), lambda b, s: (row_blk(b, s, backward), 0)),
            pl.BlockSpec((lc, gn), lambda b, s: (row_blk(b, s, backward), d_ssm // gn)),
            pl.BlockSpec((lc, gn), lambda b, s: (row_blk(b, s, backward), d_ssm // gn + 1)),
            pl.BlockSpec((lc, LANES), lambda b, s: (row_blk(b, s, backward), r)),
        ]

    return pl.pallas_call(
        functools.partial(_ssd_scan_kernel, n_heads=n_heads),
        grid=(n_batch, nc_ctx + nc_lat),
        in_specs=chunk_specs(False) + chunk_specs(True) + [pl.BlockSpec((2, 8, LANES), lambda b, s: (0, 0, 0))],
        out_specs=[pl.BlockSpec((lc, d_ssm), lambda b, s: (row_blk(b, s, False), 0)),
                   pl.BlockSpec((lc, d_ssm), lambda b, s: (row_blk(b, s, True), 0))],
        out_shape=[jax.ShapeDtypeStruct((m, d_ssm), F32), jax.ShapeDtypeStruct((m, d_ssm), F32)],
        scratch_shapes=[pltpu.VMEM((2, SSM_STATE, d_ssm), F32)],
        compiler_params=_params(("arbitrary", "arbitrary"), 8 * lc * d_ssm * 4),
        name="ssd_scan",
    )(xbc, xbc, xbc, dt_raw, xbc, xbc, xbc, dt_raw, prm)


def _ssd_finish_kernel(yf_ref, yb_ref, xs_ref, z_ref, dskip_ref, g_ref, o_ref, *, group_width):
    y = (yf_ref[...] + yb_ref[...] + dskip_ref[...] * xs_ref[...]) * _silu(z_ref[...])
    for g in range(y.shape[1] // group_width):
        sl = slice(g * group_width, (g + 1) * group_width)
        v = y[:, sl]
        vn = v * lax.rsqrt(jnp.mean(v * v, axis=-1, keepdims=True) + NORM_EPS) * g_ref[:, sl]
        o_ref[:, sl] = vn.astype(o_ref.dtype)


def _ssd_finish(y2, xbc, p0, d_skip_lanes, norm_g, d_ssm, tm=512):
    m = xbc.shape[0]
    return pl.pallas_call(
        functools.partial(_ssd_finish_kernel, group_width=d_ssm // SSM_GROUPS),
        grid=(m // tm,),
        in_specs=[
            pl.BlockSpec((tm, d_ssm), lambda i: (i, 0)),
            pl.BlockSpec((tm, d_ssm), lambda i: (i, 0)),
            pl.BlockSpec((tm, d_ssm), lambda i: (i, 0)),
            pl.BlockSpec((tm, d_ssm), lambda i: (i, 0)),
            pl.BlockSpec((1, d_ssm), lambda i: (0, 0)),
            pl.BlockSpec((1, d_ssm), lambda i: (0, 0)),
        ],
        out_specs=pl.BlockSpec((tm, d_ssm), lambda i: (i, 0)),
        out_shape=jax.ShapeDtypeStruct((m, d_ssm), BF16),
        compiler_params=_params(("arbitrary",), 5 * tm * d_ssm * 4),
        name="ssd_finish",
    )(*y2, xbc, p0, d_skip_lanes.reshape(1, d_ssm), norm_g.reshape(1, d_ssm))


def _rope(x, cos, sin_up, sin_dn):
    quarter = DIFF_HEAD_DIM // 4
    return (x * cos + pltpu.roll(x, LANES - quarter, axis=1) * sin_up + pltpu.roll(x, quarter, axis=1) * sin_dn)


def _diff_attn_kernel(*refs, with_lat, lambda_init, n_ctx, sub_q, key_chunk=768):
    if with_lat:
        (q_ref, kc_ref, kl_ref, vc_ref, vl_ref, qcos_ref, qsu_ref, qsd_ref, kcos_ref, ksu_ref, ksd_ref,
         lam_ref, g_ref, o_ref, k_scr, vt_scr) = refs
    else:
        q_ref, kc_ref, vc_ref, lam_ref, g_ref, o_ref, k_scr, vt_scr = refs

    @pl.when(pl.program_id(2) == 0)
    def _():
        k_scr[0:n_ctx, :] = kc_ref[...].astype(BF16)
        vt_scr[:, 0:n_ctx] = vc_ref[...].T.astype(BF16)
        if with_lat:
            k_scr[n_ctx:, :] = _rope(kl_ref[...], kcos_ref[...], ksu_ref[...], ksd_ref[...]).astype(BF16)
            vt_scr[:, n_ctx:] = vl_ref[...].T.astype(BF16)

    lam_p = lam_ref[...]
    lam = (jnp.exp(jnp.sum(lam_p[0:1, :] * lam_p[1:2, :], axis=-1, keepdims=True))
           - jnp.exp(jnp.sum(lam_p[2:3, :] * lam_p[3:4, :], axis=-1, keepdims=True)) + lambda_init)
    first = lax.broadcasted_iota(jnp.int32, (LANES, sub_q), 0) < DIFF_HEAD_DIM
    n_sub = q_ref.shape[0] // sub_q

    def scores(t):
        rows = slice(t * sub_q, (t + 1) * sub_q)
        q = q_ref[rows, :]
        if with_lat:
            q = _rope(q, qcos_ref[rows, :], qsu_ref[rows, :], qsd_ref[rows, :])
        qt = (q * (DIFF_HEAD_DIM ** -0.5 * LOG2E)).T
        q2t = jnp.concatenate([jnp.where(first, qt, 0.0), jnp.where(first, 0.0, qt)], axis=1).astype(BF16)
        blocks = [jnp.dot(k_scr[k0:k1, :], q2t, preferred_element_type=F32) for k0, k1 in key_chunks]
        return blocks, [jnp.max(b, axis=0, keepdims=True) for b in blocks]

    n_keys = k_scr.shape[0]
    key_chunks = [(k0, min(k0 + key_chunk, n_keys)) for k0 in range(0, n_keys, key_chunk)]
    s_next = scores(0)
    for t in range(n_sub):
        rows = slice(t * sub_q, (t + 1) * sub_q)
        s_blocks, maxes = s_next
        if t + 1 < n_sub:
            s_next = scores(t + 1)
        m = functools.reduce(jnp.maximum, maxes)
        ovt, l = None, None
        for (k0, k1), s in zip(key_chunks, s_blocks):
            e = jnp.exp2(s - m)
            part = jnp.dot(vt_scr[:, k0:k1], e.astype(BF16), preferred_element_type=F32)
            part_l = jnp.sum(e, axis=0, keepdims=True)
            ovt, l = (part, part_l) if ovt is None else (ovt + part, l + part_l)
        ovt = ovt * (1.0 / l)
        ot = ovt[:, :sub_q] - lam * ovt[:, sub_q:]
        ot = ot * lax.rsqrt(jnp.mean(ot * ot, axis=0, keepdims=True) + NORM_EPS) * g_ref[...] * (1.0 - lambda_init)
        o_ref[rows, :] = ot.T.astype(o_ref.dtype)


def _diff_attn(p0, q_col0, k_col0, v_col0, n_heads, rope_tabs, lam_p, subln_g, lambda_init,
               n_batch, seq, ctx_len, with_lat, tq=2048, sub_q=256):
    cb = lambda c0: c0 // LANES
    ctx_rb0 = (n_batch * seq) // ctx_len
    n_keys = ctx_len + (seq if with_lat else 0)
    if with_lat:
        n_q = seq // tq
        q_spec = pl.BlockSpec((tq, LANES), lambda b, h, i: (b * n_q + i, cb(q_col0) + h))
    else:
        tq = ctx_len
        n_q = 1
        q_spec = pl.BlockSpec((tq, LANES), lambda b, h, i: (ctx_rb0 + b, cb(q_col0) + h))
    kc_spec = pl.BlockSpec((ctx_len, LANES), lambda b, h, i: (ctx_rb0 + b, cb(k_col0) + h))
    vc_spec = pl.BlockSpec((ctx_len, LANES), lambda b, h, i: (ctx_rb0 + b, cb(v_col0) + h))
    small =[pl.BlockSpec((8, LANES), lambda b, h, i: (0, 0)), pl.BlockSpec((LANES, sub_q), lambda b, h, i: (0, 0))]
    small_args = [lam_p, jnp.broadcast_to(subln_g[:, None], (LANES, sub_q))]
    if with_lat:
        kl_spec = pl.BlockSpec((seq, LANES), lambda b, h, i: (b, cb(k_col0) + h))
        vl_spec = pl.BlockSpec((seq, LANES), lambda b, h, i: (b, cb(v_col0) + h))
        qt = pl.BlockSpec((tq, LANES), lambda b, h, i: (i, 0))
        kt = pl.BlockSpec((seq, LANES), lambda b, h, i: (0, 0))
        in_specs = [q_spec, kc_spec, kl_spec, vc_spec, vl_spec, qt, qt, qt, kt, kt, kt] + small
        args = [p0, p0, p0, p0, p0, *rope_tabs, *rope_tabs] + small_args
    else:
        in_specs = [q_spec, kc_spec, vc_spec] + small
        args = [p0, p0, p0] + small_args
    block_bytes = 10 * seq * LANES * 4 if with_lat else 8 * ctx_len * LANES * 4
    return pl.pallas_call(
        functools.partial(_diff_attn_kernel, with_lat=with_lat, lambda_init=lambda_init, n_ctx=ctx_len,
                          sub_q=sub_q),
        grid=(n_batch, n_heads, n_q),
        in_specs=in_specs,
        out_specs=pl.BlockSpec((tq, LANES), lambda b, h, i: (b * n_q + i, h)),
        out_shape=jax.ShapeDtypeStruct((n_batch * n_q * tq, n_heads * LANES), BF16),
        scratch_shapes=[pltpu.VMEM((n_keys, LANES), BF16), pltpu.VMEM((LANES, n_keys), BF16)],
        compiler_params=_params(("arbitrary", "arbitrary", "arbitrary"), block_bytes),
        name="diff_attn_lat" if with_lat else "diff_attn_ctx",
    )(*args)


def _rope_tables(seq):
    half = DIFF_HEAD_DIM // 2
    pos = jnp.arange(seq)
    row, col = pos // GRID_W, pos % GRID_W
    inv_freq = ROPE_BASE ** (-jnp.arange(0, half, 2, dtype=F32) / half)
    lane = np.arange(LANES) % DIFF_HEAD_DIM
    use_col = jnp.asarray(lane >= half)
    first = jnp.asarray((lane % half) < half // 2)
    freq = inv_freq[jnp.asarray(lane % (half // 2))]
    p = jnp.where(use_col[None, :], col[:, None], row[:, None]).astype(F32)
    ang = p * freq[None, :]
    cos, sin = jnp.cos(ang), jnp.sin(ang)
    return cos, jnp.where(first[None, :], -sin, 0.0), jnp.where(first[None, :], 0.0, sin)


def _rpb_gather_kernel(rpb_ref, o_ref):
    n = o_ref.shape[1]
    k = rpb_ref.shape[1]
    colid = pl.program_id(0) * n + lax.broadcasted_iota(jnp.int32, (k, n), 1)
    j = lax.broadcasted_iota(jnp.int32, (k, n), 0)
    qc = lax.shift_right_logical(colid, int(math.log2(LANES)))
    half = lax.shift_right_logical(colid & (LANES - 1), int(math.log2(GRID_W)))
    kc = colid & (GRID_W - 1)
    sel = (j == half * LANES + jnp.clip(kc - qc + (WIN_COLS - 1), 0, 2 * WIN_COLS - 2)).astype(F32)
    o_ref[...] = jnp.dot(rpb_ref[...], sel, preferred_element_type=F32, precision=lax.Precision.HIGHEST)


def _rpb_pairs(rpb):
    nh, nr, ncol = rpb.shape
    n_slot = nr + 1
    left = jnp.pad(rpb, ((0, 0), (1, 0), (0, LANES - ncol)))
    right = jnp.pad(rpb, ((0, 0), (0, 1), (0, LANES - ncol)))
    rows = jnp.concatenate([left, right], axis=2).reshape(nh * n_slot, 2 * LANES)
    tn = 1024
    out = pl.pallas_call(
        _rpb_gather_kernel,
        grid=(GRID_W * LANES // tn,),
        in_specs=[pl.BlockSpec((nh * n_slot, 2 * LANES), lambda j: (0, 0))],
        out_specs=pl.BlockSpec((nh * n_slot, tn), lambda j: (0, j)),
        out_shape=jax.ShapeDtypeStruct((nh * n_slot, GRID_W * LANES), F32),
        compiler_params=_params(("arbitrary",), nh * n_slot * tn * 4 + 2 * LANES * tn * 4),
        name="rpb_gather",
    )(rows)
    return out.reshape(nh, n_slot, GRID_W, LANES)


def _na_block_plan(rows):
    kr = min(WIN_ROWS, rows)
    n_blk = rows // NA_Q_ROWS
    kb = np.clip(np.arange(n_blk) * NA_Q_ROWS - kr // 2, 0, rows - NA_K_ROWS)
    layouts, layout_of = [], []
    for blk in range(n_blk):
        dr = np.full((NA_Q_ROWS, NA_K_ROWS), -1, np.int64)
        for i in range(NA_Q_ROWS):
            r = blk * NA_Q_ROWS + i
            rs = int(np.clip(r - kr // 2, 0, rows - kr))
            for j in range(NA_K_ROWS):
                krow = kb[blk] + j
                if rs <= krow < rs + kr:
                    dr[i, j] = krow - r + WIN_ROWS - 1
        key = dr.tobytes()
        if key not in [l.tobytes() for l in layouts]:
            layouts.append(dr)
        layout_of.append([l.tobytes() for l in layouts].index(key))
    return kb, np.stack(layouts), np.asarray(layout_of)


def _na_bias_kernel(tp_ref, o_ref, *, layouts):
    qc = lax.broadcasted_iota(jnp.int32, (GRID_W, LANES), 0)
    lane = lax.broadcasted_iota(jnp.int32, (GRID_W, LANES), 1)
    kc = lane & (GRID_W - 1)
    left = lane < GRID_W
    col_start = jnp.clip(qc - WIN_COLS // 2, 0, GRID_W - WIN_COLS)
    col_ok = (kc >= col_start) & (kc < col_start + WIN_COLS)
    n_lay, n_q, n_k = layouts.shape
    for lay in range(n_lay):
        for i in range(n_q):
            for p in range(-(-n_k // 2)):
                d_l = int(layouts[lay, i, 2 * p])
                d_r = int(layouts[lay, i, 2 * p + 1]) if 2 * p + 1 < n_k else -1
                width = LANES if 2 * p + 1 < n_k else GRID_W
                if d_l < 0 and d_r < 0:
                    tile = jnp.full((GRID_W, LANES), -jnp.inf, F32)
                else:
                    assert d_l < 0 or d_r < 0 or d_r == d_l + 1
                    ok = col_ok
                    if d_l < 0:
                        ok = ok & jnp.logical_not(left)
                    if d_r < 0:
                        ok = ok & left
                    slot = d_r if d_r >= 0 else d_l + 1
                    tile = jnp.where(ok, tp_ref[slot] * LOG2E, -jnp.inf)
                o_ref[lay, i * GRID_W:(i + 1) * GRID_W, p * LANES:p * LANES + width] = tile[:, :width]


def _na_bias(rpb_pairs, layouts):
    nh, n_slot = rpb_pairs.shape[:2]
    n_lay = layouts.shape[0]
    tq, nkw = NA_Q_ROWS * GRID_W, NA_K_ROWS * GRID_W
    return pl.pallas_call(
        functools.partial(_na_bias_kernel, layouts=layouts),
        grid=(nh,),
        in_specs=[pl.BlockSpec((None, n_slot, GRID_W, LANES), lambda h: (h, 0, 0, 0))],
        out_specs=pl.BlockSpec((None, n_lay, tq, nkw), lambda h: (h, 0, 0, 0)),
        out_shape=jax.ShapeDtypeStruct((nh, n_lay, tq, nkw), F32),
        compiler_params=_params(("arbitrary",), n_lay * tq * nkw * 4),
        name="na_bias",
    )(rpb_pairs)


def _na_kernel(q_ref, kl_ref, vl_ref, kc_ref, vc_ref, bias_ref, o_ref, *, plan):
    tq = NA_Q_ROWS * GRID_W
    nkw = NA_K_ROWS * GRID_W
    nt = (((1,), (1,)), ((), ()))
    kc = kc_ref[...]
    vc = vc_ref[...]

    def scores(blk):
        kb, lay = plan[blk]
        q = q_ref[blk * tq:(blk + 1) * tq, :]
        ks = slice(kb * GRID_W, kb * GRID_W + nkw)
        return (lax.dot_general(q, kl_ref[ks, :], nt, preferred_element_type=F32) + bias_ref[lay],
                lax.dot_general(q, kc, nt, preferred_element_type=F32))

    def weighted_values(blk, e_w, e_c, l):
        kb, _ = plan[blk]
        ks = slice(kb * GRID_W, kb * GRID_W + nkw)
        o = (jnp.dot(e_w, vl_ref[ks, :], preferred_element_type=F32) + jnp.dot(e_c, vc, preferred_element_type=F32))
        o_ref[blk * tq:(blk + 1) * tq, :] = (o * (1.0 / l)).astype(o_ref.dtype)

    s_next = scores(0)
    pending = None
    for blk in range(len(plan)):
        s_w, s_c = s_next
        if blk + 1 < len(plan):
            s_next = scores(blk + 1)
        if pending is not None:
            weighted_values(blk - 1, *pending)
        m = jnp.maximum(jnp.max(s_w, axis=-1, keepdims=True), jnp.max(s_c, axis=-1, keepdims=True))
        e_w = jnp.exp2(s_w - m)
        e_c = jnp.exp2(s_c - m)
        l = jnp.sum(e_w, axis=-1, keepdims=True) + jnp.sum(e_c, axis=-1, keepdims=True)
        pending = (e_w.astype(BF16), e_c.astype(BF16), l)
    weighted_values(len(plan) - 1, *pending)


def _na_attn(p1, rpb, n_batch, seq, ctx_len, n_heads):
    rows = seq // GRID_W
    kb, layouts, layout_of = _na_block_plan(rows)
    bias = _na_bias(_rpb_pairs(rpb), layouts)
    n_lay = layouts.shape[0]
    tq = NA_Q_ROWS * GRID_W
    nkw = NA_K_ROWS * GRID_W
    ctx_rb0 = (n_batch * seq) // ctx_len
    plan = tuple((int(k), int(l)) for k, l in zip(kb, layout_of))
    return pl.pallas_call(
        functools.partial(_na_kernel, plan=plan),
        grid=(n_heads, n_batch),
        in_specs=[
            pl.BlockSpec((seq, LANES), lambda h, b: (b, h)),
            pl.BlockSpec((seq, LANES), lambda h, b: (b, n_heads + h)),
            pl.BlockSpec((seq, LANES), lambda h, b: (b, 2 * n_heads + h)),
            pl.BlockSpec((ctx_len, LANES), lambda h, b: (ctx_rb0 + b, n_heads + h)),
            pl.BlockSpec((ctx_len, LANES), lambda h, b: (ctx_rb0 + b, 2 * n_heads + h)),
            pl.BlockSpec((None, n_lay, tq, nkw), lambda h, b: (h, 0, 0, 0)),
        ],
        out_specs=pl.BlockSpec((seq, LANES), lambda h, b: (b, h)),
        out_shape=jax.ShapeDtypeStruct((n_batch * seq, n_heads * LANES), BF16),
        compiler_params=_params(("arbitrary", "arbitrary"), n_lay * tq * nkw * 4 + 4 * seq * LANES * 2),
        name="na_attn",
    )(p1, p1, p1, p1, p1, bias)


def _pack_rows(rows, n_rows=8):
    out = jnp.zeros((n_rows, LANES), F32)
    for r, v in enumerate(rows):
        out = out.at[r, :v.shape[0]].set(v.astype(F32))
    return out


def kernel(x, c, ctx, c_ctx, ada_w, ada_b, norm_mix_g, norm_ffn_g, final_norm_g, ffn_w1, ffn_w3, ffn_w2, ev_w_in, ev_conv_w, ev_conv_b, ev_a_log, ev_dt_bias, ev_d_skip, ev_ssm_norm_g, ev_lam_q1, ev_lam_k1, ev_lam_q2, ev_lam_k2, ev_subln_g, ev_w_out, od_w_in, od_rpb, od_w_out):
    n_batch, seq, d = x.shape
    ctx_len = ctx.shape[1]
    depth = ada_w.shape[0]
    n_lat = n_batch * seq
    n_tok = n_lat + n_batch * ctx_len

    d_ssm = ev_ssm_norm_g.shape[1]
    n_ssm_heads = ev_d_skip.shape[1]
    d_xbc = ev_conv_w.shape[2]
    d_qk = d_v = (ev_w_in.shape[2] - d_ssm - d_xbc - 2 * n_ssm_heads) // 3
    n_diff_heads = d_v // (2 * DIFF_HEAD_DIM)
    n_na_heads = od_rpb.shape[1]

    stream = (x.reshape(n_lat, d), ctx.reshape(n_batch * ctx_len, d))
    cond = jnp.zeros((COND_ROWS, d), F32).at[:n_batch].set(c).at[n_batch].set(c_ctx)
    mod = _ada_mod(cond, ada_w, ada_b).reshape(depth, COND_ROWS, 1, 6 * d)
    rope_tabs = _rope_tables(seq)
    kw = dict(seq=seq, n_batch=n_batch)
    ffn_w = (ffn_w1, ffn_w3, ffn_w2)

    for i in range(depth):
        ctx_out = i < depth - 1
        m_rows = n_tok if ctx_out else n_lat
        j = i // 2
        if i % 2 == 0:
            lambda_init = 0.8 - 0.6 * math.exp(-0.3 * i)
            w_in = ev_w_in[j]
            dt0 = d_ssm + d_xbc
            w_t = jnp.swapaxes(w_in, 0, 1)
            w_parts = [w_t[:dt0].astype(BF16), w_t[dt0 + 2 * n_ssm_heads:].astype(BF16)]
            w_dt = jnp.zeros((2 * LANES, d), F32)
            w_dt = w_dt.at[:n_ssm_heads].set(w_t[dt0:dt0 + n_ssm_heads])
            w_dt = w_dt.at[LANES:LANES + n_ssm_heads].set(w_t[dt0 + n_ssm_heads:dt0 + 2 * n_ssm_heads])
            p0, dt_raw = _proj(stream, norm_mix_g[i], mod, i, w_parts, w_dt.astype(BF16), F32, **kw)
            q0 = d_ssm + d_xbc
            k0, v0 = q0 + d_qk, q0 + 2 * d_qk

            xbc = _ssd_conv(p0, d_ssm, d_xbc, ev_conv_w[j], ev_conv_b[j], seq, ctx_len, n_lat)
            prm = jnp.stack([_pack_rows([ev_a_log[j, r], ev_dt_bias[j, r]]) for r in range(2)])
            y2 = _ssd_scan(xbc, dt_raw, prm, n_batch, seq, ctx_len, d_ssm)
            mix_ssd = _ssd_finish(y2, xbc, p0, jnp.repeat(ev_d_skip[j], SSM_HEAD_DIM), ev_ssm_norm_g[j], d_ssm)

            lam_p = _pack_rows([ev_lam_q1[j], ev_lam_k1[j], ev_lam_q2[j], ev_lam_k2[j]])
            attn_args = (p0, q0, k0, v0, n_diff_heads, rope_tabs, lam_p, ev_subln_g[j], lambda_init,
                         n_batch, seq, ctx_len)
            mix_attn = _diff_attn(*attn_args, with_lat=True)
            if ctx_out:
                mix_attn = (mix_attn, _diff_attn(*attn_args, with_lat=False))
            lhs = [mix_ssd, mix_attn]
            w_out = ev_w_out[j].astype(BF16)
        else:
            d_na = n_na_heads * NA_HEAD_DIM
            q_scale = jnp.where(jnp.arange(3 * d_na) < d_na, NA_HEAD_DIM ** -0.5 * LOG2E, 1.0).astype(F32)
            p1 = _proj(stream, norm_mix_g[i], mod, i, od_w_in[j].astype(BF16), None, BF16, col_scale=q_scale, **kw)
            assert not ctx_out, "context-query neighbourhood layers are not needed at this depth"
            lhs = [_na_attn(p1, od_rpb[j], n_batch, seq, ctx_len, n_na_heads)]
            w_out = od_w_out[j].astype(BF16)
        stream = _out_proj(lhs, w_out, stream, mod, i, 2, m_rows, **kw)
        stream = _ffn(stream, norm_ffn_g[i], mod, i, *ffn_w, final_norm_g, not ctx_out, m_rows, **kw)
    return stream.reshape(n_batch, seq, d)
```

```python
import functools
import math

import jax
import jax.numpy as jnp
import numpy as np
from jax import lax
from jax.experimental import pallas as pl
from jax.experimental.pallas import tpu as pltpu

F32 = jnp.float32
BF16 = jnp.bfloat16

GRID_W = 64
SSM_HEAD_DIM = 64
SSM_GROUPS = 4
SSM_STATE = 128
SSM_CONV = 5
SSM_CHUNK = 128
DIFF_HEAD_DIM = 64
NA_HEAD_DIM = 128
WIN_ROWS = 8
WIN_COLS = 16
ROPE_BASE = 10000.0
NORM_EPS = 1e-6
LOG2E = math.log2(math.e)

LANES = 128
V7X_VMEM_BYTES = 64 * 1024 * 1024
VMEM_HEADROOM_BYTES = 3 * 1024 * 1024

NA_Q_ROWS = 4
NA_K_ROWS = NA_Q_ROWS + WIN_ROWS - 1
COND_ROWS = 16


def _vmem_limit(block_bytes):
    return int(min(V7X_VMEM_BYTES - VMEM_HEADROOM_BYTES, max(32 * 1024 * 1024, 2 * block_bytes + 16 * 1024 * 1024)))


def _params(semantics, block_bytes):
    return pltpu.CompilerParams(dimension_semantics=semantics, vmem_limit_bytes=_vmem_limit(block_bytes))


def _silu(v):
    return v * jax.nn.sigmoid(v)


def _mod_row(i, tm, seq, n_batch):
    return jnp.minimum((i * tm) // seq, n_batch)


def _ada_kernel(cond_ref, w_ref, b_ref, o_ref):
    s = _silu(cond_ref[...]).astype(BF16)
    o_ref[...] = jnp.dot(s, w_ref[...].astype(BF16), preferred_element_type=F32) + b_ref[...]


def _ada_mod(cond, ada_w, ada_b, tn=1024):
    depth, d, n = ada_w.shape
    return pl.pallas_call(
        _ada_kernel,
        grid=(depth, n // tn),
        in_specs=[
            pl.BlockSpec((COND_ROWS, d), lambda l, j: (0, 0)),
            pl.BlockSpec((None, d, tn), lambda l, j: (l, 0, j)),
            pl.BlockSpec((None, 1, tn), lambda l, j: (l, 0, j)),
        ],
        out_specs=pl.BlockSpec((None, COND_ROWS, tn), lambda l, j: (l, 0, j)),
        out_shape=jax.ShapeDtypeStruct((depth, COND_ROWS, n), F32),
        compiler_params=_params(("arbitrary", "arbitrary"), d * tn * 4),
        name="ada_mod",
    )(cond, ada_w, ada_b.reshape(depth, 1, n))


def _norm_mod_pipeline(x_ref, g_ref, mod_ref, shift_idx, scale_idx, n_chunks, consume):
    tm, d = x_ref.shape
    shift = mod_ref[:, shift_idx * d:(shift_idx + 1) * d]
    gain = g_ref[...] * (1.0 + mod_ref[:, scale_idx * d:(scale_idx + 1) * d])
    chunk = tm // n_chunks

    def norm_rows(c):
        x = x_ref[c * chunk:(c + 1) * chunk, :]
        inv = lax.rsqrt(jnp.mean(x * x, axis=-1, keepdims=True) + NORM_EPS)
        return (x * inv * gain + shift).astype(BF16)

    h_next = norm_rows(0)
    for c in range(n_chunks):
        h = h_next
        if c + 1 < n_chunks:
            h_next = norm_rows(c + 1)
        consume(slice(c * chunk, (c + 1) * chunk), h)


def _proj_kernel(*refs, n_x, n_w, tiles_per_part, n_lat_tiles, has_aux, has_scale, first_step_chunks=4):
    rest = list(refs)
    x_refs = [rest.pop(0) for _ in range(n_x)]
    g_ref, mod_ref = rest.pop(0), rest.pop(0)
    w_refs = [rest.pop(0) for _ in range(max(n_w, 1))]
    cs_ref = rest.pop(0) if has_scale else None
    if has_aux:
        waux_ref, o_ref, oaux_ref, h_ref = rest
    else:
        o_ref, h_ref = rest
    dims = (((1,), (1,)), ((), ())) if n_w else (((1,), (0,)), ((), ()))
    j = pl.program_id(1)
    tm, d = x_refs[0].shape

    def emit(h, rows, w_ref):
        acc = lax.dot_general(h, w_ref[...], dims, preferred_element_type=F32)
        if has_scale:
            acc = acc * cs_ref[...]
        o_ref[rows, :] = acc.astype(o_ref.dtype)

    def first_tile(x_ref):
        def consume(rows, h):
            h_ref[rows, :] = h
            emit(h, rows, w_refs[0])
            if has_aux:
                oaux_ref[rows, :] = lax.dot_general(h, waux_ref[...], dims, preferred_element_type=F32)

        _norm_mod_pipeline(x_ref, g_ref, mod_ref, 0, 1, first_step_chunks, consume)

    if n_x == 1:
        pl.when(j == 0)(functools.partial(first_tile, x_refs[0]))
    else:
        is_lat = pl.program_id(0) < n_lat_tiles
        pl.when(jnp.logical_and(j == 0, is_lat))(functools.partial(first_tile, x_refs[0]))
        pl.when(jnp.logical_and(j == 0, jnp.logical_not(is_lat)))(functools.partial(first_tile, x_refs[1]))

    for p in range(max(n_w, 1)):
        lo = max(p * tiles_per_part, 1)
        in_part = (j >= lo) if n_w == 0 else jnp.logical_and(j >= lo, j < (p + 1) * tiles_per_part)
        pl.when(in_part)(lambda p=p: emit(h_ref[...], slice(None), w_refs[p]))


def _proj(x, g, mod, layer, w, w_aux, out_dtype, seq, n_batch, col_scale=None, tm=1024, tn=1024):
    parts = list(w) if isinstance(w, (list, tuple)) else None
    if parts is None:
        d, n = w.shape
        n_w, tiles_per_part = 0, 0
    else:
        d = parts[0].shape[1]
        n_w, tiles_per_part = len(parts), parts[0].shape[0] // tn
        assert all(p.shape == (tiles_per_part * tn, d) for p in parts)
        n = n_w * tiles_per_part * tn
    m = sum(a.shape[0] for a in x) if isinstance(x, tuple) else x.shape[0]
    n_lat_tiles = (n_batch * seq) // tm
    has_aux = w_aux is not None
    has_scale = col_scale is not None
    mod_spec = pl.BlockSpec((None, None, 1, mod.shape[-1]),
                            lambda i, j: (layer, _mod_row(i, tm, seq, n_batch), 0, 0))
    in_specs, args = _row_operand(x, tm, n_lat_tiles, d, lambda j: 0, single_ctx_buffer=True)
    n_x = len(args)
    in_specs += [pl.BlockSpec((1, d), lambda i, j: (0, 0)), mod_spec]
    args += [g.reshape(1, d), mod]
    if parts is None:
        in_specs.append(pl.BlockSpec((d, tn), lambda i, j: (0, j)))
        args.append(w)
    else:
        for p, part in enumerate(parts):
            in_specs.append(pl.BlockSpec(
                (tn, d), lambda i, j, p=p: (jnp.clip(j - p * tiles_per_part, 0, tiles_per_part - 1), 0)))
            args.append(part)
    out_specs = [pl.BlockSpec((tm, tn), lambda i, j: (i, j))]
    out_shape = [jax.ShapeDtypeStruct((m, n), out_dtype)]
    if has_scale:
        in_specs.append(pl.BlockSpec((1, tn), lambda i, j: (0, j)))
        args.append(col_scale.reshape(1, n))
    if has_aux:
        na = w_aux.shape[0] if parts is not None else w_aux.shape[1]
        in_specs.append(pl.BlockSpec(w_aux.shape, lambda i, j: (0, 0)))
        out_specs.append(pl.BlockSpec((tm, na), lambda i, j: (i, 0)))
        out_shape.append(jax.ShapeDtypeStruct((m, na), F32))
        args.append(w_aux)
    block_bytes = n_x * tm * d * 4 + max(n_w, 1) * d * tn * 2 + tm * tn * 4 + tm * d
    outs = pl.pallas_call(
        functools.partial(_proj_kernel, n_x=n_x, n_w=n_w, tiles_per_part=tiles_per_part, n_lat_tiles=n_lat_tiles,
                          has_aux=has_aux, has_scale=has_scale),
        grid=(m // tm, n // tn),
        in_specs=in_specs,
        out_specs=out_specs,
        out_shape=out_shape,
        scratch_shapes=[pltpu.VMEM((tm, d), BF16)],
        compiler_params=_params(("arbitrary", "arbitrary"), block_bytes),
        name="proj",
    )(*args)
    return outs if has_aux else outs[0]


def _row_operand(a, tm, n_lat_tiles, width, col_of_j, single_ctx_buffer=False):
    if not isinstance(a, tuple):
        return [pl.BlockSpec((tm, width), lambda i, j: (i, col_of_j(j)))], [a]
    lat, ctx = a
    assert ctx.shape[0] == tm and lat.shape[0] == n_lat_tiles * tm

    def lat_index(i, j):
        return jnp.minimum(i, n_lat_tiles - 1), jnp.where(i < n_lat_tiles, col_of_j(j), 0)

    def ctx_index(i, j):
        return 0, jnp.where(i < n_lat_tiles, 0, col_of_j(j))

    ctx_mode = dict(pipeline_mode=pl.Buffered(1)) if single_ctx_buffer else {}
    return [pl.BlockSpec((tm, width), lat_index), pl.BlockSpec((tm, width), ctx_index, **ctx_mode)], [lat, ctx]


def _pick_rows(refs, n_lat_tiles):
    if len(refs) == 1:
        return refs[0][...]
    return jnp.where(pl.program_id(0) < n_lat_tiles, refs[0][...], refs[1][...])


def _out_proj_kernel(*refs, arity, n_lat_tiles):
    refs = list(refs)
    n_lhs = len(arity) - 1
    groups = [[refs.pop(0) for _ in range(n)] for n in arity[:-1]]
    ws = [refs.pop(0) for _ in range(n_lhs)]
    res = [refs.pop(0) for _ in range(arity[-1])]
    gate_ref, o_ref = refs
    acc = None
    for grp, w_ref in zip(groups, ws):
        part = jnp.dot(_pick_rows(grp, n_lat_tiles), w_ref[...], preferred_element_type=F32)
        acc = part if acc is None else acc + part
    o_ref[...] = _pick_rows(res, n_lat_tiles) + gate_ref[...] * acc


def _out_proj(lhs_list, w, res, mod, layer, gate_idx, m_rows, seq, n_batch, tm=1024, tn=1024):
    d = w.shape[1]
    n_lat_tiles = (n_batch * seq) // tm
    in_specs, args, arity, widths = [], [], [], []
    for a in lhs_list:
        kk = (a[0] if isinstance(a, tuple) else a).shape[1]
        sp, ar = _row_operand(a, tm, n_lat_tiles, kk, lambda j: 0)
        in_specs += sp
        args += ar
        arity.append(len(ar))
        widths.append(kk)
    row0 = 0
    for kk in widths:
        in_specs.append(pl.BlockSpec((kk, tn), lambda i, j, rb=row0 // kk: (rb, j)))
        args.append(w)
        row0 += kk
    sp, ar = _row_operand(res, tm, n_lat_tiles, tn, lambda j: j)
    in_specs += sp
    args += ar
    arity.append(len(ar))
    in_specs.append(pl.BlockSpec((None, None, 1, tn),
                                 lambda i, j: (layer, _mod_row(i, tm, seq, n_batch), 0, gate_idx * (d // tn) + j)))
    args.append(mod)
    k_total = sum(widths)
    block_bytes = 2 * tm * k_total * 2 + k_total * tn * 2 + 3 * tm * tn * 4
    return pl.pallas_call(
        functools.partial(_out_proj_kernel, arity=tuple(arity), n_lat_tiles=n_lat_tiles),
        grid=(m_rows // tm, d // tn),
        in_specs=in_specs,
        out_specs=pl.BlockSpec((tm, tn), lambda i, j: (i, j)),
        out_shape=jax.ShapeDtypeStruct((m_rows, d), F32),
        compiler_params=_params(("arbitrary", "arbitrary"), block_bytes),
        name="out_proj",
    )(*args)


def _ffn_kernel(x_ref, g_ref, mod_ref, w1_ref, w3_ref, w2_ref, fg_ref, o_ref, h_ref, *, final_norm, rows,
                first_step_chunks):
    j = pl.program_id(1)
    tm, d = x_ref.shape

    def swiglu(h, w1, w3, w2):
        a = jnp.dot(h, w1, preferred_element_type=F32)
        b = jnp.dot(h, w3, preferred_element_type=F32)
        return jnp.dot((_silu(a) * b).astype(BF16), w2, preferred_element_type=F32)

    @pl.when(j == 0)
    def _():
        w1, w3, w2 = (w_ref[...].astype(BF16) for w_ref in (w1_ref, w3_ref, w2_ref))

        def consume(rows, h):
            h_ref[rows, :] = h
            o_ref[rows, :] = swiglu(h, w1, w3, w2)

        _norm_mod_pipeline(x_ref, g_ref, mod_ref, 3, 4, first_step_chunks, consume)

    @pl.when(j > 0)
    def _():
        o_ref[...] += swiglu(h_ref[...], w1_ref[...].astype(BF16), w3_ref[...].astype(BF16),
                             w2_ref[...].astype(BF16))

    @pl.when(j == pl.num_programs(1) - 1)
    def _():
        gate = mod_ref[:, 5 * d:6 * d]
        fg = fg_ref[...]

        def body(r, carry):
            sl = pl.ds(pl.multiple_of(r * rows, rows), rows)
            y = x_ref[sl, :] + gate * o_ref[sl, :]
            if final_norm:
                y = y * lax.rsqrt(jnp.mean(y * y, axis=-1, keepdims=True) + NORM_EPS) * fg
            o_ref[sl, :] = y
            return carry

        lax.fori_loop(0, x_ref.shape[0] // rows, body, 0)


def _ffn(x, g, mod, layer, w1, w3, w2, final_g, final_norm, m_rows, seq, n_batch, tm=1024, tf=256):
    d = x.shape[1]
    ff = w1.shape[2]
    block_bytes = 2 * tm * d * 4 + d * tf * sum(w.dtype.itemsize for w in (w1, w3, w2)) + tm * d
    return pl.pallas_call(
        functools.partial(_ffn_kernel, final_norm=final_norm, rows=128, first_step_chunks=4),
        grid=(m_rows // tm, ff // tf),
        in_specs=[
            pl.BlockSpec((tm, d), lambda i, j: (i, 0)),
            pl.BlockSpec((1, d), lambda i, j: (0, 0)),
            pl.BlockSpec((None, None, 1, mod.shape[-1]),
                         lambda i, j: (layer, _mod_row(i, tm, seq, n_batch), 0, 0)),
            pl.BlockSpec((None, d, tf), lambda i, j: (layer, 0, j)),
            pl.BlockSpec((None, d, tf), lambda i, j: (layer, 0, j)),
            pl.BlockSpec((None, tf, d), lambda i, j: (layer, j, 0)),
            pl.BlockSpec((1, d), lambda i, j: (0, 0)),
        ],
        out_specs=pl.BlockSpec((tm, d), lambda i, j: (i, 0)),
        out_shape=jax.ShapeDtypeStruct((m_rows, d), F32),
        scratch_shapes=[pltpu.VMEM((tm, d), BF16)],
        compiler_params=_params(("arbitrary", "arbitrary"), block_bytes),
        name="ffn",
    )(x, g.reshape(1, d), mod, w1, w3, w2, final_g.reshape(1, d))


def _conv_kernel(prev_ref, cur_ref, next_ref, w_ref, b_ref, o_ref, ext_ref, *, tm, halo, seq, ctx_len, n_lat_tiles):
    i = pl.program_id(0)
    half = SSM_CONV // 2
    ext_ref[halo:halo + tm, :] = cur_ref[...]

    def taps(mask_of):
        acc = jnp.zeros(cur_ref.shape, F32) + b_ref[...]
        for t in range(SSM_CONV):
            tap = ext_ref[halo - half + t:halo - half + t + tm, :]
            acc = acc + mask_of(t, tap) * w_ref[t:t + 1, :]
        o_ref[...] = _silu(acc)

    @pl.when(i < n_lat_tiles)
    def _():
        prev_in_seg = ((i * tm) & (seq - 1)) != 0
        next_in_seg = (((i + 1) * tm) & (seq - 1)) != 0
        ext_ref[0:halo, :] = jnp.where(prev_in_seg, prev_ref[...], 0.0)
        ext_ref[halo + tm:, :] = jnp.where(next_in_seg, next_ref[...], 0.0)
        taps(lambda t, tap: tap)

    @pl.when(i >= n_lat_tiles)
    def _():
        ext_ref[0:halo, :] = prev_ref[...]
        ext_ref[halo + tm:, :] = next_ref[...]
        pos = (i * tm + lax.broadcasted_iota(jnp.int32, (tm, 1), 0)) & (ctx_len - 1)

        def masked(t, tap):
            src = pos + (t - half)
            return jnp.where((src >= 0) & (src < ctx_len), tap, 0.0)

        taps(masked)


def _ssd_conv(p0, col0, width, conv_w, conv_b, seq, ctx_len, n_lat_rows, tm=1024, tc=1024, halo=8):
    m = p0.shape[0]
    assert seq % tm == 0 and tm % ctx_len == 0 and (seq & (seq - 1)) == 0 and (ctx_len & (ctx_len - 1)) == 0
    cb0 = col0 // tc
    hb = tm // halo
    last_hb = m // halo - 1
    wpad = jnp.zeros((8, width), F32).at[:SSM_CONV].set(conv_w)
    return pl.pallas_call(
        functools.partial(_conv_kernel, tm=tm, halo=halo, seq=seq, ctx_len=ctx_len, n_lat_tiles=n_lat_rows // tm),
        grid=(m // tm, width // tc),
        in_specs=[
            pl.BlockSpec((halo, tc), lambda i, j: (jnp.maximum(i * hb - 1, 0), cb0 + j)),
            pl.BlockSpec((tm, tc), lambda i, j: (i, cb0 + j)),
            pl.BlockSpec((halo, tc), lambda i, j: (jnp.minimum((i + 1) * hb, last_hb), cb0 + j)),
            pl.BlockSpec((8, tc), lambda i, j: (0, j)),
            pl.BlockSpec((1, tc), lambda i, j: (0, j)),
        ],
        out_specs=pl.BlockSpec((tm, tc), lambda i, j: (i, j)),
        out_shape=jax.ShapeDtypeStruct((m, width), F32),
        scratch_shapes=[pltpu.VMEM((tm + 2 * halo, tc), F32)],
        compiler_params=_params(("arbitrary", "arbitrary"), 3 * tm * tc * 4),
        name="ssd_conv",
    )(p0, p0, p0, wpad, conv_b.reshape(1, width))


def _ssd_scan_kernel(xs_f, bm_f, cm_f, dt_f, xs_b, bm_b, cm_b, dt_b, prm_ref, yf_ref, yb_ref, state_ref, *, n_heads):
    @pl.when(pl.program_id(1) == 0)
    def _():
        state_ref[...] = jnp.zeros_like(state_ref)

    fwd = _ssd_chunk(xs_f, bm_f, cm_f, dt_f, prm_ref.at[0], yf_ref, state_ref.at[0], backward=False, n_heads=n_heads)
    bwd = _ssd_chunk(xs_b, bm_b, cm_b, dt_b, prm_ref.at[1], yb_ref, state_ref.at[1], backward=True, n_heads=n_heads)
    for g in range(SSM_GROUPS):
        fwd(g)
        bwd(g)


def _ssd_chunk(xs_ref, bm_ref, cm_ref, dt_ref, prm_ref, y_ref, state_ref, *, backward, n_heads):
    hp = SSM_HEAD_DIM
    rep = n_heads // SSM_GROUPS
    lc = SSM_CHUNK
    assert 2 * hp == LANES and rep % 2 == 0 and SSM_STATE == LANES and lc == LANES

    dt_in = dt_ref[...] + prm_ref[1:2, :]
    dt = jnp.maximum(dt_in, 0.0) + jnp.log1p(jnp.exp(-jnp.abs(dt_in)))
    dta = dt * (-jnp.exp(prm_ref[0:1, :]) * LOG2E)
    row = lax.broadcasted_iota(jnp.int32, (lc, lc), 0)
    col = lax.broadcasted_iota(jnp.int32, (lc, lc), 1)
    causal = (row <= col) if backward else (row >= col)
    a_cum = jnp.dot(causal.astype(F32), dta, preferred_element_type=F32, precision=lax.Precision.HIGHEST)
    a_tot = a_cum[0:1, :] if backward else a_cum[lc - 1:lc, :]
    w_end = dt * jnp.exp2(a_tot - a_cum)
    a_cum_t, dt_t, w_end_t = a_cum.T, dt.T, w_end.T
    lo = col < hp

    def do_group(g):
        gs = slice(g * SSM_STATE, (g + 1) * SSM_STATE)
        b_f = bm_ref[:, gs]
        c_g = cm_ref[:, gs].astype(BF16)
        cb = lax.dot_general(c_g, b_f.astype(BF16), (((1,), (1,)), ((), ())), preferred_element_type=F32)
        b_t = b_f.T
        cols_g = slice(g * rep * hp, (g + 1) * rep * hp)
        y_off = jnp.dot(c_g, state_ref[:, cols_g].astype(BF16), preferred_element_type=F32)
        for pr in range(rep // 2):
            cols = slice(g * rep * hp + pr * LANES, g * rep * hp + (pr + 1) * LANES)
            xs2 = xs_ref[:, cols].astype(BF16)
            y_d, s_n, e_a = [], [], []
            for h in (g * rep + 2 * pr, g * rep + 2 * pr + 1):
                a_col = jnp.broadcast_to(a_cum[:, h:h + 1], (lc, lc))
                decay = jnp.exp2(jnp.where(causal, a_col - a_cum_t[h:h + 1, :], -jnp.inf))
                m = (cb * decay * dt_t[h:h + 1, :]).astype(BF16)
                y_d.append(jnp.dot(m, xs2, preferred_element_type=F32))
                s_n.append(jnp.dot((b_t * w_end_t[h:h + 1, :]).astype(BF16), xs2, preferred_element_type=F32))
                e_a.append(jnp.exp2(a_col))
            h0 = g * rep + 2 * pr
            y_ref[:, cols] = (jnp.where(lo, y_d[0], y_d[1])
                              + y_off[:, pr * LANES:(pr + 1) * LANES] * jnp.where(lo, e_a[0], e_a[1]))
            chunk_decay = jnp.where(lo[0:1, :], jnp.exp2(a_tot[:, h0:h0 + 1]), jnp.exp2(a_tot[:, h0 + 1:h0 + 2]))
            state_ref[:, cols] = state_ref[:, cols] * chunk_decay + jnp.where(lo, s_n[0], s_n[1])

    return do_group


def _ssd_scan(xbc, dt_raw, prm, n_batch, seq, ctx_len, d_ssm):
    m = xbc.shape[0]
    lc = SSM_CHUNK
    n_heads = d_ssm // SSM_HEAD_DIM
    nc_ctx, nc_lat = ctx_len // lc, seq // lc
    ctx_blk0 = n_batch * nc_lat
    gn = SSM_GROUPS * SSM_STATE

    def row_blk(b, s, backward):
        ctx_c = nc_ctx - 1 - s if backward else s
        lat_c = nc_lat - 1 - (s - nc_ctx) if backward else s - nc_ctx
        return jnp.where(s < nc_ctx, ctx_blk0 + b * nc_ctx + ctx_c, b * nc_lat + lat_c)

    def chunk_specs(backward):
        r = int(backward)
        return [
            pl.BlockSpec((lc, d_ssm), lambda b, s: (row_blk(b, s, backward), 0)),
            pl.BlockSpec((lc, gn), lambda b, s: (row_blk(b, s, backward), d_ssm // gn)),
            pl.BlockSpec((lc, gn), lambda b, s: (row_blk(b, s, backward), d_ssm // gn + 1)),
            pl.BlockSpec((lc, LANES), lambda b, s: (row_blk(b, s, backward), r)),
        ]

    return pl.pallas_call(
        functools.partial(_ssd_scan_kernel, n_heads=n_heads),
        grid=(n_batch, nc_ctx + nc_lat),
        in_specs=chunk_specs(False) + chunk_specs(True) + [pl.BlockSpec((2, 8, LANES), lambda b, s: (0, 0, 0))],
        out_specs=[pl.BlockSpec((lc, d_ssm), lambda b, s: (row_blk(b, s, False), 0)),
                   pl.BlockSpec((lc, d_ssm), lambda b, s: (row_blk(b, s, True), 0))],
        out_shape=[jax.ShapeDtypeStruct((m, d_ssm), F32), jax.ShapeDtypeStruct((m, d_ssm), F32)],
        scratch_shapes=[pltpu.VMEM((2, SSM_STATE, d_ssm), F32)],
        compiler_params=_params(("arbitrary", "arbitrary"), 8 * lc * d_ssm * 4),
        name="ssd_scan",
    )(xbc, xbc, xbc, dt_raw, xbc, xbc, xbc, dt_raw, prm)


def _ssd_finish_kernel(yf_ref, yb_ref, xs_ref, z_ref, dskip_ref, g_ref, o_ref, *, group_width):
    y = (yf_ref[...] + yb_ref[...] + dskip_ref[...] * xs_ref[...]) * _silu(z_ref[...])
    for g in range(y.shape[1] // group_width):
        sl = slice(g * group_width, (g + 1) * group_width)
        v = y[:, sl]
        vn = v * lax.rsqrt(jnp.mean(v * v, axis=-1, keepdims=True) + NORM_EPS) * g_ref[:, sl]
        o_ref[:, sl] = vn.astype(o_ref.dtype)


def _ssd_finish(y2, xbc, p0, d_skip_lanes, norm_g, d_ssm, tm=512):
    m = xbc.shape[0]
    return pl.pallas_call(
        functools.partial(_ssd_finish_kernel, group_width=d_ssm // SSM_GROUPS),
        grid=(m // tm,),
        in_specs=[
            pl.BlockSpec((tm, d_ssm), lambda i: (i, 0)),
            pl.BlockSpec((tm, d_ssm), lambda i: (i, 0)),
            pl.BlockSpec((tm, d_ssm), lambda i: (i, 0)),
            pl.BlockSpec((tm, d_ssm), lambda i: (i, 0)),
            pl.BlockSpec((1, d_ssm), lambda i: (0, 0)),
            pl.BlockSpec((1, d_ssm), lambda i: (0, 0)),
        ],
        out_specs=pl.BlockSpec((tm, d_ssm), lambda i: (i, 0)),
        out_shape=jax.ShapeDtypeStruct((m, d_ssm), BF16),
        compiler_params=_params(("arbitrary",), 5 * tm * d_ssm * 4),
        name="ssd_finish",
    )(*y2, xbc, p0, d_skip_lanes.reshape(1, d_ssm), norm_g.reshape(1, d_ssm))


def _rope(x, cos, sin_up, sin_dn):
    quarter = DIFF_HEAD_DIM // 4
    return (x * cos + pltpu.roll(x, LANES - quarter, axis=1) * sin_up + pltpu.roll(x, quarter, axis=1) * sin_dn)


def _diff_attn_heads_kernel(*refs, heads_per_step, with_lat, **kw):
    n_head_refs = 5 if with_lat else 3
    *ins, o_ref, k_scr, vt_scr = refs
    for h in range(heads_per_step):
        cs = slice(h * LANES, (h + 1) * LANES)
        head_ins = [r.at[:, cs] for r in ins[:n_head_refs]] + list(ins[n_head_refs:])
        _diff_attn_kernel(*head_ins, o_ref.at[:, cs], k_scr.at[h], vt_scr.at[h], with_lat=with_lat, **kw)


def _diff_attn_kernel(*refs, with_lat, lambda_init, n_ctx, sub_q, key_chunk=768):
    if with_lat:
        (q_ref, kc_ref, kl_ref, vc_ref, vl_ref, qcos_ref, qsu_ref, qsd_ref, kcos_ref, ksu_ref, ksd_ref,
         lam_ref, g_ref, o_ref, k_scr, vt_scr) = refs
    else:
        q_ref, kc_ref, vc_ref, lam_ref, g_ref, o_ref, k_scr, vt_scr = refs

    @pl.when(pl.program_id(2) == 0)
    def _():
        k_scr[0:n_ctx, :] = kc_ref[...].astype(BF16)
        vt_scr[:, 0:n_ctx] = vc_ref[...].T.astype(BF16)
        if with_lat:
            k_scr[n_ctx:, :] = _rope(kl_ref[...], kcos_ref[...], ksu_ref[...], ksd_ref[...]).astype(BF16)
            vt_scr[:, n_ctx:] = vl_ref[...].T.astype(BF16)

    lam_p = lam_ref[...]
    lam = (jnp.exp(jnp.sum(lam_p[0:1, :] * lam_p[1:2, :], axis=-1, keepdims=True))
           - jnp.exp(jnp.sum(lam_p[2:3, :] * lam_p[3:4, :], axis=-1, keepdims=True)) + lambda_init)
    first = lax.broadcasted_iota(jnp.int32, (LANES, sub_q), 0) < DIFF_HEAD_DIM
    n_sub = q_ref.shape[0] // sub_q

    def scores(t):
        rows = slice(t * sub_q, (t + 1) * sub_q)
        q = q_ref[rows, :]
        if with_lat:
            q = _rope(q, qcos_ref[rows, :], qsu_ref[rows, :], qsd_ref[rows, :])
        qt = (q * (DIFF_HEAD_DIM ** -0.5 * LOG2E)).T
        q2t = jnp.concatenate([jnp.where(first, qt, 0.0), jnp.where(first, 0.0, qt)], axis=1).astype(BF16)
        blocks = [jnp.dot(k_scr[k0:k1, :], q2t, preferred_element_type=F32) for k0, k1 in key_chunks]
        return blocks, [jnp.max(b, axis=0, keepdims=True) for b in blocks]

    n_keys = k_scr.shape[0]
    key_chunks = [(k0, min(k0 + key_chunk, n_keys)) for k0 in range(0, n_keys, key_chunk)]
    s_next = scores(0)
    for t in range(n_sub):
        rows = slice(t * sub_q, (t + 1) * sub_q)
        s_blocks, maxes = s_next
        if t + 1 < n_sub:
            s_next = scores(t + 1)
        m = functools.reduce(jnp.maximum, maxes)
        ovt, l = None, None
        for (k0, k1), s in zip(key_chunks, s_blocks):
            e = jnp.exp2(s - m)
            part = jnp.dot(vt_scr[:, k0:k1], e.astype(BF16), preferred_element_type=F32)
            part_l = jnp.sum(e, axis=0, keepdims=True)
            ovt, l = (part, part_l) if ovt is None else (ovt + part, l + part_l)
        ovt = ovt * (1.0 / l)
        ot = ovt[:, :sub_q] - lam * ovt[:, sub_q:]
        ot = ot * lax.rsqrt(jnp.mean(ot * ot, axis=0, keepdims=True) + NORM_EPS) * g_ref[...] * (1.0 - lambda_init)
        o_ref[rows, :] = ot.T.astype(o_ref.dtype)


def _diff_attn(p0, q_col0, k_col0, v_col0, n_heads, rope_tabs, lam_p, subln_g, lambda_init,
               n_batch, seq, ctx_len, with_lat, tq=2048, sub_q=256):
    ctx_rb0 = (n_batch * seq) // ctx_len
    n_keys = ctx_len + (seq if with_lat else 0)
    hps = 1 if with_lat else n_heads
    width = hps * LANES
    cb = lambda c0: c0 // width
    if with_lat:
        n_q = seq // tq
        q_spec = pl.BlockSpec((tq, width), lambda b, h, i: (b * n_q + i, cb(q_col0) + h))
    else:
        tq = ctx_len
        n_q = 1
        q_spec = pl.BlockSpec((tq, width), lambda b, h, i: (ctx_rb0 + b, cb(q_col0) + h))
    kc_spec = pl.BlockSpec((ctx_len, width), lambda b, h, i: (ctx_rb0 + b, cb(k_col0) + h))
    vc_spec = pl.BlockSpec((ctx_len, width), lambda b, h, i: (ctx_rb0 + b, cb(v_col0) + h))
    small = [pl.BlockSpec((8, LANES), lambda b, h, i: (0, 0)), pl.BlockSpec((LANES, sub_q), lambda b, h, i: (0, 0))]
    small_args = [lam_p, jnp.broadcast_to(subln_g[:, None], (LANES, sub_q))]
    if with_lat:
        kl_spec = pl.BlockSpec((seq, width), lambda b, h, i: (b, cb(k_col0) + h))
        vl_spec = pl.BlockSpec((seq, width), lambda b, h, i: (b, cb(v_col0) + h))
        qt = pl.BlockSpec((tq, LANES), lambda b, h, i: (i, 0))
        kt = pl.BlockSpec((seq, LANES), lambda b, h, i: (0, 0))
        in_specs = [q_spec, kc_spec, kl_spec, vc_spec, vl_spec, qt, qt, qt, kt, kt, kt] + small
        args = [p0, p0, p0, p0, p0, *rope_tabs, *rope_tabs] + small_args
    else:
        in_specs = [q_spec, kc_spec, vc_spec] + small
        args = [p0, p0, p0] + small_args
    block_bytes = 10 * seq * LANES * 4 if with_lat else 8 * ctx_len * width * 4
    return pl.pallas_call(
        functools.partial(_diff_attn_heads_kernel, heads_per_step=hps, with_lat=with_lat, lambda_init=lambda_init,
                          n_ctx=ctx_len, sub_q=sub_q),
        grid=(n_batch, n_heads // hps, n_q),
        in_specs=in_specs,
        out_specs=pl.BlockSpec((tq, width), lambda b, h, i: (b * n_q + i, h)),
        out_shape=jax.ShapeDtypeStruct((n_batch * n_q * tq, n_heads * LANES), BF16),
        scratch_shapes=[pltpu.VMEM((hps, n_keys, LANES), BF16), pltpu.VMEM((hps, LANES, n_keys), BF16)],
        compiler_params=_params(("arbitrary", "arbitrary", "arbitrary"), block_bytes),
        name="diff_attn_lat" if with_lat else "diff_attn_ctx",
    )(*args)


def _rope_tables(seq):
    half = DIFF_HEAD_DIM // 2
    pos = jnp.arange(seq)
    row, col = pos // GRID_W, pos % GRID_W
    inv_freq = ROPE_BASE ** (-jnp.arange(0, half, 2, dtype=F32) / half)
    lane = np.arange(LANES) % DIFF_HEAD_DIM
    use_col = jnp.asarray(lane >= half)
    first = jnp.asarray((lane % half) < half // 2)
    freq = inv_freq[jnp.asarray(lane % (half // 2))]
    p = jnp.where(use_col[None, :], col[:, None], row[:, None]).astype(F32)
    ang = p * freq[None, :]
    cos, sin = jnp.cos(ang), jnp.sin(ang)
    return cos, jnp.where(first[None, :], -sin, 0.0), jnp.where(first[None, :], 0.0, sin)


def _rpb_gather_kernel(rpb_ref, o_ref):
    n = o_ref.shape[1]
    k = rpb_ref.shape[1]
    colid = pl.program_id(0) * n + lax.broadcasted_iota(jnp.int32, (k, n), 1)
    j = lax.broadcasted_iota(jnp.int32, (k, n), 0)
    qc = lax.shift_right_logical(colid, int(math.log2(LANES)))
    half = lax.shift_right_logical(colid & (LANES - 1), int(math.log2(GRID_W)))
    kc = colid & (GRID_W - 1)
    sel = (j == half * LANES + jnp.clip(kc - qc + (WIN_COLS - 1), 0, 2 * WIN_COLS - 2)).astype(F32)
    o_ref[...] = jnp.dot(rpb_ref[...], sel, preferred_element_type=F32, precision=lax.Precision.HIGHEST)


def _rpb_pairs(rpb):
    nh, nr, ncol = rpb.shape
    n_slot = nr + 1
    left = jnp.pad(rpb, ((0, 0), (1, 0), (0, LANES - ncol)))
    right = jnp.pad(rpb, ((0, 0), (0, 1), (0, LANES - ncol)))
    rows = jnp.concatenate([left, right], axis=2).reshape(nh * n_slot, 2 * LANES)
    tn = 1024
    out = pl.pallas_call(
        _rpb_gather_kernel,
        grid=(GRID_W * LANES // tn,),
        in_specs=[pl.BlockSpec((nh * n_slot, 2 * LANES), lambda j: (0, 0))],
        out_specs=pl.BlockSpec((nh * n_slot, tn), lambda j: (0, j)),
        out_shape=jax.ShapeDtypeStruct((nh * n_slot, GRID_W * LANES), F32),
        compiler_params=_params(("arbitrary",), nh * n_slot * tn * 4 + 2 * LANES * tn * 4),
        name="rpb_gather",
    )(rows)
    return out.reshape(nh, n_slot, GRID_W, LANES)


def _na_block_plan(rows):
    kr = min(WIN_ROWS, rows)
    n_blk = rows // NA_Q_ROWS
    kb = np.clip(np.arange(n_blk) * NA_Q_ROWS - kr // 2, 0, rows - NA_K_ROWS)
    layouts, layout_of = [], []
    for blk in range(n_blk):
        dr = np.full((NA_Q_ROWS, NA_K_ROWS), -1, np.int64)
        for i in range(NA_Q_ROWS):
            r = blk * NA_Q_ROWS + i
            rs = int(np.clip(r - kr // 2, 0, rows - kr))
            for j in range(NA_K_ROWS):
                krow = kb[blk] + j
                if rs <= krow < rs + kr:
                    dr[i, j] = krow - r + WIN_ROWS - 1
        key = dr.tobytes()
        if key not in [l.tobytes() for l in layouts]:
            layouts.append(dr)
        layout_of.append([l.tobytes() for l in layouts].index(key))
    return kb, np.stack(layouts), np.asarray(layout_of)


def _na_bias_kernel(tp_ref, o_ref, *, layouts):
    qc = lax.broadcasted_iota(jnp.int32, (GRID_W, LANES), 0)
    lane = lax.broadcasted_iota(jnp.int32, (GRID_W, LANES), 1)
    kc = lane & (GRID_W - 1)
    left = lane < GRID_W
    col_start = jnp.clip(qc - WIN_COLS // 2, 0, GRID_W - WIN_COLS)
    col_ok = (kc >= col_start) & (kc < col_start + WIN_COLS)
    n_lay, n_q, n_k = layouts.shape
    for lay in range(n_lay):
        for i in range(n_q):
            for p in range(-(-n_k // 2)):
                d_l = int(layouts[lay, i, 2 * p])
                d_r = int(layouts[lay, i, 2 * p + 1]) if 2 * p + 1 < n_k else -1
                width = LANES if 2 * p + 1 < n_k else GRID_W
                if d_l < 0 and d_r < 0:
                    tile = jnp.full((GRID_W, LANES), -jnp.inf, F32)
                else:
                    assert d_l < 0 or d_r < 0 or d_r == d_l + 1
                    ok = col_ok
                    if d_l < 0:
                        ok = ok & jnp.logical_not(left)
                    if d_r < 0:
                        ok = ok & left
                    slot = d_r if d_r >= 0 else d_l + 1
                    tile = jnp.where(ok, tp_ref[slot] * LOG2E, -jnp.inf)
                o_ref[lay, i * GRID_W:(i + 1) * GRID_W, p * LANES:p * LANES + width] = tile[:, :width]


def _na_bias(rpb_pairs, layouts):
    nh, n_slot = rpb_pairs.shape[:2]
    n_lay = layouts.shape[0]
    tq, nkw = NA_Q_ROWS * GRID_W, NA_K_ROWS * GRID_W
    return pl.pallas_call(
        functools.partial(_na_bias_kernel, layouts=layouts),
        grid=(nh,),
        in_specs=[pl.BlockSpec((None, n_slot, GRID_W, LANES), lambda h: (h, 0, 0, 0))],
        out_specs=pl.BlockSpec((None, n_lay, tq, nkw), lambda h: (h, 0, 0, 0)),
        out_shape=jax.ShapeDtypeStruct((nh, n_lay, tq, nkw), F32),
        compiler_params=_params(("arbitrary",), n_lay * tq * nkw * 4),
        name="na_bias",
    )(rpb_pairs)


def _na_kernel(q_ref, kl_ref, vl_ref, kc_ref, vc_ref, bias_ref, o_ref, *, plan):
    tq = NA_Q_ROWS * GRID_W
    nkw = NA_K_ROWS * GRID_W
    nt = (((1,), (1,)), ((), ()))
    kc = kc_ref[...]
    vc = vc_ref[...]

    def scores(blk):
        kb, lay = plan[blk]
        q = q_ref[blk * tq:(blk + 1) * tq, :]
        ks = slice(kb * GRID_W, kb * GRID_W + nkw)
        return (lax.dot_general(q, kl_ref[ks, :], nt, preferred_element_type=F32) + bias_ref[lay],
                lax.dot_general(q, kc, nt, preferred_element_type=F32))

    def weighted_values(blk, e_w, e_c, l):
        kb, _ = plan[blk]
        ks = slice(kb * GRID_W, kb * GRID_W + nkw)
        o = (jnp.dot(e_w, vl_ref[ks, :], preferred_element_type=F32) + jnp.dot(e_c, vc, preferred_element_type=F32))
        o_ref[blk * tq:(blk + 1) * tq, :] = (o * (1.0 / l)).astype(o_ref.dtype)

    s_next = scores(0)
    pending = None
    for blk in range(len(plan)):
        s_w, s_c = s_next
        if blk + 1 < len(plan):
            s_next = scores(blk + 1)
        if pending is not None:
            weighted_values(blk - 1, *pending)
        m = jnp.maximum(jnp.max(s_w, axis=-1, keepdims=True), jnp.max(s_c, axis=-1, keepdims=True))
        e_w = jnp.exp2(s_w - m)
        e_c = jnp.exp2(s_c - m)
        l = jnp.sum(e_w, axis=-1, keepdims=True) + jnp.sum(e_c, axis=-1, keepdims=True)
        pending = (e_w.astype(BF16), e_c.astype(BF16), l)
    weighted_values(len(plan) - 1, *pending)


def _na_attn(p1, rpb, n_batch, seq, ctx_len, n_heads):
    rows = seq // GRID_W
    kb, layouts, layout_of = _na_block_plan(rows)
    bias = _na_bias(_rpb_pairs(rpb), layouts)
    n_lay = layouts.shape[0]
    tq = NA_Q_ROWS * GRID_W
    nkw = NA_K_ROWS * GRID_W
    ctx_rb0 = (n_batch * seq) // ctx_len
    plan = tuple((int(k), int(l)) for k, l in zip(kb, layout_of))
    return pl.pallas_call(
        functools.partial(_na_kernel, plan=plan),
        grid=(n_heads, n_batch),
        in_specs=[
            pl.BlockSpec((seq, LANES), lambda h, b: (b, h)),
            pl.BlockSpec((seq, LANES), lambda h, b: (b, n_heads + h)),
            pl.BlockSpec((seq, LANES), lambda h, b: (b, 2 * n_heads + h)),
            pl.BlockSpec((ctx_len, LANES), lambda h, b: (ctx_rb0 + b, n_heads + h)),
            pl.BlockSpec((ctx_len, LANES), lambda h, b: (ctx_rb0 + b, 2 * n_heads + h)),
            pl.BlockSpec((None, n_lay, tq, nkw), lambda h, b: (h, 0, 0, 0)),
        ],
        out_specs=pl.BlockSpec((seq, LANES), lambda h, b: (b, h)),
        out_shape=jax.ShapeDtypeStruct((n_batch * seq, n_heads * LANES), BF16),
        compiler_params=_params(("arbitrary", "arbitrary"), n_lay * tq * nkw * 4 + 4 * seq * LANES * 2),
        name="na_attn",
    )(p1, p1, p1, p1, p1, bias)


def _pack_rows(rows, n_rows=8):
    out = jnp.zeros((n_rows, LANES), F32)
    for r, v in enumerate(rows):
        out = out.at[r, :v.shape[0]].set(v.astype(F32))
    return out


def kernel(x, c, ctx, c_ctx, ada_w, ada_b, norm_mix_g, norm_ffn_g, final_norm_g, ffn_w1, ffn_w3, ffn_w2, ev_w_in, ev_conv_w, ev_conv_b, ev_a_log, ev_dt_bias, ev_d_skip, ev_ssm_norm_g, ev_lam_q1, ev_lam_k1, ev_lam_q2, ev_lam_k2, ev_subln_g, ev_w_out, od_w_in, od_rpb, od_w_out):
    n_batch, seq, d = x.shape
    ctx_len = ctx.shape[1]
    depth = ada_w.shape[0]
    n_lat = n_batch * seq
    n_tok = n_lat + n_batch * ctx_len

    d_ssm = ev_ssm_norm_g.shape[1]
    n_ssm_heads = ev_d_skip.shape[1]
    d_xbc = ev_conv_w.shape[2]
    d_qk = d_v = (ev_w_in.shape[2] - d_ssm - d_xbc - 2 * n_ssm_heads) // 3
    n_diff_heads = d_v // (2 * DIFF_HEAD_DIM)
    n_na_heads = od_rpb.shape[1]

    stream = (x.reshape(n_lat, d), ctx.reshape(n_batch * ctx_len, d))
    cond = jnp.zeros((COND_ROWS, d), F32).at[:n_batch].set(c).at[n_batch].set(c_ctx)
    mod = _ada_mod(cond, ada_w, ada_b).reshape(depth, COND_ROWS, 1, 6 * d)
    rope_tabs = _rope_tables(seq)
    kw = dict(seq=seq, n_batch=n_batch)
    ffn_w = (ffn_w1, ffn_w3, ffn_w2)

    for i in range(depth):
        ctx_out = i < depth - 1
        m_rows = n_tok if ctx_out else n_lat
        j = i // 2
        if i % 2 == 0:
            lambda_init = 0.8 - 0.6 * math.exp(-0.3 * i)
            w_in = ev_w_in[j]
            dt0 = d_ssm + d_xbc
            w_t = jnp.swapaxes(w_in, 0, 1)
            w_parts = [w_t[:dt0].astype(BF16), w_t[dt0 + 2 * n_ssm_heads:].astype(BF16)]
            w_dt = jnp.zeros((2 * LANES, d), F32)
            w_dt = w_dt.at[:n_ssm_heads].set(w_t[dt0:dt0 + n_ssm_heads])
            w_dt = w_dt.at[LANES:LANES + n_ssm_heads].set(w_t[dt0 + n_ssm_heads:dt0 + 2 * n_ssm_heads])
            p0, dt_raw = _proj(stream, norm_mix_g[i], mod, i, w_parts, w_dt.astype(BF16), F32, **kw)
            q0 = d_ssm + d_xbc
            k0, v0 = q0 + d_qk, q0 + 2 * d_qk

            xbc = _ssd_conv(p0, d_ssm, d_xbc, ev_conv_w[j], ev_conv_b[j], seq, ctx_len, n_lat)
            prm = jnp.stack([_pack_rows([ev_a_log[j, r], ev_dt_bias[j, r]]) for r in range(2)])
            y2 = _ssd_scan(xbc, dt_raw, prm, n_batch, seq, ctx_len, d_ssm)
            mix_ssd = _ssd_finish(y2, xbc, p0, jnp.repeat(ev_d_skip[j], SSM_HEAD_DIM), ev_ssm_norm_g[j], d_ssm)

            lam_p = _pack_rows([ev_lam_q1[j], ev_lam_k1[j], ev_lam_q2[j], ev_lam_k2[j]])
            attn_args = (p0, q0, k0, v0, n_diff_heads, rope_tabs, lam_p, ev_subln_g[j], lambda_init,
                         n_batch, seq, ctx_len)
            mix_attn = _diff_attn(*attn_args, with_lat=True)
            if ctx_out:
                mix_attn = (mix_attn, _diff_attn(*attn_args, with_lat=False))
            lhs = [mix_ssd, mix_attn]
            w_out = ev_w_out[j].astype(BF16)
        else:
            d_na = n_na_heads * NA_HEAD_DIM
            q_scale = jnp.where(jnp.arange(3 * d_na) < d_na, NA_HEAD_DIM ** -0.5 * LOG2E, 1.0).astype(F32)
            p1 = _proj(stream, norm_mix_g[i], mod, i, od_w_in[j].astype(BF16), None, BF16, col_scale=q_scale, **kw)
            assert not ctx_out, "context-query neighbourhood layers are not needed at this depth"
            lhs = [_na_attn(p1, od_rpb[j], n_batch, seq, ctx_len, n_na_heads)]
            w_out = od_w_out[j].astype(BF16)
        stream = _out_proj(lhs, w_out, stream, mod, i, 2, m_rows, **kw)
        stream = _ffn(stream, norm_ffn_g[i], mod, i, *ffn_w, final_norm_g, not ctx_out, m_rows, **kw)
    return stream.reshape(n_batch, seq, d)
```

```python
import functools
import math

import jax
import jax.numpy as jnp
import numpy as np
from jax import lax
from jax.experimental import pallas as pl
from jax.experimental.pallas import tpu as pltpu

F32 = jnp.float32
BF16 = jnp.bfloat16

GRID_W = 64
SSM_HEAD_DIM = 64
SSM_GROUPS = 4
SSM_STATE = 128
SSM_CONV = 5
SSM_CHUNK = 128
DIFF_HEAD_DIM = 64
NA_HEAD_DIM = 128
WIN_ROWS = 8
WIN_COLS = 16
ROPE_BASE = 10000.0
NORM_EPS = 1e-6
LOG2E = math.log2(math.e)

LANES = 128
V7X_VMEM_BYTES = 64 * 1024 * 1024
VMEM_HEADROOM_BYTES = 3 * 1024 * 1024
VMEM_MIN_REQUEST_BYTES = 32 * 1024 * 1024
VMEM_TEMPORARIES_BYTES = 16 * 1024 * 1024

NA_Q_ROWS = 4
NA_K_ROWS = NA_Q_ROWS + WIN_ROWS - 1
COND_ROWS = 16


def _vmem_limit(block_bytes):
    want = max(VMEM_MIN_REQUEST_BYTES, 2 * block_bytes + VMEM_TEMPORARIES_BYTES)
    return int(min(V7X_VMEM_BYTES - VMEM_HEADROOM_BYTES, want))


def _params(semantics, block_bytes):
    return pltpu.CompilerParams(dimension_semantics=semantics, vmem_limit_bytes=_vmem_limit(block_bytes))


def _silu(v):
    return v * jax.nn.sigmoid(v)


def _mod_row(i, tm, seq, n_batch):
    return jnp.minimum((i * tm) // seq, n_batch)


def _ada_kernel(cond_ref, w_ref, b_ref, o_ref):
    s = _silu(cond_ref[...]).astype(BF16)
    o_ref[...] = jnp.dot(s, w_ref[...].astype(BF16), preferred_element_type=F32) + b_ref[...]


def _ada_mod(cond, ada_w, ada_b, tn=1024):
    depth, d, n = ada_w.shape
    return pl.pallas_call(
        _ada_kernel,
        grid=(depth, n // tn),
        in_specs=[
            pl.BlockSpec((COND_ROWS, d), lambda l, j: (0, 0)),
            pl.BlockSpec((None, d, tn), lambda l, j: (l, 0, j)),
            pl.BlockSpec((None, 1, tn), lambda l, j: (l, 0, j)),
        ],
        out_specs=pl.BlockSpec((None, COND_ROWS, tn), lambda l, j: (l, 0, j)),
        out_shape=jax.ShapeDtypeStruct((depth, COND_ROWS, n), F32),
        compiler_params=_params(("arbitrary", "arbitrary"), d * tn * 4),
        name="ada_mod",
    )(cond, ada_w, ada_b.reshape(depth, 1, n))


def _norm_mod_pipeline(x_ref, g_ref, mod_ref, shift_idx, scale_idx, n_chunks, consume):
    tm, d = x_ref.shape
    shift = mod_ref[:, shift_idx * d:(shift_idx + 1) * d]
    gain = g_ref[...] * (1.0 + mod_ref[:, scale_idx * d:(scale_idx + 1) * d])
    chunk = tm // n_chunks

    def norm_rows(c):
        x = x_ref[c * chunk:(c + 1) * chunk, :]
        inv = lax.rsqrt(jnp.mean(x * x, axis=-1, keepdims=True) + NORM_EPS)
        return (x * inv * gain + shift).astype(BF16)

    h_next = norm_rows(0)
    for c in range(n_chunks):
        h = h_next
        if c + 1 < n_chunks:
            h_next = norm_rows(c + 1)
        consume(slice(c * chunk, (c + 1) * chunk), h)


def _proj_kernel(*refs, n_x, n_w, tiles_per_part, n_lat_tiles, has_aux, has_scale, first_step_chunks=4):
    rest = list(refs)
    x_refs = [rest.pop(0) for _ in range(n_x)]
    g_ref, mod_ref = rest.pop(0), rest.pop(0)
    w_refs = [rest.pop(0) for _ in range(max(n_w, 1))]
    cs_ref = rest.pop(0) if has_scale else None
    if has_aux:
        waux_ref, o_ref, oaux_ref, h_ref = rest
    else:
        o_ref, h_ref = rest
    dims = (((1,), (0,)), ((), ()))
    j = pl.program_id(1)
    tm, d = x_refs[0].shape

    def emit(h, rows, w_ref):
        acc = lax.dot_general(h, w_ref[...], dims, preferred_element_type=F32)
        if has_scale:
            acc = acc * cs_ref[...]
        o_ref[rows, :] = acc.astype(o_ref.dtype)

    def first_tile(x_ref):
        def consume(rows, h):
            h_ref[rows, :] = h
            emit(h, rows, w_refs[0])
            if has_aux:
                oaux_ref[rows, :] = lax.dot_general(h, waux_ref[...], dims, preferred_element_type=F32)

        _norm_mod_pipeline(x_ref, g_ref, mod_ref, 0, 1, first_step_chunks, consume)

    if n_x == 1:
        pl.when(j == 0)(functools.partial(first_tile, x_refs[0]))
    else:
        is_lat = pl.program_id(0) < n_lat_tiles
        pl.when(jnp.logical_and(j == 0, is_lat))(functools.partial(first_tile, x_refs[0]))
        pl.when(jnp.logical_and(j == 0, jnp.logical_not(is_lat)))(functools.partial(first_tile, x_refs[1]))

    for p in range(max(n_w, 1)):
        lo = max(p * tiles_per_part, 1)
        in_part = (j >= lo) if n_w == 0 else jnp.logical_and(j >= lo, j < (p + 1) * tiles_per_part)
        pl.when(in_part)(lambda p=p: emit(h_ref[...], slice(None), w_refs[p]))


def _proj(x, g, mod, layer, w, w_aux, out_dtype, seq, n_batch, col_scale=None, tm=1024, tn=1024):
    parts = list(w) if isinstance(w, (list, tuple)) else None
    if parts is None:
        d, n = w.shape
        n_w, tiles_per_part = 0, 0
    else:
        d = parts[0].shape[0]
        n_w, tiles_per_part = len(parts), parts[0].shape[1] // tn
        assert all(p.shape == (d, tiles_per_part * tn) for p in parts)
        n = n_w * tiles_per_part * tn
    m = sum(a.shape[0] for a in x) if isinstance(x, tuple) else x.shape[0]
    n_lat_tiles = (n_batch * seq) // tm
    has_aux = w_aux is not None
    has_scale = col_scale is not None
    mod_spec = pl.BlockSpec((None, None, 1, mod.shape[-1]),
                            lambda i, j: (layer, _mod_row(i, tm, seq, n_batch), 0, 0))
    in_specs, args = _row_operand(x, tm, n_lat_tiles, d, lambda j: 0, single_ctx_buffer=True)
    n_x = len(args)
    in_specs += [pl.BlockSpec((1, d), lambda i, j: (0, 0)), mod_spec]
    args += [g.reshape(1, d), mod]
    if parts is None:
        in_specs.append(pl.BlockSpec((d, tn), lambda i, j: (0, j)))
        args.append(w)
    else:
        for p, part in enumerate(parts):
            in_specs.append(pl.BlockSpec(
                (d, tn), lambda i, j, p=p: (0, jnp.clip(j - p * tiles_per_part, 0, tiles_per_part - 1))))
            args.append(part)
    out_specs = [pl.BlockSpec((tm, tn), lambda i, j: (i, j))]
    out_shape = [jax.ShapeDtypeStruct((m, n), out_dtype)]
    if has_scale:
        in_specs.append(pl.BlockSpec((1, tn), lambda i, j: (0, j)))
        args.append(col_scale.reshape(1, n))
    if has_aux:
        na = w_aux.shape[1]
        in_specs.append(pl.BlockSpec(w_aux.shape, lambda i, j: (0, 0)))
        out_specs.append(pl.BlockSpec((tm, na), lambda i, j: (i, 0)))
        out_shape.append(jax.ShapeDtypeStruct((m, na), F32))
        args.append(w_aux)
    block_bytes = n_x * tm * d * 4 + max(n_w, 1) * d * tn * 2 + tm * tn * 4 + tm * d
    outs = pl.pallas_call(
        functools.partial(_proj_kernel, n_x=n_x, n_w=n_w, tiles_per_part=tiles_per_part, n_lat_tiles=n_lat_tiles,
                          has_aux=has_aux, has_scale=has_scale),
        grid=(m // tm, n // tn),
        in_specs=in_specs,
        out_specs=out_specs,
        out_shape=out_shape,
        scratch_shapes=[pltpu.VMEM((tm, d), BF16)],
        compiler_params=_params(("arbitrary", "arbitrary"), block_bytes),
        name="proj",
    )(*args)
    return outs if has_aux else outs[0]


def _row_operand(a, tm, n_lat_tiles, width, col_of_j, single_ctx_buffer=False):
    if not isinstance(a, tuple):
        return [pl.BlockSpec((tm, width), lambda i, j: (i, col_of_j(j)))], [a]
    lat, ctx = a
    assert ctx.shape[0] == tm and lat.shape[0] == n_lat_tiles * tm

    def lat_index(i, j):
        return jnp.minimum(i, n_lat_tiles - 1), jnp.where(i < n_lat_tiles, col_of_j(j), 0)

    def ctx_index(i, j):
        return 0, jnp.where(i < n_lat_tiles, 0, col_of_j(j))

    ctx_mode = dict(pipeline_mode=pl.Buffered(1)) if single_ctx_buffer else {}
    return [pl.BlockSpec((tm, width), lat_index), pl.BlockSpec((tm, width), ctx_index, **ctx_mode)], [lat, ctx]


def _pick_rows(refs, n_lat_tiles):
    if len(refs) == 1:
        return refs[0][...]
    return jnp.where(pl.program_id(0) < n_lat_tiles, refs[0][...], refs[1][...])


def _out_proj_kernel(*refs, arity, n_lat_tiles):
    refs = list(refs)
    n_lhs = len(arity) - 1
    groups = [[refs.pop(0) for _ in range(n)] for n in arity[:-1]]
    ws = [refs.pop(0) for _ in range(n_lhs)]
    res = [refs.pop(0) for _ in range(arity[-1])]
    gate_ref, o_ref = refs
    acc = None
    for grp, w_ref in zip(groups, ws):
        part = jnp.dot(_pick_rows(grp, n_lat_tiles), w_ref[...], preferred_element_type=F32)
        acc = part if acc is None else acc + part
    o_ref[...] = _pick_rows(res, n_lat_tiles) + gate_ref[...] * acc


def _out_proj(lhs_list, w, res, mod, layer, gate_idx, m_rows, seq, n_batch, tm=1024, tn=1024):
    d = w.shape[1]
    n_lat_tiles = (n_batch * seq) // tm
    in_specs, args, arity, widths = [], [], [], []
    for a in lhs_list:
        kk = (a[0] if isinstance(a, tuple) else a).shape[1]
        sp, ar = _row_operand(a, tm, n_lat_tiles, kk, lambda j: 0)
        in_specs += sp
        args += ar
        arity.append(len(ar))
        widths.append(kk)
    row0 = 0
    for kk in widths:
        in_specs.append(pl.BlockSpec((kk, tn), lambda i, j, rb=row0 // kk: (rb, j)))
        args.append(w)
        row0 += kk
    sp, ar = _row_operand(res, tm, n_lat_tiles, tn, lambda j: j)
    in_specs += sp
    args += ar
    arity.append(len(ar))
    in_specs.append(pl.BlockSpec((None, None, 1, tn),
                                 lambda i, j: (layer, _mod_row(i, tm, seq, n_batch), 0, gate_idx * (d // tn) + j)))
    args.append(mod)
    k_total = sum(widths)
    block_bytes = 2 * tm * k_total * 2 + k_total * tn * 2 + 3 * tm * tn * 4
    return pl.pallas_call(
        functools.partial(_out_proj_kernel, arity=tuple(arity), n_lat_tiles=n_lat_tiles),
        grid=(m_rows // tm, d // tn),
        in_specs=in_specs,
        out_specs=pl.BlockSpec((tm, tn), lambda i, j: (i, j)),
        out_shape=jax.ShapeDtypeStruct((m_rows, d), F32),
        compiler_params=_params(("arbitrary", "arbitrary"), block_bytes),
        name="out_proj",
    )(*args)


def _ffn_kernel(x_ref, g_ref, mod_ref, w1_ref, w3_ref, w2_ref, fg_ref, o_ref, h_ref, *, final_norm, rows,
                first_step_chunks):
    j = pl.program_id(1)
    tm, d = x_ref.shape

    def swiglu(h, w1, w3, w2):
        a = jnp.dot(h, w1, preferred_element_type=F32)
        b = jnp.dot(h, w3, preferred_element_type=F32)
        return jnp.dot((_silu(a) * b).astype(BF16), w2, preferred_element_type=F32)

    @pl.when(j == 0)
    def _():
        w1, w3, w2 = (w_ref[...].astype(BF16) for w_ref in (w1_ref, w3_ref, w2_ref))

        def consume(rows, h):
            h_ref[rows, :] = h
            o_ref[rows, :] = swiglu(h, w1, w3, w2)

        _norm_mod_pipeline(x_ref, g_ref, mod_ref, 3, 4, first_step_chunks, consume)

    @pl.when(j > 0)
    def _():
        o_ref[...] += swiglu(h_ref[...], w1_ref[...].astype(BF16), w3_ref[...].astype(BF16),
                             w2_ref[...].astype(BF16))

    @pl.when(j == pl.num_programs(1) - 1)
    def _():
        gate = mod_ref[:, 5 * d:6 * d]
        fg = fg_ref[...]

        def body(r, carry):
            sl = pl.ds(pl.multiple_of(r * rows, rows), rows)
            y = x_ref[sl, :] + gate * o_ref[sl, :]
            if final_norm:
                y = y * lax.rsqrt(jnp.mean(y * y, axis=-1, keepdims=True) + NORM_EPS) * fg
            o_ref[sl, :] = y
            return carry

        lax.fori_loop(0, x_ref.shape[0] // rows, body, 0)


def _ffn(x, g, mod, layer, w1, w3, w2, final_g, final_norm, m_rows, seq, n_batch, tm=1024, tf=256):
    d = x.shape[1]
    ff = w1.shape[2]
    block_bytes = 2 * tm * d * 4 + d * tf * sum(w.dtype.itemsize for w in (w1, w3, w2)) + tm * d
    return pl.pallas_call(
        functools.partial(_ffn_kernel, final_norm=final_norm, rows=128, first_step_chunks=4),
        grid=(m_rows // tm, ff // tf),
        in_specs=[
            pl.BlockSpec((tm, d), lambda i, j: (i, 0)),
            pl.BlockSpec((1, d), lambda i, j: (0, 0)),
            pl.BlockSpec((None, None, 1, mod.shape[-1]),
                         lambda i, j: (layer, _mod_row(i, tm, seq, n_batch), 0, 0)),
            pl.BlockSpec((None, d, tf), lambda i, j: (layer, 0, j)),
            pl.BlockSpec((None, d, tf), lambda i, j: (layer, 0, j)),
            pl.BlockSpec((None, tf, d), lambda i, j: (layer, j, 0)),
            pl.BlockSpec((1, d), lambda i, j: (0, 0)),
        ],
        out_specs=pl.BlockSpec((tm, d), lambda i, j: (i, 0)),
        out_shape=jax.ShapeDtypeStruct((m_rows, d), F32),
        scratch_shapes=[pltpu.VMEM((tm, d), BF16)],
        compiler_params=_params(("arbitrary", "arbitrary"), block_bytes),
        name="ffn",
    )(x, g.reshape(1, d), mod, w1, w3, w2, final_g.reshape(1, d))


def _conv_kernel(prev_ref, cur_ref, next_ref, w_ref, b_ref, o_ref, ext_ref, *, tm, halo, seq, ctx_len, n_lat_tiles):
    i = pl.program_id(0)
    half = SSM_CONV // 2
    ext_ref[halo:halo + tm, :] = cur_ref[...]

    def taps(mask_of):
        acc = jnp.zeros(cur_ref.shape, F32) + b_ref[...]
        for t in range(SSM_CONV):
            tap = ext_ref[halo - half + t:halo - half + t + tm, :]
            acc = acc + mask_of(t, tap) * w_ref[t:t + 1, :]
        o_ref[...] = _silu(acc)

    @pl.when(i < n_lat_tiles)
    def _():
        prev_in_seg = ((i * tm) & (seq - 1)) != 0
        next_in_seg = (((i + 1) * tm) & (seq - 1)) != 0
        ext_ref[0:halo, :] = jnp.where(prev_in_seg, prev_ref[...], 0.0)
        ext_ref[halo + tm:, :] = jnp.where(next_in_seg, next_ref[...], 0.0)
        taps(lambda t, tap: tap)

    @pl.when(i >= n_lat_tiles)
    def _():
        ext_ref[0:halo, :] = prev_ref[...]
        ext_ref[halo + tm:, :] = next_ref[...]
        pos = (i * tm + lax.broadcasted_iota(jnp.int32, (tm, 1), 0)) & (ctx_len - 1)

        def masked(t, tap):
            src = pos + (t - half)
            return jnp.where((src >= 0) & (src < ctx_len), tap, 0.0)

        taps(masked)


def _ssd_conv(p0, col0, width, conv_w, conv_b, seq, ctx_len, n_lat_rows, tm=1024, tc=1024, halo=8):
    m = p0.shape[0]
    assert seq % tm == 0 and tm % ctx_len == 0 and (seq & (seq - 1)) == 0 and (ctx_len & (ctx_len - 1)) == 0
    cb0 = col0 // tc
    hb = tm // halo
    last_hb = m // halo - 1
    wpad = jnp.zeros((8, width), F32).at[:SSM_CONV].set(conv_w)
    return pl.pallas_call(
        functools.partial(_conv_kernel, tm=tm, halo=halo, seq=seq, ctx_len=ctx_len, n_lat_tiles=n_lat_rows // tm),
        grid=(m // tm, width // tc),
        in_specs=[
            pl.BlockSpec((halo, tc), lambda i, j: (jnp.maximum(i * hb - 1, 0), cb0 + j)),
            pl.BlockSpec((tm, tc), lambda i, j: (i, cb0 + j)),
            pl.BlockSpec((halo, tc), lambda i, j: (jnp.minimum((i + 1) * hb, last_hb), cb0 + j)),
            pl.BlockSpec((8, tc), lambda i, j: (0, j)),
            pl.BlockSpec((1, tc), lambda i, j: (0, j)),
        ],
        out_specs=pl.BlockSpec((tm, tc), lambda i, j: (i, j)),
        out_shape=jax.ShapeDtypeStruct((m, width), F32),
        scratch_shapes=[pltpu.VMEM((tm + 2 * halo, tc), F32)],
        compiler_params=_params(("arbitrary", "arbitrary"), 3 * tm * tc * 4),
        name="ssd_conv",
    )(p0, p0, p0, wpad, conv_b.reshape(1, width))


def _ssd_scan_kernel(xs_f, bm_f, cm_f, dt_f, xs_b, bm_b, cm_b, dt_b, prm_ref, yf_ref, yb_ref, state_ref, *, n_heads):
    @pl.when(pl.program_id(1) == 0)
    def _():
        state_ref[...] = jnp.zeros_like(state_ref)

    fwd = _ssd_chunk(xs_f, bm_f, cm_f, dt_f, prm_ref.at[0], yf_ref, state_ref.at[0], backward=False, n_heads=n_heads)
    bwd = _ssd_chunk(xs_b, bm_b, cm_b, dt_b, prm_ref.at[1], yb_ref, state_ref.at[1], backward=True, n_heads=n_heads)
    for g in range(SSM_GROUPS):
        fwd(g)
        bwd(g)


def _ssd_chunk(xs_ref, bm_ref, cm_ref, dt_ref, prm_ref, y_ref, state_ref, *, backward, n_heads):
    hp = SSM_HEAD_DIM
    rep = n_heads // SSM_GROUPS
    lc = SSM_CHUNK
    assert 2 * hp == LANES and rep % 2 == 0 and SSM_STATE == LANES and lc == LANES

    dt_in = dt_ref[...] + prm_ref[1:2, :]
    dt = jnp.maximum(dt_in, 0.0) + jnp.log1p(jnp.exp(-jnp.abs(dt_in)))
    dta = dt * (-jnp.exp(prm_ref[0:1, :]) * LOG2E)
    row = lax.broadcasted_iota(jnp.int32, (lc, lc), 0)
    col = lax.broadcasted_iota(jnp.int32, (lc, lc), 1)
    causal = (row <= col) if backward else (row >= col)
    a_cum = jnp.dot(causal.astype(F32), dta, preferred_element_type=F32, precision=lax.Precision.HIGHEST)
    a_tot = a_cum[0:1, :] if backward else a_cum[lc - 1:lc, :]
    w_end = dt * jnp.exp2(a_tot - a_cum)
    a_cum_t, dt_t, w_end_t = a_cum.T, dt.T, w_end.T
    lo = col < hp

    def do_group(g):
        gs = slice(g * SSM_STATE, (g + 1) * SSM_STATE)
        b_f = bm_ref[:, gs]
        c_g = cm_ref[:, gs].astype(BF16)
        cb = lax.dot_general(c_g, b_f.astype(BF16), (((1,), (1,)), ((), ())), preferred_element_type=F32)
        b_t = b_f.T
        cols_g = slice(g * rep * hp, (g + 1) * rep * hp)
        y_off = jnp.dot(c_g, state_ref[:, cols_g].astype(BF16), preferred_element_type=F32)
        for pr in range(rep // 2):
            cols = slice(g * rep * hp + pr * LANES, g * rep * hp + (pr + 1) * LANES)
            xs2 = xs_ref[:, cols].astype(BF16)
            y_d, s_n, e_a = [], [], []
            for h in (g * rep + 2 * pr, g * rep + 2 * pr + 1):
                a_col = jnp.broadcast_to(a_cum[:, h:h + 1], (lc, lc))
                decay = jnp.exp2(jnp.where(causal, a_col - a_cum_t[h:h + 1, :], -jnp.inf))
                m = (cb * decay * dt_t[h:h + 1, :]).astype(BF16)
                y_d.append(jnp.dot(m, xs2, preferred_element_type=F32))
                s_n.append(jnp.dot((b_t * w_end_t[h:h + 1, :]).astype(BF16), xs2, preferred_element_type=F32))
                e_a.append(jnp.exp2(a_col))
            h0 = g * rep + 2 * pr
            y_ref[:, cols] = (jnp.where(lo, y_d[0], y_d[1])
                              + y_off[:, pr * LANES:(pr + 1) * LANES] * jnp.where(lo, e_a[0], e_a[1]))
            chunk_decay = jnp.where(lo[0:1, :], jnp.exp2(a_tot[:, h0:h0 + 1]), jnp.exp2(a_tot[:, h0 + 1:h0 + 2]))
            state_ref[:, cols] = state_ref[:, cols] * chunk_decay + jnp.where(lo, s_n[0], s_n[1])

    return do_group


def _ssd_scan(xbc, dt_raw, prm, n_batch, seq, ctx_len, d_ssm):
    m = xbc.shape[0]
    lc = SSM_CHUNK
    n_heads = d_ssm // SSM_HEAD_DIM
    nc_ctx, nc_lat = ctx_len // lc, seq // lc
    ctx_blk0 = n_batch * nc_lat
    gn = SSM_GROUPS * SSM_STATE

    def row_blk(b, s, backward):
        ctx_c = nc_ctx - 1 - s if backward else s
        lat_c = nc_lat - 1 - (s - nc_ctx) if backward else s - nc_ctx
        return jnp.where(s < nc_ctx, ctx_blk0 + b * nc_ctx + ctx_c, b * nc_lat + lat_c)

    def chunk_specs(backward):
        r = int(backward)
        return [
            pl.BlockSpec((lc, d_ssm), lambda b, s: (row_blk(b, s, backward), 0)),
            pl.BlockSpec((lc, gn), lambda b, s: (row_blk(b, s, backward), d_ssm // gn)),
            pl.BlockSpec((lc, gn), lambda b, s: (row_blk(b, s, backward), d_ssm // gn + 1)),
            pl.BlockSpec((lc, LANES), lambda b, s: (row_blk(b, s, backward), r)),
        ]

    return pl.pallas_call(
        functools.partial(_ssd_scan_kernel, n_heads=n_heads),
        grid=(n_batch, nc_ctx + nc_lat),
        in_specs=chunk_specs(False) + chunk_specs(True) + [pl.BlockSpec((2, 8, LANES), lambda b, s: (0, 0, 0))],
        out_specs=[pl.BlockSpec((lc, d_ssm), lambda b, s: (row_blk(b, s, False), 0)),
                   pl.BlockSpec((lc, d_ssm), lambda b, s: (row_blk(b, s, True), 0))],
        out_shape=[jax.ShapeDtypeStruct((m, d_ssm), F32), jax.ShapeDtypeStruct((m, d_ssm), F32)],
        scratch_shapes=[pltpu.VMEM((2, SSM_STATE, d_ssm), F32)],
        compiler_params=_params(("arbitrary", "arbitrary"), 8 * lc * d_ssm * 4),
        name="ssd_scan",
    )(xbc, xbc, xbc, dt_raw, xbc, xbc, xbc, dt_raw, prm)


def _ssd_finish_kernel(yf_ref, yb_ref, xs_ref, z_ref, dskip_ref, g_ref, o_ref, *, group_width):
    y = (yf_ref[...] + yb_ref[...] + dskip_ref[...] * xs_ref[...]) * _silu(z_ref[...])
    for g in range(y.shape[1] // group_width):
        sl = slice(g * group_width, (g + 1) * group_width)
        v = y[:, sl]
        vn = v * lax.rsqrt(jnp.mean(v * v, axis=-1, keepdims=True) + NORM_EPS) * g_ref[:, sl]
        o_ref[:, sl] = vn.astype(o_ref.dtype)


def _ssd_finish(y2, xbc, p0, d_skip_lanes, norm_g, d_ssm, tm=512):
    m = xbc.shape[0]
    return pl.pallas_call(
        functools.partial(_ssd_finish_kernel, group_width=d_ssm // SSM_GROUPS),
        grid=(m // tm,),
        in_specs=[
            pl.BlockSpec((tm, d_ssm), lambda i: (i, 0)),
            pl.BlockSpec((tm, d_ssm), lambda i: (i, 0)),
            pl.BlockSpec((tm, d_ssm), lambda i: (i, 0)),
            pl.BlockSpec((tm, d_ssm), lambda i: (i, 0)),
            pl.BlockSpec((1, d_ssm), lambda i: (0, 0)),
            pl.BlockSpec((1, d_ssm), lambda i: (0, 0)),
        ],
        out_specs=pl.BlockSpec((tm, d_ssm), lambda i: (i, 0)),
        out_shape=jax.ShapeDtypeStruct((m, d_ssm), BF16),
        compiler_params=_params(("arbitrary",), 5 * tm * d_ssm * 4),
        name="ssd_finish",
    )(*y2, xbc, p0, d_skip_lanes.reshape(1, d_ssm), norm_g.reshape(1, d_ssm))


def _rope(x, cos, sin_up, sin_dn):
    quarter = DIFF_HEAD_DIM // 4
    return (x * cos + pltpu.roll(x, LANES - quarter, axis=1) * sin_up + pltpu.roll(x, quarter, axis=1) * sin_dn)


def _diff_attn_heads_kernel(*refs, heads_per_step, with_lat, **kw):
    n_head_refs = 5 if with_lat else 3
    *ins, o_ref, k_scr, vt_scr = refs
    for h in range(heads_per_step):
        cs = slice(h * LANES, (h + 1) * LANES)
        head_ins = [r.at[:, cs] for r in ins[:n_head_refs]] + list(ins[n_head_refs:])
        _diff_attn_kernel(*head_ins, o_ref.at[:, cs], k_scr.at[h], vt_scr.at[h], with_lat=with_lat, **kw)


def _diff_attn_kernel(*refs, with_lat, lambda_init, n_ctx, sub_q, key_chunk=768):
    if with_lat:
        (q_ref, kc_ref, kl_ref, vc_ref, vl_ref, qcos_ref, qsu_ref, qsd_ref, kcos_ref, ksu_ref, ksd_ref,
         lam_ref, g_ref, o_ref, k_scr, vt_scr) = refs
    else:
        q_ref, kc_ref, vc_ref, lam_ref, g_ref, o_ref, k_scr, vt_scr = refs

    @pl.when(pl.program_id(2) == 0)
    def _():
        k_scr[0:n_ctx, :] = kc_ref[...].astype(BF16)
        vt_scr[:, 0:n_ctx] = vc_ref[...].T.astype(BF16)
        if with_lat:
            k_scr[n_ctx:, :] = _rope(kl_ref[...], kcos_ref[...], ksu_ref[...], ksd_ref[...]).astype(BF16)
            vt_scr[:, n_ctx:] = vl_ref[...].T.astype(BF16)

    lam_p = lam_ref[...]
    lam = (jnp.exp(jnp.sum(lam_p[0:1, :] * lam_p[1:2, :], axis=-1, keepdims=True))
           - jnp.exp(jnp.sum(lam_p[2:3, :] * lam_p[3:4, :], axis=-1, keepdims=True)) + lambda_init)
    first = lax.broadcasted_iota(jnp.int32, (LANES, sub_q), 0) < DIFF_HEAD_DIM
    n_sub = q_ref.shape[0] // sub_q

    def scores(t):
        rows = slice(t * sub_q, (t + 1) * sub_q)
        q = q_ref[rows, :]
        if with_lat:
            q = _rope(q, qcos_ref[rows, :], qsu_ref[rows, :], qsd_ref[rows, :])
        qt = (q * (DIFF_HEAD_DIM ** -0.5 * LOG2E)).T
        q2t = jnp.concatenate([jnp.where(first, qt, 0.0), jnp.where(first, 0.0, qt)], axis=1).astype(BF16)
        blocks = [jnp.dot(k_scr[k0:k1, :], q2t, preferred_element_type=F32) for k0, k1 in key_chunks]
        return blocks, [jnp.max(b, axis=0, keepdims=True) for b in blocks]

    n_keys = k_scr.shape[0]
    key_chunks = [(k0, min(k0 + key_chunk, n_keys)) for k0 in range(0, n_keys, key_chunk)]
    s_next = scores(0)
    for t in range(n_sub):
        rows = slice(t * sub_q, (t + 1) * sub_q)
        s_blocks, maxes = s_next
        if t + 1 < n_sub:
            s_next = scores(t + 1)
        m = functools.reduce(jnp.maximum, maxes)
        ovt, l = None, None
        for (k0, k1), s in zip(key_chunks, s_blocks):
            e = jnp.exp2(s - m)
            part = jnp.dot(vt_scr[:, k0:k1], e.astype(BF16), preferred_element_type=F32)
            part_l = jnp.sum(e, axis=0, keepdims=True)
            ovt, l = (part, part_l) if ovt is None else (ovt + part, l + part_l)
        ovt = ovt * (1.0 / l)
        ot = ovt[:, :sub_q] - lam * ovt[:, sub_q:]
        ot = ot * lax.rsqrt(jnp.mean(ot * ot, axis=0, keepdims=True) + NORM_EPS) * g_ref[...] * (1.0 - lambda_init)
        o_ref[rows, :] = ot.T.astype(o_ref.dtype)


def _diff_attn(p0, q_col0, k_col0, v_col0, n_heads, rope_tabs, lam_p, subln_g, lambda_init,
               n_batch, seq, ctx_len, with_lat, tq=2048, sub_q=256):
    ctx_rb0 = (n_batch * seq) // ctx_len
    n_keys = ctx_len + (seq if with_lat else 0)
    hps = 1 if with_lat else n_heads
    width = hps * LANES
    cb = lambda c0: c0 // width
    if with_lat:
        n_q = seq // tq
        q_spec = pl.BlockSpec((tq, width), lambda b, h, i: (b * n_q + i, cb(q_col0) + h))
    else:
        tq = ctx_len
        sub_q = min(sub_q, tq)
        n_q = 1
        q_spec = pl.BlockSpec((tq, width), lambda b, h, i: (ctx_rb0 + b, cb(q_col0) + h))
    kc_spec = pl.BlockSpec((ctx_len, width), lambda b, h, i: (ctx_rb0 + b, cb(k_col0) + h))
    vc_spec = pl.BlockSpec((ctx_len, width), lambda b, h, i: (ctx_rb0 + b, cb(v_col0) + h))
    small = [pl.BlockSpec((8, LANES), lambda b, h, i: (0, 0)), pl.BlockSpec((LANES, sub_q), lambda b, h, i: (0, 0))]
    small_args = [lam_p, jnp.broadcast_to(subln_g[:, None], (LANES, sub_q))]
    if with_lat:
        kl_spec = pl.BlockSpec((seq, width), lambda b, h, i: (b, cb(k_col0) + h))
        vl_spec = pl.BlockSpec((seq, width), lambda b, h, i: (b, cb(v_col0) + h))
        qt = pl.BlockSpec((tq, LANES), lambda b, h, i: (i, 0))
        kt = pl.BlockSpec((seq, LANES), lambda b, h, i: (0, 0))
        in_specs = [q_spec, kc_spec, kl_spec, vc_spec, vl_spec, qt, qt, qt, kt, kt, kt] + small
        args = [p0, p0, p0, p0, p0, *rope_tabs, *rope_tabs] + small_args
    else:
        in_specs = [q_spec, kc_spec, vc_spec] + small
        args = [p0, p0, p0] + small_args
    block_bytes = 10 * seq * LANES * 4 if with_lat else 8 * ctx_len * width * 4
    return pl.pallas_call(
        functools.partial(_diff_attn_heads_kernel, heads_per_step=hps, with_lat=with_lat, lambda_init=lambda_init,
                          n_ctx=ctx_len, sub_q=sub_q),
        grid=(n_batch, n_heads // hps, n_q),
        in_specs=in_specs,
        out_specs=pl.BlockSpec((tq, width), lambda b, h, i: (b * n_q + i, h)),
        out_shape=jax.ShapeDtypeStruct((n_batch * n_q * tq, n_heads * LANES), BF16),
        scratch_shapes=[pltpu.VMEM((hps, n_keys, LANES), BF16), pltpu.VMEM((hps, LANES, n_keys), BF16)],
        compiler_params=_params(("arbitrary", "arbitrary", "arbitrary"), block_bytes),
        name="diff_attn_lat" if with_lat else "diff_attn_ctx",
    )(*args)


def _rope_tables(seq):
    half = DIFF_HEAD_DIM // 2
    pos = jnp.arange(seq)
    row, col = pos // GRID_W, pos % GRID_W
    inv_freq = ROPE_BASE ** (-jnp.arange(0, half, 2, dtype=F32) / half)
    lane = np.arange(LANES) % DIFF_HEAD_DIM
    use_col = jnp.asarray(lane >= half)
    first = jnp.asarray((lane % half) < half // 2)
    freq = inv_freq[jnp.asarray(lane % (half // 2))]
    p = jnp.where(use_col[None, :], col[:, None], row[:, None]).astype(F32)
    ang = p * freq[None, :]
    cos, sin = jnp.cos(ang), jnp.sin(ang)
    return cos, jnp.where(first[None, :], -sin, 0.0), jnp.where(first[None, :], 0.0, sin)


def _rpb_gather_kernel(rpb_ref, o_ref):
    n = o_ref.shape[1]
    k = rpb_ref.shape[1]
    colid = pl.program_id(0) * n + lax.broadcasted_iota(jnp.int32, (k, n), 1)
    j = lax.broadcasted_iota(jnp.int32, (k, n), 0)
    qc = lax.shift_right_logical(colid, int(math.log2(LANES)))
    half = lax.shift_right_logical(colid & (LANES - 1), int(math.log2(GRID_W)))
    kc = colid & (GRID_W - 1)
    sel = (j == half * LANES + jnp.clip(kc - qc + (WIN_COLS - 1), 0, 2 * WIN_COLS - 2)).astype(F32)
    o_ref[...] = jnp.dot(rpb_ref[...], sel, preferred_element_type=F32, precision=lax.Precision.HIGHEST)


def _rpb_pairs(rpb):
    nh, nr, ncol = rpb.shape
    n_slot = nr + 1
    left = jnp.pad(rpb, ((0, 0), (1, 0), (0, LANES - ncol)))
    right = jnp.pad(rpb, ((0, 0), (0, 1), (0, LANES - ncol)))
    rows = jnp.concatenate([left, right], axis=2).reshape(nh * n_slot, 2 * LANES)
    tn = 1024
    out = pl.pallas_call(
        _rpb_gather_kernel,
        grid=(GRID_W * LANES // tn,),
        in_specs=[pl.BlockSpec((nh * n_slot, 2 * LANES), lambda j: (0, 0))],
        out_specs=pl.BlockSpec((nh * n_slot, tn), lambda j: (0, j)),
        out_shape=jax.ShapeDtypeStruct((nh * n_slot, GRID_W * LANES), F32),
        compiler_params=_params(("arbitrary",), nh * n_slot * tn * 4 + 2 * LANES * tn * 4),
        name="rpb_gather",
    )(rows)
    return out.reshape(nh, n_slot, GRID_W, LANES)


def _na_block_plan(rows):
    kr = min(WIN_ROWS, rows)
    n_blk = rows // NA_Q_ROWS
    kb = np.clip(np.arange(n_blk) * NA_Q_ROWS - kr // 2, 0, rows - NA_K_ROWS)
    layouts, layout_of = [], []
    for blk in range(n_blk):
        dr = np.full((NA_Q_ROWS, NA_K_ROWS), -1, np.int64)
        for i in range(NA_Q_ROWS):
            r = blk * NA_Q_ROWS + i
            rs = int(np.clip(r - kr // 2, 0, rows - kr))
            for j in range(NA_K_ROWS):
                krow = kb[blk] + j
                if rs <= krow < rs + kr:
                    dr[i, j] = krow - r + WIN_ROWS - 1
        key = dr.tobytes()
        if key not in [l.tobytes() for l in layouts]:
            layouts.append(dr)
        layout_of.append([l.tobytes() for l in layouts].index(key))
    return kb, np.stack(layouts), np.asarray(layout_of)


def _na_bias_kernel(tp_ref, o_ref, *, layouts):
    qc = lax.broadcasted_iota(jnp.int32, (GRID_W, LANES), 0)
    lane = lax.broadcasted_iota(jnp.int32, (GRID_W, LANES), 1)
    kc = lane & (GRID_W - 1)
    left = lane < GRID_W
    col_start = jnp.clip(qc - WIN_COLS // 2, 0, GRID_W - WIN_COLS)
    col_ok = (kc >= col_start) & (kc < col_start + WIN_COLS)
    n_lay, n_q, n_k = layouts.shape
    for lay in range(n_lay):
        for i in range(n_q):
            for p in range(-(-n_k // 2)):
                d_l = int(layouts[lay, i, 2 * p])
                d_r = int(layouts[lay, i, 2 * p + 1]) if 2 * p + 1 < n_k else -1
                width = LANES if 2 * p + 1 < n_k else GRID_W
                if d_l < 0 and d_r < 0:
                    tile = jnp.full((GRID_W, LANES), -jnp.inf, F32)
                else:
                    assert d_l < 0 or d_r < 0 or d_r == d_l + 1
                    ok = col_ok
                    if d_l < 0:
                        ok = ok & jnp.logical_not(left)
                    if d_r < 0:
                        ok = ok & left
                    slot = d_r if d_r >= 0 else d_l + 1
                    tile = jnp.where(ok, tp_ref[slot] * LOG2E, -jnp.inf)
                o_ref[lay, i * GRID_W:(i + 1) * GRID_W, p * LANES:p * LANES + width] = tile[:, :width]


def _na_bias(rpb_pairs, layouts):
    nh, n_slot = rpb_pairs.shape[:2]
    n_lay = layouts.shape[0]
    tq, nkw = NA_Q_ROWS * GRID_W, NA_K_ROWS * GRID_W
    return pl.pallas_call(
        functools.partial(_na_bias_kernel, layouts=layouts),
        grid=(nh,),
        in_specs=[pl.BlockSpec((None, n_slot, GRID_W, LANES), lambda h: (h, 0, 0, 0))],
        out_specs=pl.BlockSpec((None, n_lay, tq, nkw), lambda h: (h, 0, 0, 0)),
        out_shape=jax.ShapeDtypeStruct((nh, n_lay, tq, nkw), F32),
        compiler_params=_params(("arbitrary",), n_lay * tq * nkw * 4),
        name="na_bias",
    )(rpb_pairs)


def _na_kernel(q_ref, kl_ref, vl_ref, kc_ref, vc_ref, bias_ref, o_ref, *, plan):
    tq = NA_Q_ROWS * GRID_W
    nkw = NA_K_ROWS * GRID_W
    nt = (((1,), (1,)), ((), ()))
    kc = kc_ref[...]
    vc = vc_ref[...]

    def scores(blk):
        kb, lay = plan[blk]
        q = q_ref[blk * tq:(blk + 1) * tq, :]
        ks = slice(kb * GRID_W, kb * GRID_W + nkw)
        return (lax.dot_general(q, kl_ref[ks, :], nt, preferred_element_type=F32) + bias_ref[lay],
                lax.dot_general(q, kc, nt, preferred_element_type=F32))

    def weighted_values(blk, e_w, e_c, l):
        kb, _ = plan[blk]
        ks = slice(kb * GRID_W, kb * GRID_W + nkw)
        o = (jnp.dot(e_w, vl_ref[ks, :], preferred_element_type=F32) + jnp.dot(e_c, vc, preferred_element_type=F32))
        o_ref[blk * tq:(blk + 1) * tq, :] = (o * (1.0 / l)).astype(o_ref.dtype)

    s_next = scores(0)
    pending = None
    for blk in range(len(plan)):
        s_w, s_c = s_next
        if blk + 1 < len(plan):
            s_next = scores(blk + 1)
        if pending is not None:
            weighted_values(blk - 1, *pending)
        m = jnp.maximum(jnp.max(s_w, axis=-1, keepdims=True), jnp.max(s_c, axis=-1, keepdims=True))
        e_w = jnp.exp2(s_w - m)
        e_c = jnp.exp2(s_c - m)
        l = jnp.sum(e_w, axis=-1, keepdims=True) + jnp.sum(e_c, axis=-1, keepdims=True)
        pending = (e_w.astype(BF16), e_c.astype(BF16), l)
    weighted_values(len(plan) - 1, *pending)


def _na_attn(p1, rpb, n_batch, seq, ctx_len, n_heads):
    rows = seq // GRID_W
    kb, layouts, layout_of = _na_block_plan(rows)
    bias = _na_bias(_rpb_pairs(rpb), layouts)
    n_lay = layouts.shape[0]
    tq = NA_Q_ROWS * GRID_W
    nkw = NA_K_ROWS * GRID_W
    ctx_rb0 = (n_batch * seq) // ctx_len
    plan = tuple((int(k), int(l)) for k, l in zip(kb, layout_of))
    return pl.pallas_call(
        functools.partial(_na_kernel, plan=plan),
        grid=(n_heads, n_batch),
        in_specs=[
            pl.BlockSpec((seq, LANES), lambda h, b: (b, h)),
            pl.BlockSpec((seq, LANES), lambda h, b: (b, n_heads + h)),
            pl.BlockSpec((seq, LANES), lambda h, b: (b, 2 * n_heads + h)),
            pl.BlockSpec((ctx_len, LANES), lambda h, b: (ctx_rb0 + b, n_heads + h)),
            pl.BlockSpec((ctx_len, LANES), lambda h, b: (ctx_rb0 + b, 2 * n_heads + h)),
            pl.BlockSpec((None, n_lay, tq, nkw), lambda h, b: (h, 0, 0, 0)),
        ],
        out_specs=pl.BlockSpec((seq, LANES), lambda h, b: (b, h)),
        out_shape=jax.ShapeDtypeStruct((n_batch * seq, n_heads * LANES), BF16),
        compiler_params=_params(("arbitrary", "arbitrary"), n_lay * tq * nkw * 4 + 4 * seq * LANES * 2),
        name="na_attn",
    )(p1, p1, p1, p1, p1, bias)


def _pack_rows(rows, n_rows=8):
    out = jnp.zeros((n_rows, LANES), F32)
    for r, v in enumerate(rows):
        out = out.at[r, :v.shape[0]].set(v.astype(F32))
    return out


def kernel(x, c, ctx, c_ctx, ada_w, ada_b, norm_mix_g, norm_ffn_g, final_norm_g, ffn_w1, ffn_w3, ffn_w2, ev_w_in, ev_conv_w, ev_conv_b, ev_a_log, ev_dt_bias, ev_d_skip, ev_ssm_norm_g, ev_lam_q1, ev_lam_k1, ev_lam_q2, ev_lam_k2, ev_subln_g, ev_w_out, od_w_in, od_rpb, od_w_out):
    n_batch, seq, d = x.shape
    ctx_len = ctx.shape[1]
    depth = ada_w.shape[0]
    n_lat = n_batch * seq
    n_tok = n_lat + n_batch * ctx_len

    d_ssm = ev_ssm_norm_g.shape[1]
    n_ssm_heads = ev_d_skip.shape[1]
    d_xbc = ev_conv_w.shape[2]
    d_qk = d_v = (ev_w_in.shape[2] - d_ssm - d_xbc - 2 * n_ssm_heads) // 3
    n_diff_heads = d_v // (2 * DIFF_HEAD_DIM)
    n_na_heads = od_rpb.shape[1]

    stream = (x.reshape(n_lat, d), ctx.reshape(n_batch * ctx_len, d))
    cond = jnp.zeros((COND_ROWS, d), F32).at[:n_batch].set(c).at[n_batch].set(c_ctx)
    mod = _ada_mod(cond, ada_w, ada_b).reshape(depth, COND_ROWS, 1, 6 * d)
    rope_tabs = _rope_tables(seq)
    kw = dict(seq=seq, n_batch=n_batch)
    ffn_w = (ffn_w1, ffn_w3, ffn_w2)

    for i in range(depth):
        ctx_out = i < depth - 1
        m_rows = n_tok if ctx_out else n_lat
        j = i // 2
        if i % 2 == 0:
            lambda_init = 0.8 - 0.6 * math.exp(-0.3 * i)
            w_in = ev_w_in[j]
            dt0 = d_ssm + d_xbc
            w_parts = [w_in[:, :dt0].astype(BF16), w_in[:, dt0 + 2 * n_ssm_heads:].astype(BF16)]
            w_dt = jnp.zeros((d, 2 * LANES), F32)
            w_dt = w_dt.at[:, :n_ssm_heads].set(w_in[:, dt0:dt0 + n_ssm_heads])
            w_dt = w_dt.at[:, LANES:LANES + n_ssm_heads].set(w_in[:, dt0 + n_ssm_heads:dt0 + 2 * n_ssm_heads])
            p0, dt_raw = _proj(stream, norm_mix_g[i], mod, i, w_parts, w_dt.astype(BF16), F32, **kw)
            q0 = d_ssm + d_xbc
            k0, v0 = q0 + d_qk, q0 + 2 * d_qk

            xbc = _ssd_conv(p0, d_ssm, d_xbc, ev_conv_w[j], ev_conv_b[j], seq, ctx_len, n_lat)
            prm = jnp.stack([_pack_rows([ev_a_log[j, r], ev_dt_bias[j, r]]) for r in range(2)])
            y2 = _ssd_scan(xbc, dt_raw, prm, n_batch, seq, ctx_len, d_ssm)
            mix_ssd = _ssd_finish(y2, xbc, p0, jnp.repeat(ev_d_skip[j], SSM_HEAD_DIM), ev_ssm_norm_g[j], d_ssm)

            lam_p = _pack_rows([ev_lam_q1[j], ev_lam_k1[j], ev_lam_q2[j], ev_lam_k2[j]])
            attn_args = (p0, q0, k0, v0, n_diff_heads, rope_tabs, lam_p, ev_subln_g[j], lambda_init,
                         n_batch, seq, ctx_len)
            mix_attn = _diff_attn(*attn_args, with_lat=True)
            if ctx_out:
                mix_attn = (mix_attn, _diff_attn(*attn_args, with_lat=False))
            lhs = [mix_ssd, mix_attn]
            w_out = ev_w_out[j].astype(BF16)
        else:
            d_na = n_na_heads * NA_HEAD_DIM
            q_scale = jnp.where(jnp.arange(3 * d_na) < d_na, NA_HEAD_DIM ** -0.5 * LOG2E, 1.0).astype(F32)
            p1 = _proj(stream, norm_mix_g[i], mod, i, od_w_in[j].astype(BF16), None, BF16, col_scale=q_scale, **kw)
            assert not ctx_out, "context-query neighbourhood layers are not needed at this depth"
            lhs = [_na_attn(p1, od_rpb[j], n_batch, seq, ctx_len, n_na_heads)]
            w_out = od_w_out[j].astype(BF16)
        stream = _out_proj(lhs, w_out, stream, mod, i, 2, m_rows, **kw)
        stream = _ffn(stream, norm_ffn_g[i], mod, i, *ffn_w, final_norm_g, not ctx_out, m_rows, **kw)
    return stream.reshape(n_batch, seq, d)
```

```python
import functools
import math

import jax
import jax.numpy as jnp
import numpy as np
from jax import lax
from jax.experimental import pallas as pl
from jax.experimental.pallas import tpu as pltpu

F32 = jnp.float32
BF16 = jnp.bfloat16

GRID_W = 64
SSM_HEAD_DIM = 64
SSM_GROUPS = 4
SSM_STATE = 128
SSM_CONV = 5
SSM_CHUNK = 128
DIFF_HEAD_DIM = 64
NA_HEAD_DIM = 128
WIN_ROWS = 8
WIN_COLS = 16
ROPE_BASE = 10000.0
NORM_EPS = 1e-6
LOG2E = math.log2(math.e)

LANES = 128
V7X_VMEM_BYTES = 64 * 1024 * 1024
VMEM_HEADROOM_BYTES = 3 * 1024 * 1024

NA_Q_ROWS = 4
NA_K_ROWS = NA_Q_ROWS + WIN_ROWS - 1
COND_ROWS = 16


def _vmem_limit(block_bytes):
    return int(min(V7X_VMEM_BYTES - VMEM_HEADROOM_BYTES, max(32 * 1024 * 1024, 2 * block_bytes + 16 * 1024 * 1024)))


def _params(semantics, block_bytes):
    return pltpu.CompilerParams(dimension_semantics=semantics, vmem_limit_bytes=_vmem_limit(block_bytes))


def _silu(v):
    return v * jax.nn.sigmoid(v)


def _mod_row(i, tm, seq, n_batch):
    return jnp.minimum((i * tm) // seq, n_batch)


def _ada_kernel(cond_ref, w_ref, b_ref, o_ref):
    s = _silu(cond_ref[...]).astype(BF16)
    o_ref[...] = jnp.dot(s, w_ref[...].astype(BF16), preferred_element_type=F32) + b_ref[...]


def _ada_mod(cond, ada_w, ada_b, tn=1024):
    depth, d, n = ada_w.shape
    return pl.pallas_call(
        _ada_kernel,
        grid=(depth, n // tn),
        in_specs=[
            pl.BlockSpec((COND_ROWS, d), lambda l, j: (0, 0)),
            pl.BlockSpec((None, d, tn), lambda l, j: (l, 0, j)),
            pl.BlockSpec((None, 1, tn), lambda l, j: (l, 0, j)),
        ],
        out_specs=pl.BlockSpec((None, COND_ROWS, tn), lambda l, j: (l, 0, j)),
        out_shape=jax.ShapeDtypeStruct((depth, COND_ROWS, n), F32),
        compiler_params=_params(("arbitrary", "arbitrary"), d * tn * 4),
        name="ada_mod",
    )(cond, ada_w, ada_b.reshape(depth, 1, n))


def _norm_mod_pipeline(x_ref, g_ref, mod_ref, shift_idx, scale_idx, n_chunks, consume):
    tm, d = x_ref.shape
    shift = mod_ref[:, shift_idx * d:(shift_idx + 1) * d]
    gain = g_ref[...] * (1.0 + mod_ref[:, scale_idx * d:(scale_idx + 1) * d])
    chunk = tm // n_chunks

    def norm_rows(c):
        x = x_ref[c * chunk:(c + 1) * chunk, :]
        inv = lax.rsqrt(jnp.mean(x * x, axis=-1, keepdims=True) + NORM_EPS)
        return (x * inv * gain + shift).astype(BF16)

    h_next = norm_rows(0)
    for c in range(n_chunks):
        h = h_next
        if c + 1 < n_chunks:
            h_next = norm_rows(c + 1)
        consume(slice(c * chunk, (c + 1) * chunk), h)


def _proj_kernel(*refs, n_x, n_w, tiles_per_part, n_lat_tiles, has_aux, has_scale, first_step_chunks=4):
    rest = list(refs)
    x_refs = [rest.pop(0) for _ in range(n_x)]
    g_ref, mod_ref = rest.pop(0), rest.pop(0)
    w_refs = [rest.pop(0) for _ in range(max(n_w, 1))]
    cs_ref = rest.pop(0) if has_scale else None
    if has_aux:
        waux_ref, o_ref, oaux_ref, h_ref = rest
    else:
        o_ref, h_ref = rest
    dims = (((1,), (1,)), ((), ())) if n_w else (((1,), (0,)), ((), ()))
    j = pl.program_id(1)
    tm, d = x_refs[0].shape

    def emit(h, rows, w_ref):
        acc = lax.dot_general(h, w_ref[...], dims, preferred_element_type=F32)
        if has_scale:
            acc = acc * cs_ref[...]
        o_ref[rows, :] = acc.astype(o_ref.dtype)

    def first_tile(x_ref):
        def consume(rows, h):
            h_ref[rows, :] = h
            emit(h, rows, w_refs[0])
            if has_aux:
                oaux_ref[rows, :] = lax.dot_general(h, waux_ref[...], dims, preferred_element_type=F32)

        _norm_mod_pipeline(x_ref, g_ref, mod_ref, 0, 1, first_step_chunks, consume)

    if n_x == 1:
        pl.when(j == 0)(functools.partial(first_tile, x_refs[0]))
    else:
        is_lat = pl.program_id(0) < n_lat_tiles
        pl.when(jnp.logical_and(j == 0, is_lat))(functools.partial(first_tile, x_refs[0]))
        pl.when(jnp.logical_and(j == 0, jnp.logical_not(is_lat)))(functools.partial(first_tile, x_refs[1]))

    for p in range(max(n_w, 1)):
        lo = max(p * tiles_per_part, 1)
        in_part = (j >= lo) if n_w == 0 else jnp.logical_and(j >= lo, j < (p + 1) * tiles_per_part)
        pl.when(in_part)(lambda p=p: emit(h_ref[...], slice(None), w_refs[p]))


def _proj(x, g, mod, layer, w, w_aux, out_dtype, seq, n_batch, col_scale=None, tm=1024, tn=1024):
    parts = list(w) if isinstance(w, (list, tuple)) else None
    if parts is None:
        d, n = w.shape
        n_w, tiles_per_part = 0, 0
    else:
        d = parts[0].shape[1]
        n_w, tiles_per_part = len(parts), parts[0].shape[0] // tn
        assert all(p.shape == (tiles_per_part * tn, d) for p in parts)
        n = n_w * tiles_per_part * tn
    m = sum(a.shape[0] for a in x) if isinstance(x, tuple) else x.shape[0]
    n_lat_tiles = (n_batch * seq) // tm
    has_aux = w_aux is not None
    has_scale = col_scale is not None
    mod_spec = pl.BlockSpec((None, None, 1, mod.shape[-1]),
                            lambda i, j: (layer, _mod_row(i, tm, seq, n_batch), 0, 0))
    in_specs, args = _row_operand(x, tm, n_lat_tiles, d, lambda j: 0, single_ctx_buffer=True)
    n_x = len(args)
    in_specs += [pl.BlockSpec((1, d), lambda i, j: (0, 0)), mod_spec]
    args += [g.reshape(1, d), mod]
    if parts is None:
        in_specs.append(pl.BlockSpec((d, tn), lambda i, j: (0, j)))
        args.append(w)
    else:
        for p, part in enumerate(parts):
            in_specs.append(pl.BlockSpec(
                (tn, d), lambda i, j, p=p: (jnp.clip(j - p * tiles_per_part, 0, tiles_per_part - 1), 0)))
            args.append(part)
    out_specs = [pl.BlockSpec((tm, tn), lambda i, j: (i, j))]
    out_shape = [jax.ShapeDtypeStruct((m, n), out_dtype)]
    if has_scale:
        in_specs.append(pl.BlockSpec((1, tn), lambda i, j: (0, j)))
        args.append(col_scale.reshape(1, n))
    if has_aux:
        na = w_aux.shape[0] if parts is not None else w_aux.shape[1]
        in_specs.append(pl.BlockSpec(w_aux.shape, lambda i, j: (0, 0)))
        out_specs.append(pl.BlockSpec((tm, na), lambda i, j: (i, 0)))
        out_shape.append(jax.ShapeDtypeStruct((m, na), F32))
        args.append(w_aux)
    block_bytes = n_x * tm * d * 4 + max(n_w, 1) * d * tn * 2 + tm * tn * 4 + tm * d
    outs = pl.pallas_call(
        functools.partial(_proj_kernel, n_x=n_x, n_w=n_w, tiles_per_part=tiles_per_part, n_lat_tiles=n_lat_tiles,
                          has_aux=has_aux, has_scale=has_scale),
        grid=(m // tm, n // tn),
        in_specs=in_specs,
        out_specs=out_specs,
        out_shape=out_shape,
        scratch_shapes=[pltpu.VMEM((tm, d), BF16)],
        compiler_params=_params(("arbitrary", "arbitrary"), block_bytes),
        name="proj",
    )(*args)
    return outs if has_aux else outs[0]


def _row_operand(a, tm, n_lat_tiles, width, col_of_j, single_ctx_buffer=False):
    if not isinstance(a, tuple):
        return [pl.BlockSpec((tm, width), lambda i, j: (i, col_of_j(j)))], [a]
    lat, ctx = a
    assert ctx.shape[0] % tm == 0 and lat.shape[0] == n_lat_tiles * tm

    def lat_index(i, j):
        return jnp.minimum(i, n_lat_tiles - 1), jnp.where(i < n_lat_tiles, col_of_j(j), 0)

    def ctx_index(i, j):
        return jnp.maximum(i - n_lat_tiles, 0), jnp.where(i < n_lat_tiles, 0, col_of_j(j))

    ctx_mode = dict(pipeline_mode=pl.Buffered(1)) if single_ctx_buffer else {}
    return [pl.BlockSpec((tm, width), lat_index), pl.BlockSpec((tm, width), ctx_index, **ctx_mode)], [lat, ctx]


def _pick_rows(refs, n_lat_tiles):
    if len(refs) == 1:
        return refs[0][...]
    return jnp.where(pl.program_id(0) < n_lat_tiles, refs[0][...], refs[1][...])


def _out_proj_kernel(*refs, arity, n_lat_tiles):
    refs = list(refs)
    n_lhs = len(arity) - 1
    groups = [[refs.pop(0) for _ in range(n)] for n in arity[:-1]]
    ws = [refs.pop(0) for _ in range(n_lhs)]
    res = [refs.pop(0) for _ in range(arity[-1])]
    gate_ref, o_ref = refs
    acc = None
    for grp, w_ref in zip(groups, ws):
        part = jnp.dot(_pick_rows(grp, n_lat_tiles), w_ref[...], preferred_element_type=F32)
        acc = part if acc is None else acc + part
    o_ref[...] = _pick_rows(res, n_lat_tiles) + gate_ref[...] * acc


def _out_proj(lhs_list, w, res, mod, layer, gate_idx, m_rows, seq, n_batch, tm=512, tn=2048):
    d = w.shape[1]
    n_lat_tiles = (n_batch * seq) // tm
    in_specs, args, arity, widths = [], [], [], []
    for a in lhs_list:
        kk = (a[0] if isinstance(a, tuple) else a).shape[1]
        sp, ar = _row_operand(a, tm, n_lat_tiles, kk, lambda j: 0)
        in_specs += sp
        args += ar
        arity.append(len(ar))
        widths.append(kk)
    row0 = 0
    for kk in widths:
        in_specs.append(pl.BlockSpec((kk, tn), lambda i, j, rb=row0 // kk: (rb, j)))
        args.append(w)
        row0 += kk
    sp, ar = _row_operand(res, tm, n_lat_tiles, tn, lambda j: j)
    in_specs += sp
    args += ar
    arity.append(len(ar))
    in_specs.append(pl.BlockSpec((None, None, 1, tn),
                                 lambda i, j: (layer, _mod_row(i, tm, seq, n_batch), 0, gate_idx * (d // tn) + j)))
    args.append(mod)
    k_total = sum(widths)
    block_bytes = 2 * tm * k_total * 2 + k_total * tn * 2 + 3 * tm * tn * 4
    return pl.pallas_call(
        functools.partial(_out_proj_kernel, arity=tuple(arity), n_lat_tiles=n_lat_tiles),
        grid=(m_rows // tm, d // tn),
        in_specs=in_specs,
        out_specs=pl.BlockSpec((tm, tn), lambda i, j: (i, j)),
        out_shape=jax.ShapeDtypeStruct((m_rows, d), F32),
        compiler_params=_params(("arbitrary", "arbitrary"), block_bytes),
        name="out_proj",
    )(*args)


def _ffn_kernel(x_ref, g_ref, mod_ref, w1_ref, w3_ref, w2_ref, fg_ref, o_ref, h_ref, *, final_norm, rows,
                first_step_chunks):
    j = pl.program_id(1)
    tm, d = x_ref.shape

    def swiglu(h, w1, w3, w2):
        a = jnp.dot(h, w1, preferred_element_type=F32)
        b = jnp.dot(h, w3, preferred_element_type=F32)
        return jnp.dot((_silu(a) * b).astype(BF16), w2, preferred_element_type=F32)

    @pl.when(j == 0)
    def _():
        w1, w3, w2 = (w_ref[...].astype(BF16) for w_ref in (w1_ref, w3_ref, w2_ref))

        def consume(rows, h):
            h_ref[rows, :] = h
            o_ref[rows, :] = swiglu(h, w1, w3, w2)

        _norm_mod_pipeline(x_ref, g_ref, mod_ref, 3, 4, first_step_chunks, consume)

    @pl.when(j > 0)
    def _():
        o_ref[...] += swiglu(h_ref[...], w1_ref[...].astype(BF16), w3_ref[...].astype(BF16),
                             w2_ref[...].astype(BF16))

    @pl.when(j == pl.num_programs(1) - 1)
    def _():
        gate = mod_ref[:, 5 * d:6 * d]
        fg = fg_ref[...]

        def body(r, carry):
            sl = pl.ds(pl.multiple_of(r * rows, rows), rows)
            y = x_ref[sl, :] + gate * o_ref[sl, :]
            if final_norm:
                y = y * lax.rsqrt(jnp.mean(y * y, axis=-1, keepdims=True) + NORM_EPS) * fg
            o_ref[sl, :] = y
            return carry

        lax.fori_loop(0, x_ref.shape[0] // rows, body, 0)


def _ffn(x, g, mod, layer, w1, w3, w2, final_g, final_norm, m_rows, seq, n_batch, tm=1024, tf=256):
    d = x.shape[1]
    ff = w1.shape[2]
    block_bytes = 2 * tm * d * 4 + d * tf * sum(w.dtype.itemsize for w in (w1, w3, w2)) + tm * d
    return pl.pallas_call(
        functools.partial(_ffn_kernel, final_norm=final_norm, rows=128, first_step_chunks=4),
        grid=(m_rows // tm, ff // tf),
        in_specs=[
            pl.BlockSpec((tm, d), lambda i, j: (i, 0)),
            pl.BlockSpec((1, d), lambda i, j: (0, 0)),
            pl.BlockSpec((None, None, 1, mod.shape[-1]),
                         lambda i, j: (layer, _mod_row(i, tm, seq, n_batch), 0, 0)),
            pl.BlockSpec((None, d, tf), lambda i, j: (layer, 0, j)),
            pl.BlockSpec((None, d, tf), lambda i, j: (layer, 0, j)),
            pl.BlockSpec((None, tf, d), lambda i, j: (layer, j, 0)),
            pl.BlockSpec((1, d), lambda i, j: (0, 0)),
        ],
        out_specs=pl.BlockSpec((tm, d), lambda i, j: (i, 0)),
        out_shape=jax.ShapeDtypeStruct((m_rows, d), F32),
        scratch_shapes=[pltpu.VMEM((tm, d), BF16)],
        compiler_params=_params(("arbitrary", "arbitrary"), block_bytes),
        name="ffn",
    )(x, g.reshape(1, d), mod, w1, w3, w2, final_g.reshape(1, d))


def _conv_kernel(prev_ref, cur_ref, next_ref, w_ref, b_ref, o_ref, ext_ref, *, tm, halo, seq, ctx_len, n_lat_tiles):
    i = pl.program_id(0)
    half = SSM_CONV // 2
    ext_ref[halo:halo + tm, :] = cur_ref[...]

    def taps(mask_of):
        acc = jnp.zeros(cur_ref.shape, F32) + b_ref[...]
        for t in range(SSM_CONV):
            tap = ext_ref[halo - half + t:halo - half + t + tm, :]
            acc = acc + mask_of(t, tap) * w_ref[t:t + 1, :]
        o_ref[...] = _silu(acc)

    @pl.when(i < n_lat_tiles)
    def _():
        prev_in_seg = ((i * tm) & (seq - 1)) != 0
        next_in_seg = (((i + 1) * tm) & (seq - 1)) != 0
        ext_ref[0:halo, :] = jnp.where(prev_in_seg, prev_ref[...], 0.0)
        ext_ref[halo + tm:, :] = jnp.where(next_in_seg, next_ref[...], 0.0)
        taps(lambda t, tap: tap)

    @pl.when(i >= n_lat_tiles)
    def _():
        ext_ref[0:halo, :] = prev_ref[...]
        ext_ref[halo + tm:, :] = next_ref[...]
        pos = (i * tm + lax.broadcasted_iota(jnp.int32, (tm, 1), 0)) & (ctx_len - 1)

        def masked(t, tap):
            src = pos + (t - half)
            return jnp.where((src >= 0) & (src < ctx_len), tap, 0.0)

        taps(masked)


def _ssd_conv(p0, col0, width, conv_w, conv_b, seq, ctx_len, n_lat_rows, tm=1024, tc=1024, halo=8):
    m = p0.shape[0]
    assert seq % tm == 0 and tm % ctx_len == 0 and (seq & (seq - 1)) == 0 and (ctx_len & (ctx_len - 1)) == 0
    cb0 = col0 // tc
    hb = tm // halo
    last_hb = m // halo - 1
    wpad = jnp.zeros((8, width), F32).at[:SSM_CONV].set(conv_w)
    return pl.pallas_call(
        functools.partial(_conv_kernel, tm=tm, halo=halo, seq=seq, ctx_len=ctx_len, n_lat_tiles=n_lat_rows // tm),
        grid=(m // tm, width // tc),
        in_specs=[
            pl.BlockSpec((halo, tc), lambda i, j: (jnp.maximum(i * hb - 1, 0), cb0 + j)),
            pl.BlockSpec((tm, tc), lambda i, j: (i, cb0 + j)),
            pl.BlockSpec((halo, tc), lambda i, j: (jnp.minimum((i + 1) * hb, last_hb), cb0 + j)),
            pl.BlockSpec((8, tc), lambda i, j: (0, j)),
            pl.BlockSpec((1, tc), lambda i, j: (0, j)),
        ],
        out_specs=pl.BlockSpec((tm, tc), lambda i, j: (i, j)),
        out_shape=jax.ShapeDtypeStruct((m, width), F32),
        scratch_shapes=[pltpu.VMEM((tm + 2 * halo, tc), F32)],
        compiler_params=_params(("arbitrary", "arbitrary"), 3 * tm * tc * 4),
        name="ssd_conv",
    )(p0, p0, p0, wpad, conv_b.reshape(1, width))


def _ssd_scan_kernel(xs_f, bm_f, cm_f, dt_f, xs_b, bm_b, cm_b, dt_b, prm_ref, yf_ref, yb_ref, state_ref, *, n_heads):
    @pl.when(pl.program_id(1) == 0)
    def _():
        state_ref[...] = jnp.zeros_like(state_ref)

    fwd = _ssd_chunk(xs_f, bm_f, cm_f, dt_f, prm_ref.at[0], yf_ref, state_ref.at[0], backward=False, n_heads=n_heads)
    bwd = _ssd_chunk(xs_b, bm_b, cm_b, dt_b, prm_ref.at[1], yb_ref, state_ref.at[1], backward=True, n_heads=n_heads)
    for g in range(SSM_GROUPS):
        fwd(g)
        bwd(g)


def _ssd_chunk(xs_ref, bm_ref, cm_ref, dt_ref, prm_ref, y_ref, state_ref, *, backward, n_heads):
    hp = SSM_HEAD_DIM
    rep = n_heads // SSM_GROUPS
    lc = SSM_CHUNK
    assert 2 * hp == LANES and rep % 2 == 0 and SSM_STATE == LANES and lc == LANES

    dt_in = dt_ref[...] + prm_ref[1:2, :]
    dt = jnp.maximum(dt_in, 0.0) + jnp.log1p(jnp.exp(-jnp.abs(dt_in)))
    dta = dt * (-jnp.exp(prm_ref[0:1, :]) * LOG2E)
    row = lax.broadcasted_iota(jnp.int32, (lc, lc), 0)
    col = lax.broadcasted_iota(jnp.int32, (lc, lc), 1)
    causal = (row <= col) if backward else (row >= col)
    a_cum = jnp.dot(causal.astype(F32), dta, preferred_element_type=F32, precision=lax.Precision.HIGHEST)
    a_tot = a_cum[0:1, :] if backward else a_cum[lc - 1:lc, :]
    w_end = dt * jnp.exp2(a_tot - a_cum)
    a_cum_t, dt_t, w_end_t = a_cum.T, dt.T, w_end.T
    lo = col < hp

    def do_group(g):
        gs = slice(g * SSM_STATE, (g + 1) * SSM_STATE)
        b_f = bm_ref[:, gs]
        c_g = cm_ref[:, gs].astype(BF16)
        cb = lax.dot_general(c_g, b_f.astype(BF16), (((1,), (1,)), ((), ())), preferred_element_type=F32)
        b_t = b_f.T
        cols_g = slice(g * rep * hp, (g + 1) * rep * hp)
        y_off = jnp.dot(c_g, state_ref[:, cols_g].astype(BF16), preferred_element_type=F32)
        for pr in range(rep // 2):
            cols = slice(g * rep * hp + pr * LANES, g * rep * hp + (pr + 1) * LANES)
            xs2 = xs_ref[:, cols].astype(BF16)
            y_d, s_n, e_a = [], [], []
            for h in (g * rep + 2 * pr, g * rep + 2 * pr + 1):
                a_col = jnp.broadcast_to(a_cum[:, h:h + 1], (lc, lc))
                decay = jnp.exp2(jnp.where(causal, a_col - a_cum_t[h:h + 1, :], -jnp.inf))
                m = (cb * decay * dt_t[h:h + 1, :]).astype(BF16)
                y_d.append(jnp.dot(m, xs2, preferred_element_type=F32))
                s_n.append(jnp.dot((b_t * w_end_t[h:h + 1, :]).astype(BF16), xs2, preferred_element_type=F32))
                e_a.append(jnp.exp2(a_col))
            h0 = g * rep + 2 * pr
            y_ref[:, cols] = (jnp.where(lo, y_d[0], y_d[1])
                              + y_off[:, pr * LANES:(pr + 1) * LANES] * jnp.where(lo, e_a[0], e_a[1]))
            chunk_decay = jnp.where(lo[0:1, :], jnp.exp2(a_tot[:, h0:h0 + 1]), jnp.exp2(a_tot[:, h0 + 1:h0 + 2]))
            state_ref[:, cols] = state_ref[:, cols] * chunk_decay + jnp.where(lo, s_n[0], s_n[1])

    return do_group


def _ssd_scan(xbc, dt_raw, prm, n_batch, seq, ctx_len, d_ssm):
    m = xbc.shape[0]
    lc = SSM_CHUNK
    n_heads = d_ssm // SSM_HEAD_DIM
    nc_ctx, nc_lat = ctx_len // lc, seq // lc
    ctx_blk0 = n_batch * nc_lat
    gn = SSM_GROUPS * SSM_STATE

    def row_blk(b, s, backward):
        ctx_c = nc_ctx - 1 - s if backward else s
        lat_c = nc_lat - 1 - (s - nc_ctx) if backward else s - nc_ctx
        return jnp.where(s < nc_ctx, ctx_blk0 + b * nc_ctx + ctx_c, b * nc_lat + lat_c)

    def chunk_specs(backward):
        r = int(backward)
        return [
            pl.BlockSpec((lc, d_ssm), lambda b, s: (row_blk(b, s, backward), 0)),
            pl.BlockSpec((lc, gn), lambda b, s: (row_blk(b, s, backward), d_ssm // gn)),
            pl.BlockSpec((lc, gn), lambda b, s: (row_blk(b, s, backward), d_ssm // gn + 1)),
            pl.BlockSpec((lc, LANES), lambda b, s: (row_blk(b, s, backward), r)),
        ]

    return pl.pallas_call(
        functools.partial(_ssd_scan_kernel, n_heads=n_heads),
        grid=(n_batch, nc_ctx + nc_lat),
        in_specs=chunk_specs(False) + chunk_specs(True) + [pl.BlockSpec((2, 8, LANES), lambda b, s: (0, 0, 0))],
        out_specs=[pl.BlockSpec((lc, d_ssm), lambda b, s: (row_blk(b, s, False), 0)),
                   pl.BlockSpec((lc, d_ssm), lambda b, s: (row_blk(b, s, True), 0))],
        out_shape=[jax.ShapeDtypeStruct((m, d_ssm), F32), jax.ShapeDtypeStruct((m, d_ssm), F32)],
        scratch_shapes=[pltpu.VMEM((2, SSM_STATE, d_ssm), F32)],
        compiler_params=_params(("arbitrary", "arbitrary"), 8 * lc * d_ssm * 4),
        name="ssd_scan",
    )(xbc, xbc, xbc, dt_raw, xbc, xbc, xbc, dt_raw, prm)


def _ssd_finish_kernel(yf_ref, yb_ref, xs_ref, z_ref, dskip_ref, g_ref, o_ref, *, group_width):
    y = (yf_ref[...] + yb_ref[...] + dskip_ref[...] * xs_ref[...]) * _silu(z_ref[...])
    for g in range(y.shape[1] // group_width):
        sl = slice(g * group_width, (g + 1) * group_width)
        v = y[:, sl]
        vn = v * lax.rsqrt(jnp.mean(v * v, axis=-1, keepdims=True) + NORM_EPS) * g_ref[:, sl]
        o_ref[:, sl] = vn.astype(o_ref.dtype)


def _ssd_finish(y2, xbc, p0, d_skip_lanes, norm_g, d_ssm, tm=512):
    m = xbc.shape[0]
    return pl.pallas_call(
        functools.partial(_ssd_finish_kernel, group_width=d_ssm // SSM_GROUPS),
        grid=(m // tm,),
        in_specs=[
            pl.BlockSpec((tm, d_ssm), lambda i: (i, 0)),
            pl.BlockSpec((tm, d_ssm), lambda i: (i, 0)),
            pl.BlockSpec((tm, d_ssm), lambda i: (i, 0)),
            pl.BlockSpec((tm, d_ssm), lambda i: (i, 0)),
            pl.BlockSpec((1, d_ssm), lambda i: (0, 0)),
            pl.BlockSpec((1, d_ssm), lambda i: (0, 0)),
        ],
        out_specs=pl.BlockSpec((tm, d_ssm), lambda i: (i, 0)),
        out_shape=jax.ShapeDtypeStruct((m, d_ssm), BF16),
        compiler_params=_params(("arbitrary",), 5 * tm * d_ssm * 4),
        name="ssd_finish",
    )(*y2, xbc, p0, d_skip_lanes.reshape(1, d_ssm), norm_g.reshape(1, d_ssm))


def _rope(x, cos, sin_up, sin_dn):
    quarter = DIFF_HEAD_DIM // 4
    return (x * cos + pltpu.roll(x, LANES - quarter, axis=1) * sin_up + pltpu.roll(x, quarter, axis=1) * sin_dn)


def _diff_attn_heads_kernel(*refs, heads_per_step, with_lat, **kw):
    n_head_refs = 5 if with_lat else 3
    *ins, o_ref, k_scr, vt_scr = refs
    for h in range(heads_per_step):
        cs = slice(h * LANES, (h + 1) * LANES)
        head_ins = [r.at[:, cs] for r in ins[:n_head_refs]] + list(ins[n_head_refs:])
        _diff_attn_kernel(*head_ins, o_ref.at[:, cs], k_scr.at[h], vt_scr.at[h], with_lat=with_lat, **kw)


def _diff_attn_kernel(*refs, with_lat, lambda_init, n_ctx, sub_q, key_chunk=768):
    if with_lat:
        (q_ref, kc_ref, kl_ref, vc_ref, vl_ref, qcos_ref, qsu_ref, qsd_ref, kcos_ref, ksu_ref, ksd_ref,
         lam_ref, g_ref, o_ref, k_scr, vt_scr) = refs
    else:
        q_ref, kc_ref, vc_ref, lam_ref, g_ref, o_ref, k_scr, vt_scr = refs

    @pl.when(pl.program_id(2) == 0)
    def _():
        k_scr[0:n_ctx, :] = kc_ref[...].astype(BF16)
        vt_scr[:, 0:n_ctx] = vc_ref[...].T.astype(BF16)
        if with_lat:
            k_scr[n_ctx:, :] = _rope(kl_ref[...], kcos_ref[...], ksu_ref[...], ksd_ref[...]).astype(BF16)
            vt_scr[:, n_ctx:] = vl_ref[...].T.astype(BF16)

    lam_p = lam_ref[...]
    lam = (jnp.exp(jnp.sum(lam_p[0:1, :] * lam_p[1:2, :], axis=-1, keepdims=True))
           - jnp.exp(jnp.sum(lam_p[2:3, :] * lam_p[3:4, :], axis=-1, keepdims=True)) + lambda_init)
    first = lax.broadcasted_iota(jnp.int32, (LANES, sub_q), 0) < DIFF_HEAD_DIM
    n_sub = q_ref.shape[0] // sub_q

    def scores(t):
        rows = slice(t * sub_q, (t + 1) * sub_q)
        q = q_ref[rows, :]
        if with_lat:
            q = _rope(q, qcos_ref[rows, :], qsu_ref[rows, :], qsd_ref[rows, :])
        qt = (q * (DIFF_HEAD_DIM ** -0.5 * LOG2E)).T
        q2t = jnp.concatenate([jnp.where(first, qt, 0.0), jnp.where(first, 0.0, qt)], axis=1).astype(BF16)
        blocks = [jnp.dot(k_scr[k0:k1, :], q2t, preferred_element_type=F32) for k0, k1 in key_chunks]
        return blocks, [jnp.max(b, axis=0, keepdims=True) for b in blocks]

    n_keys = k_scr.shape[0]
    key_chunks = [(k0, min(k0 + key_chunk, n_keys)) for k0 in range(0, n_keys, key_chunk)]
    s_next = scores(0)
    for t in range(n_sub):
        rows = slice(t * sub_q, (t + 1) * sub_q)
        s_blocks, maxes = s_next
        if t + 1 < n_sub:
            s_next = scores(t + 1)
        m = functools.reduce(jnp.maximum, maxes)
        ovt, l = None, None
        for (k0, k1), s in zip(key_chunks, s_blocks):
            e = jnp.exp2(s - m)
            part = jnp.dot(vt_scr[:, k0:k1], e.astype(BF16), preferred_element_type=F32)
            part_l = jnp.sum(e, axis=0, keepdims=True)
            ovt, l = (part, part_l) if ovt is None else (ovt + part, l + part_l)
        ovt = ovt * (1.0 / l)
        ot = ovt[:, :sub_q] - lam * ovt[:, sub_q:]
        ot = ot * lax.rsqrt(jnp.mean(ot * ot, axis=0, keepdims=True) + NORM_EPS) * g_ref[...] * (1.0 - lambda_init)
        o_ref[rows, :] = ot.T.astype(o_ref.dtype)


def _diff_attn(p0, q_col0, k_col0, v_col0, n_heads, rope_tabs, lam_p, subln_g, lambda_init,
               n_batch, seq, ctx_len, with_lat, tq=2048, sub_q=256):
    ctx_rb0 = (n_batch * seq) // ctx_len
    n_keys = ctx_len + (seq if with_lat else 0)
    hps = 1 if with_lat else n_heads
    width = hps * LANES
    cb = lambda c0: c0 // width
    if with_lat:
        n_q = seq // tq
        q_spec = pl.BlockSpec((tq, width), lambda b, h, i: (b * n_q + i, cb(q_col0) + h))
    else:
        tq = ctx_len
        n_q = 1
        q_spec = pl.BlockSpec((tq, width), lambda b, h, i: (ctx_rb0 + b, cb(q_col0) + h))
    kc_spec = pl.BlockSpec((ctx_len, width), lambda b, h, i: (ctx_rb0 + b, cb(k_col0) + h))
    vc_spec = pl.BlockSpec((ctx_len, width), lambda b, h, i: (ctx_rb0 + b, cb(v_col0) + h))
    small = [pl.BlockSpec((8, LANES), lambda b, h, i: (0, 0)), pl.BlockSpec((LANES, sub_q), lambda b, h, i: (0, 0))]
    small_args = [lam_p, jnp.broadcast_to(subln_g[:, None], (LANES, sub_q))]
    if with_lat:
        kl_spec = pl.BlockSpec((seq, width), lambda b, h, i: (b, cb(k_col0) + h))
        vl_spec = pl.BlockSpec((seq, width), lambda b, h, i: (b, cb(v_col0) + h))
        qt = pl.BlockSpec((tq, LANES), lambda b, h, i: (i, 0))
        kt = pl.BlockSpec((seq, LANES), lambda b, h, i: (0, 0))
        in_specs = [q_spec, kc_spec, kl_spec, vc_spec, vl_spec, qt, qt, qt, kt, kt, kt] + small
        args = [p0, p0, p0, p0, p0, *rope_tabs, *rope_tabs] + small_args
    else:
        in_specs = [q_spec, kc_spec, vc_spec] + small
        args = [p0, p0, p0] + small_args
    block_bytes = 10 * seq * LANES * 4 if with_lat else 8 * ctx_len * width * 4
    return pl.pallas_call(
        functools.partial(_diff_attn_heads_kernel, heads_per_step=hps, with_lat=with_lat, lambda_init=lambda_init,
                          n_ctx=ctx_len, sub_q=sub_q),
        grid=(n_batch, n_heads // hps, n_q),
        in_specs=in_specs,
        out_specs=pl.BlockSpec((tq, width), lambda b, h, i: (b * n_q + i, h)),
        out_shape=jax.ShapeDtypeStruct((n_batch * n_q * tq, n_heads * LANES), BF16),
        scratch_shapes=[pltpu.VMEM((hps, n_keys, LANES), BF16), pltpu.VMEM((hps, LANES, n_keys), BF16)],
        compiler_params=_params(("arbitrary", "arbitrary", "arbitrary"), block_bytes),
        name="diff_attn_lat" if with_lat else "diff_attn_ctx",
    )(*args)


def _rope_tables(seq):
    half = DIFF_HEAD_DIM // 2
    pos = jnp.arange(seq)
    row, col = pos // GRID_W, pos % GRID_W
    inv_freq = ROPE_BASE ** (-jnp.arange(0, half, 2, dtype=F32) / half)
    lane = np.arange(LANES) % DIFF_HEAD_DIM
    use_col = jnp.asarray(lane >= half)
    first = jnp.asarray((lane % half) < half // 2)
    freq = inv_freq[jnp.asarray(lane % (half // 2))]
    p = jnp.where(use_col[None, :], col[:, None], row[:, None]).astype(F32)
    ang = p * freq[None, :]
    cos, sin = jnp.cos(ang), jnp.sin(ang)
    return cos, jnp.where(first[None, :], -sin, 0.0), jnp.where(first[None, :], 0.0, sin)


def _rpb_gather_kernel(rpb_ref, o_ref):
    n = o_ref.shape[1]
    k = rpb_ref.shape[1]
    colid = pl.program_id(0) * n + lax.broadcasted_iota(jnp.int32, (k, n), 1)
    j = lax.broadcasted_iota(jnp.int32, (k, n), 0)
    qc = lax.shift_right_logical(colid, int(math.log2(LANES)))
    half = lax.shift_right_logical(colid & (LANES - 1), int(math.log2(GRID_W)))
    kc = colid & (GRID_W - 1)
    sel = (j == half * LANES + jnp.clip(kc - qc + (WIN_COLS - 1), 0, 2 * WIN_COLS - 2)).astype(F32)
    o_ref[...] = jnp.dot(rpb_ref[...], sel, preferred_element_type=F32, precision=lax.Precision.HIGHEST)


def _rpb_pairs(rpb):
    nh, nr, ncol = rpb.shape
    n_slot = nr + 1
    left = jnp.pad(rpb, ((0, 0), (1, 0), (0, LANES - ncol)))
    right = jnp.pad(rpb, ((0, 0), (0, 1), (0, LANES - ncol)))
    rows = jnp.concatenate([left, right], axis=2).reshape(nh * n_slot, 2 * LANES)
    tn = 1024
    out = pl.pallas_call(
        _rpb_gather_kernel,
        grid=(GRID_W * LANES // tn,),
        in_specs=[pl.BlockSpec((nh * n_slot, 2 * LANES), lambda j: (0, 0))],
        out_specs=pl.BlockSpec((nh * n_slot, tn), lambda j: (0, j)),
        out_shape=jax.ShapeDtypeStruct((nh * n_slot, GRID_W * LANES), F32),
        compiler_params=_params(("arbitrary",), nh * n_slot * tn * 4 + 2 * LANES * tn * 4),
        name="rpb_gather",
    )(rows)
    return out.reshape(nh, n_slot, GRID_W, LANES)


def _na_block_plan(rows):
    kr = min(WIN_ROWS, rows)
    n_blk = rows // NA_Q_ROWS
    kb = np.clip(np.arange(n_blk) * NA_Q_ROWS - kr // 2, 0, rows - NA_K_ROWS)
    layouts, layout_of = [], []
    for blk in range(n_blk):
        dr = np.full((NA_Q_ROWS, NA_K_ROWS), -1, np.int64)
        for i in range(NA_Q_ROWS):
            r = blk * NA_Q_ROWS + i
            rs = int(np.clip(r - kr // 2, 0, rows - kr))
            for j in range(NA_K_ROWS):
                krow = kb[blk] + j
                if rs <= krow < rs + kr:
                    dr[i, j] = krow - r + WIN_ROWS - 1
        key = dr.tobytes()
        if key not in [l.tobytes() for l in layouts]:
            layouts.append(dr)
        layout_of.append([l.tobytes() for l in layouts].index(key))
    return kb, np.stack(layouts), np.asarray(layout_of)


def _na_bias_kernel(tp_ref, o_ref, *, layouts):
    qc = lax.broadcasted_iota(jnp.int32, (GRID_W, LANES), 0)
    lane = lax.broadcasted_iota(jnp.int32, (GRID_W, LANES), 1)
    kc = lane & (GRID_W - 1)
    left = lane < GRID_W
    col_start = jnp.clip(qc - WIN_COLS // 2, 0, GRID_W - WIN_COLS)
    col_ok = (kc >= col_start) & (kc < col_start + WIN_COLS)
    n_lay, n_q, n_k = layouts.shape
    for lay in range(n_lay):
        for i in range(n_q):
            for p in range(-(-n_k // 2)):
                d_l = int(layouts[lay, i, 2 * p])
                d_r = int(layouts[lay, i, 2 * p + 1]) if 2 * p + 1 < n_k else -1
                width = LANES if 2 * p + 1 < n_k else GRID_W
                if d_l < 0 and d_r < 0:
                    tile = jnp.full((GRID_W, LANES), -jnp.inf, F32)
                else:
                    assert d_l < 0 or d_r < 0 or d_r == d_l + 1
                    ok = col_ok
                    if d_l < 0:
                        ok = ok & jnp.logical_not(left)
                    if d_r < 0:
                        ok = ok & left
                    slot = d_r if d_r >= 0 else d_l + 1
                    tile = jnp.where(ok, tp_ref[slot] * LOG2E, -jnp.inf)
                o_ref[lay, i * GRID_W:(i + 1) * GRID_W, p * LANES:p * LANES + width] = tile[:, :width]


def _na_bias(rpb_pairs, layouts):
    nh, n_slot = rpb_pairs.shape[:2]
    n_lay = layouts.shape[0]
    tq, nkw = NA_Q_ROWS * GRID_W, NA_K_ROWS * GRID_W
    return pl.pallas_call(
        functools.partial(_na_bias_kernel, layouts=layouts),
        grid=(nh,),
        in_specs=[pl.BlockSpec((None, n_slot, GRID_W, LANES), lambda h: (h, 0, 0, 0))],
        out_specs=pl.BlockSpec((None, n_lay, tq, nkw), lambda h: (h, 0, 0, 0)),
        out_shape=jax.ShapeDtypeStruct((nh, n_lay, tq, nkw), F32),
        compiler_params=_params(("arbitrary",), n_lay * tq * nkw * 4),
        name="na_bias",
    )(rpb_pairs)


def _na_kernel(q_ref, kl_ref, vl_ref, kc_ref, vc_ref, bias_ref, o_ref, *, plan):
    tq = NA_Q_ROWS * GRID_W
    nkw = NA_K_ROWS * GRID_W
    nt = (((1,), (1,)), ((), ()))
    kc = kc_ref[...]
    vc = vc_ref[...]

    def scores(blk):
        kb, lay = plan[blk]
        q = q_ref[blk * tq:(blk + 1) * tq, :]
        ks = slice(kb * GRID_W, kb * GRID_W + nkw)
        return (lax.dot_general(q, kl_ref[ks, :], nt, preferred_element_type=F32) + bias_ref[lay],
                lax.dot_general(q, kc, nt, preferred_element_type=F32))

    def weighted_values(blk, e_w, e_c, l):
        kb, _ = plan[blk]
        ks = slice(kb * GRID_W, kb * GRID_W + nkw)
        o = (jnp.dot(e_w, vl_ref[ks, :], preferred_element_type=F32) + jnp.dot(e_c, vc, preferred_element_type=F32))
        o_ref[blk * tq:(blk + 1) * tq, :] = (o * (1.0 / l)).astype(o_ref.dtype)

    s_next = scores(0)
    pending = None
    for blk in range(len(plan)):
        s_w, s_c = s_next
        if blk + 1 < len(plan):
            s_next = scores(blk + 1)
        if pending is not None:
            weighted_values(blk - 1, *pending)
        m = jnp.maximum(jnp.max(s_w, axis=-1, keepdims=True), jnp.max(s_c, axis=-1, keepdims=True))
        e_w = jnp.exp2(s_w - m)
        e_c = jnp.exp2(s_c - m)
        l = jnp.sum(e_w, axis=-1, keepdims=True) + jnp.sum(e_c, axis=-1, keepdims=True)
        pending = (e_w.astype(BF16), e_c.astype(BF16), l)
    weighted_values(len(plan) - 1, *pending)


def _na_attn(p1, rpb, n_batch, seq, ctx_len, n_heads):
    rows = seq // GRID_W
    kb, layouts, layout_of = _na_block_plan(rows)
    bias = _na_bias(_rpb_pairs(rpb), layouts)
    n_lay = layouts.shape[0]
    tq = NA_Q_ROWS * GRID_W
    nkw = NA_K_ROWS * GRID_W
    ctx_rb0 = (n_batch * seq) // ctx_len
    plan = tuple((int(k), int(l)) for k, l in zip(kb, layout_of))
    return pl.pallas_call(
        functools.partial(_na_kernel, plan=plan),
        grid=(n_heads, n_batch),
        in_specs=[
            pl.BlockSpec((seq, LANES), lambda h, b: (b, h)),
            pl.BlockSpec((seq, LANES), lambda h, b: (b, n_heads + h)),
            pl.BlockSpec((seq, LANES), lambda h, b: (b, 2 * n_heads + h)),
            pl.BlockSpec((ctx_len, LANES), lambda h, b: (ctx_rb0 + b, n_heads + h)),
            pl.BlockSpec((ctx_len, LANES), lambda h, b: (ctx_rb0 + b, 2 * n_heads + h)),
            pl.BlockSpec((None, n_lay, tq, nkw), lambda h, b: (h, 0, 0, 0)),
        ],
        out_specs=pl.BlockSpec((seq, LANES), lambda h, b: (b, h)),
        out_shape=jax.ShapeDtypeStruct((n_batch * seq, n_heads * LANES), BF16),
        compiler_params=_params(("arbitrary", "arbitrary"), n_lay * tq * nkw * 4 + 4 * seq * LANES * 2),
        name="na_attn",
    )(p1, p1, p1, p1, p1, bias)


def _pack_rows(rows, n_rows=8):
    out = jnp.zeros((n_rows, LANES), F32)
    for r, v in enumerate(rows):
        out = out.at[r, :v.shape[0]].set(v.astype(F32))
    return out


def kernel(x, c, ctx, c_ctx, ada_w, ada_b, norm_mix_g, norm_ffn_g, final_norm_g, ffn_w1, ffn_w3, ffn_w2, ev_w_in, ev_conv_w, ev_conv_b, ev_a_log, ev_dt_bias, ev_d_skip, ev_ssm_norm_g, ev_lam_q1, ev_lam_k1, ev_lam_q2, ev_lam_k2, ev_subln_g, ev_w_out, od_w_in, od_rpb, od_w_out):
    n_batch, seq, d = x.shape
    ctx_len = ctx.shape[1]
    depth = ada_w.shape[0]
    n_lat = n_batch * seq
    n_tok = n_lat + n_batch * ctx_len

    d_ssm = ev_ssm_norm_g.shape[1]
    n_ssm_heads = ev_d_skip.shape[1]
    d_xbc = ev_conv_w.shape[2]
    d_qk = d_v = (ev_w_in.shape[2] - d_ssm - d_xbc - 2 * n_ssm_heads) // 3
    n_diff_heads = d_v // (2 * DIFF_HEAD_DIM)
    n_na_heads = od_rpb.shape[1]

    stream = (x.reshape(n_lat, d), ctx.reshape(n_batch * ctx_len, d))
    cond = jnp.zeros((COND_ROWS, d), F32).at[:n_batch].set(c).at[n_batch].set(c_ctx)
    mod = _ada_mod(cond, ada_w, ada_b).reshape(depth, COND_ROWS, 1, 6 * d)
    rope_tabs = _rope_tables(seq)
    kw = dict(seq=seq, n_batch=n_batch)
    ffn_w = (ffn_w1, ffn_w3, ffn_w2)

    for i in range(depth):
        ctx_out = i < depth - 1
        m_rows = n_tok if ctx_out else n_lat
        j = i // 2
        if i % 2 == 0:
            lambda_init = 0.8 - 0.6 * math.exp(-0.3 * i)
            w_in = ev_w_in[j]
            dt0 = d_ssm + d_xbc
            w_t = jnp.swapaxes(w_in, 0, 1)
            w_parts = [w_t[:dt0].astype(BF16), w_t[dt0 + 2 * n_ssm_heads:].astype(BF16)]
            w_dt = jnp.zeros((2 * LANES, d), F32)
            w_dt = w_dt.at[:n_ssm_heads].set(w_t[dt0:dt0 + n_ssm_heads])
            w_dt = w_dt.at[LANES:LANES + n_ssm_heads].set(w_t[dt0 + n_ssm_heads:dt0 + 2 * n_ssm_heads])
            p0, dt_raw = _proj(stream, norm_mix_g[i], mod, i, w_parts, w_dt.astype(BF16), F32, **kw)
            q0 = d_ssm + d_xbc
            k0, v0 = q0 + d_qk, q0 + 2 * d_qk

            xbc = _ssd_conv(p0, d_ssm, d_xbc, ev_conv_w[j], ev_conv_b[j], seq, ctx_len, n_lat)
            prm = jnp.stack([_pack_rows([ev_a_log[j, r], ev_dt_bias[j, r]]) for r in range(2)])
            y2 = _ssd_scan(xbc, dt_raw, prm, n_batch, seq, ctx_len, d_ssm)
            mix_ssd = _ssd_finish(y2, xbc, p0, jnp.repeat(ev_d_skip[j], SSM_HEAD_DIM), ev_ssm_norm_g[j], d_ssm)

            lam_p = _pack_rows([ev_lam_q1[j], ev_lam_k1[j], ev_lam_q2[j], ev_lam_k2[j]])
            attn_args = (p0, q0, k0, v0, n_diff_heads, rope_tabs, lam_p, ev_subln_g[j], lambda_init,
                         n_batch, seq, ctx_len)
            mix_attn = _diff_attn(*attn_args, with_lat=True)
            if ctx_out:
                mix_attn = (mix_attn, _diff_attn(*attn_args, with_lat=False))
            lhs = [mix_ssd, mix_attn]
            w_out = ev_w_out[j].astype(BF16)
        else:
            d_na = n_na_heads * NA_HEAD_DIM
            q_scale = jnp.where(jnp.arange(3 * d_na) < d_na, NA_HEAD_DIM ** -0.5 * LOG2E, 1.0).astype(F32)
            p1 = _proj(stream, norm_mix_g[i], mod, i, od_w_in[j].astype(BF16), None, BF16, col_scale=q_scale, **kw)
            assert not ctx_out, "context-query neighbourhood layers are not needed at this depth"
            lhs = [_na_attn(p1, od_rpb[j], n_batch, seq, ctx_len, n_na_heads)]
            w_out = od_w_out[j].astype(BF16)
        stream = _out_proj(lhs, w_out, stream, mod, i, 2, m_rows, **kw)
        stream = _ffn(stream, norm_ffn_g[i], mod, i, *ffn_w, final_norm_g, not ctx_out, m_rows, **kw)
    return stream.reshape(n_batch, seq, d)
```

```python
import functools
import math

import jax
import jax.numpy as jnp
import numpy as np
from jax import lax
from jax.experimental import pallas as pl
from jax.experimental.pallas import tpu as pltpu

F32 = jnp.float32
BF16 = jnp.bfloat16

GRID_W = 64
SSM_HEAD_DIM = 64
SSM_GROUPS = 4
SSM_STATE = 128
SSM_CONV = 5
SSM_CHUNK = 128
DIFF_HEAD_DIM = 64
NA_HEAD_DIM = 128
WIN_ROWS = 8
WIN_COLS = 16
ROPE_BASE = 10000.0
NORM_EPS = 1e-6
LOG2E = math.log2(math.e)

LANES = 128
V7X_VMEM_BYTES = 64 * 1024 * 1024
VMEM_HEADROOM_BYTES = 3 * 1024 * 1024

NA_Q_ROWS = 4
NA_K_ROWS = NA_Q_ROWS + WIN_ROWS - 1
COND_ROWS = 16


def _vmem_limit(block_bytes):
    return int(min(V7X_VMEM_BYTES - VMEM_HEADROOM_BYTES, max(32 * 1024 * 1024, 2 * block_bytes + 16 * 1024 * 1024)))


def _params(semantics, block_bytes):
    return pltpu.CompilerParams(dimension_semantics=semantics, vmem_limit_bytes=_vmem_limit(block_bytes))


def _silu(v):
    return v * jax.nn.sigmoid(v)


def _mod_row(i, tm, seq, n_batch):
    return jnp.minimum((i * tm) // seq, n_batch)


def _ada_kernel(cond_ref, w_ref, b_ref, o_ref):
    s = _silu(cond_ref[...]).astype(BF16)
    o_ref[...] = jnp.dot(s, w_ref[...].astype(BF16), preferred_element_type=F32) + b_ref[...]


def _ada_mod(cond, ada_w, ada_b, tn=1024):
    depth, d, n = ada_w.shape
    return pl.pallas_call(
        _ada_kernel,
        grid=(depth, n // tn),
        in_specs=[
            pl.BlockSpec((COND_ROWS, d), lambda l, j: (0, 0)),
            pl.BlockSpec((None, d, tn), lambda l, j: (l, 0, j)),
            pl.BlockSpec((None, 1, tn), lambda l, j: (l, 0, j)),
        ],
        out_specs=pl.BlockSpec((None, COND_ROWS, tn), lambda l, j: (l, 0, j)),
        out_shape=jax.ShapeDtypeStruct((depth, COND_ROWS, n), F32),
        compiler_params=_params(("arbitrary", "arbitrary"), d * tn * 4),
        name="ada_mod",
    )(cond, ada_w, ada_b.reshape(depth, 1, n))


def _norm_mod_pipeline(x_ref, g_ref, mod_ref, shift_idx, scale_idx, n_chunks, consume):
    tm, d = x_ref.shape
    shift = mod_ref[:, shift_idx * d:(shift_idx + 1) * d]
    gain = g_ref[...] * (1.0 + mod_ref[:, scale_idx * d:(scale_idx + 1) * d])
    chunk = tm // n_chunks

    def norm_rows(c):
        x = x_ref[c * chunk:(c + 1) * chunk, :]
        inv = lax.rsqrt(jnp.mean(x * x, axis=-1, keepdims=True) + NORM_EPS)
        return (x * inv * gain + shift).astype(BF16)

    h_next = norm_rows(0)
    for c in range(n_chunks):
        h = h_next
        if c + 1 < n_chunks:
            h_next = norm_rows(c + 1)
        consume(slice(c * chunk, (c + 1) * chunk), h)


def _proj_kernel(*refs, n_x, n_w, tiles_per_part, n_lat_tiles, has_aux, has_scale, first_step_chunks=4):
    rest = list(refs)
    x_refs = [rest.pop(0) for _ in range(n_x)]
    g_ref, mod_ref = rest.pop(0), rest.pop(0)
    w_refs = [rest.pop(0) for _ in range(max(n_w, 1))]
    cs_ref = rest.pop(0) if has_scale else None
    if has_aux:
        waux_ref, o_ref, oaux_ref, h_ref = rest
    else:
        o_ref, h_ref = rest
    dims = (((1,), (1,)), ((), ())) if n_w else (((1,), (0,)), ((), ()))
    j = pl.program_id(1)
    tm, d = x_refs[0].shape

    def emit(h, rows, w_ref):
        acc = lax.dot_general(h, w_ref[...].astype(BF16), dims, preferred_element_type=F32)
        if has_scale:
            acc = acc * cs_ref[...]
        o_ref[rows, :] = acc.astype(o_ref.dtype)

    def first_tile(x_ref):
        def consume(rows, h):
            h_ref[rows, :] = h
            emit(h, rows, w_refs[0])
            if has_aux:
                oaux_ref[rows, :] = lax.dot_general(h, waux_ref[...], dims, preferred_element_type=F32)

        _norm_mod_pipeline(x_ref, g_ref, mod_ref, 0, 1, first_step_chunks, consume)

    if n_x == 1:
        pl.when(j == 0)(functools.partial(first_tile, x_refs[0]))
    else:
        is_lat = pl.program_id(0) < n_lat_tiles
        pl.when(jnp.logical_and(j == 0, is_lat))(functools.partial(first_tile, x_refs[0]))
        pl.when(jnp.logical_and(j == 0, jnp.logical_not(is_lat)))(functools.partial(first_tile, x_refs[1]))

    for p in range(max(n_w, 1)):
        lo = max(p * tiles_per_part, 1)
        in_part = (j >= lo) if n_w == 0 else jnp.logical_and(j >= lo, j < (p + 1) * tiles_per_part)
        pl.when(in_part)(lambda p=p: emit(h_ref[...], slice(None), w_refs[p]))


def _proj(x, g, mod, layer, w, w_aux, out_dtype, seq, n_batch, col_scale=None, tm=1024, tn=1024):
    parts = list(w) if isinstance(w, (list, tuple)) else None
    if parts is None:
        d, n = w.shape
        n_w, tiles_per_part = 0, 0
    else:
        d = parts[0].shape[1]
        n_w, tiles_per_part = len(parts), parts[0].shape[0] // tn
        assert all(p.shape == (tiles_per_part * tn, d) for p in parts)
        n = n_w * tiles_per_part * tn
    m = sum(a.shape[0] for a in x) if isinstance(x, tuple) else x.shape[0]
    n_lat_tiles = (n_batch * seq) // tm
    has_aux = w_aux is not None
    has_scale = col_scale is not None
    mod_spec = pl.BlockSpec((None, None, 1, mod.shape[-1]),
                            lambda i, j: (layer, _mod_row(i, tm, seq, n_batch), 0, 0))
    in_specs, args = _row_operand(x, tm, n_lat_tiles, d, lambda j: 0, single_ctx_buffer=True)
    n_x = len(args)
    in_specs += [pl.BlockSpec((1, d), lambda i, j: (0, 0)), mod_spec]
    args += [g.reshape(1, d), mod]
    if parts is None:
        in_specs.append(pl.BlockSpec((d, tn), lambda i, j: (0, j)))
        args.append(w)
    else:
        for p, part in enumerate(parts):
            in_specs.append(pl.BlockSpec(
                (tn, d), lambda i, j, p=p: (jnp.clip(j - p * tiles_per_part, 0, tiles_per_part - 1), 0)))
            args.append(part)
    out_specs = [pl.BlockSpec((tm, tn), lambda i, j: (i, j))]
    out_shape = [jax.ShapeDtypeStruct((m, n), out_dtype)]
    if has_scale:
        in_specs.append(pl.BlockSpec((1, tn), lambda i, j: (0, j)))
        args.append(col_scale.reshape(1, n))
    if has_aux:
        na = w_aux.shape[0] if parts is not None else w_aux.shape[1]
        in_specs.append(pl.BlockSpec(w_aux.shape, lambda i, j: (0, 0)))
        out_specs.append(pl.BlockSpec((tm, na), lambda i, j: (i, 0)))
        out_shape.append(jax.ShapeDtypeStruct((m, na), F32))
        args.append(w_aux)
    w_itemsize = (parts[0] if parts is not None else w).dtype.itemsize
    block_bytes = n_x * tm * d * 4 + max(n_w, 1) * d * tn * w_itemsize + tm * tn * 4 + tm * d
    outs = pl.pallas_call(
        functools.partial(_proj_kernel, n_x=n_x, n_w=n_w, tiles_per_part=tiles_per_part, n_lat_tiles=n_lat_tiles,
                          has_aux=has_aux, has_scale=has_scale),
        grid=(m // tm, n // tn),
        in_specs=in_specs,
        out_specs=out_specs,
        out_shape=out_shape,
        scratch_shapes=[pltpu.VMEM((tm, d), BF16)],
        compiler_params=_params(("arbitrary", "arbitrary"), block_bytes),
        name="proj",
    )(*args)
    return outs if has_aux else outs[0]


def _row_operand(a, tm, n_lat_tiles, width, col_of_j, single_ctx_buffer=False):
    if not isinstance(a, tuple):
        return [pl.BlockSpec((tm, width), lambda i, j: (i, col_of_j(j)))], [a]
    lat, ctx = a
    assert ctx.shape[0] % tm == 0 and lat.shape[0] == n_lat_tiles * tm

    def lat_index(i, j):
        return jnp.minimum(i, n_lat_tiles - 1), jnp.where(i < n_lat_tiles, col_of_j(j), 0)

    def ctx_index(i, j):
        return jnp.maximum(i - n_lat_tiles, 0), jnp.where(i < n_lat_tiles, 0, col_of_j(j))

    ctx_mode = dict(pipeline_mode=pl.Buffered(1)) if single_ctx_buffer else {}
    return [pl.BlockSpec((tm, width), lat_index), pl.BlockSpec((tm, width), ctx_index, **ctx_mode)], [lat, ctx]


def _pick_rows(refs, n_lat_tiles):
    if len(refs) == 1:
        return refs[0][...]
    return jnp.where(pl.program_id(0) < n_lat_tiles, refs[0][...], refs[1][...])


def _out_proj_kernel(*refs, arity, n_lat_tiles):
    refs = list(refs)
    n_lhs = len(arity) - 1
    groups = [[refs.pop(0) for _ in range(n)] for n in arity[:-1]]
    ws = [refs.pop(0) for _ in range(n_lhs)]
    res = [refs.pop(0) for _ in range(arity[-1])]
    gate_ref, o_ref = refs
    acc = None
    for grp, w_ref in zip(groups, ws):
        part = jnp.dot(_pick_rows(grp, n_lat_tiles), w_ref[...].astype(BF16), preferred_element_type=F32)
        acc = part if acc is None else acc + part
    o_ref[...] = _pick_rows(res, n_lat_tiles) + gate_ref[...] * acc


def _out_proj(lhs_list, w, res, mod, layer, gate_idx, m_rows, seq, n_batch, tm=512, tn=2048):
    d = w.shape[1]
    n_lat_tiles = (n_batch * seq) // tm
    in_specs, args, arity, widths = [], [], [], []
    for a in lhs_list:
        kk = (a[0] if isinstance(a, tuple) else a).shape[1]
        sp, ar = _row_operand(a, tm, n_lat_tiles, kk, lambda j: 0)
        in_specs += sp
        args += ar
        arity.append(len(ar))
        widths.append(kk)
    row0 = 0
    for kk in widths:
        in_specs.append(pl.BlockSpec((kk, tn), lambda i, j, rb=row0 // kk: (rb, j)))
        args.append(w)
        row0 += kk
    sp, ar = _row_operand(res, tm, n_lat_tiles, tn, lambda j: j)
    in_specs += sp
    args += ar
    arity.append(len(ar))
    in_specs.append(pl.BlockSpec((None, None, 1, tn),
                                 lambda i, j: (layer, _mod_row(i, tm, seq, n_batch), 0, gate_idx * (d // tn) + j)))
    args.append(mod)
    k_total = sum(widths)
    block_bytes = 2 * tm * k_total * 2 + k_total * tn * w.dtype.itemsize + 3 * tm * tn * 4
    return pl.pallas_call(
        functools.partial(_out_proj_kernel, arity=tuple(arity), n_lat_tiles=n_lat_tiles),
        grid=(m_rows // tm, d // tn),
        in_specs=in_specs,
        out_specs=pl.BlockSpec((tm, tn), lambda i, j: (i, j)),
        out_shape=jax.ShapeDtypeStruct((m_rows, d), F32),
        compiler_params=_params(("arbitrary", "arbitrary"), block_bytes),
        name="out_proj",
    )(*args)


def _ffn_kernel(x_ref, g_ref, mod_ref, w1_ref, w3_ref, w2_ref, fg_ref, o_ref, h_ref, *, final_norm, rows,
                first_step_chunks):
    j = pl.program_id(1)
    tm, d = x_ref.shape

    def swiglu(h, w1, w3, w2):
        a = jnp.dot(h, w1, preferred_element_type=F32)
        b = jnp.dot(h, w3, preferred_element_type=F32)
        return jnp.dot((_silu(a) * b).astype(BF16), w2, preferred_element_type=F32)

    @pl.when(j == 0)
    def _():
        w1, w3, w2 = (w_ref[...].astype(BF16) for w_ref in (w1_ref, w3_ref, w2_ref))

        def consume(rows, h):
            h_ref[rows, :] = h
            o_ref[rows, :] = swiglu(h, w1, w3, w2)

        _norm_mod_pipeline(x_ref, g_ref, mod_ref, 3, 4, first_step_chunks, consume)

    @pl.when(j > 0)
    def _():
        o_ref[...] += swiglu(h_ref[...], w1_ref[...].astype(BF16), w3_ref[...].astype(BF16),
                             w2_ref[...].astype(BF16))

    @pl.when(j == pl.num_programs(1) - 1)
    def _():
        gate = mod_ref[:, 5 * d:6 * d]
        fg = fg_ref[...]

        def body(r, carry):
            sl = pl.ds(pl.multiple_of(r * rows, rows), rows)
            y = x_ref[sl, :] + gate * o_ref[sl, :]
            if final_norm:
                y = y * lax.rsqrt(jnp.mean(y * y, axis=-1, keepdims=True) + NORM_EPS) * fg
            o_ref[sl, :] = y
            return carry

        lax.fori_loop(0, x_ref.shape[0] // rows, body, 0)


def _ffn(x, g, mod, layer, w1, w3, w2, final_g, final_norm, m_rows, seq, n_batch, tm=1024, tf=256):
    d = x.shape[1]
    ff = w1.shape[2]
    block_bytes = 2 * tm * d * 4 + d * tf * sum(w.dtype.itemsize for w in (w1, w3, w2)) + tm * d
    return pl.pallas_call(
        functools.partial(_ffn_kernel, final_norm=final_norm, rows=128, first_step_chunks=4),
        grid=(m_rows // tm, ff // tf),
        in_specs=[
            pl.BlockSpec((tm, d), lambda i, j: (i, 0)),
            pl.BlockSpec((1, d), lambda i, j: (0, 0)),
            pl.BlockSpec((None, None, 1, mod.shape[-1]),
                         lambda i, j: (layer, _mod_row(i, tm, seq, n_batch), 0, 0)),
            pl.BlockSpec((None, d, tf), lambda i, j: (layer, 0, j)),
            pl.BlockSpec((None, d, tf), lambda i, j: (layer, 0, j)),
            pl.BlockSpec((None, tf, d), lambda i, j: (layer, j, 0)),
            pl.BlockSpec((1, d), lambda i, j: (0, 0)),
        ],
        out_specs=pl.BlockSpec((tm, d), lambda i, j: (i, 0)),
        out_shape=jax.ShapeDtypeStruct((m_rows, d), F32),
        scratch_shapes=[pltpu.VMEM((tm, d), BF16)],
        compiler_params=_params(("arbitrary", "arbitrary"), block_bytes),
        name="ffn",
    )(x, g.reshape(1, d), mod, w1, w3, w2, final_g.reshape(1, d))


def _conv_kernel(prev_ref, cur_ref, next_ref, w_ref, b_ref, o_ref, ext_ref, *, tm, halo, seq, ctx_len, n_lat_tiles):
    i = pl.program_id(0)
    half = SSM_CONV // 2
    ext_ref[halo:halo + tm, :] = cur_ref[...]

    def taps(mask_of):
        acc = jnp.zeros(cur_ref.shape, F32) + b_ref[...]
        for t in range(SSM_CONV):
            tap = ext_ref[halo - half + t:halo - half + t + tm, :]
            acc = acc + mask_of(t, tap) * w_ref[t:t + 1, :]
        o_ref[...] = _silu(acc)

    @pl.when(i < n_lat_tiles)
    def _():
        prev_in_seg = ((i * tm) & (seq - 1)) != 0
        next_in_seg = (((i + 1) * tm) & (seq - 1)) != 0
        ext_ref[0:halo, :] = jnp.where(prev_in_seg, prev_ref[...], 0.0)
        ext_ref[halo + tm:, :] = jnp.where(next_in_seg, next_ref[...], 0.0)
        taps(lambda t, tap: tap)

    @pl.when(i >= n_lat_tiles)
    def _():
        ext_ref[0:halo, :] = prev_ref[...]
        ext_ref[halo + tm:, :] = next_ref[...]
        pos = (i * tm + lax.broadcasted_iota(jnp.int32, (tm, 1), 0)) & (ctx_len - 1)

        def masked(t, tap):
            src = pos + (t - half)
            return jnp.where((src >= 0) & (src < ctx_len), tap, 0.0)

        taps(masked)


def _ssd_conv(p0, col0, width, conv_w, conv_b, seq, ctx_len, n_lat_rows, tm=1024, tc=1024, halo=8):
    m = p0.shape[0]
    assert seq % tm == 0 and tm % ctx_len == 0 and (seq & (seq - 1)) == 0 and (ctx_len & (ctx_len - 1)) == 0
    cb0 = col0 // tc
    hb = tm // halo
    last_hb = m // halo - 1
    wpad = jnp.zeros((8, width), F32).at[:SSM_CONV].set(conv_w)
    return pl.pallas_call(
        functools.partial(_conv_kernel, tm=tm, halo=halo, seq=seq, ctx_len=ctx_len, n_lat_tiles=n_lat_rows // tm),
        grid=(m // tm, width // tc),
        in_specs=[
            pl.BlockSpec((halo, tc), lambda i, j: (jnp.maximum(i * hb - 1, 0), cb0 + j)),
            pl.BlockSpec((tm, tc), lambda i, j: (i, cb0 + j)),
            pl.BlockSpec((halo, tc), lambda i, j: (jnp.minimum((i + 1) * hb, last_hb), cb0 + j)),
            pl.BlockSpec((8, tc), lambda i, j: (0, j)),
            pl.BlockSpec((1, tc), lambda i, j: (0, j)),
        ],
        out_specs=pl.BlockSpec((tm, tc), lambda i, j: (i, j)),
        out_shape=jax.ShapeDtypeStruct((m, width), F32),
        scratch_shapes=[pltpu.VMEM((tm + 2 * halo, tc), F32)],
        compiler_params=_params(("arbitrary", "arbitrary"), 3 * tm * tc * 4),
        name="ssd_conv",
    )(p0, p0, p0, wpad, conv_b.reshape(1, width))


def _ssd_scan_kernel(xs_f, bm_f, cm_f, dt_f, xs_b, bm_b, cm_b, dt_b, prm_ref, yf_ref, yb_ref, state_ref, *, n_heads):
    @pl.when(pl.program_id(1) == 0)
    def _():
        state_ref[...] = jnp.zeros_like(state_ref)

    fwd = _ssd_chunk(xs_f, bm_f, cm_f, dt_f, prm_ref.at[0], yf_ref, state_ref.at[0], backward=False, n_heads=n_heads)
    bwd = _ssd_chunk(xs_b, bm_b, cm_b, dt_b, prm_ref.at[1], yb_ref, state_ref.at[1], backward=True, n_heads=n_heads)
    for g in range(SSM_GROUPS):
        fwd(g)
        bwd(g)


def _ssd_chunk(xs_ref, bm_ref, cm_ref, dt_ref, prm_ref, y_ref, state_ref, *, backward, n_heads):
    hp = SSM_HEAD_DIM
    rep = n_heads // SSM_GROUPS
    lc = SSM_CHUNK
    assert 2 * hp == LANES and rep % 2 == 0 and SSM_STATE == LANES and lc == LANES

    dt_in = dt_ref[...] + prm_ref[1:2, :]
    dt = jnp.maximum(dt_in, 0.0) + jnp.log1p(jnp.exp(-jnp.abs(dt_in)))
    dta = dt * (-jnp.exp(prm_ref[0:1, :]) * LOG2E)
    row = lax.broadcasted_iota(jnp.int32, (lc, lc), 0)
    col = lax.broadcasted_iota(jnp.int32, (lc, lc), 1)
    causal = (row <= col) if backward else (row >= col)
    a_cum = jnp.dot(causal.astype(F32), dta, preferred_element_type=F32, precision=lax.Precision.HIGHEST)
    a_tot = a_cum[0:1, :] if backward else a_cum[lc - 1:lc, :]
    w_end = dt * jnp.exp2(a_tot - a_cum)
    a_cum_t, dt_t, w_end_t = a_cum.T, dt.T, w_end.T
    lo = col < hp

    def do_group(g):
        gs = slice(g * SSM_STATE, (g + 1) * SSM_STATE)
        b_f = bm_ref[:, gs]
        c_g = cm_ref[:, gs].astype(BF16)
        cb = lax.dot_general(c_g, b_f.astype(BF16), (((1,), (1,)), ((), ())), preferred_element_type=F32)
        b_t = b_f.T
        cols_g = slice(g * rep * hp, (g + 1) * rep * hp)
        y_off = jnp.dot(c_g, state_ref[:, cols_g].astype(BF16), preferred_element_type=F32)
        for pr in range(rep // 2):
            cols = slice(g * rep * hp + pr * LANES, g * rep * hp + (pr + 1) * LANES)
            xs2 = xs_ref[:, cols].astype(BF16)
            y_d, s_n, e_a = [], [], []
            for h in (g * rep + 2 * pr, g * rep + 2 * pr + 1):
                a_col = jnp.broadcast_to(a_cum[:, h:h + 1], (lc, lc))
                decay = jnp.exp2(jnp.where(causal, a_col - a_cum_t[h:h + 1, :], -jnp.inf))
                m = (cb * decay * dt_t[h:h + 1, :]).astype(BF16)
                y_d.append(jnp.dot(m, xs2, preferred_element_type=F32))
                s_n.append(jnp.dot((b_t * w_end_t[h:h + 1, :]).astype(BF16), xs2, preferred_element_type=F32))
                e_a.append(jnp.exp2(a_col))
            h0 = g * rep + 2 * pr
            y_ref[:, cols] = (jnp.where(lo, y_d[0], y_d[1])
                              + y_off[:, pr * LANES:(pr + 1) * LANES] * jnp.where(lo, e_a[0], e_a[1]))
            chunk_decay = jnp.where(lo[0:1, :], jnp.exp2(a_tot[:, h0:h0 + 1]), jnp.exp2(a_tot[:, h0 + 1:h0 + 2]))
            state_ref[:, cols] = state_ref[:, cols] * chunk_decay + jnp.where(lo, s_n[0], s_n[1])

    return do_group


def _ssd_scan(xbc, dt_raw, prm, n_batch, seq, ctx_len, d_ssm):
    m = xbc.shape[0]
    lc = SSM_CHUNK
    n_heads = d_ssm // SSM_HEAD_DIM
    nc_ctx, nc_lat = ctx_len // lc, seq // lc
    ctx_blk0 = n_batch * nc_lat
    gn = SSM_GROUPS * SSM_STATE

    def row_blk(b, s, backward):
        ctx_c = nc_ctx - 1 - s if backward else s
        lat_c = nc_lat - 1 - (s - nc_ctx) if backward else s - nc_ctx
        return jnp.where(s < nc_ctx, ctx_blk0 + b * nc_ctx + ctx_c, b * nc_lat + lat_c)

    def chunk_specs(backward):
        r = int(backward)
        return [
            pl.BlockSpec((lc, d_ssm), lambda b, s: (row_blk(b, s, backward), 0)),
            pl.BlockSpec((lc, gn), lambda b, s: (row_blk(b, s, backward), d_ssm // gn)),
            pl.BlockSpec((lc, gn), lambda b, s: (row_blk(b, s, backward), d_ssm // gn + 1)),
            pl.BlockSpec((lc, LANES), lambda b, s: (row_blk(b, s, backward), r)),
        ]

    return pl.pallas_call(
        functools.partial(_ssd_scan_kernel, n_heads=n_heads),
        grid=(n_batch, nc_ctx + nc_lat),
        in_specs=chunk_specs(False) + chunk_specs(True) + [pl.BlockSpec((2, 8, LANES), lambda b, s: (0, 0, 0))],
        out_specs=[pl.BlockSpec((lc, d_ssm), lambda b, s: (row_blk(b, s, False), 0)),
                   pl.BlockSpec((lc, d_ssm), lambda b, s: (row_blk(b, s, True), 0))],
        out_shape=[jax.ShapeDtypeStruct((m, d_ssm), F32), jax.ShapeDtypeStruct((m, d_ssm), F32)],
        scratch_shapes=[pltpu.VMEM((2, SSM_STATE, d_ssm), F32)],
        compiler_params=_params(("arbitrary", "arbitrary"), 8 * lc * d_ssm * 4),
        name="ssd_scan",
    )(xbc, xbc, xbc, dt_raw, xbc, xbc, xbc, dt_raw, prm)


def _ssd_finish_kernel(yf_ref, yb_ref, xs_ref, z_ref, dskip_ref, g_ref, o_ref, *, group_width):
    y = (yf_ref[...] + yb_ref[...] + dskip_ref[...] * xs_ref[...]) * _silu(z_ref[...])
    for g in range(y.shape[1] // group_width):
        sl = slice(g * group_width, (g + 1) * group_width)
        v = y[:, sl]
        vn = v * lax.rsqrt(jnp.mean(v * v, axis=-1, keepdims=True) + NORM_EPS) * g_ref[:, sl]
        o_ref[:, sl] = vn.astype(o_ref.dtype)


def _ssd_finish(y2, xbc, p0, d_skip_lanes, norm_g, d_ssm, tm=512):
    m = xbc.shape[0]
    return pl.pallas_call(
        functools.partial(_ssd_finish_kernel, group_width=d_ssm // SSM_GROUPS),
        grid=(m // tm,),
        in_specs=[
            pl.BlockSpec((tm, d_ssm), lambda i: (i, 0)),
            pl.BlockSpec((tm, d_ssm), lambda i: (i, 0)),
            pl.BlockSpec((tm, d_ssm), lambda i: (i, 0)),
            pl.BlockSpec((tm, d_ssm), lambda i: (i, 0)),
            pl.BlockSpec((1, d_ssm), lambda i: (0, 0)),
            pl.BlockSpec((1, d_ssm), lambda i: (0, 0)),
        ],
        out_specs=pl.BlockSpec((tm, d_ssm), lambda i: (i, 0)),
        out_shape=jax.ShapeDtypeStruct((m, d_ssm), BF16),
        compiler_params=_params(("arbitrary",), 5 * tm * d_ssm * 4),
        name="ssd_finish",
    )(*y2, xbc, p0, d_skip_lanes.reshape(1, d_ssm), norm_g.reshape(1, d_ssm))


def _rope(x, cos, sin_up, sin_dn):
    quarter = DIFF_HEAD_DIM // 4
    return (x * cos + pltpu.roll(x, LANES - quarter, axis=1) * sin_up + pltpu.roll(x, quarter, axis=1) * sin_dn)


def _diff_attn_heads_kernel(*refs, heads_per_step, with_lat, **kw):
    n_head_refs = 5 if with_lat else 3
    *ins, o_ref, k_scr, vt_scr = refs
    for h in range(heads_per_step):
        cs = slice(h * LANES, (h + 1) * LANES)
        head_ins = [r.at[:, cs] for r in ins[:n_head_refs]] + list(ins[n_head_refs:])
        _diff_attn_kernel(*head_ins, o_ref.at[:, cs], k_scr.at[h], vt_scr.at[h], with_lat=with_lat, **kw)


def _diff_attn_kernel(*refs, with_lat, lambda_init, n_ctx, sub_q, key_chunk=768):
    if with_lat:
        (q_ref, kc_ref, kl_ref, vc_ref, vl_ref, qcos_ref, qsu_ref, qsd_ref, kcos_ref, ksu_ref, ksd_ref,
         lam_ref, g_ref, o_ref, k_scr, vt_scr) = refs
    else:
        q_ref, kc_ref, vc_ref, lam_ref, g_ref, o_ref, k_scr, vt_scr = refs

    @pl.when(pl.program_id(2) == 0)
    def _():
        k_scr[0:n_ctx, :] = kc_ref[...].astype(BF16)
        vt_scr[:, 0:n_ctx] = vc_ref[...].T.astype(BF16)
        if with_lat:
            k_scr[n_ctx:, :] = _rope(kl_ref[...], kcos_ref[...], ksu_ref[...], ksd_ref[...]).astype(BF16)
            vt_scr[:, n_ctx:] = vl_ref[...].T.astype(BF16)

    lam_p = lam_ref[...]
    lam = (jnp.exp(jnp.sum(lam_p[0:1, :] * lam_p[1:2, :], axis=-1, keepdims=True))
           - jnp.exp(jnp.sum(lam_p[2:3, :] * lam_p[3:4, :], axis=-1, keepdims=True)) + lambda_init)
    first = lax.broadcasted_iota(jnp.int32, (LANES, sub_q), 0) < DIFF_HEAD_DIM
    n_sub = q_ref.shape[0] // sub_q

    def scores(t):
        rows = slice(t * sub_q, (t + 1) * sub_q)
        q = q_ref[rows, :]
        if with_lat:
            q = _rope(q, qcos_ref[rows, :], qsu_ref[rows, :], qsd_ref[rows, :])
        qt = (q * (DIFF_HEAD_DIM ** -0.5 * LOG2E)).T
        q2t = jnp.concatenate([jnp.where(first, qt, 0.0), jnp.where(first, 0.0, qt)], axis=1).astype(BF16)
        blocks = [jnp.dot(k_scr[k0:k1, :], q2t, preferred_element_type=F32) for k0, k1 in key_chunks]
        return blocks, [jnp.max(b, axis=0, keepdims=True) for b in blocks]

    n_keys = k_scr.shape[0]
    key_chunks = [(k0, min(k0 + key_chunk, n_keys)) for k0 in range(0, n_keys, key_chunk)]
    s_next = scores(0)
    for t in range(n_sub):
        rows = slice(t * sub_q, (t + 1) * sub_q)
        s_blocks, maxes = s_next
        if t + 1 < n_sub:
            s_next = scores(t + 1)
        m = functools.reduce(jnp.maximum, maxes)
        ovt, l = None, None
        for (k0, k1), s in zip(key_chunks, s_blocks):
            e = jnp.exp2(s - m)
            part = jnp.dot(vt_scr[:, k0:k1], e.astype(BF16), preferred_element_type=F32)
            part_l = jnp.sum(e, axis=0, keepdims=True)
            ovt, l = (part, part_l) if ovt is None else (ovt + part, l + part_l)
        ovt = ovt * (1.0 / l)
        ot = ovt[:, :sub_q] - lam * ovt[:, sub_q:]
        ot = ot * lax.rsqrt(jnp.mean(ot * ot, axis=0, keepdims=True) + NORM_EPS) * g_ref[...] * (1.0 - lambda_init)
        o_ref[rows, :] = ot.T.astype(o_ref.dtype)


def _diff_attn(p0, q_col0, k_col0, v_col0, n_heads, rope_tabs, lam_p, subln_g, lambda_init,
               n_batch, seq, ctx_len, with_lat, tq=2048, sub_q=256):
    ctx_rb0 = (n_batch * seq) // ctx_len
    n_keys = ctx_len + (seq if with_lat else 0)
    hps = 1 if with_lat else n_heads
    width = hps * LANES
    cb = lambda c0: c0 // width
    if with_lat:
        n_q = seq // tq
        q_spec = pl.BlockSpec((tq, width), lambda b, h, i: (b * n_q + i, cb(q_col0) + h))
    else:
        tq = ctx_len
        n_q = 1
        q_spec = pl.BlockSpec((tq, width), lambda b, h, i: (ctx_rb0 + b, cb(q_col0) + h))
    kc_spec = pl.BlockSpec((ctx_len, width), lambda b, h, i: (ctx_rb0 + b, cb(k_col0) + h))
    vc_spec = pl.BlockSpec((ctx_len, width), lambda b, h, i: (ctx_rb0 + b, cb(v_col0) + h))
    small = [pl.BlockSpec((8, LANES), lambda b, h, i: (0, 0)), pl.BlockSpec((LANES, sub_q), lambda b, h, i: (0, 0))]
    small_args = [lam_p, jnp.broadcast_to(subln_g[:, None], (LANES, sub_q))]
    if with_lat:
        kl_spec = pl.BlockSpec((seq, width), lambda b, h, i: (b, cb(k_col0) + h))
        vl_spec = pl.BlockSpec((seq, width), lambda b, h, i: (b, cb(v_col0) + h))
        qt = pl.BlockSpec((tq, LANES), lambda b, h, i: (i, 0))
        kt = pl.BlockSpec((seq, LANES), lambda b, h, i: (0, 0))
        in_specs = [q_spec, kc_spec, kl_spec, vc_spec, vl_spec, qt, qt, qt, kt, kt, kt] + small
        args = [p0, p0, p0, p0, p0, *rope_tabs, *rope_tabs] + small_args
    else:
        in_specs = [q_spec, kc_spec, vc_spec] + small
        args = [p0, p0, p0] + small_args
    block_bytes = 10 * seq * LANES * 4 if with_lat else 8 * ctx_len * width * 4
    return pl.pallas_call(
        functools.partial(_diff_attn_heads_kernel, heads_per_step=hps, with_lat=with_lat, lambda_init=lambda_init,
                          n_ctx=ctx_len, sub_q=sub_q),
        grid=(n_batch, n_heads // hps, n_q),
        in_specs=in_specs,
        out_specs=pl.BlockSpec((tq, width), lambda b, h, i: (b * n_q + i, h)),
        out_shape=jax.ShapeDtypeStruct((n_batch * n_q * tq, n_heads * LANES), BF16),
        scratch_shapes=[pltpu.VMEM((hps, n_keys, LANES), BF16), pltpu.VMEM((hps, LANES, n_keys), BF16)],
        compiler_params=_params(("arbitrary", "arbitrary", "arbitrary"), block_bytes),
        name="diff_attn_lat" if with_lat else "diff_attn_ctx",
    )(*args)


def _rope_tables(seq):
    half = DIFF_HEAD_DIM // 2
    pos = jnp.arange(seq)
    row, col = pos // GRID_W, pos % GRID_W
    inv_freq = ROPE_BASE ** (-jnp.arange(0, half, 2, dtype=F32) / half)
    lane = np.arange(LANES) % DIFF_HEAD_DIM
    use_col = jnp.asarray(lane >= half)
    first = jnp.asarray((lane % half) < half // 2)
    freq = inv_freq[jnp.asarray(lane % (half // 2))]
    p = jnp.where(use_col[None, :], col[:, None], row[:, None]).astype(F32)
    ang = p * freq[None, :]
    cos, sin = jnp.cos(ang), jnp.sin(ang)
    return cos, jnp.where(first[None, :], -sin, 0.0), jnp.where(first[None, :], 0.0, sin)


def _rpb_gather_kernel(rpb_ref, o_ref):
    n = o_ref.shape[1]
    k = rpb_ref.shape[1]
    colid = pl.program_id(0) * n + lax.broadcasted_iota(jnp.int32, (k, n), 1)
    j = lax.broadcasted_iota(jnp.int32, (k, n), 0)
    qc = lax.shift_right_logical(colid, int(math.log2(LANES)))
    half = lax.shift_right_logical(colid & (LANES - 1), int(math.log2(GRID_W)))
    kc = colid & (GRID_W - 1)
    sel = (j == half * LANES + jnp.clip(kc - qc + (WIN_COLS - 1), 0, 2 * WIN_COLS - 2)).astype(F32)
    o_ref[...] = jnp.dot(rpb_ref[...], sel, preferred_element_type=F32, precision=lax.Precision.HIGHEST)


def _rpb_pairs(rpb):
    nh, nr, ncol = rpb.shape
    n_slot = nr + 1
    left = jnp.pad(rpb, ((0, 0), (1, 0), (0, LANES - ncol)))
    right = jnp.pad(rpb, ((0, 0), (0, 1), (0, LANES - ncol)))
    rows = jnp.concatenate([left, right], axis=2).reshape(nh * n_slot, 2 * LANES)
    tn = 1024
    out = pl.pallas_call(
        _rpb_gather_kernel,
        grid=(GRID_W * LANES // tn,),
        in_specs=[pl.BlockSpec((nh * n_slot, 2 * LANES), lambda j: (0, 0))],
        out_specs=pl.BlockSpec((nh * n_slot, tn), lambda j: (0, j)),
        out_shape=jax.ShapeDtypeStruct((nh * n_slot, GRID_W * LANES), F32),
        compiler_params=_params(("arbitrary",), nh * n_slot * tn * 4 + 2 * LANES * tn * 4),
        name="rpb_gather",
    )(rows)
    return out.reshape(nh, n_slot, GRID_W, LANES)


def _na_block_plan(rows):
    kr = min(WIN_ROWS, rows)
    n_blk = rows // NA_Q_ROWS
    kb = np.clip(np.arange(n_blk) * NA_Q_ROWS - kr // 2, 0, rows - NA_K_ROWS)
    layouts, layout_of = [], []
    for blk in range(n_blk):
        dr = np.full((NA_Q_ROWS, NA_K_ROWS), -1, np.int64)
        for i in range(NA_Q_ROWS):
            r = blk * NA_Q_ROWS + i
            rs = int(np.clip(r - kr // 2, 0, rows - kr))
            for j in range(NA_K_ROWS):
                krow = kb[blk] + j
                if rs <= krow < rs + kr:
                    dr[i, j] = krow - r + WIN_ROWS - 1
        key = dr.tobytes()
        if key not in [l.tobytes() for l in layouts]:
            layouts.append(dr)
        layout_of.append([l.tobytes() for l in layouts].index(key))
    return kb, np.stack(layouts), np.asarray(layout_of)


def _na_bias_kernel(tp_ref, o_ref, *, layouts):
    qc = lax.broadcasted_iota(jnp.int32, (GRID_W, LANES), 0)
    lane = lax.broadcasted_iota(jnp.int32, (GRID_W, LANES), 1)
    kc = lane & (GRID_W - 1)
    left = lane < GRID_W
    col_start = jnp.clip(qc - WIN_COLS // 2, 0, GRID_W - WIN_COLS)
    col_ok = (kc >= col_start) & (kc < col_start + WIN_COLS)
    n_lay, n_q, n_k = layouts.shape
    for lay in range(n_lay):
        for i in range(n_q):
            for p in range(-(-n_k // 2)):
                d_l = int(layouts[lay, i, 2 * p])
                d_r = int(layouts[lay, i, 2 * p + 1]) if 2 * p + 1 < n_k else -1
                width = LANES if 2 * p + 1 < n_k else GRID_W
                if d_l < 0 and d_r < 0:
                    tile = jnp.full((GRID_W, LANES), -jnp.inf, F32)
                else:
                    assert d_l < 0 or d_r < 0 or d_r == d_l + 1
                    ok = col_ok
                    if d_l < 0:
                        ok = ok & jnp.logical_not(left)
                    if d_r < 0:
                        ok = ok & left
                    slot = d_r if d_r >= 0 else d_l + 1
                    tile = jnp.where(ok, tp_ref[slot] * LOG2E, -jnp.inf)
                o_ref[lay, i * GRID_W:(i + 1) * GRID_W, p * LANES:p * LANES + width] = tile[:, :width]


def _na_bias(rpb_pairs, layouts):
    nh, n_slot = rpb_pairs.shape[:2]
    n_lay = layouts.shape[0]
    tq, nkw = NA_Q_ROWS * GRID_W, NA_K_ROWS * GRID_W
    return pl.pallas_call(
        functools.partial(_na_bias_kernel, layouts=layouts),
        grid=(nh,),
        in_specs=[pl.BlockSpec((None, n_slot, GRID_W, LANES), lambda h: (h, 0, 0, 0))],
        out_specs=pl.BlockSpec((None, n_lay, tq, nkw), lambda h: (h, 0, 0, 0)),
        out_shape=jax.ShapeDtypeStruct((nh, n_lay, tq, nkw), F32),
        compiler_params=_params(("arbitrary",), n_lay * tq * nkw * 4),
        name="na_bias",
    )(rpb_pairs)


def _na_kernel(q_ref, kl_ref, vl_ref, kc_ref, vc_ref, bias_ref, o_ref, *, plan):
    tq = NA_Q_ROWS * GRID_W
    nkw = NA_K_ROWS * GRID_W
    nt = (((1,), (1,)), ((), ()))
    kc = kc_ref[...]
    vc = vc_ref[...]

    def scores(blk):
        kb, lay = plan[blk]
        q = q_ref[blk * tq:(blk + 1) * tq, :]
        ks = slice(kb * GRID_W, kb * GRID_W + nkw)
        return (lax.dot_general(q, kl_ref[ks, :], nt, preferred_element_type=F32) + bias_ref[lay],
                lax.dot_general(q, kc, nt, preferred_element_type=F32))

    def weighted_values(blk, e_w, e_c, l):
        kb, _ = plan[blk]
        ks = slice(kb * GRID_W, kb * GRID_W + nkw)
        o = (jnp.dot(e_w, vl_ref[ks, :], preferred_element_type=F32) + jnp.dot(e_c, vc, preferred_element_type=F32))
        o_ref[blk * tq:(blk + 1) * tq, :] = (o * (1.0 / l)).astype(o_ref.dtype)

    s_next = scores(0)
    pending = None
    for blk in range(len(plan)):
        s_w, s_c = s_next
        if blk + 1 < len(plan):
            s_next = scores(blk + 1)
        if pending is not None:
            weighted_values(blk - 1, *pending)
        m = jnp.maximum(jnp.max(s_w, axis=-1, keepdims=True), jnp.max(s_c, axis=-1, keepdims=True))
        e_w = jnp.exp2(s_w - m)
        e_c = jnp.exp2(s_c - m)
        l = jnp.sum(e_w, axis=-1, keepdims=True) + jnp.sum(e_c, axis=-1, keepdims=True)
        pending = (e_w.astype(BF16), e_c.astype(BF16), l)
    weighted_values(len(plan) - 1, *pending)


def _na_attn(p1, rpb, n_batch, seq, ctx_len, n_heads):
    rows = seq // GRID_W
    kb, layouts, layout_of = _na_block_plan(rows)
    bias = _na_bias(_rpb_pairs(rpb), layouts)
    n_lay = layouts.shape[0]
    tq = NA_Q_ROWS * GRID_W
    nkw = NA_K_ROWS * GRID_W
    ctx_rb0 = (n_batch * seq) // ctx_len
    plan = tuple((int(k), int(l)) for k, l in zip(kb, layout_of))
    return pl.pallas_call(
        functools.partial(_na_kernel, plan=plan),
        grid=(n_heads, n_batch),
        in_specs=[
            pl.BlockSpec((seq, LANES), lambda h, b: (b, h)),
            pl.BlockSpec((seq, LANES), lambda h, b: (b, n_heads + h)),
            pl.BlockSpec((seq, LANES), lambda h, b: (b, 2 * n_heads + h)),
            pl.BlockSpec((ctx_len, LANES), lambda h, b: (ctx_rb0 + b, n_heads + h)),
            pl.BlockSpec((ctx_len, LANES), lambda h, b: (ctx_rb0 + b, 2 * n_heads + h)),
            pl.BlockSpec((None, n_lay, tq, nkw), lambda h, b: (h, 0, 0, 0)),
        ],
        out_specs=pl.BlockSpec((seq, LANES), lambda h, b: (b, h)),
        out_shape=jax.ShapeDtypeStruct((n_batch * seq, n_heads * LANES), BF16),
        compiler_params=_params(("arbitrary", "arbitrary"), n_lay * tq * nkw * 4 + 4 * seq * LANES * 2),
        name="na_attn",
    )(p1, p1, p1, p1, p1, bias)


def _pack_rows(rows, n_rows=8):
    out = jnp.zeros((n_rows, LANES), F32)
    for r, v in enumerate(rows):
        out = out.at[r, :v.shape[0]].set(v.astype(F32))
    return out


def kernel(x, c, ctx, c_ctx, ada_w, ada_b, norm_mix_g, norm_ffn_g, final_norm_g, ffn_w1, ffn_w3, ffn_w2, ev_w_in, ev_conv_w, ev_conv_b, ev_a_log, ev_dt_bias, ev_d_skip, ev_ssm_norm_g, ev_lam_q1, ev_lam_k1, ev_lam_q2, ev_lam_k2, ev_subln_g, ev_w_out, od_w_in, od_rpb, od_w_out):
    n_batch, seq, d = x.shape
    ctx_len = ctx.shape[1]
    depth = ada_w.shape[0]
    n_lat = n_batch * seq
    n_tok = n_lat + n_batch * ctx_len

    d_ssm = ev_ssm_norm_g.shape[1]
    n_ssm_heads = ev_d_skip.shape[1]
    d_xbc = ev_conv_w.shape[2]
    d_qk = d_v = (ev_w_in.shape[2] - d_ssm - d_xbc - 2 * n_ssm_heads) // 3
    n_diff_heads = d_v // (2 * DIFF_HEAD_DIM)
    n_na_heads = od_rpb.shape[1]

    stream = (x.reshape(n_lat, d), ctx.reshape(n_batch * ctx_len, d))
    cond = jnp.zeros((COND_ROWS, d), F32).at[:n_batch].set(c).at[n_batch].set(c_ctx)
    mod = _ada_mod(cond, ada_w, ada_b).reshape(depth, COND_ROWS, 1, 6 * d)
    rope_tabs = _rope_tables(seq)
    kw = dict(seq=seq, n_batch=n_batch)
    ffn_w = (ffn_w1, ffn_w3, ffn_w2)

    for i in range(depth):
        ctx_out = i < depth - 1
        m_rows = n_tok if ctx_out else n_lat
        j = i // 2
        if i % 2 == 0:
            lambda_init = 0.8 - 0.6 * math.exp(-0.3 * i)
            w_in = ev_w_in[j]
            dt0 = d_ssm + d_xbc
            w_t = jnp.swapaxes(w_in, 0, 1)
            w_parts = [w_t[:dt0].astype(BF16), w_t[dt0 + 2 * n_ssm_heads:].astype(BF16)]
            w_dt = jnp.zeros((2 * LANES, d), F32)
            w_dt = w_dt.at[:n_ssm_heads].set(w_t[dt0:dt0 + n_ssm_heads])
            w_dt = w_dt.at[LANES:LANES + n_ssm_heads].set(w_t[dt0 + n_ssm_heads:dt0 + 2 * n_ssm_heads])
            p0, dt_raw = _proj(stream, norm_mix_g[i], mod, i, w_parts, w_dt.astype(BF16), F32, **kw)
            q0 = d_ssm + d_xbc
            k0, v0 = q0 + d_qk, q0 + 2 * d_qk

            xbc = _ssd_conv(p0, d_ssm, d_xbc, ev_conv_w[j], ev_conv_b[j], seq, ctx_len, n_lat)
            prm = jnp.stack([_pack_rows([ev_a_log[j, r], ev_dt_bias[j, r]]) for r in range(2)])
            y2 = _ssd_scan(xbc, dt_raw, prm, n_batch, seq, ctx_len, d_ssm)
            mix_ssd = _ssd_finish(y2, xbc, p0, jnp.repeat(ev_d_skip[j], SSM_HEAD_DIM), ev_ssm_norm_g[j], d_ssm)

            lam_p = _pack_rows([ev_lam_q1[j], ev_lam_k1[j], ev_lam_q2[j], ev_lam_k2[j]])
            attn_args = (p0, q0, k0, v0, n_diff_heads, rope_tabs, lam_p, ev_subln_g[j], lambda_init,
                         n_batch, seq, ctx_len)
            mix_attn = _diff_attn(*attn_args, with_lat=True)
            if ctx_out:
                mix_attn = (mix_attn, _diff_attn(*attn_args, with_lat=False))
            lhs = [mix_ssd, mix_attn]
            w_out = ev_w_out[j]
        else:
            d_na = n_na_heads * NA_HEAD_DIM
            q_scale = jnp.where(jnp.arange(3 * d_na) < d_na, NA_HEAD_DIM ** -0.5 * LOG2E, 1.0).astype(F32)
            p1 = _proj(stream, norm_mix_g[i], mod, i, od_w_in[j], None, BF16, col_scale=q_scale, **kw)
            assert not ctx_out, "context-query neighbourhood layers are not needed at this depth"
            lhs = [_na_attn(p1, od_rpb[j], n_batch, seq, ctx_len, n_na_heads)]
            w_out = od_w_out[j]
        stream = _out_proj(lhs, w_out, stream, mod, i, 2, m_rows, **kw)
        stream = _ffn(stream, norm_ffn_g[i], mod, i, *ffn_w, final_norm_g, not ctx_out, m_rows, **kw)
    return stream.reshape(n_batch, seq, d)
```

```python
import functools
import math

import jax
import jax.numpy as jnp
import numpy as np
from jax import lax
from jax.experimental import pallas as pl
from jax.experimental.pallas import tpu as pltpu

F32 = jnp.float32
BF16 = jnp.bfloat16

GRID_W = 64
SSM_HEAD_DIM = 64
SSM_GROUPS = 4
SSM_STATE = 128
SSM_CONV = 5
SSM_CHUNK = 128
DIFF_HEAD_DIM = 64
NA_HEAD_DIM = 128
WIN_ROWS = 8
WIN_COLS = 16
ROPE_BASE = 10000.0
NORM_EPS = 1e-6
LOG2E = math.log2(math.e)

LANES = 128
SUBLANES = 8
V7X_VMEM_BYTES = 64 * 1024 * 1024
VMEM_HEADROOM_BYTES = 3 * 1024 * 1024
VMEM_MIN_REQUEST_BYTES = 32 * 1024 * 1024
VMEM_TEMPORARIES_BYTES = 16 * 1024 * 1024

NA_Q_ROWS = 4
NA_K_ROWS = NA_Q_ROWS + WIN_ROWS - 1
COND_ROWS = 16


def _vmem_limit(block_bytes):
    want = max(VMEM_MIN_REQUEST_BYTES, 2 * block_bytes + VMEM_TEMPORARIES_BYTES)
    return int(min(V7X_VMEM_BYTES - VMEM_HEADROOM_BYTES, want))


def _params(semantics, block_bytes):
    return pltpu.CompilerParams(dimension_semantics=semantics, vmem_limit_bytes=_vmem_limit(block_bytes))


def _silu(v):
    return v * jax.nn.sigmoid(v)


def _mod_row(i, tm, seq, n_batch):
    return jnp.minimum((i * tm) // seq, n_batch)


def _ada_kernel(cond_ref, w_ref, b_ref, o_ref):
    s = _silu(cond_ref[...]).astype(BF16)
    o_ref[...] = jnp.dot(s, w_ref[...].astype(BF16), preferred_element_type=F32) + b_ref[...]


def _ada_mod(cond, ada_w, ada_b, tn=2048):
    depth, d, n = ada_w.shape
    return pl.pallas_call(
        _ada_kernel,
        grid=(depth, n // tn),
        in_specs=[
            pl.BlockSpec((COND_ROWS, d), lambda l, j: (0, 0)),
            pl.BlockSpec((None, d, tn), lambda l, j: (l, 0, j)),
            pl.BlockSpec((None, 1, tn), lambda l, j: (l, 0, j)),
        ],
        out_specs=pl.BlockSpec((None, COND_ROWS, tn), lambda l, j: (l, 0, j)),
        out_shape=jax.ShapeDtypeStruct((depth, COND_ROWS, n), F32),
        compiler_params=_params(("arbitrary", "arbitrary"), d * tn * 4),
        name="ada_mod",
    )(cond, ada_w, ada_b.reshape(depth, 1, n))


def _norm_mod_pipeline(x_ref, g_ref, mod_ref, shift_idx, scale_idx, n_chunks, consume):
    tm, d = x_ref.shape
    shift = mod_ref[:, shift_idx * d:(shift_idx + 1) * d]
    gain = g_ref[...] * (1.0 + mod_ref[:, scale_idx * d:(scale_idx + 1) * d])
    chunk = tm // n_chunks

    def norm_rows(c):
        x = x_ref[c * chunk:(c + 1) * chunk, :]
        inv = lax.rsqrt(jnp.mean(x * x, axis=-1, keepdims=True) + NORM_EPS)
        return (x * inv * gain + shift).astype(BF16)

    h_next = norm_rows(0)
    for c in range(n_chunks):
        h = h_next
        if c + 1 < n_chunks:
            h_next = norm_rows(c + 1)
        consume(slice(c * chunk, (c + 1) * chunk), h)


def _proj_kernel(*refs, n_x, n_w, tiles_per_part, n_lat_tiles, has_aux, has_scale, first_step_chunks=4):
    rest = list(refs)
    x_refs = [rest.pop(0) for _ in range(n_x)]
    g_ref, mod_ref = rest.pop(0), rest.pop(0)
    w_refs = [rest.pop(0) for _ in range(max(n_w, 1))]
    cs_ref = rest.pop(0) if has_scale else None
    if has_aux:
        waux_ref, o_ref, oaux_ref, h_ref = rest
    else:
        o_ref, h_ref = rest
    dims = (((1,), (1,)), ((), ())) if n_w else (((1,), (0,)), ((), ()))
    j = pl.program_id(1)
    tm, d = x_refs[0].shape

    def emit(h, rows, w_ref):
        acc = lax.dot_general(h, w_ref[...].astype(BF16), dims, preferred_element_type=F32)
        if has_scale:
            acc = acc * cs_ref[...]
        o_ref[rows, :] = acc.astype(o_ref.dtype)

    def first_tile(x_ref):
        def consume(rows, h):
            h_ref[rows, :] = h
            emit(h, rows, w_refs[0])
            if has_aux:
                oaux_ref[rows, :] = lax.dot_general(h, waux_ref[...], dims, preferred_element_type=F32)

        _norm_mod_pipeline(x_ref, g_ref, mod_ref, 0, 1, first_step_chunks, consume)

    if n_x == 1:
        pl.when(j == 0)(functools.partial(first_tile, x_refs[0]))
    else:
        is_lat = pl.program_id(0) < n_lat_tiles
        pl.when(jnp.logical_and(j == 0, is_lat))(functools.partial(first_tile, x_refs[0]))
        pl.when(jnp.logical_and(j == 0, jnp.logical_not(is_lat)))(functools.partial(first_tile, x_refs[1]))

    for p in range(max(n_w, 1)):
        lo = max(p * tiles_per_part, 1)
        in_part = (j >= lo) if n_w == 0 else jnp.logical_and(j >= lo, j < (p + 1) * tiles_per_part)
        pl.when(in_part)(lambda p=p: emit(h_ref[...], slice(None), w_refs[p]))


def _proj(x, g, mod, layer, w, w_aux, out_dtype, seq, n_batch, col_scale=None, tm=1024, tn=1024):
    parts = list(w) if isinstance(w, (list, tuple)) else None
    if parts is None:
        d, n = w.shape
        n_w, tiles_per_part = 0, 0
    else:
        d = parts[0].shape[1]
        n_w, tiles_per_part = len(parts), parts[0].shape[0] // tn
        assert all(p.shape == (tiles_per_part * tn, d) for p in parts)
        n = n_w * tiles_per_part * tn
    m = sum(a.shape[0] for a in x) if isinstance(x, tuple) else x.shape[0]
    n_lat_tiles = (n_batch * seq) // tm
    has_aux = w_aux is not None
    has_scale = col_scale is not None
    mod_spec = pl.BlockSpec((None, None, 1, mod.shape[-1]),
                            lambda i, j: (layer, _mod_row(i, tm, seq, n_batch), 0, 0))
    in_specs, args = _row_operand(x, tm, n_lat_tiles, d, lambda j: 0, single_ctx_buffer=True)
    n_x = len(args)
    in_specs += [pl.BlockSpec((1, d), lambda i, j: (0, 0)), mod_spec]
    args += [g.reshape(1, d), mod]
    if parts is None:
        in_specs.append(pl.BlockSpec((d, tn), lambda i, j: (0, j)))
        args.append(w)
    else:
        for p, part in enumerate(parts):
            in_specs.append(pl.BlockSpec(
                (tn, d), lambda i, j, p=p: (jnp.clip(j - p * tiles_per_part, 0, tiles_per_part - 1), 0)))
            args.append(part)
    out_specs = [pl.BlockSpec((tm, tn), lambda i, j: (i, j))]
    out_shape = [jax.ShapeDtypeStruct((m, n), out_dtype)]
    if has_scale:
        in_specs.append(pl.BlockSpec((1, tn), lambda i, j: (0, j)))
        args.append(col_scale.reshape(1, n))
    if has_aux:
        na = w_aux.shape[0] if parts is not None else w_aux.shape[1]
        in_specs.append(pl.BlockSpec(w_aux.shape, lambda i, j: (0, 0)))
        out_specs.append(pl.BlockSpec((tm, na), lambda i, j: (i, 0)))
        out_shape.append(jax.ShapeDtypeStruct((m, na), F32))
        args.append(w_aux)
    w_itemsize = (parts[0] if parts is not None else w).dtype.itemsize
    block_bytes = n_x * tm * d * 4 + max(n_w, 1) * d * tn * w_itemsize + tm * tn * 4 + tm * d
    outs = pl.pallas_call(
        functools.partial(_proj_kernel, n_x=n_x, n_w=n_w, tiles_per_part=tiles_per_part, n_lat_tiles=n_lat_tiles,
                          has_aux=has_aux, has_scale=has_scale),
        grid=(m // tm, n // tn),
        in_specs=in_specs,
        out_specs=out_specs,
        out_shape=out_shape,
        scratch_shapes=[pltpu.VMEM((tm, d), BF16)],
        compiler_params=_params(("arbitrary", "arbitrary"), block_bytes),
        name="proj",
    )(*args)
    return outs if has_aux else outs[0]


def _row_operand(a, tm, n_lat_tiles, width, col_of_j, single_ctx_buffer=False):
    if not isinstance(a, tuple):
        return [pl.BlockSpec((tm, width), lambda i, j: (i, col_of_j(j)))], [a]
    lat, ctx = a
    assert ctx.shape[0] % tm == 0 and lat.shape[0] == n_lat_tiles * tm

    def lat_index(i, j):
        return jnp.minimum(i, n_lat_tiles - 1), jnp.where(i < n_lat_tiles, col_of_j(j), 0)

    def ctx_index(i, j):
        return jnp.maximum(i - n_lat_tiles, 0), jnp.where(i < n_lat_tiles, 0, col_of_j(j))

    ctx_mode = dict(pipeline_mode=pl.Buffered(1)) if single_ctx_buffer else {}
    return [pl.BlockSpec((tm, width), lat_index), pl.BlockSpec((tm, width), ctx_index, **ctx_mode)], [lat, ctx]


def _pick_rows(refs, n_lat_tiles):
    if len(refs) == 1:
        return refs[0][...]
    return jnp.where(pl.program_id(0) < n_lat_tiles, refs[0][...], refs[1][...])


def _out_proj_kernel(*refs, arity, n_lat_tiles):
    refs = list(refs)
    n_lhs = len(arity) - 1
    groups = [[refs.pop(0) for _ in range(n)] for n in arity[:-1]]
    ws = [refs.pop(0) for _ in range(n_lhs)]
    res = [refs.pop(0) for _ in range(arity[-1])]
    gate_ref, o_ref = refs
    acc = None
    for grp, w_ref in zip(groups, ws):
        part = jnp.dot(_pick_rows(grp, n_lat_tiles), w_ref[...].astype(BF16), preferred_element_type=F32)
        acc = part if acc is None else acc + part
    o_ref[...] = _pick_rows(res, n_lat_tiles) + gate_ref[...] * acc


def _out_proj(lhs_list, w, res, mod, layer, gate_idx, m_rows, seq, n_batch, tm=512, tn=2048):
    d = w.shape[1]
    n_lat_tiles = (n_batch * seq) // tm
    in_specs, args, arity, widths = [], [], [], []
    for a in lhs_list:
        kk = (a[0] if isinstance(a, tuple) else a).shape[1]
        sp, ar = _row_operand(a, tm, n_lat_tiles, kk, lambda j: 0)
        in_specs += sp
        args += ar
        arity.append(len(ar))
        widths.append(kk)
    row0 = 0
    for kk in widths:
        in_specs.append(pl.BlockSpec((kk, tn), lambda i, j, rb=row0 // kk: (rb, j)))
        args.append(w)
        row0 += kk
    sp, ar = _row_operand(res, tm, n_lat_tiles, tn, lambda j: j)
    in_specs += sp
    args += ar
    arity.append(len(ar))
    in_specs.append(pl.BlockSpec((None, None, 1, tn),
                                 lambda i, j: (layer, _mod_row(i, tm, seq, n_batch), 0, gate_idx * (d // tn) + j)))
    args.append(mod)
    k_total = sum(widths)
    block_bytes = 2 * tm * k_total * 2 + k_total * tn * w.dtype.itemsize + 3 * tm * tn * 4
    return pl.pallas_call(
        functools.partial(_out_proj_kernel, arity=tuple(arity), n_lat_tiles=n_lat_tiles),
        grid=(m_rows // tm, d // tn),
        in_specs=in_specs,
        out_specs=pl.BlockSpec((tm, tn), lambda i, j: (i, j)),
        out_shape=jax.ShapeDtypeStruct((m_rows, d), F32),
        compiler_params=_params(("arbitrary", "arbitrary"), block_bytes),
        name="out_proj",
    )(*args)


def _ffn_kernel(x_ref, g_ref, mod_ref, w1_ref, w3_ref, w2_ref, fg_ref, o_ref, h_ref, *, final_norm, rows,
                first_step_chunks):
    j = pl.program_id(1)
    tm, d = x_ref.shape

    def swiglu(h, w1, w3, w2):
        a = jnp.dot(h, w1, preferred_element_type=F32)
        b = jnp.dot(h, w3, preferred_element_type=F32)
        return jnp.dot((_silu(a) * b).astype(BF16), w2, preferred_element_type=F32)

    @pl.when(j == 0)
    def _():
        w1, w3, w2 = (w_ref[...].astype(BF16) for w_ref in (w1_ref, w3_ref, w2_ref))

        def consume(rows, h):
            h_ref[rows, :] = h
            o_ref[rows, :] = swiglu(h, w1, w3, w2)

        _norm_mod_pipeline(x_ref, g_ref, mod_ref, 3, 4, first_step_chunks, consume)

    @pl.when(j > 0)
    def _():
        o_ref[...] += swiglu(h_ref[...], w1_ref[...].astype(BF16), w3_ref[...].astype(BF16),
                             w2_ref[...].astype(BF16))

    @pl.when(j == pl.num_programs(1) - 1)
    def _():
        gate = mod_ref[:, 5 * d:6 * d]
        fg = fg_ref[...]

        def body(r, carry):
            sl = pl.ds(pl.multiple_of(r * rows, rows), rows)
            y = x_ref[sl, :] + gate * o_ref[sl, :]
            if final_norm:
                y = y * lax.rsqrt(jnp.mean(y * y, axis=-1, keepdims=True) + NORM_EPS) * fg
            o_ref[sl, :] = y
            return carry

        lax.fori_loop(0, x_ref.shape[0] // rows, body, 0)


def _ffn(x, g, mod, layer, w1, w3, w2, final_g, final_norm, m_rows, seq, n_batch, tm=1024, tf=256):
    d = x.shape[1]
    ff = w1.shape[2]
    block_bytes = 2 * tm * d * 4 + d * tf * sum(w.dtype.itemsize for w in (w1, w3, w2)) + tm * d
    return pl.pallas_call(
        functools.partial(_ffn_kernel, final_norm=final_norm, rows=128, first_step_chunks=4),
        grid=(m_rows // tm, ff // tf),
        in_specs=[
            pl.BlockSpec((tm, d), lambda i, j: (i, 0)),
            pl.BlockSpec((1, d), lambda i, j: (0, 0)),
            pl.BlockSpec((None, None, 1, mod.shape[-1]),
                         lambda i, j: (layer, _mod_row(i, tm, seq, n_batch), 0, 0)),
            pl.BlockSpec((None, d, tf), lambda i, j: (layer, 0, j)),
            pl.BlockSpec((None, d, tf), lambda i, j: (layer, 0, j)),
            pl.BlockSpec((None, tf, d), lambda i, j: (layer, j, 0)),
            pl.BlockSpec((1, d), lambda i, j: (0, 0)),
        ],
        out_specs=pl.BlockSpec((tm, d), lambda i, j: (i, 0)),
        out_shape=jax.ShapeDtypeStruct((m_rows, d), F32),
        scratch_shapes=[pltpu.VMEM((tm, d), BF16)],
        compiler_params=_params(("arbitrary", "arbitrary"), block_bytes),
        name="ffn",
    )(x, g.reshape(1, d), mod, w1, w3, w2, final_g.reshape(1, d))


def _conv_kernel(prev_ref, cur_ref, next_ref, w_ref, b_ref, o_ref, ext_ref, *, tm, halo, seq, ctx_len, n_lat_tiles):
    i = pl.program_id(0)
    half = SSM_CONV // 2
    ext_ref[halo:halo + tm, :] = cur_ref[...]

    def taps(mask_of):
        acc = jnp.zeros(cur_ref.shape, F32) + b_ref[...]
        for t in range(SSM_CONV):
            tap = ext_ref[halo - half + t:halo - half + t + tm, :]
            acc = acc + mask_of(t, tap) * w_ref[t:t + 1, :]
        o_ref[...] = _silu(acc)

    @pl.when(i < n_lat_tiles)
    def _():
        prev_in_seg = ((i * tm) & (seq - 1)) != 0
        next_in_seg = (((i + 1) * tm) & (seq - 1)) != 0
        ext_ref[0:halo, :] = jnp.where(prev_in_seg, prev_ref[...], 0.0)
        ext_ref[halo + tm:, :] = jnp.where(next_in_seg, next_ref[...], 0.0)
        taps(lambda t, tap: tap)

    @pl.when(i >= n_lat_tiles)
    def _():
        ext_ref[0:halo, :] = prev_ref[...]
        ext_ref[halo + tm:, :] = next_ref[...]
        pos = (i * tm + lax.broadcasted_iota(jnp.int32, (tm, 1), 0)) & (ctx_len - 1)

        def masked(t, tap):
            src = pos + (t - half)
            return jnp.where((src >= 0) & (src < ctx_len), tap, 0.0)

        taps(masked)


def _ssd_conv(p0, col0, width, conv_w, conv_b, seq, ctx_len, n_lat_rows, tm=1024, tc=1024, halo=8):
    m = p0.shape[0]
    assert seq % tm == 0 and tm % ctx_len == 0 and (seq & (seq - 1)) == 0 and (ctx_len & (ctx_len - 1)) == 0
    cb0 = col0 // tc
    hb = tm // halo
    last_hb = m // halo - 1
    wpad = jnp.zeros((SUBLANES, width), F32).at[:SSM_CONV].set(conv_w)
    return pl.pallas_call(
        functools.partial(_conv_kernel, tm=tm, halo=halo, seq=seq, ctx_len=ctx_len, n_lat_tiles=n_lat_rows // tm),
        grid=(m // tm, width // tc),
        in_specs=[
            pl.BlockSpec((halo, tc), lambda i, j: (jnp.maximum(i * hb - 1, 0), cb0 + j)),
            pl.BlockSpec((tm, tc), lambda i, j: (i, cb0 + j)),
            pl.BlockSpec((halo, tc), lambda i, j: (jnp.minimum((i + 1) * hb, last_hb), cb0 + j)),
            pl.BlockSpec((SUBLANES, tc), lambda i, j: (0, j)),
            pl.BlockSpec((1, tc), lambda i, j: (0, j)),
        ],
        out_specs=pl.BlockSpec((tm, tc), lambda i, j: (i, j)),
        out_shape=jax.ShapeDtypeStruct((m, width), F32),
        scratch_shapes=[pltpu.VMEM((tm + 2 * halo, tc), F32)],
        compiler_params=_params(("arbitrary", "arbitrary"), 3 * tm * tc * 4),
        name="ssd_conv",
    )(p0, p0, p0, wpad, conv_b.reshape(1, width))


def _ssd_scan_kernel(xs_f, bm_f, cm_f, dt_f, xs_b, bm_b, cm_b, dt_b, prm_ref, yf_ref, yb_ref, state_ref, *, n_heads):
    @pl.when(pl.program_id(1) == 0)
    def _():
        state_ref[...] = jnp.zeros_like(state_ref)

    fwd = _ssd_chunk(xs_f, bm_f, cm_f, dt_f, prm_ref.at[0], yf_ref, state_ref.at[0], backward=False, n_heads=n_heads)
    bwd = _ssd_chunk(xs_b, bm_b, cm_b, dt_b, prm_ref.at[1], yb_ref, state_ref.at[1], backward=True, n_heads=n_heads)
    for g in range(SSM_GROUPS):
        fwd(g)
        bwd(g)


def _ssd_chunk(xs_ref, bm_ref, cm_ref, dt_ref, prm_ref, y_ref, state_ref, *, backward, n_heads):
    hp = SSM_HEAD_DIM
    rep = n_heads // SSM_GROUPS
    lc = SSM_CHUNK
    assert 2 * hp == LANES and rep % 2 == 0 and SSM_STATE == LANES and lc == LANES

    dt_in = dt_ref[...] + prm_ref[1:2, :]
    dt = jnp.maximum(dt_in, 0.0) + jnp.log1p(jnp.exp(-jnp.abs(dt_in)))
    dta = dt * (-jnp.exp(prm_ref[0:1, :]) * LOG2E)
    row = lax.broadcasted_iota(jnp.int32, (lc, lc), 0)
    col = lax.broadcasted_iota(jnp.int32, (lc, lc), 1)
    causal = (row <= col) if backward else (row >= col)
    a_cum = jnp.dot(causal.astype(F32), dta, preferred_element_type=F32, precision=lax.Precision.HIGHEST)
    a_tot = a_cum[0:1, :] if backward else a_cum[lc - 1:lc, :]
    w_end = dt * jnp.exp2(a_tot - a_cum)
    a_cum_t, dt_t, w_end_t = a_cum.T, dt.T, w_end.T
    lo = col < hp

    def do_group(g):
        gs = slice(g * SSM_STATE, (g + 1) * SSM_STATE)
        b_f = bm_ref[:, gs]
        c_g = cm_ref[:, gs].astype(BF16)
        cb = lax.dot_general(c_g, b_f.astype(BF16), (((1,), (1,)), ((), ())), preferred_element_type=F32)
        b_t = b_f.T
        cols_g = slice(g * rep * hp, (g + 1) * rep * hp)
        y_off = jnp.dot(c_g, state_ref[:, cols_g].astype(BF16), preferred_element_type=F32)
        for pr in range(rep // 2):
            cols = slice(g * rep * hp + pr * LANES, g * rep * hp + (pr + 1) * LANES)
            xs2 = xs_ref[:, cols].astype(BF16)
            y_d, s_n, e_a = [], [], []
            for h in (g * rep + 2 * pr, g * rep + 2 * pr + 1):
                a_col = jnp.broadcast_to(a_cum[:, h:h + 1], (lc, lc))
                decay = jnp.exp2(jnp.where(causal, a_col - a_cum_t[h:h + 1, :], -jnp.inf))
                m = (cb * decay * dt_t[h:h + 1, :]).astype(BF16)
                y_d.append(jnp.dot(m, xs2, preferred_element_type=F32))
                s_n.append(jnp.dot((b_t * w_end_t[h:h + 1, :]).astype(BF16), xs2, preferred_element_type=F32))
                e_a.append(jnp.exp2(a_col))
            h0 = g * rep + 2 * pr
            y_ref[:, cols] = (jnp.where(lo, y_d[0], y_d[1])
                              + y_off[:, pr * LANES:(pr + 1) * LANES] * jnp.where(lo, e_a[0], e_a[1]))
            chunk_decay = jnp.where(lo[0:1, :], jnp.exp2(a_tot[:, h0:h0 + 1]), jnp.exp2(a_tot[:, h0 + 1:h0 + 2]))
            state_ref[:, cols] = state_ref[:, cols] * chunk_decay + jnp.where(lo, s_n[0], s_n[1])

    return do_group


def _ssd_scan(xbc, dt_raw, prm, n_batch, seq, ctx_len, d_ssm):
    m = xbc.shape[0]
    lc = SSM_CHUNK
    n_heads = d_ssm // SSM_HEAD_DIM
    nc_ctx, nc_lat = ctx_len // lc, seq // lc
    ctx_blk0 = n_batch * nc_lat
    gn = SSM_GROUPS * SSM_STATE

    def row_blk(b, s, backward):
        ctx_c = nc_ctx - 1 - s if backward else s
        lat_c = nc_lat - 1 - (s - nc_ctx) if backward else s - nc_ctx
        return jnp.where(s < nc_ctx, ctx_blk0 + b * nc_ctx + ctx_c, b * nc_lat + lat_c)

    def chunk_specs(backward):
        r = int(backward)
        return [
            pl.BlockSpec((lc, d_ssm), lambda b, s: (row_blk(b, s, backward), 0)),
            pl.BlockSpec((lc, gn), lambda b, s: (row_blk(b, s, backward), d_ssm // gn)),
            pl.BlockSpec((lc, gn), lambda b, s: (row_blk(b, s, backward), d_ssm // gn + 1)),
            pl.BlockSpec((lc, LANES), lambda b, s: (row_blk(b, s, backward), r)),
        ]

    return pl.pallas_call(
        functools.partial(_ssd_scan_kernel, n_heads=n_heads),
        grid=(n_batch, nc_ctx + nc_lat),
        in_specs=chunk_specs(False) + chunk_specs(True) + [pl.BlockSpec((2, SUBLANES, LANES), lambda b, s: (0, 0, 0))],
        out_specs=[pl.BlockSpec((lc, d_ssm), lambda b, s: (row_blk(b, s, False), 0)),
                   pl.BlockSpec((lc, d_ssm), lambda b, s: (row_blk(b, s, True), 0))],
        out_shape=[jax.ShapeDtypeStruct((m, d_ssm), F32), jax.ShapeDtypeStruct((m, d_ssm), F32)],
        scratch_shapes=[pltpu.VMEM((2, SSM_STATE, d_ssm), F32)],
        compiler_params=_params(("arbitrary", "arbitrary"), 8 * lc * d_ssm * 4),
        name="ssd_scan",
    )(xbc, xbc, xbc, dt_raw, xbc, xbc, xbc, dt_raw, prm)


def _ssd_finish_kernel(yf_ref, yb_ref, xs_ref, z_ref, dskip_ref, g_ref, o_ref, *, group_width):
    y = (yf_ref[...] + yb_ref[...] + dskip_ref[...] * xs_ref[...]) * _silu(z_ref[...])
    for g in range(y.shape[1] // group_width):
        sl = slice(g * group_width, (g + 1) * group_width)
        v = y[:, sl]
        vn = v * lax.rsqrt(jnp.mean(v * v, axis=-1, keepdims=True) + NORM_EPS) * g_ref[:, sl]
        o_ref[:, sl] = vn.astype(o_ref.dtype)


def _ssd_finish(y2, xbc, p0, d_skip_lanes, norm_g, d_ssm, tm=1024):
    m = xbc.shape[0]
    return pl.pallas_call(
        functools.partial(_ssd_finish_kernel, group_width=d_ssm // SSM_GROUPS),
        grid=(m // tm,),
        in_specs=[
            pl.BlockSpec((tm, d_ssm), lambda i: (i, 0)),
            pl.BlockSpec((tm, d_ssm), lambda i: (i, 0)),
            pl.BlockSpec((tm, d_ssm), lambda i: (i, 0)),
            pl.BlockSpec((tm, d_ssm), lambda i: (i, 0)),
            pl.BlockSpec((1, d_ssm), lambda i: (0, 0)),
            pl.BlockSpec((1, d_ssm), lambda i: (0, 0)),
        ],
        out_specs=pl.BlockSpec((tm, d_ssm), lambda i: (i, 0)),
        out_shape=jax.ShapeDtypeStruct((m, d_ssm), BF16),
        compiler_params=_params(("arbitrary",), 5 * tm * d_ssm * 4),
        name="ssd_finish",
    )(*y2, xbc, p0, d_skip_lanes.reshape(1, d_ssm), norm_g.reshape(1, d_ssm))


def _rope(x, cos, sin_up, sin_dn):
    quarter = DIFF_HEAD_DIM // 4
    return (x * cos + pltpu.roll(x, LANES - quarter, axis=1) * sin_up + pltpu.roll(x, quarter, axis=1) * sin_dn)


def _diff_attn_heads_kernel(*refs, heads_per_step, with_lat, **kw):
    n_head_refs = 5 if with_lat else 3
    *ins, o_ref, k_scr, vt_scr = refs
    for h in range(heads_per_step):
        cs = slice(h * LANES, (h + 1) * LANES)
        head_ins = [r.at[:, cs] for r in ins[:n_head_refs]] + list(ins[n_head_refs:])
        _diff_attn_kernel(*head_ins, o_ref.at[:, cs], k_scr.at[h], vt_scr.at[h], with_lat=with_lat, **kw)


def _diff_attn_kernel(*refs, with_lat, lambda_init, n_ctx, sub_q, key_chunk=768):
    if with_lat:
        (q_ref, kc_ref, kl_ref, vc_ref, vl_ref, qcos_ref, qsu_ref, qsd_ref, kcos_ref, ksu_ref, ksd_ref,
         lam_ref, g_ref, o_ref, k_scr, vt_scr) = refs
    else:
        q_ref, kc_ref, vc_ref, lam_ref, g_ref, o_ref, k_scr, vt_scr = refs

    @pl.when(pl.program_id(2) == 0)
    def _():
        k_scr[0:n_ctx, :] = kc_ref[...].astype(BF16)
        vt_scr[:, 0:n_ctx] = vc_ref[...].T.astype(BF16)
        if with_lat:
            k_scr[n_ctx:, :] = _rope(kl_ref[...], kcos_ref[...], ksu_ref[...], ksd_ref[...]).astype(BF16)
            vt_scr[:, n_ctx:] = vl_ref[...].T.astype(BF16)

    lam_p = lam_ref[...]
    lam = (jnp.exp(jnp.sum(lam_p[0:1, :] * lam_p[1:2, :], axis=-1, keepdims=True))
           - jnp.exp(jnp.sum(lam_p[2:3, :] * lam_p[3:4, :], axis=-1, keepdims=True)) + lambda_init)
    first = lax.broadcasted_iota(jnp.int32, (LANES, sub_q), 0) < DIFF_HEAD_DIM
    n_sub = q_ref.shape[0] // sub_q

    def scores(t):
        rows = slice(t * sub_q, (t + 1) * sub_q)
        q = q_ref[rows, :]
        if with_lat:
            q = _rope(q, qcos_ref[rows, :], qsu_ref[rows, :], qsd_ref[rows, :])
        qt = (q * (DIFF_HEAD_DIM ** -0.5 * LOG2E)).T
        q2t = jnp.concatenate([jnp.where(first, qt, 0.0), jnp.where(first, 0.0, qt)], axis=1).astype(BF16)
        blocks = [jnp.dot(k_scr[k0:k1, :], q2t, preferred_element_type=F32) for k0, k1 in key_chunks]
        return blocks, [jnp.max(b, axis=0, keepdims=True) for b in blocks]

    n_keys = k_scr.shape[0]
    key_chunks = [(k0, min(k0 + key_chunk, n_keys)) for k0 in range(0, n_keys, key_chunk)]
    s_next = scores(0)
    for t in range(n_sub):
        rows = slice(t * sub_q, (t + 1) * sub_q)
        s_blocks, maxes = s_next
        if t + 1 < n_sub:
            s_next = scores(t + 1)
        m = functools.reduce(jnp.maximum, maxes)
        ovt, l = None, None
        for (k0, k1), s in zip(key_chunks, s_blocks):
            e = jnp.exp2(s - m)
            part = jnp.dot(vt_scr[:, k0:k1], e.astype(BF16), preferred_element_type=F32)
            part_l = jnp.sum(e, axis=0, keepdims=True)
            ovt, l = (part, part_l) if ovt is None else (ovt + part, l + part_l)
        ovt = ovt * (1.0 / l)
        ot = ovt[:, :sub_q] - lam * ovt[:, sub_q:]
        ot = ot * lax.rsqrt(jnp.mean(ot * ot, axis=0, keepdims=True) + NORM_EPS) * g_ref[...] * (1.0 - lambda_init)
        o_ref[rows, :] = ot.T.astype(o_ref.dtype)


def _diff_attn(p0, q_col0, k_col0, v_col0, n_heads, rope_tabs, lam_p, subln_g, lambda_init,
               n_batch, seq, ctx_len, with_lat, tq=2048, sub_q=256):
    ctx_rb0 = (n_batch * seq) // ctx_len
    n_keys = ctx_len + (seq if with_lat else 0)
    hps = 1 if with_lat else n_heads
    width = hps * LANES
    cb = lambda c0: c0 // width
    if with_lat:
        n_q = seq // tq
        q_spec = pl.BlockSpec((tq, width), lambda b, h, i: (b * n_q + i, cb(q_col0) + h))
    else:
        tq = ctx_len
        n_q = 1
        q_spec = pl.BlockSpec((tq, width), lambda b, h, i: (ctx_rb0 + b, cb(q_col0) + h))
    kc_spec = pl.BlockSpec((ctx_len, width), lambda b, h, i: (ctx_rb0 + b, cb(k_col0) + h))
    vc_spec = pl.BlockSpec((ctx_len, width), lambda b, h, i: (ctx_rb0 + b, cb(v_col0) + h))
    small = [pl.BlockSpec((SUBLANES, LANES), lambda b, h, i: (0, 0)),
             pl.BlockSpec((LANES, sub_q), lambda b, h, i: (0, 0))]
    small_args = [lam_p, jnp.broadcast_to(subln_g[:, None], (LANES, sub_q))]
    if with_lat:
        kl_spec = pl.BlockSpec((seq, width), lambda b, h, i: (b, cb(k_col0) + h))
        vl_spec = pl.BlockSpec((seq, width), lambda b, h, i: (b, cb(v_col0) + h))
        qt = pl.BlockSpec((tq, LANES), lambda b, h, i: (i, 0))
        kt = pl.BlockSpec((seq, LANES), lambda b, h, i: (0, 0))
        in_specs = [q_spec, kc_spec, kl_spec, vc_spec, vl_spec, qt, qt, qt, kt, kt, kt] + small
        args = [p0, p0, p0, p0, p0, *rope_tabs, *rope_tabs] + small_args
    else:
        in_specs = [q_spec, kc_spec, vc_spec] + small
        args = [p0, p0, p0] + small_args
    block_bytes = 10 * seq * LANES * 4 if with_lat else 8 * ctx_len * width * 4
    return pl.pallas_call(
        functools.partial(_diff_attn_heads_kernel, heads_per_step=hps, with_lat=with_lat, lambda_init=lambda_init,
                          n_ctx=ctx_len, sub_q=sub_q),
        grid=(n_batch, n_heads // hps, n_q),
        in_specs=in_specs,
        out_specs=pl.BlockSpec((tq, width), lambda b, h, i: (b * n_q + i, h)),
        out_shape=jax.ShapeDtypeStruct((n_batch * n_q * tq, n_heads * LANES), BF16),
        scratch_shapes=[pltpu.VMEM((hps, n_keys, LANES), BF16), pltpu.VMEM((hps, LANES, n_keys), BF16)],
        compiler_params=_params(("arbitrary", "arbitrary", "arbitrary"), block_bytes),
        name="diff_attn_lat" if with_lat else "diff_attn_ctx",
    )(*args)


def _rope_tables(seq):
    half = DIFF_HEAD_DIM // 2
    pos = jnp.arange(seq)
    row, col = pos // GRID_W, pos % GRID_W
    inv_freq = ROPE_BASE ** (-jnp.arange(0, half, 2, dtype=F32) / half)
    lane = np.arange(LANES) % DIFF_HEAD_DIM
    use_col = jnp.asarray(lane >= half)
    first = jnp.asarray((lane % half) < half // 2)
    freq = inv_freq[jnp.asarray(lane % (half // 2))]
    p = jnp.where(use_col[None, :], col[:, None], row[:, None]).astype(F32)
    ang = p * freq[None, :]
    cos, sin = jnp.cos(ang), jnp.sin(ang)
    return cos, jnp.where(first[None, :], -sin, 0.0), jnp.where(first[None, :], 0.0, sin)


def _rpb_gather_kernel(rpb_ref, o_ref):
    n = o_ref.shape[1]
    k = rpb_ref.shape[1]
    colid = pl.program_id(0) * n + lax.broadcasted_iota(jnp.int32, (k, n), 1)
    j = lax.broadcasted_iota(jnp.int32, (k, n), 0)
    qc = lax.shift_right_logical(colid, int(math.log2(LANES)))
    half = lax.shift_right_logical(colid & (LANES - 1), int(math.log2(GRID_W)))
    kc = colid & (GRID_W - 1)
    sel = (j == half * LANES + jnp.clip(kc - qc + (WIN_COLS - 1), 0, 2 * WIN_COLS - 2)).astype(F32)
    o_ref[...] = jnp.dot(rpb_ref[...], sel, preferred_element_type=F32, precision=lax.Precision.HIGHEST)


def _rpb_pairs(rpb):
    nh, nr, ncol = rpb.shape
    n_slot = nr + 1
    left = jnp.pad(rpb, ((0, 0), (1, 0), (0, LANES - ncol)))
    right = jnp.pad(rpb, ((0, 0), (0, 1), (0, LANES - ncol)))
    rows = jnp.concatenate([left, right], axis=2).reshape(nh * n_slot, 2 * LANES)
    tn = 1024
    out = pl.pallas_call(
        _rpb_gather_kernel,
        grid=(GRID_W * LANES // tn,),
        in_specs=[pl.BlockSpec((nh * n_slot, 2 * LANES), lambda j: (0, 0))],
        out_specs=pl.BlockSpec((nh * n_slot, tn), lambda j: (0, j)),
        out_shape=jax.ShapeDtypeStruct((nh * n_slot, GRID_W * LANES), F32),
        compiler_params=_params(("arbitrary",), nh * n_slot * tn * 4 + 2 * LANES * tn * 4),
        name="rpb_gather",
    )(rows)
    return out.reshape(nh, n_slot, GRID_W, LANES)


def _na_block_plan(rows):
    kr = min(WIN_ROWS, rows)
    n_blk = rows // NA_Q_ROWS
    kb = np.clip(np.arange(n_blk) * NA_Q_ROWS - kr // 2, 0, rows - NA_K_ROWS)
    layouts, layout_of = [], []
    for blk in range(n_blk):
        dr = np.full((NA_Q_ROWS, NA_K_ROWS), -1, np.int64)
        for i in range(NA_Q_ROWS):
            r = blk * NA_Q_ROWS + i
            rs = int(np.clip(r - kr // 2, 0, rows - kr))
            for j in range(NA_K_ROWS):
                krow = kb[blk] + j
                if rs <= krow < rs + kr:
                    dr[i, j] = krow - r + WIN_ROWS - 1
        key = dr.tobytes()
        if key not in [l.tobytes() for l in layouts]:
            layouts.append(dr)
        layout_of.append([l.tobytes() for l in layouts].index(key))
    return kb, np.stack(layouts), np.asarray(layout_of)


def _na_bias_kernel(tp_ref, o_ref, *, layouts):
    qc = lax.broadcasted_iota(jnp.int32, (GRID_W, LANES), 0)
    lane = lax.broadcasted_iota(jnp.int32, (GRID_W, LANES), 1)
    kc = lane & (GRID_W - 1)
    left = lane < GRID_W
    col_start = jnp.clip(qc - WIN_COLS // 2, 0, GRID_W - WIN_COLS)
    col_ok = (kc >= col_start) & (kc < col_start + WIN_COLS)
    n_lay, n_q, n_k = layouts.shape
    for lay in range(n_lay):
        for i in range(n_q):
            for p in range(-(-n_k // 2)):
                d_l = int(layouts[lay, i, 2 * p])
                d_r = int(layouts[lay, i, 2 * p + 1]) if 2 * p + 1 < n_k else -1
                width = LANES if 2 * p + 1 < n_k else GRID_W
                if d_l < 0 and d_r < 0:
                    tile = jnp.full((GRID_W, LANES), -jnp.inf, F32)
                else:
                    assert d_l < 0 or d_r < 0 or d_r == d_l + 1
                    ok = col_ok
                    if d_l < 0:
                        ok = ok & jnp.logical_not(left)
                    if d_r < 0:
                        ok = ok & left
                    slot = d_r if d_r >= 0 else d_l + 1
                    tile = jnp.where(ok, tp_ref[slot] * LOG2E, -jnp.inf)
                o_ref[lay, i * GRID_W:(i + 1) * GRID_W, p * LANES:p * LANES + width] = tile[:, :width]


def _na_bias(rpb_pairs, layouts):
    nh, n_slot = rpb_pairs.shape[:2]
    n_lay = layouts.shape[0]
    tq, nkw = NA_Q_ROWS * GRID_W, NA_K_ROWS * GRID_W
    return pl.pallas_call(
        functools.partial(_na_bias_kernel, layouts=layouts),
        grid=(nh,),
        in_specs=[pl.BlockSpec((None, n_slot, GRID_W, LANES), lambda h: (h, 0, 0, 0))],
        out_specs=pl.BlockSpec((None, n_lay, tq, nkw), lambda h: (h, 0, 0, 0)),
        out_shape=jax.ShapeDtypeStruct((nh, n_lay, tq, nkw), F32),
        compiler_params=_params(("arbitrary",), n_lay * tq * nkw * 4),
        name="na_bias",
    )(rpb_pairs)


def _na_kernel(q_ref, kl_ref, vl_ref, kc_ref, vc_ref, bias_ref, o_ref, *, plan):
    tq = NA_Q_ROWS * GRID_W
    nkw = NA_K_ROWS * GRID_W
    nt = (((1,), (1,)), ((), ()))
    kc = kc_ref[...]
    vc = vc_ref[...]

    def scores(blk):
        kb, lay = plan[blk]
        q = q_ref[blk * tq:(blk + 1) * tq, :]
        ks = slice(kb * GRID_W, kb * GRID_W + nkw)
        return (lax.dot_general(q, kl_ref[ks, :], nt, preferred_element_type=F32) + bias_ref[lay],
                lax.dot_general(q, kc, nt, preferred_element_type=F32))

    def weighted_values(blk, e_w, e_c, l):
        kb, _ = plan[blk]
        ks = slice(kb * GRID_W, kb * GRID_W + nkw)
        o = (jnp.dot(e_w, vl_ref[ks, :], preferred_element_type=F32) + jnp.dot(e_c, vc, preferred_element_type=F32))
        o_ref[blk * tq:(blk + 1) * tq, :] = (o * (1.0 / l)).astype(o_ref.dtype)

    s_next = scores(0)
    pending = None
    for blk in range(len(plan)):
        s_w, s_c = s_next
        if blk + 1 < len(plan):
            s_next = scores(blk + 1)
        if pending is not None:
            weighted_values(blk - 1, *pending)
        m = jnp.maximum(jnp.max(s_w, axis=-1, keepdims=True), jnp.max(s_c, axis=-1, keepdims=True))
        e_w = jnp.exp2(s_w - m)
        e_c = jnp.exp2(s_c - m)
        l = jnp.sum(e_w, axis=-1, keepdims=True) + jnp.sum(e_c, axis=-1, keepdims=True)
        pending = (e_w.astype(BF16), e_c.astype(BF16), l)
    weighted_values(len(plan) - 1, *pending)


def _na_attn(p1, rpb, n_batch, seq, ctx_len, n_heads):
    rows = seq // GRID_W
    kb, layouts, layout_of = _na_block_plan(rows)
    bias = _na_bias(_rpb_pairs(rpb), layouts)
    n_lay = layouts.shape[0]
    tq = NA_Q_ROWS * GRID_W
    nkw = NA_K_ROWS * GRID_W
    ctx_rb0 = (n_batch * seq) // ctx_len
    plan = tuple((int(k), int(l)) for k, l in zip(kb, layout_of))
    return pl.pallas_call(
        functools.partial(_na_kernel, plan=plan),
        grid=(n_heads, n_batch),
        in_specs=[
            pl.BlockSpec((seq, LANES), lambda h, b: (b, h)),
            pl.BlockSpec((seq, LANES), lambda h, b: (b, n_heads + h)),
            pl.BlockSpec((seq, LANES), lambda h, b: (b, 2 * n_heads + h)),
            pl.BlockSpec((ctx_len, LANES), lambda h, b: (ctx_rb0 + b, n_heads + h)),
            pl.BlockSpec((ctx_len, LANES), lambda h, b: (ctx_rb0 + b, 2 * n_heads + h)),
            pl.BlockSpec((None, n_lay, tq, nkw), lambda h, b: (h, 0, 0, 0)),
        ],
        out_specs=pl.BlockSpec((seq, LANES), lambda h, b: (b, h)),
        out_shape=jax.ShapeDtypeStruct((n_batch * seq, n_heads * LANES), BF16),
        compiler_params=_params(("arbitrary", "arbitrary"), n_lay * tq * nkw * 4 + 4 * seq * LANES * 2),
        name="na_attn",
    )(p1, p1, p1, p1, p1, bias)


def _pack_rows(rows):
    out = jnp.zeros((SUBLANES, LANES), F32)
    for r, v in enumerate(rows):
        out = out.at[r, :v.shape[0]].set(v.astype(F32))
    return out


def kernel(x, c, ctx, c_ctx, ada_w, ada_b, norm_mix_g, norm_ffn_g, final_norm_g, ffn_w1, ffn_w3, ffn_w2, ev_w_in, ev_conv_w, ev_conv_b, ev_a_log, ev_dt_bias, ev_d_skip, ev_ssm_norm_g, ev_lam_q1, ev_lam_k1, ev_lam_q2, ev_lam_k2, ev_subln_g, ev_w_out, od_w_in, od_rpb, od_w_out):
    n_batch, seq, d = x.shape
    ctx_len = ctx.shape[1]
    depth = ada_w.shape[0]
    n_lat = n_batch * seq
    n_tok = n_lat + n_batch * ctx_len

    d_ssm = ev_ssm_norm_g.shape[1]
    n_ssm_heads = ev_d_skip.shape[1]
    d_xbc = ev_conv_w.shape[2]
    d_qk = d_v = (ev_w_in.shape[2] - d_ssm - d_xbc - 2 * n_ssm_heads) // 3
    n_diff_heads = d_v // (2 * DIFF_HEAD_DIM)
    n_na_heads = od_rpb.shape[1]

    stream = (x.reshape(n_lat, d), ctx.reshape(n_batch * ctx_len, d))
    cond = jnp.zeros((COND_ROWS, d), F32).at[:n_batch].set(c).at[n_batch].set(c_ctx)
    mod = _ada_mod(cond, ada_w, ada_b).reshape(depth, COND_ROWS, 1, 6 * d)
    rope_tabs = _rope_tables(seq)
    kw = dict(seq=seq, n_batch=n_batch)
    ffn_w = (ffn_w1, ffn_w3, ffn_w2)

    for i in range(depth):
        ctx_out = i < depth - 1
        m_rows = n_tok if ctx_out else n_lat
        j = i // 2
        if i % 2 == 0:
            lambda_init = 0.8 - 0.6 * math.exp(-0.3 * i)
            w_in = ev_w_in[j]
            dt0 = d_ssm + d_xbc
            w_t = jnp.swapaxes(w_in, 0, 1)
            w_parts = [w_t[:dt0].astype(BF16), w_t[dt0 + 2 * n_ssm_heads:].astype(BF16)]
            w_dt = jnp.zeros((2 * LANES, d), F32)
            w_dt = w_dt.at[:n_ssm_heads].set(w_t[dt0:dt0 + n_ssm_heads])
            w_dt = w_dt.at[LANES:LANES + n_ssm_heads].set(w_t[dt0 + n_ssm_heads:dt0 + 2 * n_ssm_heads])
            p0, dt_raw = _proj(stream, norm_mix_g[i], mod, i, w_parts, w_dt.astype(BF16), F32, **kw)
            q0 = d_ssm + d_xbc
            k0, v0 = q0 + d_qk, q0 + 2 * d_qk

            xbc = _ssd_conv(p0, d_ssm, d_xbc, ev_conv_w[j], ev_conv_b[j], seq, ctx_len, n_lat)
            prm = jnp.stack([_pack_rows([ev_a_log[j, r], ev_dt_bias[j, r]]) for r in range(2)])
            y2 = _ssd_scan(xbc, dt_raw, prm, n_batch, seq, ctx_len, d_ssm)
            mix_ssd = _ssd_finish(y2, xbc, p0, jnp.repeat(ev_d_skip[j], SSM_HEAD_DIM), ev_ssm_norm_g[j], d_ssm)

            lam_p = _pack_rows([ev_lam_q1[j], ev_lam_k1[j], ev_lam_q2[j], ev_lam_k2[j]])
            attn_args = (p0, q0, k0, v0, n_diff_heads, rope_tabs, lam_p, ev_subln_g[j], lambda_init,
                         n_batch, seq, ctx_len)
            mix_attn = _diff_attn(*attn_args, with_lat=True)
            if ctx_out:
                mix_attn = (mix_attn, _diff_attn(*attn_args, with_lat=False))
            lhs = [mix_ssd, mix_attn]
            w_out = ev_w_out[j]
        else:
            d_na = n_na_heads * NA_HEAD_DIM
            q_scale = jnp.where(jnp.arange(3 * d_na) < d_na, NA_HEAD_DIM ** -0.5 * LOG2E, 1.0).astype(F32)
            p1 = _proj(stream, norm_mix_g[i], mod, i, od_w_in[j], None, BF16, col_scale=q_scale, **kw)
            assert not ctx_out, "context-query neighbourhood layers are not needed at this depth"
            lhs = [_na_attn(p1, od_rpb[j], n_batch, seq, ctx_len, n_na_heads)]
            w_out = od_w_out[j]
        stream = _out_proj(lhs, w_out, stream, mod, i, 2, m_rows, **kw)
        stream = _ffn(stream, norm_ffn_g[i], mod, i, *ffn_w, final_norm_g, not ctx_out, m_rows, **kw)
    return stream.reshape(n_batch, seq, d)
```

```python
import functools
import math

import jax
import jax.numpy as jnp
import numpy as np
from jax import lax
from jax.experimental import pallas as pl
from jax.experimental.pallas import tpu as pltpu

F32 = jnp.float32
BF16 = jnp.bfloat16

GRID_W = 64
SSM_HEAD_DIM = 64
SSM_GROUPS = 4
SSM_STATE = 128
SSM_CONV = 5
SSM_CHUNK = 128
DIFF_HEAD_DIM = 64
NA_HEAD_DIM = 128
WIN_ROWS = 8
WIN_COLS = 16
ROPE_BASE = 10000.0
NORM_EPS = 1e-6
LOG2E = math.log2(math.e)

LANES = 128
SUBLANES = 8
V7X_VMEM_BYTES = 64 * 1024 * 1024
VMEM_HEADROOM_BYTES = 3 * 1024 * 1024
VMEM_MIN_REQUEST_BYTES = 32 * 1024 * 1024
VMEM_TEMPORARIES_BYTES = 16 * 1024 * 1024

NA_Q_ROWS = 4
NA_K_ROWS = NA_Q_ROWS + WIN_ROWS - 1
COND_ROWS = 16


def _vmem_limit(block_bytes):
    want = max(VMEM_MIN_REQUEST_BYTES, 2 * block_bytes + VMEM_TEMPORARIES_BYTES)
    return int(min(V7X_VMEM_BYTES - VMEM_HEADROOM_BYTES, want))


def _params(semantics, block_bytes):
    return pltpu.CompilerParams(dimension_semantics=semantics, vmem_limit_bytes=_vmem_limit(block_bytes))


def _silu(v):
    return v * jax.nn.sigmoid(v)


def _mod_row(i, tm, seq, n_batch):
    return jnp.minimum((i * tm) // seq, n_batch)


def _ada_kernel(cond_ref, w_ref, b_ref, o_ref):
    s = _silu(cond_ref[...]).astype(BF16)
    o_ref[...] = jnp.dot(s, w_ref[...].astype(BF16), preferred_element_type=F32) + b_ref[...]


def _ada_mod(cond, ada_w, ada_b, tn=1024):
    depth, d, n = ada_w.shape
    return pl.pallas_call(
        _ada_kernel,
        grid=(depth, n // tn),
        in_specs=[
            pl.BlockSpec((COND_ROWS, d), lambda l, j: (0, 0)),
            pl.BlockSpec((None, d, tn), lambda l, j: (l, 0, j)),
            pl.BlockSpec((None, 1, tn), lambda l, j: (l, 0, j)),
        ],
        out_specs=pl.BlockSpec((None, COND_ROWS, tn), lambda l, j: (l, 0, j)),
        out_shape=jax.ShapeDtypeStruct((depth, COND_ROWS, n), F32),
        compiler_params=_params(("arbitrary", "arbitrary"), d * tn * 4),
        name="ada_mod",
    )(cond, ada_w, ada_b.reshape(depth, 1, n))


def _norm_mod_pipeline(x_ref, g_ref, mod_ref, shift_idx, scale_idx, n_chunks, consume):
    tm, d = x_ref.shape
    shift = mod_ref[:, shift_idx * d:(shift_idx + 1) * d]
    gain = g_ref[...] * (1.0 + mod_ref[:, scale_idx * d:(scale_idx + 1) * d])
    chunk = tm // n_chunks

    def norm_rows(c):
        x = x_ref[c * chunk:(c + 1) * chunk, :]
        inv = lax.rsqrt(jnp.mean(x * x, axis=-1, keepdims=True) + NORM_EPS)
        return (x * inv * gain + shift).astype(BF16)

    h_next = norm_rows(0)
    for c in range(n_chunks):
        h = h_next
        if c + 1 < n_chunks:
            h_next = norm_rows(c + 1)
        consume(slice(c * chunk, (c + 1) * chunk), h)


def _proj_kernel(*refs, n_x, n_w, tiles_per_part, n_lat_tiles, has_aux, has_scale, first_step_chunks=4):
    rest = list(refs)
    x_refs = [rest.pop(0) for _ in range(n_x)]
    g_ref, mod_ref = rest.pop(0), rest.pop(0)
    w_refs = [rest.pop(0) for _ in range(max(n_w, 1))]
    cs_ref = rest.pop(0) if has_scale else None
    if has_aux:
        waux_ref, o_ref, oaux_ref, h_ref = rest
    else:
        o_ref, h_ref = rest
    dims = (((1,), (1,)), ((), ())) if n_w else (((1,), (0,)), ((), ()))
    j = pl.program_id(1)
    tm, d = x_refs[0].shape

    def emit(h, rows, w_ref):
        acc = lax.dot_general(h, w_ref[...].astype(BF16), dims, preferred_element_type=F32)
        if has_scale:
            acc = acc * cs_ref[...]
        o_ref[rows, :] = acc.astype(o_ref.dtype)

    def first_tile(x_ref):
        def consume(rows, h):
            h_ref[rows, :] = h
            emit(h, rows, w_refs[0])
            if has_aux:
                oaux_ref[rows, :] = lax.dot_general(h, waux_ref[...], dims, preferred_element_type=F32)

        _norm_mod_pipeline(x_ref, g_ref, mod_ref, 0, 1, first_step_chunks, consume)

    if n_x == 1:
        pl.when(j == 0)(functools.partial(first_tile, x_refs[0]))
    else:
        is_lat = pl.program_id(0) < n_lat_tiles
        pl.when(jnp.logical_and(j == 0, is_lat))(functools.partial(first_tile, x_refs[0]))
        pl.when(jnp.logical_and(j == 0, jnp.logical_not(is_lat)))(functools.partial(first_tile, x_refs[1]))

    for p in range(max(n_w, 1)):
        lo = max(p * tiles_per_part, 1)
        in_part = (j >= lo) if n_w == 0 else jnp.logical_and(j >= lo, j < (p + 1) * tiles_per_part)
        pl.when(in_part)(lambda p=p: emit(h_ref[...], slice(None), w_refs[p]))


def _proj(x, g, mod, layer, w, w_aux, out_dtype, seq, n_batch, col_scale=None, tm=1024, tn=1024):
    parts = list(w) if isinstance(w, (list, tuple)) else None
    if parts is None:
        d, n = w.shape
        n_w, tiles_per_part = 0, 0
    else:
        d = parts[0].shape[1]
        n_w, tiles_per_part = len(parts), parts[0].shape[0] // tn
        assert all(p.shape == (tiles_per_part * tn, d) for p in parts)
        n = n_w * tiles_per_part * tn
    m = sum(a.shape[0] for a in x) if isinstance(x, tuple) else x.shape[0]
    n_lat_tiles = (n_batch * seq) // tm
    has_aux = w_aux is not None
    has_scale = col_scale is not None
    mod_spec = pl.BlockSpec((None, None, 1, mod.shape[-1]),
                            lambda i, j: (layer, _mod_row(i, tm, seq, n_batch), 0, 0))
    in_specs, args = _row_operand(x, tm, n_lat_tiles, d, lambda j: 0, single_ctx_buffer=True)
    n_x = len(args)
    in_specs += [pl.BlockSpec((1, d), lambda i, j: (0, 0)), mod_spec]
    args += [g.reshape(1, d), mod]
    if parts is None:
        in_specs.append(pl.BlockSpec((d, tn), lambda i, j: (0, j)))
        args.append(w)
    else:
        for p, part in enumerate(parts):
            in_specs.append(pl.BlockSpec(
                (tn, d), lambda i, j, p=p: (jnp.clip(j - p * tiles_per_part, 0, tiles_per_part - 1), 0)))
            args.append(part)
    out_specs = [pl.BlockSpec((tm, tn), lambda i, j: (i, j))]
    out_shape = [jax.ShapeDtypeStruct((m, n), out_dtype)]
    if has_scale:
        in_specs.append(pl.BlockSpec((1, tn), lambda i, j: (0, j)))
        args.append(col_scale.reshape(1, n))
    if has_aux:
        na = w_aux.shape[0] if parts is not None else w_aux.shape[1]
        in_specs.append(pl.BlockSpec(w_aux.shape, lambda i, j: (0, 0)))
        out_specs.append(pl.BlockSpec((tm, na), lambda i, j: (i, 0)))
        out_shape.append(jax.ShapeDtypeStruct((m, na), F32))
        args.append(w_aux)
    w_itemsize = (parts[0] if parts is not None else w).dtype.itemsize
    block_bytes = n_x * tm * d * 4 + max(n_w, 1) * d * tn * w_itemsize + tm * tn * 4 + tm * d
    outs = pl.pallas_call(
        functools.partial(_proj_kernel, n_x=n_x, n_w=n_w, tiles_per_part=tiles_per_part, n_lat_tiles=n_lat_tiles,
                          has_aux=has_aux, has_scale=has_scale),
        grid=(m // tm, n // tn),
        in_specs=in_specs,
        out_specs=out_specs,
        out_shape=out_shape,
        scratch_shapes=[pltpu.VMEM((tm, d), BF16)],
        compiler_params=_params(("arbitrary", "arbitrary"), block_bytes),
        name="proj",
    )(*args)
    return outs if has_aux else outs[0]


def _row_operand(a, tm, n_lat_tiles, width, col_of_j, single_ctx_buffer=False):
    if not isinstance(a, tuple):
        return [pl.BlockSpec((tm, width), lambda i, j: (i, col_of_j(j)))], [a]
    lat, ctx = a
    assert ctx.shape[0] % tm == 0 and lat.shape[0] == n_lat_tiles * tm

    def lat_index(i, j):
        return jnp.minimum(i, n_lat_tiles - 1), jnp.where(i < n_lat_tiles, col_of_j(j), 0)

    def ctx_index(i, j):
        return jnp.maximum(i - n_lat_tiles, 0), jnp.where(i < n_lat_tiles, 0, col_of_j(j))

    ctx_mode = dict(pipeline_mode=pl.Buffered(1)) if single_ctx_buffer else {}
    return [pl.BlockSpec((tm, width), lat_index), pl.BlockSpec((tm, width), ctx_index, **ctx_mode)], [lat, ctx]


def _pick_rows(refs, n_lat_tiles):
    if len(refs) == 1:
        return refs[0][...]
    return jnp.where(pl.program_id(0) < n_lat_tiles, refs[0][...], refs[1][...])


def _out_proj_kernel(*refs, arity, n_lat_tiles):
    refs = list(refs)
    n_lhs = len(arity) - 1
    groups = [[refs.pop(0) for _ in range(n)] for n in arity[:-1]]
    ws = [refs.pop(0) for _ in range(n_lhs)]
    res = [refs.pop(0) for _ in range(arity[-1])]
    gate_ref, o_ref = refs
    acc = None
    for grp, w_ref in zip(groups, ws):
        part = jnp.dot(_pick_rows(grp, n_lat_tiles), w_ref[...].astype(BF16), preferred_element_type=F32)
        acc = part if acc is None else acc + part
    o_ref[...] = _pick_rows(res, n_lat_tiles) + gate_ref[...] * acc


def _out_proj(lhs_list, w, res, mod, layer, gate_idx, m_rows, seq, n_batch, tm=512, tn=2048):
    d = w.shape[1]
    n_lat_tiles = (n_batch * seq) // tm
    in_specs, args, arity, widths = [], [], [], []
    for a in lhs_list:
        kk = (a[0] if isinstance(a, tuple) else a).shape[1]
        sp, ar = _row_operand(a, tm, n_lat_tiles, kk, lambda j: 0)
        in_specs += sp
        args += ar
        arity.append(len(ar))
        widths.append(kk)
    row0 = 0
    for kk in widths:
        in_specs.append(pl.BlockSpec((kk, tn), lambda i, j, rb=row0 // kk: (rb, j)))
        args.append(w)
        row0 += kk
    sp, ar = _row_operand(res, tm, n_lat_tiles, tn, lambda j: j)
    in_specs += sp
    args += ar
    arity.append(len(ar))
    in_specs.append(pl.BlockSpec((None, None, 1, tn),
                                 lambda i, j: (layer, _mod_row(i, tm, seq, n_batch), 0, gate_idx * (d // tn) + j)))
    args.append(mod)
    k_total = sum(widths)
    block_bytes = 2 * tm * k_total * 2 + k_total * tn * w.dtype.itemsize + 3 * tm * tn * 4
    return pl.pallas_call(
        functools.partial(_out_proj_kernel, arity=tuple(arity), n_lat_tiles=n_lat_tiles),
        grid=(m_rows // tm, d // tn),
        in_specs=in_specs,
        out_specs=pl.BlockSpec((tm, tn), lambda i, j: (i, j)),
        out_shape=jax.ShapeDtypeStruct((m_rows, d), F32),
        compiler_params=_params(("arbitrary", "arbitrary"), block_bytes),
        name="out_proj",
    )(*args)


def _ffn_kernel(x_ref, g_ref, mod_ref, w1_ref, w3_ref, w2_ref, fg_ref, o_ref, h_ref, *, final_norm, rows,
                first_step_chunks):
    j = pl.program_id(1)
    tm, d = x_ref.shape

    def swiglu(h, w1, w3, w2):
        a = jnp.dot(h, w1, preferred_element_type=F32)
        b = jnp.dot(h, w3, preferred_element_type=F32)
        return jnp.dot((_silu(a) * b).astype(BF16), w2, preferred_element_type=F32)

    @pl.when(j == 0)
    def _():
        w1, w3, w2 = (w_ref[...].astype(BF16) for w_ref in (w1_ref, w3_ref, w2_ref))

        def consume(rows, h):
            h_ref[rows, :] = h
            o_ref[rows, :] = swiglu(h, w1, w3, w2)

        _norm_mod_pipeline(x_ref, g_ref, mod_ref, 3, 4, first_step_chunks, consume)

    @pl.when(j > 0)
    def _():
        o_ref[...] += swiglu(h_ref[...], w1_ref[...].astype(BF16), w3_ref[...].astype(BF16),
                             w2_ref[...].astype(BF16))

    @pl.when(j == pl.num_programs(1) - 1)
    def _():
        gate = mod_ref[:, 5 * d:6 * d]
        fg = fg_ref[...]

        def body(r, carry):
            sl = pl.ds(pl.multiple_of(r * rows, rows), rows)
            y = x_ref[sl, :] + gate * o_ref[sl, :]
            if final_norm:
                y = y * lax.rsqrt(jnp.mean(y * y, axis=-1, keepdims=True) + NORM_EPS) * fg
            o_ref[sl, :] = y
            return carry

        lax.fori_loop(0, x_ref.shape[0] // rows, body, 0)


def _ffn(x, g, mod, layer, w1, w3, w2, final_g, final_norm, m_rows, seq, n_batch, tm=1024, tf=256):
    d = x.shape[1]
    ff = w1.shape[2]
    block_bytes = 2 * tm * d * 4 + d * tf * sum(w.dtype.itemsize for w in (w1, w3, w2)) + tm * d
    return pl.pallas_call(
        functools.partial(_ffn_kernel, final_norm=final_norm, rows=128, first_step_chunks=4),
        grid=(m_rows // tm, ff // tf),
        in_specs=[
            pl.BlockSpec((tm, d), lambda i, j: (i, 0)),
            pl.BlockSpec((1, d), lambda i, j: (0, 0)),
            pl.BlockSpec((None, None, 1, mod.shape[-1]),
                         lambda i, j: (layer, _mod_row(i, tm, seq, n_batch), 0, 0)),
            pl.BlockSpec((None, d, tf), lambda i, j: (layer, 0, j)),
            pl.BlockSpec((None, d, tf), lambda i, j: (layer, 0, j)),
            pl.BlockSpec((None, tf, d), lambda i, j: (layer, j, 0)),
            pl.BlockSpec((1, d), lambda i, j: (0, 0)),
        ],
        out_specs=pl.BlockSpec((tm, d), lambda i, j: (i, 0)),
        out_shape=jax.ShapeDtypeStruct((m_rows, d), F32),
        scratch_shapes=[pltpu.VMEM((tm, d), BF16)],
        compiler_params=_params(("arbitrary", "arbitrary"), block_bytes),
        name="ffn",
    )(x, g.reshape(1, d), mod, w1, w3, w2, final_g.reshape(1, d))


def _conv_kernel(prev_ref, cur_ref, next_ref, w_ref, b_ref, o_ref, ext_ref, *, tm, halo, seq, ctx_len, n_lat_tiles):
    i = pl.program_id(0)
    half = SSM_CONV // 2
    ext_ref[halo:halo + tm, :] = cur_ref[...]

    def taps(mask_of):
        acc = jnp.zeros(cur_ref.shape, F32) + b_ref[...]
        for t in range(SSM_CONV):
            tap = ext_ref[halo - half + t:halo - half + t + tm, :]
            acc = acc + mask_of(t, tap) * w_ref[t:t + 1, :]
        o_ref[...] = _silu(acc)

    @pl.when(i < n_lat_tiles)
    def _():
        prev_in_seg = ((i * tm) & (seq - 1)) != 0
        next_in_seg = (((i + 1) * tm) & (seq - 1)) != 0
        ext_ref[0:halo, :] = jnp.where(prev_in_seg, prev_ref[...], 0.0)
        ext_ref[halo + tm:, :] = jnp.where(next_in_seg, next_ref[...], 0.0)
        taps(lambda t, tap: tap)

    @pl.when(i >= n_lat_tiles)
    def _():
        ext_ref[0:halo, :] = prev_ref[...]
        ext_ref[halo + tm:, :] = next_ref[...]
        pos = (i * tm + lax.broadcasted_iota(jnp.int32, (tm, 1), 0)) & (ctx_len - 1)

        def masked(t, tap):
            src = pos + (t - half)
            return jnp.where((src >= 0) & (src < ctx_len), tap, 0.0)

        taps(masked)


def _ssd_conv(p0, col0, width, conv_w, conv_b, seq, ctx_len, n_lat_rows, tm=1024, tc=1024, halo=8):
    m = p0.shape[0]
    assert seq % tm == 0 and tm % ctx_len == 0 and (seq & (seq - 1)) == 0 and (ctx_len & (ctx_len - 1)) == 0
    cb0 = col0 // tc
    hb = tm // halo
    last_hb = m // halo - 1
    wpad = jnp.zeros((SUBLANES, width), F32).at[:SSM_CONV].set(conv_w)
    return pl.pallas_call(
        functools.partial(_conv_kernel, tm=tm, halo=halo, seq=seq, ctx_len=ctx_len, n_lat_tiles=n_lat_rows // tm),
        grid=(m // tm, width // tc),
        in_specs=[
            pl.BlockSpec((halo, tc), lambda i, j: (jnp.maximum(i * hb - 1, 0), cb0 + j)),
            pl.BlockSpec((tm, tc), lambda i, j: (i, cb0 + j)),
            pl.BlockSpec((halo, tc), lambda i, j: (jnp.minimum((i + 1) * hb, last_hb), cb0 + j)),
            pl.BlockSpec((SUBLANES, tc), lambda i, j: (0, j)),
            pl.BlockSpec((1, tc), lambda i, j: (0, j)),
        ],
        out_specs=pl.BlockSpec((tm, tc), lambda i, j: (i, j)),
        out_shape=jax.ShapeDtypeStruct((m, width), F32),
        scratch_shapes=[pltpu.VMEM((tm + 2 * halo, tc), F32)],
        compiler_params=_params(("arbitrary", "arbitrary"), 3 * tm * tc * 4),
        name="ssd_conv",
    )(p0, p0, p0, wpad, conv_b.reshape(1, width))


def _ssd_scan_kernel(xs_f, bm_f, cm_f, dt_f, xs_b, bm_b, cm_b, dt_b, prm_ref, yf_ref, yb_ref, state_ref, *, n_heads):
    @pl.when(pl.program_id(1) == 0)
    def _():
        state_ref[...] = jnp.zeros_like(state_ref)

    fwd = _ssd_chunk(xs_f, bm_f, cm_f, dt_f, prm_ref.at[0], yf_ref, state_ref.at[0], backward=False, n_heads=n_heads)
    bwd = _ssd_chunk(xs_b, bm_b, cm_b, dt_b, prm_ref.at[1], yb_ref, state_ref.at[1], backward=True, n_heads=n_heads)
    for g in range(SSM_GROUPS):
        fwd(g)
        bwd(g)


def _ssd_chunk(xs_ref, bm_ref, cm_ref, dt_ref, prm_ref, y_ref, state_ref, *, backward, n_heads):
    hp = SSM_HEAD_DIM
    rep = n_heads // SSM_GROUPS
    lc = SSM_CHUNK
    assert 2 * hp == LANES and rep % 2 == 0 and SSM_STATE == LANES and lc == LANES

    dt_in = dt_ref[...] + prm_ref[1:2, :]
    dt = jnp.maximum(dt_in, 0.0) + jnp.log1p(jnp.exp(-jnp.abs(dt_in)))
    dta = dt * (-jnp.exp(prm_ref[0:1, :]) * LOG2E)
    row = lax.broadcasted_iota(jnp.int32, (lc, lc), 0)
    col = lax.broadcasted_iota(jnp.int32, (lc, lc), 1)
    causal = (row <= col) if backward else (row >= col)
    a_cum = jnp.dot(causal.astype(F32), dta, preferred_element_type=F32, precision=lax.Precision.HIGHEST)
    a_tot = a_cum[0:1, :] if backward else a_cum[lc - 1:lc, :]
    w_end = dt * jnp.exp2(a_tot - a_cum)
    a_cum_t, dt_t, w_end_t = a_cum.T, dt.T, w_end.T
    lo = col < hp

    def do_group(g):
        gs = slice(g * SSM_STATE, (g + 1) * SSM_STATE)
        b_f = bm_ref[:, gs]
        c_g = cm_ref[:, gs].astype(BF16)
        cb = lax.dot_general(c_g, b_f.astype(BF16), (((1,), (1,)), ((), ())), preferred_element_type=F32)
        b_t = b_f.T
        cols_g = slice(g * rep * hp, (g + 1) * rep * hp)
        y_off = jnp.dot(c_g, state_ref[:, cols_g].astype(BF16), preferred_element_type=F32)
        for pr in range(rep // 2):
            cols = slice(g * rep * hp + pr * LANES, g * rep * hp + (pr + 1) * LANES)
            xs2 = xs_ref[:, cols].astype(BF16)
            y_d, s_n, e_a = [], [], []
            for h in (g * rep + 2 * pr, g * rep + 2 * pr + 1):
                a_col = jnp.broadcast_to(a_cum[:, h:h + 1], (lc, lc))
                decay = jnp.exp2(jnp.where(causal, a_col - a_cum_t[h:h + 1, :], -jnp.inf))
                m = (cb * decay * dt_t[h:h + 1, :]).astype(BF16)
                y_d.append(jnp.dot(m, xs2, preferred_element_type=F32))
                s_n.append(jnp.dot((b_t * w_end_t[h:h + 1, :]).astype(BF16), xs2, preferred_element_type=F32))
                e_a.append(jnp.exp2(a_col))
            h0 = g * rep + 2 * pr
            y_ref[:, cols] = (jnp.where(lo, y_d[0], y_d[1])
                              + y_off[:, pr * LANES:(pr + 1) * LANES] * jnp.where(lo, e_a[0], e_a[1]))
            chunk_decay = jnp.where(lo[0:1, :], jnp.exp2(a_tot[:, h0:h0 + 1]), jnp.exp2(a_tot[:, h0 + 1:h0 + 2]))
            state_ref[:, cols] = state_ref[:, cols] * chunk_decay + jnp.where(lo, s_n[0], s_n[1])

    return do_group


def _ssd_scan(xbc, dt_raw, prm, n_batch, seq, ctx_len, d_ssm):
    m = xbc.shape[0]
    lc = SSM_CHUNK
    n_heads = d_ssm // SSM_HEAD_DIM
    nc_ctx, nc_lat = ctx_len // lc, seq // lc
    ctx_blk0 = n_batch * nc_lat
    gn = SSM_GROUPS * SSM_STATE

    def row_blk(b, s, backward):
        ctx_c = nc_ctx - 1 - s if backward else s
        lat_c = nc_lat - 1 - (s - nc_ctx) if backward else s - nc_ctx
        return jnp.where(s < nc_ctx, ctx_blk0 + b * nc_ctx + ctx_c, b * nc_lat + lat_c)

    def chunk_specs(backward):
        r = int(backward)
        return [
            pl.BlockSpec((lc, d_ssm), lambda b, s: (row_blk(b, s, backward), 0)),
            pl.BlockSpec((lc, gn), lambda b, s: (row_blk(b, s, backward), d_ssm // gn)),
            pl.BlockSpec((lc, gn), lambda b, s: (row_blk(b, s, backward), d_ssm // gn + 1)),
            pl.BlockSpec((lc, LANES), lambda b, s: (row_blk(b, s, backward), r)),
        ]

    return pl.pallas_call(
        functools.partial(_ssd_scan_kernel, n_heads=n_heads),
        grid=(n_batch, nc_ctx + nc_lat),
        in_specs=chunk_specs(False) + chunk_specs(True) + [pl.BlockSpec((2, SUBLANES, LANES), lambda b, s: (0, 0, 0))],
        out_specs=[pl.BlockSpec((lc, d_ssm), lambda b, s: (row_blk(b, s, False), 0)),
                   pl.BlockSpec((lc, d_ssm), lambda b, s: (row_blk(b, s, True), 0))],
        out_shape=[jax.ShapeDtypeStruct((m, d_ssm), F32), jax.ShapeDtypeStruct((m, d_ssm), F32)],
        scratch_shapes=[pltpu.VMEM((2, SSM_STATE, d_ssm), F32)],
        compiler_params=_params(("arbitrary", "arbitrary"), 8 * lc * d_ssm * 4),
        name="ssd_scan",
    )(xbc, xbc, xbc, dt_raw, xbc, xbc, xbc, dt_raw, prm)


def _ssd_finish_kernel(yf_ref, yb_ref, xs_ref, z_ref, dskip_ref, g_ref, o_ref, *, group_width):
    y = (yf_ref[...] + yb_ref[...] + dskip_ref[...] * xs_ref[...]) * _silu(z_ref[...])
    for g in range(y.shape[1] // group_width):
        sl = slice(g * group_width, (g + 1) * group_width)
        v = y[:, sl]
        vn = v * lax.rsqrt(jnp.mean(v * v, axis=-1, keepdims=True) + NORM_EPS) * g_ref[:, sl]
        o_ref[:, sl] = vn.astype(o_ref.dtype)


def _ssd_finish(y2, xbc, p0, d_skip_lanes, norm_g, d_ssm, tm=512):
    m = xbc.shape[0]
    return pl.pallas_call(
        functools.partial(_ssd_finish_kernel, group_width=d_ssm // SSM_GROUPS),
        grid=(m // tm,),
        in_specs=[
            pl.BlockSpec((tm, d_ssm), lambda i: (i, 0)),
            pl.BlockSpec((tm, d_ssm), lambda i: (i, 0)),
            pl.BlockSpec((tm, d_ssm), lambda i: (i, 0)),
            pl.BlockSpec((tm, d_ssm), lambda i: (i, 0)),
            pl.BlockSpec((1, d_ssm), lambda i: (0, 0)),
            pl.BlockSpec((1, d_ssm), lambda i: (0, 0)),
        ],
        out_specs=pl.BlockSpec((tm, d_ssm), lambda i: (i, 0)),
        out_shape=jax.ShapeDtypeStruct((m, d_ssm), BF16),
        compiler_params=_params(("arbitrary",), 5 * tm * d_ssm * 4),
        name="ssd_finish",
    )(*y2, xbc, p0, d_skip_lanes.reshape(1, d_ssm), norm_g.reshape(1, d_ssm))


def _rope(x, cos, sin_up, sin_dn):
    quarter = DIFF_HEAD_DIM // 4
    return (x * cos + pltpu.roll(x, LANES - quarter, axis=1) * sin_up + pltpu.roll(x, quarter, axis=1) * sin_dn)


def _diff_attn_heads_kernel(*refs, heads_per_step, with_lat, **kw):
    n_head_refs = 5 if with_lat else 3
    *ins, o_ref, k_scr, vt_scr = refs
    for h in range(heads_per_step):
        cs = slice(h * LANES, (h + 1) * LANES)
        head_ins = [r.at[:, cs] for r in ins[:n_head_refs]] + list(ins[n_head_refs:])
        _diff_attn_kernel(*head_ins, o_ref.at[:, cs], k_scr.at[h], vt_scr.at[h], with_lat=with_lat, **kw)


def _diff_attn_kernel(*refs, with_lat, lambda_init, n_ctx, sub_q, key_chunk=768):
    if with_lat:
        (q_ref, kc_ref, kl_ref, vc_ref, vl_ref, qcos_ref, qsu_ref, qsd_ref, kcos_ref, ksu_ref, ksd_ref,
         lam_ref, g_ref, o_ref, k_scr, vt_scr) = refs
    else:
        q_ref, kc_ref, vc_ref, lam_ref, g_ref, o_ref, k_scr, vt_scr = refs

    @pl.when(pl.program_id(2) == 0)
    def _():
        k_scr[0:n_ctx, :] = kc_ref[...].astype(BF16)
        vt_scr[:, 0:n_ctx] = vc_ref[...].T.astype(BF16)
        if with_lat:
            k_scr[n_ctx:, :] = _rope(kl_ref[...], kcos_ref[...], ksu_ref[...], ksd_ref[...]).astype(BF16)
            vt_scr[:, n_ctx:] = vl_ref[...].T.astype(BF16)

    lam_p = lam_ref[...]
    lam = (jnp.exp(jnp.sum(lam_p[0:1, :] * lam_p[1:2, :], axis=-1, keepdims=True))
           - jnp.exp(jnp.sum(lam_p[2:3, :] * lam_p[3:4, :], axis=-1, keepdims=True)) + lambda_init)
    first = lax.broadcasted_iota(jnp.int32, (LANES, sub_q), 0) < DIFF_HEAD_DIM
    n_sub = q_ref.shape[0] // sub_q

    def scores(t):
        rows = slice(t * sub_q, (t + 1) * sub_q)
        q = q_ref[rows, :]
        if with_lat:
            q = _rope(q, qcos_ref[rows, :], qsu_ref[rows, :], qsd_ref[rows, :])
        qt = (q * (DIFF_HEAD_DIM ** -0.5 * LOG2E)).T
        q2t = jnp.concatenate([jnp.where(first, qt, 0.0), jnp.where(first, 0.0, qt)], axis=1).astype(BF16)
        blocks = [jnp.dot(k_scr[k0:k1, :], q2t, preferred_element_type=F32) for k0, k1 in key_chunks]
        return blocks, [jnp.max(b, axis=0, keepdims=True) for b in blocks]

    n_keys = k_scr.shape[0]
    key_chunks = [(k0, min(k0 + key_chunk, n_keys)) for k0 in range(0, n_keys, key_chunk)]
    s_next = scores(0)
    for t in range(n_sub):
        rows = slice(t * sub_q, (t + 1) * sub_q)
        s_blocks, maxes = s_next
        if t + 1 < n_sub:
            s_next = scores(t + 1)
        m = functools.reduce(jnp.maximum, maxes)
        ovt, l = None, None
        for (k0, k1), s in zip(key_chunks, s_blocks):
            e = jnp.exp2(s - m)
            part = jnp.dot(vt_scr[:, k0:k1], e.astype(BF16), preferred_element_type=F32)
            part_l = jnp.sum(e, axis=0, keepdims=True)
            ovt, l = (part, part_l) if ovt is None else (ovt + part, l + part_l)
        ovt = ovt * (1.0 / l)
        ot = ovt[:, :sub_q] - lam * ovt[:, sub_q:]
        ot = ot * lax.rsqrt(jnp.mean(ot * ot, axis=0, keepdims=True) + NORM_EPS) * g_ref[...] * (1.0 - lambda_init)
        o_ref[rows, :] = ot.T.astype(o_ref.dtype)


def _diff_attn(p0, q_col0, k_col0, v_col0, n_heads, rope_tabs, lam_p, subln_g, lambda_init,
               n_batch, seq, ctx_len, with_lat, tq=2048, sub_q=256):
    ctx_rb0 = (n_batch * seq) // ctx_len
    n_keys = ctx_len + (seq if with_lat else 0)
    hps = 1 if with_lat else n_heads
    width = hps * LANES
    cb = lambda c0: c0 // width
    if with_lat:
        n_q = seq // tq
        q_spec = pl.BlockSpec((tq, width), lambda b, h, i: (b * n_q + i, cb(q_col0) + h))
    else:
        tq = ctx_len
        n_q = 1
        q_spec = pl.BlockSpec((tq, width), lambda b, h, i: (ctx_rb0 + b, cb(q_col0) + h))
    kc_spec = pl.BlockSpec((ctx_len, width), lambda b, h, i: (ctx_rb0 + b, cb(k_col0) + h))
    vc_spec = pl.BlockSpec((ctx_len, width), lambda b, h, i: (ctx_rb0 + b, cb(v_col0) + h))
    small = [pl.BlockSpec((SUBLANES, LANES), lambda b, h, i: (0, 0)),
             pl.BlockSpec((LANES, sub_q), lambda b, h, i: (0, 0))]
    small_args = [lam_p, jnp.broadcast_to(subln_g[:, None], (LANES, sub_q))]
    if with_lat:
        kl_spec = pl.BlockSpec((seq, width), lambda b, h, i: (b, cb(k_col0) + h))
        vl_spec = pl.BlockSpec((seq, width), lambda b, h, i: (b, cb(v_col0) + h))
        qt = pl.BlockSpec((tq, LANES), lambda b, h, i: (i, 0))
        kt = pl.BlockSpec((seq, LANES), lambda b, h, i: (0, 0))
        in_specs = [q_spec, kc_spec, kl_spec, vc_spec, vl_spec, qt, qt, qt, kt, kt, kt] + small
        args = [p0, p0, p0, p0, p0, *rope_tabs, *rope_tabs] + small_args
    else:
        in_specs = [q_spec, kc_spec, vc_spec] + small
        args = [p0, p0, p0] + small_args
    block_bytes = 10 * seq * LANES * 4 if with_lat else 8 * ctx_len * width * 4
    return pl.pallas_call(
        functools.partial(_diff_attn_heads_kernel, heads_per_step=hps, with_lat=with_lat, lambda_init=lambda_init,
                          n_ctx=ctx_len, sub_q=sub_q),
        grid=(n_batch, n_heads // hps, n_q),
        in_specs=in_specs,
        out_specs=pl.BlockSpec((tq, width), lambda b, h, i: (b * n_q + i, h)),
        out_shape=jax.ShapeDtypeStruct((n_batch * n_q * tq, n_heads * LANES), BF16),
        scratch_shapes=[pltpu.VMEM((hps, n_keys, LANES), BF16), pltpu.VMEM((hps, LANES, n_keys), BF16)],
        compiler_params=_params(("arbitrary", "arbitrary", "arbitrary"), block_bytes),
        name="diff_attn_lat" if with_lat else "diff_attn_ctx",
    )(*args)


def _rope_tables(seq):
    half = DIFF_HEAD_DIM // 2
    pos = jnp.arange(seq)
    row, col = pos // GRID_W, pos % GRID_W
    inv_freq = ROPE_BASE ** (-jnp.arange(0, half, 2, dtype=F32) / half)
    lane = np.arange(LANES) % DIFF_HEAD_DIM
    use_col = jnp.asarray(lane >= half)
    first = jnp.asarray((lane % half) < half // 2)
    freq = inv_freq[jnp.asarray(lane % (half // 2))]
    p = jnp.where(use_col[None, :], col[:, None], row[:, None]).astype(F32)
    ang = p * freq[None, :]
    cos, sin = jnp.cos(ang), jnp.sin(ang)
    return cos, jnp.where(first[None, :], -sin, 0.0), jnp.where(first[None, :], 0.0, sin)


def _rpb_gather_kernel(rpb_ref, o_ref):
    n = o_ref.shape[1]
    k = rpb_ref.shape[1]
    colid = pl.program_id(0) * n + lax.broadcasted_iota(jnp.int32, (k, n), 1)
    j = lax.broadcasted_iota(jnp.int32, (k, n), 0)
    qc = lax.shift_right_logical(colid, int(math.log2(LANES)))
    half = lax.shift_right_logical(colid & (LANES - 1), int(math.log2(GRID_W)))
    kc = colid & (GRID_W - 1)
    sel = (j == half * LANES + jnp.clip(kc - qc + (WIN_COLS - 1), 0, 2 * WIN_COLS - 2)).astype(F32)
    o_ref[...] = jnp.dot(rpb_ref[...], sel, preferred_element_type=F32, precision=lax.Precision.HIGHEST)


def _rpb_pairs(rpb):
    nh, nr, ncol = rpb.shape
    n_slot = nr + 1
    left = jnp.pad(rpb, ((0, 0), (1, 0), (0, LANES - ncol)))
    right = jnp.pad(rpb, ((0, 0), (0, 1), (0, LANES - ncol)))
    rows = jnp.concatenate([left, right], axis=2).reshape(nh * n_slot, 2 * LANES)
    tn = 1024
    out = pl.pallas_call(
        _rpb_gather_kernel,
        grid=(GRID_W * LANES // tn,),
        in_specs=[pl.BlockSpec((nh * n_slot, 2 * LANES), lambda j: (0, 0))],
        out_specs=pl.BlockSpec((nh * n_slot, tn), lambda j: (0, j)),
        out_shape=jax.ShapeDtypeStruct((nh * n_slot, GRID_W * LANES), F32),
        compiler_params=_params(("arbitrary",), nh * n_slot * tn * 4 + 2 * LANES * tn * 4),
        name="rpb_gather",
    )(rows)
    return out.reshape(nh, n_slot, GRID_W, LANES)


def _na_block_plan(rows):
    kr = min(WIN_ROWS, rows)
    n_blk = rows // NA_Q_ROWS
    kb = np.clip(np.arange(n_blk) * NA_Q_ROWS - kr // 2, 0, rows - NA_K_ROWS)
    layouts, layout_of = [], []
    for blk in range(n_blk):
        dr = np.full((NA_Q_ROWS, NA_K_ROWS), -1, np.int64)
        for i in range(NA_Q_ROWS):
            r = blk * NA_Q_ROWS + i
            rs = int(np.clip(r - kr // 2, 0, rows - kr))
            for j in range(NA_K_ROWS):
                krow = kb[blk] + j
                if rs <= krow < rs + kr:
                    dr[i, j] = krow - r + WIN_ROWS - 1
        key = dr.tobytes()
        if key not in [l.tobytes() for l in layouts]:
            layouts.append(dr)
        layout_of.append([l.tobytes() for l in layouts].index(key))
    return kb, np.stack(layouts), np.asarray(layout_of)


def _na_bias_kernel(tp_ref, o_ref, *, layouts):
    qc = lax.broadcasted_iota(jnp.int32, (GRID_W, LANES), 0)
    lane = lax.broadcasted_iota(jnp.int32, (GRID_W, LANES), 1)
    kc = lane & (GRID_W - 1)
    left = lane < GRID_W
    col_start = jnp.clip(qc - WIN_COLS // 2, 0, GRID_W - WIN_COLS)
    col_ok = (kc >= col_start) & (kc < col_start + WIN_COLS)
    n_lay, n_q, n_k = layouts.shape
    for lay in range(n_lay):
        for i in range(n_q):
            for p in range(-(-n_k // 2)):
                d_l = int(layouts[lay, i, 2 * p])
                d_r = int(layouts[lay, i, 2 * p + 1]) if 2 * p + 1 < n_k else -1
                width = LANES if 2 * p + 1 < n_k else GRID_W
                if d_l < 0 and d_r < 0:
                    tile = jnp.full((GRID_W, LANES), -jnp.inf, F32)
                else:
                    assert d_l < 0 or d_r < 0 or d_r == d_l + 1
                    ok = col_ok
                    if d_l < 0:
                        ok = ok & jnp.logical_not(left)
                    if d_r < 0:
                        ok = ok & left
                    slot = d_r if d_r >= 0 else d_l + 1
                    tile = jnp.where(ok, tp_ref[slot] * LOG2E, -jnp.inf)
                o_ref[lay, i * GRID_W:(i + 1) * GRID_W, p * LANES:p * LANES + width] = tile[:, :width]


def _na_bias(rpb_pairs, layouts):
    nh, n_slot = rpb_pairs.shape[:2]
    n_lay = layouts.shape[0]
    tq, nkw = NA_Q_ROWS * GRID_W, NA_K_ROWS * GRID_W
    return pl.pallas_call(
        functools.partial(_na_bias_kernel, layouts=layouts),
        grid=(nh,),
        in_specs=[pl.BlockSpec((None, n_slot, GRID_W, LANES), lambda h: (h, 0, 0, 0))],
        out_specs=pl.BlockSpec((None, n_lay, tq, nkw), lambda h: (h, 0, 0, 0)),
        out_shape=jax.ShapeDtypeStruct((nh, n_lay, tq, nkw), F32),
        compiler_params=_params(("arbitrary",), n_lay * tq * nkw * 4),
        name="na_bias",
    )(rpb_pairs)


def _na_kernel(q_ref, kl_ref, vl_ref, kc_ref, vc_ref, bias_ref, o_ref, *, plan):
    tq = NA_Q_ROWS * GRID_W
    nkw = NA_K_ROWS * GRID_W
    nt = (((1,), (1,)), ((), ()))
    kc = kc_ref[...]
    vc = vc_ref[...]

    def scores(blk):
        kb, lay = plan[blk]
        q = q_ref[blk * tq:(blk + 1) * tq, :]
        ks = slice(kb * GRID_W, kb * GRID_W + nkw)
        return (lax.dot_general(q, kl_ref[ks, :], nt, preferred_element_type=F32) + bias_ref[lay],
                lax.dot_general(q, kc, nt, preferred_element_type=F32))

    def weighted_values(blk, e_w, e_c, l):
        kb, _ = plan[blk]
        ks = slice(kb * GRID_W, kb * GRID_W + nkw)
        o = (jnp.dot(e_w, vl_ref[ks, :], preferred_element_type=F32) + jnp.dot(e_c, vc, preferred_element_type=F32))
        o_ref[blk * tq:(blk + 1) * tq, :] = (o * (1.0 / l)).astype(o_ref.dtype)

    s_next = scores(0)
    pending = None
    for blk in range(len(plan)):
        s_w, s_c = s_next
        if blk + 1 < len(plan):
            s_next = scores(blk + 1)
        if pending is not None:
            weighted_values(blk - 1, *pending)
        m = jnp.maximum(jnp.max(s_w, axis=-1, keepdims=True), jnp.max(s_c, axis=-1, keepdims=True))
        e_w = jnp.exp2(s_w - m)
        e_c = jnp.exp2(s_c - m)
        l = jnp.sum(e_w, axis=-1, keepdims=True) + jnp.sum(e_c, axis=-1, keepdims=True)
        pending = (e_w.astype(BF16), e_c.astype(BF16), l)
    weighted_values(len(plan) - 1, *pending)


def _na_attn(p1, rpb, n_batch, seq, ctx_len, n_heads):
    rows = seq // GRID_W
    kb, layouts, layout_of = _na_block_plan(rows)
    bias = _na_bias(_rpb_pairs(rpb), layouts)
    n_lay = layouts.shape[0]
    tq = NA_Q_ROWS * GRID_W
    nkw = NA_K_ROWS * GRID_W
    ctx_rb0 = (n_batch * seq) // ctx_len
    plan = tuple((int(k), int(l)) for k, l in zip(kb, layout_of))
    return pl.pallas_call(
        functools.partial(_na_kernel, plan=plan),
        grid=(n_heads, n_batch),
        in_specs=[
            pl.BlockSpec((seq, LANES), lambda h, b: (b, h)),
            pl.BlockSpec((seq, LANES), lambda h, b: (b, n_heads + h)),
            pl.BlockSpec((seq, LANES), lambda h, b: (b, 2 * n_heads + h)),
            pl.BlockSpec((ctx_len, LANES), lambda h, b: (ctx_rb0 + b, n_heads + h)),
            pl.BlockSpec((ctx_len, LANES), lambda h, b: (ctx_rb0 + b, 2 * n_heads + h)),
            pl.BlockSpec((None, n_lay, tq, nkw), lambda h, b: (h, 0, 0, 0)),
        ],
        out_specs=pl.BlockSpec((seq, LANES), lambda h, b: (b, h)),
        out_shape=jax.ShapeDtypeStruct((n_batch * seq, n_heads * LANES), BF16),
        compiler_params=_params(("arbitrary", "arbitrary"), n_lay * tq * nkw * 4 + 4 * seq * LANES * 2),
        name="na_attn",
    )(p1, p1, p1, p1, p1, bias)


def _pack_rows(rows):
    out = jnp.zeros((SUBLANES, LANES), F32)
    for r, v in enumerate(rows):
        out = out.at[r, :v.shape[0]].set(v.astype(F32))
    return out


def kernel(x, c, ctx, c_ctx, ada_w, ada_b, norm_mix_g, norm_ffn_g, final_norm_g, ffn_w1, ffn_w3, ffn_w2, ev_w_in, ev_conv_w, ev_conv_b, ev_a_log, ev_dt_bias, ev_d_skip, ev_ssm_norm_g, ev_lam_q1, ev_lam_k1, ev_lam_q2, ev_lam_k2, ev_subln_g, ev_w_out, od_w_in, od_rpb, od_w_out):
    n_batch, seq, d = x.shape
    ctx_len = ctx.shape[1]
    depth = ada_w.shape[0]
    n_lat = n_batch * seq
    n_tok = n_lat + n_batch * ctx_len

    d_ssm = ev_ssm_norm_g.shape[1]
    n_ssm_heads = ev_d_skip.shape[1]
    d_xbc = ev_conv_w.shape[2]
    d_qk = d_v = (ev_w_in.shape[2] - d_ssm - d_xbc - 2 * n_ssm_heads) // 3
    n_diff_heads = d_v // (2 * DIFF_HEAD_DIM)
    n_na_heads = od_rpb.shape[1]

    stream = (x.reshape(n_lat, d), ctx.reshape(n_batch * ctx_len, d))
    cond = jnp.zeros((COND_ROWS, d), F32).at[:n_batch].set(c).at[n_batch].set(c_ctx)
    mod = _ada_mod(cond, ada_w, ada_b).reshape(depth, COND_ROWS, 1, 6 * d)
    rope_tabs = _rope_tables(seq)
    kw = dict(seq=seq, n_batch=n_batch)
    ffn_w = (ffn_w1, ffn_w3, ffn_w2)

    for i in range(depth):
        ctx_out = i < depth - 1
        m_rows = n_tok if ctx_out else n_lat
        j = i // 2
        if i % 2 == 0:
            lambda_init = 0.8 - 0.6 * math.exp(-0.3 * i)
            w_in = ev_w_in[j]
            dt0 = d_ssm + d_xbc
            w_t = jnp.swapaxes(w_in, 0, 1)
            w_parts = [w_t[:dt0].astype(BF16), w_t[dt0 + 2 * n_ssm_heads:].astype(BF16)]
            w_dt = jnp.zeros((2 * LANES, d), F32)
            w_dt = w_dt.at[:n_ssm_heads].set(w_t[dt0:dt0 + n_ssm_heads])
            w_dt = w_dt.at[LANES:LANES + n_ssm_heads].set(w_t[dt0 + n_ssm_heads:dt0 + 2 * n_ssm_heads])
            p0, dt_raw = _proj(stream, norm_mix_g[i], mod, i, w_parts, w_dt.astype(BF16), F32, **kw)
            q0 = d_ssm + d_xbc
            k0, v0 = q0 + d_qk, q0 + 2 * d_qk

            xbc = _ssd_conv(p0, d_ssm, d_xbc, ev_conv_w[j], ev_conv_b[j], seq, ctx_len, n_lat)
            prm = jnp.stack([_pack_rows([ev_a_log[j, r], ev_dt_bias[j, r]]) for r in range(2)])
            y2 = _ssd_scan(xbc, dt_raw, prm, n_batch, seq, ctx_len, d_ssm)
            mix_ssd = _ssd_finish(y2, xbc, p0, jnp.repeat(ev_d_skip[j], SSM_HEAD_DIM), ev_ssm_norm_g[j], d_ssm)

            lam_p = _pack_rows([ev_lam_q1[j], ev_lam_k1[j], ev_lam_q2[j], ev_lam_k2[j]])
            attn_args = (p0, q0, k0, v0, n_diff_heads, rope_tabs, lam_p, ev_subln_g[j], lambda_init,
                         n_batch, seq, ctx_len)
            mix_attn = _diff_attn(*attn_args, with_lat=True)
            if ctx_out:
                mix_attn = (mix_attn, _diff_attn(*attn_args, with_lat=False))
            lhs = [mix_ssd, mix_attn]
            w_out = ev_w_out[j]
        else:
            d_na = n_na_heads * NA_HEAD_DIM
            q_scale = jnp.where(jnp.arange(3 * d_na) < d_na, NA_HEAD_DIM ** -0.5 * LOG2E, 1.0).astype(F32)
            p1 = _proj(stream, norm_mix_g[i], mod, i, od_w_in[j], None, BF16, col_scale=q_scale, **kw)
            assert not ctx_out, "context-query neighbourhood layers are not needed at this depth"
            lhs = [_na_attn(p1, od_rpb[j], n_batch, seq, ctx_len, n_na_heads)]
            w_out = od_w_out[j]
        stream = _out_proj(lhs, w_out, stream, mod, i, 2, m_rows, **kw)
        stream = _ffn(stream, norm_ffn_g[i], mod, i, *ffn_w, final_norm_g, not ctx_out, m_rows, **kw)
    return stream.reshape(n_batch, seq, d)
```

```python
import functools
import math

import jax
import jax.numpy as jnp
import numpy as np
from jax import lax
from jax.experimental import pallas as pl
from jax.experimental.pallas import tpu as pltpu

F32 = jnp.float32
BF16 = jnp.bfloat16

GRID_W = 64
SSM_HEAD_DIM = 64
SSM_GROUPS = 4
SSM_STATE = 128
SSM_CONV = 5
SSM_CHUNK = 128
DIFF_HEAD_DIM = 64
NA_HEAD_DIM = 128
WIN_ROWS = 8
WIN_COLS = 16
ROPE_BASE = 10000.0
NORM_EPS = 1e-6
LOG2E = math.log2(math.e)

LANES = 128
SUBLANES = 8
V7X_VMEM_BYTES = 64 * 1024 * 1024
VMEM_HEADROOM_BYTES = 3 * 1024 * 1024
VMEM_MIN_REQUEST_BYTES = 32 * 1024 * 1024
VMEM_TEMPORARIES_BYTES = 16 * 1024 * 1024

NA_Q_ROWS = 4
NA_K_ROWS = NA_Q_ROWS + WIN_ROWS - 1
COND_ROWS = 16


def _vmem_limit(block_bytes):
    want = max(VMEM_MIN_REQUEST_BYTES, 2 * block_bytes + VMEM_TEMPORARIES_BYTES)
    return int(min(V7X_VMEM_BYTES - VMEM_HEADROOM_BYTES, want))


def _params(semantics, block_bytes):
    return pltpu.CompilerParams(dimension_semantics=semantics, vmem_limit_bytes=_vmem_limit(block_bytes))


def _silu(v):
    return v * jax.nn.sigmoid(v)


def _mod_row(i, tm, seq, n_batch):
    return jnp.minimum((i * tm) // seq, n_batch)


def _ada_kernel(cond_ref, w_ref, b_ref, o_ref):
    s = _silu(cond_ref[...]).astype(BF16)
    o_ref[...] = jnp.dot(s, w_ref[...].astype(BF16), preferred_element_type=F32) + b_ref[...]


def _ada_mod(cond, ada_w, ada_b, tn=1024):
    depth, d, n = ada_w.shape
    return pl.pallas_call(
        _ada_kernel,
        grid=(depth, n // tn),
        in_specs=[
            pl.BlockSpec((COND_ROWS, d), lambda l, j: (0, 0)),
            pl.BlockSpec((None, d, tn), lambda l, j: (l, 0, j)),
            pl.BlockSpec((None, 1, tn), lambda l, j: (l, 0, j)),
        ],
        out_specs=pl.BlockSpec((None, COND_ROWS, tn), lambda l, j: (l, 0, j)),
        out_shape=jax.ShapeDtypeStruct((depth, COND_ROWS, n), F32),
        compiler_params=_params(("arbitrary", "arbitrary"), d * tn * 4),
        name="ada_mod",
    )(cond, ada_w, ada_b.reshape(depth, 1, n))


def _norm_mod_pipeline(x_ref, g_ref, mod_ref, shift_idx, scale_idx, n_chunks, consume):
    tm, d = x_ref.shape
    shift = mod_ref[:, shift_idx * d:(shift_idx + 1) * d]
    gain = g_ref[...] * (1.0 + mod_ref[:, scale_idx * d:(scale_idx + 1) * d])
    chunk = tm // n_chunks

    def norm_rows(c):
        x = x_ref[c * chunk:(c + 1) * chunk, :]
        inv = lax.rsqrt(jnp.mean(x * x, axis=-1, keepdims=True) + NORM_EPS)
        return (x * inv * gain + shift).astype(BF16)

    h_next = norm_rows(0)
    for c in range(n_chunks):
        h = h_next
        if c + 1 < n_chunks:
            h_next = norm_rows(c + 1)
        consume(slice(c * chunk, (c + 1) * chunk), h)


def _proj_kernel(*refs, n_x, n_w, tiles_per_part, n_lat_tiles, has_aux, has_scale, first_step_chunks=4):
    rest = list(refs)
    x_refs = [rest.pop(0) for _ in range(n_x)]
    g_ref, mod_ref = rest.pop(0), rest.pop(0)
    w_refs = [rest.pop(0) for _ in range(max(n_w, 1))]
    cs_ref = rest.pop(0) if has_scale else None
    if has_aux:
        waux_ref, o_ref, oaux_ref, h_ref = rest
    else:
        o_ref, h_ref = rest
    dims = (((1,), (1,)), ((), ())) if n_w else (((1,), (0,)), ((), ()))
    j = pl.program_id(1)
    tm, d = x_refs[0].shape

    def emit(h, rows, w_ref):
        acc = lax.dot_general(h, w_ref[...].astype(BF16), dims, preferred_element_type=F32)
        if has_scale:
            acc = acc * cs_ref[...]
        o_ref[rows, :] = acc.astype(o_ref.dtype)

    def first_tile(x_ref):
        def consume(rows, h):
            h_ref[rows, :] = h
            emit(h, rows, w_refs[0])
            if has_aux:
                oaux_ref[rows, :] = lax.dot_general(h, waux_ref[...], dims, preferred_element_type=F32)

        _norm_mod_pipeline(x_ref, g_ref, mod_ref, 0, 1, first_step_chunks, consume)

    if n_x == 1:
        pl.when(j == 0)(functools.partial(first_tile, x_refs[0]))
    else:
        is_lat = pl.program_id(0) < n_lat_tiles
        pl.when(jnp.logical_and(j == 0, is_lat))(functools.partial(first_tile, x_refs[0]))
        pl.when(jnp.logical_and(j == 0, jnp.logical_not(is_lat)))(functools.partial(first_tile, x_refs[1]))

    for p in range(max(n_w, 1)):
        lo = max(p * tiles_per_part, 1)
        in_part = (j >= lo) if n_w == 0 else jnp.logical_and(j >= lo, j < (p + 1) * tiles_per_part)
        pl.when(in_part)(lambda p=p: emit(h_ref[...], slice(None), w_refs[p]))


def _proj(x, g, mod, layer, w, w_aux, out_dtype, seq, n_batch, col_scale=None, tm=1024, tn=1024):
    parts = list(w) if isinstance(w, (list, tuple)) else None
    if parts is None:
        d, n = w.shape
        n_w, tiles_per_part = 0, 0
    else:
        d = parts[0].shape[1]
        n_w, tiles_per_part = len(parts), parts[0].shape[0] // tn
        assert all(p.shape == (tiles_per_part * tn, d) for p in parts)
        n = n_w * tiles_per_part * tn
    m = sum(a.shape[0] for a in x) if isinstance(x, tuple) else x.shape[0]
    n_lat_tiles = (n_batch * seq) // tm
    has_aux = w_aux is not None
    has_scale = col_scale is not None
    mod_spec = pl.BlockSpec((None, None, 1, mod.shape[-1]),
                            lambda i, j: (layer, _mod_row(i, tm, seq, n_batch), 0, 0))
    in_specs, args = _row_operand(x, tm, n_lat_tiles, d, lambda j: 0, single_ctx_buffer=True)
    n_x = len(args)
    in_specs += [pl.BlockSpec((1, d), lambda i, j: (0, 0)), mod_spec]
    args += [g.reshape(1, d), mod]
    if parts is None:
        in_specs.append(pl.BlockSpec((d, tn), lambda i, j: (0, j)))
        args.append(w)
    else:
        for p, part in enumerate(parts):
            in_specs.append(pl.BlockSpec(
                (tn, d), lambda i, j, p=p: (jnp.clip(j - p * tiles_per_part, 0, tiles_per_part - 1), 0)))
            args.append(part)
    out_specs = [pl.BlockSpec((tm, tn), lambda i, j: (i, j))]
    out_shape = [jax.ShapeDtypeStruct((m, n), out_dtype)]
    if has_scale:
        in_specs.append(pl.BlockSpec((1, tn), lambda i, j: (0, j)))
        args.append(col_scale.reshape(1, n))
    if has_aux:
        na = w_aux.shape[0] if parts is not None else w_aux.shape[1]
        in_specs.append(pl.BlockSpec(w_aux.shape, lambda i, j: (0, 0)))
        out_specs.append(pl.BlockSpec((tm, na), lambda i, j: (i, 0)))
        out_shape.append(jax.ShapeDtypeStruct((m, na), F32))
        args.append(w_aux)
    w_itemsize = (parts[0] if parts is not None else w).dtype.itemsize
    block_bytes = n_x * tm * d * 4 + max(n_w, 1) * d * tn * w_itemsize + tm * tn * 4 + tm * d
    outs = pl.pallas_call(
        functools.partial(_proj_kernel, n_x=n_x, n_w=n_w, tiles_per_part=tiles_per_part, n_lat_tiles=n_lat_tiles,
                          has_aux=has_aux, has_scale=has_scale),
        grid=(m // tm, n // tn),
        in_specs=in_specs,
        out_specs=out_specs,
        out_shape=out_shape,
        scratch_shapes=[pltpu.VMEM((tm, d), BF16)],
        compiler_params=_params(("arbitrary", "arbitrary"), block_bytes),
        name="proj",
    )(*args)
    return outs if has_aux else outs[0]


def _row_operand(a, tm, n_lat_tiles, width, col_of_j, single_ctx_buffer=False):
    if not isinstance(a, tuple):
        return [pl.BlockSpec((tm, width), lambda i, j: (i, col_of_j(j)))], [a]
    lat, ctx = a
    assert ctx.shape[0] % tm == 0 and lat.shape[0] == n_lat_tiles * tm

    def lat_index(i, j):
        return jnp.minimum(i, n_lat_tiles - 1), jnp.where(i < n_lat_tiles, col_of_j(j), 0)

    def ctx_index(i, j):
        return jnp.maximum(i - n_lat_tiles, 0), jnp.where(i < n_lat_tiles, 0, col_of_j(j))

    ctx_mode = dict(pipeline_mode=pl.Buffered(1)) if single_ctx_buffer else {}
    return [pl.BlockSpec((tm, width), lat_index), pl.BlockSpec((tm, width), ctx_index, **ctx_mode)], [lat, ctx]


def _pick_rows(refs, n_lat_tiles):
    if len(refs) == 1:
        return refs[0][...]
    return jnp.where(pl.program_id(0) < n_lat_tiles, refs[0][...], refs[1][...])


def _out_proj_kernel(*refs, arity, n_lat_tiles):
    refs = list(refs)
    n_lhs = len(arity) - 1
    groups = [[refs.pop(0) for _ in range(n)] for n in arity[:-1]]
    ws = [refs.pop(0) for _ in range(n_lhs)]
    res = [refs.pop(0) for _ in range(arity[-1])]
    gate_ref, o_ref = refs
    acc = None
    for grp, w_ref in zip(groups, ws):
        part = jnp.dot(_pick_rows(grp, n_lat_tiles), w_ref[...].astype(BF16), preferred_element_type=F32)
        acc = part if acc is None else acc + part
    o_ref[...] = _pick_rows(res, n_lat_tiles) + gate_ref[...] * acc


def _out_proj(lhs_list, w, res, mod, layer, gate_idx, m_rows, seq, n_batch, tm=512, tn=2048):
    d = w.shape[1]
    n_lat_tiles = (n_batch * seq) // tm
    in_specs, args, arity, widths = [], [], [], []
    for a in lhs_list:
        kk = (a[0] if isinstance(a, tuple) else a).shape[1]
        sp, ar = _row_operand(a, tm, n_lat_tiles, kk, lambda j: 0)
        in_specs += sp
        args += ar
        arity.append(len(ar))
        widths.append(kk)
    row0 = 0
    for kk in widths:
        in_specs.append(pl.BlockSpec((kk, tn), lambda i, j, rb=row0 // kk: (rb, j)))
        args.append(w)
        row0 += kk
    sp, ar = _row_operand(res, tm, n_lat_tiles, tn, lambda j: j)
    in_specs += sp
    args += ar
    arity.append(len(ar))
    in_specs.append(pl.BlockSpec((None, None, 1, tn),
                                 lambda i, j: (layer, _mod_row(i, tm, seq, n_batch), 0, gate_idx * (d // tn) + j)))
    args.append(mod)
    k_total = sum(widths)
    block_bytes = 2 * tm * k_total * 2 + k_total * tn * w.dtype.itemsize + 3 * tm * tn * 4
    return pl.pallas_call(
        functools.partial(_out_proj_kernel, arity=tuple(arity), n_lat_tiles=n_lat_tiles),
        grid=(m_rows // tm, d // tn),
        in_specs=in_specs,
        out_specs=pl.BlockSpec((tm, tn), lambda i, j: (i, j)),
        out_shape=jax.ShapeDtypeStruct((m_rows, d), F32),
        compiler_params=_params(("arbitrary", "arbitrary"), block_bytes),
        name="out_proj",
    )(*args)


def _ffn_kernel(x_ref, g_ref, mod_ref, w1_ref, w3_ref, w2_ref, fg_ref, o_ref, h_ref, *, final_norm, rows,
                first_step_chunks):
    j = pl.program_id(1)
    tm, d = x_ref.shape

    def swiglu(h, w1, w3, w2):
        a = jnp.dot(h, w1, preferred_element_type=F32)
        b = jnp.dot(h, w3, preferred_element_type=F32)
        return jnp.dot((_silu(a) * b).astype(BF16), w2, preferred_element_type=F32)

    @pl.when(j == 0)
    def _():
        w1, w3, w2 = (w_ref[...].astype(BF16) for w_ref in (w1_ref, w3_ref, w2_ref))

        def consume(rows, h):
            h_ref[rows, :] = h
            o_ref[rows, :] = swiglu(h, w1, w3, w2)

        _norm_mod_pipeline(x_ref, g_ref, mod_ref, 3, 4, first_step_chunks, consume)

    @pl.when(j > 0)
    def _():
        o_ref[...] += swiglu(h_ref[...], w1_ref[...].astype(BF16), w3_ref[...].astype(BF16),
                             w2_ref[...].astype(BF16))

    @pl.when(j == pl.num_programs(1) - 1)
    def _():
        gate = mod_ref[:, 5 * d:6 * d]
        fg = fg_ref[...]

        def body(r, carry):
            sl = pl.ds(pl.multiple_of(r * rows, rows), rows)
            y = x_ref[sl, :] + gate * o_ref[sl, :]
            if final_norm:
                y = y * lax.rsqrt(jnp.mean(y * y, axis=-1, keepdims=True) + NORM_EPS) * fg
            o_ref[sl, :] = y
            return carry

        lax.fori_loop(0, x_ref.shape[0] // rows, body, 0)


def _ffn(x, g, mod, layer, w1, w3, w2, final_g, final_norm, m_rows, seq, n_batch, tm=1024, tf=256):
    d = x.shape[1]
    ff = w1.shape[2]
    block_bytes = 2 * tm * d * 4 + d * tf * sum(w.dtype.itemsize for w in (w1, w3, w2)) + tm * d
    return pl.pallas_call(
        functools.partial(_ffn_kernel, final_norm=final_norm, rows=128, first_step_chunks=4),
        grid=(m_rows // tm, ff // tf),
        in_specs=[
            pl.BlockSpec((tm, d), lambda i, j: (i, 0)),
            pl.BlockSpec((1, d), lambda i, j: (0, 0)),
            pl.BlockSpec((None, None, 1, mod.shape[-1]),
                         lambda i, j: (layer, _mod_row(i, tm, seq, n_batch), 0, 0)),
            pl.BlockSpec((None, d, tf), lambda i, j: (layer, 0, j)),
            pl.BlockSpec((None, d, tf), lambda i, j: (layer, 0, j)),
            pl.BlockSpec((None, tf, d), lambda i, j: (layer, j, 0)),
            pl.BlockSpec((1, d), lambda i, j: (0, 0)),
        ],
        out_specs=pl.BlockSpec((tm, d), lambda i, j: (i, 0)),
        out_shape=jax.ShapeDtypeStruct((m_rows, d), F32),
        scratch_shapes=[pltpu.VMEM((tm, d), BF16)],
        compiler_params=_params(("arbitrary", "arbitrary"), block_bytes),
        name="ffn",
    )(x, g.reshape(1, d), mod, w1, w3, w2, final_g.reshape(1, d))


def _conv_kernel(prev_ref, cur_ref, next_ref, w_ref, b_ref, o_ref, ext_ref, *, tm, halo, seq, ctx_len, n_lat_tiles):
    i = pl.program_id(0)
    half = SSM_CONV // 2
    ext_ref[halo:halo + tm, :] = cur_ref[...]

    def taps(mask_of):
        acc = jnp.zeros(cur_ref.shape, F32) + b_ref[...]
        for t in range(SSM_CONV):
            tap = ext_ref[halo - half + t:halo - half + t + tm, :]
            acc = acc + mask_of(t, tap) * w_ref[t:t + 1, :]
        o_ref[...] = _silu(acc)

    @pl.when(i < n_lat_tiles)
    def _():
        prev_in_seg = ((i * tm) & (seq - 1)) != 0
        next_in_seg = (((i + 1) * tm) & (seq - 1)) != 0
        ext_ref[0:halo, :] = jnp.where(prev_in_seg, prev_ref[...], 0.0)
        ext_ref[halo + tm:, :] = jnp.where(next_in_seg, next_ref[...], 0.0)
        taps(lambda t, tap: tap)

    @pl.when(i >= n_lat_tiles)
    def _():
        ext_ref[0:halo, :] = prev_ref[...]
        ext_ref[halo + tm:, :] = next_ref[...]
        pos = (i * tm + lax.broadcasted_iota(jnp.int32, (tm, 1), 0)) & (ctx_len - 1)

        def masked(t, tap):
            src = pos + (t - half)
            return jnp.where((src >= 0) & (src < ctx_len), tap, 0.0)

        taps(masked)


def _ssd_conv(p0, col0, width, conv_w, conv_b, seq, ctx_len, n_lat_rows, tm=1024, tc=1024, halo=8):
    m = p0.shape[0]
    assert seq % tm == 0 and tm % ctx_len == 0 and (seq & (seq - 1)) == 0 and (ctx_len & (ctx_len - 1)) == 0
    cb0 = col0 // tc
    hb = tm // halo
    last_hb = m // halo - 1
    wpad = jnp.zeros((SUBLANES, width), F32).at[:SSM_CONV].set(conv_w)
    return pl.pallas_call(
        functools.partial(_conv_kernel, tm=tm, halo=halo, seq=seq, ctx_len=ctx_len, n_lat_tiles=n_lat_rows // tm),
        grid=(m // tm, width // tc),
        in_specs=[
            pl.BlockSpec((halo, tc), lambda i, j: (jnp.maximum(i * hb - 1, 0), cb0 + j)),
            pl.BlockSpec((tm, tc), lambda i, j: (i, cb0 + j)),
            pl.BlockSpec((halo, tc), lambda i, j: (jnp.minimum((i + 1) * hb, last_hb), cb0 + j)),
            pl.BlockSpec((SUBLANES, tc), lambda i, j: (0, j)),
            pl.BlockSpec((1, tc), lambda i, j: (0, j)),
        ],
        out_specs=pl.BlockSpec((tm, tc), lambda i, j: (i, j)),
        out_shape=jax.ShapeDtypeStruct((m, width), F32),
        scratch_shapes=[pltpu.VMEM((tm + 2 * halo, tc), F32)],
        compiler_params=_params(("arbitrary", "arbitrary"), 3 * tm * tc * 4),
        name="ssd_conv",
    )(p0, p0, p0, wpad, conv_b.reshape(1, width))


def _ssd_scan_kernel(xs_f, bm_f, cm_f, dt_f, xs_b, bm_b, cm_b, dt_b, prm_ref, yf_ref, yb_ref, state_ref, *, n_heads):
    @pl.when(pl.program_id(1) == 0)
    def _():
        state_ref[...] = jnp.zeros_like(state_ref)

    fwd = _ssd_chunk(xs_f, bm_f, cm_f, dt_f, prm_ref.at[0], yf_ref, state_ref.at[0], backward=False, n_heads=n_heads)
    bwd = _ssd_chunk(xs_b, bm_b, cm_b, dt_b, prm_ref.at[1], yb_ref, state_ref.at[1], backward=True, n_heads=n_heads)
    for g in range(SSM_GROUPS):
        for _ in zip(fwd(g), bwd(g)):
            pass


def _ssd_chunk(xs_ref, bm_ref, cm_ref, dt_ref, prm_ref, y_ref, state_ref, *, backward, n_heads):
    hp = SSM_HEAD_DIM
    rep = n_heads // SSM_GROUPS
    lc = SSM_CHUNK
    assert 2 * hp == LANES and rep % 2 == 0 and SSM_STATE == LANES and lc == LANES

    dt_in = dt_ref[...] + prm_ref[1:2, :]
    dt = jnp.maximum(dt_in, 0.0) + jnp.log1p(jnp.exp(-jnp.abs(dt_in)))
    dta = dt * (-jnp.exp(prm_ref[0:1, :]) * LOG2E)
    row = lax.broadcasted_iota(jnp.int32, (lc, lc), 0)
    col = lax.broadcasted_iota(jnp.int32, (lc, lc), 1)
    causal = (row <= col) if backward else (row >= col)
    a_cum = jnp.dot(causal.astype(F32), dta, preferred_element_type=F32, precision=lax.Precision.HIGHEST)
    a_tot = a_cum[0:1, :] if backward else a_cum[lc - 1:lc, :]
    w_end = dt * jnp.exp2(a_tot - a_cum)
    a_cum_t, dt_t, w_end_t = a_cum.T, dt.T, w_end.T
    lo = col < hp

    def do_group(g):
        gs = slice(g * SSM_STATE, (g + 1) * SSM_STATE)
        b_f = bm_ref[:, gs]
        c_g = cm_ref[:, gs].astype(BF16)
        cb = lax.dot_general(c_g, b_f.astype(BF16), (((1,), (1,)), ((), ())), preferred_element_type=F32)
        b_t = b_f.T
        cols_g = slice(g * rep * hp, (g + 1) * rep * hp)
        y_off = jnp.dot(c_g, state_ref[:, cols_g].astype(BF16), preferred_element_type=F32)
        yield
        for pr in range(rep // 2):
            cols = slice(g * rep * hp + pr * LANES, g * rep * hp + (pr + 1) * LANES)
            xs2 = xs_ref[:, cols].astype(BF16)
            y_d, s_n, e_a = [], [], []
            for h in (g * rep + 2 * pr, g * rep + 2 * pr + 1):
                a_col = jnp.broadcast_to(a_cum[:, h:h + 1], (lc, lc))
                decay = jnp.exp2(jnp.where(causal, a_col - a_cum_t[h:h + 1, :], -jnp.inf))
                m = (cb * decay * dt_t[h:h + 1, :]).astype(BF16)
                y_d.append(jnp.dot(m, xs2, preferred_element_type=F32))
                s_n.append(jnp.dot((b_t * w_end_t[h:h + 1, :]).astype(BF16), xs2, preferred_element_type=F32))
                e_a.append(jnp.exp2(a_col))
            h0 = g * rep + 2 * pr
            y_ref[:, cols] = (jnp.where(lo, y_d[0], y_d[1])
                              + y_off[:, pr * LANES:(pr + 1) * LANES] * jnp.where(lo, e_a[0], e_a[1]))
            chunk_decay = jnp.where(lo[0:1, :], jnp.exp2(a_tot[:, h0:h0 + 1]), jnp.exp2(a_tot[:, h0 + 1:h0 + 2]))
            state_ref[:, cols] = state_ref[:, cols] * chunk_decay + jnp.where(lo, s_n[0], s_n[1])
            yield

    return do_group


def _ssd_scan(xbc, dt_raw, prm, n_batch, seq, ctx_len, d_ssm):
    m = xbc.shape[0]
    lc = SSM_CHUNK
    n_heads = d_ssm // SSM_HEAD_DIM
    nc_ctx, nc_lat = ctx_len // lc, seq // lc
    ctx_blk0 = n_batch * nc_lat
    gn = SSM_GROUPS * SSM_STATE

    def row_blk(b, s, backward):
        ctx_c = nc_ctx - 1 - s if backward else s
        lat_c = nc_lat - 1 - (s - nc_ctx) if backward else s - nc_ctx
        return jnp.where(s < nc_ctx, ctx_blk0 + b * nc_ctx + ctx_c, b * nc_lat + lat_c)

    def chunk_specs(backward):
        r = int(backward)
        return [
            pl.BlockSpec((lc, d_ssm), lambda b, s: (row_blk(b, s, backward), 0)),
            pl.BlockSpec((lc, gn), lambda b, s: (row_blk(b, s, backward), d_ssm // gn)),
            pl.BlockSpec((lc, gn), lambda b, s: (row_blk(b, s, backward), d_ssm // gn + 1)),
            pl.BlockSpec((lc, LANES), lambda b, s: (row_blk(b, s, backward), r)),
        ]

    return pl.pallas_call(
        functools.partial(_ssd_scan_kernel, n_heads=n_heads),
        grid=(n_batch, nc_ctx + nc_lat),
        in_specs=chunk_specs(False) + chunk_specs(True) + [pl.BlockSpec((2, SUBLANES, LANES), lambda b, s: (0, 0, 0))],
        out_specs=[pl.BlockSpec((lc, d_ssm), lambda b, s: (row_blk(b, s, False), 0)),
                   pl.BlockSpec((lc, d_ssm), lambda b, s: (row_blk(b, s, True), 0))],
        out_shape=[jax.ShapeDtypeStruct((m, d_ssm), F32), jax.ShapeDtypeStruct((m, d_ssm), F32)],
        scratch_shapes=[pltpu.VMEM((2, SSM_STATE, d_ssm), F32)],
        compiler_params=_params(("arbitrary", "arbitrary"), 8 * lc * d_ssm * 4),
        name="ssd_scan",
    )(xbc, xbc, xbc, dt_raw, xbc, xbc, xbc, dt_raw, prm)


def _ssd_finish_kernel(yf_ref, yb_ref, xs_ref, z_ref, dskip_ref, g_ref, o_ref, *, group_width):
    y = (yf_ref[...] + yb_ref[...] + dskip_ref[...] * xs_ref[...]) * _silu(z_ref[...])
    for g in range(y.shape[1] // group_width):
        sl = slice(g * group_width, (g + 1) * group_width)
        v = y[:, sl]
        vn = v * lax.rsqrt(jnp.mean(v * v, axis=-1, keepdims=True) + NORM_EPS) * g_ref[:, sl]
        o_ref[:, sl] = vn.astype(o_ref.dtype)


def _ssd_finish(y2, xbc, p0, d_skip_lanes, norm_g, d_ssm, tm=512):
    m = xbc.shape[0]
    return pl.pallas_call(
        functools.partial(_ssd_finish_kernel, group_width=d_ssm // SSM_GROUPS),
        grid=(m // tm,),
        in_specs=[
            pl.BlockSpec((tm, d_ssm), lambda i: (i, 0)),
            pl.BlockSpec((tm, d_ssm), lambda i: (i, 0)),
            pl.BlockSpec((tm, d_ssm), lambda i: (i, 0)),
            pl.BlockSpec((tm, d_ssm), lambda i: (i, 0)),
            pl.BlockSpec((1, d_ssm), lambda i: (0, 0)),
            pl.BlockSpec((1, d_ssm), lambda i: (0, 0)),
        ],
        out_specs=pl.BlockSpec((tm, d_ssm), lambda i: (i, 0)),
        out_shape=jax.ShapeDtypeStruct((m, d_ssm), BF16),
        compiler_params=_params(("arbitrary",), 5 * tm * d_ssm * 4),
        name="ssd_finish",
    )(*y2, xbc, p0, d_skip_lanes.reshape(1, d_ssm), norm_g.reshape(1, d_ssm))


def _rope(x, cos, sin_up, sin_dn):
    quarter = DIFF_HEAD_DIM // 4
    return (x * cos + pltpu.roll(x, LANES - quarter, axis=1) * sin_up + pltpu.roll(x, quarter, axis=1) * sin_dn)


def _diff_attn_heads_kernel(*refs, heads_per_step, with_lat, **kw):
    n_head_refs = 5 if with_lat else 3
    *ins, o_ref, k_scr, vt_scr = refs
    for h in range(heads_per_step):
        cs = slice(h * LANES, (h + 1) * LANES)
        head_ins = [r.at[:, cs] for r in ins[:n_head_refs]] + list(ins[n_head_refs:])
        _diff_attn_kernel(*head_ins, o_ref.at[:, cs], k_scr.at[h], vt_scr.at[h], with_lat=with_lat, **kw)


def _diff_attn_kernel(*refs, with_lat, lambda_init, n_ctx, sub_q, key_chunk=768):
    if with_lat:
        (q_ref, kc_ref, kl_ref, vc_ref, vl_ref, qcos_ref, qsu_ref, qsd_ref, kcos_ref, ksu_ref, ksd_ref,
         lam_ref, g_ref, o_ref, k_scr, vt_scr) = refs
    else:
        q_ref, kc_ref, vc_ref, lam_ref, g_ref, o_ref, k_scr, vt_scr = refs

    @pl.when(pl.program_id(2) == 0)
    def _():
        k_scr[0:n_ctx, :] = kc_ref[...].astype(BF16)
        vt_scr[:, 0:n_ctx] = vc_ref[...].T.astype(BF16)
        if with_lat:
            k_scr[n_ctx:, :] = _rope(kl_ref[...], kcos_ref[...], ksu_ref[...], ksd_ref[...]).astype(BF16)
            vt_scr[:, n_ctx:] = vl_ref[...].T.astype(BF16)

    lam_p = lam_ref[...]
    lam = (jnp.exp(jnp.sum(lam_p[0:1, :] * lam_p[1:2, :], axis=-1, keepdims=True))
           - jnp.exp(jnp.sum(lam_p[2:3, :] * lam_p[3:4, :], axis=-1, keepdims=True)) + lambda_init)
    first = lax.broadcasted_iota(jnp.int32, (LANES, sub_q), 0) < DIFF_HEAD_DIM
    n_sub = q_ref.shape[0] // sub_q

    def scores(t):
        rows = slice(t * sub_q, (t + 1) * sub_q)
        q = q_ref[rows, :]
        if with_lat:
            q = _rope(q, qcos_ref[rows, :], qsu_ref[rows, :], qsd_ref[rows, :])
        qt = (q * (DIFF_HEAD_DIM ** -0.5 * LOG2E)).T
        q2t = jnp.concatenate([jnp.where(first, qt, 0.0), jnp.where(first, 0.0, qt)], axis=1).astype(BF16)
        blocks = [jnp.dot(k_scr[k0:k1, :], q2t, preferred_element_type=F32) for k0, k1 in key_chunks]
        return blocks, [jnp.max(b, axis=0, keepdims=True) for b in blocks]

    n_keys = k_scr.shape[0]
    key_chunks = [(k0, min(k0 + key_chunk, n_keys)) for k0 in range(0, n_keys, key_chunk)]
    s_next = scores(0)
    for t in range(n_sub):
        rows = slice(t * sub_q, (t + 1) * sub_q)
        s_blocks, maxes = s_next
        if t + 1 < n_sub:
            s_next = scores(t + 1)
        m = functools.reduce(jnp.maximum, maxes)
        ovt, l = None, None
        for (k0, k1), s in zip(key_chunks, s_blocks):
            e = jnp.exp2(s - m)
            part = jnp.dot(vt_scr[:, k0:k1], e.astype(BF16), preferred_element_type=F32)
            part_l = jnp.sum(e, axis=0, keepdims=True)
            ovt, l = (part, part_l) if ovt is None else (ovt + part, l + part_l)
        ovt = ovt * (1.0 / l)
        ot = ovt[:, :sub_q] - lam * ovt[:, sub_q:]
        ot = ot * lax.rsqrt(jnp.mean(ot * ot, axis=0, keepdims=True) + NORM_EPS) * g_ref[...] * (1.0 - lambda_init)
        o_ref[rows, :] = ot.T.astype(o_ref.dtype)


def _diff_attn(p0, q_col0, k_col0, v_col0, n_heads, rope_tabs, lam_p, subln_g, lambda_init,
               n_batch, seq, ctx_len, with_lat, tq=2048, sub_q=256):
    ctx_rb0 = (n_batch * seq) // ctx_len
    n_keys = ctx_len + (seq if with_lat else 0)
    hps = 1 if with_lat else n_heads
    width = hps * LANES
    cb = lambda c0: c0 // width
    if with_lat:
        n_q = seq // tq
        q_spec = pl.BlockSpec((tq, width), lambda b, h, i: (b * n_q + i, cb(q_col0) + h))
    else:
        tq = ctx_len
        n_q = 1
        q_spec = pl.BlockSpec((tq, width), lambda b, h, i: (ctx_rb0 + b, cb(q_col0) + h))
    kc_spec = pl.BlockSpec((ctx_len, width), lambda b, h, i: (ctx_rb0 + b, cb(k_col0) + h))
    vc_spec = pl.BlockSpec((ctx_len, width), lambda b, h, i: (ctx_rb0 + b, cb(v_col0) + h))
    small = [pl.BlockSpec((SUBLANES, LANES), lambda b, h, i: (0, 0)),
             pl.BlockSpec((LANES, sub_q), lambda b, h, i: (0, 0))]
    small_args = [lam_p, jnp.broadcast_to(subln_g[:, None], (LANES, sub_q))]
    if with_lat:
        kl_spec = pl.BlockSpec((seq, width), lambda b, h, i: (b, cb(k_col0) + h))
        vl_spec = pl.BlockSpec((seq, width), lambda b, h, i: (b, cb(v_col0) + h))
        qt = pl.BlockSpec((tq, LANES), lambda b, h, i: (i, 0))
        kt = pl.BlockSpec((seq, LANES), lambda b, h, i: (0, 0))
        in_specs = [q_spec, kc_spec, kl_spec, vc_spec, vl_spec, qt, qt, qt, kt, kt, kt] + small
        args = [p0, p0, p0, p0, p0, *rope_tabs, *rope_tabs] + small_args
    else:
        in_specs = [q_spec, kc_spec, vc_spec] + small
        args = [p0, p0, p0] + small_args
    block_bytes = 10 * seq * LANES * 4 if with_lat else 8 * ctx_len * width * 4
    return pl.pallas_call(
        functools.partial(_diff_attn_heads_kernel, heads_per_step=hps, with_lat=with_lat, lambda_init=lambda_init,
                          n_ctx=ctx_len, sub_q=sub_q),
        grid=(n_batch, n_heads // hps, n_q),
        in_specs=in_specs,
        out_specs=pl.BlockSpec((tq, width), lambda b, h, i: (b * n_q + i, h)),
        out_shape=jax.ShapeDtypeStruct((n_batch * n_q * tq, n_heads * LANES), BF16),
        scratch_shapes=[pltpu.VMEM((hps, n_keys, LANES), BF16), pltpu.VMEM((hps, LANES, n_keys), BF16)],
        compiler_params=_params(("arbitrary", "arbitrary", "arbitrary"), block_bytes),
        name="diff_attn_lat" if with_lat else "diff_attn_ctx",
    )(*args)


def _rope_tables(seq):
    half = DIFF_HEAD_DIM // 2
    pos = jnp.arange(seq)
    row, col = pos // GRID_W, pos % GRID_W
    inv_freq = ROPE_BASE ** (-jnp.arange(0, half, 2, dtype=F32) / half)
    lane = np.arange(LANES) % DIFF_HEAD_DIM
    use_col = jnp.asarray(lane >= half)
    first = jnp.asarray((lane % half) < half // 2)
    freq = inv_freq[jnp.asarray(lane % (half // 2))]
    p = jnp.where(use_col[None, :], col[:, None], row[:, None]).astype(F32)
    ang = p * freq[None, :]
    cos, sin = jnp.cos(ang), jnp.sin(ang)
    return cos, jnp.where(first[None, :], -sin, 0.0), jnp.where(first[None, :], 0.0, sin)


def _rpb_gather_kernel(rpb_ref, o_ref):
    n = o_ref.shape[1]
    k = rpb_ref.shape[1]
    colid = pl.program_id(0) * n + lax.broadcasted_iota(jnp.int32, (k, n), 1)
    j = lax.broadcasted_iota(jnp.int32, (k, n), 0)
    qc = lax.shift_right_logical(colid, int(math.log2(LANES)))
    half = lax.shift_right_logical(colid & (LANES - 1), int(math.log2(GRID_W)))
    kc = colid & (GRID_W - 1)
    sel = (j == half * LANES + jnp.clip(kc - qc + (WIN_COLS - 1), 0, 2 * WIN_COLS - 2)).astype(F32)
    o_ref[...] = jnp.dot(rpb_ref[...], sel, preferred_element_type=F32, precision=lax.Precision.HIGHEST)


def _rpb_pairs(rpb):
    nh, nr, ncol = rpb.shape
    n_slot = nr + 1
    left = jnp.pad(rpb, ((0, 0), (1, 0), (0, LANES - ncol)))
    right = jnp.pad(rpb, ((0, 0), (0, 1), (0, LANES - ncol)))
    rows = jnp.concatenate([left, right], axis=2).reshape(nh * n_slot, 2 * LANES)
    tn = 1024
    out = pl.pallas_call(
        _rpb_gather_kernel,
        grid=(GRID_W * LANES // tn,),
        in_specs=[pl.BlockSpec((nh * n_slot, 2 * LANES), lambda j: (0, 0))],
        out_specs=pl.BlockSpec((nh * n_slot, tn), lambda j: (0, j)),
        out_shape=jax.ShapeDtypeStruct((nh * n_slot, GRID_W * LANES), F32),
        compiler_params=_params(("arbitrary",), nh * n_slot * tn * 4 + 2 * LANES * tn * 4),
        name="rpb_gather",
    )(rows)
    return out.reshape(nh, n_slot, GRID_W, LANES)


def _na_block_plan(rows):
    kr = min(WIN_ROWS, rows)
    n_blk = rows // NA_Q_ROWS
    kb = np.clip(np.arange(n_blk) * NA_Q_ROWS - kr // 2, 0, rows - NA_K_ROWS)
    layouts, layout_of = [], []
    for blk in range(n_blk):
        dr = np.full((NA_Q_ROWS, NA_K_ROWS), -1, np.int64)
        for i in range(NA_Q_ROWS):
            r = blk * NA_Q_ROWS + i
            rs = int(np.clip(r - kr // 2, 0, rows - kr))
            for j in range(NA_K_ROWS):
                krow = kb[blk] + j
                if rs <= krow < rs + kr:
                    dr[i, j] = krow - r + WIN_ROWS - 1
        key = dr.tobytes()
        if key not in [l.tobytes() for l in layouts]:
            layouts.append(dr)
        layout_of.append([l.tobytes() for l in layouts].index(key))
    return kb, np.stack(layouts), np.asarray(layout_of)


def _na_bias_kernel(tp_ref, o_ref, *, layouts):
    qc = lax.broadcasted_iota(jnp.int32, (GRID_W, LANES), 0)
    lane = lax.broadcasted_iota(jnp.int32, (GRID_W, LANES), 1)
    kc = lane & (GRID_W - 1)
    left = lane < GRID_W
    col_start = jnp.clip(qc - WIN_COLS // 2, 0, GRID_W - WIN_COLS)
    col_ok = (kc >= col_start) & (kc < col_start + WIN_COLS)
    n_lay, n_q, n_k = layouts.shape
    for lay in range(n_lay):
        for i in range(n_q):
            for p in range(-(-n_k // 2)):
                d_l = int(layouts[lay, i, 2 * p])
                d_r = int(layouts[lay, i, 2 * p + 1]) if 2 * p + 1 < n_k else -1
                width = LANES if 2 * p + 1 < n_k else GRID_W
                if d_l < 0 and d_r < 0:
                    tile = jnp.full((GRID_W, LANES), -jnp.inf, F32)
                else:
                    assert d_l < 0 or d_r < 0 or d_r == d_l + 1
                    ok = col_ok
                    if d_l < 0:
                        ok = ok & jnp.logical_not(left)
                    if d_r < 0:
                        ok = ok & left
                    slot = d_r if d_r >= 0 else d_l + 1
                    tile = jnp.where(ok, tp_ref[slot] * LOG2E, -jnp.inf)
                o_ref[lay, i * GRID_W:(i + 1) * GRID_W, p * LANES:p * LANES + width] = tile[:, :width]


def _na_bias(rpb_pairs, layouts):
    nh, n_slot = rpb_pairs.shape[:2]
    n_lay = layouts.shape[0]
    tq, nkw = NA_Q_ROWS * GRID_W, NA_K_ROWS * GRID_W
    return pl.pallas_call(
        functools.partial(_na_bias_kernel, layouts=layouts),
        grid=(nh,),
        in_specs=[pl.BlockSpec((None, n_slot, GRID_W, LANES), lambda h: (h, 0, 0, 0))],
        out_specs=pl.BlockSpec((None, n_lay, tq, nkw), lambda h: (h, 0, 0, 0)),
        out_shape=jax.ShapeDtypeStruct((nh, n_lay, tq, nkw), F32),
        compiler_params=_params(("arbitrary",), n_lay * tq * nkw * 4),
        name="na_bias",
    )(rpb_pairs)


def _na_kernel(q_ref, kl_ref, vl_ref, kc_ref, vc_ref, bias_ref, o_ref, *, plan):
    tq = NA_Q_ROWS * GRID_W
    nkw = NA_K_ROWS * GRID_W
    nt = (((1,), (1,)), ((), ()))
    kc = kc_ref[...]
    vc = vc_ref[...]

    def scores(blk):
        kb, lay = plan[blk]
        q = q_ref[blk * tq:(blk + 1) * tq, :]
        ks = slice(kb * GRID_W, kb * GRID_W + nkw)
        return (lax.dot_general(q, kl_ref[ks, :], nt, preferred_element_type=F32) + bias_ref[lay],
                lax.dot_general(q, kc, nt, preferred_element_type=F32))

    def weighted_values(blk, e_w, e_c, l):
        kb, _ = plan[blk]
        ks = slice(kb * GRID_W, kb * GRID_W + nkw)
        o = (jnp.dot(e_w, vl_ref[ks, :], preferred_element_type=F32) + jnp.dot(e_c, vc, preferred_element_type=F32))
        o_ref[blk * tq:(blk + 1) * tq, :] = (o * (1.0 / l)).astype(o_ref.dtype)

    s_next = scores(0)
    pending = None
    for blk in range(len(plan)):
        s_w, s_c = s_next
        if blk + 1 < len(plan):
            s_next = scores(blk + 1)
        if pending is not None:
            weighted_values(blk - 1, *pending)
        m = jnp.maximum(jnp.max(s_w, axis=-1, keepdims=True), jnp.max(s_c, axis=-1, keepdims=True))
        e_w = jnp.exp2(s_w - m)
        e_c = jnp.exp2(s_c - m)
        l = jnp.sum(e_w, axis=-1, keepdims=True) + jnp.sum(e_c, axis=-1, keepdims=True)
        pending = (e_w.astype(BF16), e_c.astype(BF16), l)
    weighted_values(len(plan) - 1, *pending)


def _na_attn(p1, rpb, n_batch, seq, ctx_len, n_heads):
    rows = seq // GRID_W
    kb, layouts, layout_of = _na_block_plan(rows)
    bias = _na_bias(_rpb_pairs(rpb), layouts)
    n_lay = layouts.shape[0]
    tq = NA_Q_ROWS * GRID_W
    nkw = NA_K_ROWS * GRID_W
    ctx_rb0 = (n_batch * seq) // ctx_len
    plan = tuple((int(k), int(l)) for k, l in zip(kb, layout_of))
    return pl.pallas_call(
        functools.partial(_na_kernel, plan=plan),
        grid=(n_heads, n_batch),
        in_specs=[
            pl.BlockSpec((seq, LANES), lambda h, b: (b, h)),
            pl.BlockSpec((seq, LANES), lambda h, b: (b, n_heads + h)),
            pl.BlockSpec((seq, LANES), lambda h, b: (b, 2 * n_heads + h)),
            pl.BlockSpec((ctx_len, LANES), lambda h, b: (ctx_rb0 + b, n_heads + h)),
            pl.BlockSpec((ctx_len, LANES), lambda h, b: (ctx_rb0 + b, 2 * n_heads + h)),
            pl.BlockSpec((None, n_lay, tq, nkw), lambda h, b: (h, 0, 0, 0)),
        ],
        out_specs=pl.BlockSpec((seq, LANES), lambda h, b: (b, h)),
        out_shape=jax.ShapeDtypeStruct((n_batch * seq, n_heads * LANES), BF16),
        compiler_params=_params(("arbitrary", "arbitrary"), n_lay * tq * nkw * 4 + 4 * seq * LANES * 2),
        name="na_attn",
    )(p1, p1, p1, p1, p1, bias)


def _pack_rows(rows):
    out = jnp.zeros((SUBLANES, LANES), F32)
    for r, v in enumerate(rows):
        out = out.at[r, :v.shape[0]].set(v.astype(F32))
    return out


def kernel(x, c, ctx, c_ctx, ada_w, ada_b, norm_mix_g, norm_ffn_g, final_norm_g, ffn_w1, ffn_w3, ffn_w2, ev_w_in, ev_conv_w, ev_conv_b, ev_a_log, ev_dt_bias, ev_d_skip, ev_ssm_norm_g, ev_lam_q1, ev_lam_k1, ev_lam_q2, ev_lam_k2, ev_subln_g, ev_w_out, od_w_in, od_rpb, od_w_out):
    n_batch, seq, d = x.shape
    ctx_len = ctx.shape[1]
    depth = ada_w.shape[0]
    n_lat = n_batch * seq
    n_tok = n_lat + n_batch * ctx_len

    d_ssm = ev_ssm_norm_g.shape[1]
    n_ssm_heads = ev_d_skip.shape[1]
    d_xbc = ev_conv_w.shape[2]
    d_qk = d_v = (ev_w_in.shape[2] - d_ssm - d_xbc - 2 * n_ssm_heads) // 3
    n_diff_heads = d_v // (2 * DIFF_HEAD_DIM)
    n_na_heads = od_rpb.shape[1]

    stream = (x.reshape(n_lat, d), ctx.reshape(n_batch * ctx_len, d))
    cond = jnp.zeros((COND_ROWS, d), F32).at[:n_batch].set(c).at[n_batch].set(c_ctx)
    mod = _ada_mod(cond, ada_w, ada_b).reshape(depth, COND_ROWS, 1, 6 * d)
    rope_tabs = _rope_tables(seq)
    kw = dict(seq=seq, n_batch=n_batch)
    ffn_w = (ffn_w1, ffn_w3, ffn_w2)

    for i in range(depth):
        ctx_out = i < depth - 1
        m_rows = n_tok if ctx_out else n_lat
        j = i // 2
        if i % 2 == 0:
            lambda_init = 0.8 - 0.6 * math.exp(-0.3 * i)
            w_in = ev_w_in[j]
            dt0 = d_ssm + d_xbc
            w_t = jnp.swapaxes(w_in, 0, 1)
            w_parts = [w_t[:dt0].astype(BF16), w_t[dt0 + 2 * n_ssm_heads:].astype(BF16)]
            w_dt = jnp.zeros((2 * LANES, d), F32)
            w_dt = w_dt.at[:n_ssm_heads].set(w_t[dt0:dt0 + n_ssm_heads])
            w_dt = w_dt.at[LANES:LANES + n_ssm_heads].set(w_t[dt0 + n_ssm_heads:dt0 + 2 * n_ssm_heads])
            p0, dt_raw = _proj(stream, norm_mix_g[i], mod, i, w_parts, w_dt.astype(BF16), F32, **kw)
            q0 = d_ssm + d_xbc
            k0, v0 = q0 + d_qk, q0 + 2 * d_qk

            xbc = _ssd_conv(p0, d_ssm, d_xbc, ev_conv_w[j], ev_conv_b[j], seq, ctx_len, n_lat)
            prm = jnp.stack([_pack_rows([ev_a_log[j, r], ev_dt_bias[j, r]]) for r in range(2)])
            y2 = _ssd_scan(xbc, dt_raw, prm, n_batch, seq, ctx_len, d_ssm)
            mix_ssd = _ssd_finish(y2, xbc, p0, jnp.repeat(ev_d_skip[j], SSM_HEAD_DIM), ev_ssm_norm_g[j], d_ssm)

            lam_p = _pack_rows([ev_lam_q1[j], ev_lam_k1[j], ev_lam_q2[j], ev_lam_k2[j]])
            attn_args = (p0, q0, k0, v0, n_diff_heads, rope_tabs, lam_p, ev_subln_g[j], lambda_init,
                         n_batch, seq, ctx_len)
            mix_attn = _diff_attn(*attn_args, with_lat=True)
            if ctx_out:
                mix_attn = (mix_attn, _diff_attn(*attn_args, with_lat=False))
            lhs = [mix_ssd, mix_attn]
            w_out = ev_w_out[j]
        else:
            d_na = n_na_heads * NA_HEAD_DIM
            q_scale = jnp.where(jnp.arange(3 * d_na) < d_na, NA_HEAD_DIM ** -0.5 * LOG2E, 1.0).astype(F32)
            p1 = _proj(stream, norm_mix_g[i], mod, i, od_w_in[j], None, BF16, col_scale=q_scale, **kw)
            assert not ctx_out, "context-query neighbourhood layers are not needed at this depth"
            lhs = [_na_attn(p1, od_rpb[j], n_batch, seq, ctx_len, n_na_heads)]
            w_out = od_w_out[j]
        stream = _out_proj(lhs, w_out, stream, mod, i, 2, m_rows, **kw)
        stream = _ffn(stream, norm_ffn_g[i], mod, i, *ffn_w, final_norm_g, not ctx_out, m_rows, **kw)
    return stream.reshape(n_batch, seq, d)
```

```python
import functools
import math

import jax
import jax.numpy as jnp
import numpy as np
from jax import lax
from jax.experimental import pallas as pl
from jax.experimental.pallas import tpu as pltpu

F32 = jnp.float32
BF16 = jnp.bfloat16

GRID_W = 64
SSM_HEAD_DIM = 64
SSM_GROUPS = 4
SSM_STATE = 128
SSM_CONV = 5
SSM_CHUNK = 128
DIFF_HEAD_DIM = 64
NA_HEAD_DIM = 128
WIN_ROWS = 8
WIN_COLS = 16
ROPE_BASE = 10000.0
NORM_EPS = 1e-6
LOG2E = math.log2(math.e)

LANES = 128
SUBLANES = 8
V7X_VMEM_BYTES = 64 * 1024 * 1024
VMEM_HEADROOM_BYTES = 3 * 1024 * 1024
VMEM_MIN_REQUEST_BYTES = 32 * 1024 * 1024
VMEM_TEMPORARIES_BYTES = 16 * 1024 * 1024

NA_Q_ROWS = 4
NA_K_ROWS = NA_Q_ROWS + WIN_ROWS - 1
COND_ROWS = 16


def _vmem_limit(block_bytes):
    want = max(VMEM_MIN_REQUEST_BYTES, 2 * block_bytes + VMEM_TEMPORARIES_BYTES)
    return int(min(V7X_VMEM_BYTES - VMEM_HEADROOM_BYTES, want))


def _params(semantics, block_bytes):
    return pltpu.CompilerParams(dimension_semantics=semantics, vmem_limit_bytes=_vmem_limit(block_bytes))


def _silu(v):
    return v * jax.nn.sigmoid(v)


def _mod_row(i, tm, seq, n_batch):
    return jnp.minimum((i * tm) // seq, n_batch)


def _ada_kernel(cond_ref, w_ref, b_ref, o_ref):
    s = _silu(cond_ref[...]).astype(BF16)
    o_ref[...] = jnp.dot(s, w_ref[...].astype(BF16), preferred_element_type=F32) + b_ref[...]


def _ada_mod(cond, ada_w, ada_b, tn=1024):
    depth, d, n = ada_w.shape
    return pl.pallas_call(
        _ada_kernel,
        grid=(depth, n // tn),
        in_specs=[
            pl.BlockSpec((COND_ROWS, d), lambda l, j: (0, 0)),
            pl.BlockSpec((None, d, tn), lambda l, j: (l, 0, j)),
            pl.BlockSpec((None, 1, tn), lambda l, j: (l, 0, j)),
        ],
        out_specs=pl.BlockSpec((None, COND_ROWS, tn), lambda l, j: (l, 0, j)),
        out_shape=jax.ShapeDtypeStruct((depth, COND_ROWS, n), F32),
        compiler_params=_params(("arbitrary", "arbitrary"), d * tn * 4),
        name="ada_mod",
    )(cond, ada_w, ada_b.reshape(depth, 1, n))


def _norm_mod_pipeline(x_ref, g_ref, mod_ref, shift_idx, scale_idx, n_chunks, consume):
    tm, d = x_ref.shape
    shift = mod_ref[:, shift_idx * d:(shift_idx + 1) * d]
    gain = g_ref[...] * (1.0 + mod_ref[:, scale_idx * d:(scale_idx + 1) * d])
    chunk = tm // n_chunks

    def norm_rows(c):
        x = x_ref[c * chunk:(c + 1) * chunk, :]
        inv = lax.rsqrt(jnp.mean(x * x, axis=-1, keepdims=True) + NORM_EPS)
        return (x * inv * gain + shift).astype(BF16)

    h_next = norm_rows(0)
    for c in range(n_chunks):
        h = h_next
        if c + 1 < n_chunks:
            h_next = norm_rows(c + 1)
        consume(slice(c * chunk, (c + 1) * chunk), h)


def _proj_kernel(*refs, n_x, n_w, tiles_per_part, n_lat_tiles, has_aux, has_scale, first_step_chunks=4):
    rest = list(refs)
    x_refs = [rest.pop(0) for _ in range(n_x)]
    g_ref, mod_ref = rest.pop(0), rest.pop(0)
    w_refs = [rest.pop(0) for _ in range(max(n_w, 1))]
    cs_ref = rest.pop(0) if has_scale else None
    if has_aux:
        waux_ref, o_ref, oaux_ref, h_ref = rest
    else:
        o_ref, h_ref = rest
    dims = (((1,), (1,)), ((), ())) if n_w else (((1,), (0,)), ((), ()))
    j = pl.program_id(1)
    tm, d = x_refs[0].shape

    def emit(h, rows, w_ref):
        acc = lax.dot_general(h, w_ref[...].astype(BF16), dims, preferred_element_type=F32)
        if has_scale:
            acc = acc * cs_ref[...]
        o_ref[rows, :] = acc.astype(o_ref.dtype)

    def first_tile(x_ref):
        def consume(rows, h):
            h_ref[rows, :] = h
            emit(h, rows, w_refs[0])
            if has_aux:
                oaux_ref[rows, :] = lax.dot_general(h, waux_ref[...], dims, preferred_element_type=F32)

        _norm_mod_pipeline(x_ref, g_ref, mod_ref, 0, 1, first_step_chunks, consume)

    if n_x == 1:
        pl.when(j == 0)(functools.partial(first_tile, x_refs[0]))
    else:
        is_lat = pl.program_id(0) < n_lat_tiles
        pl.when(jnp.logical_and(j == 0, is_lat))(functools.partial(first_tile, x_refs[0]))
        pl.when(jnp.logical_and(j == 0, jnp.logical_not(is_lat)))(functools.partial(first_tile, x_refs[1]))

    for p in range(max(n_w, 1)):
        lo = max(p * tiles_per_part, 1)
        in_part = (j >= lo) if n_w == 0 else jnp.logical_and(j >= lo, j < (p + 1) * tiles_per_part)
        pl.when(in_part)(lambda p=p: emit(h_ref[...], slice(None), w_refs[p]))


def _proj(x, g, mod, layer, w, w_aux, out_dtype, seq, n_batch, col_scale=None, tm=1024, tn=1024):
    parts = list(w) if isinstance(w, (list, tuple)) else None
    if parts is None:
        d, n = w.shape
        n_w, tiles_per_part = 0, 0
    else:
        d = parts[0].shape[1]
        n_w, tiles_per_part = len(parts), parts[0].shape[0] // tn
        assert all(p.shape == (tiles_per_part * tn, d) for p in parts)
        n = n_w * tiles_per_part * tn
    m = sum(a.shape[0] for a in x) if isinstance(x, tuple) else x.shape[0]
    n_lat_tiles = (n_batch * seq) // tm
    has_aux = w_aux is not None
    has_scale = col_scale is not None
    mod_spec = pl.BlockSpec((None, None, 1, mod.shape[-1]),
                            lambda i, j: (layer, _mod_row(i, tm, seq, n_batch), 0, 0))
    in_specs, args = _row_operand(x, tm, n_lat_tiles, d, lambda j: 0, single_ctx_buffer=True)
    n_x = len(args)
    in_specs += [pl.BlockSpec((1, d), lambda i, j: (0, 0)), mod_spec]
    args += [g.reshape(1, d), mod]
    if parts is None:
        in_specs.append(pl.BlockSpec((d, tn), lambda i, j: (0, j)))
        args.append(w)
    else:
        for p, part in enumerate(parts):
            in_specs.append(pl.BlockSpec(
                (tn, d), lambda i, j, p=p: (jnp.clip(j - p * tiles_per_part, 0, tiles_per_part - 1), 0)))
            args.append(part)
    out_specs = [pl.BlockSpec((tm, tn), lambda i, j: (i, j))]
    out_shape = [jax.ShapeDtypeStruct((m, n), out_dtype)]
    if has_scale:
        in_specs.append(pl.BlockSpec((1, tn), lambda i, j: (0, j)))
        args.append(col_scale.reshape(1, n))
    if has_aux:
        na = w_aux.shape[0] if parts is not None else w_aux.shape[1]
        in_specs.append(pl.BlockSpec(w_aux.shape, lambda i, j: (0, 0)))
        out_specs.append(pl.BlockSpec((tm, na), lambda i, j: (i, 0)))
        out_shape.append(jax.ShapeDtypeStruct((m, na), F32))
        args.append(w_aux)
    w_itemsize = (parts[0] if parts is not None else w).dtype.itemsize
    block_bytes = n_x * tm * d * 4 + max(n_w, 1) * d * tn * w_itemsize + tm * tn * 4 + tm * d
    outs = pl.pallas_call(
        functools.partial(_proj_kernel, n_x=n_x, n_w=n_w, tiles_per_part=tiles_per_part, n_lat_tiles=n_lat_tiles,
                          has_aux=has_aux, has_scale=has_scale),
        grid=(m // tm, n // tn),
        in_specs=in_specs,
        out_specs=out_specs,
        out_shape=out_shape,
        scratch_shapes=[pltpu.VMEM((tm, d), BF16)],
        compiler_params=_params(("arbitrary", "arbitrary"), block_bytes),
        name="proj",
    )(*args)
    return outs if has_aux else outs[0]


def _row_operand(a, tm, n_lat_tiles, width, col_of_j, single_ctx_buffer=False):
    if not isinstance(a, tuple):
        return [pl.BlockSpec((tm, width), lambda i, j: (i, col_of_j(j)))], [a]
    lat, ctx = a
    assert ctx.shape[0] % tm == 0 and lat.shape[0] == n_lat_tiles * tm

    def lat_index(i, j):
        return jnp.minimum(i, n_lat_tiles - 1), jnp.where(i < n_lat_tiles, col_of_j(j), 0)

    def ctx_index(i, j):
        return jnp.maximum(i - n_lat_tiles, 0), jnp.where(i < n_lat_tiles, 0, col_of_j(j))

    ctx_mode = dict(pipeline_mode=pl.Buffered(1)) if single_ctx_buffer else {}
    return [pl.BlockSpec((tm, width), lat_index), pl.BlockSpec((tm, width), ctx_index, **ctx_mode)], [lat, ctx]


def _pick_rows(refs, n_lat_tiles):
    if len(refs) == 1:
        return refs[0][...]
    return jnp.where(pl.program_id(0) < n_lat_tiles, refs[0][...], refs[1][...])


def _out_proj_kernel(*refs, arity, n_lat_tiles):
    refs = list(refs)
    n_lhs = len(arity) - 1
    groups = [[refs.pop(0) for _ in range(n)] for n in arity[:-1]]
    ws = [refs.pop(0) for _ in range(n_lhs)]
    res = [refs.pop(0) for _ in range(arity[-1])]
    gate_ref, o_ref = refs
    acc = None
    for grp, w_ref in zip(groups, ws):
        part = jnp.dot(_pick_rows(grp, n_lat_tiles), w_ref[...].astype(BF16), preferred_element_type=F32)
        acc = part if acc is None else acc + part
    o_ref[...] = _pick_rows(res, n_lat_tiles) + gate_ref[...] * acc


def _out_proj(lhs_list, w, res, mod, layer, gate_idx, m_rows, seq, n_batch, tm=512, tn=2048):
    d = w.shape[1]
    n_lat_tiles = (n_batch * seq) // tm
    in_specs, args, arity, widths = [], [], [], []
    for a in lhs_list:
        kk = (a[0] if isinstance(a, tuple) else a).shape[1]
        sp, ar = _row_operand(a, tm, n_lat_tiles, kk, lambda j: 0)
        in_specs += sp
        args += ar
        arity.append(len(ar))
        widths.append(kk)
    row0 = 0
    for kk in widths:
        in_specs.append(pl.BlockSpec((kk, tn), lambda i, j, rb=row0 // kk: (rb, j)))
        args.append(w)
        row0 += kk
    sp, ar = _row_operand(res, tm, n_lat_tiles, tn, lambda j: j)
    in_specs += sp
    args += ar
    arity.append(len(ar))
    in_specs.append(pl.BlockSpec((None, None, 1, tn),
                                 lambda i, j: (layer, _mod_row(i, tm, seq, n_batch), 0, gate_idx * (d // tn) + j)))
    args.append(mod)
    k_total = sum(widths)
    block_bytes = 2 * tm * k_total * 2 + k_total * tn * w.dtype.itemsize + 3 * tm * tn * 4
    return pl.pallas_call(
        functools.partial(_out_proj_kernel, arity=tuple(arity), n_lat_tiles=n_lat_tiles),
        grid=(m_rows // tm, d // tn),
        in_specs=in_specs,
        out_specs=pl.BlockSpec((tm, tn), lambda i, j: (i, j)),
        out_shape=jax.ShapeDtypeStruct((m_rows, d), F32),
        compiler_params=_params(("arbitrary", "arbitrary"), block_bytes),
        name="out_proj",
    )(*args)


def _ffn_kernel(x_ref, g_ref, mod_ref, w1_ref, w3_ref, w2_ref, fg_ref, o_ref, h_ref, *, final_norm, rows,
                first_step_chunks):
    j = pl.program_id(1)
    tm, d = x_ref.shape

    def swiglu(h, w1, w3, w2):
        a = jnp.dot(h, w1, preferred_element_type=F32)
        b = jnp.dot(h, w3, preferred_element_type=F32)
        return jnp.dot((_silu(a) * b).astype(BF16), w2, preferred_element_type=F32)

    @pl.when(j == 0)
    def _():
        w1, w3, w2 = (w_ref[...].astype(BF16) for w_ref in (w1_ref, w3_ref, w2_ref))

        def consume(rows, h):
            h_ref[rows, :] = h
            o_ref[rows, :] = swiglu(h, w1, w3, w2)

        _norm_mod_pipeline(x_ref, g_ref, mod_ref, 3, 4, first_step_chunks, consume)

    @pl.when(j > 0)
    def _():
        o_ref[...] += swiglu(h_ref[...], w1_ref[...].astype(BF16), w3_ref[...].astype(BF16),
                             w2_ref[...].astype(BF16))

    @pl.when(j == pl.num_programs(1) - 1)
    def _():
        gate = mod_ref[:, 5 * d:6 * d]
        fg = fg_ref[...]

        def body(r, carry):
            sl = pl.ds(pl.multiple_of(r * rows, rows), rows)
            y = x_ref[sl, :] + gate * o_ref[sl, :]
            if final_norm:
                y = y * lax.rsqrt(jnp.mean(y * y, axis=-1, keepdims=True) + NORM_EPS) * fg
            o_ref[sl, :] = y
            return carry

        lax.fori_loop(0, x_ref.shape[0] // rows, body, 0)


def _ffn(x, g, mod, layer, w1, w3, w2, final_g, final_norm, m_rows, seq, n_batch, tm=1024, tf=256):
    d = x.shape[1]
    ff = w1.shape[2]
    block_bytes = 2 * tm * d * 4 + d * tf * sum(w.dtype.itemsize for w in (w1, w3, w2)) + tm * d
    return pl.pallas_call(
        functools.partial(_ffn_kernel, final_norm=final_norm, rows=128, first_step_chunks=4),
        grid=(m_rows // tm, ff // tf),
        in_specs=[
            pl.BlockSpec((tm, d), lambda i, j: (i, 0)),
            pl.BlockSpec((1, d), lambda i, j: (0, 0)),
            pl.BlockSpec((None, None, 1, mod.shape[-1]),
                         lambda i, j: (layer, _mod_row(i, tm, seq, n_batch), 0, 0)),
            pl.BlockSpec((None, d, tf), lambda i, j: (layer, 0, j)),
            pl.BlockSpec((None, d, tf), lambda i, j: (layer, 0, j)),
            pl.BlockSpec((None, tf, d), lambda i, j: (layer, j, 0)),
            pl.BlockSpec((1, d), lambda i, j: (0, 0)),
        ],
        out_specs=pl.BlockSpec((tm, d), lambda i, j: (i, 0)),
        out_shape=jax.ShapeDtypeStruct((m_rows, d), F32),
        scratch_shapes=[pltpu.VMEM((tm, d), BF16)],
        compiler_params=_params(("arbitrary", "arbitrary"), block_bytes),
        name="ffn",
    )(x, g.reshape(1, d), mod, w1, w3, w2, final_g.reshape(1, d))


def _conv_kernel(prev_ref, cur_ref, next_ref, w_ref, b_ref, o_ref, ext_ref, *, tm, halo, seq, ctx_len, n_lat_tiles):
    i = pl.program_id(0)
    half = SSM_CONV // 2
    ext_ref[halo:halo + tm, :] = cur_ref[...]

    def taps(mask_of):
        acc = jnp.zeros(cur_ref.shape, F32) + b_ref[...]
        for t in range(SSM_CONV):
            tap = ext_ref[halo - half + t:halo - half + t + tm, :]
            acc = acc + mask_of(t, tap) * w_ref[t:t + 1, :]
        o_ref[...] = _silu(acc)

    @pl.when(i < n_lat_tiles)
    def _():
        prev_in_seg = ((i * tm) & (seq - 1)) != 0
        next_in_seg = (((i + 1) * tm) & (seq - 1)) != 0
        ext_ref[0:halo, :] = jnp.where(prev_in_seg, prev_ref[...], 0.0)
        ext_ref[halo + tm:, :] = jnp.where(next_in_seg, next_ref[...], 0.0)
        taps(lambda t, tap: tap)

    @pl.when(i >= n_lat_tiles)
    def _():
        ext_ref[0:halo, :] = prev_ref[...]
        ext_ref[halo + tm:, :] = next_ref[...]
        pos = (i * tm + lax.broadcasted_iota(jnp.int32, (tm, 1), 0)) & (ctx_len - 1)

        def masked(t, tap):
            src = pos + (t - half)
            return jnp.where((src >= 0) & (src < ctx_len), tap, 0.0)

        taps(masked)


def _ssd_conv(p0, col0, width, conv_w, conv_b, seq, ctx_len, n_lat_rows, tm=1024, tc=1024, halo=8):
    m = p0.shape[0]
    assert seq % tm == 0 and tm % ctx_len == 0 and (seq & (seq - 1)) == 0 and (ctx_len & (ctx_len - 1)) == 0
    cb0 = col0 // tc
    hb = tm // halo
    last_hb = m // halo - 1
    wpad = jnp.zeros((SUBLANES, width), F32).at[:SSM_CONV].set(conv_w)
    return pl.pallas_call(
        functools.partial(_conv_kernel, tm=tm, halo=halo, seq=seq, ctx_len=ctx_len, n_lat_tiles=n_lat_rows // tm),
        grid=(m // tm, width // tc),
        in_specs=[
            pl.BlockSpec((halo, tc), lambda i, j: (jnp.maximum(i * hb - 1, 0), cb0 + j)),
            pl.BlockSpec((tm, tc), lambda i, j: (i, cb0 + j)),
            pl.BlockSpec((halo, tc), lambda i, j: (jnp.minimum((i + 1) * hb, last_hb), cb0 + j)),
            pl.BlockSpec((SUBLANES, tc), lambda i, j: (0, j)),
            pl.BlockSpec((1, tc), lambda i, j: (0, j)),
        ],
        out_specs=pl.BlockSpec((tm, tc), lambda i, j: (i, j)),
        out_shape=jax.ShapeDtypeStruct((m, width), F32),
        scratch_shapes=[pltpu.VMEM((tm + 2 * halo, tc), F32)],
        compiler_params=_params(("arbitrary", "arbitrary"), 3 * tm * tc * 4),
        name="ssd_conv",
    )(p0, p0, p0, wpad, conv_b.reshape(1, width))


def _ssd_scan_kernel(xs_f, bm_f, cm_f, dt_f, xs_b, bm_b, cm_b, dt_b, prm_ref, yf_ref, yb_ref, state_ref, *, n_heads):
    @pl.when(pl.program_id(1) == 0)
    def _():
        state_ref[...] = jnp.zeros_like(state_ref)

    fwd = _ssd_chunk(xs_f, bm_f, cm_f, dt_f, prm_ref.at[0], yf_ref, state_ref.at[0], backward=False, n_heads=n_heads)
    bwd = _ssd_chunk(xs_b, bm_b, cm_b, dt_b, prm_ref.at[1], yb_ref, state_ref.at[1], backward=True, n_heads=n_heads)
    stages = [(fwd(g), bwd(g)) for g in range(SSM_GROUPS)]

    def advance(g):
        for gen in stages[g]:
            next(gen)

    advance(0)
    for g in range(SSM_GROUPS):
        advance(g)
        if g + 1 < SSM_GROUPS:
            advance(g + 1)
        for _ in range(n_heads // SSM_GROUPS // 2 - 1):
            advance(g)


def _ssd_chunk(xs_ref, bm_ref, cm_ref, dt_ref, prm_ref, y_ref, state_ref, *, backward, n_heads):
    hp = SSM_HEAD_DIM
    rep = n_heads // SSM_GROUPS
    lc = SSM_CHUNK
    assert 2 * hp == LANES and rep % 2 == 0 and SSM_STATE == LANES and lc == LANES

    dt_in = dt_ref[...] + prm_ref[1:2, :]
    dt = jnp.maximum(dt_in, 0.0) + jnp.log1p(jnp.exp(-jnp.abs(dt_in)))
    dta = dt * (-jnp.exp(prm_ref[0:1, :]) * LOG2E)
    row = lax.broadcasted_iota(jnp.int32, (lc, lc), 0)
    col = lax.broadcasted_iota(jnp.int32, (lc, lc), 1)
    causal = (row <= col) if backward else (row >= col)
    a_cum = jnp.dot(causal.astype(F32), dta, preferred_element_type=F32, precision=lax.Precision.HIGHEST)
    a_tot = a_cum[0:1, :] if backward else a_cum[lc - 1:lc, :]
    w_end = dt * jnp.exp2(a_tot - a_cum)
    a_cum_t, dt_t, w_end_t = a_cum.T, dt.T, w_end.T
    lo = col < hp

    def do_group(g):
        gs = slice(g * SSM_STATE, (g + 1) * SSM_STATE)
        b_f = bm_ref[:, gs]
        c_g = cm_ref[:, gs].astype(BF16)
        cb = lax.dot_general(c_g, b_f.astype(BF16), (((1,), (1,)), ((), ())), preferred_element_type=F32)
        b_t = b_f.T
        cols_g = slice(g * rep * hp, (g + 1) * rep * hp)
        y_off = jnp.dot(c_g, state_ref[:, cols_g].astype(BF16), preferred_element_type=F32)
        yield
        for pr in range(rep // 2):
            cols = slice(g * rep * hp + pr * LANES, g * rep * hp + (pr + 1) * LANES)
            xs2 = xs_ref[:, cols].astype(BF16)
            y_d, s_n, e_a = [], [], []
            for h in (g * rep + 2 * pr, g * rep + 2 * pr + 1):
                a_col = jnp.broadcast_to(a_cum[:, h:h + 1], (lc, lc))
                decay = jnp.exp2(jnp.where(causal, a_col - a_cum_t[h:h + 1, :], -jnp.inf))
                m = (cb * decay * dt_t[h:h + 1, :]).astype(BF16)
                y_d.append(jnp.dot(m, xs2, preferred_element_type=F32))
                s_n.append(jnp.dot((b_t * w_end_t[h:h + 1, :]).astype(BF16), xs2, preferred_element_type=F32))
                e_a.append(jnp.exp2(a_col))
            h0 = g * rep + 2 * pr
            y_ref[:, cols] = (jnp.where(lo, y_d[0], y_d[1])
                              + y_off[:, pr * LANES:(pr + 1) * LANES] * jnp.where(lo, e_a[0], e_a[1]))
            chunk_decay = jnp.where(lo[0:1, :], jnp.exp2(a_tot[:, h0:h0 + 1]), jnp.exp2(a_tot[:, h0 + 1:h0 + 2]))
            state_ref[:, cols] = state_ref[:, cols] * chunk_decay + jnp.where(lo, s_n[0], s_n[1])
            yield

    return do_group


def _ssd_scan(xbc, dt_raw, prm, n_batch, seq, ctx_len, d_ssm):
    m = xbc.shape[0]
    lc = SSM_CHUNK
    n_heads = d_ssm // SSM_HEAD_DIM
    nc_ctx, nc_lat = ctx_len // lc, seq // lc
    ctx_blk0 = n_batch * nc_lat
    gn = SSM_GROUPS * SSM_STATE

    def row_blk(b, s, backward):
        ctx_c = nc_ctx - 1 - s if backward else s
        lat_c = nc_lat - 1 - (s - nc_ctx) if backward else s - nc_ctx
        return jnp.where(s < nc_ctx, ctx_blk0 + b * nc_ctx + ctx_c, b * nc_lat + lat_c)

    def chunk_specs(backward):
        r = int(backward)
        return [
            pl.BlockSpec((lc, d_ssm), lambda b, s: (row_blk(b, s, backward), 0)),
            pl.BlockSpec((lc, gn), lambda b, s: (row_blk(b, s, backward), d_ssm // gn)),
            pl.BlockSpec((lc, gn), lambda b, s: (row_blk(b, s, backward), d_ssm // gn + 1)),
            pl.BlockSpec((lc, LANES), lambda b, s: (row_blk(b, s, backward), r)),
        ]

    return pl.pallas_call(
        functools.partial(_ssd_scan_kernel, n_heads=n_heads),
        grid=(n_batch, nc_ctx + nc_lat),
        in_specs=chunk_specs(False) + chunk_specs(True) + [pl.BlockSpec((2, SUBLANES, LANES), lambda b, s: (0, 0, 0))],
        out_specs=[pl.BlockSpec((lc, d_ssm), lambda b, s: (row_blk(b, s, False), 0)),
                   pl.BlockSpec((lc, d_ssm), lambda b, s: (row_blk(b, s, True), 0))],
        out_shape=[jax.ShapeDtypeStruct((m, d_ssm), F32), jax.ShapeDtypeStruct((m, d_ssm), F32)],
        scratch_shapes=[pltpu.VMEM((2, SSM_STATE, d_ssm), F32)],
        compiler_params=_params(("arbitrary", "arbitrary"), 8 * lc * d_ssm * 4),
        name="ssd_scan",
    )(xbc, xbc, xbc, dt_raw, xbc, xbc, xbc, dt_raw, prm)


def _ssd_finish_kernel(yf_ref, yb_ref, xs_ref, z_ref, dskip_ref, g_ref, o_ref, *, group_width):
    y = (yf_ref[...] + yb_ref[...] + dskip_ref[...] * xs_ref[...]) * _silu(z_ref[...])
    for g in range(y.shape[1] // group_width):
        sl = slice(g * group_width, (g + 1) * group_width)
        v = y[:, sl]
        vn = v * lax.rsqrt(jnp.mean(v * v, axis=-1, keepdims=True) + NORM_EPS) * g_ref[:, sl]
        o_ref[:, sl] = vn.astype(o_ref.dtype)


def _ssd_finish(y2, xbc, p0, d_skip_lanes, norm_g, d_ssm, tm=512):
    m = xbc.shape[0]
    return pl.pallas_call(
        functools.partial(_ssd_finish_kernel, group_width=d_ssm // SSM_GROUPS),
        grid=(m // tm,),
        in_specs=[
            pl.BlockSpec((tm, d_ssm), lambda i: (i, 0)),
            pl.BlockSpec((tm, d_ssm), lambda i: (i, 0)),
            pl.BlockSpec((tm, d_ssm), lambda i: (i, 0)),
            pl.BlockSpec((tm, d_ssm), lambda i: (i, 0)),
            pl.BlockSpec((1, d_ssm), lambda i: (0, 0)),
            pl.BlockSpec((1, d_ssm), lambda i: (0, 0)),
        ],
        out_specs=pl.BlockSpec((tm, d_ssm), lambda i: (i, 0)),
        out_shape=jax.ShapeDtypeStruct((m, d_ssm), BF16),
        compiler_params=_params(("arbitrary",), 5 * tm * d_ssm * 4),
        name="ssd_finish",
    )(*y2, xbc, p0, d_skip_lanes.reshape(1, d_ssm), norm_g.reshape(1, d_ssm))


def _rope(x, cos, sin_up, sin_dn):
    quarter = DIFF_HEAD_DIM // 4
    return (x * cos + pltpu.roll(x, LANES - quarter, axis=1) * sin_up + pltpu.roll(x, quarter, axis=1) * sin_dn)


def _diff_attn_heads_kernel(*refs, heads_per_step, with_lat, **kw):
    n_head_refs = 5 if with_lat else 3
    *ins, o_ref, k_scr, vt_scr = refs
    for h in range(heads_per_step):
        cs = slice(h * LANES, (h + 1) * LANES)
        head_ins = [r.at[:, cs] for r in ins[:n_head_refs]] + list(ins[n_head_refs:])
        _diff_attn_kernel(*head_ins, o_ref.at[:, cs], k_scr.at[h], vt_scr.at[h], with_lat=with_lat, **kw)


def _diff_attn_kernel(*refs, with_lat, lambda_init, n_ctx, sub_q, key_chunk=768):
    if with_lat:
        (q_ref, kc_ref, kl_ref, vc_ref, vl_ref, qcos_ref, qsu_ref, qsd_ref, kcos_ref, ksu_ref, ksd_ref,
         lam_ref, g_ref, o_ref, k_scr, vt_scr) = refs
    else:
        q_ref, kc_ref, vc_ref, lam_ref, g_ref, o_ref, k_scr, vt_scr = refs

    @pl.when(pl.program_id(2) == 0)
    def _():
        k_scr[0:n_ctx, :] = kc_ref[...].astype(BF16)
        vt_scr[:, 0:n_ctx] = vc_ref[...].T.astype(BF16)
        if with_lat:
            k_scr[n_ctx:, :] = _rope(kl_ref[...], kcos_ref[...], ksu_ref[...], ksd_ref[...]).astype(BF16)
            vt_scr[:, n_ctx:] = vl_ref[...].T.astype(BF16)

    lam_p = lam_ref[...]
    lam = (jnp.exp(jnp.sum(lam_p[0:1, :] * lam_p[1:2, :], axis=-1, keepdims=True))
           - jnp.exp(jnp.sum(lam_p[2:3, :] * lam_p[3:4, :], axis=-1, keepdims=True)) + lambda_init)
    first = lax.broadcasted_iota(jnp.int32, (LANES, sub_q), 0) < DIFF_HEAD_DIM
    n_sub = q_ref.shape[0] // sub_q

    def scores(t):
        rows = slice(t * sub_q, (t + 1) * sub_q)
        q = q_ref[rows, :]
        if with_lat:
            q = _rope(q, qcos_ref[rows, :], qsu_ref[rows, :], qsd_ref[rows, :])
        qt = (q * (DIFF_HEAD_DIM ** -0.5 * LOG2E)).T
        q2t = jnp.concatenate([jnp.where(first, qt, 0.0), jnp.where(first, 0.0, qt)], axis=1).astype(BF16)
        blocks = [jnp.dot(k_scr[k0:k1, :], q2t, preferred_element_type=F32) for k0, k1 in key_chunks]
        return blocks, [jnp.max(b, axis=0, keepdims=True) for b in blocks]

    n_keys = k_scr.shape[0]
    key_chunks = [(k0, min(k0 + key_chunk, n_keys)) for k0 in range(0, n_keys, key_chunk)]
    def weighted_values(t, e_blocks, l):
        ovt = None
        for (k0, k1), e in zip(key_chunks, e_blocks):
            part = jnp.dot(vt_scr[:, k0:k1], e, preferred_element_type=F32)
            ovt = part if ovt is None else ovt + part
        ovt = ovt * (1.0 / l)
        ot = ovt[:, :sub_q] - lam * ovt[:, sub_q:]
        ot = ot * lax.rsqrt(jnp.mean(ot * ot, axis=0, keepdims=True) + NORM_EPS) * g_ref[...] * (1.0 - lambda_init)
        o_ref[t * sub_q:(t + 1) * sub_q, :] = ot.T.astype(o_ref.dtype)

    s_next = scores(0)
    pending = None
    for t in range(n_sub):
        s_blocks, maxes = s_next
        if t + 1 < n_sub:
            s_next = scores(t + 1)
        if pending is not None:
            weighted_values(t - 1, *pending)
        m = functools.reduce(jnp.maximum, maxes)
        e_blocks = [jnp.exp2(s - m) for s in s_blocks]
        l = functools.reduce(lambda a, b: a + b, [jnp.sum(e, axis=0, keepdims=True) for e in e_blocks])
        pending = ([e.astype(BF16) for e in e_blocks], l)
    weighted_values(n_sub - 1, *pending)


def _diff_attn(p0, q_col0, k_col0, v_col0, n_heads, rope_tabs, lam_p, subln_g, lambda_init,
               n_batch, seq, ctx_len, with_lat, tq=2048, sub_q=256):
    ctx_rb0 = (n_batch * seq) // ctx_len
    n_keys = ctx_len + (seq if with_lat else 0)
    hps = 1 if with_lat else n_heads
    width = hps * LANES
    cb = lambda c0: c0 // width
    if with_lat:
        n_q = seq // tq
        q_spec = pl.BlockSpec((tq, width), lambda b, h, i: (b * n_q + i, cb(q_col0) + h))
    else:
        tq = ctx_len
        n_q = 1
        q_spec = pl.BlockSpec((tq, width), lambda b, h, i: (ctx_rb0 + b, cb(q_col0) + h))
    kc_spec = pl.BlockSpec((ctx_len, width), lambda b, h, i: (ctx_rb0 + b, cb(k_col0) + h))
    vc_spec = pl.BlockSpec((ctx_len, width), lambda b, h, i: (ctx_rb0 + b, cb(v_col0) + h))
    small = [pl.BlockSpec((SUBLANES, LANES), lambda b, h, i: (0, 0)),
             pl.BlockSpec((LANES, sub_q), lambda b, h, i: (0, 0))]
    small_args = [lam_p, jnp.broadcast_to(subln_g[:, None], (LANES, sub_q))]
    if with_lat:
        kl_spec = pl.BlockSpec((seq, width), lambda b, h, i: (b, cb(k_col0) + h))
        vl_spec = pl.BlockSpec((seq, width), lambda b, h, i: (b, cb(v_col0) + h))
        qt = pl.BlockSpec((tq, LANES), lambda b, h, i: (i, 0))
        kt = pl.BlockSpec((seq, LANES), lambda b, h, i: (0, 0))
        in_specs = [q_spec, kc_spec, kl_spec, vc_spec, vl_spec, qt, qt, qt, kt, kt, kt] + small
        args = [p0, p0, p0, p0, p0, *rope_tabs, *rope_tabs] + small_args
    else:
        in_specs = [q_spec, kc_spec, vc_spec] + small
        args = [p0, p0, p0] + small_args
    block_bytes = 10 * seq * LANES * 4 if with_lat else 8 * ctx_len * width * 4
    return pl.pallas_call(
        functools.partial(_diff_attn_heads_kernel, heads_per_step=hps, with_lat=with_lat, lambda_init=lambda_init,
                          n_ctx=ctx_len, sub_q=sub_q),
        grid=(n_batch, n_heads // hps, n_q),
        in_specs=in_specs,
        out_specs=pl.BlockSpec((tq, width), lambda b, h, i: (b * n_q + i, h)),
        out_shape=jax.ShapeDtypeStruct((n_batch * n_q * tq, n_heads * LANES), BF16),
        scratch_shapes=[pltpu.VMEM((hps, n_keys, LANES), BF16), pltpu.VMEM((hps, LANES, n_keys), BF16)],
        compiler_params=_params(("arbitrary", "arbitrary", "arbitrary"), block_bytes),
        name="diff_attn_lat" if with_lat else "diff_attn_ctx",
    )(*args)


def _rope_tables(seq):
    half = DIFF_HEAD_DIM // 2
    pos = jnp.arange(seq)
    row, col = pos // GRID_W, pos % GRID_W
    inv_freq = ROPE_BASE ** (-jnp.arange(0, half, 2, dtype=F32) / half)
    lane = np.arange(LANES) % DIFF_HEAD_DIM
    use_col = jnp.asarray(lane >= half)
    first = jnp.asarray((lane % half) < half // 2)
    freq = inv_freq[jnp.asarray(lane % (half // 2))]
    p = jnp.where(use_col[None, :], col[:, None], row[:, None]).astype(F32)
    ang = p * freq[None, :]
    cos, sin = jnp.cos(ang), jnp.sin(ang)
    return cos, jnp.where(first[None, :], -sin, 0.0), jnp.where(first[None, :], 0.0, sin)


def _rpb_gather_kernel(rpb_ref, o_ref):
    n = o_ref.shape[1]
    k = rpb_ref.shape[1]
    colid = pl.program_id(0) * n + lax.broadcasted_iota(jnp.int32, (k, n), 1)
    j = lax.broadcasted_iota(jnp.int32, (k, n), 0)
    qc = lax.shift_right_logical(colid, int(math.log2(LANES)))
    half = lax.shift_right_logical(colid & (LANES - 1), int(math.log2(GRID_W)))
    kc = colid & (GRID_W - 1)
    sel = (j == half * LANES + jnp.clip(kc - qc + (WIN_COLS - 1), 0, 2 * WIN_COLS - 2)).astype(F32)
    o_ref[...] = jnp.dot(rpb_ref[...], sel, preferred_element_type=F32, precision=lax.Precision.HIGHEST)


def _rpb_pairs(rpb):
    nh, nr, ncol = rpb.shape
    n_slot = nr + 1
    left = jnp.pad(rpb, ((0, 0), (1, 0), (0, LANES - ncol)))
    right = jnp.pad(rpb, ((0, 0), (0, 1), (0, LANES - ncol)))
    rows = jnp.concatenate([left, right], axis=2).reshape(nh * n_slot, 2 * LANES)
    tn = 1024
    out = pl.pallas_call(
        _rpb_gather_kernel,
        grid=(GRID_W * LANES // tn,),
        in_specs=[pl.BlockSpec((nh * n_slot, 2 * LANES), lambda j: (0, 0))],
        out_specs=pl.BlockSpec((nh * n_slot, tn), lambda j: (0, j)),
        out_shape=jax.ShapeDtypeStruct((nh * n_slot, GRID_W * LANES), F32),
        compiler_params=_params(("arbitrary",), nh * n_slot * tn * 4 + 2 * LANES * tn * 4),
        name="rpb_gather",
    )(rows)
    return out.reshape(nh, n_slot, GRID_W, LANES)


def _na_block_plan(rows):
    kr = min(WIN_ROWS, rows)
    n_blk = rows // NA_Q_ROWS
    kb = np.clip(np.arange(n_blk) * NA_Q_ROWS - kr // 2, 0, rows - NA_K_ROWS)
    layouts, layout_of = [], []
    for blk in range(n_blk):
        dr = np.full((NA_Q_ROWS, NA_K_ROWS), -1, np.int64)
        for i in range(NA_Q_ROWS):
            r = blk * NA_Q_ROWS + i
            rs = int(np.clip(r - kr // 2, 0, rows - kr))
            for j in range(NA_K_ROWS):
                krow = kb[blk] + j
                if rs <= krow < rs + kr:
                    dr[i, j] = krow - r + WIN_ROWS - 1
        key = dr.tobytes()
        if key not in [l.tobytes() for l in layouts]:
            layouts.append(dr)
        layout_of.append([l.tobytes() for l in layouts].index(key))
    return kb, np.stack(layouts), np.asarray(layout_of)


def _na_bias_kernel(tp_ref, o_ref, *, layouts):
    qc = lax.broadcasted_iota(jnp.int32, (GRID_W, LANES), 0)
    lane = lax.broadcasted_iota(jnp.int32, (GRID_W, LANES), 1)
    kc = lane & (GRID_W - 1)
    left = lane < GRID_W
    col_start = jnp.clip(qc - WIN_COLS // 2, 0, GRID_W - WIN_COLS)
    col_ok = (kc >= col_start) & (kc < col_start + WIN_COLS)
    n_lay, n_q, n_k = layouts.shape
    for lay in range(n_lay):
        for i in range(n_q):
            for p in range(-(-n_k // 2)):
                d_l = int(layouts[lay, i, 2 * p])
                d_r = int(layouts[lay, i, 2 * p + 1]) if 2 * p + 1 < n_k else -1
                width = LANES if 2 * p + 1 < n_k else GRID_W
                if d_l < 0 and d_r < 0:
                    tile = jnp.full((GRID_W, LANES), -jnp.inf, F32)
                else:
                    assert d_l < 0 or d_r < 0 or d_r == d_l + 1
                    ok = col_ok
                    if d_l < 0:
                        ok = ok & jnp.logical_not(left)
                    if d_r < 0:
                        ok = ok & left
                    slot = d_r if d_r >= 0 else d_l + 1
                    tile = jnp.where(ok, tp_ref[slot] * LOG2E, -jnp.inf)
                o_ref[lay, i * GRID_W:(i + 1) * GRID_W, p * LANES:p * LANES + width] = tile[:, :width]


def _na_bias(rpb_pairs, layouts):
    nh, n_slot = rpb_pairs.shape[:2]
    n_lay = layouts.shape[0]
    tq, nkw = NA_Q_ROWS * GRID_W, NA_K_ROWS * GRID_W
    return pl.pallas_call(
        functools.partial(_na_bias_kernel, layouts=layouts),
        grid=(nh,),
        in_specs=[pl.BlockSpec((None, n_slot, GRID_W, LANES), lambda h: (h, 0, 0, 0))],
        out_specs=pl.BlockSpec((None, n_lay, tq, nkw), lambda h: (h, 0, 0, 0)),
        out_shape=jax.ShapeDtypeStruct((nh, n_lay, tq, nkw), F32),
        compiler_params=_params(("arbitrary",), n_lay * tq * nkw * 4),
        name="na_bias",
    )(rpb_pairs)


def _na_kernel(q_ref, kl_ref, vl_ref, kc_ref, vc_ref, bias_ref, o_ref, *, plan):
    tq = NA_Q_ROWS * GRID_W
    nkw = NA_K_ROWS * GRID_W
    nt = (((1,), (1,)), ((), ()))
    kc = kc_ref[...]
    vc = vc_ref[...]

    def scores(blk):
        kb, lay = plan[blk]
        q = q_ref[blk * tq:(blk + 1) * tq, :]
        ks = slice(kb * GRID_W, kb * GRID_W + nkw)
        return (lax.dot_general(q, kl_ref[ks, :], nt, preferred_element_type=F32) + bias_ref[lay],
                lax.dot_general(q, kc, nt, preferred_element_type=F32))

    def weighted_values(blk, e_w, e_c, l):
        kb, _ = plan[blk]
        ks = slice(kb * GRID_W, kb * GRID_W + nkw)
        o = (jnp.dot(e_w, vl_ref[ks, :], preferred_element_type=F32) + jnp.dot(e_c, vc, preferred_element_type=F32))
        o_ref[blk * tq:(blk + 1) * tq, :] = (o * (1.0 / l)).astype(o_ref.dtype)

    s_next = scores(0)
    pending = None
    for blk in range(len(plan)):
        s_w, s_c = s_next
        if blk + 1 < len(plan):
            s_next = scores(blk + 1)
        if pending is not None:
            weighted_values(blk - 1, *pending)
        m = jnp.maximum(jnp.max(s_w, axis=-1, keepdims=True), jnp.max(s_c, axis=-1, keepdims=True))
        e_w = jnp.exp2(s_w - m)
        e_c = jnp.exp2(s_c - m)
        l = jnp.sum(e_w, axis=-1, keepdims=True) + jnp.sum(e_c, axis=-1, keepdims=True)
        pending = (e_w.astype(BF16), e_c.astype(BF16), l)
    weighted_values(len(plan) - 1, *pending)


def _na_attn(p1, rpb, n_batch, seq, ctx_len, n_heads):
    rows = seq // GRID_W
    kb, layouts, layout_of = _na_block_plan(rows)
    bias = _na_bias(_rpb_pairs(rpb), layouts)
    n_lay = layouts.shape[0]
    tq = NA_Q_ROWS * GRID_W
    nkw = NA_K_ROWS * GRID_W
    ctx_rb0 = (n_batch * seq) // ctx_len
    plan = tuple((int(k), int(l)) for k, l in zip(kb, layout_of))
    return pl.pallas_call(
        functools.partial(_na_kernel, plan=plan),
        grid=(n_heads, n_batch),
        in_specs=[
            pl.BlockSpec((seq, LANES), lambda h, b: (b, h)),
            pl.BlockSpec((seq, LANES), lambda h, b: (b, n_heads + h)),
            pl.BlockSpec((seq, LANES), lambda h, b: (b, 2 * n_heads + h)),
            pl.BlockSpec((ctx_len, LANES), lambda h, b: (ctx_rb0 + b, n_heads + h)),
            pl.BlockSpec((ctx_len, LANES), lambda h, b: (ctx_rb0 + b, 2 * n_heads + h)),
            pl.BlockSpec((None, n_lay, tq, nkw), lambda h, b: (h, 0, 0, 0)),
        ],
        out_specs=pl.BlockSpec((seq, LANES), lambda h, b: (b, h)),
        out_shape=jax.ShapeDtypeStruct((n_batch * seq, n_heads * LANES), BF16),
        compiler_params=_params(("arbitrary", "arbitrary"), n_lay * tq * nkw * 4 + 4 * seq * LANES * 2),
        name="na_attn",
    )(p1, p1, p1, p1, p1, bias)


def _pack_rows(rows):
    out = jnp.zeros((SUBLANES, LANES), F32)
    for r, v in enumerate(rows):
        out = out.at[r, :v.shape[0]].set(v.astype(F32))
    return out


def kernel(x, c, ctx, c_ctx, ada_w, ada_b, norm_mix_g, norm_ffn_g, final_norm_g, ffn_w1, ffn_w3, ffn_w2, ev_w_in, ev_conv_w, ev_conv_b, ev_a_log, ev_dt_bias, ev_d_skip, ev_ssm_norm_g, ev_lam_q1, ev_lam_k1, ev_lam_q2, ev_lam_k2, ev_subln_g, ev_w_out, od_w_in, od_rpb, od_w_out):
    n_batch, seq, d = x.shape
    ctx_len = ctx.shape[1]
    depth = ada_w.shape[0]
    n_lat = n_batch * seq
    n_tok = n_lat + n_batch * ctx_len

    d_ssm = ev_ssm_norm_g.shape[1]
    n_ssm_heads = ev_d_skip.shape[1]
    d_xbc = ev_conv_w.shape[2]
    d_qk = d_v = (ev_w_in.shape[2] - d_ssm - d_xbc - 2 * n_ssm_heads) // 3
    n_diff_heads = d_v // (2 * DIFF_HEAD_DIM)
    n_na_heads = od_rpb.shape[1]

    stream = (x.reshape(n_lat, d), ctx.reshape(n_batch * ctx_len, d))
    cond = jnp.zeros((COND_ROWS, d), F32).at[:n_batch].set(c).at[n_batch].set(c_ctx)
    mod = _ada_mod(cond, ada_w, ada_b).reshape(depth, COND_ROWS, 1, 6 * d)
    rope_tabs = _rope_tables(seq)
    kw = dict(seq=seq, n_batch=n_batch)
    ffn_w = (ffn_w1, ffn_w3, ffn_w2)

    for i in range(depth):
        ctx_out = i < depth - 1
        m_rows = n_tok if ctx_out else n_lat
        j = i // 2
        if i % 2 == 0:
            lambda_init = 0.8 - 0.6 * math.exp(-0.3 * i)
            w_in = ev_w_in[j]
            dt0 = d_ssm + d_xbc
            w_t = jnp.swapaxes(w_in, 0, 1)
            w_parts = [w_t[:dt0].astype(BF16), w_t[dt0 + 2 * n_ssm_heads:].astype(BF16)]
            w_dt = jnp.zeros((2 * LANES, d), F32)
            w_dt = w_dt.at[:n_ssm_heads].set(w_t[dt0:dt0 + n_ssm_heads])
            w_dt = w_dt.at[LANES:LANES + n_ssm_heads].set(w_t[dt0 + n_ssm_heads:dt0 + 2 * n_ssm_heads])
            p0, dt_raw = _proj(stream, norm_mix_g[i], mod, i, w_parts, w_dt.astype(BF16), F32, **kw)
            q0 = d_ssm + d_xbc
            k0, v0 = q0 + d_qk, q0 + 2 * d_qk

            xbc = _ssd_conv(p0, d_ssm, d_xbc, ev_conv_w[j], ev_conv_b[j], seq, ctx_len, n_lat)
            prm = jnp.stack([_pack_rows([ev_a_log[j, r], ev_dt_bias[j, r]]) for r in range(2)])
            y2 = _ssd_scan(xbc, dt_raw, prm, n_batch, seq, ctx_len, d_ssm)
            mix_ssd = _ssd_finish(y2, xbc, p0, jnp.repeat(ev_d_skip[j], SSM_HEAD_DIM), ev_ssm_norm_g[j], d_ssm)

            lam_p = _pack_rows([ev_lam_q1[j], ev_lam_k1[j], ev_lam_q2[j], ev_lam_k2[j]])
            attn_args = (p0, q0, k0, v0, n_diff_heads, rope_tabs, lam_p, ev_subln_g[j], lambda_init,
                         n_batch, seq, ctx_len)
            mix_attn = _diff_attn(*attn_args, with_lat=True)
            if ctx_out:
                mix_attn = (mix_attn, _diff_attn(*attn_args, with_lat=False))
            lhs = [mix_ssd, mix_attn]
            w_out = ev_w_out[j]
        else:
            d_na = n_na_heads * NA_HEAD_DIM
            q_scale = jnp.where(jnp.arange(3 * d_na) < d_na, NA_HEAD_DIM ** -0.5 * LOG2E, 1.0).astype(F32)
            p1 = _proj(stream, norm_mix_g[i], mod, i, od_w_in[j], None, BF16, col_scale=q_scale, **kw)
            assert not ctx_out, "context-query neighbourhood layers are not needed at this depth"
            lhs = [_na_attn(p1, od_rpb[j], n_batch, seq, ctx_len, n_na_heads)]
            w_out = od_w_out[j]
        stream = _out_proj(lhs, w_out, stream, mod, i, 2, m_rows, **kw)
        stream = _ffn(stream, norm_ffn_g[i], mod, i, *ffn_w, final_norm_g, not ctx_out, m_rows, **kw)
    return stream.reshape(n_batch, seq, d)
```
